```python
import math
import numpy as np
import jax
import jax.numpy as jnp
from jax import lax

D_MODEL = 1024
BATCH = 8
SEQ = 4096
DEPTH = 2
DEC_BATCH = 32
DEC_SEQ = 4
PAST_LEN = 16384
PAGE_SIZE = 128

HEAD_DIM = 64
N_HEADS = D_MODEL // HEAD_DIM
ROPE_THETA = 10000.0
RMS_EPS = 1e-6
QBLK = 128
NEG = -1e30

FOX_HEADS = N_HEADS // 2
FOX_BIAS_MEAN = 3.0

NSA_HEADS = N_HEADS - FOX_HEADS
NSA_KV_HEADS = 2
NSA_GROUP = NSA_HEADS // NSA_KV_HEADS
NSA_CMP_LEN = 32
NSA_CMP_STRIDE = 16
NSA_CMP_HIDDEN = 2 * HEAD_DIM
NSA_SLC_BLOCK = 64
NSA_TOPN = 16
NSA_WINDOW = 512
NSA_FORCE_BONUS = 1e3

DIL_HEADS = N_HEADS
DIL_PAIRS = ((128, 1), (512, 4), (2048, 16))
DIL_WINDOW_MAX = 2048

FFN_DIM = 2816
N_EXPERTS = 8
TOP_K = 2
EXPERT_DIM = 1792

FOX_W = FOX_HEADS * HEAD_DIM
NSA_W = NSA_HEADS * HEAD_DIM
NSA_KV_W = NSA_KV_HEADS * HEAD_DIM
DIL_W = DIL_HEADS * HEAD_DIM
IN0_SPLITS = (FOX_W, FOX_W, FOX_W, FOX_HEADS, NSA_W) + (NSA_KV_W,) * 6 + (3 * NSA_HEADS,)
IN0_DIM = sum(IN0_SPLITS)

kernel_name = 'fox_nsa_dilated_hybrid_decode_step'

F32 = jnp.float32
SCALE = HEAD_DIM ** -0.5


def rmsnorm(x, g):
    xf = x.astype(F32)
    y = xf * lax.rsqrt(jnp.mean(xf * xf, axis=-1, keepdims=True) + RMS_EPS)
    return (y * g.astype(F32)).astype(x.dtype)


def split_heads(t, h):
    return t.reshape(t.shape[:-1] + (h, HEAD_DIM))


def rope(x, pos):
    half = HEAD_DIM // 2
    inv = jnp.exp(-math.log(ROPE_THETA) * jnp.arange(half, dtype=F32) / half)
    ang = pos.astype(F32)[:, None] * inv[None, :]
    cos = jnp.cos(ang)[None, :, None, :]
    sin = jnp.sin(ang)[None, :, None, :]
    xf = x.astype(F32)
    x1, x2 = xf[..., :half], xf[..., half:]
    return jnp.concatenate([x1 * cos - x2 * sin, x2 * cos + x1 * sin], axis=-1).astype(x.dtype)


def masked_softmax(s, mask):
    p = jax.nn.softmax(jnp.where(mask, s, NEG), axis=-1)
    return jnp.where(mask, p, 0.0)


def gather_pages(cache, page_table):
    g = cache[page_table]
    return g.reshape((g.shape[0], g.shape[1] * g.shape[2]) + g.shape[3:])


def swiglu(x, wg, wu, wd):
    return (jax.nn.silu(x @ wg) * (x @ wu)) @ wd


def moe(x, w_router, wg, wu, wd):
    logits = jnp.einsum('...d,de->...e', x, w_router, preferred_element_type=F32)
    top_v, top_i = lax.top_k(logits, TOP_K)
    gate = jax.nn.softmax(top_v, axis=-1)
    combine = jnp.sum(jax.nn.one_hot(top_i, N_EXPERTS, dtype=F32) * gate[..., None], axis=-2)
    y = jnp.zeros(x.shape, F32)
    for e in range(N_EXPERTS):
        y = y + combine[..., e:e + 1] * swiglu(x, wg[e], wu[e], wd[e])
    return y.astype(x.dtype)


def fox_attend(q, k, v, cq, ck, qpos, kpos):
    s = jnp.einsum('nqhd,nkhd->nhqk', q, k, preferred_element_type=F32) * SCALE
    s = s + jnp.swapaxes(cq, 1, 2)[..., :, None] - jnp.swapaxes(ck, 1, 2)[..., None, :]
    p = masked_softmax(s, kpos[None, :] <= qpos[:, None])
    return jnp.einsum('nhqk,nkhd->nqhd', p.astype(v.dtype), v, preferred_element_type=F32)


def fox_prompt(q, k, v, logf):
    B, S, H, D = q.shape
    c = jnp.cumsum(logf, axis=1)
    kpos = jnp.arange(S)

    def block(b):
        st = b * QBLK
        qb = lax.dynamic_slice_in_dim(q, st, QBLK, axis=1)
        cb = lax.dynamic_slice_in_dim(c, st, QBLK, axis=1)
        return fox_attend(qb, k, v, cb, c, st + jnp.arange(QBLK), kpos)

    o = lax.map(block, jnp.arange(S // QBLK))
    return jnp.swapaxes(o, 0, 1).reshape(B, S, H, D)


def nsa_compress(x, pe, w1, w2):
    N, L = x.shape[:2]
    r = NSA_CMP_LEN // NSA_CMP_STRIDE
    n_ch = L // NSA_CMP_STRIDE
    nc = n_ch - r + 1
    ch = x[:, :n_ch * NSA_CMP_STRIDE].reshape(N, n_ch, NSA_CMP_STRIDE, x.shape[2], HEAD_DIM)
    blocks = jnp.concatenate([ch[:, i:i + nc] for i in range(r)], axis=2)
    blocks = blocks + pe[None, None, :, None, :].astype(x.dtype)
    h = jax.nn.silu(jnp.einsum('nclkd,lde->ncke', blocks, w1))
    return jnp.einsum('ncke,ed->nckd', h, w2)


def cmp_ends(nc):
    return jnp.arange(nc) * NSA_CMP_STRIDE + NSA_CMP_LEN - 1


def slc_map(nc, ns):
    i = np.arange(nc)[:, None] * NSA_CMP_STRIDE
    j = np.arange(ns)[None, :] * NSA_SLC_BLOCK
    shared_len = np.minimum(i + NSA_CMP_LEN, j + NSA_SLC_BLOCK) - np.maximum(i, j)
    return jnp.asarray(np.clip(shared_len, 0, None) / NSA_CMP_LEN, dtype=F32)


def to_blocks(x):
    N, L = x.shape[:2]
    ns = -(-L // NSA_SLC_BLOCK)
    xp = jnp.pad(x, ((0, 0), (0, ns * NSA_SLC_BLOCK - L), (0, 0), (0, 0)))
    return xp.reshape(N, ns, NSA_SLC_BLOCK, NSA_KV_HEADS, HEAD_DIM).transpose(0, 3, 1, 2, 4)


def nsa_attend(q, qpos, kc, vc, ks, vs, kw, vw, kw_pos, gates):
    N, Q = q.shape[:2]
    nc, ns = kc.shape[1], ks.shape[2]
    qg = q.reshape(N, Q, NSA_KV_HEADS, NSA_GROUP, HEAD_DIM)
    sc = jnp.einsum('nqkgd,nckd->nkgqc', qg, kc, preferred_element_type=F32) * SCALE
    pc = masked_softmax(sc, cmp_ends(nc)[None, :] <= qpos[:, None])
    oc = jnp.einsum('nkgqc,nckd->nqkgd', pc.astype(vc.dtype), vc, preferred_element_type=F32)
    imp = jnp.einsum('nkgqc,cj->nkqj', pc, slc_map(nc, ns))
    blk = jnp.arange(ns)[None, :]
    cur = (qpos // NSA_SLC_BLOCK)[:, None]
    valid = blk * NSA_SLC_BLOCK <= qpos[:, None]
    forced = (blk == 0) | (blk == cur) | (blk == cur - 1)
    score = jnp.where(valid, imp + NSA_FORCE_BONUS * forced, NEG)
    n_sel = min(NSA_TOPN, ns)
    _, idx = lax.top_k(score, n_sel)
    ni = jnp.arange(N)[:, None, None, None]
    hi = jnp.arange(NSA_KV_HEADS)[None, :, None, None]
    m_sel = n_sel * NSA_SLC_BLOCK
    kg = ks[ni, hi, idx].reshape(N, NSA_KV_HEADS, Q, m_sel, HEAD_DIM)
    vg = vs[ni, hi, idx].reshape(N, NSA_KV_HEADS, Q, m_sel, HEAD_DIM)
    kpos = (idx[..., None] * NSA_SLC_BLOCK + jnp.arange(NSA_SLC_BLOCK)).reshape(N, NSA_KV_HEADS, Q, m_sel)
    ss = jnp.einsum('nqkgd,nkqmd->nkgqm', qg, kg, preferred_element_type=F32) * SCALE
    ps = masked_softmax(ss, (kpos <= qpos[:, None])[:, :, None])
    osl = jnp.einsum('nkgqm,nkqmd->nqkgd', ps.astype(vg.dtype), vg, preferred_element_type=F32)
    sw = jnp.einsum('nqkgd,nwkd->nkgqw', qg, kw, preferred_element_type=F32) * SCALE
    dist = qpos[:, None] - kw_pos[None, :]
    pw = masked_softmax(sw, (dist >= 0) & (dist <= NSA_WINDOW) & (kw_pos[None, :] >= 0))
    ow = jnp.einsum('nkgqw,nwkd->nqkgd', pw.astype(vw.dtype), vw, preferred_element_type=F32)
    g = gates.reshape(N, Q, NSA_KV_HEADS, NSA_GROUP, 3)
    o = g[..., 0:1] * oc + g[..., 1:2] * osl + g[..., 2:3] * ow
    return o.reshape(N, Q, NSA_HEADS, HEAD_DIM)


def stride_split(x, d):
    B, S = x.shape[:2]
    x = x.reshape((B, S // d, d) + x.shape[2:])
    return jnp.swapaxes(x, 1, 2).reshape((B * d, S // d) + x.shape[3:])


def stride_merge(x, B, d):
    L = x.shape[1]
    x = x.reshape((B, d, L) + x.shape[2:])
    return jnp.swapaxes(x, 1, 2).reshape((B, L * d) + x.shape[3:])


def band_attend(q, k, v, span):
    N, L, H, D = q.shape
    nb = -(-L // QBLK)
    Lp = nb * QBLK
    r = span // QBLK
    qb = jnp.pad(q, ((0, 0), (0, Lp - L), (0, 0), (0, 0))).reshape(N, nb, QBLK, H, D)

    def windows(x):
        xp = jnp.pad(x, ((0, 0), (span, Lp - L), (0, 0), (0, 0))).reshape(N, nb + r, QBLK, H, D)
        return jnp.concatenate([xp[:, i:i + nb] for i in range(r + 1)], axis=2)

    kb, vb = windows(k), windows(v)
    s = jnp.einsum('nbqhd,nbkhd->nbhqk', qb, kb, preferred_element_type=F32) * SCALE
    qi = jnp.arange(QBLK)[:, None]
    kj = jnp.arange(span + QBLK)[None, :]
    dist = qi + span - kj
    kabs = jnp.arange(nb)[:, None, None] * QBLK - span + kj[None]
    mask = ((dist >= 0) & (dist <= span))[None] & (kabs >= 0)
    s = jnp.where(mask[None, :, None], s, NEG)
    lse = jax.nn.logsumexp(s, axis=-1)
    p = jnp.exp(s - lse[..., None])
    o = jnp.einsum('nbhqk,nbkhd->nbqhd', p.astype(vb.dtype), vb, preferred_element_type=F32)
    o = o.reshape(N, Lp, H, D)[:, :L]
    lse = jnp.swapaxes(lse, 2, 3).reshape(N, Lp, H)[:, :L]
    return o, lse


def dilated_gather_attend(q, k_ext, v_ext, wc, window, dil):
    T = q.shape[1]
    j = jnp.arange(window // dil + 1)
    idx = wc + jnp.arange(T)[:, None] - j[None, :] * dil
    valid = idx >= 0
    cidx = jnp.clip(idx, 0)
    kg, vg = k_ext[:, cidx], v_ext[:, cidx]
    s = jnp.einsum('nqhd,nqjhd->nhqj', q, kg, preferred_element_type=F32) * SCALE
    s = jnp.where(valid[None, None], s, NEG)
    lse = jax.nn.logsumexp(s, axis=-1)
    p = jnp.exp(s - lse[..., None])
    o = jnp.einsum('nhqj,nqjhd->nqhd', p.astype(vg.dtype), vg, preferred_element_type=F32)
    return o, jnp.swapaxes(lse, 1, 2)


def merge_by_denominator(outs, lses):
    w = jax.nn.softmax(jnp.stack(lses, 0), axis=0)
    return jnp.einsum('gnlh,gnlhd->nlhd', w, jnp.stack(outs, 0))


def even_project(xn, w_in0, fox_bf):
    cuts = [int(c) for c in np.cumsum(IN0_SPLITS)[:-1]]
    qa, ka, va, fa, qb, kc, vc, ks, vs, kw, vw, gb = jnp.split(xn @ w_in0, cuts, axis=-1)
    logf = jax.nn.log_sigmoid(fa.astype(F32) + fox_bf.astype(F32))
    gates = jax.nn.sigmoid(gb.astype(F32)).reshape(gb.shape[:-1] + (NSA_HEADS, 3))
    kvs = [split_heads(t, NSA_KV_HEADS) for t in (kc, vc, ks, vs, kw, vw)]
    return (split_heads(qa, FOX_HEADS), split_heads(ka, FOX_HEADS), split_heads(va, FOX_HEADS), logf,
            split_heads(qb, NSA_HEADS), kvs[0], kvs[1], kvs[2], kvs[3], kvs[4], kvs[5], gates)


def even_mixer_prompt(xn, w_in0, fox_bf, pe_k, w1_k, w2_k, pe_v, w1_v, w2_v):
    B, S = xn.shape[:2]
    qa, ka, va, logf, qb, kc, vc, ks, vs, kw, vw, gates = even_project(xn, w_in0, fox_bf)
    o_a = fox_prompt(qa, ka, va, logf)
    pos = jnp.arange(S)
    qb, kc, ks, kw = rope(qb, pos), rope(kc, pos), rope(ks, pos), rope(kw, pos)
    ckc = nsa_compress(kc, pe_k, w1_k, w2_k)
    cvc = nsa_compress(vc, pe_v, w1_v, w2_v)
    ksb, vsb = to_blocks(ks), to_blocks(vs)
    front = ((0, 0), (NSA_WINDOW, 0), (0, 0), (0, 0))
    kwp, vwp = jnp.pad(kw, front), jnp.pad(vw, front)
    span = NSA_WINDOW + QBLK

    def block(b):
        st = b * QBLK
        return nsa_attend(lax.dynamic_slice_in_dim(qb, st, QBLK, axis=1), st + jnp.arange(QBLK),
                          ckc, cvc, ksb, vsb,
                          lax.dynamic_slice_in_dim(kwp, st, span, axis=1),
                          lax.dynamic_slice_in_dim(vwp, st, span, axis=1),
                          st - NSA_WINDOW + jnp.arange(span),
                          lax.dynamic_slice_in_dim(gates, st, QBLK, axis=1))

    o_b = jnp.swapaxes(lax.map(block, jnp.arange(S // QBLK)), 0, 1).reshape(B, S, NSA_W)
    o = jnp.concatenate([o_a.reshape(B, S, FOX_W), o_b], axis=-1)
    win = min(NSA_WINDOW, S)
    return o, (ka, va, logf.astype(xn.dtype), kc, vc, ks, vs, kw[:, S - win:], vw[:, S - win:])


def even_mixer_sample(xn, past_len, page_table, cache_a_k, cache_a_v, cache_a_logf,
                      cache_b_cmp_k, cache_b_cmp_v, cache_b_slc_k, cache_b_slc_v,
                      cache_b_swa_k, cache_b_swa_v,
                      w_in0, fox_bf, pe_k, w1_k, w2_k, pe_v, w1_v, w2_v):
    DB, T = xn.shape[:2]
    qpos = past_len + jnp.arange(T)
    qa, ka, va, logf, qb, kc, vc, ks, vs, kw, vw, gates = even_project(xn, w_in0, fox_bf)
    ka_all = jnp.concatenate([gather_pages(cache_a_k, page_table), ka], axis=1)
    va_all = jnp.concatenate([gather_pages(cache_a_v, page_table), va], axis=1)
    c = jnp.cumsum(jnp.concatenate([gather_pages(cache_a_logf, page_table).astype(F32), logf], axis=1), axis=1)
    o_a = fox_attend(qa, ka_all, va_all, c[:, past_len:], c, qpos, jnp.arange(past_len + T))
    qb, kc, ks, kw = rope(qb, qpos), rope(kc, qpos), rope(ks, qpos), rope(kw, qpos)
    ckc = nsa_compress(jnp.concatenate([gather_pages(cache_b_cmp_k, page_table), kc], axis=1), pe_k, w1_k, w2_k)
    cvc = nsa_compress(jnp.concatenate([gather_pages(cache_b_cmp_v, page_table), vc], axis=1), pe_v, w1_v, w2_v)
    ksb = to_blocks(jnp.concatenate([gather_pages(cache_b_slc_k, page_table), ks], axis=1))
    vsb = to_blocks(jnp.concatenate([gather_pages(cache_b_slc_v, page_table), vs], axis=1))
    wb = cache_b_swa_k.shape[1]
    kw_all = jnp.concatenate([cache_b_swa_k, kw], axis=1)
    vw_all = jnp.concatenate([cache_b_swa_v, vw], axis=1)
    kw_pos = past_len - wb + jnp.arange(wb + T)
    o_b = nsa_attend(qb, qpos, ckc, cvc, ksb, vsb, kw_all, vw_all, kw_pos, gates)
    o = jnp.concatenate([o_a.reshape(DB, T, FOX_W), o_b.reshape(DB, T, NSA_W)], axis=-1)
    return o, (ka, va, logf.astype(xn.dtype), kc, vc, ks, vs, kw, vw)


def odd_project(xn, w_in1):
    q, k, v = jnp.split(xn @ w_in1, 3, axis=-1)
    return split_heads(q, DIL_HEADS), split_heads(k, DIL_HEADS), split_heads(v, DIL_HEADS)


def odd_mixer_prompt(xn, w_in1):
    B, S = xn.shape[:2]
    q, k, v = odd_project(xn, w_in1)
    pos = jnp.arange(S)
    q, k = rope(q, pos), rope(k, pos)
    outs, lses = [], []
    for window, dil in DIL_PAIRS:
        o, lse = band_attend(stride_split(q, dil), stride_split(k, dil), stride_split(v, dil), window // dil)
        outs.append(stride_merge(o, B, dil))
        lses.append(stride_merge(lse, B, dil))
    o = merge_by_denominator(outs, lses).reshape(B, S, DIL_W)
    win = min(DIL_WINDOW_MAX, S)
    return o, (k[:, S - win:], v[:, S - win:])


def odd_mixer_sample(xn, past_len, cache_c_k, cache_c_v, w_in1):
    DB, T = xn.shape[:2]
    q, k, v = odd_project(xn, w_in1)
    qpos = past_len + jnp.arange(T)
    q, k = rope(q, qpos), rope(k, qpos)
    wc = cache_c_k.shape[1]
    k_ext = jnp.concatenate([cache_c_k, k], axis=1)
    v_ext = jnp.concatenate([cache_c_v, v], axis=1)
    outs, lses = [], []
    for window, dil in DIL_PAIRS:
        o, lse = dilated_gather_attend(q, k_ext, v_ext, wc, window, dil)
        outs.append(o)
        lses.append(lse)
    o = merge_by_denominator(outs, lses).reshape(DB, T, DIL_W)
    return o, (k, v)


def setup_inputs(seed: int = 0) -> dict:
    key = jax.random.key(seed)
    ks = jax.random.split(key, 40)
    n_pages = PAST_LEN // PAGE_SIZE
    n_used = DEC_BATCH * n_pages
    n_pool = n_used + max(1, n_used // 4)
    swa_buf = min(NSA_WINDOW, PAST_LEN)
    dil_buf = min(DIL_WINDOW_MAX, PAST_LEN)

    def nrm(i, shape, scale=1.0):
        return jax.random.normal(ks[i], shape, F32) * scale

    def gain(i):
        return 1.0 + 0.02 * jax.random.normal(ks[i], (D_MODEL,), F32)

    page_table = jax.random.permutation(ks[13], n_pool)[:n_used].reshape(DEC_BATCH, n_pages).astype(jnp.int32)
    return {
        'x_prompt': nrm(0, (BATCH, SEQ, D_MODEL)),
        'x_sample': nrm(1, (DEC_BATCH, DEC_SEQ, D_MODEL)),
        'cache_a_k': nrm(2, (n_pool, PAGE_SIZE, FOX_HEADS, HEAD_DIM)),
        'cache_a_v': nrm(3, (n_pool, PAGE_SIZE, FOX_HEADS, HEAD_DIM)),
        'cache_a_logf': jax.nn.log_sigmoid(FOX_BIAS_MEAN + nrm(4, (n_pool, PAGE_SIZE, FOX_HEADS))),
        'cache_b_cmp_k': nrm(5, (n_pool, PAGE_SIZE, NSA_KV_HEADS, HEAD_DIM)),
        'cache_b_cmp_v': nrm(6, (n_pool, PAGE_SIZE, NSA_KV_HEADS, HEAD_DIM)),
        'cache_b_slc_k': nrm(7, (n_pool, PAGE_SIZE, NSA_KV_HEADS, HEAD_DIM)),
        'cache_b_slc_v': nrm(8, (n_pool, PAGE_SIZE, NSA_KV_HEADS, HEAD_DIM)),
        'cache_b_swa_k': nrm(9, (DEC_BATCH, swa_buf, NSA_KV_HEADS, HEAD_DIM)),
        'cache_b_swa_v': nrm(10, (DEC_BATCH, swa_buf, NSA_KV_HEADS, HEAD_DIM)),
        'cache_c_k': nrm(11, (DEC_BATCH, dil_buf, DIL_HEADS, HEAD_DIM)),
        'cache_c_v': nrm(12, (DEC_BATCH, dil_buf, DIL_HEADS, HEAD_DIM)),
        'page_table': page_table,
        'norm_mix0': gain(14),
        'w_in0': nrm(15, (D_MODEL, IN0_DIM), D_MODEL ** -0.5),
        'fox_bf': FOX_BIAS_MEAN + nrm(16, (FOX_HEADS,), 0.5),
        'nsa_pe_k': nrm(17, (NSA_CMP_LEN, HEAD_DIM), 0.1),
        'nsa_w1_k': nrm(18, (NSA_CMP_LEN, HEAD_DIM, NSA_CMP_HIDDEN), (NSA_CMP_LEN * HEAD_DIM) ** -0.5),
        'nsa_w2_k': nrm(19, (NSA_CMP_HIDDEN, HEAD_DIM), NSA_CMP_HIDDEN ** -0.5),
        'nsa_pe_v': nrm(20, (NSA_CMP_LEN, HEAD_DIM), 0.1),
        'nsa_w1_v': nrm(21, (NSA_CMP_LEN, HEAD_DIM, NSA_CMP_HIDDEN), (NSA_CMP_LEN * HEAD_DIM) ** -0.5),
        'nsa_w2_v': nrm(22, (NSA_CMP_HIDDEN, HEAD_DIM), NSA_CMP_HIDDEN ** -0.5),
        'w_out0': nrm(23, (FOX_W + NSA_W, D_MODEL), (FOX_W + NSA_W) ** -0.5),
        'norm_ffn0': gain(24),
        'ffn_w_gate': nrm(25, (D_MODEL, FFN_DIM), D_MODEL ** -0.5),
        'ffn_w_up': nrm(26, (D_MODEL, FFN_DIM), D_MODEL ** -0.5),
        'ffn_w_down': nrm(27, (FFN_DIM, D_MODEL), FFN_DIM ** -0.5),
        'norm_mix1': gain(28),
        'w_in1': nrm(29, (D_MODEL, 3 * DIL_W), D_MODEL ** -0.5),
        'w_out1': nrm(30, (DIL_W, D_MODEL), DIL_W ** -0.5),
        'norm_ffn1': gain(31),
        'moe_router': nrm(32, (D_MODEL, N_EXPERTS), D_MODEL ** -0.5),
        'moe_w_gate': nrm(33, (N_EXPERTS, D_MODEL, EXPERT_DIM), D_MODEL ** -0.5),
        'moe_w_up': nrm(34, (N_EXPERTS, D_MODEL, EXPERT_DIM), D_MODEL ** -0.5),
        'moe_w_down': nrm(35, (N_EXPERTS, EXPERT_DIM, D_MODEL), EXPERT_DIM ** -0.5),
        'norm_final': gain(36),
    }


def reference(x_prompt, x_sample, cache_a_k, cache_a_v, cache_a_logf,
              cache_b_cmp_k, cache_b_cmp_v, cache_b_slc_k, cache_b_slc_v, cache_b_swa_k, cache_b_swa_v,
              cache_c_k, cache_c_v, page_table,
              norm_mix0, w_in0, fox_bf, nsa_pe_k, nsa_w1_k, nsa_w2_k, nsa_pe_v, nsa_w1_v, nsa_w2_v, w_out0,
              norm_ffn0, ffn_w_gate, ffn_w_up, ffn_w_down,
              norm_mix1, w_in1, w_out1,
              norm_ffn1, moe_router, moe_w_gate, moe_w_up, moe_w_down,
              norm_final):
    past_len = page_table.shape[1] * cache_a_k.shape[1]
    hp, hs = x_prompt, x_sample
    for layer in range(DEPTH):
        if layer % 2 == 0:
            mix_p, st0_p = even_mixer_prompt(rmsnorm(hp, norm_mix0), w_in0, fox_bf,
                                             nsa_pe_k, nsa_w1_k, nsa_w2_k, nsa_pe_v, nsa_w1_v, nsa_w2_v)
            mix_s, st0_s = even_mixer_sample(rmsnorm(hs, norm_mix0), past_len, page_table,
                                             cache_a_k, cache_a_v, cache_a_logf,
                                             cache_b_cmp_k, cache_b_cmp_v, cache_b_slc_k, cache_b_slc_v,
                                             cache_b_swa_k, cache_b_swa_v, w_in0, fox_bf,
                                             nsa_pe_k, nsa_w1_k, nsa_w2_k, nsa_pe_v, nsa_w1_v, nsa_w2_v)
            hp = hp + mix_p.astype(hp.dtype) @ w_out0
            hs = hs + mix_s.astype(hs.dtype) @ w_out0
            hp = hp + swiglu(rmsnorm(hp, norm_ffn0), ffn_w_gate, ffn_w_up, ffn_w_down)
            hs = hs + swiglu(rmsnorm(hs, norm_ffn0), ffn_w_gate, ffn_w_up, ffn_w_down)
        else:
            mix_p, st1_p = odd_mixer_prompt(rmsnorm(hp, norm_mix1), w_in1)
            mix_s, st1_s = odd_mixer_sample(rmsnorm(hs, norm_mix1), past_len, cache_c_k, cache_c_v, w_in1)
            hp = hp + mix_p.astype(hp.dtype) @ w_out1
            hs = hs + mix_s.astype(hs.dtype) @ w_out1
            hp = hp + moe(rmsnorm(hp, norm_ffn1), moe_router, moe_w_gate, moe_w_up, moe_w_down)
            hs = hs + moe(rmsnorm(hs, norm_ffn1), moe_router, moe_w_gate, moe_w_up, moe_w_down)
    y_prompt = rmsnorm(hp, norm_final)
    y_sample = rmsnorm(hs, norm_final)
    p_a_k, p_a_v, p_a_logf, p_b_cmp_k, p_b_cmp_v, p_b_slc_k, p_b_slc_v, p_b_swa_k, p_b_swa_v = st0_p
    s_a_k, s_a_v, s_a_logf, s_b_cmp_k, s_b_cmp_v, s_b_slc_k, s_b_slc_v, s_b_swa_k, s_b_swa_v = st0_s
    p_c_k, p_c_v = st1_p
    s_c_k, s_c_v = st1_s
    return (y_prompt, y_sample,
            p_a_k, p_a_v, p_a_logf, p_b_cmp_k, p_b_cmp_v, p_b_slc_k, p_b_slc_v, p_b_swa_k, p_b_swa_v, p_c_k, p_c_v,
            s_a_k, s_a_v, s_a_logf, s_b_cmp_k, s_b_cmp_v, s_b_slc_k, s_b_slc_v, s_b_swa_k, s_b_swa_v, s_c_k, s_c_v)
```

```python
import functools
import math

import numpy as np
import jax
import jax.numpy as jnp
from jax import lax
from jax.experimental import pallas as pl
from jax.experimental.pallas import tpu as pltpu

F32 = jnp.float32
BF16 = jnp.bfloat16

HEAD_DIM = 64
HALF = HEAD_DIM // 2
LANES = 128
ROPE_THETA = 10000.0
RMS_EPS = 1e-6
NEG = -1e30
SCALE = HEAD_DIM ** -0.5

FOX_HEADS = 8
NSA_HEADS = 8
NSA_KV_HEADS = 2
NSA_GROUP = NSA_HEADS // NSA_KV_HEADS
NSA_CMP_LEN = 32
NSA_CMP_STRIDE = 16
NSA_SLC_BLOCK = 64
NSA_TOPN = 16
NSA_WINDOW = 512
NSA_FORCE_BONUS = 1e3
DIL_PAIRS = ((128, 1), (512, 4), (2048, 16))
DIL_WINDOW_MAX = 2048
TOP_K = 2
NEW_PAD = 16

VMEM_LIMIT = 56 * 1024 * 1024


def _tile(n, pref):
    return pref if n % pref == 0 else n


def _cparams(sem):
    return pltpu.CompilerParams(dimension_semantics=sem, vmem_limit_bytes=VMEM_LIMIT)


def _mm(a, b):
    return jnp.dot(a.astype(BF16), b.astype(BF16), preferred_element_type=F32)


def _mm_nt(a, b):
    return lax.dot_general(a.astype(BF16), b.astype(BF16), (((1,), (1,)), ((), ())),
                           preferred_element_type=F32)


def _mm_tn(a, b):
    return lax.dot_general(a.astype(BF16), b.astype(BF16), (((0,), (0,)), ((), ())),
                           preferred_element_type=F32)


def _split3(x):
    hi = x.astype(BF16)
    r = x - hi.astype(F32)
    mid = r.astype(BF16)
    lo = (r - mid.astype(F32)).astype(BF16)
    return hi, mid, lo


def _mm3_right(exact_lhs, x):
    a = exact_lhs.astype(BF16)
    hi, mid, lo = _split3(x)
    d = lambda p: jnp.dot(a, p, preferred_element_type=F32)
    return d(hi) + d(mid) + d(lo)


def _mm3_left(x, exact_rhs):
    b = exact_rhs.astype(BF16)
    hi, mid, lo = _split3(x)
    d = lambda p: jnp.dot(p, b, preferred_element_type=F32)
    return d(hi) + d(mid) + d(lo)


def _sigmoid(z):
    return 1.0 / (1.0 + jnp.exp(-z))


def _silu(z):
    return z * _sigmoid(z)


def _log_sigmoid(z):
    return jnp.minimum(z, 0.0) - jnp.log1p(jnp.exp(-jnp.abs(z)))


def _rmsnorm(x, g):
    return x * lax.rsqrt(jnp.mean(x * x, axis=-1, keepdims=True) + RMS_EPS) * g


def _rope(y, cos, sin_signed):
    n = y.shape[1]
    lane = lax.broadcasted_iota(jnp.int32, y.shape, 1)
    first = (lane % HEAD_DIM) < HALF
    rot = jnp.where(first, pltpu.roll(y, n - HALF, 1), pltpu.roll(y, HALF, 1))
    reps = n // LANES
    if reps > 1:
        cos = jnp.concatenate([cos] * reps, axis=1)
        sin_signed = jnp.concatenate([sin_signed] * reps, axis=1)
    return y * cos + rot * sin_signed


def _rope_tables(pos):
    inv = jnp.exp(-math.log(ROPE_THETA) * jnp.arange(HALF, dtype=F32) / HALF)
    ang = pos.astype(F32)[:, None] * inv[None, :]
    cos, sin = jnp.cos(ang), jnp.sin(ang)
    return (jnp.concatenate([cos, cos, cos, cos], axis=1),
            jnp.concatenate([-sin, sin, -sin, sin], axis=1))


def _proj_kernel(*refs, segs, with_small):
    x_ref, g_ref, w_ref, cos_ref, sin_ref = refs[:5]
    pos = 5
    if with_small:
        ws_ref, wft_ref, brow_ref, bcol_ref = refs[5:9]
        pos = 9
    outs = refs[pos:]
    xn = _rmsnorm(x_ref[...], g_ref[...]).astype(BF16)
    for (c0, width, rope), o_ref in zip(segs, outs):
        y = jnp.dot(xn, w_ref[:, c0:c0 + width], preferred_element_type=F32)
        if rope:
            y = _rope(y, cos_ref[...], sin_ref[...])
        o_ref[...] = y
    if with_small:
        small_ref, lft_ref = outs[len(segs):]
        ys = jnp.dot(xn, ws_ref[...], preferred_element_type=F32)
        lane = lax.broadcasted_iota(jnp.int32, ys.shape, 1)
        small_ref[...] = jnp.where(lane < FOX_HEADS, _log_sigmoid(ys + brow_ref[...]),
                                   jnp.where(lane < FOX_HEADS + 3 * NSA_HEADS, _sigmoid(ys), 0.0))
        yt = _mm_nt(wft_ref[...], xn)
        lft_ref[0] = _log_sigmoid(yt[0:FOX_HEADS] + bcol_ref[...])


def _project(x2d, gain, w_packed, segs, cos_t, sin_t, n_pos_blocks, tm, small=None):
    M, D = x2d.shape
    nt = M // tm
    in_specs = [pl.BlockSpec((tm, D), lambda i: (i, 0)),
                pl.BlockSpec((1, D), lambda i: (0, 0)),
                pl.BlockSpec(w_packed.shape, lambda i: (0, 0)),
                pl.BlockSpec((tm, LANES), lambda i: (i % n_pos_blocks, 0)),
                pl.BlockSpec((tm, LANES), lambda i: (i % n_pos_blocks, 0))]
    args = [x2d, gain.reshape(1, D), w_packed, cos_t, sin_t]
    out_shape = [jax.ShapeDtypeStruct((M, w), F32) for _, w, _ in segs]
    out_specs = [pl.BlockSpec((tm, w), lambda i: (i, 0)) for _, w, _ in segs]
    if small is not None:
        ws, wft, brow, bcol = small
        in_specs += [pl.BlockSpec(ws.shape, lambda i: (0, 0)), pl.BlockSpec(wft.shape, lambda i: (0, 0)),
                     pl.BlockSpec(brow.shape, lambda i: (0, 0)), pl.BlockSpec(bcol.shape, lambda i: (0, 0))]
        args += [ws, wft, brow, bcol]
        rows = n_pos_blocks * tm
        out_shape += [jax.ShapeDtypeStruct((M, LANES), F32),
                      jax.ShapeDtypeStruct((M // rows, FOX_HEADS, rows), F32)]
        out_specs += [pl.BlockSpec((tm, LANES), lambda i: (i, 0)),
                      pl.BlockSpec((1, FOX_HEADS, tm), lambda i: (i // n_pos_blocks, 0, i % n_pos_blocks))]
    return pl.pallas_call(
        functools.partial(_proj_kernel, segs=tuple(segs), with_small=small is not None),
        grid=(nt,), in_specs=in_specs, out_specs=out_specs, out_shape=out_shape,
        compiler_params=_cparams(("parallel",)))(*args)


def _cumsum_kernel(lf_ref, lft_ref, c_ref, ct_ref, crow_ref, ccol_ref):
    j = pl.program_id(1)

    @pl.when(j == 0)
    def _():
        crow_ref[...] = jnp.zeros_like(crow_ref)
        ccol_ref[...] = jnp.zeros_like(ccol_ref)

    tc = lf_ref.shape[1]
    r = lax.broadcasted_iota(jnp.int32, (tc, tc), 0)
    c = lax.broadcasted_iota(jnp.int32, (tc, tc), 1)
    csum = _mm3_right(c <= r, lf_ref[0]) + crow_ref[0:1, :]
    c_ref[0] = csum
    crow_ref[...] = jnp.broadcast_to(csum[tc - 1:tc, :], crow_ref.shape)
    ct = _mm3_left(lft_ref[0], r <= c) + ccol_ref[:, 0:1]
    ct_ref[0] = ct
    ccol_ref[...] = jnp.broadcast_to(ct[:, tc - 1:tc], ccol_ref.shape)


def _cumsum(small, lft):
    B, S, _ = small.shape
    tc = _tile(S, 512)
    return pl.pallas_call(
        _cumsum_kernel, grid=(B, S // tc),
        in_specs=[pl.BlockSpec((1, tc, LANES), lambda b, j: (b, j, 0)),
                  pl.BlockSpec((1, FOX_HEADS, tc), lambda b, j: (b, 0, j))],
        out_specs=[pl.BlockSpec((1, tc, LANES), lambda b, j: (b, j, 0)),
                   pl.BlockSpec((1, FOX_HEADS, tc), lambda b, j: (b, 0, j))],
        out_shape=[jax.ShapeDtypeStruct(small.shape, F32), jax.ShapeDtypeStruct(lft.shape, F32)],
        scratch_shapes=[pltpu.VMEM((8, LANES), F32), pltpu.VMEM((FOX_HEADS, LANES), F32)],
        compiler_params=_cparams(("parallel", "arbitrary")))(small, lft)


def _fox_kernel(q_ref, k_ref, v_ref, c_ref, ct_ref, o_ref, qs_ref, cq_ref, m_ref, l_ref, acc_ref):
    hp, qi, ki = pl.program_id(1), pl.program_id(2), pl.program_id(3)
    tq, tk = q_ref.shape[1], k_ref.shape[1]
    lane = lax.broadcasted_iota(jnp.int32, (tq, LANES), 1)

    @pl.when(ki == 0)
    def _():
        q = q_ref[0] * SCALE
        cfull = c_ref[0]
        for h in range(2):
            qs_ref[h] = jnp.where((lane // HEAD_DIM) == h, q, 0.0).astype(BF16)
            cq_ref[h] = jnp.sum(jnp.where(lane == 2 * hp + h, cfull, 0.0), axis=1, keepdims=True)
        m_ref[...] = jnp.full(m_ref.shape, NEG, F32)
        l_ref[...] = jnp.zeros_like(l_ref)
        acc_ref[...] = jnp.zeros_like(acc_ref)

    @pl.when(ki <= qi)
    def _():
        k = k_ref[0].astype(BF16)
        v = v_ref[0].astype(BF16)
        row = qi * tq + lax.broadcasted_iota(jnp.int32, (tq, tk), 0)
        col = ki * tk + lax.broadcasted_iota(jnp.int32, (tq, tk), 1)
        mask = col <= row
        for h in range(2):
            ck = ct_ref[0, pl.ds(2 * hp + h, 1), :]
            s = _mm_nt(qs_ref[h], k) + (cq_ref[h] - ck)
            s = jnp.where(mask, s, NEG)
            m_prev = m_ref[h]
            m_new = jnp.maximum(m_prev, jnp.max(s, axis=1, keepdims=True))
            alpha = jnp.exp(m_prev - m_new)
            p = jnp.exp(s - m_new)
            l_ref[h] = alpha * l_ref[h] + jnp.sum(p, axis=1, keepdims=True)
            acc_ref[h] = alpha * acc_ref[h] + jnp.dot(p.astype(BF16), v, preferred_element_type=F32)
            m_ref[h] = m_new

    @pl.when(ki == qi)
    def _():
        o_ref[0] = jnp.where(lane < HEAD_DIM, acc_ref[0] / l_ref[0], acc_ref[1] / l_ref[1])


def _fox_prompt(qa, ka, va, c, ct):
    B, S, W = qa.shape
    tq = _tile(S, 512)
    nq = S // tq
    kv_spec = pl.BlockSpec((1, tq, LANES), lambda b, hp, qi, ki: (b, jnp.minimum(ki, qi), hp))
    return pl.pallas_call(
        _fox_kernel, grid=(B, W // LANES, nq, nq),
        in_specs=[pl.BlockSpec((1, tq, LANES), lambda b, hp, qi, ki: (b, qi, hp)), kv_spec, kv_spec,
                  pl.BlockSpec((1, tq, LANES), lambda b, hp, qi, ki: (b, qi, 0)),
                  pl.BlockSpec((1, FOX_HEADS, tq), lambda b, hp, qi, ki: (b, 0, jnp.minimum(ki, qi)))],
        out_specs=pl.BlockSpec((1, tq, LANES), lambda b, hp, qi, ki: (b, qi, hp)),
        out_shape=jax.ShapeDtypeStruct(qa.shape, F32),
        scratch_shapes=[pltpu.VMEM((2, tq, LANES), BF16), pltpu.VMEM((2, tq, 1), F32),
                        pltpu.VMEM((2, tq, 1), F32), pltpu.VMEM((2, tq, 1), F32),
                        pltpu.VMEM((2, tq, LANES), F32)],
        compiler_params=_cparams(("parallel", "parallel", "parallel", "arbitrary")))(qa, ka, va, c, ct)


def _cmp_compute(x, pea_ref, peb_ref, wa_ref, wb_ref, w2_ref, o_ref, carry_ref):
    n = x.shape[0]
    a = jnp.dot((x + pea_ref[...]).astype(BF16), wa_ref[...], preferred_element_type=F32)
    b = jnp.dot((x + peb_ref[...]).astype(BF16), wb_ref[...], preferred_element_type=F32)
    rowi = lax.broadcasted_iota(jnp.int32, a.shape, 0)
    a_prev = jnp.where(rowi == 0, carry_ref[0:1, :], pltpu.roll(a, 1, 0))
    carry_ref[...] = jnp.broadcast_to(a[n - 1:n, :], carry_ref.shape)
    o_ref[0] = jnp.dot(_silu(a_prev + b).astype(BF16), w2_ref[...], preferred_element_type=F32)


def _cmp_kernel(x_ref, pea_ref, peb_ref, wa_ref, wb_ref, w2_ref, o_ref, carry_ref):
    @pl.when(pl.program_id(1) == 0)
    def _():
        carry_ref[...] = jnp.zeros_like(carry_ref)
    _cmp_compute(x_ref[0], pea_ref, peb_ref, wa_ref, wb_ref, w2_ref, o_ref, carry_ref)


def _cmp_paged_kernel(pt_ref, x_ref, pea_ref, peb_ref, wa_ref, wb_ref, w2_ref, o_ref, stage_ref, carry_ref,
                      *, group):
    p = pl.program_id(1)
    rows = x_ref.shape[1]

    @pl.when(p == 0)
    def _():
        carry_ref[...] = jnp.zeros_like(carry_ref)

    stage_ref[pl.ds(pl.multiple_of((p % group) * rows, rows), rows), :] = x_ref[0]

    @pl.when(p % group == group - 1)
    def _():
        _cmp_compute(stage_ref[...], pea_ref, peb_ref, wa_ref, wb_ref, w2_ref, o_ref, carry_ref)


def _cmp_weights(pe, w1, w2):
    eye = jnp.eye(NSA_KV_HEADS, dtype=F32)
    hid = w1.shape[2]
    half = NSA_CMP_STRIDE

    def wpart(w):
        return jnp.einsum('lde,hg->lhdge', w, eye).reshape(half * LANES, NSA_KV_HEADS * hid).astype(BF16)

    def ppart(p):
        return jnp.broadcast_to(p[:, None, :], (half, NSA_KV_HEADS, HEAD_DIM)).reshape(1, half * LANES)

    w2b = jnp.einsum('ed,hg->hegd', w2, eye).reshape(NSA_KV_HEADS * hid, LANES).astype(BF16)
    return ppart(pe[:half]), ppart(pe[half:]), wpart(w1[:half]), wpart(w1[half:]), w2b


def _compress(x, weights):
    N, L, _ = x.shape
    n_ch = L // NSA_CMP_STRIDE
    xc = x[:, :n_ch * NSA_CMP_STRIDE].reshape(N, n_ch, NSA_CMP_STRIDE * LANES)
    tch = _tile(n_ch, 256)
    wspecs = [pl.BlockSpec(w.shape, lambda n, j: (0, 0)) for w in weights]
    return pl.pallas_call(
        _cmp_kernel, grid=(N, n_ch // tch),
        in_specs=[pl.BlockSpec((1, tch, xc.shape[2]), lambda n, j: (n, j, 0))] + wspecs,
        out_specs=pl.BlockSpec((1, tch, LANES), lambda n, j: (n, j, 0)),
        out_shape=jax.ShapeDtypeStruct((N, n_ch, LANES), F32),
        scratch_shapes=[pltpu.VMEM((8, weights[2].shape[1]), F32)],
        compiler_params=_cparams(("parallel", "arbitrary")))(xc, *weights)


def _compress_paged(cache, pt_flat, n_db, weights):
    n_pool, page = cache.shape[:2]
    rows = page // NSA_CMP_STRIDE
    xc = cache.reshape(n_pool, rows, NSA_CMP_STRIDE * LANES)
    n_pages = pt_flat.shape[0] // n_db
    group = math.gcd(n_pages, max(1, 256 // rows))
    wspecs = [pl.BlockSpec(w.shape, lambda b, p, pt: (0, 0)) for w in weights]
    grid_spec = pltpu.PrefetchScalarGridSpec(
        num_scalar_prefetch=1, grid=(n_db, n_pages),
        in_specs=[pl.BlockSpec((1, rows, xc.shape[2]), lambda b, p, pt: (pt[b * n_pages + p], 0, 0))] + wspecs,
        out_specs=pl.BlockSpec((1, group * rows, LANES), lambda b, p, pt: (b, p // group, 0)),
        scratch_shapes=[pltpu.VMEM((group * rows, xc.shape[2]), F32),
                        pltpu.VMEM((8, weights[2].shape[1]), F32)])
    return pl.pallas_call(
        functools.partial(_cmp_paged_kernel, group=group), grid_spec=grid_spec,
        out_shape=jax.ShapeDtypeStruct((n_db, n_pages * rows, LANES), F32),
        compiler_params=_cparams(("parallel", "arbitrary")))(pt_flat, xc, *weights)


def _slc_map_rows(n_ch, ns_pad):
    i = (np.arange(n_ch)[:, None] - 1) * NSA_CMP_STRIDE
    j = np.arange(ns_pad)[None, :] * NSA_SLC_BLOCK
    shared = np.minimum(i + NSA_CMP_LEN, j + NSA_SLC_BLOCK) - np.maximum(i, j)
    m = np.clip(shared, 0, None) / NSA_CMP_LEN
    m[0, :] = 0.0
    return m.astype(np.float32)


def _select_blocks(imp, qpos, ns, n_sel):
    blk = lax.broadcasted_iota(jnp.int32, imp.shape, 1)
    cur = qpos // NSA_SLC_BLOCK
    valid = blk * NSA_SLC_BLOCK <= qpos
    forced = (blk == 0) | (blk == cur) | (blk == cur - 1)
    score = jnp.where(valid, imp + jnp.where(forced, NSA_FORCE_BONUS, 0.0), NEG)
    rank = jnp.zeros(imp.shape, jnp.int32)
    for jp in range(ns):
        sj = score[:, jp:jp + 1]
        beats = (sj > score) | ((sj == score) & (blk > jp))
        rank = rank + beats.astype(jnp.int32)
    return rank < n_sel


def _masked_softmax_rows(s, mask):
    sm = jnp.where(mask, s, NEG)
    m = jnp.max(sm, axis=1, keepdims=True)
    p = jnp.where(mask, jnp.exp(sm - m), 0.0)
    l = jnp.sum(p, axis=1, keepdims=True)
    return p / jnp.where(l > 0.0, l, 1.0)


def _nsa_kernel(q_ref, kc_ref, vc_ref, ks_ref, vs_ref, kw_ref, vw_ref, gt_ref, map_ref, o_ref,
                m_ref, l_ref, acc_ref, *, ns, n_sel, tk, win):
    g, i = pl.program_id(1), pl.program_id(2)
    tq = q_ref.shape[1]
    n_ch = kc_ref.shape[1]
    st = i * tq
    lane = lax.broadcasted_iota(jnp.int32, (tq, LANES), 1)
    in_g = (lane // HEAD_DIM) == g
    qpos = st + lax.broadcasted_iota(jnp.int32, (tq, 1), 0)
    rep = lambda a: jnp.concatenate([a] * NSA_GROUP, axis=0)

    qfull = q_ref[0]
    qs = []
    for j in range(NSA_GROUP):
        c = qfull[:, (j // 2) * LANES:(j // 2 + 1) * LANES] * SCALE
        same = jnp.where(g == (j % 2), c, pltpu.roll(c, HEAD_DIM, 1))
        qs.append(jnp.where(in_g, same, 0.0).astype(BF16))
    q4 = jnp.concatenate(qs, axis=0)

    cidx = lax.broadcasted_iota(jnp.int32, (tq, n_ch), 1)
    cmask = ((cidx - 1) * NSA_CMP_STRIDE + NSA_CMP_LEN - 1 <= qpos) & (cidx >= 1)
    pc = _masked_softmax_rows(_mm_nt(q4, kc_ref[0]), rep(cmask))
    oc = _mm(pc, vc_ref[0])
    pcat = jnp.concatenate([pc[j * tq:(j + 1) * tq] for j in range(NSA_GROUP)], axis=1)
    imp = jnp.dot(pcat.astype(BF16), map_ref[...], preferred_element_type=F32)
    sel = _select_blocks(imp, qpos, ns, n_sel).astype(BF16)

    m_ref[...] = jnp.full(m_ref.shape, NEG, F32)
    l_ref[...] = jnp.zeros_like(l_ref)
    acc_ref[...] = jnp.zeros_like(acc_ref)

    def body(kt, carry):
        k0 = pl.multiple_of(kt * tk, tk)
        k = ks_ref[0, pl.ds(k0, tk), :]
        v = vs_ref[0, pl.ds(k0, tk), :]
        s = _mm_nt(q4, k)
        kpos = k0 + lax.broadcasted_iota(jnp.int32, (tq, tk), 1)
        expand = ((k0 + lax.broadcasted_iota(jnp.int32, (LANES, tk), 1)) // NSA_SLC_BLOCK
                  == lax.broadcasted_iota(jnp.int32, (LANES, tk), 0)).astype(BF16)
        chosen = jnp.dot(sel, expand, preferred_element_type=F32) > 0.5
        km = rep(chosen & (kpos <= qpos))
        s = jnp.where(km, s, NEG)
        m_prev = m_ref[...]
        m_new = jnp.maximum(m_prev, jnp.max(s, axis=1, keepdims=True))
        alpha = jnp.exp(m_prev - m_new)
        p = jnp.where(km, jnp.exp(s - m_new), 0.0)
        l_ref[...] = alpha * l_ref[...] + jnp.sum(p, axis=1, keepdims=True)
        acc_ref[...] = alpha * acc_ref[...] + _mm(p, v)
        m_ref[...] = m_new
        return carry

    lax.fori_loop(0, (st + tq + tk - 1) // tk, body, 0)
    osl = acc_ref[...] / l_ref[...]

    w0 = pl.multiple_of(jnp.maximum(st + tq - win, 0), tq)
    kpos = w0 + lax.broadcasted_iota(jnp.int32, (tq, win), 1)
    dist = qpos - kpos
    wmask = (dist >= 0) & (dist <= NSA_WINDOW)
    pw = _masked_softmax_rows(_mm_nt(q4, kw_ref[0, pl.ds(w0, win), :]), rep(wmask))
    ow = _mm(pw, vw_ref[0, pl.ds(w0, win), :])

    gt = gt_ref[0]

    def gate(col):
        return jnp.sum(jnp.where(lane == col, gt, 0.0), axis=1, keepdims=True)

    placed = []
    for j in range(NSA_GROUP):
        base = FOX_HEADS + (NSA_GROUP * g + j) * 3
        rows = slice(j * tq, (j + 1) * tq)
        o = gate(base) * oc[rows] + gate(base + 1) * osl[rows] + gate(base + 2) * ow[rows]
        placed.append(jnp.where(g == (j % 2), o, pltpu.roll(o, HEAD_DIM, 1)))
    for c in range(NSA_GROUP // 2):
        o_ref[0, :, c * LANES:(c + 1) * LANES] = jnp.where(lane < HEAD_DIM, placed[2 * c], placed[2 * c + 1])


def _nsa_prompt(qb, kcmp, vcmp, ks, vs, kw, vw, small):
    B, S, W = qb.shape
    n_ch = kcmp.shape[1]
    ns = -(-S // NSA_SLC_BLOCK)
    assert ns <= LANES
    n_sel = min(NSA_TOPN, ns)
    tq = _tile(S, 128)
    tk = _tile(S, 256)
    win = min(NSA_WINDOW + tq, S)
    smap = jnp.asarray(np.tile(_slc_map_rows(n_ch, LANES), (NSA_GROUP, 1)), dtype=BF16)
    full = lambda a: pl.BlockSpec((1,) + a.shape[1:], lambda b, g, i: (b, 0, 0))
    gw = W // NSA_KV_HEADS
    return pl.pallas_call(
        functools.partial(_nsa_kernel, ns=ns, n_sel=n_sel, tk=tk, win=win),
        grid=(B, NSA_KV_HEADS, S // tq),
        in_specs=[pl.BlockSpec((1, tq, gw), lambda b, g, i: (b, i, g)),
                  full(kcmp), full(vcmp), full(ks), full(vs), full(kw), full(vw),
                  pl.BlockSpec((1, tq, LANES), lambda b, g, i: (b, i, 0)),
                  pl.BlockSpec(smap.shape, lambda b, g, i: (0, 0))],
        out_specs=pl.BlockSpec((1, tq, gw), lambda b, g, i: (b, i, g)),
        out_shape=jax.ShapeDtypeStruct(qb.shape, F32),
        scratch_shapes=[pltpu.VMEM((NSA_GROUP * tq, 1), F32), pltpu.VMEM((NSA_GROUP * tq, 1), F32),
                        pltpu.VMEM((NSA_GROUP * tq, LANES), F32)],
        compiler_params=_cparams(("parallel", "parallel", "arbitrary")))(
            qb, kcmp, vcmp, ks, vs, kw, vw, small, smap)


def _outproj_kernel(x_ref, o_ref, w_ref, y_ref):
    y_ref[...] = x_ref[...] + jnp.dot(o_ref[...].astype(BF16), w_ref[...], preferred_element_type=F32)


def _outproj(x2d, o2d, w_bf):
    M, D = x2d.shape
    K = o2d.shape[1]
    tm = _tile(M, 512)
    return pl.pallas_call(
        _outproj_kernel, grid=(M // tm,),
        in_specs=[pl.BlockSpec((tm, D), lambda i: (i, 0)), pl.BlockSpec((tm, K), lambda i: (i, 0)),
                  pl.BlockSpec((K, D), lambda i: (0, 0))],
        out_specs=pl.BlockSpec((tm, D), lambda i: (i, 0)),
        out_shape=jax.ShapeDtypeStruct((M, D), F32),
        compiler_params=_cparams(("parallel",)))(x2d, o2d, w_bf)


def _ffn_kernel(x_ref, g_ref, wg_ref, wu_ref, wd_ref, y_ref, xn_ref):
    f = pl.program_id(1)

    @pl.when(f == 0)
    def _():
        x = x_ref[...]
        xn_ref[...] = _rmsnorm(x, g_ref[...]).astype(BF16)
        y_ref[...] = x

    xn = xn_ref[...]
    h = _silu(jnp.dot(xn, wg_ref[...], preferred_element_type=F32)) * \
        jnp.dot(xn, wu_ref[...], preferred_element_type=F32)
    y_ref[...] += jnp.dot(h.astype(BF16), wd_ref[...], preferred_element_type=F32)


def _ffn(x2d, gain, wg, wu, wd):
    M, D = x2d.shape
    Fd = wg.shape[1]
    tm = _tile(M, 512)
    nf = 2 if Fd % (2 * LANES) == 0 else 1
    fc = Fd // nf
    return pl.pallas_call(
        _ffn_kernel, grid=(M // tm, nf),
        in_specs=[pl.BlockSpec((tm, D), lambda i, f: (i, 0)), pl.BlockSpec((1, D), lambda i, f: (0, 0)),
                  pl.BlockSpec((D, fc), lambda i, f: (0, f)), pl.BlockSpec((D, fc), lambda i, f: (0, f)),
                  pl.BlockSpec((fc, D), lambda i, f: (f, 0))],
        out_specs=pl.BlockSpec((tm, D), lambda i, f: (i, 0)),
        out_shape=jax.ShapeDtypeStruct((M, D), F32),
        scratch_shapes=[pltpu.VMEM((tm, D), BF16)],
        compiler_params=_cparams(("parallel", "arbitrary")))(x2d, gain.reshape(1, D), wg, wu, wd)


def _moe_kernel(x_ref, g_ref, wr_ref, wg_ref, wu_ref, wd_ref, gf_ref, y_ref, xn_ref, comb_ref, acc_ref,
                *, n_exp):
    e, f = pl.program_id(1), pl.program_id(2)
    tm = x_ref.shape[0]
    lane = lax.broadcasted_iota(jnp.int32, (tm, LANES), 1)

    @pl.when((e == 0) & (f == 0))
    def _():
        xn = _rmsnorm(x_ref[...], g_ref[...])
        xn_ref[...] = xn.astype(BF16)
        xh, xm, _ = _split3(xn)
        wh, wm, _ = _split3(wr_ref[...])
        d = lambda a, b: jnp.dot(a, b, preferred_element_type=F32)
        logits = jnp.where(lane < n_exp, d(xh, wh) + d(xh, wm) + d(xm, wh), NEG)
        v1 = jnp.max(logits, axis=1, keepdims=True)
        i1 = jnp.min(jnp.where(logits == v1, lane, LANES), axis=1, keepdims=True)
        rest = jnp.where(lane == i1, NEG, logits)
        v2 = jnp.max(rest, axis=1, keepdims=True)
        i2 = jnp.min(jnp.where(rest == v2, lane, LANES), axis=1, keepdims=True)
        ex = jnp.exp(v2 - v1)
        comb_ref[...] = jnp.where(lane == i1, 1.0 / (1.0 + ex), jnp.where(lane == i2, ex / (1.0 + ex), 0.0))
        acc_ref[...] = jnp.zeros_like(acc_ref)

    xn = xn_ref[...]
    h = _silu(jnp.dot(xn, wg_ref[0], preferred_element_type=F32)) * \
        jnp.dot(xn, wu_ref[0], preferred_element_type=F32)
    w = jnp.sum(jnp.where(lane == e, comb_ref[...], 0.0), axis=1, keepdims=True)
    acc_ref[...] += w * jnp.dot(h.astype(BF16), wd_ref[0], preferred_element_type=F32)

    @pl.when((e == n_exp - 1) & (f == pl.num_programs(2) - 1))
    def _():
        y_ref[...] = _rmsnorm(x_ref[...] + acc_ref[...], gf_ref[...])


def _moe_final(x2d, gain, w_router, wg, wu, wd, gain_final):
    M, D = x2d.shape
    n_exp, _, Fd = wg.shape
    tm = _tile(M, 512)
    nf = 2 if Fd % (2 * LANES) == 0 else 1
    fc = Fd // nf
    wr = jnp.zeros((D, LANES), F32).at[:, :n_exp].set(w_router)
    return pl.pallas_call(
        functools.partial(_moe_kernel, n_exp=n_exp), grid=(M // tm, n_exp, nf),
        in_specs=[pl.BlockSpec((tm, D), lambda i, e, f: (i, 0)), pl.BlockSpec((1, D), lambda i, e, f: (0, 0)),
                  pl.BlockSpec((D, LANES), lambda i, e, f: (0, 0)),
                  pl.BlockSpec((1, D, fc), lambda i, e, f: (e, 0, f)),
                  pl.BlockSpec((1, D, fc), lambda i, e, f: (e, 0, f)),
                  pl.BlockSpec((1, fc, D), lambda i, e, f: (e, f, 0)),
                  pl.BlockSpec((1, D), lambda i, e, f: (0, 0))],
        out_specs=pl.BlockSpec((tm, D), lambda i, e, f: (i, 0)),
        out_shape=jax.ShapeDtypeStruct((M, D), F32),
        scratch_shapes=[pltpu.VMEM((tm, D), BF16), pltpu.VMEM((tm, LANES), F32), pltpu.VMEM((tm, D), F32)],
        compiler_params=_cparams(("parallel", "arbitrary", "arbitrary")))(
            x2d, gain.reshape(1, D), wr, wg, wu, wd, gain_final.reshape(1, D))


def _dil_kernel(q_ref, k_ref, v_ref, o_ref, m_ref, l_ref, acc_ref, *, ta):
    S = q_ref.shape[1]
    lane = lax.broadcasted_iota(jnp.int32, (ta, LANES), 1)
    first = True
    for window, dil in DIL_PAIRS:
        span = window // dil
        L = S // dil
        n_tiles = L // ta
        kw = min(span + ta, L)
        is_first = first

        def body(it, carry, dil=dil, span=span, n_tiles=n_tiles, kw=kw, is_first=is_first):
            res = it // n_tiles
            a0 = (it % n_tiles) * ta
            ak0 = jnp.maximum(a0 + ta - kw, 0)
            qrows = pl.ds(res + dil * a0, ta, stride=dil) if dil > 1 else pl.ds(pl.multiple_of(a0, ta), ta)
            krows = pl.ds(res + dil * ak0, kw, stride=dil) if dil > 1 else pl.ds(pl.multiple_of(ak0, ta), kw)
            q = q_ref[0, qrows, :] * SCALE
            k = k_ref[0, krows, :].astype(BF16)
            v = v_ref[0, krows, :].astype(BF16)
            d = (a0 + lax.broadcasted_iota(jnp.int32, (ta, kw), 0)) - \
                (ak0 + lax.broadcasted_iota(jnp.int32, (ta, kw), 1))
            mask = (d >= 0) & (d <= span)
            ms, ls, accs = [], [], []
            for h in range(2):
                qh = jnp.where((lane // HEAD_DIM) == h, q, 0.0)
                s = jnp.where(mask, _mm_nt(qh, k), NEG)
                m = jnp.max(s, axis=1, keepdims=True)
                p = jnp.exp(s - m)
                ms.append(m)
                ls.append(jnp.sum(p, axis=1, keepdims=True))
                accs.append(jnp.dot(p.astype(BF16), v, preferred_element_type=F32))
            m2 = jnp.where(lane < HEAD_DIM, ms[0], ms[1])
            l2 = jnp.where(lane < HEAD_DIM, ls[0], ls[1])
            a2 = jnp.where(lane < HEAD_DIM, accs[0], accs[1])
            if not is_first:
                m_old = m_ref[qrows, :]
                m_new = jnp.maximum(m_old, m2)
                w_old = jnp.exp(m_old - m_new)
                w_new = jnp.exp(m2 - m_new)
                l2 = w_old * l_ref[qrows, :] + w_new * l2
                a2 = w_old * acc_ref[qrows, :] + w_new * a2
                m2 = m_new
            m_ref[qrows, :] = m2
            l_ref[qrows, :] = l2
            acc_ref[qrows, :] = a2
            return carry

        lax.fori_loop(0, dil * n_tiles, body, 0)
        first = False
    o_ref[0] = acc_ref[...] / l_ref[...]


def _dilated_prompt(q, k, v):
    B, S, W = q.shape
    ta = _tile(S // DIL_PAIRS[-1][1], 128)
    spec = pl.BlockSpec((1, S, LANES), lambda b, hp: (b, 0, hp))
    return pl.pallas_call(
        functools.partial(_dil_kernel, ta=ta), grid=(B, W // LANES),
        in_specs=[spec, spec, spec], out_specs=spec,
        out_shape=jax.ShapeDtypeStruct(q.shape, F32),
        scratch_shapes=[pltpu.VMEM((S, LANES), F32)] * 3,
        compiler_params=_cparams(("parallel", "parallel")))(q, k, v)


def _query_rows(q, n_heads, feat_of_head):
    T, Fw = q.shape
    rowi = lax.broadcasted_iota(jnp.int32, (LANES, Fw), 0)
    lane = lax.broadcasted_iota(jnp.int32, (LANES, Fw), 1)
    out = jnp.zeros((LANES, Fw), F32)
    for t in range(T):
        hit = (rowi // n_heads == t) & (lane // HEAD_DIM == feat_of_head(rowi % n_heads))
        out = jnp.where(hit, q[t:t + 1, :] * SCALE, out)
    return out.astype(BF16)


def _row_to_col(row):
    return jnp.transpose(jnp.broadcast_to(row, (8, LANES)))[:, 0:1]


def _dec_update(s, weight, v, m_ref, l_ref, acc_ref):
    live = weight > 0.0
    s = jnp.where(live, s, NEG)
    m_prev = m_ref[...]
    m_new = jnp.maximum(m_prev, jnp.max(s, axis=0, keepdims=True))
    alpha = jnp.exp(m_prev - m_new)
    p = jnp.where(live, weight * jnp.exp(s - m_new), 0.0)
    l_ref[...] = alpha * l_ref[...] + jnp.sum(p, axis=0, keepdims=True)
    acc_ref[...] = _row_to_col(alpha) * acc_ref[...] + _mm_tn(p, v)
    m_ref[...] = m_new


def _dec_init(m_ref, l_ref, acc_ref):
    m_ref[...] = jnp.full(m_ref.shape, NEG, F32)
    l_ref[...] = jnp.zeros_like(l_ref)
    acc_ref[...] = jnp.zeros_like(acc_ref)


def _head_lanes(x8, n_q, n_heads):
    r = lax.broadcasted_iota(jnp.int32, (n_heads, LANES), 0)
    c = lax.broadcasted_iota(jnp.int32, (n_heads, LANES), 1)
    return _mm3_left(x8, (c % n_heads == r) & (c < n_q * n_heads))


def _fox_dec_kernel(pt_ref, q_ref, kn_ref, vn_ref, lfn_ref, k_ref, v_ref, lf_ref, o_ref,
                    qt_ref, base_ref, carry_ref, m_ref, l_ref, acc_ref, *, n_q):
    p = pl.program_id(1)
    page = k_ref.shape[1]
    nh = FOX_HEADS
    lane = lax.broadcasted_iota(jnp.int32, (1, LANES), 1)
    t_of_lane = lane // nh

    @pl.when(p == 0)
    def _():
        qt = _query_rows(q_ref[0], nh, lambda h: h)
        qt_ref[...] = qt
        _dec_init(m_ref, l_ref, acc_ref)
        r = lax.broadcasted_iota(jnp.int32, (NEW_PAD, NEW_PAD), 0)
        c = lax.broadcasted_iota(jnp.int32, (NEW_PAD, NEW_PAD), 1)
        cnew = _head_lanes(_mm3_right(c <= r, lfn_ref[0][:, 0:nh]), n_q, nh)
        u = lax.broadcasted_iota(jnp.int32, (NEW_PAD, LANES), 0)
        base = jnp.zeros((1, LANES), F32)
        for t in range(n_q):
            base = jnp.where(t_of_lane == t, cnew[t:t + 1, :], base)
        base_ref[...] = base
        carry_ref[...] = jnp.zeros_like(carry_ref)
        s = _mm_nt(kn_ref[0], qt) + (base - cnew)
        w = ((u <= t_of_lane) & (t_of_lane < n_q)).astype(F32)
        _dec_update(s, w, vn_ref[0], m_ref, l_ref, acc_ref)

    r = lax.broadcasted_iota(jnp.int32, (page, page), 0)
    c = lax.broadcasted_iota(jnp.int32, (page, page), 1)
    lf = _head_lanes(lf_ref[0], n_q, nh)
    after = _mm3_right(c > r, lf)
    s = _mm_nt(k_ref[0], qt_ref[...]) + (base_ref[...] + carry_ref[...] + after)
    w = jnp.broadcast_to((t_of_lane < n_q).astype(F32), s.shape)
    _dec_update(s, w, v_ref[0], m_ref, l_ref, acc_ref)
    carry_ref[...] = carry_ref[...] + after[0:1, :] + lf[0:1, :]

    @pl.when(p == pl.num_programs(1) - 1)
    def _():
        o_ref[0] = acc_ref[...] * _row_to_col(1.0 / jnp.where(l_ref[...] > 0.0, l_ref[...], 1.0))


def _pad_rows(x, n):
    return jnp.pad(x, ((0, 0), (0, n - x.shape[1]), (0, 0)))


def _fox_decode(q, k_new, v_new, small_new, cache_k, cache_v, cache_lf, pt_flat):
    DB, T, W = q.shape
    n_pool, page = cache_k.shape[:2]
    n_pages = pt_flat.shape[0] // DB
    ck = cache_k.reshape(n_pool, page, W)
    cv = cache_v.reshape(n_pool, page, W)
    new = lambda a: pl.BlockSpec((1,) + a.shape[1:], lambda b, p, pt: (b, 0, 0))
    pg = lambda a: pl.BlockSpec((1,) + a.shape[1:], lambda b, p, pt: (pt[b * n_pages + n_pages - 1 - p], 0, 0))
    kn, vn, lfn = _pad_rows(k_new, NEW_PAD), _pad_rows(v_new, NEW_PAD), _pad_rows(small_new, NEW_PAD)
    grid_spec = pltpu.PrefetchScalarGridSpec(
        num_scalar_prefetch=1, grid=(DB, n_pages),
        in_specs=[new(q), new(kn), new(vn), new(lfn), pg(ck), pg(cv), pg(cache_lf)],
        out_specs=pl.BlockSpec((1, LANES, W), lambda b, p, pt: (b, 0, 0)),
        scratch_shapes=[pltpu.VMEM((LANES, W), BF16), pltpu.VMEM((1, LANES), F32), pltpu.VMEM((1, LANES), F32),
                        pltpu.VMEM((1, LANES), F32), pltpu.VMEM((1, LANES), F32), pltpu.VMEM((LANES, W), F32)])
    rows = pl.pallas_call(
        functools.partial(_fox_dec_kernel, n_q=T), grid_spec=grid_spec,
        out_shape=jax.ShapeDtypeStruct((DB, LANES, W), F32),
        compiler_params=_cparams(("parallel", "arbitrary")))(pt_flat, q, kn, vn, lfn, ck, cv, cache_lf)
    r = rows[:, :T * FOX_HEADS].reshape(DB, T, FOX_HEADS, FOX_HEADS, HEAD_DIM)
    return jnp.einsum('bthhd->bthd', r).reshape(DB, T, W)


def _nsa_dec_a_kernel(q_ref, kc_ref, vc_ref, kw_ref, vw_ref, kwn_ref, vwn_ref, gt_ref, map_ref,
                      o_ref, sel_ref, *, n_q, ns, n_sel, past):
    nr = NSA_HEADS * n_q
    q = q_ref[0]
    n_ch = kc_ref.shape[1]
    rowi = lax.broadcasted_iota(jnp.int32, (nr, 1), 0)
    t_row = rowi % n_q
    cidx = lax.broadcasted_iota(jnp.int32, (nr, n_ch), 1)
    pc = _masked_softmax_rows(_mm_nt(q, kc_ref[0]), cidx >= 1)
    oc = _mm(pc, vc_ref[0])
    grp = NSA_KV_HEADS * n_q
    pcat = jnp.concatenate([pc[j * grp:(j + 1) * grp] for j in range(NSA_GROUP)], axis=1)
    imp = jnp.dot(pcat.astype(BF16), map_ref[...], preferred_element_type=F32)
    qpos = past + lax.broadcasted_iota(jnp.int32, (grp, 1), 0) % n_q
    sel_ref[0] = _select_blocks(imp, qpos, ns, n_sel).astype(F32)

    wb = kw_ref.shape[1]
    sw = _mm_nt(q, kw_ref[0])
    sn = _mm_nt(q, kwn_ref[0])
    iw = lax.broadcasted_iota(jnp.int32, (nr, wb), 1)
    un = lax.broadcasted_iota(jnp.int32, (nr, NEW_PAD), 1)
    mw = (wb + t_row - iw <= NSA_WINDOW) & (past - wb + iw >= 0)
    mn = un <= t_row
    sw = jnp.where(mw, sw, NEG)
    sn = jnp.where(mn, sn, NEG)
    m = jnp.maximum(jnp.max(sw, axis=1, keepdims=True), jnp.max(sn, axis=1, keepdims=True))
    pw = jnp.where(mw, jnp.exp(sw - m), 0.0)
    pn = jnp.where(mn, jnp.exp(sn - m), 0.0)
    l = jnp.sum(pw, axis=1, keepdims=True) + jnp.sum(pn, axis=1, keepdims=True)
    ow = (_mm(pw, vw_ref[0]) + _mm(pn, vwn_ref[0])) / l
    gt = gt_ref[0]
    o_ref[0] = gt[:, 0:1] * oc + gt[:, 2:3] * ow


def _nsa_dec_b_kernel(pt_ref, q_ref, kn_ref, vn_ref, selt_ref, k_ref, v_ref, o_ref,
                      m_ref, l_ref, acc_ref, *, n_q, past):
    p = pl.program_id(1)
    page = k_ref.shape[1]
    per_page = page // NSA_SLC_BLOCK
    lane = lax.broadcasted_iota(jnp.int32, (1, LANES), 1)
    t_of_lane = lane // NSA_HEADS

    @pl.when(p == 0)
    def _():
        _dec_init(m_ref, l_ref, acc_ref)
        u = lax.broadcasted_iota(jnp.int32, (NEW_PAD, LANES), 0)
        chosen = selt_ref[0, pl.ds(past // NSA_SLC_BLOCK, 1), :]
        w = jnp.where((u <= t_of_lane) & (t_of_lane < n_q), chosen, 0.0)
        _dec_update(_mm_nt(kn_ref[0], q_ref[0]), w, vn_ref[0], m_ref, l_ref, acc_ref)

    key = lax.broadcasted_iota(jnp.int32, (page, LANES), 0)
    w = jnp.zeros((page, LANES), F32)
    for j in range(per_page):
        w = jnp.where(key // NSA_SLC_BLOCK == j, selt_ref[0, pl.ds(p * per_page + j, 1), :], w)
    w = jnp.where(t_of_lane < n_q, w, 0.0)
    _dec_update(_mm_nt(k_ref[0], q_ref[0]), w, v_ref[0], m_ref, l_ref, acc_ref)

    @pl.when(p == pl.num_programs(1) - 1)
    def _():
        o_ref[0] = acc_ref[...] * _row_to_col(1.0 / jnp.where(l_ref[...] > 0.0, l_ref[...], 1.0))


def _nsa_decode(qb, kcmp, vcmp, cache_ks, cache_vs, ks_new, vs_new, swa_k, swa_v, kw_new, vw_new,
                gates, pt_flat, past):
    DB, T, W = qb.shape
    n_pool, page = cache_ks.shape[:2]
    n_pages = pt_flat.shape[0] // DB
    n_ch = kcmp.shape[1]
    ns = -(-(past + T) // NSA_SLC_BLOCK)
    ns_pad = -(-ns // LANES) * LANES
    n_sel = min(NSA_TOPN, ns)
    nr = NSA_HEADS * T
    grp = NSA_KV_HEADS * T
    eye = jnp.eye(NSA_KV_HEADS, dtype=F32)
    q5 = qb.reshape(DB, T, NSA_KV_HEADS, NSA_GROUP, HEAD_DIM) * SCALE
    q_jkt = jnp.einsum('btkjd,kg->bjktgd', q5, eye).reshape(DB, nr, LANES)
    g_jkt = gates.reshape(DB, T, NSA_KV_HEADS, NSA_GROUP, 3).transpose(0, 3, 2, 1, 4).reshape(DB, nr, 3)
    smap = jnp.asarray(np.tile(_slc_map_rows(n_ch, ns_pad), (NSA_GROUP, 1)), dtype=BF16)
    wb = swa_k.shape[1]
    kw = swa_k.reshape(DB, wb, LANES)
    vw = swa_v.reshape(DB, wb, LANES)
    kwn, vwn = _pad_rows(kw_new, NEW_PAD), _pad_rows(vw_new, NEW_PAD)
    per = lambda a: pl.BlockSpec((1,) + a.shape[1:], lambda b: (b, 0, 0))
    part, sel = pl.pallas_call(
        functools.partial(_nsa_dec_a_kernel, n_q=T, ns=ns, n_sel=n_sel, past=past), grid=(DB,),
        in_specs=[per(q_jkt), per(kcmp), per(vcmp), per(kw), per(vw), per(kwn), per(vwn), per(g_jkt),
                  pl.BlockSpec(smap.shape, lambda b: (0, 0))],
        out_specs=[pl.BlockSpec((1, nr, LANES), lambda b: (b, 0, 0)),
                   pl.BlockSpec((1, grp, ns_pad), lambda b: (b, 0, 0))],
        out_shape=[jax.ShapeDtypeStruct((DB, nr, LANES), F32), jax.ShapeDtypeStruct((DB, grp, ns_pad), F32)],
        compiler_params=_cparams(("parallel",)))(q_jkt, kcmp, vcmp, kw, vw, kwn, vwn, g_jkt, smap)

    sel_kt = sel.reshape(DB, NSA_KV_HEADS, T, ns_pad)
    selt = jnp.broadcast_to(sel_kt.transpose(0, 3, 2, 1)[..., None], (DB, ns_pad, T, NSA_KV_HEADS, NSA_GROUP))
    selt = jnp.pad(selt.reshape(DB, ns_pad, nr), ((0, 0), (0, 0), (0, LANES - nr)))
    q_th = jnp.einsum('btkjd,kg->btkjgd', q5, eye).reshape(DB, nr, LANES)
    q_th = jnp.pad(q_th, ((0, 0), (0, LANES - nr), (0, 0))).astype(BF16)
    cks = cache_ks.reshape(n_pool, page, LANES)
    cvs = cache_vs.reshape(n_pool, page, LANES)
    ksn, vsn = _pad_rows(ks_new, NEW_PAD), _pad_rows(vs_new, NEW_PAD)
    new = lambda a: pl.BlockSpec((1,) + a.shape[1:], lambda b, p, pt: (b, 0, 0))
    pg = lambda a: pl.BlockSpec((1,) + a.shape[1:], lambda b, p, pt: (pt[b * n_pages + p], 0, 0))
    grid_spec = pltpu.PrefetchScalarGridSpec(
        num_scalar_prefetch=1, grid=(DB, n_pages),
        in_specs=[new(q_th), new(ksn), new(vsn), new(selt), pg(cks), pg(cvs)],
        out_specs=pl.BlockSpec((1, LANES, LANES), lambda b, p, pt: (b, 0, 0)),
        scratch_shapes=[pltpu.VMEM((1, LANES), F32), pltpu.VMEM((1, LANES), F32), pltpu.VMEM((LANES, LANES), F32)])
    osl = pl.pallas_call(
        functools.partial(_nsa_dec_b_kernel, n_q=T, past=past), grid_spec=grid_spec,
        out_shape=jax.ShapeDtypeStruct((DB, LANES, LANES), F32),
        compiler_params=_cparams(("parallel", "arbitrary")))(pt_flat, q_th, ksn, vsn, selt, cks, cvs)

    part5 = part.reshape(DB, NSA_GROUP, NSA_KV_HEADS, T, NSA_KV_HEADS, HEAD_DIM)
    part_o = jnp.einsum('bjktkd->btkjd', part5)
    osl5 = osl[:, :nr].reshape(DB, T, NSA_KV_HEADS, NSA_GROUP, NSA_KV_HEADS, HEAD_DIM)
    osl_o = jnp.einsum('btkjkd->btkjd', osl5)
    g1 = gates.reshape(DB, T, NSA_KV_HEADS, NSA_GROUP, 3)[..., 1:2]
    return (part_o + g1 * osl_o).reshape(DB, T, W)


def _dil_dec_kernel(q_ref, kn_ref, vn_ref, k_ref, v_ref, o_ref, qt_ref, m_ref, l_ref, acc_ref,
                    *, n_q, n_heads, wc):
    j = pl.program_id(1)
    tk = k_ref.shape[1]
    lane = lax.broadcasted_iota(jnp.int32, (1, LANES), 1)
    t_of_lane = lane // n_heads

    def multiplicity(d):
        w = jnp.zeros(d.shape, F32)
        for window, dil in DIL_PAIRS:
            w = w + ((d >= 0) & (d <= window) & (d % dil == 0)).astype(F32)
        return jnp.where(t_of_lane < n_q, w, 0.0)

    @pl.when(j == 0)
    def _():
        qt = _query_rows(q_ref[0], n_heads, lambda h: h)
        qt_ref[...] = qt
        _dec_init(m_ref, l_ref, acc_ref)
        u = lax.broadcasted_iota(jnp.int32, (NEW_PAD, LANES), 0)
        w = jnp.where(u < n_q, multiplicity(t_of_lane - u), 0.0)
        _dec_update(_mm_nt(kn_ref[0], qt), w, vn_ref[0], m_ref, l_ref, acc_ref)

    i = j * tk + lax.broadcasted_iota(jnp.int32, (tk, LANES), 0)
    _dec_update(_mm_nt(k_ref[0], qt_ref[...]), multiplicity(wc + t_of_lane - i), v_ref[0], m_ref, l_ref, acc_ref)

    @pl.when(j == pl.num_programs(1) - 1)
    def _():
        o_ref[0] = acc_ref[...] * _row_to_col(1.0 / jnp.where(l_ref[...] > 0.0, l_ref[...], 1.0))


def _dilated_decode(q, k_new, v_new, cache_k, cache_v):
    DB, T, W = q.shape
    wc = cache_k.shape[1]
    n_heads = W // HEAD_DIM
    assert T * n_heads <= LANES
    ck = cache_k.reshape(DB, wc, W)
    cv = cache_v.reshape(DB, wc, W)
    tk = _tile(wc, 256)
    kn, vn = _pad_rows(k_new, NEW_PAD), _pad_rows(v_new, NEW_PAD)
    new = lambda a: pl.BlockSpec((1,) + a.shape[1:], lambda b, j: (b, 0, 0))
    rows = pl.pallas_call(
        functools.partial(_dil_dec_kernel, n_q=T, n_heads=n_heads, wc=wc), grid=(DB, wc // tk),
        in_specs=[new(q), new(kn), new(vn),
                  pl.BlockSpec((1, tk, W), lambda b, j: (b, j, 0)), pl.BlockSpec((1, tk, W), lambda b, j: (b, j, 0))],
        out_specs=pl.BlockSpec((1, LANES, W), lambda b, j: (b, 0, 0)),
        out_shape=jax.ShapeDtypeStruct((DB, LANES, W), F32),
        scratch_shapes=[pltpu.VMEM((LANES, W), BF16), pltpu.VMEM((1, LANES), F32), pltpu.VMEM((1, LANES), F32),
                        pltpu.VMEM((LANES, W), F32)],
        compiler_params=_cparams(("parallel", "arbitrary")))(q, kn, vn, ck, cv)
    r = rows[:, :T * n_heads].reshape(DB, T, n_heads, n_heads, HEAD_DIM)
    return jnp.einsum('bthhd->bthd', r).reshape(DB, T, W)


def kernel(x_prompt, x_sample, cache_a_k, cache_a_v, cache_a_logf, cache_b_cmp_k, cache_b_cmp_v, cache_b_slc_k, cache_b_slc_v, cache_b_swa_k, cache_b_swa_v, cache_c_k, cache_c_v, page_table, norm_mix0, w_in0, fox_bf, nsa_pe_k, nsa_w1_k, nsa_w2_k, nsa_pe_v, nsa_w1_v, nsa_w2_v, w_out0, norm_ffn0, ffn_w_gate, ffn_w_up, ffn_w_down, norm_mix1, w_in1, w_out1, norm_ffn1, moe_router, moe_w_gate, moe_w_up, moe_w_down, norm_final):
    B, S, D = x_prompt.shape
    DB, T, _ = x_sample.shape
    n_pages = page_table.shape[1]
    past = n_pages * cache_a_k.shape[1]
    pt_flat = page_table.reshape(-1).astype(jnp.int32)
    fw = FOX_HEADS * HEAD_DIM
    nw = NSA_HEADS * HEAD_DIM
    kvw = NSA_KV_HEADS * HEAD_DIM

    cuts = np.cumsum([0, fw, fw, fw, FOX_HEADS, nw] + [kvw] * 6 + [3 * NSA_HEADS])
    col = lambda i: w_in0[:, cuts[i]:cuts[i + 1]]
    qa_w, ka_w, va_w, fa_w, qb_w, kc_w, vc_w, ks_w, vs_w, kw_w, vw_w, gb_w = [col(i) for i in range(12)]
    w0 = jnp.concatenate([qa_w, ka_w, va_w, qb_w, kc_w, ks_w, kw_w, vc_w, vs_w, vw_w], axis=1).astype(BF16)
    widths0 = [fw, fw, fw, nw, kvw, kvw, kvw, kvw, kvw, kvw]
    ropes0 = [False, False, False, True, True, True, True, False, False, False]
    starts0 = np.concatenate([[0], np.cumsum(widths0)[:-1]])
    segs0 = [(int(s), int(w), r) for s, w, r in zip(starts0, widths0, ropes0)]
    n_small = FOX_HEADS + 3 * NSA_HEADS
    ws = jnp.zeros((D, LANES), F32).at[:, :FOX_HEADS].set(fa_w).at[:, FOX_HEADS:n_small].set(gb_w).astype(BF16)
    wft = jnp.zeros((16, D), F32).at[:FOX_HEADS].set(fa_w.T).astype(BF16)
    brow = jnp.zeros((1, LANES), F32).at[0, :FOX_HEADS].set(fox_bf)
    bcol = fox_bf.reshape(FOX_HEADS, 1).astype(F32)
    small0 = (ws, wft, brow, bcol)
    w1 = w_in1.astype(BF16)
    dw = w_in1.shape[1] // 3
    segs1 = [(0, dw, True), (dw, dw, True), (2 * dw, dw, False)]
    w_out0_b, w_out1_b = w_out0.astype(BF16), w_out1.astype(BF16)
    ffn_g, ffn_u, ffn_d = ffn_w_gate.astype(BF16), ffn_w_up.astype(BF16), ffn_w_down.astype(BF16)
    moe_g, moe_u, moe_d = moe_w_gate.astype(BF16), moe_w_up.astype(BF16), moe_w_down.astype(BF16)
    cmp_k_w = _cmp_weights(nsa_pe_k, nsa_w1_k, nsa_w2_k)
    cmp_v_w = _cmp_weights(nsa_pe_v, nsa_w1_v, nsa_w2_v)

    cos_p, sin_p = _rope_tables(jnp.arange(S))
    pos_s = past + jnp.arange(DB * T) % T
    cos_s, sin_s = _rope_tables(pos_s)
    tm_p = _tile(S, 512)
    npb = S // tm_p

    xp = x_prompt.reshape(B * S, D)
    (qa, ka, va, qb, kc, ks, kw, vc, vs, vw, small_p, lft_p) = _project(
        xp, norm_mix0, w0, segs0, cos_p, sin_p, npb, tm_p, small0)
    r3 = lambda a: a.reshape(B, S, a.shape[-1])
    small_p3 = r3(small_p)
    c_p, ct_p = _cumsum(small_p3, lft_p)
    o_a = _fox_prompt(r3(qa), r3(ka), r3(va), c_p, ct_p)
    kcmp_p = _compress(r3(kc), cmp_k_w)
    vcmp_p = _compress(r3(vc), cmp_v_w)
    o_b = _nsa_prompt(r3(qb), kcmp_p, vcmp_p, r3(ks), r3(vs), r3(kw), r3(vw), small_p3)
    mix_p = jnp.concatenate([o_a, o_b], axis=-1).reshape(B * S, fw + nw)
    hp = _outproj(xp, mix_p, w_out0_b)
    hp = _ffn(hp, norm_ffn0, ffn_g, ffn_u, ffn_d)

    xs = x_sample.reshape(DB * T, D)
    (qa_s, ka_s, va_s, qb_s, kc_s, ks_s, kw_s, vc_s, vs_s, vw_s, small_s, _) = _project(
        xs, norm_mix0, w0, segs0, cos_s, sin_s, 1, DB * T, small0)
    s3 = lambda a: a.reshape(DB, T, a.shape[-1])
    small_s3 = s3(small_s)
    o_a_s = _fox_decode(s3(qa_s), s3(ka_s), s3(va_s), small_s3, cache_a_k, cache_a_v, cache_a_logf, pt_flat)
    kcmp_s = _compress_paged(cache_b_cmp_k, pt_flat, DB, cmp_k_w)
    vcmp_s = _compress_paged(cache_b_cmp_v, pt_flat, DB, cmp_v_w)
    gates_s = small_s3[..., FOX_HEADS:n_small]
    o_b_s = _nsa_decode(s3(qb_s), kcmp_s, vcmp_s, cache_b_slc_k, cache_b_slc_v, s3(ks_s), s3(vs_s),
                        cache_b_swa_k, cache_b_swa_v, s3(kw_s), s3(vw_s), gates_s, pt_flat, past)
    mix_s = jnp.concatenate([o_a_s, o_b_s], axis=-1).reshape(DB * T, fw + nw)
    hs = _outproj(xs, mix_s, w_out0_b)
    hs = _ffn(hs, norm_ffn0, ffn_g, ffn_u, ffn_d)

    q1, k1, v1 = _project(hp, norm_mix1, w1, segs1, cos_p, sin_p, npb, tm_p)
    o1 = _dilated_prompt(q1.reshape(B, S, dw), k1.reshape(B, S, dw), v1.reshape(B, S, dw))
    hp = _outproj(hp, o1.reshape(B * S, dw), w_out1_b)
    y_prompt = _moe_final(hp, norm_ffn1, moe_router, moe_g, moe_u, moe_d, norm_final).reshape(B, S, D)

    q1s, k1s, v1s = _project(hs, norm_mix1, w1, segs1, cos_s, sin_s, 1, DB * T)
    o1s = _dilated_decode(s3(q1s), s3(k1s), s3(v1s), cache_c_k, cache_c_v)
    hs = _outproj(hs, o1s.reshape(DB * T, dw), w_out1_b)
    y_sample = _moe_final(hs, norm_ffn1, moe_router, moe_g, moe_u, moe_d, norm_final).reshape(DB, T, D)

    win_b = min(NSA_WINDOW, S)
    win_c = min(DIL_WINDOW_MAX, S)
    h4 = lambda a, n, h: a.reshape(n, -1, h, HEAD_DIM)
    nh1 = dw // HEAD_DIM
    return (y_prompt, y_sample,
            h4(ka, B, FOX_HEADS), h4(va, B, FOX_HEADS), small_p3[..., :FOX_HEADS],
            h4(kc, B, NSA_KV_HEADS), h4(vc, B, NSA_KV_HEADS), h4(ks, B, NSA_KV_HEADS), h4(vs, B, NSA_KV_HEADS),
            h4(kw, B, NSA_KV_HEADS)[:, S - win_b:], h4(vw, B, NSA_KV_HEADS)[:, S - win_b:],
            h4(k1, B, nh1)[:, S - win_c:], h4(v1, B, nh1)[:, S - win_c:],
            h4(ka_s, DB, FOX_HEADS), h4(va_s, DB, FOX_HEADS), small_s3[..., :FOX_HEADS],
            h4(kc_s, DB, NSA_KV_HEADS), h4(vc_s, DB, NSA_KV_HEADS), h4(ks_s, DB, NSA_KV_HEADS),
            h4(vs_s, DB, NSA_KV_HEADS), h4(kw_s, DB, NSA_KV_HEADS), h4(vw_s, DB, NSA_KV_HEADS),
            h4(k1s, DB, nh1), h4(v1s, DB, nh1))
```

```python
import functools
import math

import numpy as np
import jax
import jax.numpy as jnp
from jax import lax
from jax.experimental import pallas as pl
from jax.experimental.pallas import tpu as pltpu

F32 = jnp.float32
BF16 = jnp.bfloat16

HEAD_DIM = 64
HALF = HEAD_DIM // 2
LANES = 128
SUBLANES = 8
ROPE_THETA = 10000.0
RMS_EPS = 1e-6
NEG = -1e30
MASK_BIG = 30000.0
SCALE = HEAD_DIM ** -0.5

FOX_HEADS = 8
NSA_HEADS = 8
NSA_KV_HEADS = 2
NSA_GROUP = NSA_HEADS // NSA_KV_HEADS
NSA_CMP_LEN = 32
NSA_CMP_STRIDE = 16
NSA_SLC_BLOCK = 64
NSA_TOPN = 16
NSA_WINDOW = 512
NSA_FORCE_BONUS = 1e3
DIL_PAIRS = ((128, 1), (512, 4), (2048, 16))
DIL_WINDOW_MAX = 2048
TOP_K = 2
QPAD = SUBLANES
NEW_PAD = LANES
PAGES_PER_STEP = 8

VMEM_LIMIT = 56 * 1024 * 1024


def _tile(n, pref):
    return pref if n % pref == 0 else n


def _cparams(sem):
    return pltpu.CompilerParams(dimension_semantics=sem, vmem_limit_bytes=VMEM_LIMIT)


def _mm(a, b):
    return jnp.dot(a.astype(BF16), b.astype(BF16), preferred_element_type=F32)


def _mm_nt(a, b):
    return lax.dot_general(a.astype(BF16), b.astype(BF16), (((1,), (1,)), ((), ())),
                           preferred_element_type=F32)


def _split3(x):
    hi = x.astype(BF16)
    r = x - hi.astype(F32)
    mid = r.astype(BF16)
    lo = (r - mid.astype(F32)).astype(BF16)
    return hi, mid, lo


def _mm3_left(x, exact_rhs):
    b = exact_rhs.astype(BF16)
    hi, mid, lo = _split3(x)
    d = lambda p: jnp.dot(p, b, preferred_element_type=F32)
    return d(hi) + d(mid) + d(lo)


def _sigmoid(z):
    return 1.0 / (1.0 + jnp.exp(-z))


def _silu(z):
    return z * _sigmoid(z)


def _log_sigmoid(z):
    return jnp.minimum(z, 0.0) - jnp.log1p(jnp.exp(-jnp.abs(z)))


def _rmsnorm(x, g):
    return x * lax.rsqrt(jnp.mean(x * x, axis=-1, keepdims=True) + RMS_EPS) * g


def _rope_rows(y, cos, sin_signed):
    n = y.shape[1]
    lane = lax.broadcasted_iota(jnp.int32, y.shape, 1)
    first = (lane % HEAD_DIM) < HALF
    rot = jnp.where(first, pltpu.roll(y, n - HALF, 1), pltpu.roll(y, HALF, 1))
    reps = n // LANES
    if reps > 1:
        cos = jnp.concatenate([cos] * reps, axis=1)
        sin_signed = jnp.concatenate([sin_signed] * reps, axis=1)
    return y * cos + rot * sin_signed


def _rope_cols(yt, cos_t, sin_t):
    out = []
    for h in range(yt.shape[0] // HEAD_DIM):
        a = yt[h * HEAD_DIM:h * HEAD_DIM + HALF]
        b = yt[h * HEAD_DIM + HALF:(h + 1) * HEAD_DIM]
        out += [a * cos_t - b * sin_t, b * cos_t + a * sin_t]
    return jnp.concatenate(out, axis=0)


def _rope_tables(pos):
    inv = jnp.exp(-math.log(ROPE_THETA) * jnp.arange(HALF, dtype=F32) / HALF)
    ang = pos.astype(F32)[:, None] * inv[None, :]
    cos, sin = jnp.cos(ang), jnp.sin(ang)
    return (jnp.concatenate([cos, cos, cos, cos], axis=1),
            jnp.concatenate([-sin, sin, -sin, sin], axis=1), cos.T, sin.T)


def _softmax_rows(s, mask):
    sm = jnp.where(mask, s, NEG)
    m = jnp.max(sm, axis=1, keepdims=True)
    p = jnp.where(mask, jnp.exp(sm - m), 0.0)
    l = jnp.sum(p, axis=1, keepdims=True)
    return p / jnp.where(l > 0.0, l, 1.0)


def _online_step(s, m_ref, l_ref):
    m_prev = m_ref[...]
    m_new = jnp.maximum(m_prev, jnp.max(s, axis=1, keepdims=True))
    alpha = jnp.exp(m_prev - m_new)
    p = jnp.exp(s - m_new)
    l_ref[...] = alpha * l_ref[...] + jnp.sum(p, axis=1, keepdims=True)
    m_ref[...] = m_new
    return alpha, p


def _online_init(m_ref, l_ref, acc_ref):
    m_ref[...] = jnp.full(m_ref.shape, NEG, F32)
    l_ref[...] = jnp.zeros_like(l_ref)
    acc_ref[...] = jnp.zeros_like(acc_ref)


def _proj_kernel(*refs, row_segs, col_segs, n_gate, with_logf, col_from):
    it = iter(refs)
    x_ref, g_ref, cos_ref, sin_ref, cost_ref, sint_ref = [next(it) for _ in range(6)]
    w_ref = next(it) if row_segs else None
    wt_ref = next(it) if col_segs else None
    wg_ref = next(it) if n_gate else None
    if with_logf:
        wft_ref, bcol_ref = next(it), next(it)
    outs = list(it)
    xn = _rmsnorm(x_ref[...], g_ref[...]).astype(BF16)
    k = 0
    for c0, width, rope in row_segs:
        y = jnp.dot(xn, w_ref[:, c0:c0 + width], preferred_element_type=F32)
        outs[k][...] = _rope_rows(y, cos_ref[...], sin_ref[...]) if rope else y
        k += 1

    def cols():
        kk = k
        for r0, height, rope in col_segs:
            yt = _mm_nt(wt_ref[r0:r0 + height, :], xn)
            outs[kk][0] = _rope_cols(yt, cost_ref[...], sint_ref[...]) if rope else yt
            kk += 1

    if col_segs:
        if col_from:
            pl.when(pl.program_id(0) % col_from[0] >= col_from[1])(cols)
        else:
            cols()
        k += len(col_segs)
    if n_gate:
        outs[k][...] = _sigmoid(jnp.dot(xn, wg_ref[...], preferred_element_type=F32))
        k += 1
    if with_logf:
        yt = _mm_nt(wft_ref[...], xn)
        outs[k][0] = _log_sigmoid(yt[0:FOX_HEADS] + bcol_ref[...])


def _project(x2d, gain, tables, n_pos_blocks, tm, w=None, row_segs=(), wt=None, col_segs=(),
             w_gate=None, logf=None, col_from=None, name="proj"):
    M, D = x2d.shape
    nt = M // tm
    n_seq = nt // n_pos_blocks
    cos_t, sin_t, cos_c, sin_c = tables
    pos_map = lambda i: (i % n_pos_blocks, 0)
    posc_map = lambda i: (0, i % n_pos_blocks)
    const = lambda a: pl.BlockSpec(a.shape, lambda i: (0,) * a.ndim)
    in_specs = [pl.BlockSpec((tm, D), lambda i: (i, 0)), const(gain.reshape(1, D)),
                pl.BlockSpec((tm, LANES), pos_map), pl.BlockSpec((tm, LANES), pos_map),
                pl.BlockSpec((HALF, tm), posc_map), pl.BlockSpec((HALF, tm), posc_map)]
    args = [x2d, gain.reshape(1, D), cos_t, sin_t, cos_c, sin_c]
    for a in (w, wt, w_gate):
        if a is not None:
            in_specs.append(const(a))
            args.append(a)
    out_shape = [jax.ShapeDtypeStruct((M, wd), F32) for _, wd, _ in row_segs]
    out_specs = [pl.BlockSpec((tm, wd), lambda i: (i, 0)) for _, wd, _ in row_segs]
    first = col_from[1] if col_from else 0
    n_cb = n_pos_blocks - first
    col_map = lambda i: (i // n_pos_blocks, 0, jnp.maximum(i % n_pos_blocks - first, 0))
    for _, ht, _ in col_segs:
        out_shape.append(jax.ShapeDtypeStruct((n_seq, ht, n_cb * tm), F32))
        out_specs.append(pl.BlockSpec((1, ht, tm), col_map))
    n_gate = 0
    if w_gate is not None:
        n_gate = w_gate.shape[1]
        out_shape.append(jax.ShapeDtypeStruct((M, n_gate), F32))
        out_specs.append(pl.BlockSpec((tm, n_gate), lambda i: (i, 0)))
    if logf is not None:
        in_specs += [const(logf[0]), const(logf[1])]
        args += list(logf)
        out_shape.append(jax.ShapeDtypeStruct((n_seq, FOX_HEADS, n_pos_blocks * tm), F32))
        out_specs.append(pl.BlockSpec((1, FOX_HEADS, tm), lambda i: (i // n_pos_blocks, 0, i % n_pos_blocks)))
    return pl.pallas_call(
        functools.partial(_proj_kernel, row_segs=tuple(row_segs), col_segs=tuple(col_segs), n_gate=n_gate,
                          with_logf=logf is not None, col_from=col_from),
        grid=(nt,), in_specs=in_specs, out_specs=out_specs, out_shape=out_shape, name=name,
        compiler_params=_cparams(("arbitrary",)))(*args)


def _cumsum_kernel(lft_ref, ct_ref, carry_ref):
    @pl.when(pl.program_id(1) == 0)
    def _():
        carry_ref[...] = jnp.zeros_like(carry_ref)

    tc = lft_ref.shape[2]
    r = lax.broadcasted_iota(jnp.int32, (tc, tc), 0)
    c = lax.broadcasted_iota(jnp.int32, (tc, tc), 1)
    ct = _mm3_left(lft_ref[0], r <= c) + carry_ref[:, 0:1]
    ct_ref[0] = ct
    carry_ref[...] = jnp.broadcast_to(ct[:, tc - 1:tc], carry_ref.shape)


def _cumsum(lft):
    B, H, S = lft.shape
    tc = _tile(S, 512)
    spec = pl.BlockSpec((1, H, tc), lambda b, j: (b, 0, j))
    return pl.pallas_call(
        _cumsum_kernel, grid=(B, S // tc), in_specs=[spec], out_specs=spec,
        out_shape=jax.ShapeDtypeStruct(lft.shape, F32), name="fox_cumsum",
        scratch_shapes=[pltpu.VMEM((H, LANES), F32)],
        compiler_params=_cparams(("parallel", "arbitrary")))(lft)


def _fox_kernel(qi_ref, ki_ref, q_ref, kt_ref, vt_ref, cq_ref, ck_ref, o_ref, qa_ref, m_ref, l_ref, acc_ref):
    hp, step = pl.program_id(1), pl.program_id(2)
    qi, ki = qi_ref[step], ki_ref[step]
    tq, tk = q_ref.shape[1], kt_ref.shape[3]
    lane = lax.broadcasted_iota(jnp.int32, (tq, LANES), 1)

    @pl.when(ki == 0)
    def _():
        q2 = q_ref[0] * SCALE
        cq_cols = jnp.transpose(cq_ref[0])
        hcol = lax.broadcasted_iota(jnp.int32, cq_cols.shape, 1)
        for h in range(2):
            qh = q2 if h == 0 else pltpu.roll(q2, HEAD_DIM, 1)
            cq = jnp.sum(jnp.where(hcol == 2 * hp + h, cq_cols, 0.0), axis=1, keepdims=True)
            hi, mid, lo = [t.astype(F32) for t in _split3(cq)]
            ext = jnp.where(lane == HEAD_DIM, hi, jnp.where(lane == HEAD_DIM + 1, mid, jnp.where(
                lane == HEAD_DIM + 2, lo, jnp.where(lane < HEAD_DIM + 6, 1.0, 0.0))))
            qa_ref[h] = jnp.where(lane < HEAD_DIM, qh, ext).astype(BF16)
        _online_init(m_ref, l_ref, acc_ref)

    def tile(diagonal):
        row8 = lax.broadcasted_iota(jnp.int32, (16, tk), 0)
        pad = jnp.zeros((LANES - HEAD_DIM - 16, tk), BF16)
        if diagonal:
            causal = lax.broadcasted_iota(jnp.int32, (tq, tk), 1) <= lax.broadcasted_iota(jnp.int32, (tq, tk), 0)
        for h in range(2):
            ck = ck_ref[0, pl.ds(2 * hp + h, 1), :]
            hi, mid, lo = [t.astype(F32) for t in _split3(-ck)]
            ext = jnp.where(row8 < 3, 1.0, jnp.where(row8 == 3, hi, jnp.where(row8 == 4, mid, jnp.where(
                row8 == 5, lo, 0.0))))
            kaug = jnp.concatenate([kt_ref[0, h].astype(BF16), ext.astype(BF16), pad], axis=0)
            s = jnp.dot(qa_ref[h], kaug, preferred_element_type=F32)
            if diagonal:
                s = jnp.where(causal, s, NEG)
            alpha, p = _online_step(s, m_ref.at[h], l_ref.at[h])
            acc_ref[h] = alpha * acc_ref[h] + _mm_nt(p, vt_ref[0, h])

    pl.when(ki < qi)(lambda: tile(False))

    @pl.when(ki == qi)
    def _():
        tile(True)
        o_ref[0] = jnp.concatenate([acc_ref[0] / l_ref[0], acc_ref[1] / l_ref[1]], axis=1)


def _fox_prompt(qa, kat, vat, ct):
    B, S, W = qa.shape
    tq = _tile(S, 512)
    nq = S // tq
    pairs = [(q, k) for q in range(nq) for k in range(q + 1)]
    qi_tab = jnp.asarray([p[0] for p in pairs], jnp.int32)
    ki_tab = jnp.asarray([p[1] for p in pairs], jnp.int32)
    kv_spec = pl.BlockSpec((1, 2, HEAD_DIM, tq), lambda b, hp, s, qi, ki: (b, hp, 0, ki[s]))
    grid_spec = pltpu.PrefetchScalarGridSpec(
        num_scalar_prefetch=2, grid=(B, W // LANES, len(pairs)),
        in_specs=[pl.BlockSpec((1, tq, LANES), lambda b, hp, s, qi, ki: (b, qi[s], hp)), kv_spec, kv_spec,
                  pl.BlockSpec((1, FOX_HEADS, tq), lambda b, hp, s, qi, ki: (b, 0, qi[s])),
                  pl.BlockSpec((1, FOX_HEADS, tq), lambda b, hp, s, qi, ki: (b, 0, ki[s]))],
        out_specs=pl.BlockSpec((1, tq, LANES), lambda b, hp, s, qi, ki: (b, qi[s], hp)),
        scratch_shapes=[pltpu.VMEM((2, tq, LANES), BF16), pltpu.VMEM((2, tq, 1), F32),
                        pltpu.VMEM((2, tq, 1), F32), pltpu.VMEM((2, tq, HEAD_DIM), F32)])
    return pl.pallas_call(
        _fox_kernel, grid_spec=grid_spec, out_shape=jax.ShapeDtypeStruct(qa.shape, F32), name="fox_prompt",
        compiler_params=_cparams(("parallel", "parallel", "arbitrary")))(qi_tab, ki_tab, qa, kat, vat, ct, ct)


def _cmp_compute(x, pea_ref, peb_ref, wa_ref, wb_ref, w2t_ref, o_ref, carry_ref):
    n = x.shape[0]
    a = jnp.dot((x + pea_ref[...]).astype(BF16), wa_ref[...], preferred_element_type=F32)
    b = jnp.dot((x + peb_ref[...]).astype(BF16), wb_ref[...], preferred_element_type=F32)
    rowi = lax.broadcasted_iota(jnp.int32, a.shape, 0)
    a_prev = jnp.where(rowi == 0, carry_ref[0:1, :], pltpu.roll(a, 1, 0))
    carry_ref[...] = jnp.broadcast_to(a[n - 1:n, :], carry_ref.shape)
    o_ref[0] = _mm_nt(w2t_ref[...], _silu(a_prev + b))


def _cmp_kernel(x_ref, pea_ref, peb_ref, wa_ref, wb_ref, w2t_ref, o_ref, carry_ref):
    @pl.when(pl.program_id(1) == 0)
    def _():
        carry_ref[...] = jnp.zeros_like(carry_ref)
    _cmp_compute(x_ref[0], pea_ref, peb_ref, wa_ref, wb_ref, w2t_ref, o_ref, carry_ref)


def _cmp_paged_kernel(pt_ref, *refs, n_pg):
    pages = refs[:n_pg]
    pea_ref, peb_ref, wa_ref, wb_ref, w2t_ref, o_ref, carry_ref = refs[n_pg:]

    @pl.when(pl.program_id(1) == 0)
    def _():
        carry_ref[...] = jnp.zeros_like(carry_ref)

    x = jnp.concatenate([r[0] for r in pages], axis=0)
    _cmp_compute(x, pea_ref, peb_ref, wa_ref, wb_ref, w2t_ref, o_ref, carry_ref)


def _cmp_weights(pe, w1, w2):
    eye = jnp.eye(NSA_KV_HEADS, dtype=F32)
    hid = w1.shape[2]
    half = NSA_CMP_STRIDE

    def wpart(w):
        return jnp.einsum('lde,hg->lhdge', w, eye).reshape(half * LANES, NSA_KV_HEADS * hid).astype(BF16)

    def ppart(p):
        return jnp.broadcast_to(p[:, None, :], (half, NSA_KV_HEADS, HEAD_DIM)).reshape(1, half * LANES)

    w2t = jnp.einsum('ed,hg->gdhe', w2, eye).reshape(LANES, NSA_KV_HEADS * hid).astype(BF16)
    return ppart(pe[:half]), ppart(pe[half:]), wpart(w1[:half]), wpart(w1[half:]), w2t


def _compress(x, weights, name):
    N, L, _ = x.shape
    n_ch = L // NSA_CMP_STRIDE
    xc = x[:, :n_ch * NSA_CMP_STRIDE].reshape(N, n_ch, NSA_CMP_STRIDE * LANES)
    tch = _tile(n_ch, 256)
    wspecs = [pl.BlockSpec(w.shape, lambda n, j: (0, 0)) for w in weights]
    return pl.pallas_call(
        _cmp_kernel, grid=(N, n_ch // tch),
        in_specs=[pl.BlockSpec((1, tch, xc.shape[2]), lambda n, j: (n, j, 0))] + wspecs,
        out_specs=pl.BlockSpec((1, LANES, tch), lambda n, j: (n, 0, j)),
        out_shape=jax.ShapeDtypeStruct((N, LANES, n_ch), F32), name=name,
        scratch_shapes=[pltpu.VMEM((8, weights[2].shape[1]), F32)],
        compiler_params=_cparams(("parallel", "arbitrary")))(xc, *weights)


def _compress_paged(cache, pt_flat, n_db, weights, name):
    n_pool, page = cache.shape[:2]
    rows = page // NSA_CMP_STRIDE
    xc = cache.reshape(n_pool, rows, NSA_CMP_STRIDE * LANES)
    n_pages = pt_flat.shape[0] // n_db
    n_pg = math.gcd(n_pages, max(1, 256 // rows))
    wspecs = [pl.BlockSpec(w.shape, lambda b, p, pt: (0, 0)) for w in weights]
    page_spec = lambda j: pl.BlockSpec((1, rows, xc.shape[2]),
                                       lambda b, p, pt: (pt[b * n_pages + p * n_pg + j], 0, 0))
    grid_spec = pltpu.PrefetchScalarGridSpec(
        num_scalar_prefetch=1, grid=(n_db, n_pages // n_pg),
        in_specs=[page_spec(j) for j in range(n_pg)] + wspecs,
        out_specs=pl.BlockSpec((1, LANES, n_pg * rows), lambda b, p, pt: (b, 0, p)),
        scratch_shapes=[pltpu.VMEM((8, weights[2].shape[1]), F32)])
    return pl.pallas_call(
        functools.partial(_cmp_paged_kernel, n_pg=n_pg), grid_spec=grid_spec,
        out_shape=jax.ShapeDtypeStruct((n_db, LANES, n_pages * rows), F32), name=name,
        compiler_params=_cparams(("parallel", "arbitrary")))(pt_flat, *([xc] * n_pg), *weights)


def _slc_map_rows(n_ch, ns_pad):
    i = (np.arange(n_ch)[:, None] - 1) * NSA_CMP_STRIDE
    j = np.arange(ns_pad)[None, :] * NSA_SLC_BLOCK
    shared = np.minimum(i + NSA_CMP_LEN, j + NSA_SLC_BLOCK) - np.maximum(i, j)
    m = np.clip(shared, 0, None) / NSA_CMP_LEN
    m[0, :] = 0.0
    return m.astype(np.float32)


def _select_blocks(imp, qpos, ns, n_sel):
    blk = lax.broadcasted_iota(jnp.int32, imp.shape, 1)
    cur = qpos // NSA_SLC_BLOCK
    valid = blk * NSA_SLC_BLOCK <= qpos
    forced = (blk == 0) | (blk == cur) | (blk == cur - 1)
    score = jnp.where(valid, imp + jnp.where(forced, NSA_FORCE_BONUS, 0.0), NEG)
    rank = jnp.zeros(imp.shape, jnp.int32)
    for jp in range(ns):
        sj = score[:, jp:jp + 1]
        beats = (sj > score) | ((sj == score) & (blk > jp))
        rank = rank + beats.astype(jnp.int32)
    return rank < n_sel


def _nsa_kernel(q_ref, kct_ref, vct_ref, kst_ref, vst_ref, kwt_ref, vwt_ref, gt_ref, map_ref, o_ref,
                qa_ref, m_ref, l_ref, acc_ref, *, ns, n_sel, tk, win):
    g, i = pl.program_id(1), pl.program_id(2)
    tq = q_ref.shape[1]
    n_ch = kct_ref.shape[3]
    st = i * tq
    lane = lax.broadcasted_iota(jnp.int32, (tq, LANES), 1)
    qpos = st + lax.broadcasted_iota(jnp.int32, (tq, 1), 0)
    rep = lambda a: jnp.concatenate([a] * NSA_GROUP, axis=0)

    qfull = q_ref[0]
    qh = []
    for j in range(NSA_GROUP):
        c = qfull[:, (j // 2) * LANES:(j // 2 + 1) * LANES] * SCALE
        qh.append(c if j % 2 == 0 else pltpu.roll(c, HEAD_DIM, 1))
    q4 = jnp.concatenate([t[:, :HEAD_DIM] for t in qh], axis=0).astype(BF16)

    cidx = lax.broadcasted_iota(jnp.int32, (tq, n_ch), 1)
    cmask = ((cidx - 1) * NSA_CMP_STRIDE + NSA_CMP_LEN - 1 <= qpos) & (cidx >= 1)
    pc = _softmax_rows(_mm(q4, kct_ref[0, 0]), rep(cmask))
    oc = _mm_nt(pc, vct_ref[0, 0])
    pcat = jnp.concatenate([pc[j * tq:(j + 1) * tq] for j in range(NSA_GROUP)], axis=1)
    imp = jnp.dot(pcat.astype(BF16), map_ref[...], preferred_element_type=F32)
    sel = _select_blocks(imp, qpos, ns, n_sel)

    negm = pltpu.roll(jnp.where(sel, 0.0, -MASK_BIG), HEAD_DIM, 1)
    for j in range(NSA_GROUP):
        qa_ref[j * tq:(j + 1) * tq, :] = jnp.where(lane < HEAD_DIM, qh[j], negm).astype(BF16)
    _online_init(m_ref, l_ref, acc_ref)

    def tile(kt, diagonal):
        k0 = pl.multiple_of(kt * tk, tk)
        kt_tile = kst_ref[0, 0, :, pl.ds(k0, tk)].astype(BF16)
        onehot = ((k0 + lax.broadcasted_iota(jnp.int32, (HEAD_DIM, tk), 1)) // NSA_SLC_BLOCK
                  == lax.broadcasted_iota(jnp.int32, (HEAD_DIM, tk), 0)).astype(BF16)
        s = jnp.dot(qa_ref[...], jnp.concatenate([kt_tile, onehot], axis=0), preferred_element_type=F32)
        if diagonal:
            kpos = k0 + lax.broadcasted_iota(jnp.int32, (tq, tk), 1)
            s = jnp.where(rep(kpos <= qpos), s, NEG)
        alpha, p = _online_step(s, m_ref, l_ref)
        acc_ref[...] = alpha * acc_ref[...] + _mm_nt(p, vst_ref[0, 0, :, pl.ds(k0, tk)])

    last = st // tk

    def body(kt, carry):
        tile(kt, False)
        return carry

    lax.fori_loop(0, last, body, 0)
    tile(last, True)
    osl = acc_ref[...] / l_ref[...]

    w0 = pl.multiple_of(jnp.maximum(st + tq - win, 0), tq)
    kpos = w0 + lax.broadcasted_iota(jnp.int32, (tq, win), 1)
    dist = qpos - kpos
    pw = _softmax_rows(_mm(q4, kwt_ref[0, 0, :, pl.ds(w0, win)]), rep((dist >= 0) & (dist <= NSA_WINDOW)))
    ow = _mm_nt(pw, vwt_ref[0, 0, :, pl.ds(w0, win)])

    gt = gt_ref[...]
    glane = lax.broadcasted_iota(jnp.int32, gt.shape, 1)

    def gate(col):
        return jnp.sum(jnp.where(glane == col, gt, 0.0), axis=1, keepdims=True)

    outs = []
    for j in range(NSA_GROUP):
        base = (NSA_GROUP * g + j) * 3
        rows = slice(j * tq, (j + 1) * tq)
        outs.append(gate(base) * oc[rows] + gate(base + 1) * osl[rows] + gate(base + 2) * ow[rows])
    o_ref[0] = jnp.concatenate(outs, axis=1)


def _nsa_prompt(qb, kct, vct, kst, vst, kwt, vwt, gates):
    B, S, W = qb.shape
    n_ch = kct.shape[3]
    ns = -(-S // NSA_SLC_BLOCK)
    assert ns <= HEAD_DIM
    n_sel = min(NSA_TOPN, ns)
    tq = _tile(S, 128)
    tk = _tile(S, 512)
    win = min(NSA_WINDOW + tq, S)
    smap = jnp.asarray(np.tile(_slc_map_rows(n_ch, LANES), (NSA_GROUP, 1)), dtype=BF16)
    per_head = lambda a: pl.BlockSpec((1, 1) + a.shape[2:], lambda b, g, i: (b, g, 0, 0))
    gw = W // NSA_KV_HEADS
    nq = S // tq
    return pl.pallas_call(
        functools.partial(_nsa_kernel, ns=ns, n_sel=n_sel, tk=tk, win=win),
        grid=(B, NSA_KV_HEADS, nq),
        in_specs=[pl.BlockSpec((1, tq, gw), lambda b, g, i: (b, i, g)),
                  per_head(kct), per_head(vct), per_head(kst), per_head(vst), per_head(kwt), per_head(vwt),
                  pl.BlockSpec((tq, gates.shape[1]), lambda b, g, i: (b * nq + i, 0)),
                  pl.BlockSpec(smap.shape, lambda b, g, i: (0, 0))],
        out_specs=pl.BlockSpec((1, tq, gw), lambda b, g, i: (b, i, g)),
        out_shape=jax.ShapeDtypeStruct(qb.shape, F32), name="nsa_prompt",
        scratch_shapes=[pltpu.VMEM((NSA_GROUP * tq, LANES), BF16), pltpu.VMEM((NSA_GROUP * tq, 1), F32),
                        pltpu.VMEM((NSA_GROUP * tq, 1), F32), pltpu.VMEM((NSA_GROUP * tq, HEAD_DIM), F32)],
        compiler_params=_cparams(("parallel", "parallel", "arbitrary")))(
            qb, kct, vct, kst, vst, kwt, vwt, gates, smap)


def _outproj_kernel(*refs):
    x_ref, w_ref, y_ref = refs[0], refs[-2], refs[-1]
    y = x_ref[...]
    k0 = 0
    for o_ref in refs[1:-2]:
        kw = o_ref.shape[1]
        y = y + jnp.dot(o_ref[...].astype(BF16), w_ref[k0:k0 + kw, :], preferred_element_type=F32)
        k0 += kw
    y_ref[...] = y


def _outproj(x2d, parts, w_bf, name):
    M, D = x2d.shape
    tm = _tile(M, 512)
    return pl.pallas_call(
        _outproj_kernel, grid=(M // tm,),
        in_specs=[pl.BlockSpec((tm, D), lambda i: (i, 0))]
        + [pl.BlockSpec((tm, o.shape[1]), lambda i: (i, 0)) for o in parts]
        + [pl.BlockSpec(w_bf.shape, lambda i: (0, 0))],
        out_specs=pl.BlockSpec((tm, D), lambda i: (i, 0)),
        out_shape=jax.ShapeDtypeStruct((M, D), F32), name=name,
        compiler_params=_cparams(("parallel",)))(x2d, *parts, w_bf)


def _ffn_kernel(x_ref, g_ref, wg_ref, wu_ref, wd_ref, y_ref, xn_ref):
    f = pl.program_id(1)

    @pl.when(f == 0)
    def _():
        x = x_ref[...]
        xn_ref[...] = _rmsnorm(x, g_ref[...]).astype(BF16)
        y_ref[...] = x

    xn = xn_ref[...]
    h = _silu(jnp.dot(xn, wg_ref[...], preferred_element_type=F32)) * \
        jnp.dot(xn, wu_ref[...], preferred_element_type=F32)
    y_ref[...] += jnp.dot(h.astype(BF16), wd_ref[...], preferred_element_type=F32)


def _ffn(x2d, gain, wg, wu, wd, name):
    M, D = x2d.shape
    Fd = wg.shape[1]
    tm = _tile(M, 512)
    nf = 2 if Fd % (2 * LANES) == 0 else 1
    fc = Fd // nf
    return pl.pallas_call(
        _ffn_kernel, grid=(M // tm, nf),
        in_specs=[pl.BlockSpec((tm, D), lambda i, f: (i, 0)), pl.BlockSpec((1, D), lambda i, f: (0, 0)),
                  pl.BlockSpec((D, fc), lambda i, f: (0, f)), pl.BlockSpec((D, fc), lambda i, f: (0, f)),
                  pl.BlockSpec((fc, D), lambda i, f: (f, 0))],
        out_specs=pl.BlockSpec((tm, D), lambda i, f: (i, 0)),
        out_shape=jax.ShapeDtypeStruct((M, D), F32), name=name,
        scratch_shapes=[pltpu.VMEM((tm, D), BF16)],
        compiler_params=_cparams(("parallel", "arbitrary")))(x2d, gain.reshape(1, D), wg, wu, wd)


def _moe_kernel(x_ref, g_ref, wr_ref, wg_ref, wu_ref, wd_ref, gf_ref, y_ref, xn_ref, comb_ref, acc_ref,
                *, n_exp):
    e, f = pl.program_id(1), pl.program_id(2)
    tm = x_ref.shape[0]
    lane = lax.broadcasted_iota(jnp.int32, (tm, LANES), 1)

    @pl.when((e == 0) & (f == 0))
    def _():
        xn = _rmsnorm(x_ref[...], g_ref[...])
        xn_ref[...] = xn.astype(BF16)
        xh, xm, _ = _split3(xn)
        wh, wm, _ = _split3(wr_ref[...])
        d = lambda a, b: jnp.dot(a, b, preferred_element_type=F32)
        logits = jnp.where(lane < n_exp, d(xh, wh) + d(xh, wm) + d(xm, wh), NEG)
        v1 = jnp.max(logits, axis=1, keepdims=True)
        i1 = jnp.min(jnp.where(logits == v1, lane, LANES), axis=1, keepdims=True)
        rest = jnp.where(lane == i1, NEG, logits)
        v2 = jnp.max(rest, axis=1, keepdims=True)
        i2 = jnp.min(jnp.where(rest == v2, lane, LANES), axis=1, keepdims=True)
        ex = jnp.exp(v2 - v1)
        comb_ref[...] = jnp.where(lane == i1, 1.0 / (1.0 + ex), jnp.where(lane == i2, ex / (1.0 + ex), 0.0))
        acc_ref[...] = jnp.zeros_like(acc_ref)

    xn = xn_ref[...]
    h = _silu(jnp.dot(xn, wg_ref[0], preferred_element_type=F32)) * \
        jnp.dot(xn, wu_ref[0], preferred_element_type=F32)
    w = jnp.sum(jnp.where(lane == e, comb_ref[...], 0.0), axis=1, keepdims=True)
    acc_ref[...] += w * jnp.dot(h.astype(BF16), wd_ref[0], preferred_element_type=F32)

    @pl.when((e == n_exp - 1) & (f == pl.num_programs(2) - 1))
    def _():
        y_ref[...] = _rmsnorm(x_ref[...] + acc_ref[...], gf_ref[...])


def _moe_final(x2d, gain, w_router, wg, wu, wd, gain_final, name):
    M, D = x2d.shape
    n_exp, _, Fd = wg.shape
    tm = _tile(M, 512)
    nf = 2 if Fd % (2 * LANES) == 0 else 1
    fc = Fd // nf
    wr = jnp.zeros((D, LANES), F32).at[:, :n_exp].set(w_router)
    return pl.pallas_call(
        functools.partial(_moe_kernel, n_exp=n_exp), grid=(M // tm, n_exp, nf),
        in_specs=[pl.BlockSpec((tm, D), lambda i, e, f: (i, 0)), pl.BlockSpec((1, D), lambda i, e, f: (0, 0)),
                  pl.BlockSpec((D, LANES), lambda i, e, f: (0, 0)),
                  pl.BlockSpec((1, D, fc), lambda i, e, f: (e, 0, f)),
                  pl.BlockSpec((1, D, fc), lambda i, e, f: (e, 0, f)),
                  pl.BlockSpec((1, fc, D), lambda i, e, f: (e, f, 0)),
                  pl.BlockSpec((1, D), lambda i, e, f: (0, 0))],
        out_specs=pl.BlockSpec((tm, D), lambda i, e, f: (i, 0)),
        out_shape=jax.ShapeDtypeStruct((M, D), F32), name=name,
        scratch_shapes=[pltpu.VMEM((tm, D), BF16), pltpu.VMEM((tm, LANES), F32), pltpu.VMEM((tm, D), F32)],
        compiler_params=_cparams(("parallel", "arbitrary", "arbitrary")))(
            x2d, gain.reshape(1, D), wr, wg, wu, wd, gain_final.reshape(1, D))


def _dil_kernel(q_ref, k_ref, v_ref, o_ref, m_ref, l_ref, acc_ref, *, ta):
    S = q_ref.shape[1]
    lane = lax.broadcasted_iota(jnp.int32, (ta, LANES), 1)
    first = True
    for window, dil in DIL_PAIRS:
        span = window // dil
        L = S // dil
        n_tiles = L // ta
        kw = min(span + ta, L)
        is_first = first

        def body(it, carry, dil=dil, span=span, n_tiles=n_tiles, kw=kw, is_first=is_first):
            res = it // n_tiles
            a0 = (it % n_tiles) * ta
            ak0 = jnp.maximum(a0 + ta - kw, 0)
            qrows = pl.ds(res + dil * a0, ta, stride=dil) if dil > 1 else pl.ds(pl.multiple_of(a0, ta), ta)
            krows = pl.ds(res + dil * ak0, kw, stride=dil) if dil > 1 else pl.ds(pl.multiple_of(ak0, ta), kw)
            q = q_ref[0, qrows, :] * SCALE
            k = k_ref[0, krows, :].astype(BF16)
            v = v_ref[0, krows, :].astype(BF16)
            d = (a0 + lax.broadcasted_iota(jnp.int32, (ta, kw), 0)) - \
                (ak0 + lax.broadcasted_iota(jnp.int32, (ta, kw), 1))
            mask = (d >= 0) & (d <= span)
            ms, ls, accs = [], [], []
            for h in range(2):
                qh = jnp.where((lane // HEAD_DIM) == h, q, 0.0)
                s = jnp.where(mask, _mm_nt(qh, k), NEG)
                m = jnp.max(s, axis=1, keepdims=True)
                p = jnp.exp(s - m)
                ms.append(m)
                ls.append(jnp.sum(p, axis=1, keepdims=True))
                accs.append(jnp.dot(p.astype(BF16), v, preferred_element_type=F32))
            m2 = jnp.where(lane < HEAD_DIM, ms[0], ms[1])
            l2 = jnp.where(lane < HEAD_DIM, ls[0], ls[1])
            a2 = jnp.where(lane < HEAD_DIM, accs[0], accs[1])
            if not is_first:
                m_old = m_ref[qrows, :]
                m_new = jnp.maximum(m_old, m2)
                w_old = jnp.exp(m_old - m_new)
                w_new = jnp.exp(m2 - m_new)
                l2 = w_old * l_ref[qrows, :] + w_new * l2
                a2 = w_old * acc_ref[qrows, :] + w_new * a2
                m2 = m_new
            m_ref[qrows, :] = m2
            l_ref[qrows, :] = l2
            acc_ref[qrows, :] = a2
            return carry

        lax.fori_loop(0, dil * n_tiles, body, 0)
        first = False
    o_ref[0] = acc_ref[...] / l_ref[...]


def _dilated_prompt(q, k, v):
    B, S, W = q.shape
    ta = _tile(S // DIL_PAIRS[-1][1], 128)
    spec = pl.BlockSpec((1, S, LANES), lambda b, hp: (b, 0, hp))
    return pl.pallas_call(
        functools.partial(_dil_kernel, ta=ta), grid=(B, W // LANES),
        in_specs=[spec, spec, spec], out_specs=spec,
        out_shape=jax.ShapeDtypeStruct(q.shape, F32), name="dilated_prompt",
        scratch_shapes=[pltpu.VMEM((S, LANES), F32)] * 3,
        compiler_params=_cparams(("parallel", "parallel")))(q, k, v)


def _head_major(x, n_heads, pad_rows):
    DB, T, _ = x.shape
    y = x.reshape(DB, T, n_heads, HEAD_DIM).transpose(0, 2, 1, 3)
    return jnp.pad(y, ((0, 0), (0, 0), (0, pad_rows - T), (0, 0)))


def _new_cols(x, n_heads):
    DB, T, _ = x.shape
    y = x.reshape(DB, T, n_heads, HEAD_DIM).transpose(0, 2, 3, 1)
    return jnp.pad(y, ((0, 0), (0, 0), (0, 0), (0, NEW_PAD - T)))


def _row_of(col_vec_row):
    return jnp.transpose(jnp.broadcast_to(col_vec_row, (SUBLANES, LANES)))[0:SUBLANES, 0:1]


def _fox_dec_kernel(pt_ref, *refs, n_pg, n_q):
    kts, vts, lfs = refs[:n_pg], refs[n_pg:2 * n_pg], refs[2 * n_pg:3 * n_pg]
    q_ref, knt_ref, vnt_ref, lfn_ref, o_ref, base_ref, carry_ref, m_ref, l_ref, acc_ref = refs[3 * n_pg:]
    p = pl.program_id(1)
    nh = FOX_HEADS
    page = LANES
    lane = lax.broadcasted_iota(jnp.int32, (QPAD, LANES), 1)
    trow = lax.broadcasted_iota(jnp.int32, (QPAD, LANES), 0)
    r = lax.broadcasted_iota(jnp.int32, (page, page), 0)
    c = lax.broadcasted_iota(jnp.int32, (page, page), 1)
    stack = lambda xs: jnp.concatenate(xs, axis=0)

    @pl.when(p == 0)
    def _():
        _online_init(m_ref, l_ref, acc_ref)
        carry_ref[...] = jnp.zeros_like(carry_ref)
        cnew = _mm3_left(lfn_ref[0], r <= c)
        bases, ss = [], []
        for h in range(nh):
            bh = _row_of(cnew[h:h + 1, :])
            bases.append(bh)
            ss.append(_mm(q_ref[0, h] * SCALE, knt_ref[0, h]) + (bh - cnew[h:h + 1, :]))
        base_ref[...] = jnp.broadcast_to(stack(bases), base_ref.shape)
        live = (lane <= trow) & (lane < n_q)
        s = jnp.where(stack([live] * nh), stack(ss), NEG)
        alpha, pr = _online_step(s, m_ref, l_ref)
        pv = stack([_mm_nt(pr[h * QPAD:(h + 1) * QPAD], vnt_ref[0, h]) for h in range(nh)])
        acc_ref[...] = alpha * acc_ref[...] + pv

    lf_all = stack([lf[0] for lf in lfs])
    after_all = _mm3_left(lf_all, r > c)
    tot_all = after_all[:, 0:1] + lf_all[:, 0:1]
    carry = carry_ref[:, 0:1]
    decay = [None] * n_pg
    for j in reversed(range(n_pg)):
        decay[j] = carry + after_all[j * nh:(j + 1) * nh]
        carry = carry + tot_all[j * nh:(j + 1) * nh]
    carry_ref[...] = jnp.broadcast_to(carry, carry_ref.shape)
    ss = []
    for h in range(nh):
        qh = (q_ref[0, h] * SCALE).astype(BF16)
        sh = jnp.concatenate([jnp.dot(qh, kts[j][0, h].astype(BF16), preferred_element_type=F32)
                              + decay[j][h:h + 1, :] for j in range(n_pg)], axis=1)
        ss.append(sh)
    s = stack(ss) + base_ref[:, 0:1]
    alpha, pr = _online_step(s, m_ref, l_ref)
    pvs = []
    for h in range(nh):
        ph = pr[h * QPAD:(h + 1) * QPAD]
        pvs.append(sum(_mm_nt(ph[:, j * page:(j + 1) * page], vts[j][0, h]) for j in range(n_pg)))
    acc_ref[...] = alpha * acc_ref[...] + stack(pvs)

    @pl.when(p == pl.num_programs(1) - 1)
    def _():
        o_ref[0] = acc_ref[...] / l_ref[...]


def _fox_decode(q, k_new, v_new, lf_new, cache_k, cache_v, cache_lf, pt_flat):
    DB, T, W = q.shape
    n_pool, page, nh, _ = cache_k.shape
    assert page == LANES and T <= QPAD
    n_pages = pt_flat.shape[0] // DB
    n_pg = math.gcd(n_pages, PAGES_PER_STEP)
    n_steps = n_pages // n_pg
    ckt = jnp.transpose(cache_k, (0, 2, 3, 1))
    cvt = jnp.transpose(cache_v, (0, 2, 3, 1))
    clf = jnp.transpose(cache_lf, (0, 2, 1))
    qh = _head_major(q, nh, QPAD)
    knt, vnt = _new_cols(k_new, nh), _new_cols(v_new, nh)
    lfn = jnp.pad(jnp.transpose(lf_new, (0, 2, 1)), ((0, 0), (0, 0), (0, LANES - T)))
    page_idx = lambda j: (lambda b, p, pt: (pt[b * n_pages + (n_steps - 1 - p) * n_pg + j], 0, 0, 0))
    lf_idx = lambda j: (lambda b, p, pt: (pt[b * n_pages + (n_steps - 1 - p) * n_pg + j], 0, 0))
    per_db = lambda a: pl.BlockSpec((1,) + a.shape[1:], lambda b, p, pt: (b,) + (0,) * (a.ndim - 1))
    rows = nh * QPAD
    grid_spec = pltpu.PrefetchScalarGridSpec(
        num_scalar_prefetch=1, grid=(DB, n_steps),
        in_specs=([pl.BlockSpec((1, nh, HEAD_DIM, page), page_idx(j)) for j in range(n_pg)] * 2
                  + [pl.BlockSpec((1, nh, page), lf_idx(j)) for j in range(n_pg)]
                  + [per_db(qh), per_db(knt), per_db(vnt), per_db(lfn)]),
        out_specs=pl.BlockSpec((1, rows, HEAD_DIM), lambda b, p, pt: (b, 0, 0)),
        scratch_shapes=[pltpu.VMEM((rows, LANES), F32), pltpu.VMEM((nh, LANES), F32),
                        pltpu.VMEM((rows, 1), F32), pltpu.VMEM((rows, 1), F32), pltpu.VMEM((rows, HEAD_DIM), F32)])
    o = pl.pallas_call(
        functools.partial(_fox_dec_kernel, n_pg=n_pg, n_q=T), grid_spec=grid_spec,
        out_shape=jax.ShapeDtypeStruct((DB, rows, HEAD_DIM), F32), name="fox_decode",
        compiler_params=_cparams(("parallel", "arbitrary")))(
            pt_flat, *([ckt] * n_pg), *([cvt] * n_pg), *([clf] * n_pg), qh, knt, vnt, lfn)
    return o.reshape(DB, nh, QPAD, HEAD_DIM)[:, :, :T].transpose(0, 2, 1, 3).reshape(DB, T, W)


def _nsa_dec_a_kernel(q_ref, kct_ref, vct_ref, kwt_ref, vwt_ref, kwnt_ref, vwnt_ref, gt_ref, map_ref,
                      o_ref, bias_ref, *, n_q, ns, n_sel, past, n_keys):
    nr = NSA_GROUP * QPAD
    n_ch = kct_ref.shape[3]
    wb = kwt_ref.shape[3]
    t_row = lax.broadcasted_iota(jnp.int32, (nr, 1), 0) % QPAD
    imps = []
    for kv in range(NSA_KV_HEADS):
        q = (q_ref[0, kv] * SCALE).astype(BF16)
        cidx = lax.broadcasted_iota(jnp.int32, (nr, n_ch), 1)
        pc = _softmax_rows(_mm(q, kct_ref[0, kv]), cidx >= 1)
        oc = _mm_nt(pc, vct_ref[0, kv])
        pcat = jnp.concatenate([pc[j * QPAD:(j + 1) * QPAD] for j in range(NSA_GROUP)], axis=1)
        imps.append(jnp.dot(pcat.astype(BF16), map_ref[...], preferred_element_type=F32))
        sw = _mm(q, kwt_ref[0, kv])
        sn = _mm(q, kwnt_ref[0, kv])
        iw = lax.broadcasted_iota(jnp.int32, (nr, wb), 1)
        un = lax.broadcasted_iota(jnp.int32, (nr, NEW_PAD), 1)
        mw = (wb + t_row - iw <= NSA_WINDOW) & (past - wb + iw >= 0)
        mn = (un <= t_row) & (un < n_q)
        sw = jnp.where(mw, sw, NEG)
        sn = jnp.where(mn, sn, NEG)
        m = jnp.maximum(jnp.max(sw, axis=1, keepdims=True), jnp.max(sn, axis=1, keepdims=True))
        pw = jnp.where(mw, jnp.exp(sw - m), 0.0)
        pn = jnp.where(mn, jnp.exp(sn - m), 0.0)
        l = jnp.sum(pw, axis=1, keepdims=True) + jnp.sum(pn, axis=1, keepdims=True)
        ow = (_mm_nt(pw, vwt_ref[0, kv]) + _mm_nt(pn, vwnt_ref[0, kv])) / l
        gt = gt_ref[0, kv]
        o_ref[0, kv] = gt[:, 0:1] * oc + gt[:, 2:3] * ow

    imp = jnp.concatenate(imps, axis=0)
    t_sel = lax.broadcasted_iota(jnp.int32, (imp.shape[0], 1), 0) % QPAD
    sel = _select_blocks(imp, past + t_sel, ns, n_sel).astype(BF16)
    ch = 8 * LANES
    for c0 in range(0, n_keys, ch):
        w = min(ch, n_keys - c0)
        key = c0 + lax.broadcasted_iota(jnp.int32, (sel.shape[1], w), 1)
        onehot = (key // NSA_SLC_BLOCK == lax.broadcasted_iota(jnp.int32, (sel.shape[1], w), 0)).astype(BF16)
        chosen = jnp.dot(sel, onehot, preferred_element_type=F32) > 0.5
        kpos = c0 + lax.broadcasted_iota(jnp.int32, (sel.shape[0], w), 1)
        ok = chosen & (kpos <= past + t_sel) & (kpos < past + n_q)
        bias_ref[0, :, c0:c0 + w] = jnp.where(ok, 0.0, NEG)


def _nsa_dec_b_kernel(pt_ref, *refs, n_pg):
    kts, vts = refs[:n_pg], refs[n_pg:2 * n_pg]
    q_ref, knt_ref, vnt_ref, bias_ref, biasn_ref, o_ref, m_ref, l_ref, acc_ref = refs[2 * n_pg:]
    p = pl.program_id(1)
    nr = NSA_GROUP * QPAD
    page = LANES
    stack = lambda xs: jnp.concatenate(xs, axis=0)

    @pl.when(p == 0)
    def _():
        _online_init(m_ref, l_ref, acc_ref)

    def update(s_of, v_of):
        ss = []
        for kv in range(NSA_KV_HEADS):
            ss.append(s_of(kv, (q_ref[0, kv] * SCALE).astype(BF16)))
        alpha, pr = _online_step(stack(ss), m_ref, l_ref)
        acc_ref[...] = alpha * acc_ref[...] + stack([v_of(kv, pr[kv * nr:(kv + 1) * nr])
                                                     for kv in range(NSA_KV_HEADS)])

    def s_pages(kv, q):
        s = jnp.concatenate([jnp.dot(q, kts[j][0, kv].astype(BF16), preferred_element_type=F32)
                             for j in range(n_pg)], axis=1)
        return s + jnp.concatenate([bias_ref[0, kv * QPAD:(kv + 1) * QPAD]] * NSA_GROUP, axis=0)

    def v_pages(kv, pr):
        return sum(_mm_nt(pr[:, j * page:(j + 1) * page], vts[j][0, kv]) for j in range(n_pg))

    update(s_pages, v_pages)

    @pl.when(p == pl.num_programs(1) - 1)
    def _():
        update(lambda kv, q: _mm(q, knt_ref[0, kv])
               + jnp.concatenate([biasn_ref[0, kv * QPAD:(kv + 1) * QPAD]] * NSA_GROUP, axis=0),
               lambda kv, pr: _mm_nt(pr, vnt_ref[0, kv]))
        o_ref[0] = acc_ref[...] / l_ref[...]


def _nsa_decode(qb, kct, vct, cache_ks, cache_vs, ks_new, vs_new, swa_k, swa_v, kw_new, vw_new,
                gates, pt_flat, past):
    DB, T, W = qb.shape
    n_pool, page = cache_ks.shape[:2]
    assert page == LANES and past % LANES == 0 and T <= QPAD
    n_pages = pt_flat.shape[0] // DB
    n_ch = kct.shape[3]
    ns = -(-(past + T) // NSA_SLC_BLOCK)
    ns_pad = -(-ns // LANES) * LANES
    n_sel = min(NSA_TOPN, ns)
    nr = NSA_GROUP * QPAD
    n_keys = past + NEW_PAD
    q5 = _head_major(qb, NSA_HEADS, QPAD).reshape(DB, NSA_KV_HEADS, nr, HEAD_DIM)
    g5 = _head_major(jnp.pad(gates.reshape(DB, T, NSA_HEADS, 3), ((0, 0),) * 3 + ((0, HEAD_DIM - 3),))
                     .reshape(DB, T, NSA_HEADS * HEAD_DIM), NSA_HEADS, QPAD)[..., :3]
    g5 = g5.reshape(DB, NSA_KV_HEADS, nr, 3)
    smap = jnp.asarray(np.tile(_slc_map_rows(n_ch, ns_pad), (NSA_GROUP, 1)), dtype=BF16)
    kwt = jnp.transpose(swa_k, (0, 2, 3, 1))
    vwt = jnp.transpose(swa_v, (0, 2, 3, 1))
    kwnt, vwnt = _new_cols(kw_new, NSA_KV_HEADS), _new_cols(vw_new, NSA_KV_HEADS)
    per = lambda a: pl.BlockSpec((1,) + a.shape[1:], lambda b: (b,) + (0,) * (a.ndim - 1))
    part, bias = pl.pallas_call(
        functools.partial(_nsa_dec_a_kernel, n_q=T, ns=ns, n_sel=n_sel, past=past, n_keys=n_keys), grid=(DB,),
        in_specs=[per(q5), per(kct), per(vct), per(kwt), per(vwt), per(kwnt), per(vwnt), per(g5),
                  pl.BlockSpec(smap.shape, lambda b: (0, 0))],
        out_specs=[pl.BlockSpec((1, NSA_KV_HEADS, nr, HEAD_DIM), lambda b: (b, 0, 0, 0)),
                   pl.BlockSpec((1, NSA_KV_HEADS * QPAD, n_keys), lambda b: (b, 0, 0))],
        out_shape=[jax.ShapeDtypeStruct((DB, NSA_KV_HEADS, nr, HEAD_DIM), F32),
                   jax.ShapeDtypeStruct((DB, NSA_KV_HEADS * QPAD, n_keys), F32)], name="nsa_decode_a",
        compiler_params=_cparams(("parallel",)))(q5, kct, vct, kwt, vwt, kwnt, vwnt, g5, smap)

    n_pg = math.gcd(n_pages, PAGES_PER_STEP)
    ckt = jnp.transpose(cache_ks, (0, 2, 3, 1))
    cvt = jnp.transpose(cache_vs, (0, 2, 3, 1))
    ksnt, vsnt = _new_cols(ks_new, NSA_KV_HEADS), _new_cols(vs_new, NSA_KV_HEADS)
    page_idx = lambda j: (lambda b, p, pt: (pt[b * n_pages + p * n_pg + j], 0, 0, 0))
    per_db = lambda a: pl.BlockSpec((1,) + a.shape[1:], lambda b, p, pt: (b,) + (0,) * (a.ndim - 1))
    rows = NSA_KV_HEADS * nr
    grid_spec = pltpu.PrefetchScalarGridSpec(
        num_scalar_prefetch=1, grid=(DB, n_pages // n_pg),
        in_specs=([pl.BlockSpec((1, NSA_KV_HEADS, HEAD_DIM, page), page_idx(j)) for j in range(n_pg)] * 2
                  + [per_db(q5), per_db(ksnt), per_db(vsnt),
                     pl.BlockSpec((1, NSA_KV_HEADS * QPAD, n_pg * page), lambda b, p, pt: (b, 0, p)),
                     pl.BlockSpec((1, NSA_KV_HEADS * QPAD, NEW_PAD), lambda b, p, pt: (b, 0, past // NEW_PAD))]),
        out_specs=pl.BlockSpec((1, rows, HEAD_DIM), lambda b, p, pt: (b, 0, 0)),
        scratch_shapes=[pltpu.VMEM((rows, 1), F32), pltpu.VMEM((rows, 1), F32), pltpu.VMEM((rows, HEAD_DIM), F32)])
    osl = pl.pallas_call(
        functools.partial(_nsa_dec_b_kernel, n_pg=n_pg), grid_spec=grid_spec,
        out_shape=jax.ShapeDtypeStruct((DB, rows, HEAD_DIM), F32), name="nsa_decode_b",
        compiler_params=_cparams(("parallel", "arbitrary")))(
            pt_flat, *([ckt] * n_pg), *([cvt] * n_pg), q5, ksnt, vsnt, bias, bias)

    o = part + g5[..., 1:2] * osl.reshape(DB, NSA_KV_HEADS, nr, HEAD_DIM)
    return o.reshape(DB, NSA_HEADS, QPAD, HEAD_DIM)[:, :, :T].transpose(0, 2, 1, 3).reshape(DB, T, W)


def _dil_dec_kernel(q_ref, kt_ref, vt_ref, knt_ref, vnt_ref, o_ref, *, n_q, wc):
    hb = kt_ref.shape[1]
    t = lax.broadcasted_iota(jnp.int32, (QPAD, 1), 0)

    def log_mult(d, ok):
        w = jnp.zeros(d.shape, F32)
        for window, dil in DIL_PAIRS:
            w = w + ((d >= 0) & (d <= window) & (d % dil == 0)).astype(F32)
        return jnp.where(ok, w, 0.0)

    wk = log_mult(wc + t - lax.broadcasted_iota(jnp.int32, (QPAD, wc), 1), t < n_q)
    un = lax.broadcasted_iota(jnp.int32, (QPAD, NEW_PAD), 1)
    wn = log_mult(t - un, (t < n_q) & (un < n_q))
    for h in range(hb):
        q = (q_ref[0, h] * SCALE).astype(BF16)
        sk = jnp.where(wk > 0.0, _mm(q, kt_ref[0, h]), NEG)
        sn = jnp.where(wn > 0.0, _mm(q, knt_ref[0, h]), NEG)
        m = jnp.maximum(jnp.max(sk, axis=1, keepdims=True), jnp.max(sn, axis=1, keepdims=True))
        pk = wk * jnp.exp(sk - m)
        pn = wn * jnp.exp(sn - m)
        l = jnp.sum(pk, axis=1, keepdims=True) + jnp.sum(pn, axis=1, keepdims=True)
        o_ref[0, h] = (_mm_nt(pk, vt_ref[0, h]) + _mm_nt(pn, vnt_ref[0, h])) / jnp.where(l > 0.0, l, 1.0)


def _dilated_decode(q, k_new, v_new, cache_k, cache_v):
    DB, T, W = q.shape
    wc, nh = cache_k.shape[1], cache_k.shape[2]
    ckt = jnp.transpose(cache_k, (0, 2, 3, 1))
    cvt = jnp.transpose(cache_v, (0, 2, 3, 1))
    qh = _head_major(q, nh, QPAD)
    knt, vnt = _new_cols(k_new, nh), _new_cols(v_new, nh)
    hb = math.gcd(nh, 4)
    spec = lambda a: pl.BlockSpec((1, hb) + a.shape[2:], lambda b, j: (b, j, 0, 0))
    o = pl.pallas_call(
        functools.partial(_dil_dec_kernel, n_q=T, wc=wc), grid=(DB, nh // hb),
        in_specs=[spec(qh), spec(ckt), spec(cvt), spec(knt), spec(vnt)],
        out_specs=pl.BlockSpec((1, hb, QPAD, HEAD_DIM), lambda b, j: (b, j, 0, 0)),
        out_shape=jax.ShapeDtypeStruct((DB, nh, QPAD, HEAD_DIM), F32), name="dilated_decode",
        compiler_params=_cparams(("parallel", "parallel")))(qh, ckt, cvt, knt, vnt)
    return o[:, :, :T].transpose(0, 2, 1, 3).reshape(DB, T, W)


def kernel(x_prompt, x_sample, cache_a_k, cache_a_v, cache_a_logf, cache_b_cmp_k, cache_b_cmp_v, cache_b_slc_k, cache_b_slc_v, cache_b_swa_k, cache_b_swa_v, cache_c_k, cache_c_v, page_table, norm_mix0, w_in0, fox_bf, nsa_pe_k, nsa_w1_k, nsa_w2_k, nsa_pe_v, nsa_w1_v, nsa_w2_v, w_out0, norm_ffn0, ffn_w_gate, ffn_w_up, ffn_w_down, norm_mix1, w_in1, w_out1, norm_ffn1, moe_router, moe_w_gate, moe_w_up, moe_w_down, norm_final):
    B, S, D = x_prompt.shape
    DB, T, _ = x_sample.shape
    n_pages = page_table.shape[1]
    past = n_pages * cache_a_k.shape[1]
    pt_flat = page_table.reshape(-1).astype(jnp.int32)
    fw = FOX_HEADS * HEAD_DIM
    nw = NSA_HEADS * HEAD_DIM
    kvw = NSA_KV_HEADS * HEAD_DIM

    cuts = np.cumsum([0, fw, fw, fw, FOX_HEADS, nw] + [kvw] * 6 + [3 * NSA_HEADS])
    col = lambda i: w_in0[:, cuts[i]:cuts[i + 1]]
    qa_w, ka_w, va_w, fa_w, qb_w, kc_w, vc_w, ks_w, vs_w, kw_w, vw_w, gb_w = [col(i) for i in range(12)]

    def pack(ws, ropes):
        widths = [w.shape[1] for w in ws]
        starts = np.concatenate([[0], np.cumsum(widths)[:-1]])
        return (jnp.concatenate(ws, axis=1).astype(BF16),
                [(int(s), int(w), r) for s, w, r in zip(starts, widths, ropes)])

    w0r, segs0r = pack([qa_w, qb_w, kc_w, vc_w], [False, True, True, False])
    w0c, segs0c = pack([ka_w, va_w, kc_w, ks_w, kw_w, vc_w, vs_w, vw_w],
                       [False, False, True, True, True, False, False, False])
    w0c = w0c.T
    w0s, segs0s = pack([qa_w, ka_w, va_w, qb_w, kc_w, ks_w, kw_w, vc_w, vs_w, vw_w],
                       [False, False, False, True, True, True, True, False, False, False])
    wgate = gb_w.astype(BF16)
    wft = jnp.zeros((16, D), F32).at[:FOX_HEADS].set(fa_w.T).astype(BF16)
    logf_args = (wft, fox_bf.reshape(FOX_HEADS, 1).astype(F32))
    dw = w_in1.shape[1] // 3
    w1 = w_in1.astype(BF16)
    segs1 = [(0, dw, True), (dw, dw, True), (2 * dw, dw, False)]
    w1c = w1[:, dw:].T
    segs1c = [(0, dw, True), (dw, dw, False)]
    w_out0_b, w_out1_b = w_out0.astype(BF16), w_out1.astype(BF16)
    ffn_g, ffn_u, ffn_d = ffn_w_gate.astype(BF16), ffn_w_up.astype(BF16), ffn_w_down.astype(BF16)
    moe_g, moe_u, moe_d = moe_w_gate.astype(BF16), moe_w_up.astype(BF16), moe_w_down.astype(BF16)
    cmp_k_w = _cmp_weights(nsa_pe_k, nsa_w1_k, nsa_w2_k)
    cmp_v_w = _cmp_weights(nsa_pe_v, nsa_w1_v, nsa_w2_v)

    tab_p = _rope_tables(jnp.arange(S))
    tab_s = _rope_tables(past + jnp.arange(DB * T) % T)
    tm_p = _tile(S, 512)
    npb = S // tm_p
    win_b = min(NSA_WINDOW, S)
    win_c = min(DIL_WINDOW_MAX, S)

    xp = x_prompt.reshape(B * S, D)
    (qa, qb, kc, vc, kat, vat, kct, kst, kwt, vct, vst, vwt, gates_p, lft_p) = _project(
        xp, norm_mix0, tab_p, npb, tm_p, w=w0r, row_segs=segs0r, wt=w0c, col_segs=segs0c,
        w_gate=wgate, logf=logf_args, name="proj0_prompt")
    r3 = lambda a: a.reshape(B, S, a.shape[-1])
    heads = lambda a, h: a.reshape(B, h, HEAD_DIM, a.shape[-1])
    o_a = _fox_prompt(r3(qa), heads(kat, FOX_HEADS), heads(vat, FOX_HEADS), _cumsum(lft_p))
    kcmp_p = _compress(r3(kc), cmp_k_w, "compress_k_prompt")
    vcmp_p = _compress(r3(vc), cmp_v_w, "compress_v_prompt")
    kv2 = lambda a: heads(a, NSA_KV_HEADS)
    o_b = _nsa_prompt(r3(qb), kv2(kcmp_p), kv2(vcmp_p), kv2(kst), kv2(vst), kv2(kwt), kv2(vwt), gates_p)
    hp = _outproj(xp, [o_a.reshape(B * S, fw), o_b.reshape(B * S, nw)], w_out0_b, "outproj0_prompt")
    hp = _ffn(hp, norm_ffn0, ffn_g, ffn_u, ffn_d, "ffn_prompt")

    xs = x_sample.reshape(DB * T, D)
    (qa_s, ka_s, va_s, qb_s, kc_s, ks_s, kw_s, vc_s, vs_s, vw_s, gates_s, lft_s) = _project(
        xs, norm_mix0, tab_s, 1, DB * T, w=w0s, row_segs=segs0s, w_gate=wgate, logf=logf_args,
        name="proj0_sample")
    s3 = lambda a: a.reshape(DB, T, a.shape[-1])
    lf_s = jnp.transpose(lft_s[0].reshape(FOX_HEADS, DB, T), (1, 2, 0))
    o_a_s = _fox_decode(s3(qa_s), s3(ka_s), s3(va_s), lf_s, cache_a_k, cache_a_v, cache_a_logf, pt_flat)
    kcmp_s = _compress_paged(cache_b_cmp_k, pt_flat, DB, cmp_k_w, "compress_k_paged")
    vcmp_s = _compress_paged(cache_b_cmp_v, pt_flat, DB, cmp_v_w, "compress_v_paged")
    kvs = lambda a: a.reshape(DB, NSA_KV_HEADS, HEAD_DIM, a.shape[-1])
    o_b_s = _nsa_decode(s3(qb_s), kvs(kcmp_s), kvs(vcmp_s), cache_b_slc_k, cache_b_slc_v, s3(ks_s), s3(vs_s),
                        cache_b_swa_k, cache_b_swa_v, s3(kw_s), s3(vw_s), s3(gates_s), pt_flat, past)
    hs = _outproj(xs, [o_a_s.reshape(DB * T, fw), o_b_s.reshape(DB * T, nw)], w_out0_b, "outproj0_sample")
    hs = _ffn(hs, norm_ffn0, ffn_g, ffn_u, ffn_d, "ffn_sample")

    first_c = (S - win_c) // tm_p
    q1, k1, v1, k1t, v1t = _project(hp, norm_mix1, tab_p, npb, tm_p, w=w1, row_segs=segs1, wt=w1c,
                                    col_segs=segs1c, col_from=(npb, first_c), name="proj1_prompt")
    o1 = _dilated_prompt(q1.reshape(B, S, dw), k1.reshape(B, S, dw), v1.reshape(B, S, dw))
    hp = _outproj(hp, [o1.reshape(B * S, dw)], w_out1_b, "outproj1_prompt")
    y_prompt = _moe_final(hp, norm_ffn1, moe_router, moe_g, moe_u, moe_d, norm_final, "moe_prompt").reshape(B, S, D)

    q1s, k1s, v1s = _project(hs, norm_mix1, tab_s, 1, DB * T, w=w1, row_segs=segs1, name="proj1_sample")
    o1s = _dilated_decode(s3(q1s), s3(k1s), s3(v1s), cache_c_k, cache_c_v)
    hs = _outproj(hs, [o1s.reshape(DB * T, dw)], w_out1_b, "outproj1_sample")
    y_sample = _moe_final(hs, norm_ffn1, moe_router, moe_g, moe_u, moe_d, norm_final, "moe_sample").reshape(DB, T, D)

    def state(a, h, last=None):
        a = a.reshape(a.shape[0], h, HEAD_DIM, a.shape[-1])
        if last is not None:
            a = a[..., a.shape[-1] - last:]
        return jnp.transpose(a, (0, 3, 1, 2))

    h4 = lambda a, h: a.reshape(DB, T, h, HEAD_DIM)
    nh1 = dw // HEAD_DIM
    return (y_prompt, y_sample,
            state(kat, FOX_HEADS), state(vat, FOX_HEADS), jnp.transpose(lft_p, (0, 2, 1)),
            state(kct, NSA_KV_HEADS), state(vct, NSA_KV_HEADS), state(kst, NSA_KV_HEADS), state(vst, NSA_KV_HEADS),
            state(kwt, NSA_KV_HEADS, win_b), state(vwt, NSA_KV_HEADS, win_b),
            state(k1t, nh1, win_c), state(v1t, nh1, win_c),
            h4(ka_s, FOX_HEADS), h4(va_s, FOX_HEADS), lf_s,
            h4(kc_s, NSA_KV_HEADS), h4(vc_s, NSA_KV_HEADS), h4(ks_s, NSA_KV_HEADS),
            h4(vs_s, NSA_KV_HEADS), h4(kw_s, NSA_KV_HEADS), h4(vw_s, NSA_KV_HEADS),
            h4(k1s, nh1), h4(v1s, nh1))
```

```python
import functools
import math

import numpy as np
import jax
import jax.numpy as jnp
from jax import lax
from jax.experimental import pallas as pl
from jax.experimental.pallas import tpu as pltpu

F32 = jnp.float32
BF16 = jnp.bfloat16

HEAD_DIM = 64
HALF = HEAD_DIM // 2
LANES = 128
SUBLANES = 8
ROPE_THETA = 10000.0
RMS_EPS = 1e-6
NEG = -1e30
MASK_BIG = 30000.0
SCALE = HEAD_DIM ** -0.5

FOX_HEADS = 8
NSA_HEADS = 8
NSA_KV_HEADS = 2
NSA_GROUP = NSA_HEADS // NSA_KV_HEADS
NSA_CMP_LEN = 32
NSA_CMP_STRIDE = 16
NSA_SLC_BLOCK = 64
NSA_TOPN = 16
NSA_WINDOW = 512
NSA_FORCE_BONUS = 1e3
DIL_PAIRS = ((128, 1), (512, 4), (2048, 16))
DIL_WINDOW_MAX = 2048
TOP_K = 2
QPAD = SUBLANES
NEW_PAD = LANES
PAGES_PER_STEP = 8

VMEM_LIMIT = 56 * 1024 * 1024


def _tile(n, pref):
    return pref if n % pref == 0 else n


def _cparams(sem):
    return pltpu.CompilerParams(dimension_semantics=sem, vmem_limit_bytes=VMEM_LIMIT)


def _mm(a, b):
    return jnp.dot(a.astype(BF16), b.astype(BF16), preferred_element_type=F32)


def _mm_nt(a, b):
    return lax.dot_general(a.astype(BF16), b.astype(BF16), (((1,), (1,)), ((), ())),
                           preferred_element_type=F32)


def _split3(x):
    hi = x.astype(BF16)
    r = x - hi.astype(F32)
    mid = r.astype(BF16)
    lo = (r - mid.astype(F32)).astype(BF16)
    return hi, mid, lo


def _mm3_left(x, exact_rhs):
    b = exact_rhs.astype(BF16)
    hi, mid, lo = _split3(x)
    d = lambda p: jnp.dot(p, b, preferred_element_type=F32)
    return d(hi) + d(mid) + d(lo)


def _sigmoid(z):
    return 1.0 / (1.0 + jnp.exp(-z))


def _silu(z):
    return z * _sigmoid(z)


def _log_sigmoid(z):
    return jnp.minimum(z, 0.0) - jnp.log1p(jnp.exp(-jnp.abs(z)))


def _rmsnorm(x, g):
    return x * lax.rsqrt(jnp.mean(x * x, axis=-1, keepdims=True) + RMS_EPS) * g


def _rope_rows(y, cos, sin_signed):
    n = y.shape[1]
    lane = lax.broadcasted_iota(jnp.int32, y.shape, 1)
    first = (lane % HEAD_DIM) < HALF
    rot = jnp.where(first, pltpu.roll(y, n - HALF, 1), pltpu.roll(y, HALF, 1))
    reps = n // LANES
    if reps > 1:
        cos = jnp.concatenate([cos] * reps, axis=1)
        sin_signed = jnp.concatenate([sin_signed] * reps, axis=1)
    return y * cos + rot * sin_signed


def _rope_cols(yt, cos_t, sin_t):
    out = []
    for h in range(yt.shape[0] // HEAD_DIM):
        a = yt[h * HEAD_DIM:h * HEAD_DIM + HALF]
        b = yt[h * HEAD_DIM + HALF:(h + 1) * HEAD_DIM]
        out += [a * cos_t - b * sin_t, b * cos_t + a * sin_t]
    return jnp.concatenate(out, axis=0)


def _rope_tables(pos):
    inv = jnp.exp(-math.log(ROPE_THETA) * jnp.arange(HALF, dtype=F32) / HALF)
    ang = pos.astype(F32)[:, None] * inv[None, :]
    cos, sin = jnp.cos(ang), jnp.sin(ang)
    return (jnp.concatenate([cos, cos, cos, cos], axis=1),
            jnp.concatenate([-sin, sin, -sin, sin], axis=1), cos.T, sin.T)


def _softmax_rows(s, mask):
    sm = jnp.where(mask, s, NEG)
    m = jnp.max(sm, axis=1, keepdims=True)
    p = jnp.where(mask, jnp.exp(sm - m), 0.0)
    l = jnp.sum(p, axis=1, keepdims=True)
    return p / jnp.where(l > 0.0, l, 1.0)


def _online_step(s, m_ref, l_ref):
    m_prev = m_ref[...]
    m_new = jnp.maximum(m_prev, jnp.max(s, axis=1, keepdims=True))
    alpha = jnp.exp(m_prev - m_new)
    p = jnp.exp(s - m_new)
    l_ref[...] = alpha * l_ref[...] + jnp.sum(p, axis=1, keepdims=True)
    m_ref[...] = m_new
    return alpha, p


def _online_init(m_ref, l_ref, acc_ref):
    m_ref[...] = jnp.full(m_ref.shape, NEG, F32)
    l_ref[...] = jnp.zeros_like(l_ref)
    acc_ref[...] = jnp.zeros_like(acc_ref)


def _proj_kernel(*refs, row_segs, col_segs, n_gate, with_logf, col_from):
    it = iter(refs)
    x_ref, g_ref, cos_ref, sin_ref, cost_ref, sint_ref = [next(it) for _ in range(6)]
    w_ref = next(it) if row_segs else None
    wt_ref = next(it) if col_segs else None
    wg_ref = next(it) if n_gate else None
    if with_logf:
        wft_ref, bcol_ref, wfr_ref, brow_ref = next(it), next(it), next(it), next(it)
    outs = list(it)
    xn = _rmsnorm(x_ref[...], g_ref[...]).astype(BF16)
    k = 0
    for c0, width, rope, emits in row_segs:
        y = jnp.dot(xn, w_ref[:, c0:c0 + width], preferred_element_type=F32)
        if rope:
            y = _rope_rows(y, cos_ref[...], sin_ref[...])
        for dtype, scale in emits:
            outs[k][...] = (y if scale == 1.0 else y * scale).astype(dtype)
            k += 1

    def cols():
        kk = k
        for r0, height, rope, emits in col_segs:
            yt = _mm_nt(wt_ref[r0:r0 + height, :], xn)
            if rope == "sigmoid":
                yt = _sigmoid(yt)
            elif rope:
                yt = _rope_cols(yt, cost_ref[...], sint_ref[...])
            for dtype, scale in emits:
                outs[kk][0] = (yt if scale == 1.0 else yt * scale).astype(dtype)
                kk += 1

    if col_segs:
        if col_from:
            pl.when(pl.program_id(0) % col_from[0] >= col_from[1])(cols)
        else:
            cols()
        k += sum(len(e) for _, _, _, e in col_segs)
    if n_gate:
        outs[k][...] = _sigmoid(jnp.dot(xn, wg_ref[...], preferred_element_type=F32))
        k += 1
    if with_logf:
        yt = _mm_nt(wft_ref[...], xn)
        outs[k][0] = _log_sigmoid(yt[0:FOX_HEADS] + bcol_ref[...])
        outs[k + 1][...] = _log_sigmoid(jnp.dot(xn, wfr_ref[...], preferred_element_type=F32) + brow_ref[...])


def _project(x2d, gain, tables, n_pos_blocks, tm, w=None, row_segs=(), wt=None, col_segs=(),
             w_gate=None, logf=None, col_from=None, name="proj"):
    M, D = x2d.shape
    nt = M // tm
    n_seq = nt // n_pos_blocks
    cos_t, sin_t, cos_c, sin_c = tables
    pos_map = lambda i: (i % n_pos_blocks, 0)
    posc_map = lambda i: (0, i % n_pos_blocks)
    const = lambda a: pl.BlockSpec(a.shape, lambda i: (0,) * a.ndim)
    in_specs = [pl.BlockSpec((tm, D), lambda i: (i, 0)), const(gain.reshape(1, D)),
                pl.BlockSpec((tm, LANES), pos_map), pl.BlockSpec((tm, LANES), pos_map),
                pl.BlockSpec((HALF, tm), posc_map), pl.BlockSpec((HALF, tm), posc_map)]
    args = [x2d, gain.reshape(1, D), cos_t, sin_t, cos_c, sin_c]
    for a in (w, wt, w_gate):
        if a is not None:
            in_specs.append(const(a))
            args.append(a)
    out_shape, out_specs = [], []
    for _, wd, _, emits in row_segs:
        for dtype, _ in emits:
            out_shape.append(jax.ShapeDtypeStruct((M, wd), dtype))
            out_specs.append(pl.BlockSpec((tm, wd), lambda i: (i, 0)))
    first = col_from[1] if col_from else 0
    n_cb = n_pos_blocks - first
    col_map = lambda i: (i // n_pos_blocks, 0, jnp.maximum(i % n_pos_blocks - first, 0))
    for _, ht, _, emits in col_segs:
        for dtype, _ in emits:
            out_shape.append(jax.ShapeDtypeStruct((n_seq, ht, n_cb * tm), dtype))
            out_specs.append(pl.BlockSpec((1, ht, tm), col_map))
    n_gate = 0
    if w_gate is not None:
        n_gate = w_gate.shape[1]
        out_shape.append(jax.ShapeDtypeStruct((M, n_gate), F32))
        out_specs.append(pl.BlockSpec((tm, n_gate), lambda i: (i, 0)))
    if logf is not None:
        in_specs += [const(a) for a in logf]
        args += list(logf)
        out_shape.append(jax.ShapeDtypeStruct((n_seq, FOX_HEADS, n_pos_blocks * tm), F32))
        out_specs.append(pl.BlockSpec((1, FOX_HEADS, tm), lambda i: (i // n_pos_blocks, 0, i % n_pos_blocks)))
        out_shape.append(jax.ShapeDtypeStruct((M, LANES), F32))
        out_specs.append(pl.BlockSpec((tm, LANES), lambda i: (i, 0)))
    return pl.pallas_call(
        functools.partial(_proj_kernel, row_segs=tuple(row_segs), col_segs=tuple(col_segs), n_gate=n_gate,
                          with_logf=logf is not None, col_from=col_from),
        grid=(nt,), in_specs=in_specs, out_specs=out_specs, out_shape=out_shape, name=name,
        compiler_params=_cparams(("arbitrary",)))(*args)


def _fox_prep_kernel(k_ref, lf_ref, kaug_ref, c0_ref, carry_ref):
    j = pl.program_id(1)
    tc = k_ref.shape[1]

    @pl.when(j == 0)
    def _():
        carry_ref[...] = jnp.zeros_like(carry_ref)
        c0_ref[...] = jnp.zeros_like(c0_ref)

    lane = lax.broadcasted_iota(jnp.int32, (FOX_HEADS, LANES), 1)
    start = jnp.transpose(carry_ref[...])[0:FOX_HEADS, 0:1]
    c0_ref[0] = jnp.where(lane == j, start, c0_ref[0])

    r = lax.broadcasted_iota(jnp.int32, (tc, tc), 0)
    c = lax.broadcasted_iota(jnp.int32, (tc, tc), 1)
    tri = (c <= r).astype(BF16)
    hi, mid, lo = _split3(lf_ref[0])
    d = lambda p: jnp.dot(tri, p, preferred_element_type=F32)
    local = d(hi) + d(mid) + d(lo)
    carry_ref[...] = carry_ref[...] + jnp.broadcast_to(local[tc - 1:tc, :], carry_ref.shape)
    parts = jnp.concatenate(_split3(-local), axis=1)
    k = k_ref[0]
    kr = lax.broadcasted_iota(jnp.int32, (k.shape[1], LANES), 0)
    kc = lax.broadcasted_iota(jnp.int32, (k.shape[1], LANES), 1)
    pr = lax.broadcasted_iota(jnp.int32, (3 * LANES, LANES), 0)
    pc = lax.broadcasted_iota(jnp.int32, (3 * LANES, LANES), 1)
    for h in range(FOX_HEADS):
        place_k = ((kr == HEAD_DIM * h + kc) & (kc < HEAD_DIM)).astype(BF16)
        place_c = ((pr % LANES == h) & (pc == HEAD_DIM + pr // LANES)).astype(BF16)
        kaug_ref[0, h] = (jnp.dot(k, place_k, preferred_element_type=F32)
                          + jnp.dot(parts, place_c, preferred_element_type=F32)).astype(BF16)


def _fox_prep(k_b, lf_rows, tc):
    B, S, W = k_b.shape
    assert S // tc <= LANES
    return pl.pallas_call(
        _fox_prep_kernel, grid=(B, S // tc),
        in_specs=[pl.BlockSpec((1, tc, W), lambda b, j: (b, j, 0)),
                  pl.BlockSpec((1, tc, LANES), lambda b, j: (b, j, 0))],
        out_specs=[pl.BlockSpec((1, FOX_HEADS, tc, LANES), lambda b, j: (b, 0, j, 0)),
                   pl.BlockSpec((1, FOX_HEADS, LANES), lambda b, j: (b, 0, 0))],
        out_shape=[jax.ShapeDtypeStruct((B, FOX_HEADS, S, LANES), BF16),
                   jax.ShapeDtypeStruct((B, FOX_HEADS, LANES), F32)], name="fox_prep",
        scratch_shapes=[pltpu.VMEM((SUBLANES, LANES), F32)],
        compiler_params=_cparams(("parallel", "arbitrary")))(k_b, lf_rows)


def _fox_kernel(qi_ref, ki_ref, qt_ref, kaug_ref, vt_ref, c0_ref, o_ref, qa_ref, m_ref, l_ref, acc_ref):
    hp, step = pl.program_id(1), pl.program_id(2)
    qi, ki = qi_ref[step], ki_ref[step]
    tq, tk = qt_ref.shape[2], kaug_ref.shape[2]
    lane1 = lax.broadcasted_iota(jnp.int32, (1, LANES), 1)

    @pl.when(ki == 0)
    def _():
        row = lax.broadcasted_iota(jnp.int32, (HEAD_DIM, tq), 0)
        ones = jnp.where(row < 3, 1.0, 0.0).astype(BF16)
        for h in range(2):
            qa_ref[h] = jnp.concatenate([qt_ref[0, h * HEAD_DIM:(h + 1) * HEAD_DIM, :], ones], axis=0)
        _online_init(m_ref, l_ref, acc_ref)

    def tile(diagonal):
        if diagonal:
            live = lax.broadcasted_iota(jnp.int32, (tk, tq), 0) <= lax.broadcasted_iota(jnp.int32, (tk, tq), 1)
        for h in range(2):
            c0 = c0_ref[0, pl.ds(2 * hp + h, 1), :]
            delta = jnp.sum(jnp.where(lane1 == qi, c0, 0.0) - jnp.where(lane1 == ki, c0, 0.0),
                            axis=1, keepdims=True)
            s = jnp.dot(kaug_ref[0, h], qa_ref[h], preferred_element_type=F32)
            if diagonal:
                s = jnp.where(live, s, NEG)
            m_prev = m_ref[h]
            m_new = jnp.maximum(m_prev, jnp.max(s, axis=0, keepdims=True) + delta)
            p = jnp.exp(s - (m_new - delta))
            alpha = jnp.exp(m_prev - m_new)
            l_ref[h] = alpha * l_ref[h] + jnp.sum(p, axis=0, keepdims=True)
            acc_ref[h] = alpha * acc_ref[h] + jnp.dot(vt_ref[0, h * HEAD_DIM:(h + 1) * HEAD_DIM, :],
                                                      p.astype(BF16), preferred_element_type=F32)
            m_ref[h] = m_new

    pl.when(ki < qi)(lambda: tile(False))

    @pl.when(ki == qi)
    def _():
        tile(True)
        for h in range(2):
            o_ref[0, h * HEAD_DIM:(h + 1) * HEAD_DIM, :] = (acc_ref[h] / l_ref[h]).astype(BF16)


def _fox_prompt(qt_b, kaug, vt_b, c0, tq):
    B, W, S = qt_b.shape
    nq = S // tq
    pairs = [(q, k) for q in range(nq) for k in range(q + 1)]
    qi_tab = jnp.asarray([p[0] for p in pairs], jnp.int32)
    ki_tab = jnp.asarray([p[1] for p in pairs], jnp.int32)
    grid_spec = pltpu.PrefetchScalarGridSpec(
        num_scalar_prefetch=2, grid=(B, W // LANES, len(pairs)),
        in_specs=[pl.BlockSpec((1, LANES, tq), lambda b, hp, s, qi, ki: (b, hp, qi[s])),
                  pl.BlockSpec((1, 2, tq, LANES), lambda b, hp, s, qi, ki: (b, hp, ki[s], 0)),
                  pl.BlockSpec((1, LANES, tq), lambda b, hp, s, qi, ki: (b, hp, ki[s])),
                  pl.BlockSpec((1, FOX_HEADS, LANES), lambda b, hp, s, qi, ki: (b, 0, 0))],
        out_specs=pl.BlockSpec((1, LANES, tq), lambda b, hp, s, qi, ki: (b, hp, qi[s])),
        scratch_shapes=[pltpu.VMEM((2, LANES, tq), BF16), pltpu.VMEM((2, 1, tq), F32),
                        pltpu.VMEM((2, 1, tq), F32), pltpu.VMEM((2, HEAD_DIM, tq), F32)])
    return pl.pallas_call(
        _fox_kernel, grid_spec=grid_spec, out_shape=jax.ShapeDtypeStruct(qt_b.shape, BF16), name="fox_prompt",
        compiler_params=_cparams(("parallel", "parallel", "arbitrary")))(qi_tab, ki_tab, qt_b, kaug, vt_b, c0)


def _cmp_compute(x, pea_ref, peb_ref, wa_ref, wb_ref, w2t_ref, o_ref, carry_ref):
    n = x.shape[0]
    a = jnp.dot((x + pea_ref[...]).astype(BF16), wa_ref[...], preferred_element_type=F32)
    b = jnp.dot((x + peb_ref[...]).astype(BF16), wb_ref[...], preferred_element_type=F32)
    rowi = lax.broadcasted_iota(jnp.int32, a.shape, 0)
    a_prev = jnp.where(rowi == 0, carry_ref[0:1, :], pltpu.roll(a, 1, 0))
    carry_ref[...] = jnp.broadcast_to(a[n - 1:n, :], carry_ref.shape)
    o_ref[0] = _mm_nt(w2t_ref[...], _silu(a_prev + b))


def _cmp_kernel(x_ref, pea_ref, peb_ref, wa_ref, wb_ref, w2t_ref, o_ref, carry_ref):
    @pl.when(pl.program_id(1) == 0)
    def _():
        carry_ref[...] = jnp.zeros_like(carry_ref)
    _cmp_compute(x_ref[0], pea_ref, peb_ref, wa_ref, wb_ref, w2t_ref, o_ref, carry_ref)


def _cmp_paged_kernel(pt_ref, *refs, n_pg):
    pages = refs[:n_pg]
    pea_ref, peb_ref, wa_ref, wb_ref, w2t_ref, o_ref, carry_ref = refs[n_pg:]

    @pl.when(pl.program_id(1) == 0)
    def _():
        carry_ref[...] = jnp.zeros_like(carry_ref)

    x = jnp.concatenate([r[0] for r in pages], axis=0)
    _cmp_compute(x, pea_ref, peb_ref, wa_ref, wb_ref, w2t_ref, o_ref, carry_ref)


def _cmp_weights(pe, w1, w2):
    eye = jnp.eye(NSA_KV_HEADS, dtype=F32)
    hid = w1.shape[2]
    half = NSA_CMP_STRIDE

    def wpart(w):
        return jnp.einsum('lde,hg->lhdge', w, eye).reshape(half * LANES, NSA_KV_HEADS * hid).astype(BF16)

    def ppart(p):
        return jnp.broadcast_to(p[:, None, :], (half, NSA_KV_HEADS, HEAD_DIM)).reshape(1, half * LANES)

    w2t = jnp.einsum('ed,hg->gdhe', w2, eye).reshape(LANES, NSA_KV_HEADS * hid).astype(BF16)
    return ppart(pe[:half]), ppart(pe[half:]), wpart(w1[:half]), wpart(w1[half:]), w2t


def _compress(x, weights, name):
    N, L, _ = x.shape
    n_ch = L // NSA_CMP_STRIDE
    xc = x[:, :n_ch * NSA_CMP_STRIDE].reshape(N, n_ch, NSA_CMP_STRIDE * LANES)
    tch = _tile(n_ch, 256)
    wspecs = [pl.BlockSpec(w.shape, lambda n, j: (0, 0)) for w in weights]
    return pl.pallas_call(
        _cmp_kernel, grid=(N, n_ch // tch),
        in_specs=[pl.BlockSpec((1, tch, xc.shape[2]), lambda n, j: (n, j, 0))] + wspecs,
        out_specs=pl.BlockSpec((1, LANES, tch), lambda n, j: (n, 0, j)),
        out_shape=jax.ShapeDtypeStruct((N, LANES, n_ch), F32), name=name,
        scratch_shapes=[pltpu.VMEM((8, weights[2].shape[1]), F32)],
        compiler_params=_cparams(("parallel", "arbitrary")))(xc, *weights)


def _compress_paged(cache, pt_flat, n_db, weights, name):
    n_pool, page = cache.shape[:2]
    rows = page // NSA_CMP_STRIDE
    xc = cache.reshape(n_pool, rows, NSA_CMP_STRIDE * LANES)
    n_pages = pt_flat.shape[0] // n_db
    n_pg = math.gcd(n_pages, max(1, 256 // rows))
    wspecs = [pl.BlockSpec(w.shape, lambda b, p, pt: (0, 0)) for w in weights]
    page_spec = lambda j: pl.BlockSpec((1, rows, xc.shape[2]),
                                       lambda b, p, pt: (pt[b * n_pages + p * n_pg + j], 0, 0))
    grid_spec = pltpu.PrefetchScalarGridSpec(
        num_scalar_prefetch=1, grid=(n_db, n_pages // n_pg),
        in_specs=[page_spec(j) for j in range(n_pg)] + wspecs,
        out_specs=pl.BlockSpec((1, LANES, n_pg * rows), lambda b, p, pt: (b, 0, p)),
        scratch_shapes=[pltpu.VMEM((8, weights[2].shape[1]), F32)])
    return pl.pallas_call(
        functools.partial(_cmp_paged_kernel, n_pg=n_pg), grid_spec=grid_spec,
        out_shape=jax.ShapeDtypeStruct((n_db, LANES, n_pages * rows), F32), name=name,
        compiler_params=_cparams(("parallel", "arbitrary")))(pt_flat, *([xc] * n_pg), *weights)


def _slc_map_rows(n_ch, ns_pad):
    i = (np.arange(n_ch)[:, None] - 1) * NSA_CMP_STRIDE
    j = np.arange(ns_pad)[None, :] * NSA_SLC_BLOCK
    shared = np.minimum(i + NSA_CMP_LEN, j + NSA_SLC_BLOCK) - np.maximum(i, j)
    m = np.clip(shared, 0, None) / NSA_CMP_LEN
    m[0, :] = 0.0
    return m.astype(np.float32)


def _select_blocks(imp, qpos, ns, n_sel):
    blk = lax.broadcasted_iota(jnp.int32, imp.shape, 1)
    cur = qpos // NSA_SLC_BLOCK
    valid = blk * NSA_SLC_BLOCK <= qpos
    forced = (blk == 0) | (blk == cur) | (blk == cur - 1)
    score = jnp.where(valid, imp + jnp.where(forced, NSA_FORCE_BONUS, 0.0), NEG)
    rank = jnp.zeros(imp.shape, jnp.int32)
    for jp in range(ns):
        sj = score[:, jp:jp + 1]
        beats = (sj > score) | ((sj == score) & (blk > jp))
        rank = rank + beats.astype(jnp.int32)
    return rank < n_sel


def _select_blocks_cols(imp, qpos, ns, n_sel):
    blk = lax.broadcasted_iota(jnp.int32, imp.shape, 0)
    cur = qpos // NSA_SLC_BLOCK
    valid = blk * NSA_SLC_BLOCK <= qpos
    forced = (blk == 0) | (blk == cur) | (blk == cur - 1)
    score = jnp.where(valid, imp + jnp.where(forced, NSA_FORCE_BONUS, 0.0), NEG)
    rank = jnp.zeros(imp.shape, jnp.int32)
    for jp in range(ns):
        sj = score[jp:jp + 1, :]
        beats = (sj > score) | ((sj == score) & (blk > jp))
        rank = rank + beats.astype(jnp.int32)
    return rank < n_sel


def _softmax_cols(s, mask):
    sm = jnp.where(mask, s, NEG)
    m = jnp.max(sm, axis=0, keepdims=True)
    p = jnp.where(mask, jnp.exp(sm - m), 0.0)
    l = jnp.sum(p, axis=0, keepdims=True)
    return p / jnp.where(l > 0.0, l, 1.0)


def _nsa_kernel(qt_ref, kct_ref, vct_ref, ks_ref, vst_ref, kw_ref, vwt_ref, gt_ref, mapt_ref, o_ref,
                qaug_ref, negm_ref, m_ref, l_ref, acc_ref, *, ns, n_sel, tk, win):
    g, i = pl.program_id(1), pl.program_id(2)
    tq = qt_ref.shape[2]
    nl = NSA_GROUP * tq
    n_ch = kct_ref.shape[3]
    per_tile = tk // NSA_SLC_BLOCK
    st = i * tq
    qpos1 = st + lax.broadcasted_iota(jnp.int32, (1, tq), 1)
    rep = lambda a: jnp.concatenate([a] * NSA_GROUP, axis=1)
    qpos = rep(qpos1)

    q4t = jnp.concatenate([qt_ref[0, j * HEAD_DIM:(j + 1) * HEAD_DIM, :] for j in range(NSA_GROUP)], axis=1)
    row = lax.broadcasted_iota(jnp.int32, (LANES, nl), 0)
    qaug_ref[0:LANES, :] = jnp.where(row // HEAD_DIM == g, jnp.concatenate([q4t, q4t], axis=0),
                                     jnp.zeros((LANES, nl), BF16))
    qaug_ref[LANES:2 * LANES, :] = jnp.zeros((LANES, nl), BF16)

    cidx = lax.broadcasted_iota(jnp.int32, (n_ch, tq), 0)
    cmask = rep(((cidx - 1) * NSA_CMP_STRIDE + NSA_CMP_LEN - 1 <= qpos1) & (cidx >= 1))
    sc = lax.dot_general(kct_ref[0, 0].astype(BF16), q4t, (((0,), (0,)), ((), ())), preferred_element_type=F32)
    pc = _softmax_cols(sc, cmask)
    oc = _mm(vct_ref[0, 0], pc)
    pstack = jnp.concatenate([pc[:, j * tq:(j + 1) * tq] for j in range(NSA_GROUP)], axis=0)
    imp = jnp.dot(mapt_ref[...], pstack.astype(BF16), preferred_element_type=F32)[0:HEAD_DIM]
    sel = _select_blocks_cols(imp, qpos1, ns, n_sel)

    negm_ref[...] = rep(jnp.where(sel, 0.0, -MASK_BIG))
    _online_init(m_ref, l_ref, acc_ref)
    onehot = ((lax.broadcasted_iota(jnp.int32, (tk, LANES), 0) // NSA_SLC_BLOCK)
              == lax.broadcasted_iota(jnp.int32, (tk, LANES), 1)).astype(BF16)

    def tile(kt, diagonal):
        k0 = pl.multiple_of(kt * tk, tk)
        nm = negm_ref[pl.ds(pl.multiple_of(kt * per_tile, per_tile), per_tile), :]
        qaug_ref[LANES:LANES + 16, :] = jnp.concatenate(
            [nm, jnp.zeros((16 - per_tile, nl), F32)], axis=0).astype(BF16)
        kaug = jnp.concatenate([ks_ref[0, pl.ds(k0, tk), :], onehot], axis=1)
        s = jnp.dot(kaug, qaug_ref[...], preferred_element_type=F32)
        if diagonal:
            kpos = k0 + lax.broadcasted_iota(jnp.int32, (tk, nl), 0)
            s = jnp.where(kpos <= qpos, s, NEG)
        m_prev = m_ref[...]
        m_new = jnp.maximum(m_prev, jnp.max(s, axis=0, keepdims=True))
        alpha = jnp.exp(m_prev - m_new)
        p = jnp.exp(s - m_new)
        l_ref[...] = alpha * l_ref[...] + jnp.sum(p, axis=0, keepdims=True)
        acc_ref[...] = alpha * acc_ref[...] + jnp.dot(vst_ref[0, :, pl.ds(k0, tk)], p.astype(BF16),
                                                      preferred_element_type=F32)
        m_ref[...] = m_new

    last = st // tk

    def body(kt, carry):
        tile(kt, False)
        return carry

    lax.fori_loop(0, last, body, 0)
    tile(last, True)
    osl = acc_ref[...] / l_ref[...]

    w0 = pl.multiple_of(jnp.maximum(st + tq - win, 0), tq)
    dist = qpos1 - (w0 + lax.broadcasted_iota(jnp.int32, (win, tq), 0))
    sw = jnp.dot(kw_ref[0, pl.ds(w0, win), :], qaug_ref[0:LANES, :], preferred_element_type=F32)
    pw = _softmax_cols(sw, rep((dist >= 0) & (dist <= NSA_WINDOW)))
    ow = jnp.dot(vwt_ref[0, :, pl.ds(w0, win)], pw.astype(BF16), preferred_element_type=F32)

    for j in range(NSA_GROUP):
        base = (NSA_GROUP * g + j) * 3
        gate = lambda r: gt_ref[0, pl.ds(base + r, 1), :]
        cols = slice(j * tq, (j + 1) * tq)
        o = gate(0) * oc[:, cols] + gate(1) * osl[:, cols] + gate(2) * ow[:, cols]
        o_ref[0, j * HEAD_DIM:(j + 1) * HEAD_DIM, :] = o.astype(BF16)


def _nsa_prompt(qt_b, kct, vct, ks_b, vst_b, kw_b, vwt_b, gates_t):
    B, W, S = qt_b.shape
    n_ch = kct.shape[3]
    ns = -(-S // NSA_SLC_BLOCK)
    assert ns <= HEAD_DIM
    n_sel = min(NSA_TOPN, ns)
    tq = _tile(S, 128)
    tk = _tile(S, 512)
    assert tk // NSA_SLC_BLOCK <= 16
    win = min(NSA_WINDOW + tq, S)
    nl = NSA_GROUP * tq
    smap_t = jnp.asarray(np.tile(_slc_map_rows(n_ch, LANES).T, (1, NSA_GROUP)), dtype=BF16)
    per_head = lambda a: pl.BlockSpec((1, 1) + a.shape[2:], lambda b, g, i: (b, g, 0, 0))
    gw = W // NSA_KV_HEADS
    return pl.pallas_call(
        functools.partial(_nsa_kernel, ns=ns, n_sel=n_sel, tk=tk, win=win),
        grid=(B, NSA_KV_HEADS, S // tq),
        in_specs=[pl.BlockSpec((1, gw, tq), lambda b, g, i: (b, g, i)),
                  per_head(kct), per_head(vct),
                  pl.BlockSpec((1, S, LANES), lambda b, g, i: (b, 0, 0)),
                  pl.BlockSpec((1, HEAD_DIM, S), lambda b, g, i: (b, g, 0)),
                  pl.BlockSpec((1, S, LANES), lambda b, g, i: (b, 0, 0)),
                  pl.BlockSpec((1, HEAD_DIM, S), lambda b, g, i: (b, g, 0)),
                  pl.BlockSpec((1, gates_t.shape[1], tq), lambda b, g, i: (b, 0, i)),
                  pl.BlockSpec(smap_t.shape, lambda b, g, i: (0, 0))],
        out_specs=pl.BlockSpec((1, gw, tq), lambda b, g, i: (b, g, i)),
        out_shape=jax.ShapeDtypeStruct(qt_b.shape, BF16), name="nsa_prompt",
        scratch_shapes=[pltpu.VMEM((2 * LANES, nl), BF16), pltpu.VMEM((HEAD_DIM, nl), F32),
                        pltpu.VMEM((1, nl), F32), pltpu.VMEM((1, nl), F32), pltpu.VMEM((HEAD_DIM, nl), F32)],
        compiler_params=_cparams(("parallel", "parallel", "arbitrary")))(
            qt_b, kct, vct, ks_b, vst_b, kw_b, vwt_b, gates_t, smap_t)


def _outproj_kernel(*refs):
    x_ref, w_ref, y_ref = refs[0], refs[-2], refs[-1]
    y = x_ref[...]
    k0 = 0
    for o_ref in refs[1:-2]:
        if len(o_ref.shape) == 3:
            kw = o_ref.shape[1]
            y = y + lax.dot_general(o_ref[0].astype(BF16), w_ref[k0:k0 + kw, :], (((0,), (0,)), ((), ())),
                                    preferred_element_type=F32)
        else:
            kw = o_ref.shape[1]
            y = y + jnp.dot(o_ref[...].astype(BF16), w_ref[k0:k0 + kw, :], preferred_element_type=F32)
        k0 += kw
    y_ref[...] = y


def _outproj(x2d, parts, w_bf, name, tm=None):
    M, D = x2d.shape
    tm = tm or _tile(M, 512)
    specs = []
    for o in parts:
        if o.ndim == 3:
            npb = o.shape[2] // tm
            specs.append(pl.BlockSpec((1, o.shape[1], tm), lambda i, npb=npb: (i // npb, 0, i % npb)))
        else:
            specs.append(pl.BlockSpec((tm, o.shape[1]), lambda i: (i, 0)))
    return pl.pallas_call(
        _outproj_kernel, grid=(M // tm,),
        in_specs=[pl.BlockSpec((tm, D), lambda i: (i, 0))] + specs + [pl.BlockSpec(w_bf.shape, lambda i: (0, 0))],
        out_specs=pl.BlockSpec((tm, D), lambda i: (i, 0)),
        out_shape=jax.ShapeDtypeStruct((M, D), F32), name=name,
        compiler_params=_cparams(("parallel",)))(x2d, *parts, w_bf)


def _ffn_kernel(x_ref, g_ref, wg_ref, wu_ref, wd_ref, y_ref, xn_ref):
    f = pl.program_id(1)

    @pl.when(f == 0)
    def _():
        x = x_ref[...]
        xn_ref[...] = _rmsnorm(x, g_ref[...]).astype(BF16)
        y_ref[...] = x

    xn = xn_ref[...]
    h = _silu(jnp.dot(xn, wg_ref[...], preferred_element_type=F32)) * \
        jnp.dot(xn, wu_ref[...], preferred_element_type=F32)
    y_ref[...] += jnp.dot(h.astype(BF16), wd_ref[...], preferred_element_type=F32)


def _ffn(x2d, gain, wg, wu, wd, name):
    M, D = x2d.shape
    Fd = wg.shape[1]
    tm = _tile(M, 512)
    nf = 2 if Fd % (2 * LANES) == 0 else 1
    fc = Fd // nf
    return pl.pallas_call(
        _ffn_kernel, grid=(M // tm, nf),
        in_specs=[pl.BlockSpec((tm, D), lambda i, f: (i, 0)), pl.BlockSpec((1, D), lambda i, f: (0, 0)),
                  pl.BlockSpec((D, fc), lambda i, f: (0, f)), pl.BlockSpec((D, fc), lambda i, f: (0, f)),
                  pl.BlockSpec((fc, D), lambda i, f: (f, 0))],
        out_specs=pl.BlockSpec((tm, D), lambda i, f: (i, 0)),
        out_shape=jax.ShapeDtypeStruct((M, D), F32), name=name,
        scratch_shapes=[pltpu.VMEM((tm, D), BF16)],
        compiler_params=_cparams(("parallel", "arbitrary")))(x2d, gain.reshape(1, D), wg, wu, wd)


def _moe_kernel(x_ref, g_ref, wr_ref, wg_ref, wu_ref, wd_ref, gf_ref, y_ref, xn_ref, comb_ref, acc_ref,
                *, n_exp):
    e, f = pl.program_id(1), pl.program_id(2)
    tm = x_ref.shape[0]
    lane = lax.broadcasted_iota(jnp.int32, (tm, LANES), 1)

    @pl.when((e == 0) & (f == 0))
    def _():
        xn = _rmsnorm(x_ref[...], g_ref[...])
        xn_ref[...] = xn.astype(BF16)
        xh, xm, _ = _split3(xn)
        wh, wm, _ = _split3(wr_ref[...])
        d = lambda a, b: jnp.dot(a, b, preferred_element_type=F32)
        logits = jnp.where(lane < n_exp, d(xh, wh) + d(xh, wm) + d(xm, wh), NEG)
        v1 = jnp.max(logits, axis=1, keepdims=True)
        i1 = jnp.min(jnp.where(logits == v1, lane, LANES), axis=1, keepdims=True)
        rest = jnp.where(lane == i1, NEG, logits)
        v2 = jnp.max(rest, axis=1, keepdims=True)
        i2 = jnp.min(jnp.where(rest == v2, lane, LANES), axis=1, keepdims=True)
        ex = jnp.exp(v2 - v1)
        comb_ref[...] = jnp.where(lane == i1, 1.0 / (1.0 + ex), jnp.where(lane == i2, ex / (1.0 + ex), 0.0))
        acc_ref[...] = jnp.zeros_like(acc_ref)

    xn = xn_ref[...]
    h = _silu(jnp.dot(xn, wg_ref[0], preferred_element_type=F32)) * \
        jnp.dot(xn, wu_ref[0], preferred_element_type=F32)
    w = jnp.sum(jnp.where(lane == e, comb_ref[...], 0.0), axis=1, keepdims=True)
    acc_ref[...] += w * jnp.dot(h.astype(BF16), wd_ref[0], preferred_element_type=F32)

    @pl.when((e == n_exp - 1) & (f == pl.num_programs(2) - 1))
    def _():
        y_ref[...] = _rmsnorm(x_ref[...] + acc_ref[...], gf_ref[...])


def _moe_final(x2d, gain, w_router, wg, wu, wd, gain_final, name):
    M, D = x2d.shape
    n_exp, _, Fd = wg.shape
    tm = _tile(M, 512)
    nf = 2 if Fd % (2 * LANES) == 0 else 1
    fc = Fd // nf
    wr = jnp.zeros((D, LANES), F32).at[:, :n_exp].set(w_router)
    return pl.pallas_call(
        functools.partial(_moe_kernel, n_exp=n_exp), grid=(M // tm, n_exp, nf),
        in_specs=[pl.BlockSpec((tm, D), lambda i, e, f: (i, 0)), pl.BlockSpec((1, D), lambda i, e, f: (0, 0)),
                  pl.BlockSpec((D, LANES), lambda i, e, f: (0, 0)),
                  pl.BlockSpec((1, D, fc), lambda i, e, f: (e, 0, f)),
                  pl.BlockSpec((1, D, fc), lambda i, e, f: (e, 0, f)),
                  pl.BlockSpec((1, fc, D), lambda i, e, f: (e, f, 0)),
                  pl.BlockSpec((1, D), lambda i, e, f: (0, 0))],
        out_specs=pl.BlockSpec((tm, D), lambda i, e, f: (i, 0)),
        out_shape=jax.ShapeDtypeStruct((M, D), F32), name=name,
        scratch_shapes=[pltpu.VMEM((tm, D), BF16), pltpu.VMEM((tm, LANES), F32), pltpu.VMEM((tm, D), F32)],
        compiler_params=_cparams(("parallel", "arbitrary", "arbitrary")))(
            x2d, gain.reshape(1, D), wr, wg, wu, wd, gain_final.reshape(1, D))


def _dil_kernel(q_ref, k_ref, v_ref, o_ref, m_ref, l_ref, acc_ref, *, ta):
    S = q_ref.shape[1]
    lane = lax.broadcasted_iota(jnp.int32, (ta, LANES), 1)
    first = True
    for window, dil in DIL_PAIRS:
        span = window // dil
        L = S // dil
        n_tiles = L // ta
        kw = min(span + ta, L)
        is_first = first

        def body(it, carry, dil=dil, span=span, n_tiles=n_tiles, kw=kw, is_first=is_first):
            res = it // n_tiles
            a0 = (it % n_tiles) * ta
            ak0 = jnp.maximum(a0 + ta - kw, 0)
            qrows = pl.ds(res + dil * a0, ta, stride=dil) if dil > 1 else pl.ds(pl.multiple_of(a0, ta), ta)
            krows = pl.ds(res + dil * ak0, kw, stride=dil) if dil > 1 else pl.ds(pl.multiple_of(ak0, ta), kw)
            q = q_ref[0, qrows, :] * SCALE
            k = k_ref[0, krows, :].astype(BF16)
            v = v_ref[0, krows, :].astype(BF16)
            d = (a0 + lax.broadcasted_iota(jnp.int32, (ta, kw), 0)) - \
                (ak0 + lax.broadcasted_iota(jnp.int32, (ta, kw), 1))
            mask = (d >= 0) & (d <= span)
            ms, ls, accs = [], [], []
            for h in range(2):
                qh = jnp.where((lane // HEAD_DIM) == h, q, 0.0)
                s = jnp.where(mask, _mm_nt(qh, k), NEG)
                m = jnp.max(s, axis=1, keepdims=True)
                p = jnp.exp(s - m)
                ms.append(m)
                ls.append(jnp.sum(p, axis=1, keepdims=True))
                accs.append(jnp.dot(p.astype(BF16), v, preferred_element_type=F32))
            m2 = jnp.where(lane < HEAD_DIM, ms[0], ms[1])
            l2 = jnp.where(lane < HEAD_DIM, ls[0], ls[1])
            a2 = jnp.where(lane < HEAD_DIM, accs[0], accs[1])
            if not is_first:
                m_old = m_ref[qrows, :]
                m_new = jnp.maximum(m_old, m2)
                w_old = jnp.exp(m_old - m_new)
                w_new = jnp.exp(m2 - m_new)
                l2 = w_old * l_ref[qrows, :] + w_new * l2
                a2 = w_old * acc_ref[qrows, :] + w_new * a2
                m2 = m_new
            m_ref[qrows, :] = m2
            l_ref[qrows, :] = l2
            acc_ref[qrows, :] = a2
            return carry

        lax.fori_loop(0, dil * n_tiles, body, 0)
        first = False
    o_ref[0] = acc_ref[...] / l_ref[...]


def _dilated_prompt(q, k, v):
    B, S, W = q.shape
    ta = _tile(S // DIL_PAIRS[-1][1], 128)
    spec = pl.BlockSpec((1, S, LANES), lambda b, hp: (b, 0, hp))
    return pl.pallas_call(
        functools.partial(_dil_kernel, ta=ta), grid=(B, W // LANES),
        in_specs=[spec, spec, spec], out_specs=spec,
        out_shape=jax.ShapeDtypeStruct(q.shape, F32), name="dilated_prompt",
        scratch_shapes=[pltpu.VMEM((S, LANES), F32)] * 3,
        compiler_params=_cparams(("parallel", "parallel")))(q, k, v)


def _head_major(x, n_heads, pad_rows):
    DB, T, _ = x.shape
    y = x.reshape(DB, T, n_heads, HEAD_DIM).transpose(0, 2, 1, 3)
    return jnp.pad(y, ((0, 0), (0, 0), (0, pad_rows - T), (0, 0)))


def _new_cols(x, n_heads):
    DB, T, _ = x.shape
    y = x.reshape(DB, T, n_heads, HEAD_DIM).transpose(0, 2, 3, 1)
    return jnp.pad(y, ((0, 0), (0, 0), (0, 0), (0, NEW_PAD - T)))


def _row_of(col_vec_row):
    return jnp.transpose(jnp.broadcast_to(col_vec_row, (SUBLANES, LANES)))[0:SUBLANES, 0:1]


def _fox_dec_kernel(pt_ref, *refs, n_pg, n_q):
    kts, vts, lfs = refs[:n_pg], refs[n_pg:2 * n_pg], refs[2 * n_pg:3 * n_pg]
    q_ref, knt_ref, vnt_ref, lfn_ref, o_ref, base_ref, carry_ref, m_ref, l_ref, acc_ref = refs[3 * n_pg:]
    p = pl.program_id(1)
    nh = FOX_HEADS
    page = LANES
    lane = lax.broadcasted_iota(jnp.int32, (QPAD, LANES), 1)
    trow = lax.broadcasted_iota(jnp.int32, (QPAD, LANES), 0)
    r = lax.broadcasted_iota(jnp.int32, (page, page), 0)
    c = lax.broadcasted_iota(jnp.int32, (page, page), 1)
    stack = lambda xs: jnp.concatenate(xs, axis=0)

    @pl.when(p == 0)
    def _():
        _online_init(m_ref, l_ref, acc_ref)
        carry_ref[...] = jnp.zeros_like(carry_ref)
        cnew = _mm3_left(lfn_ref[0], r <= c)
        bases, ss = [], []
        for h in range(nh):
            bh = _row_of(cnew[h:h + 1, :])
            bases.append(bh)
            ss.append(_mm(q_ref[0, h] * SCALE, knt_ref[0, h]) + (bh - cnew[h:h + 1, :]))
        base_ref[...] = jnp.broadcast_to(stack(bases), base_ref.shape)
        live = (lane <= trow) & (lane < n_q)
        s = jnp.where(stack([live] * nh), stack(ss), NEG)
        alpha, pr = _online_step(s, m_ref, l_ref)
        pv = stack([_mm_nt(pr[h * QPAD:(h + 1) * QPAD], vnt_ref[0, h]) for h in range(nh)])
        acc_ref[...] = alpha * acc_ref[...] + pv

    lf_all = stack([lf[0] for lf in lfs])
    after_all = _mm3_left(lf_all, r > c)
    tot_all = after_all[:, 0:1] + lf_all[:, 0:1]
    carry = carry_ref[:, 0:1]
    decay = [None] * n_pg
    for j in reversed(range(n_pg)):
        decay[j] = carry + after_all[j * nh:(j + 1) * nh]
        carry = carry + tot_all[j * nh:(j + 1) * nh]
    carry_ref[...] = jnp.broadcast_to(carry, carry_ref.shape)
    ss = []
    for h in range(nh):
        qh = (q_ref[0, h] * SCALE).astype(BF16)
        sh = jnp.concatenate([jnp.dot(qh, kts[j][0, h].astype(BF16), preferred_element_type=F32)
                              + decay[j][h:h + 1, :] for j in range(n_pg)], axis=1)
        ss.append(sh)
    s = stack(ss) + base_ref[:, 0:1]
    alpha, pr = _online_step(s, m_ref, l_ref)
    pvs = []
    for h in range(nh):
        ph = pr[h * QPAD:(h + 1) * QPAD]
        pvs.append(sum(_mm_nt(ph[:, j * page:(j + 1) * page], vts[j][0, h]) for j in range(n_pg)))
    acc_ref[...] = alpha * acc_ref[...] + stack(pvs)

    @pl.when(p == pl.num_programs(1) - 1)
    def _():
        o_ref[0] = acc_ref[...] / l_ref[...]


def _fox_decode(q, k_new, v_new, lf_new, cache_k, cache_v, cache_lf, pt_flat):
    DB, T, W = q.shape
    n_pool, page, nh, _ = cache_k.shape
    assert page == LANES and T <= QPAD
    n_pages = pt_flat.shape[0] // DB
    n_pg = math.gcd(n_pages, PAGES_PER_STEP)
    n_steps = n_pages // n_pg
    ckt = jnp.transpose(cache_k, (0, 2, 3, 1))
    cvt = jnp.transpose(cache_v, (0, 2, 3, 1))
    clf = jnp.transpose(cache_lf, (0, 2, 1))
    qh = _head_major(q, nh, QPAD)
    knt, vnt = _new_cols(k_new, nh), _new_cols(v_new, nh)
    lfn = jnp.pad(jnp.transpose(lf_new, (0, 2, 1)), ((0, 0), (0, 0), (0, LANES - T)))
    page_idx = lambda j: (lambda b, p, pt: (pt[b * n_pages + (n_steps - 1 - p) * n_pg + j], 0, 0, 0))
    lf_idx = lambda j: (lambda b, p, pt: (pt[b * n_pages + (n_steps - 1 - p) * n_pg + j], 0, 0))
    per_db = lambda a: pl.BlockSpec((1,) + a.shape[1:], lambda b, p, pt: (b,) + (0,) * (a.ndim - 1))
    rows = nh * QPAD
    grid_spec = pltpu.PrefetchScalarGridSpec(
        num_scalar_prefetch=1, grid=(DB, n_steps),
        in_specs=([pl.BlockSpec((1, nh, HEAD_DIM, page), page_idx(j)) for j in range(n_pg)] * 2
                  + [pl.BlockSpec((1, nh, page), lf_idx(j)) for j in range(n_pg)]
                  + [per_db(qh), per_db(knt), per_db(vnt), per_db(lfn)]),
        out_specs=pl.BlockSpec((1, rows, HEAD_DIM), lambda b, p, pt: (b, 0, 0)),
        scratch_shapes=[pltpu.VMEM((rows, LANES), F32), pltpu.VMEM((nh, LANES), F32),
                        pltpu.VMEM((rows, 1), F32), pltpu.VMEM((rows, 1), F32), pltpu.VMEM((rows, HEAD_DIM), F32)])
    o = pl.pallas_call(
        functools.partial(_fox_dec_kernel, n_pg=n_pg, n_q=T), grid_spec=grid_spec,
        out_shape=jax.ShapeDtypeStruct((DB, rows, HEAD_DIM), F32), name="fox_decode",
        compiler_params=_cparams(("parallel", "arbitrary")))(
            pt_flat, *([ckt] * n_pg), *([cvt] * n_pg), *([clf] * n_pg), qh, knt, vnt, lfn)
    return o.reshape(DB, nh, QPAD, HEAD_DIM)[:, :, :T].transpose(0, 2, 1, 3).reshape(DB, T, W)


def _nsa_dec_a_kernel(q_ref, kct_ref, vct_ref, kwt_ref, vwt_ref, kwnt_ref, vwnt_ref, gt_ref, map_ref,
                      o_ref, bias_ref, *, n_q, ns, n_sel, past, n_keys):
    nr = NSA_GROUP * QPAD
    n_ch = kct_ref.shape[3]
    wb = kwt_ref.shape[3]
    t_row = lax.broadcasted_iota(jnp.int32, (nr, 1), 0) % QPAD
    imps = []
    for kv in range(NSA_KV_HEADS):
        q = (q_ref[0, kv] * SCALE).astype(BF16)
        cidx = lax.broadcasted_iota(jnp.int32, (nr, n_ch), 1)
        pc = _softmax_rows(_mm(q, kct_ref[0, kv]), cidx >= 1)
        oc = _mm_nt(pc, vct_ref[0, kv])
        pcat = jnp.concatenate([pc[j * QPAD:(j + 1) * QPAD] for j in range(NSA_GROUP)], axis=1)
        imps.append(jnp.dot(pcat.astype(BF16), map_ref[...], preferred_element_type=F32))
        sw = _mm(q, kwt_ref[0, kv])
        sn = _mm(q, kwnt_ref[0, kv])
        iw = lax.broadcasted_iota(jnp.int32, (nr, wb), 1)
        un = lax.broadcasted_iota(jnp.int32, (nr, NEW_PAD), 1)
        mw = (wb + t_row - iw <= NSA_WINDOW) & (past - wb + iw >= 0)
        mn = (un <= t_row) & (un < n_q)
        sw = jnp.where(mw, sw, NEG)
        sn = jnp.where(mn, sn, NEG)
        m = jnp.maximum(jnp.max(sw, axis=1, keepdims=True), jnp.max(sn, axis=1, keepdims=True))
        pw = jnp.where(mw, jnp.exp(sw - m), 0.0)
        pn = jnp.where(mn, jnp.exp(sn - m), 0.0)
        l = jnp.sum(pw, axis=1, keepdims=True) + jnp.sum(pn, axis=1, keepdims=True)
        ow = (_mm_nt(pw, vwt_ref[0, kv]) + _mm_nt(pn, vwnt_ref[0, kv])) / l
        gt = gt_ref[0, kv]
        o_ref[0, kv] = gt[:, 0:1] * oc + gt[:, 2:3] * ow

    imp = jnp.concatenate(imps, axis=0)
    t_sel = lax.broadcasted_iota(jnp.int32, (imp.shape[0], 1), 0) % QPAD
    sel = _select_blocks(imp, past + t_sel, ns, n_sel).astype(BF16)
    ch = 8 * LANES
    for c0 in range(0, n_keys, ch):
        w = min(ch, n_keys - c0)
        key = c0 + lax.broadcasted_iota(jnp.int32, (sel.shape[1], w), 1)
        onehot = (key // NSA_SLC_BLOCK == lax.broadcasted_iota(jnp.int32, (sel.shape[1], w), 0)).astype(BF16)
        chosen = jnp.dot(sel, onehot, preferred_element_type=F32) > 0.5
        kpos = c0 + lax.broadcasted_iota(jnp.int32, (sel.shape[0], w), 1)
        ok = chosen & (kpos <= past + t_sel) & (kpos < past + n_q)
        bias_ref[0, :, c0:c0 + w] = jnp.where(ok, 0.0, NEG)


def _nsa_dec_b_kernel(pt_ref, *refs, n_pg):
    kts, vts = refs[:n_pg], refs[n_pg:2 * n_pg]
    q_ref, knt_ref, vnt_ref, bias_ref, biasn_ref, o_ref, m_ref, l_ref, acc_ref = refs[2 * n_pg:]
    p = pl.program_id(1)
    nr = NSA_GROUP * QPAD
    page = LANES
    stack = lambda xs: jnp.concatenate(xs, axis=0)

    @pl.when(p == 0)
    def _():
        _online_init(m_ref, l_ref, acc_ref)

    def update(s_of, v_of):
        ss = []
        for kv in range(NSA_KV_HEADS):
            ss.append(s_of(kv, (q_ref[0, kv] * SCALE).astype(BF16)))
        alpha, pr = _online_step(stack(ss), m_ref, l_ref)
        acc_ref[...] = alpha * acc_ref[...] + stack([v_of(kv, pr[kv * nr:(kv + 1) * nr])
                                                     for kv in range(NSA_KV_HEADS)])

    def s_pages(kv, q):
        s = jnp.concatenate([jnp.dot(q, kts[j][0, kv].astype(BF16), preferred_element_type=F32)
                             for j in range(n_pg)], axis=1)
        return s + jnp.concatenate([bias_ref[0, kv * QPAD:(kv + 1) * QPAD]] * NSA_GROUP, axis=0)

    def v_pages(kv, pr):
        return sum(_mm_nt(pr[:, j * page:(j + 1) * page], vts[j][0, kv]) for j in range(n_pg))

    update(s_pages, v_pages)

    @pl.when(p == pl.num_programs(1) - 1)
    def _():
        update(lambda kv, q: _mm(q, knt_ref[0, kv])
               + jnp.concatenate([biasn_ref[0, kv * QPAD:(kv + 1) * QPAD]] * NSA_GROUP, axis=0),
               lambda kv, pr: _mm_nt(pr, vnt_ref[0, kv]))
        o_ref[0] = acc_ref[...] / l_ref[...]


def _nsa_decode(qb, kct, vct, cache_ks, cache_vs, ks_new, vs_new, swa_k, swa_v, kw_new, vw_new,
                gates, pt_flat, past):
    DB, T, W = qb.shape
    n_pool, page = cache_ks.shape[:2]
    assert page == LANES and past % LANES == 0 and T <= QPAD
    n_pages = pt_flat.shape[0] // DB
    n_ch = kct.shape[3]
    ns = -(-(past + T) // NSA_SLC_BLOCK)
    ns_pad = -(-ns // LANES) * LANES
    n_sel = min(NSA_TOPN, ns)
    nr = NSA_GROUP * QPAD
    n_keys = past + NEW_PAD
    q5 = _head_major(qb, NSA_HEADS, QPAD).reshape(DB, NSA_KV_HEADS, nr, HEAD_DIM)
    g5 = _head_major(jnp.pad(gates.reshape(DB, T, NSA_HEADS, 3), ((0, 0),) * 3 + ((0, HEAD_DIM - 3),))
                     .reshape(DB, T, NSA_HEADS * HEAD_DIM), NSA_HEADS, QPAD)[..., :3]
    g5 = g5.reshape(DB, NSA_KV_HEADS, nr, 3)
    smap = jnp.asarray(np.tile(_slc_map_rows(n_ch, ns_pad), (NSA_GROUP, 1)), dtype=BF16)
    kwt = jnp.transpose(swa_k, (0, 2, 3, 1))
    vwt = jnp.transpose(swa_v, (0, 2, 3, 1))
    kwnt, vwnt = _new_cols(kw_new, NSA_KV_HEADS), _new_cols(vw_new, NSA_KV_HEADS)
    per = lambda a: pl.BlockSpec((1,) + a.shape[1:], lambda b: (b,) + (0,) * (a.ndim - 1))
    part, bias = pl.pallas_call(
        functools.partial(_nsa_dec_a_kernel, n_q=T, ns=ns, n_sel=n_sel, past=past, n_keys=n_keys), grid=(DB,),
        in_specs=[per(q5), per(kct), per(vct), per(kwt), per(vwt), per(kwnt), per(vwnt), per(g5),
                  pl.BlockSpec(smap.shape, lambda b: (0, 0))],
        out_specs=[pl.BlockSpec((1, NSA_KV_HEADS, nr, HEAD_DIM), lambda b: (b, 0, 0, 0)),
                   pl.BlockSpec((1, NSA_KV_HEADS * QPAD, n_keys), lambda b: (b, 0, 0))],
        out_shape=[jax.ShapeDtypeStruct((DB, NSA_KV_HEADS, nr, HEAD_DIM), F32),
                   jax.ShapeDtypeStruct((DB, NSA_KV_HEADS * QPAD, n_keys), F32)], name="nsa_decode_a",
        compiler_params=_cparams(("parallel",)))(q5, kct, vct, kwt, vwt, kwnt, vwnt, g5, smap)

    n_pg = math.gcd(n_pages, PAGES_PER_STEP)
    ckt = jnp.transpose(cache_ks, (0, 2, 3, 1))
    cvt = jnp.transpose(cache_vs, (0, 2, 3, 1))
    ksnt, vsnt = _new_cols(ks_new, NSA_KV_HEADS), _new_cols(vs_new, NSA_KV_HEADS)
    page_idx = lambda j: (lambda b, p, pt: (pt[b * n_pages + p * n_pg + j], 0, 0, 0))
    per_db = lambda a: pl.BlockSpec((1,) + a.shape[1:], lambda b, p, pt: (b,) + (0,) * (a.ndim - 1))
    rows = NSA_KV_HEADS * nr
    grid_spec = pltpu.PrefetchScalarGridSpec(
        num_scalar_prefetch=1, grid=(DB, n_pages // n_pg),
        in_specs=([pl.BlockSpec((1, NSA_KV_HEADS, HEAD_DIM, page), page_idx(j)) for j in range(n_pg)] * 2
                  + [per_db(q5), per_db(ksnt), per_db(vsnt),
                     pl.BlockSpec((1, NSA_KV_HEADS * QPAD, n_pg * page), lambda b, p, pt: (b, 0, p)),
                     pl.BlockSpec((1, NSA_KV_HEADS * QPAD, NEW_PAD), lambda b, p, pt: (b, 0, past // NEW_PAD))]),
        out_specs=pl.BlockSpec((1, rows, HEAD_DIM), lambda b, p, pt: (b, 0, 0)),
        scratch_shapes=[pltpu.VMEM((rows, 1), F32), pltpu.VMEM((rows, 1), F32), pltpu.VMEM((rows, HEAD_DIM), F32)])
    osl = pl.pallas_call(
        functools.partial(_nsa_dec_b_kernel, n_pg=n_pg), grid_spec=grid_spec,
        out_shape=jax.ShapeDtypeStruct((DB, rows, HEAD_DIM), F32), name="nsa_decode_b",
        compiler_params=_cparams(("parallel", "arbitrary")))(
            pt_flat, *([ckt] * n_pg), *([cvt] * n_pg), q5, ksnt, vsnt, bias, bias)

    o = part + g5[..., 1:2] * osl.reshape(DB, NSA_KV_HEADS, nr, HEAD_DIM)
    return o.reshape(DB, NSA_HEADS, QPAD, HEAD_DIM)[:, :, :T].transpose(0, 2, 1, 3).reshape(DB, T, W)


def _dil_dec_kernel(q_ref, kt_ref, vt_ref, knt_ref, vnt_ref, o_ref, *, n_q, wc):
    hb = kt_ref.shape[1]
    t = lax.broadcasted_iota(jnp.int32, (QPAD, 1), 0)

    def log_mult(d, ok):
        w = jnp.zeros(d.shape, F32)
        for window, dil in DIL_PAIRS:
            w = w + ((d >= 0) & (d <= window) & (d % dil == 0)).astype(F32)
        return jnp.where(ok, w, 0.0)

    wk = log_mult(wc + t - lax.broadcasted_iota(jnp.int32, (QPAD, wc), 1), t < n_q)
    un = lax.broadcasted_iota(jnp.int32, (QPAD, NEW_PAD), 1)
    wn = log_mult(t - un, (t < n_q) & (un < n_q))
    for h in range(hb):
        q = (q_ref[0, h] * SCALE).astype(BF16)
        sk = jnp.where(wk > 0.0, _mm(q, kt_ref[0, h]), NEG)
        sn = jnp.where(wn > 0.0, _mm(q, knt_ref[0, h]), NEG)
        m = jnp.maximum(jnp.max(sk, axis=1, keepdims=True), jnp.max(sn, axis=1, keepdims=True))
        pk = wk * jnp.exp(sk - m)
        pn = wn * jnp.exp(sn - m)
        l = jnp.sum(pk, axis=1, keepdims=True) + jnp.sum(pn, axis=1, keepdims=True)
        o_ref[0, h] = (_mm_nt(pk, vt_ref[0, h]) + _mm_nt(pn, vnt_ref[0, h])) / jnp.where(l > 0.0, l, 1.0)


def _dilated_decode(q, k_new, v_new, cache_k, cache_v):
    DB, T, W = q.shape
    wc, nh = cache_k.shape[1], cache_k.shape[2]
    ckt = jnp.transpose(cache_k, (0, 2, 3, 1))
    cvt = jnp.transpose(cache_v, (0, 2, 3, 1))
    qh = _head_major(q, nh, QPAD)
    knt, vnt = _new_cols(k_new, nh), _new_cols(v_new, nh)
    hb = math.gcd(nh, 4)
    spec = lambda a: pl.BlockSpec((1, hb) + a.shape[2:], lambda b, j: (b, j, 0, 0))
    o = pl.pallas_call(
        functools.partial(_dil_dec_kernel, n_q=T, wc=wc), grid=(DB, nh // hb),
        in_specs=[spec(qh), spec(ckt), spec(cvt), spec(knt), spec(vnt)],
        out_specs=pl.BlockSpec((1, hb, QPAD, HEAD_DIM), lambda b, j: (b, j, 0, 0)),
        out_shape=jax.ShapeDtypeStruct((DB, nh, QPAD, HEAD_DIM), F32), name="dilated_decode",
        compiler_params=_cparams(("parallel", "parallel")))(qh, ckt, cvt, knt, vnt)
    return o[:, :, :T].transpose(0, 2, 1, 3).reshape(DB, T, W)


def kernel(x_prompt, x_sample, cache_a_k, cache_a_v, cache_a_logf, cache_b_cmp_k, cache_b_cmp_v, cache_b_slc_k, cache_b_slc_v, cache_b_swa_k, cache_b_swa_v, cache_c_k, cache_c_v, page_table, norm_mix0, w_in0, fox_bf, nsa_pe_k, nsa_w1_k, nsa_w2_k, nsa_pe_v, nsa_w1_v, nsa_w2_v, w_out0, norm_ffn0, ffn_w_gate, ffn_w_up, ffn_w_down, norm_mix1, w_in1, w_out1, norm_ffn1, moe_router, moe_w_gate, moe_w_up, moe_w_down, norm_final):
    B, S, D = x_prompt.shape
    DB, T, _ = x_sample.shape
    n_pages = page_table.shape[1]
    past = n_pages * cache_a_k.shape[1]
    pt_flat = page_table.reshape(-1).astype(jnp.int32)
    fw = FOX_HEADS * HEAD_DIM
    nw = NSA_HEADS * HEAD_DIM
    kvw = NSA_KV_HEADS * HEAD_DIM

    cuts = np.cumsum([0, fw, fw, fw, FOX_HEADS, nw] + [kvw] * 6 + [3 * NSA_HEADS])
    col = lambda i: w_in0[:, cuts[i]:cuts[i + 1]]
    qa_w, ka_w, va_w, fa_w, qb_w, kc_w, vc_w, ks_w, vs_w, kw_w, vw_w, gb_w = [col(i) for i in range(12)]

    f32_out = ((F32, 1.0),)
    bf16_out = ((BF16, 1.0),)

    def pack(ws, ropes, emits=None):
        widths = [w.shape[1] for w in ws]
        starts = np.concatenate([[0], np.cumsum(widths)[:-1]])
        emits = emits or [f32_out] * len(ws)
        return (jnp.concatenate(ws, axis=1).astype(BF16),
                [(int(s), int(w), r, e) for s, w, r, e in zip(starts, widths, ropes, emits)])

    both_out = ((F32, 1.0), (BF16, 1.0))
    q_out = ((BF16, SCALE),)
    w0r, segs0r = pack([kc_w, vc_w, ka_w, ks_w, kw_w], [True, False, False, True, True],
                       [f32_out] * 2 + [bf16_out] * 3)
    w0c, segs0c = pack([qa_w, ka_w, va_w, qb_w, kc_w, ks_w, kw_w, vc_w, vs_w, vw_w, gb_w],
                       [False, False, False, True, True, True, True, False, False, False, "sigmoid"],
                       [q_out, f32_out, both_out, q_out, f32_out, f32_out, f32_out, f32_out, both_out, both_out,
                        f32_out])
    w0c = w0c.T
    w0s, segs0s = pack([qa_w, ka_w, va_w, qb_w, kc_w, ks_w, kw_w, vc_w, vs_w, vw_w],
                       [False, False, False, True, True, True, True, False, False, False])
    wgate = gb_w.astype(BF16)
    wft = jnp.zeros((16, D), F32).at[:FOX_HEADS].set(fa_w.T).astype(BF16)
    wfr = jnp.zeros((D, LANES), F32).at[:, :FOX_HEADS].set(fa_w).astype(BF16)
    brow = jnp.zeros((1, LANES), F32).at[0, :FOX_HEADS].set(fox_bf)
    logf_args = (wft, fox_bf.reshape(FOX_HEADS, 1).astype(F32), wfr, brow)
    dw = w_in1.shape[1] // 3
    w1 = w_in1.astype(BF16)
    segs1 = [(0, dw, True, f32_out), (dw, dw, True, f32_out), (2 * dw, dw, False, f32_out)]
    w1c = w1[:, dw:].T
    segs1c = [(0, dw, True, f32_out), (dw, dw, False, f32_out)]
    w_out0_b, w_out1_b = w_out0.astype(BF16), w_out1.astype(BF16)
    ffn_g, ffn_u, ffn_d = ffn_w_gate.astype(BF16), ffn_w_up.astype(BF16), ffn_w_down.astype(BF16)
    moe_g, moe_u, moe_d = moe_w_gate.astype(BF16), moe_w_up.astype(BF16), moe_w_down.astype(BF16)
    cmp_k_w = _cmp_weights(nsa_pe_k, nsa_w1_k, nsa_w2_k)
    cmp_v_w = _cmp_weights(nsa_pe_v, nsa_w1_v, nsa_w2_v)

    tab_p = _rope_tables(jnp.arange(S))
    tab_s = _rope_tables(past + jnp.arange(DB * T) % T)
    tm_p = _tile(S, 512)
    npb = S // tm_p
    win_b = min(NSA_WINDOW, S)
    win_c = min(DIL_WINDOW_MAX, S)

    xp = x_prompt.reshape(B * S, D)
    (kc, vc, ka_b, ks_b, kw_b, qat_b, kat, vat, vat_b, qbt_b, kct, kst, kwt, vct, vst, vst_b, vwt, vwt_b,
     gates_t, lft_p, lf_rows) = _project(
        xp, norm_mix0, tab_p, npb, tm_p, w=w0r, row_segs=segs0r, wt=w0c, col_segs=segs0c,
        logf=logf_args, name="proj0_prompt")
    r3 = lambda a: a.reshape(B, S, a.shape[-1])
    heads = lambda a, h: a.reshape(B, h, HEAD_DIM, a.shape[-1])
    kaug, c0 = _fox_prep(r3(ka_b), r3(lf_rows), tm_p)
    o_at = _fox_prompt(qat_b, kaug, vat_b, c0, tm_p)
    kcmp_p = _compress(r3(kc), cmp_k_w, "compress_k_prompt")
    vcmp_p = _compress(r3(vc), cmp_v_w, "compress_v_prompt")
    kv2 = lambda a: heads(a, NSA_KV_HEADS)
    o_bt = _nsa_prompt(qbt_b, kv2(kcmp_p), kv2(vcmp_p), r3(ks_b), vst_b, r3(kw_b), vwt_b, gates_t)
    hp = _outproj(xp, [o_at, o_bt], w_out0_b, "outproj0_prompt", tm=tm_p)
    hp = _ffn(hp, norm_ffn0, ffn_g, ffn_u, ffn_d, "ffn_prompt")

    xs = x_sample.reshape(DB * T, D)
    (qa_s, ka_s, va_s, qb_s, kc_s, ks_s, kw_s, vc_s, vs_s, vw_s, gates_s, lft_s, _) = _project(
        xs, norm_mix0, tab_s, 1, DB * T, w=w0s, row_segs=segs0s, w_gate=wgate, logf=logf_args,
        name="proj0_sample")
    s3 = lambda a: a.reshape(DB, T, a.shape[-1])
    lf_s = jnp.transpose(lft_s[0].reshape(FOX_HEADS, DB, T), (1, 2, 0))
    o_a_s = _fox_decode(s3(qa_s), s3(ka_s), s3(va_s), lf_s, cache_a_k, cache_a_v, cache_a_logf, pt_flat)
    kcmp_s = _compress_paged(cache_b_cmp_k, pt_flat, DB, cmp_k_w, "compress_k_paged")
    vcmp_s = _compress_paged(cache_b_cmp_v, pt_flat, DB, cmp_v_w, "compress_v_paged")
    kvs = lambda a: a.reshape(DB, NSA_KV_HEADS, HEAD_DIM, a.shape[-1])
    o_b_s = _nsa_decode(s3(qb_s), kvs(kcmp_s), kvs(vcmp_s), cache_b_slc_k, cache_b_slc_v, s3(ks_s), s3(vs_s),
                        cache_b_swa_k, cache_b_swa_v, s3(kw_s), s3(vw_s), s3(gates_s), pt_flat, past)
    hs = _outproj(xs, [o_a_s.reshape(DB * T, fw), o_b_s.reshape(DB * T, nw)], w_out0_b, "outproj0_sample")
    hs = _ffn(hs, norm_ffn0, ffn_g, ffn_u, ffn_d, "ffn_sample")

    first_c = (S - win_c) // tm_p
    q1, k1, v1, k1t, v1t = _project(hp, norm_mix1, tab_p, npb, tm_p, w=w1, row_segs=segs1, wt=w1c,
                                    col_segs=segs1c, col_from=(npb, first_c), name="proj1_prompt")
    o1 = _dilated_prompt(q1.reshape(B, S, dw), k1.reshape(B, S, dw), v1.reshape(B, S, dw))
    hp = _outproj(hp, [o1.reshape(B * S, dw)], w_out1_b, "outproj1_prompt")
    y_prompt = _moe_final(hp, norm_ffn1, moe_router, moe_g, moe_u, moe_d, norm_final, "moe_prompt").reshape(B, S, D)

    q1s, k1s, v1s = _project(hs, norm_mix1, tab_s, 1, DB * T, w=w1, row_segs=segs1, name="proj1_sample")
    o1s = _dilated_decode(s3(q1s), s3(k1s), s3(v1s), cache_c_k, cache_c_v)
    hs = _outproj(hs, [o1s.reshape(DB * T, dw)], w_out1_b, "outproj1_sample")
    y_sample = _moe_final(hs, norm_ffn1, moe_router, moe_g, moe_u, moe_d, norm_final, "moe_sample").reshape(DB, T, D)

    def state(a, h, last=None):
        a = a.reshape(a.shape[0], h, HEAD_DIM, a.shape[-1])
        if last is not None:
            a = a[..., a.shape[-1] - last:]
        return jnp.transpose(a, (0, 3, 1, 2))

    h4 = lambda a, h: a.reshape(DB, T, h, HEAD_DIM)
    nh1 = dw // HEAD_DIM
    return (y_prompt, y_sample,
            state(kat, FOX_HEADS), state(vat, FOX_HEADS), jnp.transpose(lft_p, (0, 2, 1)),
            state(kct, NSA_KV_HEADS), state(vct, NSA_KV_HEADS), state(kst, NSA_KV_HEADS), state(vst, NSA_KV_HEADS),
            state(kwt, NSA_KV_HEADS, win_b), state(vwt, NSA_KV_HEADS, win_b),
            state(k1t, nh1, win_c), state(v1t, nh1, win_c),
            h4(ka_s, FOX_HEADS), h4(va_s, FOX_HEADS), lf_s,
            h4(kc_s, NSA_KV_HEADS), h4(vc_s, NSA_KV_HEADS), h4(ks_s, NSA_KV_HEADS),
            h4(vs_s, NSA_KV_HEADS), h4(kw_s, NSA_KV_HEADS), h4(vw_s, NSA_KV_HEADS),
            h4(k1s, nh1), h4(v1s, nh1))
```

```python
import functools
import math

import numpy as np
import jax
import jax.numpy as jnp
from jax import lax
from jax.experimental import pallas as pl
from jax.experimental.pallas import tpu as pltpu

F32 = jnp.float32
BF16 = jnp.bfloat16

HEAD_DIM = 64
HALF = HEAD_DIM // 2
LANES = 128
SUBLANES = 8
ROPE_THETA = 10000.0
RMS_EPS = 1e-6
NEG = -1e30
MASK_BIG = 30000.0
SCALE = HEAD_DIM ** -0.5

FOX_HEADS = 8
NSA_HEADS = 8
NSA_KV_HEADS = 2
NSA_GROUP = NSA_HEADS // NSA_KV_HEADS
NSA_CMP_LEN = 32
NSA_CMP_STRIDE = 16
NSA_SLC_BLOCK = 64
NSA_TOPN = 16
NSA_WINDOW = 512
NSA_FORCE_BONUS = 1e3
DIL_PAIRS = ((128, 1), (512, 4), (2048, 16))
DIL_WINDOW_MAX = 2048
TOP_K = 2
QPAD = SUBLANES
NEW_PAD = LANES
PAGES_PER_STEP = 8

VMEM_LIMIT = 56 * 1024 * 1024


def _tile(n, pref):
    return pref if n % pref == 0 else n


def _cparams(sem):
    return pltpu.CompilerParams(dimension_semantics=sem, vmem_limit_bytes=VMEM_LIMIT)


def _mm(a, b):
    return jnp.dot(a.astype(BF16), b.astype(BF16), preferred_element_type=F32)


def _mm_nt(a, b):
    return lax.dot_general(a.astype(BF16), b.astype(BF16), (((1,), (1,)), ((), ())),
                           preferred_element_type=F32)


def _split3(x):
    hi = x.astype(BF16)
    r = x - hi.astype(F32)
    mid = r.astype(BF16)
    lo = (r - mid.astype(F32)).astype(BF16)
    return hi, mid, lo


def _mm3_left(x, exact_rhs):
    b = exact_rhs.astype(BF16)
    hi, mid, lo = _split3(x)
    d = lambda p: jnp.dot(p, b, preferred_element_type=F32)
    return d(hi) + d(mid) + d(lo)


def _sigmoid(z):
    return 1.0 / (1.0 + jnp.exp(-z))


def _silu(z):
    return z * _sigmoid(z)


def _log_sigmoid(z):
    return jnp.minimum(z, 0.0) - jnp.log1p(jnp.exp(-jnp.abs(z)))


def _rmsnorm(x, g):
    return x * lax.rsqrt(jnp.mean(x * x, axis=-1, keepdims=True) + RMS_EPS) * g


def _rope_rows(y, cos, sin_signed):
    n = y.shape[1]
    lane = lax.broadcasted_iota(jnp.int32, y.shape, 1)
    first = (lane % HEAD_DIM) < HALF
    rot = jnp.where(first, pltpu.roll(y, n - HALF, 1), pltpu.roll(y, HALF, 1))
    reps = n // LANES
    if reps > 1:
        cos = jnp.concatenate([cos] * reps, axis=1)
        sin_signed = jnp.concatenate([sin_signed] * reps, axis=1)
    return y * cos + rot * sin_signed


def _rope_cols(yt, cos_t, sin_t):
    out = []
    for h in range(yt.shape[0] // HEAD_DIM):
        a = yt[h * HEAD_DIM:h * HEAD_DIM + HALF]
        b = yt[h * HEAD_DIM + HALF:(h + 1) * HEAD_DIM]
        out += [a * cos_t - b * sin_t, b * cos_t + a * sin_t]
    return jnp.concatenate(out, axis=0)


def _rope_tables(pos):
    inv = jnp.exp(-math.log(ROPE_THETA) * jnp.arange(HALF, dtype=F32) / HALF)
    ang = pos.astype(F32)[:, None] * inv[None, :]
    cos, sin = jnp.cos(ang), jnp.sin(ang)
    return (jnp.concatenate([cos, cos, cos, cos], axis=1),
            jnp.concatenate([-sin, sin, -sin, sin], axis=1), cos.T, sin.T)


def _softmax_rows(s, mask):
    sm = jnp.where(mask, s, NEG)
    m = jnp.max(sm, axis=1, keepdims=True)
    p = jnp.where(mask, jnp.exp(sm - m), 0.0)
    l = jnp.sum(p, axis=1, keepdims=True)
    return p / jnp.where(l > 0.0, l, 1.0)


def _online_step(s, m_ref, l_ref):
    m_prev = m_ref[...]
    m_new = jnp.maximum(m_prev, jnp.max(s, axis=1, keepdims=True))
    alpha = jnp.exp(m_prev - m_new)
    p = jnp.exp(s - m_new)
    l_ref[...] = alpha * l_ref[...] + jnp.sum(p, axis=1, keepdims=True)
    m_ref[...] = m_new
    return alpha, p


def _online_init(m_ref, l_ref, acc_ref):
    m_ref[...] = jnp.full(m_ref.shape, NEG, F32)
    l_ref[...] = jnp.zeros_like(l_ref)
    acc_ref[...] = jnp.zeros_like(acc_ref)


def _proj_kernel(*refs, row_segs, col_segs, n_gate, with_logf, col_from):
    it = iter(refs)
    x_ref, g_ref, cos_ref, sin_ref, cost_ref, sint_ref = [next(it) for _ in range(6)]
    w_ref = next(it) if row_segs else None
    wt_ref = next(it) if col_segs else None
    wg_ref = next(it) if n_gate else None
    if with_logf:
        wft_ref, bcol_ref, wfr_ref, brow_ref = next(it), next(it), next(it), next(it)
    outs = list(it)
    xn = _rmsnorm(x_ref[...], g_ref[...]).astype(BF16)
    k = 0
    for c0, width, rope, emits in row_segs:
        y = jnp.dot(xn, w_ref[:, c0:c0 + width], preferred_element_type=F32)
        if rope:
            y = _rope_rows(y, cos_ref[...], sin_ref[...])
        for dtype, scale in emits:
            outs[k][...] = (y if scale == 1.0 else y * scale).astype(dtype)
            k += 1

    def cols():
        kk = k
        for r0, height, rope, emits in col_segs:
            yt = _mm_nt(wt_ref[r0:r0 + height, :], xn)
            if rope == "sigmoid":
                yt = _sigmoid(yt)
            elif rope:
                yt = _rope_cols(yt, cost_ref[...], sint_ref[...])
            for dtype, scale in emits:
                outs[kk][0] = (yt if scale == 1.0 else yt * scale).astype(dtype)
                kk += 1

    if col_segs:
        if col_from:
            pl.when(pl.program_id(0) % col_from[0] >= col_from[1])(cols)
        else:
            cols()
        k += sum(len(e) for _, _, _, e in col_segs)
    if n_gate:
        outs[k][...] = _sigmoid(jnp.dot(xn, wg_ref[...], preferred_element_type=F32))
        k += 1
    if with_logf:
        yt = _mm_nt(wft_ref[...], xn)
        outs[k][0] = _log_sigmoid(yt[0:FOX_HEADS] + bcol_ref[...])
        outs[k + 1][...] = _log_sigmoid(jnp.dot(xn, wfr_ref[...], preferred_element_type=F32) + brow_ref[...])


def _project(x2d, gain, tables, n_pos_blocks, tm, w=None, row_segs=(), wt=None, col_segs=(),
             w_gate=None, logf=None, col_from=None, name="proj"):
    M, D = x2d.shape
    nt = M // tm
    n_seq = nt // n_pos_blocks
    cos_t, sin_t, cos_c, sin_c = tables
    pos_map = lambda i: (i % n_pos_blocks, 0)
    posc_map = lambda i: (0, i % n_pos_blocks)
    const = lambda a: pl.BlockSpec(a.shape, lambda i: (0,) * a.ndim)
    in_specs = [pl.BlockSpec((tm, D), lambda i: (i, 0)), const(gain.reshape(1, D)),
                pl.BlockSpec((tm, LANES), pos_map), pl.BlockSpec((tm, LANES), pos_map),
                pl.BlockSpec((HALF, tm), posc_map), pl.BlockSpec((HALF, tm), posc_map)]
    args = [x2d, gain.reshape(1, D), cos_t, sin_t, cos_c, sin_c]
    for a in (w, wt, w_gate):
        if a is not None:
            in_specs.append(const(a))
            args.append(a)
    out_shape, out_specs = [], []
    for _, wd, _, emits in row_segs:
        for dtype, _ in emits:
            out_shape.append(jax.ShapeDtypeStruct((M, wd), dtype))
            out_specs.append(pl.BlockSpec((tm, wd), lambda i: (i, 0)))
    first = col_from[1] if col_from else 0
    n_cb = n_pos_blocks - first
    col_map = lambda i: (i // n_pos_blocks, 0, jnp.maximum(i % n_pos_blocks - first, 0))
    for _, ht, _, emits in col_segs:
        for dtype, _ in emits:
            out_shape.append(jax.ShapeDtypeStruct((n_seq, ht, n_cb * tm), dtype))
            out_specs.append(pl.BlockSpec((1, ht, tm), col_map))
    n_gate = 0
    if w_gate is not None:
        n_gate = w_gate.shape[1]
        out_shape.append(jax.ShapeDtypeStruct((M, n_gate), F32))
        out_specs.append(pl.BlockSpec((tm, n_gate), lambda i: (i, 0)))
    if logf is not None:
        in_specs += [const(a) for a in logf]
        args += list(logf)
        out_shape.append(jax.ShapeDtypeStruct((n_seq, FOX_HEADS, n_pos_blocks * tm), F32))
        out_specs.append(pl.BlockSpec((1, FOX_HEADS, tm), lambda i: (i // n_pos_blocks, 0, i % n_pos_blocks)))
        out_shape.append(jax.ShapeDtypeStruct((M, LANES), F32))
        out_specs.append(pl.BlockSpec((tm, LANES), lambda i: (i, 0)))
    return pl.pallas_call(
        functools.partial(_proj_kernel, row_segs=tuple(row_segs), col_segs=tuple(col_segs), n_gate=n_gate,
                          with_logf=logf is not None, col_from=col_from),
        grid=(nt,), in_specs=in_specs, out_specs=out_specs, out_shape=out_shape, name=name,
        compiler_params=_cparams(("arbitrary",)))(*args)


def _fox_prep_kernel(k_ref, lf_ref, kaug_ref, c0_ref, carry_ref):
    j = pl.program_id(1)
    tc = k_ref.shape[1]

    @pl.when(j == 0)
    def _():
        carry_ref[...] = jnp.zeros_like(carry_ref)
        c0_ref[...] = jnp.zeros_like(c0_ref)

    lane = lax.broadcasted_iota(jnp.int32, (FOX_HEADS, LANES), 1)
    start = jnp.transpose(carry_ref[...])[0:FOX_HEADS, 0:1]
    c0_ref[0] = jnp.where(lane == j, start, c0_ref[0])

    r = lax.broadcasted_iota(jnp.int32, (tc, tc), 0)
    c = lax.broadcasted_iota(jnp.int32, (tc, tc), 1)
    tri = (c <= r).astype(BF16)
    hi, mid, lo = _split3(lf_ref[0])
    d = lambda p: jnp.dot(tri, p, preferred_element_type=F32)
    local = d(hi) + d(mid) + d(lo)
    carry_ref[...] = carry_ref[...] + jnp.broadcast_to(local[tc - 1:tc, :], carry_ref.shape)
    parts = jnp.concatenate(_split3(-local), axis=1)
    k = k_ref[0]
    kr = lax.broadcasted_iota(jnp.int32, (k.shape[1], LANES), 0)
    kc = lax.broadcasted_iota(jnp.int32, (k.shape[1], LANES), 1)
    pr = lax.broadcasted_iota(jnp.int32, (3 * LANES, LANES), 0)
    pc = lax.broadcasted_iota(jnp.int32, (3 * LANES, LANES), 1)
    for h in range(FOX_HEADS):
        place_k = ((kr == HEAD_DIM * h + kc) & (kc < HEAD_DIM)).astype(BF16)
        place_c = ((pr % LANES == h) & (pc == HEAD_DIM + pr // LANES)).astype(BF16)
        kaug_ref[0, h] = (jnp.dot(k, place_k, preferred_element_type=F32)
                          + jnp.dot(parts, place_c, preferred_element_type=F32)).astype(BF16)


def _fox_prep(k_b, lf_rows, tc):
    B, S, W = k_b.shape
    assert S // tc <= LANES
    return pl.pallas_call(
        _fox_prep_kernel, grid=(B, S // tc),
        in_specs=[pl.BlockSpec((1, tc, W), lambda b, j: (b, j, 0)),
                  pl.BlockSpec((1, tc, LANES), lambda b, j: (b, j, 0))],
        out_specs=[pl.BlockSpec((1, FOX_HEADS, tc, LANES), lambda b, j: (b, 0, j, 0)),
                   pl.BlockSpec((1, FOX_HEADS, LANES), lambda b, j: (b, 0, 0))],
        out_shape=[jax.ShapeDtypeStruct((B, FOX_HEADS, S, LANES), BF16),
                   jax.ShapeDtypeStruct((B, FOX_HEADS, LANES), F32)], name="fox_prep",
        scratch_shapes=[pltpu.VMEM((SUBLANES, LANES), F32)],
        compiler_params=_cparams(("parallel", "arbitrary")))(k_b, lf_rows)


def _fox_kernel(qi_ref, ki_ref, qt_ref, kaug_ref, vt_ref, c0_ref, o_ref, qa_ref, m_ref, l_ref, acc_ref):
    hp, step = pl.program_id(1), pl.program_id(2)
    qi, ki = qi_ref[step], ki_ref[step]
    tq, tk = qt_ref.shape[2], kaug_ref.shape[2]
    lane1 = lax.broadcasted_iota(jnp.int32, (1, LANES), 1)

    @pl.when(ki == 0)
    def _():
        row = lax.broadcasted_iota(jnp.int32, (HEAD_DIM, tq), 0)
        ones = jnp.where(row < 3, 1.0, 0.0).astype(BF16)
        for h in range(2):
            qa_ref[h] = jnp.concatenate([qt_ref[0, h * HEAD_DIM:(h + 1) * HEAD_DIM, :], ones], axis=0)
        _online_init(m_ref, l_ref, acc_ref)

    def tile(diagonal):
        if diagonal:
            live = lax.broadcasted_iota(jnp.int32, (tk, tq), 0) <= lax.broadcasted_iota(jnp.int32, (tk, tq), 1)
        for h in range(2):
            c0 = c0_ref[0, pl.ds(2 * hp + h, 1), :]
            delta = jnp.sum(jnp.where(lane1 == qi, c0, 0.0) - jnp.where(lane1 == ki, c0, 0.0),
                            axis=1, keepdims=True)
            s = jnp.dot(kaug_ref[0, h], qa_ref[h], preferred_element_type=F32)
            if diagonal:
                s = jnp.where(live, s, NEG)
            m_prev = m_ref[h]
            m_new = jnp.maximum(m_prev, jnp.max(s, axis=0, keepdims=True) + delta)
            p = jnp.exp(s - (m_new - delta))
            alpha = jnp.exp(m_prev - m_new)
            l_ref[h] = alpha * l_ref[h] + jnp.sum(p, axis=0, keepdims=True)
            acc_ref[h] = alpha * acc_ref[h] + jnp.dot(vt_ref[0, h * HEAD_DIM:(h + 1) * HEAD_DIM, :],
                                                      p.astype(BF16), preferred_element_type=F32)
            m_ref[h] = m_new

    pl.when(ki < qi)(lambda: tile(False))

    @pl.when(ki == qi)
    def _():
        tile(True)
        for h in range(2):
            o_ref[0, h * HEAD_DIM:(h + 1) * HEAD_DIM, :] = (acc_ref[h] / l_ref[h]).astype(BF16)


def _fox_prompt(qt_b, kaug, vt_b, c0, tq):
    B, W, S = qt_b.shape
    nq = S // tq
    pairs = [(q, k) for q in range(nq) for k in range(q + 1)]
    qi_tab = jnp.asarray([p[0] for p in pairs], jnp.int32)
    ki_tab = jnp.asarray([p[1] for p in pairs], jnp.int32)
    grid_spec = pltpu.PrefetchScalarGridSpec(
        num_scalar_prefetch=2, grid=(B, W // LANES, len(pairs)),
        in_specs=[pl.BlockSpec((1, LANES, tq), lambda b, hp, s, qi, ki: (b, hp, qi[s])),
                  pl.BlockSpec((1, 2, tq, LANES), lambda b, hp, s, qi, ki: (b, hp, ki[s], 0)),
                  pl.BlockSpec((1, LANES, tq), lambda b, hp, s, qi, ki: (b, hp, ki[s])),
                  pl.BlockSpec((1, FOX_HEADS, LANES), lambda b, hp, s, qi, ki: (b, 0, 0))],
        out_specs=pl.BlockSpec((1, LANES, tq), lambda b, hp, s, qi, ki: (b, hp, qi[s])),
        scratch_shapes=[pltpu.VMEM((2, LANES, tq), BF16), pltpu.VMEM((2, 1, tq), F32),
                        pltpu.VMEM((2, 1, tq), F32), pltpu.VMEM((2, HEAD_DIM, tq), F32)])
    return pl.pallas_call(
        _fox_kernel, grid_spec=grid_spec, out_shape=jax.ShapeDtypeStruct(qt_b.shape, BF16), name="fox_prompt",
        compiler_params=_cparams(("parallel", "parallel", "arbitrary")))(qi_tab, ki_tab, qt_b, kaug, vt_b, c0)


def _cmp_compute(x, pea_ref, peb_ref, wa_ref, wb_ref, w2t_ref, o_ref, carry_ref):
    n = x.shape[0]
    a = jnp.dot((x + pea_ref[...]).astype(BF16), wa_ref[...], preferred_element_type=F32)
    b = jnp.dot((x + peb_ref[...]).astype(BF16), wb_ref[...], preferred_element_type=F32)
    rowi = lax.broadcasted_iota(jnp.int32, a.shape, 0)
    a_prev = jnp.where(rowi == 0, carry_ref[0:1, :], pltpu.roll(a, 1, 0))
    carry_ref[...] = jnp.broadcast_to(a[n - 1:n, :], carry_ref.shape)
    o_ref[0] = _mm_nt(w2t_ref[...], _silu(a_prev + b))


def _cmp_kernel(x_ref, pea_ref, peb_ref, wa_ref, wb_ref, w2t_ref, o_ref, carry_ref):
    @pl.when(pl.program_id(1) == 0)
    def _():
        carry_ref[...] = jnp.zeros_like(carry_ref)
    _cmp_compute(x_ref[0], pea_ref, peb_ref, wa_ref, wb_ref, w2t_ref, o_ref, carry_ref)


def _cmp_paged_kernel(pt_ref, *refs, n_pg):
    pages = refs[:n_pg]
    pea_ref, peb_ref, wa_ref, wb_ref, w2t_ref, o_ref, carry_ref = refs[n_pg:]

    @pl.when(pl.program_id(1) == 0)
    def _():
        carry_ref[...] = jnp.zeros_like(carry_ref)

    x = jnp.concatenate([r[0] for r in pages], axis=0)
    _cmp_compute(x, pea_ref, peb_ref, wa_ref, wb_ref, w2t_ref, o_ref, carry_ref)


def _cmp_weights(pe, w1, w2):
    eye = jnp.eye(NSA_KV_HEADS, dtype=F32)
    hid = w1.shape[2]
    half = NSA_CMP_STRIDE

    def wpart(w):
        return jnp.einsum('lde,hg->lhdge', w, eye).reshape(half * LANES, NSA_KV_HEADS * hid).astype(BF16)

    def ppart(p):
        return jnp.broadcast_to(p[:, None, :], (half, NSA_KV_HEADS, HEAD_DIM)).reshape(1, half * LANES)

    w2t = jnp.einsum('ed,hg->gdhe', w2, eye).reshape(LANES, NSA_KV_HEADS * hid).astype(BF16)
    return ppart(pe[:half]), ppart(pe[half:]), wpart(w1[:half]), wpart(w1[half:]), w2t


def _compress(x, weights, name):
    N, L, _ = x.shape
    n_ch = L // NSA_CMP_STRIDE
    xc = x[:, :n_ch * NSA_CMP_STRIDE].reshape(N, n_ch, NSA_CMP_STRIDE * LANES)
    tch = _tile(n_ch, 256)
    wspecs = [pl.BlockSpec(w.shape, lambda n, j: (0, 0)) for w in weights]
    return pl.pallas_call(
        _cmp_kernel, grid=(N, n_ch // tch),
        in_specs=[pl.BlockSpec((1, tch, xc.shape[2]), lambda n, j: (n, j, 0))] + wspecs,
        out_specs=pl.BlockSpec((1, LANES, tch), lambda n, j: (n, 0, j)),
        out_shape=jax.ShapeDtypeStruct((N, LANES, n_ch), F32), name=name,
        scratch_shapes=[pltpu.VMEM((8, weights[2].shape[1]), F32)],
        compiler_params=_cparams(("parallel", "arbitrary")))(xc, *weights)


def _compress_paged(cache, pt_flat, n_db, weights, name):
    n_pool, page = cache.shape[:2]
    rows = page // NSA_CMP_STRIDE
    xc = cache.reshape(n_pool, rows, NSA_CMP_STRIDE * LANES)
    n_pages = pt_flat.shape[0] // n_db
    n_pg = math.gcd(n_pages, max(1, 256 // rows))
    wspecs = [pl.BlockSpec(w.shape, lambda b, p, pt: (0, 0)) for w in weights]
    page_spec = lambda j: pl.BlockSpec((1, rows, xc.shape[2]),
                                       lambda b, p, pt: (pt[b * n_pages + p * n_pg + j], 0, 0))
    grid_spec = pltpu.PrefetchScalarGridSpec(
        num_scalar_prefetch=1, grid=(n_db, n_pages // n_pg),
        in_specs=[page_spec(j) for j in range(n_pg)] + wspecs,
        out_specs=pl.BlockSpec((1, LANES, n_pg * rows), lambda b, p, pt: (b, 0, p)),
        scratch_shapes=[pltpu.VMEM((8, weights[2].shape[1]), F32)])
    return pl.pallas_call(
        functools.partial(_cmp_paged_kernel, n_pg=n_pg), grid_spec=grid_spec,
        out_shape=jax.ShapeDtypeStruct((n_db, LANES, n_pages * rows), F32), name=name,
        compiler_params=_cparams(("parallel", "arbitrary")))(pt_flat, *([xc] * n_pg), *weights)


def _slc_map_rows(n_ch, ns_pad):
    i = (np.arange(n_ch)[:, None] - 1) * NSA_CMP_STRIDE
    j = np.arange(ns_pad)[None, :] * NSA_SLC_BLOCK
    shared = np.minimum(i + NSA_CMP_LEN, j + NSA_SLC_BLOCK) - np.maximum(i, j)
    m = np.clip(shared, 0, None) / NSA_CMP_LEN
    m[0, :] = 0.0
    return m.astype(np.float32)


def _select_blocks(imp, qpos, ns, n_sel):
    blk = lax.broadcasted_iota(jnp.int32, imp.shape, 1)
    cur = qpos // NSA_SLC_BLOCK
    valid = blk * NSA_SLC_BLOCK <= qpos
    forced = (blk == 0) | (blk == cur) | (blk == cur - 1)
    score = jnp.where(valid, imp + jnp.where(forced, NSA_FORCE_BONUS, 0.0), NEG)
    rank = jnp.zeros(imp.shape, jnp.int32)
    for jp in range(ns):
        sj = score[:, jp:jp + 1]
        beats = (sj > score) | ((sj == score) & (blk > jp))
        rank = rank + beats.astype(jnp.int32)
    return rank < n_sel


def _select_blocks_cols(imp, qpos, ns, n_sel):
    blk = lax.broadcasted_iota(jnp.int32, imp.shape, 0)
    cur = qpos // NSA_SLC_BLOCK
    valid = blk * NSA_SLC_BLOCK <= qpos
    forced = (blk == 0) | (blk == cur) | (blk == cur - 1)
    score = jnp.where(valid, imp + jnp.where(forced, NSA_FORCE_BONUS, 0.0), NEG)
    rank = jnp.zeros(imp.shape, jnp.int32)
    for jp in range(ns):
        sj = score[jp:jp + 1, :]
        beats = (sj > score) | ((sj == score) & (blk > jp))
        rank = rank + beats.astype(jnp.int32)
    return rank < n_sel


def _softmax_cols(s, mask):
    sm = jnp.where(mask, s, NEG)
    m = jnp.max(sm, axis=0, keepdims=True)
    p = jnp.where(mask, jnp.exp(sm - m), 0.0)
    l = jnp.sum(p, axis=0, keepdims=True)
    return p / jnp.where(l > 0.0, l, 1.0)


def _nsa_kernel(qt_ref, kct_ref, vct_ref, ks_ref, vst_ref, kw_ref, vwt_ref, gt_ref, mapt_ref, o_ref,
                qaug_ref, negm_ref, m_ref, l_ref, acc_ref, *, ns, n_sel, tk, win):
    g, i = pl.program_id(1), pl.program_id(2)
    tq = qt_ref.shape[2]
    nl = NSA_GROUP * tq
    n_ch = kct_ref.shape[3]
    per_tile = tk // NSA_SLC_BLOCK
    st = i * tq
    qpos1 = st + lax.broadcasted_iota(jnp.int32, (1, tq), 1)
    rep = lambda a: jnp.concatenate([a] * NSA_GROUP, axis=1)
    qpos = rep(qpos1)

    q4t = jnp.concatenate([qt_ref[0, j * HEAD_DIM:(j + 1) * HEAD_DIM, :] for j in range(NSA_GROUP)], axis=1)
    row = lax.broadcasted_iota(jnp.int32, (LANES, nl), 0)
    qaug_ref[0:LANES, :] = jnp.where(row // HEAD_DIM == g, jnp.concatenate([q4t, q4t], axis=0),
                                     jnp.zeros((LANES, nl), BF16))
    qaug_ref[LANES:2 * LANES, :] = jnp.zeros((LANES, nl), BF16)

    cidx = lax.broadcasted_iota(jnp.int32, (n_ch, tq), 0)
    cmask = rep(((cidx - 1) * NSA_CMP_STRIDE + NSA_CMP_LEN - 1 <= qpos1) & (cidx >= 1))
    sc = lax.dot_general(kct_ref[0, 0].astype(BF16), q4t, (((0,), (0,)), ((), ())), preferred_element_type=F32)
    pc = _softmax_cols(sc, cmask)
    oc = _mm(vct_ref[0, 0], pc)
    pstack = jnp.concatenate([pc[:, j * tq:(j + 1) * tq] for j in range(NSA_GROUP)], axis=0)
    imp = jnp.dot(mapt_ref[...], pstack.astype(BF16), preferred_element_type=F32)[0:HEAD_DIM]
    sel = _select_blocks_cols(imp, qpos1, ns, n_sel)

    negm_ref[...] = rep(jnp.where(sel, 0.0, -MASK_BIG))
    _online_init(m_ref, l_ref, acc_ref)
    onehot = ((lax.broadcasted_iota(jnp.int32, (tk, LANES), 0) // NSA_SLC_BLOCK)
              == lax.broadcasted_iota(jnp.int32, (tk, LANES), 1)).astype(BF16)

    def tile(kt, diagonal):
        k0 = pl.multiple_of(kt * tk, tk)
        nm = negm_ref[pl.ds(pl.multiple_of(kt * per_tile, per_tile), per_tile), :]
        qaug_ref[LANES:LANES + 16, :] = jnp.concatenate(
            [nm, jnp.zeros((16 - per_tile, nl), F32)], axis=0).astype(BF16)
        kaug = jnp.concatenate([ks_ref[0, pl.ds(k0, tk), :], onehot], axis=1)
        s = jnp.dot(kaug, qaug_ref[...], preferred_element_type=F32)
        if diagonal:
            kpos = k0 + lax.broadcasted_iota(jnp.int32, (tk, nl), 0)
            s = jnp.where(kpos <= qpos, s, NEG)
        m_prev = m_ref[...]
        m_new = jnp.maximum(m_prev, jnp.max(s, axis=0, keepdims=True))
        alpha = jnp.exp(m_prev - m_new)
        p = jnp.exp(s - m_new)
        l_ref[...] = alpha * l_ref[...] + jnp.sum(p, axis=0, keepdims=True)
        acc_ref[...] = alpha * acc_ref[...] + jnp.dot(vst_ref[0, :, pl.ds(k0, tk)], p.astype(BF16),
                                                      preferred_element_type=F32)
        m_ref[...] = m_new

    last = st // tk

    def body(kt, carry):
        tile(kt, False)
        return carry

    lax.fori_loop(0, last, body, 0)
    tile(last, True)
    osl = acc_ref[...] / l_ref[...]

    w0 = pl.multiple_of(jnp.maximum(st + tq - win, 0), tq)
    dist = qpos1 - (w0 + lax.broadcasted_iota(jnp.int32, (win, tq), 0))
    sw = jnp.dot(kw_ref[0, pl.ds(w0, win), :], qaug_ref[0:LANES, :], preferred_element_type=F32)
    pw = _softmax_cols(sw, rep((dist >= 0) & (dist <= NSA_WINDOW)))
    ow = jnp.dot(vwt_ref[0, :, pl.ds(w0, win)], pw.astype(BF16), preferred_element_type=F32)

    for j in range(NSA_GROUP):
        base = (NSA_GROUP * g + j) * 3
        gate = lambda r: gt_ref[0, pl.ds(base + r, 1), :]
        cols = slice(j * tq, (j + 1) * tq)
        o = gate(0) * oc[:, cols] + gate(1) * osl[:, cols] + gate(2) * ow[:, cols]
        o_ref[0, j * HEAD_DIM:(j + 1) * HEAD_DIM, :] = o.astype(BF16)


def _nsa_prompt(qt_b, kct, vct, ks_b, vst_b, kw_b, vwt_b, gates_t):
    B, W, S = qt_b.shape
    n_ch = kct.shape[3]
    ns = -(-S // NSA_SLC_BLOCK)
    assert ns <= HEAD_DIM
    n_sel = min(NSA_TOPN, ns)
    tq = _tile(S, 128)
    tk = _tile(S, 512)
    assert tk // NSA_SLC_BLOCK <= 16
    win = min(NSA_WINDOW + tq, S)
    nl = NSA_GROUP * tq
    smap_t = jnp.asarray(np.tile(_slc_map_rows(n_ch, LANES).T, (1, NSA_GROUP)), dtype=BF16)
    per_head = lambda a: pl.BlockSpec((1, 1) + a.shape[2:], lambda b, g, i: (b, g, 0, 0))
    gw = W // NSA_KV_HEADS
    return pl.pallas_call(
        functools.partial(_nsa_kernel, ns=ns, n_sel=n_sel, tk=tk, win=win),
        grid=(B, NSA_KV_HEADS, S // tq),
        in_specs=[pl.BlockSpec((1, gw, tq), lambda b, g, i: (b, g, i)),
                  per_head(kct), per_head(vct),
                  pl.BlockSpec((1, S, LANES), lambda b, g, i: (b, 0, 0)),
                  pl.BlockSpec((1, HEAD_DIM, S), lambda b, g, i: (b, g, 0)),
                  pl.BlockSpec((1, S, LANES), lambda b, g, i: (b, 0, 0)),
                  pl.BlockSpec((1, HEAD_DIM, S), lambda b, g, i: (b, g, 0)),
                  pl.BlockSpec((1, gates_t.shape[1], tq), lambda b, g, i: (b, 0, i)),
                  pl.BlockSpec(smap_t.shape, lambda b, g, i: (0, 0))],
        out_specs=pl.BlockSpec((1, gw, tq), lambda b, g, i: (b, g, i)),
        out_shape=jax.ShapeDtypeStruct(qt_b.shape, BF16), name="nsa_prompt",
        scratch_shapes=[pltpu.VMEM((2 * LANES, nl), BF16), pltpu.VMEM((HEAD_DIM, nl), F32),
                        pltpu.VMEM((1, nl), F32), pltpu.VMEM((1, nl), F32), pltpu.VMEM((HEAD_DIM, nl), F32)],
        compiler_params=_cparams(("parallel", "parallel", "arbitrary")))(
            qt_b, kct, vct, ks_b, vst_b, kw_b, vwt_b, gates_t, smap_t)


def _outproj_kernel(*refs):
    x_ref, w_ref, y_ref = refs[0], refs[-2], refs[-1]
    y = x_ref[...]
    k0 = 0
    for o_ref in refs[1:-2]:
        if len(o_ref.shape) == 3:
            kw = o_ref.shape[1]
            y = y + lax.dot_general(o_ref[0].astype(BF16), w_ref[k0:k0 + kw, :], (((0,), (0,)), ((), ())),
                                    preferred_element_type=F32)
        else:
            kw = o_ref.shape[1]
            y = y + jnp.dot(o_ref[...].astype(BF16), w_ref[k0:k0 + kw, :], preferred_element_type=F32)
        k0 += kw
    y_ref[...] = y


def _outproj(x2d, parts, w_bf, name, tm=None):
    M, D = x2d.shape
    tm = tm or _tile(M, 512)
    specs = []
    for o in parts:
        if o.ndim == 3:
            npb = o.shape[2] // tm
            specs.append(pl.BlockSpec((1, o.shape[1], tm), lambda i, npb=npb: (i // npb, 0, i % npb)))
        else:
            specs.append(pl.BlockSpec((tm, o.shape[1]), lambda i: (i, 0)))
    return pl.pallas_call(
        _outproj_kernel, grid=(M // tm,),
        in_specs=[pl.BlockSpec((tm, D), lambda i: (i, 0))] + specs + [pl.BlockSpec(w_bf.shape, lambda i: (0, 0))],
        out_specs=pl.BlockSpec((tm, D), lambda i: (i, 0)),
        out_shape=jax.ShapeDtypeStruct((M, D), F32), name=name,
        compiler_params=_cparams(("parallel",)))(x2d, *parts, w_bf)


def _ffn_kernel(x_ref, g_ref, wg_ref, wu_ref, wd_ref, y_ref, xn_ref):
    f = pl.program_id(1)

    @pl.when(f == 0)
    def _():
        x = x_ref[...]
        xn_ref[...] = _rmsnorm(x, g_ref[...]).astype(BF16)
        y_ref[...] = x

    xn = xn_ref[...]
    h = _silu(jnp.dot(xn, wg_ref[...], preferred_element_type=F32)) * \
        jnp.dot(xn, wu_ref[...], preferred_element_type=F32)
    y_ref[...] += jnp.dot(h.astype(BF16), wd_ref[...], preferred_element_type=F32)


def _ffn(x2d, gain, wg, wu, wd, name):
    M, D = x2d.shape
    Fd = wg.shape[1]
    tm = _tile(M, 512)
    nf = 2 if Fd % (2 * LANES) == 0 else 1
    fc = Fd // nf
    return pl.pallas_call(
        _ffn_kernel, grid=(M // tm, nf),
        in_specs=[pl.BlockSpec((tm, D), lambda i, f: (i, 0)), pl.BlockSpec((1, D), lambda i, f: (0, 0)),
                  pl.BlockSpec((D, fc), lambda i, f: (0, f)), pl.BlockSpec((D, fc), lambda i, f: (0, f)),
                  pl.BlockSpec((fc, D), lambda i, f: (f, 0))],
        out_specs=pl.BlockSpec((tm, D), lambda i, f: (i, 0)),
        out_shape=jax.ShapeDtypeStruct((M, D), F32), name=name,
        scratch_shapes=[pltpu.VMEM((tm, D), BF16)],
        compiler_params=_cparams(("parallel", "arbitrary")))(x2d, gain.reshape(1, D), wg, wu, wd)


def _moe_kernel(x_ref, g_ref, wr_ref, wg_ref, wu_ref, wd_ref, gf_ref, y_ref, xn_ref, comb_ref, acc_ref,
                *, n_exp):
    e, f = pl.program_id(1), pl.program_id(2)
    tm = x_ref.shape[0]
    lane = lax.broadcasted_iota(jnp.int32, (tm, LANES), 1)

    @pl.when((e == 0) & (f == 0))
    def _():
        xn = _rmsnorm(x_ref[...], g_ref[...])
        xn_ref[...] = xn.astype(BF16)
        xh, xm, _ = _split3(xn)
        wh, wm, _ = _split3(wr_ref[...])
        d = lambda a, b: jnp.dot(a, b, preferred_element_type=F32)
        logits = jnp.where(lane < n_exp, d(xh, wh) + d(xh, wm) + d(xm, wh), NEG)
        v1 = jnp.max(logits, axis=1, keepdims=True)
        i1 = jnp.min(jnp.where(logits == v1, lane, LANES), axis=1, keepdims=True)
        rest = jnp.where(lane == i1, NEG, logits)
        v2 = jnp.max(rest, axis=1, keepdims=True)
        i2 = jnp.min(jnp.where(rest == v2, lane, LANES), axis=1, keepdims=True)
        ex = jnp.exp(v2 - v1)
        comb_ref[...] = jnp.where(lane == i1, 1.0 / (1.0 + ex), jnp.where(lane == i2, ex / (1.0 + ex), 0.0))
        acc_ref[...] = jnp.zeros_like(acc_ref)

    xn = xn_ref[...]
    h = _silu(jnp.dot(xn, wg_ref[0], preferred_element_type=F32)) * \
        jnp.dot(xn, wu_ref[0], preferred_element_type=F32)
    w = jnp.sum(jnp.where(lane == e, comb_ref[...], 0.0), axis=1, keepdims=True)
    acc_ref[...] += w * jnp.dot(h.astype(BF16), wd_ref[0], preferred_element_type=F32)

    @pl.when((e == n_exp - 1) & (f == pl.num_programs(2) - 1))
    def _():
        y_ref[...] = _rmsnorm(x_ref[...] + acc_ref[...], gf_ref[...])


def _moe_final(x2d, gain, w_router, wg, wu, wd, gain_final, name):
    M, D = x2d.shape
    n_exp, _, Fd = wg.shape
    tm = _tile(M, 512)
    nf = 2 if Fd % (2 * LANES) == 0 else 1
    fc = Fd // nf
    wr = jnp.zeros((D, LANES), F32).at[:, :n_exp].set(w_router)
    return pl.pallas_call(
        functools.partial(_moe_kernel, n_exp=n_exp), grid=(M // tm, n_exp, nf),
        in_specs=[pl.BlockSpec((tm, D), lambda i, e, f: (i, 0)), pl.BlockSpec((1, D), lambda i, e, f: (0, 0)),
                  pl.BlockSpec((D, LANES), lambda i, e, f: (0, 0)),
                  pl.BlockSpec((1, D, fc), lambda i, e, f: (e, 0, f)),
                  pl.BlockSpec((1, D, fc), lambda i, e, f: (e, 0, f)),
                  pl.BlockSpec((1, fc, D), lambda i, e, f: (e, f, 0)),
                  pl.BlockSpec((1, D), lambda i, e, f: (0, 0))],
        out_specs=pl.BlockSpec((tm, D), lambda i, e, f: (i, 0)),
        out_shape=jax.ShapeDtypeStruct((M, D), F32), name=name,
        scratch_shapes=[pltpu.VMEM((tm, D), BF16), pltpu.VMEM((tm, LANES), F32), pltpu.VMEM((tm, D), F32)],
        compiler_params=_cparams(("parallel", "arbitrary", "arbitrary")))(
            x2d, gain.reshape(1, D), wr, wg, wu, wd, gain_final.reshape(1, D))


def _band_kernel(q_ref, ka_ref, kb_ref, va_ref, vb_ref, bias_ref, o_ref, lse_ref, *, span):
    qt = pl.program_id(2)
    tq, W = q_ref.shape[1], q_ref.shape[2]
    nk = span + tq
    rowk = lax.broadcasted_iota(jnp.int32, (nk, tq), 0)
    bias = bias_ref[...] + jnp.where((rowk < span) & (qt == 0), NEG, 0.0)
    row = lax.broadcasted_iota(jnp.int32, (LANES, tq), 0)
    lses = []
    for hp in range(W // LANES):
        lanes = slice(hp * LANES, (hp + 1) * LANES)
        qT = jnp.transpose(q_ref[0, :, lanes]).astype(BF16)
        k = jnp.concatenate([ka_ref[0, :, lanes], kb_ref[0, :, lanes]], axis=0)
        vT = jnp.transpose(jnp.concatenate([va_ref[0, :, lanes], vb_ref[0, :, lanes]], axis=0)).astype(BF16)
        outs = []
        for h in range(2):
            qpad = jnp.where(row // HEAD_DIM == h, qT, jnp.zeros_like(qT))
            s = jnp.dot(k, qpad, preferred_element_type=F32) + bias
            m = jnp.max(s, axis=0, keepdims=True)
            p = jnp.exp(s - m)
            l = jnp.sum(p, axis=0, keepdims=True)
            outs.append(jnp.dot(vT[h * HEAD_DIM:(h + 1) * HEAD_DIM], p.astype(BF16),
                                preferred_element_type=F32) / l)
            lses.append(m + jnp.log(l))
        o_ref[0, :, lanes] = jnp.transpose(jnp.concatenate(outs, axis=0))
    stat = jnp.concatenate(lses + [jnp.zeros((LANES - len(lses), tq), F32)], axis=0)
    lse_ref[0] = jnp.transpose(stat)


def _band_attention(q, k_b, v, dil, span, tq, name):
    B, S, W = q.shape
    L = S // dil
    tq = min(tq, L)
    assert L % tq == 0 and tq % span == 0
    per_tile = tq // span
    view = lambda a: a.reshape(B, L, dil * W)
    i = np.arange(span + tq)[:, None]
    j = np.arange(tq)[None, :]
    bias = jnp.asarray(np.where((j - i + span >= 0) & (j - i + span <= span), 0.0, NEG), F32)
    cur = lambda b, r, t: (b, t, r)
    prev = lambda b, r, t: (b, jnp.maximum(t * per_tile - 1, 0), r)
    o, lse = pl.pallas_call(
        functools.partial(_band_kernel, span=span), grid=(B, dil, L // tq),
        in_specs=[pl.BlockSpec((1, tq, W), cur),
                  pl.BlockSpec((1, span, W), prev), pl.BlockSpec((1, tq, W), cur),
                  pl.BlockSpec((1, span, W), prev), pl.BlockSpec((1, tq, W), cur),
                  pl.BlockSpec(bias.shape, lambda b, r, t: (0, 0))],
        out_specs=[pl.BlockSpec((1, tq, W), cur), pl.BlockSpec((1, tq, LANES), cur)],
        out_shape=[jax.ShapeDtypeStruct((B, L, dil * W), F32), jax.ShapeDtypeStruct((B, L, dil * LANES), F32)],
        name=name, compiler_params=_cparams(("parallel", "parallel", "parallel")))(
            view(q), view(k_b), view(k_b), view(v), view(v), bias)
    return o.reshape(B * S, W), lse.reshape(B * S, LANES)


def _merge_outproj_kernel(*refs, n_groups):
    x_ref = refs[0]
    o_refs, l_refs = refs[1:1 + n_groups], refs[1 + n_groups:1 + 2 * n_groups]
    w_ref, y_ref = refs[-2], refs[-1]
    W = o_refs[0].shape[1]
    lses = [r[...] for r in l_refs]
    m = functools.reduce(jnp.maximum, lses)
    es = [jnp.exp(l - m) for l in lses]
    tot = functools.reduce(lambda a, b: a + b, es)
    expand = (lax.broadcasted_iota(jnp.int32, (LANES, W), 1) // HEAD_DIM
              == lax.broadcasted_iota(jnp.int32, (LANES, W), 0)).astype(BF16)
    mix = None
    for e, o_ref in zip(es, o_refs):
        hi, mid, _ = _split3(e / tot)
        wexp = jnp.dot(hi, expand, preferred_element_type=F32) + jnp.dot(mid, expand, preferred_element_type=F32)
        mix = wexp * o_ref[...] if mix is None else mix + wexp * o_ref[...]
    y_ref[...] = x_ref[...] + jnp.dot(mix.astype(BF16), w_ref[...], preferred_element_type=F32)


def _merge_outproj(x2d, outs, lses, w_bf, name):
    M, D = x2d.shape
    W = outs[0].shape[1]
    tm = _tile(M, 256)
    row = lambda n: pl.BlockSpec((tm, n), lambda i: (i, 0))
    return pl.pallas_call(
        functools.partial(_merge_outproj_kernel, n_groups=len(outs)), grid=(M // tm,),
        in_specs=[row(D)] + [row(W)] * len(outs) + [row(LANES)] * len(lses)
        + [pl.BlockSpec(w_bf.shape, lambda i: (0, 0))],
        out_specs=row(D), out_shape=jax.ShapeDtypeStruct((M, D), F32), name=name,
        compiler_params=_cparams(("parallel",)))(x2d, *outs, *lses, w_bf)


def _head_major(x, n_heads, pad_rows):
    DB, T, _ = x.shape
    y = x.reshape(DB, T, n_heads, HEAD_DIM).transpose(0, 2, 1, 3)
    return jnp.pad(y, ((0, 0), (0, 0), (0, pad_rows - T), (0, 0)))


def _new_cols(x, n_heads):
    DB, T, _ = x.shape
    y = x.reshape(DB, T, n_heads, HEAD_DIM).transpose(0, 2, 3, 1)
    return jnp.pad(y, ((0, 0), (0, 0), (0, 0), (0, NEW_PAD - T)))


def _row_of(col_vec_row):
    return jnp.transpose(jnp.broadcast_to(col_vec_row, (SUBLANES, LANES)))[0:SUBLANES, 0:1]


def _fox_dec_kernel(pt_ref, *refs, n_pg, n_q):
    kts, vts, lfs = refs[:n_pg], refs[n_pg:2 * n_pg], refs[2 * n_pg:3 * n_pg]
    q_ref, knt_ref, vnt_ref, lfn_ref, o_ref, base_ref, carry_ref, m_ref, l_ref, acc_ref = refs[3 * n_pg:]
    p = pl.program_id(1)
    nh = FOX_HEADS
    page = LANES
    lane = lax.broadcasted_iota(jnp.int32, (QPAD, LANES), 1)
    trow = lax.broadcasted_iota(jnp.int32, (QPAD, LANES), 0)
    r = lax.broadcasted_iota(jnp.int32, (page, page), 0)
    c = lax.broadcasted_iota(jnp.int32, (page, page), 1)
    stack = lambda xs: jnp.concatenate(xs, axis=0)

    @pl.when(p == 0)
    def _():
        _online_init(m_ref, l_ref, acc_ref)
        carry_ref[...] = jnp.zeros_like(carry_ref)
        cnew = _mm3_left(lfn_ref[0], r <= c)
        bases, ss = [], []
        for h in range(nh):
            bh = _row_of(cnew[h:h + 1, :])
            bases.append(bh)
            ss.append(_mm(q_ref[0, h] * SCALE, knt_ref[0, h]) + (bh - cnew[h:h + 1, :]))
        base_ref[...] = jnp.broadcast_to(stack(bases), base_ref.shape)
        live = (lane <= trow) & (lane < n_q)
        s = jnp.where(stack([live] * nh), stack(ss), NEG)
        alpha, pr = _online_step(s, m_ref, l_ref)
        pv = stack([_mm_nt(pr[h * QPAD:(h + 1) * QPAD], vnt_ref[0, h]) for h in range(nh)])
        acc_ref[...] = alpha * acc_ref[...] + pv

    lf_all = stack([lf[0] for lf in lfs])
    after_all = _mm3_left(lf_all, r > c)
    tot_all = after_all[:, 0:1] + lf_all[:, 0:1]
    carry = carry_ref[:, 0:1]
    decay = [None] * n_pg
    for j in reversed(range(n_pg)):
        decay[j] = carry + after_all[j * nh:(j + 1) * nh]
        carry = carry + tot_all[j * nh:(j + 1) * nh]
    carry_ref[...] = jnp.broadcast_to(carry, carry_ref.shape)
    ss = []
    for h in range(nh):
        qh = (q_ref[0, h] * SCALE).astype(BF16)
        sh = jnp.concatenate([jnp.dot(qh, kts[j][0, h].astype(BF16), preferred_element_type=F32)
                              + decay[j][h:h + 1, :] for j in range(n_pg)], axis=1)
        ss.append(sh)
    s = stack(ss) + base_ref[:, 0:1]
    alpha, pr = _online_step(s, m_ref, l_ref)
    pvs = []
    for h in range(nh):
        ph = pr[h * QPAD:(h + 1) * QPAD]
        pvs.append(sum(_mm_nt(ph[:, j * page:(j + 1) * page], vts[j][0, h]) for j in range(n_pg)))
    acc_ref[...] = alpha * acc_ref[...] + stack(pvs)

    @pl.when(p == pl.num_programs(1) - 1)
    def _():
        o_ref[0] = acc_ref[...] / l_ref[...]


def _fox_decode(q, k_new, v_new, lf_new, cache_k, cache_v, cache_lf, pt_flat):
    DB, T, W = q.shape
    n_pool, page, nh, _ = cache_k.shape
    assert page == LANES and T <= QPAD
    n_pages = pt_flat.shape[0] // DB
    n_pg = math.gcd(n_pages, PAGES_PER_STEP)
    n_steps = n_pages // n_pg
    ckt = jnp.transpose(cache_k, (0, 2, 3, 1))
    cvt = jnp.transpose(cache_v, (0, 2, 3, 1))
    clf = jnp.transpose(cache_lf, (0, 2, 1))
    qh = _head_major(q, nh, QPAD)
    knt, vnt = _new_cols(k_new, nh), _new_cols(v_new, nh)
    lfn = jnp.pad(jnp.transpose(lf_new, (0, 2, 1)), ((0, 0), (0, 0), (0, LANES - T)))
    page_idx = lambda j: (lambda b, p, pt: (pt[b * n_pages + (n_steps - 1 - p) * n_pg + j], 0, 0, 0))
    lf_idx = lambda j: (lambda b, p, pt: (pt[b * n_pages + (n_steps - 1 - p) * n_pg + j], 0, 0))
    per_db = lambda a: pl.BlockSpec((1,) + a.shape[1:], lambda b, p, pt: (b,) + (0,) * (a.ndim - 1))
    rows = nh * QPAD
    grid_spec = pltpu.PrefetchScalarGridSpec(
        num_scalar_prefetch=1, grid=(DB, n_steps),
        in_specs=([pl.BlockSpec((1, nh, HEAD_DIM, page), page_idx(j)) for j in range(n_pg)] * 2
                  + [pl.BlockSpec((1, nh, page), lf_idx(j)) for j in range(n_pg)]
                  + [per_db(qh), per_db(knt), per_db(vnt), per_db(lfn)]),
        out_specs=pl.BlockSpec((1, rows, HEAD_DIM), lambda b, p, pt: (b, 0, 0)),
        scratch_shapes=[pltpu.VMEM((rows, LANES), F32), pltpu.VMEM((nh, LANES), F32),
                        pltpu.VMEM((rows, 1), F32), pltpu.VMEM((rows, 1), F32), pltpu.VMEM((rows, HEAD_DIM), F32)])
    o = pl.pallas_call(
        functools.partial(_fox_dec_kernel, n_pg=n_pg, n_q=T), grid_spec=grid_spec,
        out_shape=jax.ShapeDtypeStruct((DB, rows, HEAD_DIM), F32), name="fox_decode",
        compiler_params=_cparams(("parallel", "arbitrary")))(
            pt_flat, *([ckt] * n_pg), *([cvt] * n_pg), *([clf] * n_pg), qh, knt, vnt, lfn)
    return o.reshape(DB, nh, QPAD, HEAD_DIM)[:, :, :T].transpose(0, 2, 1, 3).reshape(DB, T, W)


def _nsa_dec_a_kernel(q_ref, kct_ref, vct_ref, kwt_ref, vwt_ref, kwnt_ref, vwnt_ref, gt_ref, map_ref,
                      o_ref, bias_ref, *, n_q, ns, n_sel, past, n_keys):
    nr = NSA_GROUP * QPAD
    n_ch = kct_ref.shape[3]
    wb = kwt_ref.shape[3]
    t_row = lax.broadcasted_iota(jnp.int32, (nr, 1), 0) % QPAD
    imps = []
    for kv in range(NSA_KV_HEADS):
        q = (q_ref[0, kv] * SCALE).astype(BF16)
        cidx = lax.broadcasted_iota(jnp.int32, (nr, n_ch), 1)
        pc = _softmax_rows(_mm(q, kct_ref[0, kv]), cidx >= 1)
        oc = _mm_nt(pc, vct_ref[0, kv])
        pcat = jnp.concatenate([pc[j * QPAD:(j + 1) * QPAD] for j in range(NSA_GROUP)], axis=1)
        imps.append(jnp.dot(pcat.astype(BF16), map_ref[...], preferred_element_type=F32))
        sw = _mm(q, kwt_ref[0, kv])
        sn = _mm(q, kwnt_ref[0, kv])
        iw = lax.broadcasted_iota(jnp.int32, (nr, wb), 1)
        un = lax.broadcasted_iota(jnp.int32, (nr, NEW_PAD), 1)
        mw = (wb + t_row - iw <= NSA_WINDOW) & (past - wb + iw >= 0)
        mn = (un <= t_row) & (un < n_q)
        sw = jnp.where(mw, sw, NEG)
        sn = jnp.where(mn, sn, NEG)
        m = jnp.maximum(jnp.max(sw, axis=1, keepdims=True), jnp.max(sn, axis=1, keepdims=True))
        pw = jnp.where(mw, jnp.exp(sw - m), 0.0)
        pn = jnp.where(mn, jnp.exp(sn - m), 0.0)
        l = jnp.sum(pw, axis=1, keepdims=True) + jnp.sum(pn, axis=1, keepdims=True)
        ow = (_mm_nt(pw, vwt_ref[0, kv]) + _mm_nt(pn, vwnt_ref[0, kv])) / l
        gt = gt_ref[0, kv]
        o_ref[0, kv] = gt[:, 0:1] * oc + gt[:, 2:3] * ow

    imp = jnp.concatenate(imps, axis=0)
    t_sel = lax.broadcasted_iota(jnp.int32, (imp.shape[0], 1), 0) % QPAD
    sel = _select_blocks(imp, past + t_sel, ns, n_sel).astype(BF16)
    ch = 8 * LANES
    for c0 in range(0, n_keys, ch):
        w = min(ch, n_keys - c0)
        key = c0 + lax.broadcasted_iota(jnp.int32, (sel.shape[1], w), 1)
        onehot = (key // NSA_SLC_BLOCK == lax.broadcasted_iota(jnp.int32, (sel.shape[1], w), 0)).astype(BF16)
        chosen = jnp.dot(sel, onehot, preferred_element_type=F32) > 0.5
        kpos = c0 + lax.broadcasted_iota(jnp.int32, (sel.shape[0], w), 1)
        ok = chosen & (kpos <= past + t_sel) & (kpos < past + n_q)
        bias_ref[0, :, c0:c0 + w] = jnp.where(ok, 0.0, NEG)


def _nsa_dec_b_kernel(pt_ref, *refs, n_pg):
    kts, vts = refs[:n_pg], refs[n_pg:2 * n_pg]
    q_ref, knt_ref, vnt_ref, bias_ref, biasn_ref, o_ref, m_ref, l_ref, acc_ref = refs[2 * n_pg:]
    p = pl.program_id(1)
    nr = NSA_GROUP * QPAD
    page = LANES
    stack = lambda xs: jnp.concatenate(xs, axis=0)

    @pl.when(p == 0)
    def _():
        _online_init(m_ref, l_ref, acc_ref)

    def update(s_of, v_of):
        ss = []
        for kv in range(NSA_KV_HEADS):
            ss.append(s_of(kv, (q_ref[0, kv] * SCALE).astype(BF16)))
        alpha, pr = _online_step(stack(ss), m_ref, l_ref)
        acc_ref[...] = alpha * acc_ref[...] + stack([v_of(kv, pr[kv * nr:(kv + 1) * nr])
                                                     for kv in range(NSA_KV_HEADS)])

    def s_pages(kv, q):
        s = jnp.concatenate([jnp.dot(q, kts[j][0, kv].astype(BF16), preferred_element_type=F32)
                             for j in range(n_pg)], axis=1)
        return s + jnp.concatenate([bias_ref[0, kv * QPAD:(kv + 1) * QPAD]] * NSA_GROUP, axis=0)

    def v_pages(kv, pr):
        return sum(_mm_nt(pr[:, j * page:(j + 1) * page], vts[j][0, kv]) for j in range(n_pg))

    update(s_pages, v_pages)

    @pl.when(p == pl.num_programs(1) - 1)
    def _():
        update(lambda kv, q: _mm(q, knt_ref[0, kv])
               + jnp.concatenate([biasn_ref[0, kv * QPAD:(kv + 1) * QPAD]] * NSA_GROUP, axis=0),
               lambda kv, pr: _mm_nt(pr, vnt_ref[0, kv]))
        o_ref[0] = acc_ref[...] / l_ref[...]


def _nsa_decode(qb, kct, vct, cache_ks, cache_vs, ks_new, vs_new, swa_k, swa_v, kw_new, vw_new,
                gates, pt_flat, past):
    DB, T, W = qb.shape
    n_pool, page = cache_ks.shape[:2]
    assert page == LANES and past % LANES == 0 and T <= QPAD
    n_pages = pt_flat.shape[0] // DB
    n_ch = kct.shape[3]
    ns = -(-(past + T) // NSA_SLC_BLOCK)
    ns_pad = -(-ns // LANES) * LANES
    n_sel = min(NSA_TOPN, ns)
    nr = NSA_GROUP * QPAD
    n_keys = past + NEW_PAD
    q5 = _head_major(qb, NSA_HEADS, QPAD).reshape(DB, NSA_KV_HEADS, nr, HEAD_DIM)
    g5 = _head_major(jnp.pad(gates.reshape(DB, T, NSA_HEADS, 3), ((0, 0),) * 3 + ((0, HEAD_DIM - 3),))
                     .reshape(DB, T, NSA_HEADS * HEAD_DIM), NSA_HEADS, QPAD)[..., :3]
    g5 = g5.reshape(DB, NSA_KV_HEADS, nr, 3)
    smap = jnp.asarray(np.tile(_slc_map_rows(n_ch, ns_pad), (NSA_GROUP, 1)), dtype=BF16)
    kwt = jnp.transpose(swa_k, (0, 2, 3, 1))
    vwt = jnp.transpose(swa_v, (0, 2, 3, 1))
    kwnt, vwnt = _new_cols(kw_new, NSA_KV_HEADS), _new_cols(vw_new, NSA_KV_HEADS)
    per = lambda a: pl.BlockSpec((1,) + a.shape[1:], lambda b: (b,) + (0,) * (a.ndim - 1))
    part, bias = pl.pallas_call(
        functools.partial(_nsa_dec_a_kernel, n_q=T, ns=ns, n_sel=n_sel, past=past, n_keys=n_keys), grid=(DB,),
        in_specs=[per(q5), per(kct), per(vct), per(kwt), per(vwt), per(kwnt), per(vwnt), per(g5),
                  pl.BlockSpec(smap.shape, lambda b: (0, 0))],
        out_specs=[pl.BlockSpec((1, NSA_KV_HEADS, nr, HEAD_DIM), lambda b: (b, 0, 0, 0)),
                   pl.BlockSpec((1, NSA_KV_HEADS * QPAD, n_keys), lambda b: (b, 0, 0))],
        out_shape=[jax.ShapeDtypeStruct((DB, NSA_KV_HEADS, nr, HEAD_DIM), F32),
                   jax.ShapeDtypeStruct((DB, NSA_KV_HEADS * QPAD, n_keys), F32)], name="nsa_decode_a",
        compiler_params=_cparams(("parallel",)))(q5, kct, vct, kwt, vwt, kwnt, vwnt, g5, smap)

    n_pg = math.gcd(n_pages, PAGES_PER_STEP)
    ckt = jnp.transpose(cache_ks, (0, 2, 3, 1))
    cvt = jnp.transpose(cache_vs, (0, 2, 3, 1))
    ksnt, vsnt = _new_cols(ks_new, NSA_KV_HEADS), _new_cols(vs_new, NSA_KV_HEADS)
    page_idx = lambda j: (lambda b, p, pt: (pt[b * n_pages + p * n_pg + j], 0, 0, 0))
    per_db = lambda a: pl.BlockSpec((1,) + a.shape[1:], lambda b, p, pt: (b,) + (0,) * (a.ndim - 1))
    rows = NSA_KV_HEADS * nr
    grid_spec = pltpu.PrefetchScalarGridSpec(
        num_scalar_prefetch=1, grid=(DB, n_pages // n_pg),
        in_specs=([pl.BlockSpec((1, NSA_KV_HEADS, HEAD_DIM, page), page_idx(j)) for j in range(n_pg)] * 2
                  + [per_db(q5), per_db(ksnt), per_db(vsnt),
                     pl.BlockSpec((1, NSA_KV_HEADS * QPAD, n_pg * page), lambda b, p, pt: (b, 0, p)),
                     pl.BlockSpec((1, NSA_KV_HEADS * QPAD, NEW_PAD), lambda b, p, pt: (b, 0, past // NEW_PAD))]),
        out_specs=pl.BlockSpec((1, rows, HEAD_DIM), lambda b, p, pt: (b, 0, 0)),
        scratch_shapes=[pltpu.VMEM((rows, 1), F32), pltpu.VMEM((rows, 1), F32), pltpu.VMEM((rows, HEAD_DIM), F32)])
    osl = pl.pallas_call(
        functools.partial(_nsa_dec_b_kernel, n_pg=n_pg), grid_spec=grid_spec,
        out_shape=jax.ShapeDtypeStruct((DB, rows, HEAD_DIM), F32), name="nsa_decode_b",
        compiler_params=_cparams(("parallel", "arbitrary")))(
            pt_flat, *([ckt] * n_pg), *([cvt] * n_pg), q5, ksnt, vsnt, bias, bias)

    o = part + g5[..., 1:2] * osl.reshape(DB, NSA_KV_HEADS, nr, HEAD_DIM)
    return o.reshape(DB, NSA_HEADS, QPAD, HEAD_DIM)[:, :, :T].transpose(0, 2, 1, 3).reshape(DB, T, W)


def _dil_dec_kernel(q_ref, kt_ref, vt_ref, knt_ref, vnt_ref, o_ref, *, n_q, wc):
    hb = kt_ref.shape[1]
    t = lax.broadcasted_iota(jnp.int32, (QPAD, 1), 0)

    def log_mult(d, ok):
        w = jnp.zeros(d.shape, F32)
        for window, dil in DIL_PAIRS:
            w = w + ((d >= 0) & (d <= window) & (d % dil == 0)).astype(F32)
        return jnp.where(ok, w, 0.0)

    wk = log_mult(wc + t - lax.broadcasted_iota(jnp.int32, (QPAD, wc), 1), t < n_q)
    un = lax.broadcasted_iota(jnp.int32, (QPAD, NEW_PAD), 1)
    wn = log_mult(t - un, (t < n_q) & (un < n_q))
    for h in range(hb):
        q = (q_ref[0, h] * SCALE).astype(BF16)
        sk = jnp.where(wk > 0.0, _mm(q, kt_ref[0, h]), NEG)
        sn = jnp.where(wn > 0.0, _mm(q, knt_ref[0, h]), NEG)
        m = jnp.maximum(jnp.max(sk, axis=1, keepdims=True), jnp.max(sn, axis=1, keepdims=True))
        pk = wk * jnp.exp(sk - m)
        pn = wn * jnp.exp(sn - m)
        l = jnp.sum(pk, axis=1, keepdims=True) + jnp.sum(pn, axis=1, keepdims=True)
        o_ref[0, h] = (_mm_nt(pk, vt_ref[0, h]) + _mm_nt(pn, vnt_ref[0, h])) / jnp.where(l > 0.0, l, 1.0)


def _dilated_decode(q, k_new, v_new, cache_k, cache_v):
    DB, T, W = q.shape
    wc, nh = cache_k.shape[1], cache_k.shape[2]
    ckt = jnp.transpose(cache_k, (0, 2, 3, 1))
    cvt = jnp.transpose(cache_v, (0, 2, 3, 1))
    qh = _head_major(q, nh, QPAD)
    knt, vnt = _new_cols(k_new, nh), _new_cols(v_new, nh)
    hb = math.gcd(nh, 4)
    spec = lambda a: pl.BlockSpec((1, hb) + a.shape[2:], lambda b, j: (b, j, 0, 0))
    o = pl.pallas_call(
        functools.partial(_dil_dec_kernel, n_q=T, wc=wc), grid=(DB, nh // hb),
        in_specs=[spec(qh), spec(ckt), spec(cvt), spec(knt), spec(vnt)],
        out_specs=pl.BlockSpec((1, hb, QPAD, HEAD_DIM), lambda b, j: (b, j, 0, 0)),
        out_shape=jax.ShapeDtypeStruct((DB, nh, QPAD, HEAD_DIM), F32), name="dilated_decode",
        compiler_params=_cparams(("parallel", "parallel")))(qh, ckt, cvt, knt, vnt)
    return o[:, :, :T].transpose(0, 2, 1, 3).reshape(DB, T, W)


def kernel(x_prompt, x_sample, cache_a_k, cache_a_v, cache_a_logf, cache_b_cmp_k, cache_b_cmp_v, cache_b_slc_k, cache_b_slc_v, cache_b_swa_k, cache_b_swa_v, cache_c_k, cache_c_v, page_table, norm_mix0, w_in0, fox_bf, nsa_pe_k, nsa_w1_k, nsa_w2_k, nsa_pe_v, nsa_w1_v, nsa_w2_v, w_out0, norm_ffn0, ffn_w_gate, ffn_w_up, ffn_w_down, norm_mix1, w_in1, w_out1, norm_ffn1, moe_router, moe_w_gate, moe_w_up, moe_w_down, norm_final):
    B, S, D = x_prompt.shape
    DB, T, _ = x_sample.shape
    n_pages = page_table.shape[1]
    past = n_pages * cache_a_k.shape[1]
    pt_flat = page_table.reshape(-1).astype(jnp.int32)
    fw = FOX_HEADS * HEAD_DIM
    nw = NSA_HEADS * HEAD_DIM
    kvw = NSA_KV_HEADS * HEAD_DIM

    cuts = np.cumsum([0, fw, fw, fw, FOX_HEADS, nw] + [kvw] * 6 + [3 * NSA_HEADS])
    col = lambda i: w_in0[:, cuts[i]:cuts[i + 1]]
    qa_w, ka_w, va_w, fa_w, qb_w, kc_w, vc_w, ks_w, vs_w, kw_w, vw_w, gb_w = [col(i) for i in range(12)]

    f32_out = ((F32, 1.0),)
    bf16_out = ((BF16, 1.0),)

    def pack(ws, ropes, emits=None):
        widths = [w.shape[1] for w in ws]
        starts = np.concatenate([[0], np.cumsum(widths)[:-1]])
        emits = emits or [f32_out] * len(ws)
        return (jnp.concatenate(ws, axis=1).astype(BF16),
                [(int(s), int(w), r, e) for s, w, r, e in zip(starts, widths, ropes, emits)])

    both_out = ((F32, 1.0), (BF16, 1.0))
    q_out = ((BF16, SCALE),)
    w0r, segs0r = pack([kc_w, vc_w, ka_w, ks_w, kw_w], [True, False, False, True, True],
                       [f32_out] * 2 + [bf16_out] * 3)
    w0c, segs0c = pack([qa_w, ka_w, va_w, qb_w, kc_w, ks_w, kw_w, vc_w, vs_w, vw_w, gb_w],
                       [False, False, False, True, True, True, True, False, False, False, "sigmoid"],
                       [q_out, f32_out, both_out, q_out, f32_out, f32_out, f32_out, f32_out, both_out, both_out,
                        f32_out])
    w0c = w0c.T
    w0s, segs0s = pack([qa_w, ka_w, va_w, qb_w, kc_w, ks_w, kw_w, vc_w, vs_w, vw_w],
                       [False, False, False, True, True, True, True, False, False, False])
    wgate = gb_w.astype(BF16)
    wft = jnp.zeros((16, D), F32).at[:FOX_HEADS].set(fa_w.T).astype(BF16)
    wfr = jnp.zeros((D, LANES), F32).at[:, :FOX_HEADS].set(fa_w).astype(BF16)
    brow = jnp.zeros((1, LANES), F32).at[0, :FOX_HEADS].set(fox_bf)
    logf_args = (wft, fox_bf.reshape(FOX_HEADS, 1).astype(F32), wfr, brow)
    dw = w_in1.shape[1] // 3
    w1 = w_in1.astype(BF16)
    segs1 = [(0, dw, True, f32_out), (dw, dw, True, f32_out), (2 * dw, dw, False, f32_out)]
    w1c = w1[:, dw:].T
    segs1c = [(0, dw, True, f32_out), (dw, dw, False, f32_out)]
    w_out0_b, w_out1_b = w_out0.astype(BF16), w_out1.astype(BF16)
    ffn_g, ffn_u, ffn_d = ffn_w_gate.astype(BF16), ffn_w_up.astype(BF16), ffn_w_down.astype(BF16)
    moe_g, moe_u, moe_d = moe_w_gate.astype(BF16), moe_w_up.astype(BF16), moe_w_down.astype(BF16)
    cmp_k_w = _cmp_weights(nsa_pe_k, nsa_w1_k, nsa_w2_k)
    cmp_v_w = _cmp_weights(nsa_pe_v, nsa_w1_v, nsa_w2_v)

    tab_p = _rope_tables(jnp.arange(S))
    tab_s = _rope_tables(past + jnp.arange(DB * T) % T)
    tm_p = _tile(S, 512)
    npb = S // tm_p
    win_b = min(NSA_WINDOW, S)
    win_c = min(DIL_WINDOW_MAX, S)

    xp = x_prompt.reshape(B * S, D)
    (kc, vc, ka_b, ks_b, kw_b, qat_b, kat, vat, vat_b, qbt_b, kct, kst, kwt, vct, vst, vst_b, vwt, vwt_b,
     gates_t, lft_p, lf_rows) = _project(
        xp, norm_mix0, tab_p, npb, tm_p, w=w0r, row_segs=segs0r, wt=w0c, col_segs=segs0c,
        logf=logf_args, name="proj0_prompt")
    r3 = lambda a: a.reshape(B, S, a.shape[-1])
    heads = lambda a, h: a.reshape(B, h, HEAD_DIM, a.shape[-1])
    kaug, c0 = _fox_prep(r3(ka_b), r3(lf_rows), tm_p)
    o_at = _fox_prompt(qat_b, kaug, vat_b, c0, tm_p)
    kcmp_p = _compress(r3(kc), cmp_k_w, "compress_k_prompt")
    vcmp_p = _compress(r3(vc), cmp_v_w, "compress_v_prompt")
    kv2 = lambda a: heads(a, NSA_KV_HEADS)
    o_bt = _nsa_prompt(qbt_b, kv2(kcmp_p), kv2(vcmp_p), r3(ks_b), vst_b, r3(kw_b), vwt_b, gates_t)
    hp = _outproj(xp, [o_at, o_bt], w_out0_b, "outproj0_prompt", tm=tm_p)
    hp = _ffn(hp, norm_ffn0, ffn_g, ffn_u, ffn_d, "ffn_prompt")

    xs = x_sample.reshape(DB * T, D)
    (qa_s, ka_s, va_s, qb_s, kc_s, ks_s, kw_s, vc_s, vs_s, vw_s, gates_s, lft_s, _) = _project(
        xs, norm_mix0, tab_s, 1, DB * T, w=w0s, row_segs=segs0s, w_gate=wgate, logf=logf_args,
        name="proj0_sample")
    s3 = lambda a: a.reshape(DB, T, a.shape[-1])
    lf_s = jnp.transpose(lft_s[0].reshape(FOX_HEADS, DB, T), (1, 2, 0))
    o_a_s = _fox_decode(s3(qa_s), s3(ka_s), s3(va_s), lf_s, cache_a_k, cache_a_v, cache_a_logf, pt_flat)
    kcmp_s = _compress_paged(cache_b_cmp_k, pt_flat, DB, cmp_k_w, "compress_k_paged")
    vcmp_s = _compress_paged(cache_b_cmp_v, pt_flat, DB, cmp_v_w, "compress_v_paged")
    kvs = lambda a: a.reshape(DB, NSA_KV_HEADS, HEAD_DIM, a.shape[-1])
    o_b_s = _nsa_decode(s3(qb_s), kvs(kcmp_s), kvs(vcmp_s), cache_b_slc_k, cache_b_slc_v, s3(ks_s), s3(vs_s),
                        cache_b_swa_k, cache_b_swa_v, s3(kw_s), s3(vw_s), s3(gates_s), pt_flat, past)
    hs = _outproj(xs, [o_a_s.reshape(DB * T, fw), o_b_s.reshape(DB * T, nw)], w_out0_b, "outproj0_sample")
    hs = _ffn(hs, norm_ffn0, ffn_g, ffn_u, ffn_d, "ffn_sample")

    first_c = (S - win_c) // tm_p
    segs1p = [(0, dw, True, ((F32, SCALE),)), (dw, dw, True, bf16_out), (2 * dw, dw, False, f32_out)]
    q1, k1_b, v1, k1t, v1t = _project(hp, norm_mix1, tab_p, npb, tm_p, w=w1, row_segs=segs1p, wt=w1c,
                                      col_segs=segs1c, col_from=(npb, first_c), name="proj1_prompt")
    groups = [_band_attention(q1.reshape(B, S, dw), k1_b.reshape(B, S, dw), v1.reshape(B, S, dw), dil,
                              window // dil, 256, "dilated_prompt_d%d" % dil) for window, dil in DIL_PAIRS]
    hp = _merge_outproj(hp, [g[0] for g in groups], [g[1] for g in groups], w_out1_b, "outproj1_prompt")
    y_prompt = _moe_final(hp, norm_ffn1, moe_router, moe_g, moe_u, moe_d, norm_final, "moe_prompt").reshape(B, S, D)

    q1s, k1s, v1s = _project(hs, norm_mix1, tab_s, 1, DB * T, w=w1, row_segs=segs1, name="proj1_sample")
    o1s = _dilated_decode(s3(q1s), s3(k1s), s3(v1s), cache_c_k, cache_c_v)
    hs = _outproj(hs, [o1s.reshape(DB * T, dw)], w_out1_b, "outproj1_sample")
    y_sample = _moe_final(hs, norm_ffn1, moe_router, moe_g, moe_u, moe_d, norm_final, "moe_sample").reshape(DB, T, D)

    def state(a, h, last=None):
        a = a.reshape(a.shape[0], h, HEAD_DIM, a.shape[-1])
        if last is not None:
            a = a[..., a.shape[-1] - last:]
        return jnp.transpose(a, (0, 3, 1, 2))

    h4 = lambda a, h: a.reshape(DB, T, h, HEAD_DIM)
    nh1 = dw // HEAD_DIM
    return (y_prompt, y_sample,
            state(kat, FOX_HEADS), state(vat, FOX_HEADS), jnp.transpose(lft_p, (0, 2, 1)),
            state(kct, NSA_KV_HEADS), state(vct, NSA_KV_HEADS), state(kst, NSA_KV_HEADS), state(vst, NSA_KV_HEADS),
            state(kwt, NSA_KV_HEADS, win_b), state(vwt, NSA_KV_HEADS, win_b),
            state(k1t, nh1, win_c), state(v1t, nh1, win_c),
            h4(ka_s, FOX_HEADS), h4(va_s, FOX_HEADS), lf_s,
            h4(kc_s, NSA_KV_HEADS), h4(vc_s, NSA_KV_HEADS), h4(ks_s, NSA_KV_HEADS),
            h4(vs_s, NSA_KV_HEADS), h4(kw_s, NSA_KV_HEADS), h4(vw_s, NSA_KV_HEADS),
            h4(k1s, nh1), h4(v1s, nh1))
```

```python
import functools
import math

import numpy as np
import jax
import jax.numpy as jnp
from jax import lax
from jax.experimental import pallas as pl
from jax.experimental.pallas import tpu as pltpu

F32 = jnp.float32
BF16 = jnp.bfloat16

HEAD_DIM = 64
HALF = HEAD_DIM // 2
LANES = 128
SUBLANES = 8
ROPE_THETA = 10000.0
RMS_EPS = 1e-6
NEG = -1e30
MASK_BIG = 30000.0
SCALE = HEAD_DIM ** -0.5

FOX_HEADS = 8
NSA_HEADS = 8
NSA_KV_HEADS = 2
NSA_GROUP = NSA_HEADS // NSA_KV_HEADS
NSA_CMP_LEN = 32
NSA_CMP_STRIDE = 16
NSA_SLC_BLOCK = 64
NSA_TOPN = 16
NSA_WINDOW = 512
NSA_FORCE_BONUS = 1e3
DIL_PAIRS = ((128, 1), (512, 4), (2048, 16))
DIL_WINDOW_MAX = 2048
TOP_K = 2
QPAD = SUBLANES
NEW_PAD = LANES
PAGES_PER_STEP = 8

VMEM_LIMIT = 56 * 1024 * 1024


def _tile(n, pref):
    return pref if n % pref == 0 else n


def _cparams(sem):
    return pltpu.CompilerParams(dimension_semantics=sem, vmem_limit_bytes=VMEM_LIMIT)


def _mm(a, b):
    return jnp.dot(a.astype(BF16), b.astype(BF16), preferred_element_type=F32)


def _mm_nt(a, b):
    return lax.dot_general(a.astype(BF16), b.astype(BF16), (((1,), (1,)), ((), ())),
                           preferred_element_type=F32)


def _split3(x):
    hi = x.astype(BF16)
    r = x - hi.astype(F32)
    mid = r.astype(BF16)
    lo = (r - mid.astype(F32)).astype(BF16)
    return hi, mid, lo


def _mm3_left(x, exact_rhs):
    b = exact_rhs.astype(BF16)
    hi, mid, lo = _split3(x)
    d = lambda p: jnp.dot(p, b, preferred_element_type=F32)
    return d(hi) + d(mid) + d(lo)


def _sigmoid(z):
    return 1.0 / (1.0 + jnp.exp(-z))


def _silu(z):
    return z * _sigmoid(z)


def _log_sigmoid(z):
    return jnp.minimum(z, 0.0) - jnp.log1p(jnp.exp(-jnp.abs(z)))


def _rmsnorm(x, g):
    return x * lax.rsqrt(jnp.mean(x * x, axis=-1, keepdims=True) + RMS_EPS) * g


def _rope_rows(y, cos, sin_signed):
    n = y.shape[1]
    lane = lax.broadcasted_iota(jnp.int32, y.shape, 1)
    first = (lane % HEAD_DIM) < HALF
    rot = jnp.where(first, pltpu.roll(y, n - HALF, 1), pltpu.roll(y, HALF, 1))
    reps = n // LANES
    if reps > 1:
        cos = jnp.concatenate([cos] * reps, axis=1)
        sin_signed = jnp.concatenate([sin_signed] * reps, axis=1)
    return y * cos + rot * sin_signed


def _rope_cols(yt, cos_t, sin_t):
    out = []
    for h in range(yt.shape[0] // HEAD_DIM):
        a = yt[h * HEAD_DIM:h * HEAD_DIM + HALF]
        b = yt[h * HEAD_DIM + HALF:(h + 1) * HEAD_DIM]
        out += [a * cos_t - b * sin_t, b * cos_t + a * sin_t]
    return jnp.concatenate(out, axis=0)


def _rope_tables(pos):
    inv = jnp.exp(-math.log(ROPE_THETA) * jnp.arange(HALF, dtype=F32) / HALF)
    ang = pos.astype(F32)[:, None] * inv[None, :]
    cos, sin = jnp.cos(ang), jnp.sin(ang)
    return (jnp.concatenate([cos, cos, cos, cos], axis=1),
            jnp.concatenate([-sin, sin, -sin, sin], axis=1), cos.T, sin.T)


def _softmax_rows(s, mask):
    sm = jnp.where(mask, s, NEG)
    m = jnp.max(sm, axis=1, keepdims=True)
    p = jnp.where(mask, jnp.exp(sm - m), 0.0)
    l = jnp.sum(p, axis=1, keepdims=True)
    return p / jnp.where(l > 0.0, l, 1.0)


def _online_step(s, m_ref, l_ref):
    m_prev = m_ref[...]
    m_new = jnp.maximum(m_prev, jnp.max(s, axis=1, keepdims=True))
    alpha = jnp.exp(m_prev - m_new)
    p = jnp.exp(s - m_new)
    l_ref[...] = alpha * l_ref[...] + jnp.sum(p, axis=1, keepdims=True)
    m_ref[...] = m_new
    return alpha, p


def _online_init(m_ref, l_ref, acc_ref):
    m_ref[...] = jnp.full(m_ref.shape, NEG, F32)
    l_ref[...] = jnp.zeros_like(l_ref)
    acc_ref[...] = jnp.zeros_like(acc_ref)


def _proj_kernel(*refs, row_segs, col_segs, n_gate, with_logf, col_from):
    it = iter(refs)
    x_ref, g_ref, cos_ref, sin_ref, cost_ref, sint_ref = [next(it) for _ in range(6)]
    w_ref = next(it) if row_segs else None
    wt_ref = next(it) if col_segs else None
    wg_ref = next(it) if n_gate else None
    if with_logf:
        wft_ref, bcol_ref, wfr_ref, brow_ref = next(it), next(it), next(it), next(it)
    outs = list(it)
    xn = _rmsnorm(x_ref[...], g_ref[...]).astype(BF16)
    k = 0
    for c0, width, rope, emits in row_segs:
        y = jnp.dot(xn, w_ref[:, c0:c0 + width], preferred_element_type=F32)
        if rope:
            y = _rope_rows(y, cos_ref[...], sin_ref[...])
        for dtype, scale in emits:
            outs[k][...] = (y if scale == 1.0 else y * scale).astype(dtype)
            k += 1

    def cols():
        kk = k
        for r0, height, rope, emits in col_segs:
            yt = _mm_nt(wt_ref[r0:r0 + height, :], xn)
            if rope == "sigmoid":
                yt = _sigmoid(yt)
            elif rope:
                yt = _rope_cols(yt, cost_ref[...], sint_ref[...])
            for dtype, scale in emits:
                outs[kk][0] = (yt if scale == 1.0 else yt * scale).astype(dtype)
                kk += 1

    if col_segs:
        if col_from:
            pl.when(pl.program_id(0) % col_from[0] >= col_from[1])(cols)
        else:
            cols()
        k += sum(len(e) for _, _, _, e in col_segs)
    if n_gate:
        outs[k][...] = _sigmoid(jnp.dot(xn, wg_ref[...], preferred_element_type=F32))
        k += 1
    if with_logf:
        yt = _mm_nt(wft_ref[...], xn)
        outs[k][0] = _log_sigmoid(yt[0:FOX_HEADS] + bcol_ref[...])
        outs[k + 1][...] = _log_sigmoid(jnp.dot(xn, wfr_ref[...], preferred_element_type=F32) + brow_ref[...])


def _project(x2d, gain, tables, n_pos_blocks, tm, w=None, row_segs=(), wt=None, col_segs=(),
             w_gate=None, logf=None, col_from=None, name="proj"):
    M, D = x2d.shape
    nt = M // tm
    n_seq = nt // n_pos_blocks
    cos_t, sin_t, cos_c, sin_c = tables
    pos_map = lambda i: (i % n_pos_blocks, 0)
    posc_map = lambda i: (0, i % n_pos_blocks)
    const = lambda a: pl.BlockSpec(a.shape, lambda i: (0,) * a.ndim)
    in_specs = [pl.BlockSpec((tm, D), lambda i: (i, 0)), const(gain.reshape(1, D)),
                pl.BlockSpec((tm, LANES), pos_map), pl.BlockSpec((tm, LANES), pos_map),
                pl.BlockSpec((HALF, tm), posc_map), pl.BlockSpec((HALF, tm), posc_map)]
    args = [x2d, gain.reshape(1, D), cos_t, sin_t, cos_c, sin_c]
    for a in (w, wt, w_gate):
        if a is not None:
            in_specs.append(const(a))
            args.append(a)
    out_shape, out_specs = [], []
    for _, wd, _, emits in row_segs:
        for dtype, _ in emits:
            out_shape.append(jax.ShapeDtypeStruct((M, wd), dtype))
            out_specs.append(pl.BlockSpec((tm, wd), lambda i: (i, 0)))
    first = col_from[1] if col_from else 0
    n_cb = n_pos_blocks - first
    col_map = lambda i: (i // n_pos_blocks, 0, jnp.maximum(i % n_pos_blocks - first, 0))
    for _, ht, _, emits in col_segs:
        for dtype, _ in emits:
            out_shape.append(jax.ShapeDtypeStruct((n_seq, ht, n_cb * tm), dtype))
            out_specs.append(pl.BlockSpec((1, ht, tm), col_map))
    n_gate = 0
    if w_gate is not None:
        n_gate = w_gate.shape[1]
        out_shape.append(jax.ShapeDtypeStruct((M, n_gate), F32))
        out_specs.append(pl.BlockSpec((tm, n_gate), lambda i: (i, 0)))
    if logf is not None:
        in_specs += [const(a) for a in logf]
        args += list(logf)
        out_shape.append(jax.ShapeDtypeStruct((n_seq, FOX_HEADS, n_pos_blocks * tm), F32))
        out_specs.append(pl.BlockSpec((1, FOX_HEADS, tm), lambda i: (i // n_pos_blocks, 0, i % n_pos_blocks)))
        out_shape.append(jax.ShapeDtypeStruct((M, LANES), F32))
        out_specs.append(pl.BlockSpec((tm, LANES), lambda i: (i, 0)))
    return pl.pallas_call(
        functools.partial(_proj_kernel, row_segs=tuple(row_segs), col_segs=tuple(col_segs), n_gate=n_gate,
                          with_logf=logf is not None, col_from=col_from),
        grid=(nt,), in_specs=in_specs, out_specs=out_specs, out_shape=out_shape, name=name,
        compiler_params=_cparams(("arbitrary",)))(*args)


def _fox_prep_kernel(k_ref, lf_ref, kaug_ref, c0_ref, carry_ref):
    j = pl.program_id(1)
    tc = k_ref.shape[1]

    @pl.when(j == 0)
    def _():
        carry_ref[...] = jnp.zeros_like(carry_ref)
        c0_ref[...] = jnp.zeros_like(c0_ref)

    lane = lax.broadcasted_iota(jnp.int32, (FOX_HEADS, LANES), 1)
    start = jnp.transpose(carry_ref[...])[0:FOX_HEADS, 0:1]
    c0_ref[0] = jnp.where(lane == j, start, c0_ref[0])

    r = lax.broadcasted_iota(jnp.int32, (tc, tc), 0)
    c = lax.broadcasted_iota(jnp.int32, (tc, tc), 1)
    tri = (c <= r).astype(BF16)
    hi, mid, lo = _split3(lf_ref[0])
    d = lambda p: jnp.dot(tri, p, preferred_element_type=F32)
    local = d(hi) + d(mid) + d(lo)
    carry_ref[...] = carry_ref[...] + jnp.broadcast_to(local[tc - 1:tc, :], carry_ref.shape)
    parts = jnp.concatenate(_split3(-local), axis=1)
    k = k_ref[0]
    kr = lax.broadcasted_iota(jnp.int32, (k.shape[1], LANES), 0)
    kc = lax.broadcasted_iota(jnp.int32, (k.shape[1], LANES), 1)
    pr = lax.broadcasted_iota(jnp.int32, (3 * LANES, LANES), 0)
    pc = lax.broadcasted_iota(jnp.int32, (3 * LANES, LANES), 1)
    for h in range(FOX_HEADS):
        place_k = ((kr == HEAD_DIM * h + kc) & (kc < HEAD_DIM)).astype(BF16)
        place_c = ((pr % LANES == h) & (pc == HEAD_DIM + pr // LANES)).astype(BF16)
        kaug_ref[0, h] = (jnp.dot(k, place_k, preferred_element_type=F32)
                          + jnp.dot(parts, place_c, preferred_element_type=F32)).astype(BF16)


def _fox_prep(k_b, lf_rows, tc):
    B, S, W = k_b.shape
    assert S // tc <= LANES
    return pl.pallas_call(
        _fox_prep_kernel, grid=(B, S // tc),
        in_specs=[pl.BlockSpec((1, tc, W), lambda b, j: (b, j, 0)),
                  pl.BlockSpec((1, tc, LANES), lambda b, j: (b, j, 0))],
        out_specs=[pl.BlockSpec((1, FOX_HEADS, tc, LANES), lambda b, j: (b, 0, j, 0)),
                   pl.BlockSpec((1, FOX_HEADS, LANES), lambda b, j: (b, 0, 0))],
        out_shape=[jax.ShapeDtypeStruct((B, FOX_HEADS, S, LANES), BF16),
                   jax.ShapeDtypeStruct((B, FOX_HEADS, LANES), F32)], name="fox_prep",
        scratch_shapes=[pltpu.VMEM((SUBLANES, LANES), F32)],
        compiler_params=_cparams(("parallel", "arbitrary")))(k_b, lf_rows)


def _fox_kernel(qi_ref, ki_ref, qt_ref, kaug_ref, vt_ref, c0_ref, o_ref, qa_ref, m_ref, l_ref, acc_ref):
    hp, step = pl.program_id(1), pl.program_id(2)
    qi, ki = qi_ref[step], ki_ref[step]
    tq, tk = qt_ref.shape[2], kaug_ref.shape[2]
    lane1 = lax.broadcasted_iota(jnp.int32, (1, LANES), 1)

    @pl.when(ki == 0)
    def _():
        row = lax.broadcasted_iota(jnp.int32, (HEAD_DIM, tq), 0)
        ones = jnp.where(row < 3, 1.0, 0.0).astype(BF16)
        for h in range(2):
            qa_ref[h] = jnp.concatenate([qt_ref[0, h * HEAD_DIM:(h + 1) * HEAD_DIM, :], ones], axis=0)
        _online_init(m_ref, l_ref, acc_ref)

    def tile(diagonal):
        if diagonal:
            live = lax.broadcasted_iota(jnp.int32, (tk, tq), 0) <= lax.broadcasted_iota(jnp.int32, (tk, tq), 1)
        for h in range(2):
            c0 = c0_ref[0, pl.ds(2 * hp + h, 1), :]
            delta = jnp.sum(jnp.where(lane1 == qi, c0, 0.0) - jnp.where(lane1 == ki, c0, 0.0),
                            axis=1, keepdims=True)
            s = jnp.dot(kaug_ref[0, h], qa_ref[h], preferred_element_type=F32)
            if diagonal:
                s = jnp.where(live, s, NEG)
            m_prev = m_ref[h]
            m_new = jnp.maximum(m_prev, jnp.max(s, axis=0, keepdims=True) + delta)
            p = jnp.exp(s - (m_new - delta))
            alpha = jnp.exp(m_prev - m_new)
            l_ref[h] = alpha * l_ref[h] + jnp.sum(p, axis=0, keepdims=True)
            acc_ref[h] = alpha * acc_ref[h] + jnp.dot(vt_ref[0, h * HEAD_DIM:(h + 1) * HEAD_DIM, :],
                                                      p.astype(BF16), preferred_element_type=F32)
            m_ref[h] = m_new

    pl.when(ki < qi)(lambda: tile(False))

    @pl.when(ki == qi)
    def _():
        tile(True)
        for h in range(2):
            o_ref[0, h * HEAD_DIM:(h + 1) * HEAD_DIM, :] = (acc_ref[h] / l_ref[h]).astype(BF16)


def _fox_prompt(qt_b, kaug, vt_b, c0, tq):
    B, W, S = qt_b.shape
    nq = S // tq
    pairs = [(q, k) for q in range(nq) for k in range(q + 1)]
    qi_tab = jnp.asarray([p[0] for p in pairs], jnp.int32)
    ki_tab = jnp.asarray([p[1] for p in pairs], jnp.int32)
    grid_spec = pltpu.PrefetchScalarGridSpec(
        num_scalar_prefetch=2, grid=(B, W // LANES, len(pairs)),
        in_specs=[pl.BlockSpec((1, LANES, tq), lambda b, hp, s, qi, ki: (b, hp, qi[s])),
                  pl.BlockSpec((1, 2, tq, LANES), lambda b, hp, s, qi, ki: (b, hp, ki[s], 0)),
                  pl.BlockSpec((1, LANES, tq), lambda b, hp, s, qi, ki: (b, hp, ki[s])),
                  pl.BlockSpec((1, FOX_HEADS, LANES), lambda b, hp, s, qi, ki: (b, 0, 0))],
        out_specs=pl.BlockSpec((1, LANES, tq), lambda b, hp, s, qi, ki: (b, hp, qi[s])),
        scratch_shapes=[pltpu.VMEM((2, LANES, tq), BF16), pltpu.VMEM((2, 1, tq), F32),
                        pltpu.VMEM((2, 1, tq), F32), pltpu.VMEM((2, HEAD_DIM, tq), F32)])
    return pl.pallas_call(
        _fox_kernel, grid_spec=grid_spec, out_shape=jax.ShapeDtypeStruct(qt_b.shape, BF16), name="fox_prompt",
        compiler_params=_cparams(("parallel", "parallel", "arbitrary")))(qi_tab, ki_tab, qt_b, kaug, vt_b, c0)


def _cmp_compute(x, pea_ref, peb_ref, wa_ref, wb_ref, w2t_ref, o_ref, carry_ref):
    n = x.shape[0]
    a = jnp.dot((x + pea_ref[...]).astype(BF16), wa_ref[...], preferred_element_type=F32)
    b = jnp.dot((x + peb_ref[...]).astype(BF16), wb_ref[...], preferred_element_type=F32)
    rowi = lax.broadcasted_iota(jnp.int32, a.shape, 0)
    a_prev = jnp.where(rowi == 0, carry_ref[0:1, :], pltpu.roll(a, 1, 0))
    carry_ref[...] = jnp.broadcast_to(a[n - 1:n, :], carry_ref.shape)
    o_ref[0] = _mm_nt(w2t_ref[...], _silu(a_prev + b))


def _cmp_kernel(x_ref, pea_ref, peb_ref, wa_ref, wb_ref, w2t_ref, o_ref, carry_ref):
    @pl.when(pl.program_id(1) == 0)
    def _():
        carry_ref[...] = jnp.zeros_like(carry_ref)
    _cmp_compute(x_ref[0], pea_ref, peb_ref, wa_ref, wb_ref, w2t_ref, o_ref, carry_ref)


def _cmp_paged_kernel(pt_ref, *refs, n_pg):
    pages = refs[:n_pg]
    pea_ref, peb_ref, wa_ref, wb_ref, w2t_ref, o_ref, carry_ref = refs[n_pg:]

    @pl.when(pl.program_id(1) == 0)
    def _():
        carry_ref[...] = jnp.zeros_like(carry_ref)

    x = jnp.concatenate([r[0] for r in pages], axis=0)
    _cmp_compute(x, pea_ref, peb_ref, wa_ref, wb_ref, w2t_ref, o_ref, carry_ref)


def _cmp_weights(pe, w1, w2):
    eye = jnp.eye(NSA_KV_HEADS, dtype=F32)
    hid = w1.shape[2]
    half = NSA_CMP_STRIDE

    def wpart(w):
        return jnp.einsum('lde,hg->lhdge', w, eye).reshape(half * LANES, NSA_KV_HEADS * hid).astype(BF16)

    def ppart(p):
        return jnp.broadcast_to(p[:, None, :], (half, NSA_KV_HEADS, HEAD_DIM)).reshape(1, half * LANES)

    w2t = jnp.einsum('ed,hg->gdhe', w2, eye).reshape(LANES, NSA_KV_HEADS * hid).astype(BF16)
    return ppart(pe[:half]), ppart(pe[half:]), wpart(w1[:half]), wpart(w1[half:]), w2t


def _compress(x, weights, name):
    N, L, _ = x.shape
    n_ch = L // NSA_CMP_STRIDE
    xc = x[:, :n_ch * NSA_CMP_STRIDE].reshape(N, n_ch, NSA_CMP_STRIDE * LANES)
    tch = _tile(n_ch, 256)
    wspecs = [pl.BlockSpec(w.shape, lambda n, j: (0, 0)) for w in weights]
    return pl.pallas_call(
        _cmp_kernel, grid=(N, n_ch // tch),
        in_specs=[pl.BlockSpec((1, tch, xc.shape[2]), lambda n, j: (n, j, 0))] + wspecs,
        out_specs=pl.BlockSpec((1, LANES, tch), lambda n, j: (n, 0, j)),
        out_shape=jax.ShapeDtypeStruct((N, LANES, n_ch), F32), name=name,
        scratch_shapes=[pltpu.VMEM((8, weights[2].shape[1]), F32)],
        compiler_params=_cparams(("parallel", "arbitrary")))(xc, *weights)


def _compress_paged(cache, pt_flat, n_db, weights, name):
    n_pool, page = cache.shape[:2]
    rows = page // NSA_CMP_STRIDE
    xc = cache.reshape(n_pool, rows, NSA_CMP_STRIDE * LANES)
    n_pages = pt_flat.shape[0] // n_db
    n_pg = math.gcd(n_pages, max(1, 256 // rows))
    wspecs = [pl.BlockSpec(w.shape, lambda b, p, pt: (0, 0)) for w in weights]
    page_spec = lambda j: pl.BlockSpec((1, rows, xc.shape[2]),
                                       lambda b, p, pt: (pt[b * n_pages + p * n_pg + j], 0, 0))
    grid_spec = pltpu.PrefetchScalarGridSpec(
        num_scalar_prefetch=1, grid=(n_db, n_pages // n_pg),
        in_specs=[page_spec(j) for j in range(n_pg)] + wspecs,
        out_specs=pl.BlockSpec((1, LANES, n_pg * rows), lambda b, p, pt: (b, 0, p)),
        scratch_shapes=[pltpu.VMEM((8, weights[2].shape[1]), F32)])
    return pl.pallas_call(
        functools.partial(_cmp_paged_kernel, n_pg=n_pg), grid_spec=grid_spec,
        out_shape=jax.ShapeDtypeStruct((n_db, LANES, n_pages * rows), F32), name=name,
        compiler_params=_cparams(("parallel", "arbitrary")))(pt_flat, *([xc] * n_pg), *weights)


def _slc_map_rows(n_ch, ns_pad):
    i = (np.arange(n_ch)[:, None] - 1) * NSA_CMP_STRIDE
    j = np.arange(ns_pad)[None, :] * NSA_SLC_BLOCK
    shared = np.minimum(i + NSA_CMP_LEN, j + NSA_SLC_BLOCK) - np.maximum(i, j)
    m = np.clip(shared, 0, None) / NSA_CMP_LEN
    m[0, :] = 0.0
    return m.astype(np.float32)


def _select_blocks(imp, qpos, ns, n_sel):
    blk = lax.broadcasted_iota(jnp.int32, imp.shape, 1)
    cur = qpos // NSA_SLC_BLOCK
    valid = blk * NSA_SLC_BLOCK <= qpos
    forced = (blk == 0) | (blk == cur) | (blk == cur - 1)
    score = jnp.where(valid, imp + jnp.where(forced, NSA_FORCE_BONUS, 0.0), NEG)
    rank = jnp.zeros(imp.shape, jnp.int32)
    for jp in range(ns):
        sj = score[:, jp:jp + 1]
        beats = (sj > score) | ((sj == score) & (blk > jp))
        rank = rank + beats.astype(jnp.int32)
    return rank < n_sel


def _select_blocks_cols(imp, qpos, ns, n_sel):
    blk = lax.broadcasted_iota(jnp.int32, imp.shape, 0)
    cur = qpos // NSA_SLC_BLOCK
    valid = blk * NSA_SLC_BLOCK <= qpos
    forced = (blk == 0) | (blk == cur) | (blk == cur - 1)
    score = jnp.where(valid, imp + jnp.where(forced, NSA_FORCE_BONUS, 0.0), NEG)
    rank = jnp.zeros(imp.shape, jnp.int32)
    for jp in range(ns):
        sj = score[jp:jp + 1, :]
        beats = (sj > score) | ((sj == score) & (blk > jp))
        rank = rank + beats.astype(jnp.int32)
    return rank < n_sel


def _softmax_cols(s, mask):
    sm = jnp.where(mask, s, NEG)
    m = jnp.max(sm, axis=0, keepdims=True)
    p = jnp.where(mask, jnp.exp(sm - m), 0.0)
    l = jnp.sum(p, axis=0, keepdims=True)
    return p / jnp.where(l > 0.0, l, 1.0)


def _nsa_kernel(qt_ref, kct_ref, vct_ref, ks_ref, vst_ref, kw_ref, vwt_ref, gt_ref, mapt_ref, o_ref,
                qaug_ref, negm_ref, m_ref, l_ref, acc_ref, *, ns, n_sel, tk, win):
    g, i = pl.program_id(1), pl.program_id(2)
    tq = qt_ref.shape[2]
    nl = NSA_GROUP * tq
    n_ch = kct_ref.shape[3]
    per_tile = tk // NSA_SLC_BLOCK
    st = i * tq
    qpos1 = st + lax.broadcasted_iota(jnp.int32, (1, tq), 1)
    rep = lambda a: jnp.concatenate([a] * NSA_GROUP, axis=1)
    qpos = rep(qpos1)

    q4t = jnp.concatenate([qt_ref[0, j * HEAD_DIM:(j + 1) * HEAD_DIM, :] for j in range(NSA_GROUP)], axis=1)
    row = lax.broadcasted_iota(jnp.int32, (LANES, nl), 0)
    qaug_ref[0:LANES, :] = jnp.where(row // HEAD_DIM == g, jnp.concatenate([q4t, q4t], axis=0),
                                     jnp.zeros((LANES, nl), BF16))
    qaug_ref[LANES:2 * LANES, :] = jnp.zeros((LANES, nl), BF16)

    cidx = lax.broadcasted_iota(jnp.int32, (n_ch, tq), 0)
    cmask = rep(((cidx - 1) * NSA_CMP_STRIDE + NSA_CMP_LEN - 1 <= qpos1) & (cidx >= 1))
    sc = lax.dot_general(kct_ref[0, 0].astype(BF16), q4t, (((0,), (0,)), ((), ())), preferred_element_type=F32)
    pc = _softmax_cols(sc, cmask)
    oc = _mm(vct_ref[0, 0], pc)
    pstack = jnp.concatenate([pc[:, j * tq:(j + 1) * tq] for j in range(NSA_GROUP)], axis=0)
    imp = jnp.dot(mapt_ref[...], pstack.astype(BF16), preferred_element_type=F32)[0:HEAD_DIM]
    sel = _select_blocks_cols(imp, qpos1, ns, n_sel)

    negm_ref[...] = rep(jnp.where(sel, 0.0, -MASK_BIG))
    _online_init(m_ref, l_ref, acc_ref)
    onehot = ((lax.broadcasted_iota(jnp.int32, (tk, LANES), 0) // NSA_SLC_BLOCK)
              == lax.broadcasted_iota(jnp.int32, (tk, LANES), 1)).astype(BF16)

    def tile(kt, diagonal):
        k0 = pl.multiple_of(kt * tk, tk)
        nm = negm_ref[pl.ds(pl.multiple_of(kt * per_tile, per_tile), per_tile), :]
        qaug_ref[LANES:LANES + 16, :] = jnp.concatenate(
            [nm, jnp.zeros((16 - per_tile, nl), F32)], axis=0).astype(BF16)
        kaug = jnp.concatenate([ks_ref[0, pl.ds(k0, tk), :], onehot], axis=1)
        s = jnp.dot(kaug, qaug_ref[...], preferred_element_type=F32)
        if diagonal:
            kpos = k0 + lax.broadcasted_iota(jnp.int32, (tk, nl), 0)
            s = jnp.where(kpos <= qpos, s, NEG)
        m_prev = m_ref[...]
        m_new = jnp.maximum(m_prev, jnp.max(s, axis=0, keepdims=True))
        alpha = jnp.exp(m_prev - m_new)
        p = jnp.exp(s - m_new)
        l_ref[...] = alpha * l_ref[...] + jnp.sum(p, axis=0, keepdims=True)
        acc_ref[...] = alpha * acc_ref[...] + jnp.dot(vst_ref[0, :, pl.ds(k0, tk)], p.astype(BF16),
                                                      preferred_element_type=F32)
        m_ref[...] = m_new

    last = st // tk

    def body(kt, carry):
        tile(kt, False)
        return carry

    lax.fori_loop(0, last, body, 0)
    tile(last, True)
    osl = acc_ref[...] / l_ref[...]

    w0 = pl.multiple_of(jnp.maximum(st + tq - win, 0), tq)
    dist = qpos1 - (w0 + lax.broadcasted_iota(jnp.int32, (win, tq), 0))
    sw = jnp.dot(kw_ref[0, pl.ds(w0, win), :], qaug_ref[0:LANES, :], preferred_element_type=F32)
    pw = _softmax_cols(sw, rep((dist >= 0) & (dist <= NSA_WINDOW)))
    ow = jnp.dot(vwt_ref[0, :, pl.ds(w0, win)], pw.astype(BF16), preferred_element_type=F32)

    for j in range(NSA_GROUP):
        base = (NSA_GROUP * g + j) * 3
        gate = lambda r: gt_ref[0, pl.ds(base + r, 1), :]
        cols = slice(j * tq, (j + 1) * tq)
        o = gate(0) * oc[:, cols] + gate(1) * osl[:, cols] + gate(2) * ow[:, cols]
        o_ref[0, j * HEAD_DIM:(j + 1) * HEAD_DIM, :] = o.astype(BF16)


def _nsa_prompt(qt_b, kct, vct, ks_b, vst_b, kw_b, vwt_b, gates_t):
    B, W, S = qt_b.shape
    n_ch = kct.shape[3]
    ns = -(-S // NSA_SLC_BLOCK)
    assert ns <= HEAD_DIM
    n_sel = min(NSA_TOPN, ns)
    tq = _tile(S, 128)
    tk = _tile(S, 512)
    assert tk // NSA_SLC_BLOCK <= 16
    win = min(NSA_WINDOW + tq, S)
    nl = NSA_GROUP * tq
    smap_t = jnp.asarray(np.tile(_slc_map_rows(n_ch, LANES).T, (1, NSA_GROUP)), dtype=BF16)
    per_head = lambda a: pl.BlockSpec((1, 1) + a.shape[2:], lambda b, g, i: (b, g, 0, 0))
    gw = W // NSA_KV_HEADS
    return pl.pallas_call(
        functools.partial(_nsa_kernel, ns=ns, n_sel=n_sel, tk=tk, win=win),
        grid=(B, NSA_KV_HEADS, S // tq),
        in_specs=[pl.BlockSpec((1, gw, tq), lambda b, g, i: (b, g, i)),
                  per_head(kct), per_head(vct),
                  pl.BlockSpec((1, S, LANES), lambda b, g, i: (b, 0, 0)),
                  pl.BlockSpec((1, HEAD_DIM, S), lambda b, g, i: (b, g, 0)),
                  pl.BlockSpec((1, S, LANES), lambda b, g, i: (b, 0, 0)),
                  pl.BlockSpec((1, HEAD_DIM, S), lambda b, g, i: (b, g, 0)),
                  pl.BlockSpec((1, gates_t.shape[1], tq), lambda b, g, i: (b, 0, i)),
                  pl.BlockSpec(smap_t.shape, lambda b, g, i: (0, 0))],
        out_specs=pl.BlockSpec((1, gw, tq), lambda b, g, i: (b, g, i)),
        out_shape=jax.ShapeDtypeStruct(qt_b.shape, BF16), name="nsa_prompt",
        scratch_shapes=[pltpu.VMEM((2 * LANES, nl), BF16), pltpu.VMEM((HEAD_DIM, nl), F32),
                        pltpu.VMEM((1, nl), F32), pltpu.VMEM((1, nl), F32), pltpu.VMEM((HEAD_DIM, nl), F32)],
        compiler_params=_cparams(("parallel", "parallel", "arbitrary")))(
            qt_b, kct, vct, ks_b, vst_b, kw_b, vwt_b, gates_t, smap_t)


def _outproj_kernel(*refs):
    x_ref, w_ref, y_ref = refs[0], refs[-2], refs[-1]
    y = x_ref[...]
    k0 = 0
    for o_ref in refs[1:-2]:
        if len(o_ref.shape) == 3:
            kw = o_ref.shape[1]
            y = y + lax.dot_general(o_ref[0].astype(BF16), w_ref[k0:k0 + kw, :], (((0,), (0,)), ((), ())),
                                    preferred_element_type=F32)
        else:
            kw = o_ref.shape[1]
            y = y + jnp.dot(o_ref[...].astype(BF16), w_ref[k0:k0 + kw, :], preferred_element_type=F32)
        k0 += kw
    y_ref[...] = y


def _outproj(x2d, parts, w_bf, name, tm=None):
    M, D = x2d.shape
    tm = tm or _tile(M, 512)
    specs = []
    for o in parts:
        if o.ndim == 3:
            npb = o.shape[2] // tm
            specs.append(pl.BlockSpec((1, o.shape[1], tm), lambda i, npb=npb: (i // npb, 0, i % npb)))
        else:
            specs.append(pl.BlockSpec((tm, o.shape[1]), lambda i: (i, 0)))
    return pl.pallas_call(
        _outproj_kernel, grid=(M // tm,),
        in_specs=[pl.BlockSpec((tm, D), lambda i: (i, 0))] + specs + [pl.BlockSpec(w_bf.shape, lambda i: (0, 0))],
        out_specs=pl.BlockSpec((tm, D), lambda i: (i, 0)),
        out_shape=jax.ShapeDtypeStruct((M, D), F32), name=name,
        compiler_params=_cparams(("parallel",)))(x2d, *parts, w_bf)


def _ffn_kernel(x_ref, g_ref, wg_ref, wu_ref, wd_ref, y_ref, xn_ref):
    f = pl.program_id(1)

    @pl.when(f == 0)
    def _():
        x = x_ref[...]
        xn_ref[...] = _rmsnorm(x, g_ref[...]).astype(BF16)
        y_ref[...] = x

    xn = xn_ref[...]
    h = _silu(jnp.dot(xn, wg_ref[...], preferred_element_type=F32)) * \
        jnp.dot(xn, wu_ref[...], preferred_element_type=F32)
    y_ref[...] += jnp.dot(h.astype(BF16), wd_ref[...], preferred_element_type=F32)


def _ffn(x2d, gain, wg, wu, wd, name):
    M, D = x2d.shape
    Fd = wg.shape[1]
    tm = _tile(M, 512)
    nf = 2 if Fd % (2 * LANES) == 0 else 1
    fc = Fd // nf
    return pl.pallas_call(
        _ffn_kernel, grid=(M // tm, nf),
        in_specs=[pl.BlockSpec((tm, D), lambda i, f: (i, 0)), pl.BlockSpec((1, D), lambda i, f: (0, 0)),
                  pl.BlockSpec((D, fc), lambda i, f: (0, f)), pl.BlockSpec((D, fc), lambda i, f: (0, f)),
                  pl.BlockSpec((fc, D), lambda i, f: (f, 0))],
        out_specs=pl.BlockSpec((tm, D), lambda i, f: (i, 0)),
        out_shape=jax.ShapeDtypeStruct((M, D), F32), name=name,
        scratch_shapes=[pltpu.VMEM((tm, D), BF16)],
        compiler_params=_cparams(("parallel", "arbitrary")))(x2d, gain.reshape(1, D), wg, wu, wd)


def _moe_kernel(x_ref, g_ref, wrt_ref, wg_ref, wu_ref, wd_ref, gf_ref, y_ref, xn_ref, rank_ref, comb_ref,
                rankc_ref, *, n_exp, chunk):
    e = pl.program_id(1)
    tm = x_ref.shape[0]

    @pl.when(e == 0)
    def _():
        xn = _rmsnorm(x_ref[...], g_ref[...])
        xn_ref[...] = xn.astype(BF16)
        xh, xm, _ = _split3(xn)
        wh, wm, _ = _split3(wrt_ref[...])
        dn = lambda a, b: lax.dot_general(a, b, (((1,), (1,)), ((), ())), preferred_element_type=F32)
        row = lax.broadcasted_iota(jnp.int32, (LANES, tm), 0)
        logits = jnp.where(row < n_exp, dn(wh, xh) + dn(wh, xm) + dn(wm, xh), NEG)
        v1 = jnp.max(logits, axis=0, keepdims=True)
        i1 = jnp.min(jnp.where(logits == v1, row, LANES), axis=0, keepdims=True)
        rest = jnp.where(row == i1, NEG, logits)
        v2 = jnp.max(rest, axis=0, keepdims=True)
        i2 = jnp.min(jnp.where(rest == v2, row, LANES), axis=0, keepdims=True)
        ex = jnp.exp(v2 - v1)
        comb = jnp.where(row == i1, 1.0 / (1.0 + ex), jnp.where(row == i2, ex / (1.0 + ex), 0.0))
        member = ((row == i1) | (row == i2))[0:SUBLANES]
        before = (lax.broadcasted_iota(jnp.int32, (tm, tm), 0)
                  < lax.broadcasted_iota(jnp.int32, (tm, tm), 1)).astype(BF16)
        rank = jnp.dot(jnp.where(member, 1.0, 0.0).astype(BF16), before, preferred_element_type=F32)
        rank = jnp.where(member, rank, -1.0)
        rank_ref[...] = rank
        comb_ref[...] = comb[0:SUBLANES]
        rankc_ref[...] = jnp.transpose(rank)
        y_ref[...] = jnp.zeros_like(y_ref)

    rrow = rank_ref[pl.ds(e, 1), :]
    crow = comb_ref[pl.ds(e, 1), :]
    rc = rankc_ref[...]
    rcol = jnp.sum(jnp.where(lax.broadcasted_iota(jnp.int32, rc.shape, 1) == e, rc, 0.0), axis=1, keepdims=True)
    n_tok = jnp.max(rrow).astype(jnp.int32) + 1

    def body(c, carry):
        base = (c * chunk).astype(F32)
        pick = rrow == lax.broadcasted_iota(jnp.int32, (chunk, tm), 0).astype(F32) + base
        xg = jnp.dot(jnp.where(pick, 1.0, 0.0).astype(BF16), xn_ref[...],
                     preferred_element_type=F32).astype(BF16)
        wcol = jnp.sum(jnp.where(pick, crow, 0.0), axis=1, keepdims=True)
        h = _silu(jnp.dot(xg, wg_ref[0], preferred_element_type=F32)) * \
            jnp.dot(xg, wu_ref[0], preferred_element_type=F32)
        yv = jnp.dot((h * wcol).astype(BF16), wd_ref[0], preferred_element_type=F32)
        place = jnp.where(rcol == lax.broadcasted_iota(jnp.int32, (tm, chunk), 1).astype(F32) + base,
                          1.0, 0.0).astype(BF16)
        y_ref[...] += jnp.dot(place, yv.astype(BF16), preferred_element_type=F32)
        return carry

    lax.fori_loop(0, (n_tok + chunk - 1) // chunk, body, 0)

    @pl.when(e == n_exp - 1)
    def _():
        y_ref[...] = _rmsnorm(x_ref[...] + y_ref[...], gf_ref[...])


def _moe_final(x2d, gain, w_router, wg, wu, wd, gain_final, name):
    M, D = x2d.shape
    n_exp, _, Fd = wg.shape
    assert n_exp <= SUBLANES
    tm = _tile(M, 1024)
    chunk = min(320, tm) if tm >= 1024 else min(128, tm)
    wrt = jnp.zeros((LANES, D), F32).at[:n_exp].set(w_router.T)
    return pl.pallas_call(
        functools.partial(_moe_kernel, n_exp=n_exp, chunk=chunk), grid=(M // tm, n_exp),
        in_specs=[pl.BlockSpec((tm, D), lambda i, e: (i, 0)), pl.BlockSpec((1, D), lambda i, e: (0, 0)),
                  pl.BlockSpec((LANES, D), lambda i, e: (0, 0)),
                  pl.BlockSpec((1, D, Fd), lambda i, e: (e, 0, 0)),
                  pl.BlockSpec((1, D, Fd), lambda i, e: (e, 0, 0)),
                  pl.BlockSpec((1, Fd, D), lambda i, e: (e, 0, 0)),
                  pl.BlockSpec((1, D), lambda i, e: (0, 0))],
        out_specs=pl.BlockSpec((tm, D), lambda i, e: (i, 0)),
        out_shape=jax.ShapeDtypeStruct((M, D), F32), name=name,
        scratch_shapes=[pltpu.VMEM((tm, D), BF16), pltpu.VMEM((SUBLANES, tm), F32),
                        pltpu.VMEM((SUBLANES, tm), F32), pltpu.VMEM((tm, SUBLANES), F32)],
        compiler_params=_cparams(("parallel", "arbitrary")))(
            x2d, gain.reshape(1, D), wrt, wg, wu, wd, gain_final.reshape(1, D))


def _band_kernel(*refs, n_hp, dil, span):
    group = lambda i: refs[i * n_hp:(i + 1) * n_hp]
    q_refs, kp_refs, kc_refs, vp_refs, vc_refs = [group(i) for i in range(5)]
    bias_ref, o_ref, lse_ref, o_scr = refs[5 * n_hp:]
    t, hs = pl.program_id(1), pl.program_id(2)
    tq = q_refs[0].shape[1] // dil
    rowk = lax.broadcasted_iota(jnp.int32, (span + tq, tq), 0)
    bias = bias_ref[...] + jnp.where((rowk < span) & (t == 0), NEG, 0.0)
    row = lax.broadcasted_iota(jnp.int32, (LANES, tq), 0)
    lane = lax.broadcasted_iota(jnp.int32, (tq, LANES), 1)
    first_head = hs * (2 * n_hp)

    @pl.when(hs == 0)
    def _():
        lse_ref[...] = jnp.zeros_like(lse_ref)

    def stream(r, carry):
        rows = lambda n: pl.ds(r, n, stride=dil) if dil > 1 else pl.ds(0, n)
        lses = []
        for j in range(n_hp):
            qT = jnp.transpose(q_refs[j][0, rows(tq), :]).astype(BF16)
            k = jnp.concatenate([kp_refs[j][0, rows(span), :], kc_refs[j][0, rows(tq), :]], axis=0).astype(BF16)
            vT = jnp.transpose(jnp.concatenate([vp_refs[j][0, rows(span), :], vc_refs[j][0, rows(tq), :]],
                                               axis=0)).astype(BF16)
            outs = []
            for h in range(2):
                qpad = jnp.where(row // HEAD_DIM == h, qT, jnp.zeros_like(qT))
                s = jnp.dot(k, qpad, preferred_element_type=F32) + bias
                m = jnp.max(s, axis=0, keepdims=True)
                p = jnp.exp(s - m)
                l = jnp.sum(p, axis=0, keepdims=True)
                outs.append(jnp.dot(vT[h * HEAD_DIM:(h + 1) * HEAD_DIM], p.astype(BF16),
                                    preferred_element_type=F32) / l)
                lses.append(m + jnp.log(l))
            o_scr[j, rows(tq), :] = jnp.transpose(jnp.concatenate(outs, axis=0))
        stat = jnp.transpose(jnp.concatenate(lses + [jnp.zeros((LANES - len(lses), tq), F32)], axis=0))
        stat = pltpu.roll(stat, first_head, 1)
        mine = (lane >= first_head) & (lane < first_head + 2 * n_hp)
        lse_ref[0, rows(tq), :] = jnp.where(mine, stat, lse_ref[0, rows(tq), :])
        return carry

    lax.fori_loop(0, dil, stream, 0)
    for j in range(n_hp):
        o_ref[0, :, j * LANES:(j + 1) * LANES] = o_scr[j]


def _band_attention(q, k, v, dil, span, tq, name, n_hp=2):
    B, S, W = q.shape
    L = S // dil
    tq = min(tq, L)
    assert L % tq == 0 and tq % span == 0
    per_tile = tq // span
    n_pairs = W // LANES
    assert n_pairs % n_hp == 0
    i = np.arange(span + tq)[:, None]
    j = np.arange(tq)[None, :]
    bias = jnp.asarray(np.where((j - i + span >= 0) & (j - i + span <= span), 0.0, NEG), F32)
    cur = lambda jj: pl.BlockSpec((1, dil * tq, LANES), lambda b, t, hs: (b, t, hs * n_hp + jj))
    prev = lambda jj: pl.BlockSpec((1, dil * span, LANES),
                                   lambda b, t, hs: (b, jnp.maximum(t * per_tile - 1, 0), hs * n_hp + jj))
    each = lambda mk: [mk(jj) for jj in range(n_hp)]
    res = pl.pallas_call(
        functools.partial(_band_kernel, n_hp=n_hp, dil=dil, span=span),
        grid=(B, L // tq, n_pairs // n_hp),
        in_specs=each(cur) + each(prev) + each(cur) + each(prev) + each(cur)
        + [pl.BlockSpec(bias.shape, lambda b, t, hs: (0, 0))],
        out_specs=[pl.BlockSpec((1, dil * tq, n_hp * LANES), lambda b, t, hs: (b, t, hs)),
                   pl.BlockSpec((1, dil * tq, LANES), lambda b, t, hs: (b, t, 0))],
        out_shape=[jax.ShapeDtypeStruct((B, S, W), F32), jax.ShapeDtypeStruct((B, S, LANES), F32)],
        scratch_shapes=[pltpu.VMEM((n_hp, dil * tq, LANES), F32)],
        name=name, compiler_params=_cparams(("parallel", "parallel", "arbitrary")))(
            *([q] * n_hp), *([k] * n_hp), *([k] * n_hp), *([v] * n_hp), *([v] * n_hp), bias)
    return res[0].reshape(B * S, W), res[1].reshape(B * S, LANES)


def _merge_outproj_kernel(*refs, n_groups):
    x_ref = refs[0]
    o_refs, l_refs = refs[1:1 + n_groups], refs[1 + n_groups:1 + 2 * n_groups]
    w_ref, y_ref = refs[-2], refs[-1]
    W = o_refs[0].shape[1]
    lses = [r[...] for r in l_refs]
    m = functools.reduce(jnp.maximum, lses)
    es = [jnp.exp(l - m) for l in lses]
    tot = functools.reduce(lambda a, b: a + b, es)
    expand = (lax.broadcasted_iota(jnp.int32, (LANES, W), 1) // HEAD_DIM
              == lax.broadcasted_iota(jnp.int32, (LANES, W), 0)).astype(BF16)
    mix = None
    for e, o_ref in zip(es, o_refs):
        hi, mid, _ = _split3(e / tot)
        wexp = jnp.dot(hi, expand, preferred_element_type=F32) + jnp.dot(mid, expand, preferred_element_type=F32)
        mix = wexp * o_ref[...] if mix is None else mix + wexp * o_ref[...]
    y_ref[...] = x_ref[...] + jnp.dot(mix.astype(BF16), w_ref[...], preferred_element_type=F32)


def _merge_outproj(x2d, outs, lses, w_bf, name):
    M, D = x2d.shape
    W = outs[0].shape[1]
    tm = _tile(M, 256)
    row = lambda n: pl.BlockSpec((tm, n), lambda i: (i, 0))
    return pl.pallas_call(
        functools.partial(_merge_outproj_kernel, n_groups=len(outs)), grid=(M // tm,),
        in_specs=[row(D)] + [row(W)] * len(outs) + [row(LANES)] * len(lses)
        + [pl.BlockSpec(w_bf.shape, lambda i: (0, 0))],
        out_specs=row(D), out_shape=jax.ShapeDtypeStruct((M, D), F32), name=name,
        compiler_params=_cparams(("parallel",)))(x2d, *outs, *lses, w_bf)


def _head_major(x, n_heads, pad_rows):
    DB, T, _ = x.shape
    y = x.reshape(DB, T, n_heads, HEAD_DIM).transpose(0, 2, 1, 3)
    return jnp.pad(y, ((0, 0), (0, 0), (0, pad_rows - T), (0, 0)))


def _new_cols(x, n_heads):
    DB, T, _ = x.shape
    y = x.reshape(DB, T, n_heads, HEAD_DIM).transpose(0, 2, 3, 1)
    return jnp.pad(y, ((0, 0), (0, 0), (0, 0), (0, NEW_PAD - T)))


def _row_of(col_vec_row):
    return jnp.transpose(jnp.broadcast_to(col_vec_row, (SUBLANES, LANES)))[0:SUBLANES, 0:1]


def _fox_dec_kernel(pt_ref, *refs, n_pg, n_q):
    kts, vts, lfs = refs[:n_pg], refs[n_pg:2 * n_pg], refs[2 * n_pg:3 * n_pg]
    q_ref, knt_ref, vnt_ref, lfn_ref, o_ref, base_ref, carry_ref, m_ref, l_ref, acc_ref = refs[3 * n_pg:]
    p = pl.program_id(1)
    nh = FOX_HEADS
    page = LANES
    lane = lax.broadcasted_iota(jnp.int32, (QPAD, LANES), 1)
    trow = lax.broadcasted_iota(jnp.int32, (QPAD, LANES), 0)
    r = lax.broadcasted_iota(jnp.int32, (page, page), 0)
    c = lax.broadcasted_iota(jnp.int32, (page, page), 1)
    stack = lambda xs: jnp.concatenate(xs, axis=0)

    @pl.when(p == 0)
    def _():
        _online_init(m_ref, l_ref, acc_ref)
        carry_ref[...] = jnp.zeros_like(carry_ref)
        cnew = _mm3_left(lfn_ref[0], r <= c)
        bases, ss = [], []
        for h in range(nh):
            bh = _row_of(cnew[h:h + 1, :])
            bases.append(bh)
            ss.append(_mm(q_ref[0, h] * SCALE, knt_ref[0, h]) + (bh - cnew[h:h + 1, :]))
        base_ref[...] = jnp.broadcast_to(stack(bases), base_ref.shape)
        live = (lane <= trow) & (lane < n_q)
        s = jnp.where(stack([live] * nh), stack(ss), NEG)
        alpha, pr = _online_step(s, m_ref, l_ref)
        pv = stack([_mm_nt(pr[h * QPAD:(h + 1) * QPAD], vnt_ref[0, h]) for h in range(nh)])
        acc_ref[...] = alpha * acc_ref[...] + pv

    lf_all = stack([lf[0] for lf in lfs])
    after_all = _mm3_left(lf_all, r > c)
    tot_all = after_all[:, 0:1] + lf_all[:, 0:1]
    carry = carry_ref[:, 0:1]
    decay = [None] * n_pg
    for j in reversed(range(n_pg)):
        decay[j] = carry + after_all[j * nh:(j + 1) * nh]
        carry = carry + tot_all[j * nh:(j + 1) * nh]
    carry_ref[...] = jnp.broadcast_to(carry, carry_ref.shape)
    ss = []
    for h in range(nh):
        qh = (q_ref[0, h] * SCALE).astype(BF16)
        sh = jnp.concatenate([jnp.dot(qh, kts[j][0, h].astype(BF16), preferred_element_type=F32)
                              + decay[j][h:h + 1, :] for j in range(n_pg)], axis=1)
        ss.append(sh)
    s = stack(ss) + base_ref[:, 0:1]
    alpha, pr = _online_step(s, m_ref, l_ref)
    pvs = []
    for h in range(nh):
        ph = pr[h * QPAD:(h + 1) * QPAD]
        pvs.append(sum(_mm_nt(ph[:, j * page:(j + 1) * page], vts[j][0, h]) for j in range(n_pg)))
    acc_ref[...] = alpha * acc_ref[...] + stack(pvs)

    @pl.when(p == pl.num_programs(1) - 1)
    def _():
        o_ref[0] = acc_ref[...] / l_ref[...]


def _fox_decode(q, k_new, v_new, lf_new, cache_k, cache_v, cache_lf, pt_flat):
    DB, T, W = q.shape
    n_pool, page, nh, _ = cache_k.shape
    assert page == LANES and T <= QPAD
    n_pages = pt_flat.shape[0] // DB
    n_pg = math.gcd(n_pages, PAGES_PER_STEP)
    n_steps = n_pages // n_pg
    ckt = jnp.transpose(cache_k, (0, 2, 3, 1))
    cvt = jnp.transpose(cache_v, (0, 2, 3, 1))
    clf = jnp.transpose(cache_lf, (0, 2, 1))
    qh = _head_major(q, nh, QPAD)
    knt, vnt = _new_cols(k_new, nh), _new_cols(v_new, nh)
    lfn = jnp.pad(jnp.transpose(lf_new, (0, 2, 1)), ((0, 0), (0, 0), (0, LANES - T)))
    page_idx = lambda j: (lambda b, p, pt: (pt[b * n_pages + (n_steps - 1 - p) * n_pg + j], 0, 0, 0))
    lf_idx = lambda j: (lambda b, p, pt: (pt[b * n_pages + (n_steps - 1 - p) * n_pg + j], 0, 0))
    per_db = lambda a: pl.BlockSpec((1,) + a.shape[1:], lambda b, p, pt: (b,) + (0,) * (a.ndim - 1))
    rows = nh * QPAD
    grid_spec = pltpu.PrefetchScalarGridSpec(
        num_scalar_prefetch=1, grid=(DB, n_steps),
        in_specs=([pl.BlockSpec((1, nh, HEAD_DIM, page), page_idx(j)) for j in range(n_pg)] * 2
                  + [pl.BlockSpec((1, nh, page), lf_idx(j)) for j in range(n_pg)]
                  + [per_db(qh), per_db(knt), per_db(vnt), per_db(lfn)]),
        out_specs=pl.BlockSpec((1, rows, HEAD_DIM), lambda b, p, pt: (b, 0, 0)),
        scratch_shapes=[pltpu.VMEM((rows, LANES), F32), pltpu.VMEM((nh, LANES), F32),
                        pltpu.VMEM((rows, 1), F32), pltpu.VMEM((rows, 1), F32), pltpu.VMEM((rows, HEAD_DIM), F32)])
    o = pl.pallas_call(
        functools.partial(_fox_dec_kernel, n_pg=n_pg, n_q=T), grid_spec=grid_spec,
        out_shape=jax.ShapeDtypeStruct((DB, rows, HEAD_DIM), F32), name="fox_decode",
        compiler_params=_cparams(("parallel", "arbitrary")))(
            pt_flat, *([ckt] * n_pg), *([cvt] * n_pg), *([clf] * n_pg), qh, knt, vnt, lfn)
    return o.reshape(DB, nh, QPAD, HEAD_DIM)[:, :, :T].transpose(0, 2, 1, 3).reshape(DB, T, W)


def _nsa_dec_a_kernel(q_ref, kct_ref, vct_ref, kwt_ref, vwt_ref, kwnt_ref, vwnt_ref, gt_ref, map_ref,
                      o_ref, bias_ref, *, n_q, ns, n_sel, past, n_keys):
    nr = NSA_GROUP * QPAD
    n_ch = kct_ref.shape[3]
    wb = kwt_ref.shape[3]
    t_row = lax.broadcasted_iota(jnp.int32, (nr, 1), 0) % QPAD
    imps = []
    for kv in range(NSA_KV_HEADS):
        q = (q_ref[0, kv] * SCALE).astype(BF16)
        cidx = lax.broadcasted_iota(jnp.int32, (nr, n_ch), 1)
        pc = _softmax_rows(_mm(q, kct_ref[0, kv]), cidx >= 1)
        oc = _mm_nt(pc, vct_ref[0, kv])
        pcat = jnp.concatenate([pc[j * QPAD:(j + 1) * QPAD] for j in range(NSA_GROUP)], axis=1)
        imps.append(jnp.dot(pcat.astype(BF16), map_ref[...], preferred_element_type=F32))
        sw = _mm(q, kwt_ref[0, kv])
        sn = _mm(q, kwnt_ref[0, kv])
        iw = lax.broadcasted_iota(jnp.int32, (nr, wb), 1)
        un = lax.broadcasted_iota(jnp.int32, (nr, NEW_PAD), 1)
        mw = (wb + t_row - iw <= NSA_WINDOW) & (past - wb + iw >= 0)
        mn = (un <= t_row) & (un < n_q)
        sw = jnp.where(mw, sw, NEG)
        sn = jnp.where(mn, sn, NEG)
        m = jnp.maximum(jnp.max(sw, axis=1, keepdims=True), jnp.max(sn, axis=1, keepdims=True))
        pw = jnp.where(mw, jnp.exp(sw - m), 0.0)
        pn = jnp.where(mn, jnp.exp(sn - m), 0.0)
        l = jnp.sum(pw, axis=1, keepdims=True) + jnp.sum(pn, axis=1, keepdims=True)
        ow = (_mm_nt(pw, vwt_ref[0, kv]) + _mm_nt(pn, vwnt_ref[0, kv])) / l
        gt = gt_ref[0, kv]
        o_ref[0, kv] = gt[:, 0:1] * oc + gt[:, 2:3] * ow

    imp = jnp.concatenate(imps, axis=0)
    t_sel = lax.broadcasted_iota(jnp.int32, (imp.shape[0], 1), 0) % QPAD
    sel = _select_blocks(imp, past + t_sel, ns, n_sel).astype(BF16)
    ch = 8 * LANES
    for c0 in range(0, n_keys, ch):
        w = min(ch, n_keys - c0)
        key = c0 + lax.broadcasted_iota(jnp.int32, (sel.shape[1], w), 1)
        onehot = (key // NSA_SLC_BLOCK == lax.broadcasted_iota(jnp.int32, (sel.shape[1], w), 0)).astype(BF16)
        chosen = jnp.dot(sel, onehot, preferred_element_type=F32) > 0.5
        kpos = c0 + lax.broadcasted_iota(jnp.int32, (sel.shape[0], w), 1)
        ok = chosen & (kpos <= past + t_sel) & (kpos < past + n_q)
        bias_ref[0, :, c0:c0 + w] = jnp.where(ok, 0.0, NEG)


def _nsa_dec_b_kernel(pt_ref, *refs, n_pg):
    kts, vts = refs[:n_pg], refs[n_pg:2 * n_pg]
    q_ref, knt_ref, vnt_ref, bias_ref, biasn_ref, o_ref, m_ref, l_ref, acc_ref = refs[2 * n_pg:]
    p = pl.program_id(1)
    nr = NSA_GROUP * QPAD
    page = LANES
    stack = lambda xs: jnp.concatenate(xs, axis=0)

    @pl.when(p == 0)
    def _():
        _online_init(m_ref, l_ref, acc_ref)

    def update(s_of, v_of):
        ss = []
        for kv in range(NSA_KV_HEADS):
            ss.append(s_of(kv, (q_ref[0, kv] * SCALE).astype(BF16)))
        alpha, pr = _online_step(stack(ss), m_ref, l_ref)
        acc_ref[...] = alpha * acc_ref[...] + stack([v_of(kv, pr[kv * nr:(kv + 1) * nr])
                                                     for kv in range(NSA_KV_HEADS)])

    def s_pages(kv, q):
        s = jnp.concatenate([jnp.dot(q, kts[j][0, kv].astype(BF16), preferred_element_type=F32)
                             for j in range(n_pg)], axis=1)
        return s + jnp.concatenate([bias_ref[0, kv * QPAD:(kv + 1) * QPAD]] * NSA_GROUP, axis=0)

    def v_pages(kv, pr):
        return sum(_mm_nt(pr[:, j * page:(j + 1) * page], vts[j][0, kv]) for j in range(n_pg))

    update(s_pages, v_pages)

    @pl.when(p == pl.num_programs(1) - 1)
    def _():
        update(lambda kv, q: _mm(q, knt_ref[0, kv])
               + jnp.concatenate([biasn_ref[0, kv * QPAD:(kv + 1) * QPAD]] * NSA_GROUP, axis=0),
               lambda kv, pr: _mm_nt(pr, vnt_ref[0, kv]))
        o_ref[0] = acc_ref[...] / l_ref[...]


def _nsa_decode(qb, kct, vct, cache_ks, cache_vs, ks_new, vs_new, swa_k, swa_v, kw_new, vw_new,
                gates, pt_flat, past):
    DB, T, W = qb.shape
    n_pool, page = cache_ks.shape[:2]
    assert page == LANES and past % LANES == 0 and T <= QPAD
    n_pages = pt_flat.shape[0] // DB
    n_ch = kct.shape[3]
    ns = -(-(past + T) // NSA_SLC_BLOCK)
    ns_pad = -(-ns // LANES) * LANES
    n_sel = min(NSA_TOPN, ns)
    nr = NSA_GROUP * QPAD
    n_keys = past + NEW_PAD
    q5 = _head_major(qb, NSA_HEADS, QPAD).reshape(DB, NSA_KV_HEADS, nr, HEAD_DIM)
    g5 = _head_major(jnp.pad(gates.reshape(DB, T, NSA_HEADS, 3), ((0, 0),) * 3 + ((0, HEAD_DIM - 3),))
                     .reshape(DB, T, NSA_HEADS * HEAD_DIM), NSA_HEADS, QPAD)[..., :3]
    g5 = g5.reshape(DB, NSA_KV_HEADS, nr, 3)
    smap = jnp.asarray(np.tile(_slc_map_rows(n_ch, ns_pad), (NSA_GROUP, 1)), dtype=BF16)
    kwt = jnp.transpose(swa_k, (0, 2, 3, 1))
    vwt = jnp.transpose(swa_v, (0, 2, 3, 1))
    kwnt, vwnt = _new_cols(kw_new, NSA_KV_HEADS), _new_cols(vw_new, NSA_KV_HEADS)
    per = lambda a: pl.BlockSpec((1,) + a.shape[1:], lambda b: (b,) + (0,) * (a.ndim - 1))
    part, bias = pl.pallas_call(
        functools.partial(_nsa_dec_a_kernel, n_q=T, ns=ns, n_sel=n_sel, past=past, n_keys=n_keys), grid=(DB,),
        in_specs=[per(q5), per(kct), per(vct), per(kwt), per(vwt), per(kwnt), per(vwnt), per(g5),
                  pl.BlockSpec(smap.shape, lambda b: (0, 0))],
        out_specs=[pl.BlockSpec((1, NSA_KV_HEADS, nr, HEAD_DIM), lambda b: (b, 0, 0, 0)),
                   pl.BlockSpec((1, NSA_KV_HEADS * QPAD, n_keys), lambda b: (b, 0, 0))],
        out_shape=[jax.ShapeDtypeStruct((DB, NSA_KV_HEADS, nr, HEAD_DIM), F32),
                   jax.ShapeDtypeStruct((DB, NSA_KV_HEADS * QPAD, n_keys), F32)], name="nsa_decode_a",
        compiler_params=_cparams(("parallel",)))(q5, kct, vct, kwt, vwt, kwnt, vwnt, g5, smap)

    n_pg = math.gcd(n_pages, PAGES_PER_STEP)
    ckt = jnp.transpose(cache_ks, (0, 2, 3, 1))
    cvt = jnp.transpose(cache_vs, (0, 2, 3, 1))
    ksnt, vsnt = _new_cols(ks_new, NSA_KV_HEADS), _new_cols(vs_new, NSA_KV_HEADS)
    page_idx = lambda j: (lambda b, p, pt: (pt[b * n_pages + p * n_pg + j], 0, 0, 0))
    per_db = lambda a: pl.BlockSpec((1,) + a.shape[1:], lambda b, p, pt: (b,) + (0,) * (a.ndim - 1))
    rows = NSA_KV_HEADS * nr
    grid_spec = pltpu.PrefetchScalarGridSpec(
        num_scalar_prefetch=1, grid=(DB, n_pages // n_pg),
        in_specs=([pl.BlockSpec((1, NSA_KV_HEADS, HEAD_DIM, page), page_idx(j)) for j in range(n_pg)] * 2
                  + [per_db(q5), per_db(ksnt), per_db(vsnt),
                     pl.BlockSpec((1, NSA_KV_HEADS * QPAD, n_pg * page), lambda b, p, pt: (b, 0, p)),
                     pl.BlockSpec((1, NSA_KV_HEADS * QPAD, NEW_PAD), lambda b, p, pt: (b, 0, past // NEW_PAD))]),
        out_specs=pl.BlockSpec((1, rows, HEAD_DIM), lambda b, p, pt: (b, 0, 0)),
        scratch_shapes=[pltpu.VMEM((rows, 1), F32), pltpu.VMEM((rows, 1), F32), pltpu.VMEM((rows, HEAD_DIM), F32)])
    osl = pl.pallas_call(
        functools.partial(_nsa_dec_b_kernel, n_pg=n_pg), grid_spec=grid_spec,
        out_shape=jax.ShapeDtypeStruct((DB, rows, HEAD_DIM), F32), name="nsa_decode_b",
        compiler_params=_cparams(("parallel", "arbitrary")))(
            pt_flat, *([ckt] * n_pg), *([cvt] * n_pg), q5, ksnt, vsnt, bias, bias)

    o = part + g5[..., 1:2] * osl.reshape(DB, NSA_KV_HEADS, nr, HEAD_DIM)
    return o.reshape(DB, NSA_HEADS, QPAD, HEAD_DIM)[:, :, :T].transpose(0, 2, 1, 3).reshape(DB, T, W)


def _dil_dec_kernel(q_ref, kt_ref, vt_ref, knt_ref, vnt_ref, o_ref, *, n_q, wc):
    hb = kt_ref.shape[1]
    t = lax.broadcasted_iota(jnp.int32, (QPAD, 1), 0)

    def log_mult(d, ok):
        w = jnp.zeros(d.shape, F32)
        for window, dil in DIL_PAIRS:
            w = w + ((d >= 0) & (d <= window) & (d % dil == 0)).astype(F32)
        return jnp.where(ok, w, 0.0)

    wk = log_mult(wc + t - lax.broadcasted_iota(jnp.int32, (QPAD, wc), 1), t < n_q)
    un = lax.broadcasted_iota(jnp.int32, (QPAD, NEW_PAD), 1)
    wn = log_mult(t - un, (t < n_q) & (un < n_q))
    for h in range(hb):
        q = (q_ref[0, h] * SCALE).astype(BF16)
        sk = jnp.where(wk > 0.0, _mm(q, kt_ref[0, h]), NEG)
        sn = jnp.where(wn > 0.0, _mm(q, knt_ref[0, h]), NEG)
        m = jnp.maximum(jnp.max(sk, axis=1, keepdims=True), jnp.max(sn, axis=1, keepdims=True))
        pk = wk * jnp.exp(sk - m)
        pn = wn * jnp.exp(sn - m)
        l = jnp.sum(pk, axis=1, keepdims=True) + jnp.sum(pn, axis=1, keepdims=True)
        o_ref[0, h] = (_mm_nt(pk, vt_ref[0, h]) + _mm_nt(pn, vnt_ref[0, h])) / jnp.where(l > 0.0, l, 1.0)


def _dilated_decode(q, k_new, v_new, cache_k, cache_v):
    DB, T, W = q.shape
    wc, nh = cache_k.shape[1], cache_k.shape[2]
    ckt = jnp.transpose(cache_k, (0, 2, 3, 1))
    cvt = jnp.transpose(cache_v, (0, 2, 3, 1))
    qh = _head_major(q, nh, QPAD)
    knt, vnt = _new_cols(k_new, nh), _new_cols(v_new, nh)
    hb = math.gcd(nh, 4)
    spec = lambda a: pl.BlockSpec((1, hb) + a.shape[2:], lambda b, j: (b, j, 0, 0))
    o = pl.pallas_call(
        functools.partial(_dil_dec_kernel, n_q=T, wc=wc), grid=(DB, nh // hb),
        in_specs=[spec(qh), spec(ckt), spec(cvt), spec(knt), spec(vnt)],
        out_specs=pl.BlockSpec((1, hb, QPAD, HEAD_DIM), lambda b, j: (b, j, 0, 0)),
        out_shape=jax.ShapeDtypeStruct((DB, nh, QPAD, HEAD_DIM), F32), name="dilated_decode",
        compiler_params=_cparams(("parallel", "parallel")))(qh, ckt, cvt, knt, vnt)
    return o[:, :, :T].transpose(0, 2, 1, 3).reshape(DB, T, W)


def kernel(x_prompt, x_sample, cache_a_k, cache_a_v, cache_a_logf, cache_b_cmp_k, cache_b_cmp_v, cache_b_slc_k, cache_b_slc_v, cache_b_swa_k, cache_b_swa_v, cache_c_k, cache_c_v, page_table, norm_mix0, w_in0, fox_bf, nsa_pe_k, nsa_w1_k, nsa_w2_k, nsa_pe_v, nsa_w1_v, nsa_w2_v, w_out0, norm_ffn0, ffn_w_gate, ffn_w_up, ffn_w_down, norm_mix1, w_in1, w_out1, norm_ffn1, moe_router, moe_w_gate, moe_w_up, moe_w_down, norm_final):
    B, S, D = x_prompt.shape
    DB, T, _ = x_sample.shape
    n_pages = page_table.shape[1]
    past = n_pages * cache_a_k.shape[1]
    pt_flat = page_table.reshape(-1).astype(jnp.int32)
    fw = FOX_HEADS * HEAD_DIM
    nw = NSA_HEADS * HEAD_DIM
    kvw = NSA_KV_HEADS * HEAD_DIM

    cuts = np.cumsum([0, fw, fw, fw, FOX_HEADS, nw] + [kvw] * 6 + [3 * NSA_HEADS])
    col = lambda i: w_in0[:, cuts[i]:cuts[i + 1]]
    qa_w, ka_w, va_w, fa_w, qb_w, kc_w, vc_w, ks_w, vs_w, kw_w, vw_w, gb_w = [col(i) for i in range(12)]

    f32_out = ((F32, 1.0),)
    bf16_out = ((BF16, 1.0),)

    def pack(ws, ropes, emits=None):
        widths = [w.shape[1] for w in ws]
        starts = np.concatenate([[0], np.cumsum(widths)[:-1]])
        emits = emits or [f32_out] * len(ws)
        return (jnp.concatenate(ws, axis=1).astype(BF16),
                [(int(s), int(w), r, e) for s, w, r, e in zip(starts, widths, ropes, emits)])

    both_out = ((F32, 1.0), (BF16, 1.0))
    q_out = ((BF16, SCALE),)
    w0r, segs0r = pack([kc_w, vc_w, ka_w, ks_w, kw_w], [True, False, False, True, True],
                       [f32_out] * 2 + [bf16_out] * 3)
    w0c, segs0c = pack([qa_w, ka_w, va_w, qb_w, kc_w, ks_w, kw_w, vc_w, vs_w, vw_w, gb_w],
                       [False, False, False, True, True, True, True, False, False, False, "sigmoid"],
                       [q_out, f32_out, both_out, q_out, f32_out, f32_out, f32_out, f32_out, both_out, both_out,
                        f32_out])
    w0c = w0c.T
    w0s, segs0s = pack([qa_w, ka_w, va_w, qb_w, kc_w, ks_w, kw_w, vc_w, vs_w, vw_w],
                       [False, False, False, True, True, True, True, False, False, False])
    wgate = gb_w.astype(BF16)
    wft = jnp.zeros((16, D), F32).at[:FOX_HEADS].set(fa_w.T).astype(BF16)
    wfr = jnp.zeros((D, LANES), F32).at[:, :FOX_HEADS].set(fa_w).astype(BF16)
    brow = jnp.zeros((1, LANES), F32).at[0, :FOX_HEADS].set(fox_bf)
    logf_args = (wft, fox_bf.reshape(FOX_HEADS, 1).astype(F32), wfr, brow)
    dw = w_in1.shape[1] // 3
    w1 = w_in1.astype(BF16)
    segs1 = [(0, dw, True, f32_out), (dw, dw, True, f32_out), (2 * dw, dw, False, f32_out)]
    w1c = w1[:, dw:].T
    segs1c = [(0, dw, True, f32_out), (dw, dw, False, f32_out)]
    w_out0_b, w_out1_b = w_out0.astype(BF16), w_out1.astype(BF16)
    ffn_g, ffn_u, ffn_d = ffn_w_gate.astype(BF16), ffn_w_up.astype(BF16), ffn_w_down.astype(BF16)
    moe_g, moe_u, moe_d = moe_w_gate.astype(BF16), moe_w_up.astype(BF16), moe_w_down.astype(BF16)
    cmp_k_w = _cmp_weights(nsa_pe_k, nsa_w1_k, nsa_w2_k)
    cmp_v_w = _cmp_weights(nsa_pe_v, nsa_w1_v, nsa_w2_v)

    tab_p = _rope_tables(jnp.arange(S))
    tab_s = _rope_tables(past + jnp.arange(DB * T) % T)
    tm_p = _tile(S, 512)
    npb = S // tm_p
    win_b = min(NSA_WINDOW, S)
    win_c = min(DIL_WINDOW_MAX, S)

    xp = x_prompt.reshape(B * S, D)
    (kc, vc, ka_b, ks_b, kw_b, qat_b, kat, vat, vat_b, qbt_b, kct, kst, kwt, vct, vst, vst_b, vwt, vwt_b,
     gates_t, lft_p, lf_rows) = _project(
        xp, norm_mix0, tab_p, npb, tm_p, w=w0r, row_segs=segs0r, wt=w0c, col_segs=segs0c,
        logf=logf_args, name="proj0_prompt")
    r3 = lambda a: a.reshape(B, S, a.shape[-1])
    heads = lambda a, h: a.reshape(B, h, HEAD_DIM, a.shape[-1])
    kaug, c0 = _fox_prep(r3(ka_b), r3(lf_rows), tm_p)
    o_at = _fox_prompt(qat_b, kaug, vat_b, c0, tm_p)
    kcmp_p = _compress(r3(kc), cmp_k_w, "compress_k_prompt")
    vcmp_p = _compress(r3(vc), cmp_v_w, "compress_v_prompt")
    kv2 = lambda a: heads(a, NSA_KV_HEADS)
    o_bt = _nsa_prompt(qbt_b, kv2(kcmp_p), kv2(vcmp_p), r3(ks_b), vst_b, r3(kw_b), vwt_b, gates_t)
    hp = _outproj(xp, [o_at, o_bt], w_out0_b, "outproj0_prompt", tm=tm_p)
    hp = _ffn(hp, norm_ffn0, ffn_g, ffn_u, ffn_d, "ffn_prompt")

    xs = x_sample.reshape(DB * T, D)
    (qa_s, ka_s, va_s, qb_s, kc_s, ks_s, kw_s, vc_s, vs_s, vw_s, gates_s, lft_s, _) = _project(
        xs, norm_mix0, tab_s, 1, DB * T, w=w0s, row_segs=segs0s, w_gate=wgate, logf=logf_args,
        name="proj0_sample")
    s3 = lambda a: a.reshape(DB, T, a.shape[-1])
    lf_s = jnp.transpose(lft_s[0].reshape(FOX_HEADS, DB, T), (1, 2, 0))
    o_a_s = _fox_decode(s3(qa_s), s3(ka_s), s3(va_s), lf_s, cache_a_k, cache_a_v, cache_a_logf, pt_flat)
    kcmp_s = _compress_paged(cache_b_cmp_k, pt_flat, DB, cmp_k_w, "compress_k_paged")
    vcmp_s = _compress_paged(cache_b_cmp_v, pt_flat, DB, cmp_v_w, "compress_v_paged")
    kvs = lambda a: a.reshape(DB, NSA_KV_HEADS, HEAD_DIM, a.shape[-1])
    o_b_s = _nsa_decode(s3(qb_s), kvs(kcmp_s), kvs(vcmp_s), cache_b_slc_k, cache_b_slc_v, s3(ks_s), s3(vs_s),
                        cache_b_swa_k, cache_b_swa_v, s3(kw_s), s3(vw_s), s3(gates_s), pt_flat, past)
    hs = _outproj(xs, [o_a_s.reshape(DB * T, fw), o_b_s.reshape(DB * T, nw)], w_out0_b, "outproj0_sample")
    hs = _ffn(hs, norm_ffn0, ffn_g, ffn_u, ffn_d, "ffn_sample")

    first_c = (S - win_c) // tm_p
    segs1p = [(0, dw, True, ((F32, SCALE),)), (dw, dw, True, f32_out), (2 * dw, dw, False, f32_out)]
    q1, k1, v1, k1t, v1t = _project(hp, norm_mix1, tab_p, npb, tm_p, w=w1, row_segs=segs1p, wt=w1c,
                                    col_segs=segs1c, col_from=(npb, first_c), name="proj1_prompt")
    groups = [_band_attention(q1.reshape(B, S, dw), k1.reshape(B, S, dw), v1.reshape(B, S, dw), dil,
                              window // dil, max(window // dil, 512 // math.isqrt(dil)), "dilated_prompt_d%d" % dil)
              for window, dil in DIL_PAIRS]
    hp = _merge_outproj(hp, [g[0] for g in groups], [g[1] for g in groups], w_out1_b, "outproj1_prompt")
    y_prompt = _moe_final(hp, norm_ffn1, moe_router, moe_g, moe_u, moe_d, norm_final, "moe_prompt").reshape(B, S, D)

    q1s, k1s, v1s = _project(hs, norm_mix1, tab_s, 1, DB * T, w=w1, row_segs=segs1, name="proj1_sample")
    o1s = _dilated_decode(s3(q1s), s3(k1s), s3(v1s), cache_c_k, cache_c_v)
    hs = _outproj(hs, [o1s.reshape(DB * T, dw)], w_out1_b, "outproj1_sample")
    y_sample = _moe_final(hs, norm_ffn1, moe_router, moe_g, moe_u, moe_d, norm_final, "moe_sample").reshape(DB, T, D)

    def state(a, h, last=None):
        a = a.reshape(a.shape[0], h, HEAD_DIM, a.shape[-1])
        if last is not None:
            a = a[..., a.shape[-1] - last:]
        return jnp.transpose(a, (0, 3, 1, 2))

    h4 = lambda a, h: a.reshape(DB, T, h, HEAD_DIM)
    nh1 = dw // HEAD_DIM
    return (y_prompt, y_sample,
            state(kat, FOX_HEADS), state(vat, FOX_HEADS), jnp.transpose(lft_p, (0, 2, 1)),
            state(kct, NSA_KV_HEADS), state(vct, NSA_KV_HEADS), state(kst, NSA_KV_HEADS), state(vst, NSA_KV_HEADS),
            state(kwt, NSA_KV_HEADS, win_b), state(vwt, NSA_KV_HEADS, win_b),
            state(k1t, nh1, win_c), state(v1t, nh1, win_c),
            h4(ka_s, FOX_HEADS), h4(va_s, FOX_HEADS), lf_s,
            h4(kc_s, NSA_KV_HEADS), h4(vc_s, NSA_KV_HEADS), h4(ks_s, NSA_KV_HEADS),
            h4(vs_s, NSA_KV_HEADS), h4(kw_s, NSA_KV_HEADS), h4(vw_s, NSA_KV_HEADS),
            h4(k1s, nh1), h4(v1s, nh1))
```

```python
import functools
import math

import numpy as np
import jax
import jax.numpy as jnp
from jax import lax
from jax.experimental import pallas as pl
from jax.experimental.pallas import tpu as pltpu

F32 = jnp.float32
BF16 = jnp.bfloat16

HEAD_DIM = 64
HALF = HEAD_DIM // 2
LANES = 128
SUBLANES = 8
ROPE_THETA = 10000.0
RMS_EPS = 1e-6
NEG = -1e30
MASK_BIG = 30000.0
SCALE = HEAD_DIM ** -0.5

FOX_HEADS = 8
NSA_HEADS = 8
NSA_KV_HEADS = 2
NSA_GROUP = NSA_HEADS // NSA_KV_HEADS
NSA_CMP_LEN = 32
NSA_CMP_STRIDE = 16
NSA_SLC_BLOCK = 64
NSA_TOPN = 16
NSA_WINDOW = 512
NSA_FORCE_BONUS = 1e3
DIL_PAIRS = ((128, 1), (512, 4), (2048, 16))
DIL_WINDOW_MAX = 2048
TOP_K = 2
QPAD = SUBLANES
NEW_PAD = LANES
PAGES_PER_STEP = 8

VMEM_LIMIT = 56 * 1024 * 1024


def _tile(n, pref):
    return pref if n % pref == 0 else n


def _cparams(sem):
    return pltpu.CompilerParams(dimension_semantics=sem, vmem_limit_bytes=VMEM_LIMIT)


def _mm(a, b):
    return jnp.dot(a.astype(BF16), b.astype(BF16), preferred_element_type=F32)


def _mm_nt(a, b):
    return lax.dot_general(a.astype(BF16), b.astype(BF16), (((1,), (1,)), ((), ())),
                           preferred_element_type=F32)


def _split3(x):
    hi = x.astype(BF16)
    r = x - hi.astype(F32)
    mid = r.astype(BF16)
    lo = (r - mid.astype(F32)).astype(BF16)
    return hi, mid, lo


def _mm3_left(x, exact_rhs):
    b = exact_rhs.astype(BF16)
    hi, mid, lo = _split3(x)
    d = lambda p: jnp.dot(p, b, preferred_element_type=F32)
    return d(hi) + d(mid) + d(lo)


def _sigmoid(z):
    return 1.0 / (1.0 + jnp.exp(-z))


def _silu(z):
    return z * _sigmoid(z)


def _log_sigmoid(z):
    return jnp.minimum(z, 0.0) - jnp.log1p(jnp.exp(-jnp.abs(z)))


def _rmsnorm(x, g):
    return x * lax.rsqrt(jnp.mean(x * x, axis=-1, keepdims=True) + RMS_EPS) * g


def _rope_rows(y, cos, sin_signed):
    n = y.shape[1]
    lane = lax.broadcasted_iota(jnp.int32, y.shape, 1)
    first = (lane % HEAD_DIM) < HALF
    rot = jnp.where(first, pltpu.roll(y, n - HALF, 1), pltpu.roll(y, HALF, 1))
    reps = n // LANES
    if reps > 1:
        cos = jnp.concatenate([cos] * reps, axis=1)
        sin_signed = jnp.concatenate([sin_signed] * reps, axis=1)
    return y * cos + rot * sin_signed


def _rope_cols(yt, cos_t, sin_t):
    out = []
    for h in range(yt.shape[0] // HEAD_DIM):
        a = yt[h * HEAD_DIM:h * HEAD_DIM + HALF]
        b = yt[h * HEAD_DIM + HALF:(h + 1) * HEAD_DIM]
        out += [a * cos_t - b * sin_t, b * cos_t + a * sin_t]
    return jnp.concatenate(out, axis=0)


def _rope_tables(pos):
    inv = jnp.exp(-math.log(ROPE_THETA) * jnp.arange(HALF, dtype=F32) / HALF)
    ang = pos.astype(F32)[:, None] * inv[None, :]
    cos, sin = jnp.cos(ang), jnp.sin(ang)
    return (jnp.concatenate([cos, cos, cos, cos], axis=1),
            jnp.concatenate([-sin, sin, -sin, sin], axis=1), cos.T, sin.T)


def _softmax_rows(s, mask):
    sm = jnp.where(mask, s, NEG)
    m = jnp.max(sm, axis=1, keepdims=True)
    p = jnp.where(mask, jnp.exp(sm - m), 0.0)
    l = jnp.sum(p, axis=1, keepdims=True)
    return p / jnp.where(l > 0.0, l, 1.0)


def _online_step(s, m_ref, l_ref):
    m_prev = m_ref[...]
    m_new = jnp.maximum(m_prev, jnp.max(s, axis=1, keepdims=True))
    alpha = jnp.exp(m_prev - m_new)
    p = jnp.exp(s - m_new)
    l_ref[...] = alpha * l_ref[...] + jnp.sum(p, axis=1, keepdims=True)
    m_ref[...] = m_new
    return alpha, p


def _online_init(m_ref, l_ref, acc_ref):
    m_ref[...] = jnp.full(m_ref.shape, NEG, F32)
    l_ref[...] = jnp.zeros_like(l_ref)
    acc_ref[...] = jnp.zeros_like(acc_ref)


def _proj_kernel(*refs, row_segs, col_segs, n_gate, with_logf, col_from):
    it = iter(refs)
    x_ref, g_ref, cos_ref, sin_ref, cost_ref, sint_ref = [next(it) for _ in range(6)]
    w_ref = next(it) if row_segs else None
    wt_ref = next(it) if col_segs else None
    wg_ref = next(it) if n_gate else None
    if with_logf:
        wft_ref, bcol_ref, wfr_ref, brow_ref = next(it), next(it), next(it), next(it)
    outs = list(it)
    xn = _rmsnorm(x_ref[...], g_ref[...]).astype(BF16)
    k = 0
    for c0, width, rope, emits in row_segs:
        y = jnp.dot(xn, w_ref[:, c0:c0 + width], preferred_element_type=F32)
        if rope:
            y = _rope_rows(y, cos_ref[...], sin_ref[...])
        for dtype, scale in emits:
            outs[k][...] = (y if scale == 1.0 else y * scale).astype(dtype)
            k += 1

    def cols():
        kk = k
        for r0, height, rope, emits in col_segs:
            yt = _mm_nt(wt_ref[r0:r0 + height, :], xn)
            if rope == "sigmoid":
                yt = _sigmoid(yt)
            elif rope:
                yt = _rope_cols(yt, cost_ref[...], sint_ref[...])
            for dtype, scale in emits:
                outs[kk][0] = (yt if scale == 1.0 else yt * scale).astype(dtype)
                kk += 1

    if col_segs:
        if col_from:
            pl.when(pl.program_id(0) % col_from[0] >= col_from[1])(cols)
        else:
            cols()
        k += sum(len(e) for _, _, _, e in col_segs)
    if n_gate:
        outs[k][...] = _sigmoid(jnp.dot(xn, wg_ref[...], preferred_element_type=F32))
        k += 1
    if with_logf:
        yt = _mm_nt(wft_ref[...], xn)
        outs[k][0] = _log_sigmoid(yt[0:FOX_HEADS] + bcol_ref[...])
        outs[k + 1][...] = _log_sigmoid(jnp.dot(xn, wfr_ref[...], preferred_element_type=F32) + brow_ref[...])


def _project(x2d, gain, tables, n_pos_blocks, tm, w=None, row_segs=(), wt=None, col_segs=(),
             w_gate=None, logf=None, col_from=None, name="proj"):
    M, D = x2d.shape
    nt = M // tm
    n_seq = nt // n_pos_blocks
    cos_t, sin_t, cos_c, sin_c = tables
    pos_map = lambda i: (i % n_pos_blocks, 0)
    posc_map = lambda i: (0, i % n_pos_blocks)
    const = lambda a: pl.BlockSpec(a.shape, lambda i: (0,) * a.ndim)
    in_specs = [pl.BlockSpec((tm, D), lambda i: (i, 0)), const(gain.reshape(1, D)),
                pl.BlockSpec((tm, LANES), pos_map), pl.BlockSpec((tm, LANES), pos_map),
                pl.BlockSpec((HALF, tm), posc_map), pl.BlockSpec((HALF, tm), posc_map)]
    args = [x2d, gain.reshape(1, D), cos_t, sin_t, cos_c, sin_c]
    for a in (w, wt, w_gate):
        if a is not None:
            in_specs.append(const(a))
            args.append(a)
    out_shape, out_specs = [], []
    for _, wd, _, emits in row_segs:
        for dtype, _ in emits:
            out_shape.append(jax.ShapeDtypeStruct((M, wd), dtype))
            out_specs.append(pl.BlockSpec((tm, wd), lambda i: (i, 0)))
    first = col_from[1] if col_from else 0
    n_cb = n_pos_blocks - first
    col_map = lambda i: (i // n_pos_blocks, 0, jnp.maximum(i % n_pos_blocks - first, 0))
    for _, ht, _, emits in col_segs:
        for dtype, _ in emits:
            out_shape.append(jax.ShapeDtypeStruct((n_seq, ht, n_cb * tm), dtype))
            out_specs.append(pl.BlockSpec((1, ht, tm), col_map))
    n_gate = 0
    if w_gate is not None:
        n_gate = w_gate.shape[1]
        out_shape.append(jax.ShapeDtypeStruct((M, n_gate), F32))
        out_specs.append(pl.BlockSpec((tm, n_gate), lambda i: (i, 0)))
    if logf is not None:
        in_specs += [const(a) for a in logf]
        args += list(logf)
        out_shape.append(jax.ShapeDtypeStruct((n_seq, FOX_HEADS, n_pos_blocks * tm), F32))
        out_specs.append(pl.BlockSpec((1, FOX_HEADS, tm), lambda i: (i // n_pos_blocks, 0, i % n_pos_blocks)))
        out_shape.append(jax.ShapeDtypeStruct((M, LANES), F32))
        out_specs.append(pl.BlockSpec((tm, LANES), lambda i: (i, 0)))
    return pl.pallas_call(
        functools.partial(_proj_kernel, row_segs=tuple(row_segs), col_segs=tuple(col_segs), n_gate=n_gate,
                          with_logf=logf is not None, col_from=col_from),
        grid=(nt,), in_specs=in_specs, out_specs=out_specs, out_shape=out_shape, name=name,
        compiler_params=_cparams(("arbitrary",)))(*args)


def _fox_prep_kernel(k_ref, lf_ref, kaug_ref, c0_ref, carry_ref):
    j = pl.program_id(1)
    tc = k_ref.shape[1]

    @pl.when(j == 0)
    def _():
        carry_ref[...] = jnp.zeros_like(carry_ref)
        c0_ref[...] = jnp.zeros_like(c0_ref)

    lane = lax.broadcasted_iota(jnp.int32, (FOX_HEADS, LANES), 1)
    start = jnp.transpose(carry_ref[...])[0:FOX_HEADS, 0:1]
    c0_ref[0] = jnp.where(lane == j, start, c0_ref[0])

    r = lax.broadcasted_iota(jnp.int32, (tc, tc), 0)
    c = lax.broadcasted_iota(jnp.int32, (tc, tc), 1)
    tri = (c <= r).astype(BF16)
    hi, mid, lo = _split3(lf_ref[0])
    d = lambda p: jnp.dot(tri, p, preferred_element_type=F32)
    local = d(hi) + d(mid) + d(lo)
    carry_ref[...] = carry_ref[...] + jnp.broadcast_to(local[tc - 1:tc, :], carry_ref.shape)
    parts = jnp.concatenate(_split3(-local), axis=1)
    k = k_ref[0]
    kr = lax.broadcasted_iota(jnp.int32, (k.shape[1], LANES), 0)
    kc = lax.broadcasted_iota(jnp.int32, (k.shape[1], LANES), 1)
    pr = lax.broadcasted_iota(jnp.int32, (3 * LANES, LANES), 0)
    pc = lax.broadcasted_iota(jnp.int32, (3 * LANES, LANES), 1)
    for h in range(FOX_HEADS):
        place_k = ((kr == HEAD_DIM * h + kc) & (kc < HEAD_DIM)).astype(BF16)
        place_c = ((pr % LANES == h) & (pc == HEAD_DIM + pr // LANES)).astype(BF16)
        kaug_ref[0, h] = (jnp.dot(k, place_k, preferred_element_type=F32)
                          + jnp.dot(parts, place_c, preferred_element_type=F32)).astype(BF16)


def _fox_prep(k_b, lf_rows, tc):
    B, S, W = k_b.shape
    assert S // tc <= LANES
    return pl.pallas_call(
        _fox_prep_kernel, grid=(B, S // tc),
        in_specs=[pl.BlockSpec((1, tc, W), lambda b, j: (b, j, 0)),
                  pl.BlockSpec((1, tc, LANES), lambda b, j: (b, j, 0))],
        out_specs=[pl.BlockSpec((1, FOX_HEADS, tc, LANES), lambda b, j: (b, 0, j, 0)),
                   pl.BlockSpec((1, FOX_HEADS, LANES), lambda b, j: (b, 0, 0))],
        out_shape=[jax.ShapeDtypeStruct((B, FOX_HEADS, S, LANES), BF16),
                   jax.ShapeDtypeStruct((B, FOX_HEADS, LANES), F32)], name="fox_prep",
        scratch_shapes=[pltpu.VMEM((SUBLANES, LANES), F32)],
        compiler_params=_cparams(("parallel", "arbitrary")))(k_b, lf_rows)


def _fox_kernel(qi_ref, ki_ref, qt_ref, kaug_ref, vt_ref, c0_ref, o_ref, qa_ref, m_ref, l_ref, acc_ref):
    hp, step = pl.program_id(1), pl.program_id(2)
    qi, ki = qi_ref[step], ki_ref[step]
    tq, tk = qt_ref.shape[2], kaug_ref.shape[2]
    lane1 = lax.broadcasted_iota(jnp.int32, (1, LANES), 1)

    @pl.when(ki == 0)
    def _():
        row = lax.broadcasted_iota(jnp.int32, (HEAD_DIM, tq), 0)
        ones = jnp.where(row < 3, 1.0, 0.0).astype(BF16)
        for h in range(2):
            qa_ref[h] = jnp.concatenate([qt_ref[0, h * HEAD_DIM:(h + 1) * HEAD_DIM, :], ones], axis=0)
        _online_init(m_ref, l_ref, acc_ref)

    def tile(diagonal):
        if diagonal:
            live = lax.broadcasted_iota(jnp.int32, (tk, tq), 0) <= lax.broadcasted_iota(jnp.int32, (tk, tq), 1)
        for h in range(2):
            c0 = c0_ref[0, pl.ds(2 * hp + h, 1), :]
            delta = jnp.sum(jnp.where(lane1 == qi, c0, 0.0) - jnp.where(lane1 == ki, c0, 0.0),
                            axis=1, keepdims=True)
            s = jnp.dot(kaug_ref[0, h], qa_ref[h], preferred_element_type=F32)
            if diagonal:
                s = jnp.where(live, s, NEG)
            m_prev = m_ref[h]
            m_new = jnp.maximum(m_prev, jnp.max(s, axis=0, keepdims=True) + delta)
            p = jnp.exp(s - (m_new - delta))
            alpha = jnp.exp(m_prev - m_new)
            l_ref[h] = alpha * l_ref[h] + jnp.sum(p, axis=0, keepdims=True)
            acc_ref[h] = alpha * acc_ref[h] + jnp.dot(vt_ref[0, h * HEAD_DIM:(h + 1) * HEAD_DIM, :],
                                                      p.astype(BF16), preferred_element_type=F32)
            m_ref[h] = m_new

    pl.when(ki < qi)(lambda: tile(False))

    @pl.when(ki == qi)
    def _():
        tile(True)
        for h in range(2):
            o_ref[0, h * HEAD_DIM:(h + 1) * HEAD_DIM, :] = (acc_ref[h] / l_ref[h]).astype(BF16)


def _fox_prompt(qt_b, kaug, vt_b, c0, tq):
    B, W, S = qt_b.shape
    nq = S // tq
    pairs = [(q, k) for q in range(nq) for k in range(q + 1)]
    qi_tab = jnp.asarray([p[0] for p in pairs], jnp.int32)
    ki_tab = jnp.asarray([p[1] for p in pairs], jnp.int32)
    grid_spec = pltpu.PrefetchScalarGridSpec(
        num_scalar_prefetch=2, grid=(B, W // LANES, len(pairs)),
        in_specs=[pl.BlockSpec((1, LANES, tq), lambda b, hp, s, qi, ki: (b, hp, qi[s])),
                  pl.BlockSpec((1, 2, tq, LANES), lambda b, hp, s, qi, ki: (b, hp, ki[s], 0)),
                  pl.BlockSpec((1, LANES, tq), lambda b, hp, s, qi, ki: (b, hp, ki[s])),
                  pl.BlockSpec((1, FOX_HEADS, LANES), lambda b, hp, s, qi, ki: (b, 0, 0))],
        out_specs=pl.BlockSpec((1, LANES, tq), lambda b, hp, s, qi, ki: (b, hp, qi[s])),
        scratch_shapes=[pltpu.VMEM((2, LANES, tq), BF16), pltpu.VMEM((2, 1, tq), F32),
                        pltpu.VMEM((2, 1, tq), F32), pltpu.VMEM((2, HEAD_DIM, tq), F32)])
    return pl.pallas_call(
        _fox_kernel, grid_spec=grid_spec, out_shape=jax.ShapeDtypeStruct(qt_b.shape, BF16), name="fox_prompt",
        compiler_params=_cparams(("parallel", "parallel", "arbitrary")))(qi_tab, ki_tab, qt_b, kaug, vt_b, c0)


def _cmp_compute(x, pea_ref, peb_ref, wa_ref, wb_ref, w2t_ref, o_ref, carry_ref):
    n = x.shape[0]
    a = jnp.dot((x + pea_ref[...]).astype(BF16), wa_ref[...], preferred_element_type=F32)
    b = jnp.dot((x + peb_ref[...]).astype(BF16), wb_ref[...], preferred_element_type=F32)
    rowi = lax.broadcasted_iota(jnp.int32, a.shape, 0)
    a_prev = jnp.where(rowi == 0, carry_ref[0:1, :], pltpu.roll(a, 1, 0))
    carry_ref[...] = jnp.broadcast_to(a[n - 1:n, :], carry_ref.shape)
    o_ref[0] = _mm_nt(w2t_ref[...], _silu(a_prev + b))


def _cmp_kernel(x_ref, pea_ref, peb_ref, wa_ref, wb_ref, w2t_ref, o_ref, carry_ref):
    @pl.when(pl.program_id(1) == 0)
    def _():
        carry_ref[...] = jnp.zeros_like(carry_ref)
    _cmp_compute(x_ref[0], pea_ref, peb_ref, wa_ref, wb_ref, w2t_ref, o_ref, carry_ref)


def _cmp_paged_kernel(pt_ref, *refs, n_pg):
    pages = refs[:n_pg]
    pea_ref, peb_ref, wa_ref, wb_ref, w2t_ref, o_ref, xs_ref, carry_ref = refs[n_pg:]
    page = pages[0].shape[3]

    @pl.when(pl.program_id(1) == 0)
    def _():
        carry_ref[...] = jnp.zeros_like(carry_ref)

    for j, r in enumerate(pages):
        xs_ref[j * page:(j + 1) * page, :] = jnp.transpose(r[0].reshape(LANES, page))
    n = n_pg * page // NSA_CMP_STRIDE
    a = jnp.zeros((n, wa_ref.shape[2]), F32)
    b = jnp.zeros((n, wb_ref.shape[2]), F32)
    for l in range(NSA_CMP_STRIDE):
        xl = xs_ref[pl.ds(l, n, stride=NSA_CMP_STRIDE), :]
        a = a + jnp.dot((xl + pea_ref[l:l + 1, :]).astype(BF16), wa_ref[l], preferred_element_type=F32)
        b = b + jnp.dot((xl + peb_ref[l:l + 1, :]).astype(BF16), wb_ref[l], preferred_element_type=F32)
    rowi = lax.broadcasted_iota(jnp.int32, a.shape, 0)
    a_prev = jnp.where(rowi == 0, carry_ref[0:1, :], pltpu.roll(a, 1, 0))
    carry_ref[...] = jnp.broadcast_to(a[n - 1:n, :], carry_ref.shape)
    o_ref[0] = _mm_nt(w2t_ref[...], _silu(a_prev + b))


def _cmp_weights(pe, w1, w2):
    eye = jnp.eye(NSA_KV_HEADS, dtype=F32)
    hid = w1.shape[2]
    half = NSA_CMP_STRIDE

    def wpart(w):
        return jnp.einsum('lde,hg->lhdge', w, eye).reshape(half * LANES, NSA_KV_HEADS * hid).astype(BF16)

    def ppart(p):
        return jnp.broadcast_to(p[:, None, :], (half, NSA_KV_HEADS, HEAD_DIM)).reshape(1, half * LANES)

    w2t = jnp.einsum('ed,hg->gdhe', w2, eye).reshape(LANES, NSA_KV_HEADS * hid).astype(BF16)
    return ppart(pe[:half]), ppart(pe[half:]), wpart(w1[:half]), wpart(w1[half:]), w2t


def _compress(x, weights, name):
    N, L, _ = x.shape
    n_ch = L // NSA_CMP_STRIDE
    xc = x[:, :n_ch * NSA_CMP_STRIDE].reshape(N, n_ch, NSA_CMP_STRIDE * LANES)
    tch = _tile(n_ch, 256)
    wspecs = [pl.BlockSpec(w.shape, lambda n, j: (0, 0)) for w in weights]
    return pl.pallas_call(
        _cmp_kernel, grid=(N, n_ch // tch),
        in_specs=[pl.BlockSpec((1, tch, xc.shape[2]), lambda n, j: (n, j, 0))] + wspecs,
        out_specs=pl.BlockSpec((1, LANES, tch), lambda n, j: (n, 0, j)),
        out_shape=jax.ShapeDtypeStruct((N, LANES, n_ch), F32), name=name,
        scratch_shapes=[pltpu.VMEM((8, weights[2].shape[1]), F32)],
        compiler_params=_cparams(("parallel", "arbitrary")))(xc, *weights)


def _compress_paged(cache, pt_flat, n_db, weights, name):
    n_pool, page = cache.shape[:2]
    assert page == LANES
    rows = page // NSA_CMP_STRIDE
    ct = jnp.transpose(cache, (0, 2, 3, 1))
    n_pages = pt_flat.shape[0] // n_db
    n_pg = math.gcd(n_pages, max(1, 256 // rows))
    pea, peb, wa, wb, w2t = weights
    per_pos = lambda a: a.reshape(NSA_CMP_STRIDE, LANES, -1)
    weights = (pea.reshape(NSA_CMP_STRIDE, LANES), peb.reshape(NSA_CMP_STRIDE, LANES), per_pos(wa), per_pos(wb), w2t)
    wspecs = [pl.BlockSpec(w.shape, lambda b, p, pt, nd=w.ndim: (0,) * nd) for w in weights]
    page_spec = lambda j: pl.BlockSpec((1,) + ct.shape[1:],
                                       lambda b, p, pt: (pt[b * n_pages + p * n_pg + j], 0, 0, 0))
    grid_spec = pltpu.PrefetchScalarGridSpec(
        num_scalar_prefetch=1, grid=(n_db, n_pages // n_pg),
        in_specs=[page_spec(j) for j in range(n_pg)] + wspecs,
        out_specs=pl.BlockSpec((1, LANES, n_pg * rows), lambda b, p, pt: (b, 0, p)),
        scratch_shapes=[pltpu.VMEM((n_pg * page, LANES), F32), pltpu.VMEM((8, wa.shape[1]), F32)])
    return pl.pallas_call(
        functools.partial(_cmp_paged_kernel, n_pg=n_pg), grid_spec=grid_spec,
        out_shape=jax.ShapeDtypeStruct((n_db, LANES, n_pages * rows), F32), name=name,
        compiler_params=_cparams(("parallel", "arbitrary")))(pt_flat, *([ct] * n_pg), *weights)


def _slc_map_rows(n_ch, ns_pad):
    i = (np.arange(n_ch)[:, None] - 1) * NSA_CMP_STRIDE
    j = np.arange(ns_pad)[None, :] * NSA_SLC_BLOCK
    shared = np.minimum(i + NSA_CMP_LEN, j + NSA_SLC_BLOCK) - np.maximum(i, j)
    m = np.clip(shared, 0, None) / NSA_CMP_LEN
    m[0, :] = 0.0
    return m.astype(np.float32)


def _select_blocks(imp, qpos, ns, n_sel):
    blk = lax.broadcasted_iota(jnp.int32, imp.shape, 1)
    cur = qpos // NSA_SLC_BLOCK
    valid = blk * NSA_SLC_BLOCK <= qpos
    forced = (blk == 0) | (blk == cur) | (blk == cur - 1)
    score = jnp.where(valid, imp + jnp.where(forced, NSA_FORCE_BONUS, 0.0), NEG)
    rank = jnp.zeros(imp.shape, jnp.int32)
    for jp in range(ns):
        sj = score[:, jp:jp + 1]
        beats = (sj > score) | ((sj == score) & (blk > jp))
        rank = rank + beats.astype(jnp.int32)
    return rank < n_sel


def _select_blocks_cols(imp, qpos, ns, n_sel):
    blk = lax.broadcasted_iota(jnp.int32, imp.shape, 0)
    cur = qpos // NSA_SLC_BLOCK
    valid = blk * NSA_SLC_BLOCK <= qpos
    forced = (blk == 0) | (blk == cur) | (blk == cur - 1)
    score = jnp.where(valid, imp + jnp.where(forced, NSA_FORCE_BONUS, 0.0), NEG)
    rank = jnp.zeros(imp.shape, jnp.int32)
    for jp in range(ns):
        sj = score[jp:jp + 1, :]
        beats = (sj > score) | ((sj == score) & (blk > jp))
        rank = rank + beats.astype(jnp.int32)
    return rank < n_sel


def _softmax_cols(s, mask):
    sm = jnp.where(mask, s, NEG)
    m = jnp.max(sm, axis=0, keepdims=True)
    p = jnp.where(mask, jnp.exp(sm - m), 0.0)
    l = jnp.sum(p, axis=0, keepdims=True)
    return p / jnp.where(l > 0.0, l, 1.0)


def _nsa_kernel(qt_ref, kct_ref, vct_ref, ks_ref, vst_ref, kw_ref, vwt_ref, gt_ref, mapt_ref, o_ref,
                qaug_ref, negm_ref, m_ref, l_ref, acc_ref, *, ns, n_sel, tk, win):
    g, i = pl.program_id(1), pl.program_id(2)
    tq = qt_ref.shape[2]
    nl = NSA_GROUP * tq
    n_ch = kct_ref.shape[3]
    per_tile = tk // NSA_SLC_BLOCK
    st = i * tq
    qpos1 = st + lax.broadcasted_iota(jnp.int32, (1, tq), 1)
    rep = lambda a: jnp.concatenate([a] * NSA_GROUP, axis=1)
    qpos = rep(qpos1)

    q4t = jnp.concatenate([qt_ref[0, j * HEAD_DIM:(j + 1) * HEAD_DIM, :] for j in range(NSA_GROUP)], axis=1)
    row = lax.broadcasted_iota(jnp.int32, (LANES, nl), 0)
    qaug_ref[0:LANES, :] = jnp.where(row // HEAD_DIM == g, jnp.concatenate([q4t, q4t], axis=0),
                                     jnp.zeros((LANES, nl), BF16))
    qaug_ref[LANES:2 * LANES, :] = jnp.zeros((LANES, nl), BF16)

    cidx = lax.broadcasted_iota(jnp.int32, (n_ch, tq), 0)
    cmask = rep(((cidx - 1) * NSA_CMP_STRIDE + NSA_CMP_LEN - 1 <= qpos1) & (cidx >= 1))
    sc = lax.dot_general(kct_ref[0, 0].astype(BF16), q4t, (((0,), (0,)), ((), ())), preferred_element_type=F32)
    pc = _softmax_cols(sc, cmask)
    oc = _mm(vct_ref[0, 0], pc)
    pstack = jnp.concatenate([pc[:, j * tq:(j + 1) * tq] for j in range(NSA_GROUP)], axis=0)
    imp = jnp.dot(mapt_ref[...], pstack.astype(BF16), preferred_element_type=F32)[0:HEAD_DIM]
    sel = _select_blocks_cols(imp, qpos1, ns, n_sel)

    negm_ref[...] = rep(jnp.where(sel, 0.0, -MASK_BIG))
    _online_init(m_ref, l_ref, acc_ref)
    onehot = ((lax.broadcasted_iota(jnp.int32, (tk, LANES), 0) // NSA_SLC_BLOCK)
              == lax.broadcasted_iota(jnp.int32, (tk, LANES), 1)).astype(BF16)

    def tile(kt, diagonal):
        k0 = pl.multiple_of(kt * tk, tk)
        nm = negm_ref[pl.ds(pl.multiple_of(kt * per_tile, per_tile), per_tile), :]
        qaug_ref[LANES:LANES + 16, :] = jnp.concatenate(
            [nm, jnp.zeros((16 - per_tile, nl), F32)], axis=0).astype(BF16)
        kaug = jnp.concatenate([ks_ref[0, pl.ds(k0, tk), :], onehot], axis=1)
        s = jnp.dot(kaug, qaug_ref[...], preferred_element_type=F32)
        if diagonal:
            kpos = k0 + lax.broadcasted_iota(jnp.int32, (tk, nl), 0)
            s = jnp.where(kpos <= qpos, s, NEG)
        m_prev = m_ref[...]
        m_new = jnp.maximum(m_prev, jnp.max(s, axis=0, keepdims=True))
        alpha = jnp.exp(m_prev - m_new)
        p = jnp.exp(s - m_new)
        l_ref[...] = alpha * l_ref[...] + jnp.sum(p, axis=0, keepdims=True)
        acc_ref[...] = alpha * acc_ref[...] + jnp.dot(vst_ref[0, :, pl.ds(k0, tk)], p.astype(BF16),
                                                      preferred_element_type=F32)
        m_ref[...] = m_new

    last = st // tk

    def body(kt, carry):
        tile(kt, False)
        return carry

    lax.fori_loop(0, last, body, 0)
    tile(last, True)
    osl = acc_ref[...] / l_ref[...]

    w0 = pl.multiple_of(jnp.maximum(st + tq - win, 0), tq)
    dist = qpos1 - (w0 + lax.broadcasted_iota(jnp.int32, (win, tq), 0))
    sw = jnp.dot(kw_ref[0, pl.ds(w0, win), :], qaug_ref[0:LANES, :], preferred_element_type=F32)
    pw = _softmax_cols(sw, rep((dist >= 0) & (dist <= NSA_WINDOW)))
    ow = jnp.dot(vwt_ref[0, :, pl.ds(w0, win)], pw.astype(BF16), preferred_element_type=F32)

    for j in range(NSA_GROUP):
        base = (NSA_GROUP * g + j) * 3
        gate = lambda r: gt_ref[0, pl.ds(base + r, 1), :]
        cols = slice(j * tq, (j + 1) * tq)
        o = gate(0) * oc[:, cols] + gate(1) * osl[:, cols] + gate(2) * ow[:, cols]
        o_ref[0, j * HEAD_DIM:(j + 1) * HEAD_DIM, :] = o.astype(BF16)


def _nsa_prompt(qt_b, kct, vct, ks_b, vst_b, kw_b, vwt_b, gates_t):
    B, W, S = qt_b.shape
    n_ch = kct.shape[3]
    ns = -(-S // NSA_SLC_BLOCK)
    assert ns <= HEAD_DIM
    n_sel = min(NSA_TOPN, ns)
    tq = _tile(S, 128)
    tk = _tile(S, 512)
    assert tk // NSA_SLC_BLOCK <= 16
    win = min(NSA_WINDOW + tq, S)
    nl = NSA_GROUP * tq
    smap_t = jnp.asarray(np.tile(_slc_map_rows(n_ch, LANES).T, (1, NSA_GROUP)), dtype=BF16)
    per_head = lambda a: pl.BlockSpec((1, 1) + a.shape[2:], lambda b, g, i: (b, g, 0, 0))
    gw = W // NSA_KV_HEADS
    return pl.pallas_call(
        functools.partial(_nsa_kernel, ns=ns, n_sel=n_sel, tk=tk, win=win),
        grid=(B, NSA_KV_HEADS, S // tq),
        in_specs=[pl.BlockSpec((1, gw, tq), lambda b, g, i: (b, g, i)),
                  per_head(kct), per_head(vct),
                  pl.BlockSpec((1, S, LANES), lambda b, g, i: (b, 0, 0)),
                  pl.BlockSpec((1, HEAD_DIM, S), lambda b, g, i: (b, g, 0)),
                  pl.BlockSpec((1, S, LANES), lambda b, g, i: (b, 0, 0)),
                  pl.BlockSpec((1, HEAD_DIM, S), lambda b, g, i: (b, g, 0)),
                  pl.BlockSpec((1, gates_t.shape[1], tq), lambda b, g, i: (b, 0, i)),
                  pl.BlockSpec(smap_t.shape, lambda b, g, i: (0, 0))],
        out_specs=pl.BlockSpec((1, gw, tq), lambda b, g, i: (b, g, i)),
        out_shape=jax.ShapeDtypeStruct(qt_b.shape, BF16), name="nsa_prompt",
        scratch_shapes=[pltpu.VMEM((2 * LANES, nl), BF16), pltpu.VMEM((HEAD_DIM, nl), F32),
                        pltpu.VMEM((1, nl), F32), pltpu.VMEM((1, nl), F32), pltpu.VMEM((HEAD_DIM, nl), F32)],
        compiler_params=_cparams(("parallel", "parallel", "arbitrary")))(
            qt_b, kct, vct, ks_b, vst_b, kw_b, vwt_b, gates_t, smap_t)


def _outproj_kernel(*refs):
    x_ref, w_ref, y_ref = refs[0], refs[-2], refs[-1]
    y = x_ref[...]
    k0 = 0
    for o_ref in refs[1:-2]:
        if len(o_ref.shape) == 3:
            kw = o_ref.shape[1]
            y = y + lax.dot_general(o_ref[0].astype(BF16), w_ref[k0:k0 + kw, :], (((0,), (0,)), ((), ())),
                                    preferred_element_type=F32)
        else:
            kw = o_ref.shape[1]
            y = y + jnp.dot(o_ref[...].astype(BF16), w_ref[k0:k0 + kw, :], preferred_element_type=F32)
        k0 += kw
    y_ref[...] = y


def _outproj(x2d, parts, w_bf, name, tm=None):
    M, D = x2d.shape
    tm = tm or _tile(M, 512)
    specs = []
    for o in parts:
        if o.ndim == 3:
            npb = o.shape[2] // tm
            specs.append(pl.BlockSpec((1, o.shape[1], tm), lambda i, npb=npb: (i // npb, 0, i % npb)))
        else:
            specs.append(pl.BlockSpec((tm, o.shape[1]), lambda i: (i, 0)))
    return pl.pallas_call(
        _outproj_kernel, grid=(M // tm,),
        in_specs=[pl.BlockSpec((tm, D), lambda i: (i, 0))] + specs + [pl.BlockSpec(w_bf.shape, lambda i: (0, 0))],
        out_specs=pl.BlockSpec((tm, D), lambda i: (i, 0)),
        out_shape=jax.ShapeDtypeStruct((M, D), F32), name=name,
        compiler_params=_cparams(("parallel",)))(x2d, *parts, w_bf)


def _ffn_kernel(x_ref, g_ref, wg_ref, wu_ref, wd_ref, y_ref, xn_ref):
    f = pl.program_id(1)

    @pl.when(f == 0)
    def _():
        x = x_ref[...]
        xn_ref[...] = _rmsnorm(x, g_ref[...]).astype(BF16)
        y_ref[...] = x

    xn = xn_ref[...]
    h = _silu(jnp.dot(xn, wg_ref[...], preferred_element_type=F32)) * \
        jnp.dot(xn, wu_ref[...], preferred_element_type=F32)
    y_ref[...] += jnp.dot(h.astype(BF16), wd_ref[...], preferred_element_type=F32)


def _ffn(x2d, gain, wg, wu, wd, name):
    M, D = x2d.shape
    Fd = wg.shape[1]
    tm = _tile(M, 512)
    nf = 2 if Fd % (2 * LANES) == 0 else 1
    fc = Fd // nf
    return pl.pallas_call(
        _ffn_kernel, grid=(M // tm, nf),
        in_specs=[pl.BlockSpec((tm, D), lambda i, f: (i, 0)), pl.BlockSpec((1, D), lambda i, f: (0, 0)),
                  pl.BlockSpec((D, fc), lambda i, f: (0, f)), pl.BlockSpec((D, fc), lambda i, f: (0, f)),
                  pl.BlockSpec((fc, D), lambda i, f: (f, 0))],
        out_specs=pl.BlockSpec((tm, D), lambda i, f: (i, 0)),
        out_shape=jax.ShapeDtypeStruct((M, D), F32), name=name,
        scratch_shapes=[pltpu.VMEM((tm, D), BF16)],
        compiler_params=_cparams(("parallel", "arbitrary")))(x2d, gain.reshape(1, D), wg, wu, wd)


def _moe_kernel(x_ref, g_ref, wrt_ref, wg_ref, wu_ref, wd_ref, gf_ref, y_ref, xn_ref, rank_ref, comb_ref,
                rankc_ref, *, n_exp, chunk):
    e = pl.program_id(1)
    tm = x_ref.shape[0]

    @pl.when(e == 0)
    def _():
        xn = _rmsnorm(x_ref[...], g_ref[...])
        xn_ref[...] = xn.astype(BF16)
        xh, xm, _ = _split3(xn)
        wh, wm, _ = _split3(wrt_ref[...])
        dn = lambda a, b: lax.dot_general(a, b, (((1,), (1,)), ((), ())), preferred_element_type=F32)
        row = lax.broadcasted_iota(jnp.int32, (LANES, tm), 0)
        logits = jnp.where(row < n_exp, dn(wh, xh) + dn(wh, xm) + dn(wm, xh), NEG)
        v1 = jnp.max(logits, axis=0, keepdims=True)
        i1 = jnp.min(jnp.where(logits == v1, row, LANES), axis=0, keepdims=True)
        rest = jnp.where(row == i1, NEG, logits)
        v2 = jnp.max(rest, axis=0, keepdims=True)
        i2 = jnp.min(jnp.where(rest == v2, row, LANES), axis=0, keepdims=True)
        ex = jnp.exp(v2 - v1)
        comb = jnp.where(row == i1, 1.0 / (1.0 + ex), jnp.where(row == i2, ex / (1.0 + ex), 0.0))
        member = ((row == i1) | (row == i2))[0:SUBLANES]
        before = (lax.broadcasted_iota(jnp.int32, (tm, tm), 0)
                  < lax.broadcasted_iota(jnp.int32, (tm, tm), 1)).astype(BF16)
        rank = jnp.dot(jnp.where(member, 1.0, 0.0).astype(BF16), before, preferred_element_type=F32)
        rank = jnp.where(member, rank, -1.0)
        rank_ref[...] = rank
        comb_ref[...] = comb[0:SUBLANES]
        rankc_ref[...] = jnp.transpose(rank)
        y_ref[...] = jnp.zeros_like(y_ref)

    rrow = rank_ref[pl.ds(e, 1), :]
    crow = comb_ref[pl.ds(e, 1), :]
    rc = rankc_ref[...]
    rcol = jnp.sum(jnp.where(lax.broadcasted_iota(jnp.int32, rc.shape, 1) == e, rc, 0.0), axis=1, keepdims=True)
    n_tok = jnp.max(rrow).astype(jnp.int32) + 1

    def body(c, carry):
        base = (c * chunk).astype(F32)
        pick = rrow == lax.broadcasted_iota(jnp.int32, (chunk, tm), 0).astype(F32) + base
        xg = jnp.dot(jnp.where(pick, 1.0, 0.0).astype(BF16), xn_ref[...],
                     preferred_element_type=F32).astype(BF16)
        wcol = jnp.sum(jnp.where(pick, crow, 0.0), axis=1, keepdims=True)
        h = _silu(jnp.dot(xg, wg_ref[0], preferred_element_type=F32)) * \
            jnp.dot(xg, wu_ref[0], preferred_element_type=F32)
        yv = jnp.dot((h * wcol).astype(BF16), wd_ref[0], preferred_element_type=F32)
        place = jnp.where(rcol == lax.broadcasted_iota(jnp.int32, (tm, chunk), 1).astype(F32) + base,
                          1.0, 0.0).astype(BF16)
        y_ref[...] += jnp.dot(place, yv.astype(BF16), preferred_element_type=F32)
        return carry

    lax.fori_loop(0, (n_tok + chunk - 1) // chunk, body, 0)

    @pl.when(e == n_exp - 1)
    def _():
        y_ref[...] = _rmsnorm(x_ref[...] + y_ref[...], gf_ref[...])


def _moe_final(x2d, gain, w_router, wg, wu, wd, gain_final, name):
    M, D = x2d.shape
    n_exp, _, Fd = wg.shape
    assert n_exp <= SUBLANES
    tm = _tile(M, 1024)
    chunk = min(320, tm) if tm >= 1024 else min(128, tm)
    wrt = jnp.zeros((LANES, D), F32).at[:n_exp].set(w_router.T)
    return pl.pallas_call(
        functools.partial(_moe_kernel, n_exp=n_exp, chunk=chunk), grid=(M // tm, n_exp),
        in_specs=[pl.BlockSpec((tm, D), lambda i, e: (i, 0)), pl.BlockSpec((1, D), lambda i, e: (0, 0)),
                  pl.BlockSpec((LANES, D), lambda i, e: (0, 0)),
                  pl.BlockSpec((1, D, Fd), lambda i, e: (e, 0, 0)),
                  pl.BlockSpec((1, D, Fd), lambda i, e: (e, 0, 0)),
                  pl.BlockSpec((1, Fd, D), lambda i, e: (e, 0, 0)),
                  pl.BlockSpec((1, D), lambda i, e: (0, 0))],
        out_specs=pl.BlockSpec((tm, D), lambda i, e: (i, 0)),
        out_shape=jax.ShapeDtypeStruct((M, D), F32), name=name,
        scratch_shapes=[pltpu.VMEM((tm, D), BF16), pltpu.VMEM((SUBLANES, tm), F32),
                        pltpu.VMEM((SUBLANES, tm), F32), pltpu.VMEM((tm, SUBLANES), F32)],
        compiler_params=_cparams(("parallel", "arbitrary")))(
            x2d, gain.reshape(1, D), wrt, wg, wu, wd, gain_final.reshape(1, D))


def _band_kernel(*refs, n_hp, dil, span, has_prev):
    group = lambda i: refs[i * n_hp:(i + 1) * n_hp]
    if has_prev:
        q_refs, kp_refs, kc_refs, vp_refs, vc_refs = [group(i) for i in range(5)]
        bias_ref, o_ref, lse_ref, o_scr = refs[5 * n_hp:]
    else:
        q_refs, kc_refs, vc_refs = [group(i) for i in range(3)]
        bias_ref, o_ref, lse_ref, o_scr = refs[3 * n_hp:]
    t, hs = pl.program_id(1), pl.program_id(2)
    tq = q_refs[0].shape[1] // dil
    if has_prev:
        rowk = lax.broadcasted_iota(jnp.int32, (span + tq, tq), 0)
        bias = bias_ref[...] + jnp.where((rowk < span) & (t == 0), NEG, 0.0)
    else:
        bias = bias_ref[span:, :]
    row = lax.broadcasted_iota(jnp.int32, (LANES, tq), 0)
    lane = lax.broadcasted_iota(jnp.int32, (tq, LANES), 1)
    first_head = hs * (2 * n_hp)

    @pl.when(hs == 0)
    def _():
        lse_ref[...] = jnp.zeros_like(lse_ref)

    def stream(r, carry):
        rows = lambda n: pl.ds(r, n, stride=dil) if dil > 1 else pl.ds(0, n)
        lses = []
        for j in range(n_hp):
            qT = jnp.transpose(q_refs[j][0, rows(tq), :]).astype(BF16)
            k, v = kc_refs[j][0, rows(tq), :], vc_refs[j][0, rows(tq), :]
            if has_prev:
                k = jnp.concatenate([kp_refs[j][0, rows(span), :], k], axis=0)
                v = jnp.concatenate([vp_refs[j][0, rows(span), :], v], axis=0)
            k = k.astype(BF16)
            vT = jnp.transpose(v).astype(BF16)
            outs = []
            for h in range(2):
                qpad = jnp.where(row // HEAD_DIM == h, qT, jnp.zeros_like(qT))
                s = jnp.dot(k, qpad, preferred_element_type=F32) + bias
                m = jnp.max(s, axis=0, keepdims=True)
                p = jnp.exp(s - m)
                l = jnp.sum(p, axis=0, keepdims=True)
                outs.append(jnp.dot(vT[h * HEAD_DIM:(h + 1) * HEAD_DIM], p.astype(BF16),
                                    preferred_element_type=F32) / l)
                lses.append(m + jnp.log(l))
            o_scr[j, rows(tq), :] = jnp.transpose(jnp.concatenate(outs, axis=0))
        stat = jnp.transpose(jnp.concatenate(lses + [jnp.zeros((LANES - len(lses), tq), F32)], axis=0))
        stat = pltpu.roll(stat, first_head, 1)
        mine = (lane >= first_head) & (lane < first_head + 2 * n_hp)
        lse_ref[0, rows(tq), :] = jnp.where(mine, stat, lse_ref[0, rows(tq), :])
        return carry

    lax.fori_loop(0, dil, stream, 0)
    for j in range(n_hp):
        o_ref[0, :, j * LANES:(j + 1) * LANES] = o_scr[j]


def _band_attention(q, k, v, dil, span, tq, name, n_hp=2):
    B, S, W = q.shape
    L = S // dil
    tq = min(tq, L)
    assert L % tq == 0 and tq % span == 0
    per_tile = tq // span
    n_pairs = W // LANES
    assert n_pairs % n_hp == 0
    i = np.arange(span + tq)[:, None]
    j = np.arange(tq)[None, :]
    bias = jnp.asarray(np.where((j - i + span >= 0) & (j - i + span <= span), 0.0, NEG), F32)
    cur = lambda jj: pl.BlockSpec((1, dil * tq, LANES), lambda b, t, hs: (b, t, hs * n_hp + jj))
    prev = lambda jj: pl.BlockSpec((1, dil * span, LANES),
                                   lambda b, t, hs: (b, jnp.maximum(t * per_tile - 1, 0), hs * n_hp + jj))
    each = lambda mk: [mk(jj) for jj in range(n_hp)]
    has_prev = L > tq
    if has_prev:
        specs = each(cur) + each(prev) + each(cur) + each(prev) + each(cur)
        args = [q] * n_hp + [k] * (2 * n_hp) + [v] * (2 * n_hp)
    else:
        specs = each(cur) * 3
        args = [q] * n_hp + [k] * n_hp + [v] * n_hp
    res = pl.pallas_call(
        functools.partial(_band_kernel, n_hp=n_hp, dil=dil, span=span, has_prev=has_prev),
        grid=(B, L // tq, n_pairs // n_hp),
        in_specs=specs + [pl.BlockSpec(bias.shape, lambda b, t, hs: (0, 0))],
        out_specs=[pl.BlockSpec((1, dil * tq, n_hp * LANES), lambda b, t, hs: (b, t, hs)),
                   pl.BlockSpec((1, dil * tq, LANES), lambda b, t, hs: (b, t, 0))],
        out_shape=[jax.ShapeDtypeStruct((B, S, W), F32), jax.ShapeDtypeStruct((B, S, LANES), F32)],
        scratch_shapes=[pltpu.VMEM((n_hp, dil * tq, LANES), F32)],
        name=name, compiler_params=_cparams(("parallel", "parallel", "arbitrary")))(*args, bias)
    return res[0].reshape(B * S, W), res[1].reshape(B * S, LANES)


def _merge_outproj_kernel(*refs, n_groups):
    x_ref = refs[0]
    o_refs, l_refs = refs[1:1 + n_groups], refs[1 + n_groups:1 + 2 * n_groups]
    w_ref, y_ref = refs[-2], refs[-1]
    W = o_refs[0].shape[1]
    lses = [r[...] for r in l_refs]
    m = functools.reduce(jnp.maximum, lses)
    es = [jnp.exp(l - m) for l in lses]
    tot = functools.reduce(lambda a, b: a + b, es)
    expand = (lax.broadcasted_iota(jnp.int32, (LANES, W), 1) // HEAD_DIM
              == lax.broadcasted_iota(jnp.int32, (LANES, W), 0)).astype(BF16)
    mix = None
    for e, o_ref in zip(es, o_refs):
        hi, mid, _ = _split3(e / tot)
        wexp = jnp.dot(hi, expand, preferred_element_type=F32) + jnp.dot(mid, expand, preferred_element_type=F32)
        mix = wexp * o_ref[...] if mix is None else mix + wexp * o_ref[...]
    y_ref[...] = x_ref[...] + jnp.dot(mix.astype(BF16), w_ref[...], preferred_element_type=F32)


def _merge_outproj(x2d, outs, lses, w_bf, name):
    M, D = x2d.shape
    W = outs[0].shape[1]
    tm = _tile(M, 256)
    row = lambda n: pl.BlockSpec((tm, n), lambda i: (i, 0))
    return pl.pallas_call(
        functools.partial(_merge_outproj_kernel, n_groups=len(outs)), grid=(M // tm,),
        in_specs=[row(D)] + [row(W)] * len(outs) + [row(LANES)] * len(lses)
        + [pl.BlockSpec(w_bf.shape, lambda i: (0, 0))],
        out_specs=row(D), out_shape=jax.ShapeDtypeStruct((M, D), F32), name=name,
        compiler_params=_cparams(("parallel",)))(x2d, *outs, *lses, w_bf)


def _head_major(x, n_heads, pad_rows):
    DB, T, _ = x.shape
    y = x.reshape(DB, T, n_heads, HEAD_DIM).transpose(0, 2, 1, 3)
    return jnp.pad(y, ((0, 0), (0, 0), (0, pad_rows - T), (0, 0)))


def _new_cols(x, n_heads):
    DB, T, _ = x.shape
    y = x.reshape(DB, T, n_heads, HEAD_DIM).transpose(0, 2, 3, 1)
    return jnp.pad(y, ((0, 0), (0, 0), (0, 0), (0, NEW_PAD - T)))


def _row_of(col_vec_row):
    return jnp.transpose(jnp.broadcast_to(col_vec_row, (SUBLANES, LANES)))[0:SUBLANES, 0:1]


def _fox_dec_kernel(pt_ref, *refs, n_pg, n_q):
    kts, vts, lfs = refs[:n_pg], refs[n_pg:2 * n_pg], refs[2 * n_pg:3 * n_pg]
    q_ref, knt_ref, vnt_ref, lfn_ref, o_ref, qbd_ref, base_ref, carry_ref, m_ref, l_ref, acc_ref = refs[3 * n_pg:]
    p = pl.program_id(1)
    nh = FOX_HEADS
    page = LANES
    rows, width = nh * QPAD, nh * HEAD_DIM
    lane = lax.broadcasted_iota(jnp.int32, (QPAD, LANES), 1)
    trow = lax.broadcasted_iota(jnp.int32, (QPAD, LANES), 0)
    r = lax.broadcasted_iota(jnp.int32, (page, page), 0)
    c = lax.broadcasted_iota(jnp.int32, (page, page), 1)
    stack = lambda xs: jnp.concatenate(xs, axis=0)
    own = (lax.broadcasted_iota(jnp.int32, (rows, width), 0) // QPAD
           == lax.broadcasted_iota(jnp.int32, (rows, width), 1) // HEAD_DIM)
    heads_of = lambda ref: ref[0].reshape(width, ref.shape[3])

    @pl.when(p == 0)
    def _():
        _online_init(m_ref, l_ref, acc_ref)
        carry_ref[...] = jnp.zeros_like(carry_ref)
        q_all = q_ref[0].reshape(rows, HEAD_DIM) * SCALE
        qbd_ref[...] = jnp.where(own, jnp.concatenate([q_all] * nh, axis=1), 0.0).astype(BF16)
        cnew = _mm3_left(lfn_ref[0], r <= c)
        bases, decs = [], []
        for h in range(nh):
            bh = _row_of(cnew[h:h + 1, :])
            bases.append(bh)
            decs.append(bh - cnew[h:h + 1, :])
        base_ref[...] = jnp.broadcast_to(stack(bases), base_ref.shape)
        live = (lane <= trow) & (lane < n_q)
        s = jnp.dot(qbd_ref[...], heads_of(knt_ref).astype(BF16), preferred_element_type=F32) + stack(decs)
        s = jnp.where(stack([live] * nh), s, NEG)
        alpha, pr = _online_step(s, m_ref, l_ref)
        acc_ref[...] = alpha * acc_ref[...] + _mm_nt(pr, heads_of(vnt_ref))

    lf_all = stack([lf[0] for lf in lfs])
    after_all = _mm3_left(lf_all, r > c)
    tot_all = after_all[:, 0:1] + lf_all[:, 0:1]
    carry = carry_ref[:, 0:1]
    decay = [None] * n_pg
    for j in reversed(range(n_pg)):
        decay[j] = carry + after_all[j * nh:(j + 1) * nh]
        carry = carry + tot_all[j * nh:(j + 1) * nh]
    carry_ref[...] = jnp.broadcast_to(carry, carry_ref.shape)
    per_row = lambda d: stack([jnp.broadcast_to(d[h:h + 1, :], (QPAD, page)) for h in range(nh)])
    s = jnp.concatenate([jnp.dot(qbd_ref[...], heads_of(kts[j]).astype(BF16), preferred_element_type=F32)
                         + per_row(decay[j]) for j in range(n_pg)], axis=1) + base_ref[:, 0:1]
    alpha, pr = _online_step(s, m_ref, l_ref)
    acc_ref[...] = alpha * acc_ref[...] + sum(_mm_nt(pr[:, j * page:(j + 1) * page], heads_of(vts[j]))
                                              for j in range(n_pg))

    @pl.when(p == pl.num_programs(1) - 1)
    def _():
        acc = jnp.where(own, acc_ref[...], 0.0)
        o_ref[0] = sum(acc[:, h * HEAD_DIM:(h + 1) * HEAD_DIM] for h in range(nh)) / l_ref[...]


def _fox_decode(q, k_new, v_new, lf_new, cache_k, cache_v, cache_lf, pt_flat):
    DB, T, W = q.shape
    n_pool, page, nh, _ = cache_k.shape
    assert page == LANES and T <= QPAD
    n_pages = pt_flat.shape[0] // DB
    n_pg = math.gcd(n_pages, PAGES_PER_STEP)
    n_steps = n_pages // n_pg
    ckt = jnp.transpose(cache_k, (0, 2, 3, 1))
    cvt = jnp.transpose(cache_v, (0, 2, 3, 1))
    clf = jnp.transpose(cache_lf, (0, 2, 1))
    qh = _head_major(q, nh, QPAD)
    knt, vnt = _new_cols(k_new, nh), _new_cols(v_new, nh)
    lfn = jnp.pad(jnp.transpose(lf_new, (0, 2, 1)), ((0, 0), (0, 0), (0, LANES - T)))
    page_idx = lambda j: (lambda b, p, pt: (pt[b * n_pages + (n_steps - 1 - p) * n_pg + j], 0, 0, 0))
    lf_idx = lambda j: (lambda b, p, pt: (pt[b * n_pages + (n_steps - 1 - p) * n_pg + j], 0, 0))
    per_db = lambda a: pl.BlockSpec((1,) + a.shape[1:], lambda b, p, pt: (b,) + (0,) * (a.ndim - 1))
    rows = nh * QPAD
    grid_spec = pltpu.PrefetchScalarGridSpec(
        num_scalar_prefetch=1, grid=(DB, n_steps),
        in_specs=([pl.BlockSpec((1, nh, HEAD_DIM, page), page_idx(j)) for j in range(n_pg)] * 2
                  + [pl.BlockSpec((1, nh, page), lf_idx(j)) for j in range(n_pg)]
                  + [per_db(qh), per_db(knt), per_db(vnt), per_db(lfn)]),
        out_specs=pl.BlockSpec((1, rows, HEAD_DIM), lambda b, p, pt: (b, 0, 0)),
        scratch_shapes=[pltpu.VMEM((rows, W), BF16), pltpu.VMEM((rows, LANES), F32), pltpu.VMEM((nh, LANES), F32),
                        pltpu.VMEM((rows, 1), F32), pltpu.VMEM((rows, 1), F32), pltpu.VMEM((rows, W), F32)])
    o = pl.pallas_call(
        functools.partial(_fox_dec_kernel, n_pg=n_pg, n_q=T), grid_spec=grid_spec,
        out_shape=jax.ShapeDtypeStruct((DB, rows, HEAD_DIM), F32), name="fox_decode",
        compiler_params=_cparams(("parallel", "arbitrary")))(
            pt_flat, *([ckt] * n_pg), *([cvt] * n_pg), *([clf] * n_pg), qh, knt, vnt, lfn)
    return o.reshape(DB, nh, QPAD, HEAD_DIM)[:, :, :T].transpose(0, 2, 1, 3).reshape(DB, T, W)


def _nsa_dec_a_kernel(q_ref, kct_ref, vct_ref, kwt_ref, vwt_ref, kwnt_ref, vwnt_ref, gt_ref, map_ref,
                      o_ref, bias_ref, *, n_q, ns, n_sel, past, n_keys):
    nr = NSA_GROUP * QPAD
    n_ch = kct_ref.shape[3]
    wb = kwt_ref.shape[3]
    t_row = lax.broadcasted_iota(jnp.int32, (nr, 1), 0) % QPAD
    imps = []
    for kv in range(NSA_KV_HEADS):
        q = (q_ref[0, kv] * SCALE).astype(BF16)
        cidx = lax.broadcasted_iota(jnp.int32, (nr, n_ch), 1)
        pc = _softmax_rows(_mm(q, kct_ref[0, kv]), cidx >= 1)
        oc = _mm_nt(pc, vct_ref[0, kv])
        pcat = jnp.concatenate([pc[j * QPAD:(j + 1) * QPAD] for j in range(NSA_GROUP)], axis=1)
        imps.append(jnp.dot(pcat.astype(BF16), map_ref[...], preferred_element_type=F32))
        sw = _mm(q, kwt_ref[0, kv])
        sn = _mm(q, kwnt_ref[0, kv])
        iw = lax.broadcasted_iota(jnp.int32, (nr, wb), 1)
        un = lax.broadcasted_iota(jnp.int32, (nr, NEW_PAD), 1)
        mw = (wb + t_row - iw <= NSA_WINDOW) & (past - wb + iw >= 0)
        mn = (un <= t_row) & (un < n_q)
        sw = jnp.where(mw, sw, NEG)
        sn = jnp.where(mn, sn, NEG)
        m = jnp.maximum(jnp.max(sw, axis=1, keepdims=True), jnp.max(sn, axis=1, keepdims=True))
        pw = jnp.where(mw, jnp.exp(sw - m), 0.0)
        pn = jnp.where(mn, jnp.exp(sn - m), 0.0)
        l = jnp.sum(pw, axis=1, keepdims=True) + jnp.sum(pn, axis=1, keepdims=True)
        ow = (_mm_nt(pw, vwt_ref[0, kv]) + _mm_nt(pn, vwnt_ref[0, kv])) / l
        gt = gt_ref[0, kv]
        o_ref[0, kv] = gt[:, 0:1] * oc + gt[:, 2:3] * ow

    imp = jnp.concatenate(imps, axis=0)
    t_sel = lax.broadcasted_iota(jnp.int32, (imp.shape[0], 1), 0) % QPAD
    sel = _select_blocks(imp, past + t_sel, ns, n_sel).astype(BF16)
    ch = 8 * LANES
    for c0 in range(0, n_keys, ch):
        w = min(ch, n_keys - c0)
        key = c0 + lax.broadcasted_iota(jnp.int32, (sel.shape[1], w), 1)
        onehot = (key // NSA_SLC_BLOCK == lax.broadcasted_iota(jnp.int32, (sel.shape[1], w), 0)).astype(BF16)
        chosen = jnp.dot(sel, onehot, preferred_element_type=F32) > 0.5
        kpos = c0 + lax.broadcasted_iota(jnp.int32, (sel.shape[0], w), 1)
        ok = chosen & (kpos <= past + t_sel) & (kpos < past + n_q)
        bias_ref[0, :, c0:c0 + w] = jnp.where(ok, 0.0, NEG)


def _nsa_dec_b_kernel(pt_ref, *refs, n_pg):
    kts, vts = refs[:n_pg], refs[n_pg:2 * n_pg]
    q_ref, knt_ref, vnt_ref, bias_ref, biasn_ref, o_ref, qbd_ref, m_ref, l_ref, acc_ref = refs[2 * n_pg:]
    p = pl.program_id(1)
    nr = NSA_GROUP * QPAD
    page = LANES
    rows, width = NSA_KV_HEADS * nr, NSA_KV_HEADS * HEAD_DIM
    own = (lax.broadcasted_iota(jnp.int32, (rows, width), 0) // nr
           == lax.broadcasted_iota(jnp.int32, (rows, width), 1) // HEAD_DIM)
    heads_of = lambda ref: ref[0].reshape(width, ref.shape[3])
    per_row = lambda b: jnp.concatenate([b[kv * QPAD:(kv + 1) * QPAD] for kv in range(NSA_KV_HEADS)
                                         for _ in range(NSA_GROUP)], axis=0)

    @pl.when(p == 0)
    def _():
        _online_init(m_ref, l_ref, acc_ref)
        q_all = q_ref[0].reshape(rows, HEAD_DIM) * SCALE
        qbd_ref[...] = jnp.where(own, jnp.concatenate([q_all] * NSA_KV_HEADS, axis=1), 0.0).astype(BF16)

    s = jnp.concatenate([jnp.dot(qbd_ref[...], heads_of(kts[j]).astype(BF16), preferred_element_type=F32)
                         for j in range(n_pg)], axis=1) + per_row(bias_ref[0])
    alpha, pr = _online_step(s, m_ref, l_ref)
    acc_ref[...] = alpha * acc_ref[...] + sum(_mm_nt(pr[:, j * page:(j + 1) * page], heads_of(vts[j]))
                                              for j in range(n_pg))

    @pl.when(p == pl.num_programs(1) - 1)
    def _():
        s = jnp.dot(qbd_ref[...], heads_of(knt_ref).astype(BF16), preferred_element_type=F32) + per_row(biasn_ref[0])
        alpha, pr = _online_step(s, m_ref, l_ref)
        acc = jnp.where(own, alpha * acc_ref[...] + _mm_nt(pr, heads_of(vnt_ref)), 0.0)
        o_ref[0] = sum(acc[:, kv * HEAD_DIM:(kv + 1) * HEAD_DIM] for kv in range(NSA_KV_HEADS)) / l_ref[...]


def _nsa_decode(qb, kct, vct, cache_ks, cache_vs, ks_new, vs_new, swa_k, swa_v, kw_new, vw_new,
                gates, pt_flat, past):
    DB, T, W = qb.shape
    n_pool, page = cache_ks.shape[:2]
    assert page == LANES and past % LANES == 0 and T <= QPAD
    n_pages = pt_flat.shape[0] // DB
    n_ch = kct.shape[3]
    ns = -(-(past + T) // NSA_SLC_BLOCK)
    ns_pad = -(-ns // LANES) * LANES
    n_sel = min(NSA_TOPN, ns)
    nr = NSA_GROUP * QPAD
    n_keys = past + NEW_PAD
    q5 = _head_major(qb, NSA_HEADS, QPAD).reshape(DB, NSA_KV_HEADS, nr, HEAD_DIM)
    g5 = _head_major(jnp.pad(gates.reshape(DB, T, NSA_HEADS, 3), ((0, 0),) * 3 + ((0, HEAD_DIM - 3),))
                     .reshape(DB, T, NSA_HEADS * HEAD_DIM), NSA_HEADS, QPAD)[..., :3]
    g5 = g5.reshape(DB, NSA_KV_HEADS, nr, 3)
    smap = jnp.asarray(np.tile(_slc_map_rows(n_ch, ns_pad), (NSA_GROUP, 1)), dtype=BF16)
    kwt = jnp.transpose(swa_k, (0, 2, 3, 1))
    vwt = jnp.transpose(swa_v, (0, 2, 3, 1))
    kwnt, vwnt = _new_cols(kw_new, NSA_KV_HEADS), _new_cols(vw_new, NSA_KV_HEADS)
    per = lambda a: pl.BlockSpec((1,) + a.shape[1:], lambda b: (b,) + (0,) * (a.ndim - 1))
    part, bias = pl.pallas_call(
        functools.partial(_nsa_dec_a_kernel, n_q=T, ns=ns, n_sel=n_sel, past=past, n_keys=n_keys), grid=(DB,),
        in_specs=[per(q5), per(kct), per(vct), per(kwt), per(vwt), per(kwnt), per(vwnt), per(g5),
                  pl.BlockSpec(smap.shape, lambda b: (0, 0))],
        out_specs=[pl.BlockSpec((1, NSA_KV_HEADS, nr, HEAD_DIM), lambda b: (b, 0, 0, 0)),
                   pl.BlockSpec((1, NSA_KV_HEADS * QPAD, n_keys), lambda b: (b, 0, 0))],
        out_shape=[jax.ShapeDtypeStruct((DB, NSA_KV_HEADS, nr, HEAD_DIM), F32),
                   jax.ShapeDtypeStruct((DB, NSA_KV_HEADS * QPAD, n_keys), F32)], name="nsa_decode_a",
        compiler_params=_cparams(("parallel",)))(q5, kct, vct, kwt, vwt, kwnt, vwnt, g5, smap)

    n_pg = math.gcd(n_pages, 2 * PAGES_PER_STEP)
    ckt = jnp.transpose(cache_ks, (0, 2, 3, 1))
    cvt = jnp.transpose(cache_vs, (0, 2, 3, 1))
    ksnt, vsnt = _new_cols(ks_new, NSA_KV_HEADS), _new_cols(vs_new, NSA_KV_HEADS)
    page_idx = lambda j: (lambda b, p, pt: (pt[b * n_pages + p * n_pg + j], 0, 0, 0))
    per_db = lambda a: pl.BlockSpec((1,) + a.shape[1:], lambda b, p, pt: (b,) + (0,) * (a.ndim - 1))
    rows = NSA_KV_HEADS * nr
    grid_spec = pltpu.PrefetchScalarGridSpec(
        num_scalar_prefetch=1, grid=(DB, n_pages // n_pg),
        in_specs=([pl.BlockSpec((1, NSA_KV_HEADS, HEAD_DIM, page), page_idx(j)) for j in range(n_pg)] * 2
                  + [per_db(q5), per_db(ksnt), per_db(vsnt),
                     pl.BlockSpec((1, NSA_KV_HEADS * QPAD, n_pg * page), lambda b, p, pt: (b, 0, p)),
                     pl.BlockSpec((1, NSA_KV_HEADS * QPAD, NEW_PAD), lambda b, p, pt: (b, 0, past // NEW_PAD))]),
        out_specs=pl.BlockSpec((1, rows, HEAD_DIM), lambda b, p, pt: (b, 0, 0)),
        scratch_shapes=[pltpu.VMEM((rows, LANES), BF16), pltpu.VMEM((rows, 1), F32), pltpu.VMEM((rows, 1), F32),
                        pltpu.VMEM((rows, LANES), F32)])
    osl = pl.pallas_call(
        functools.partial(_nsa_dec_b_kernel, n_pg=n_pg), grid_spec=grid_spec,
        out_shape=jax.ShapeDtypeStruct((DB, rows, HEAD_DIM), F32), name="nsa_decode_b",
        compiler_params=_cparams(("parallel", "arbitrary")))(
            pt_flat, *([ckt] * n_pg), *([cvt] * n_pg), q5, ksnt, vsnt, bias, bias)

    o = part + g5[..., 1:2] * osl.reshape(DB, NSA_KV_HEADS, nr, HEAD_DIM)
    return o.reshape(DB, NSA_HEADS, QPAD, HEAD_DIM)[:, :, :T].transpose(0, 2, 1, 3).reshape(DB, T, W)


def _dil_dec_kernel(q_ref, kt_ref, vt_ref, knt_ref, vnt_ref, o_ref, *, n_q, wc):
    hb = kt_ref.shape[1]
    t = lax.broadcasted_iota(jnp.int32, (QPAD, 1), 0)

    def log_mult(d, ok):
        w = jnp.zeros(d.shape, F32)
        for window, dil in DIL_PAIRS:
            w = w + ((d >= 0) & (d <= window) & (d % dil == 0)).astype(F32)
        return jnp.where(ok, w, 0.0)

    wk = log_mult(wc + t - lax.broadcasted_iota(jnp.int32, (QPAD, wc), 1), t < n_q)
    un = lax.broadcasted_iota(jnp.int32, (QPAD, NEW_PAD), 1)
    wn = log_mult(t - un, (t < n_q) & (un < n_q))
    for h in range(hb):
        q = (q_ref[0, h] * SCALE).astype(BF16)
        sk = jnp.where(wk > 0.0, _mm(q, kt_ref[0, h]), NEG)
        sn = jnp.where(wn > 0.0, _mm(q, knt_ref[0, h]), NEG)
        m = jnp.maximum(jnp.max(sk, axis=1, keepdims=True), jnp.max(sn, axis=1, keepdims=True))
        pk = wk * jnp.exp(sk - m)
        pn = wn * jnp.exp(sn - m)
        l = jnp.sum(pk, axis=1, keepdims=True) + jnp.sum(pn, axis=1, keepdims=True)
        o_ref[0, h] = (_mm_nt(pk, vt_ref[0, h]) + _mm_nt(pn, vnt_ref[0, h])) / jnp.where(l > 0.0, l, 1.0)


def _dilated_decode(q, k_new, v_new, cache_k, cache_v):
    DB, T, W = q.shape
    wc, nh = cache_k.shape[1], cache_k.shape[2]
    ckt = jnp.transpose(cache_k, (0, 2, 3, 1))
    cvt = jnp.transpose(cache_v, (0, 2, 3, 1))
    qh = _head_major(q, nh, QPAD)
    knt, vnt = _new_cols(k_new, nh), _new_cols(v_new, nh)
    hb = math.gcd(nh, 4)
    spec = lambda a: pl.BlockSpec((1, hb) + a.shape[2:], lambda b, j: (b, j, 0, 0))
    o = pl.pallas_call(
        functools.partial(_dil_dec_kernel, n_q=T, wc=wc), grid=(DB, nh // hb),
        in_specs=[spec(qh), spec(ckt), spec(cvt), spec(knt), spec(vnt)],
        out_specs=pl.BlockSpec((1, hb, QPAD, HEAD_DIM), lambda b, j: (b, j, 0, 0)),
        out_shape=jax.ShapeDtypeStruct((DB, nh, QPAD, HEAD_DIM), F32), name="dilated_decode",
        compiler_params=_cparams(("parallel", "parallel")))(qh, ckt, cvt, knt, vnt)
    return o[:, :, :T].transpose(0, 2, 1, 3).reshape(DB, T, W)


def kernel(x_prompt, x_sample, cache_a_k, cache_a_v, cache_a_logf, cache_b_cmp_k, cache_b_cmp_v, cache_b_slc_k, cache_b_slc_v, cache_b_swa_k, cache_b_swa_v, cache_c_k, cache_c_v, page_table, norm_mix0, w_in0, fox_bf, nsa_pe_k, nsa_w1_k, nsa_w2_k, nsa_pe_v, nsa_w1_v, nsa_w2_v, w_out0, norm_ffn0, ffn_w_gate, ffn_w_up, ffn_w_down, norm_mix1, w_in1, w_out1, norm_ffn1, moe_router, moe_w_gate, moe_w_up, moe_w_down, norm_final):
    B, S, D = x_prompt.shape
    DB, T, _ = x_sample.shape
    n_pages = page_table.shape[1]
    past = n_pages * cache_a_k.shape[1]
    pt_flat = page_table.reshape(-1).astype(jnp.int32)
    fw = FOX_HEADS * HEAD_DIM
    nw = NSA_HEADS * HEAD_DIM
    kvw = NSA_KV_HEADS * HEAD_DIM

    cuts = np.cumsum([0, fw, fw, fw, FOX_HEADS, nw] + [kvw] * 6 + [3 * NSA_HEADS])
    col = lambda i: w_in0[:, cuts[i]:cuts[i + 1]]
    qa_w, ka_w, va_w, fa_w, qb_w, kc_w, vc_w, ks_w, vs_w, kw_w, vw_w, gb_w = [col(i) for i in range(12)]

    f32_out = ((F32, 1.0),)
    bf16_out = ((BF16, 1.0),)

    def pack(ws, ropes, emits=None):
        widths = [w.shape[1] for w in ws]
        starts = np.concatenate([[0], np.cumsum(widths)[:-1]])
        emits = emits or [f32_out] * len(ws)
        return (jnp.concatenate(ws, axis=1).astype(BF16),
                [(int(s), int(w), r, e) for s, w, r, e in zip(starts, widths, ropes, emits)])

    both_out = ((F32, 1.0), (BF16, 1.0))
    q_out = ((BF16, SCALE),)
    w0r, segs0r = pack([kc_w, vc_w, ka_w, ks_w, kw_w], [True, False, False, True, True],
                       [f32_out] * 2 + [bf16_out] * 3)
    w0c, segs0c = pack([qa_w, ka_w, va_w, qb_w, kc_w, ks_w, kw_w, vc_w, vs_w, vw_w, gb_w],
                       [False, False, False, True, True, True, True, False, False, False, "sigmoid"],
                       [q_out, f32_out, both_out, q_out, f32_out, f32_out, f32_out, f32_out, both_out, both_out,
                        f32_out])
    w0c = w0c.T
    w0s, segs0s = pack([qa_w, ka_w, va_w, qb_w, kc_w, ks_w, kw_w, vc_w, vs_w, vw_w],
                       [False, False, False, True, True, True, True, False, False, False])
    wgate = gb_w.astype(BF16)
    wft = jnp.zeros((16, D), F32).at[:FOX_HEADS].set(fa_w.T).astype(BF16)
    wfr = jnp.zeros((D, LANES), F32).at[:, :FOX_HEADS].set(fa_w).astype(BF16)
    brow = jnp.zeros((1, LANES), F32).at[0, :FOX_HEADS].set(fox_bf)
    logf_args = (wft, fox_bf.reshape(FOX_HEADS, 1).astype(F32), wfr, brow)
    dw = w_in1.shape[1] // 3
    w1 = w_in1.astype(BF16)
    segs1 = [(0, dw, True, f32_out), (dw, dw, True, f32_out), (2 * dw, dw, False, f32_out)]
    w1c = w1[:, dw:].T
    segs1c = [(0, dw, True, f32_out), (dw, dw, False, f32_out)]
    w_out0_b, w_out1_b = w_out0.astype(BF16), w_out1.astype(BF16)
    ffn_g, ffn_u, ffn_d = ffn_w_gate.astype(BF16), ffn_w_up.astype(BF16), ffn_w_down.astype(BF16)
    moe_g, moe_u, moe_d = moe_w_gate.astype(BF16), moe_w_up.astype(BF16), moe_w_down.astype(BF16)
    cmp_k_w = _cmp_weights(nsa_pe_k, nsa_w1_k, nsa_w2_k)
    cmp_v_w = _cmp_weights(nsa_pe_v, nsa_w1_v, nsa_w2_v)

    tab_p = _rope_tables(jnp.arange(S))
    tab_s = _rope_tables(past + jnp.arange(DB * T) % T)
    tm_p = _tile(S, 512)
    npb = S // tm_p
    win_b = min(NSA_WINDOW, S)
    win_c = min(DIL_WINDOW_MAX, S)

    xp = x_prompt.reshape(B * S, D)
    (kc, vc, ka_b, ks_b, kw_b, qat_b, kat, vat, vat_b, qbt_b, kct, kst, kwt, vct, vst, vst_b, vwt, vwt_b,
     gates_t, lft_p, lf_rows) = _project(
        xp, norm_mix0, tab_p, npb, tm_p, w=w0r, row_segs=segs0r, wt=w0c, col_segs=segs0c,
        logf=logf_args, name="proj0_prompt")
    r3 = lambda a: a.reshape(B, S, a.shape[-1])
    heads = lambda a, h: a.reshape(B, h, HEAD_DIM, a.shape[-1])
    kaug, c0 = _fox_prep(r3(ka_b), r3(lf_rows), tm_p)
    o_at = _fox_prompt(qat_b, kaug, vat_b, c0, tm_p)
    kcmp_p = _compress(r3(kc), cmp_k_w, "compress_k_prompt")
    vcmp_p = _compress(r3(vc), cmp_v_w, "compress_v_prompt")
    kv2 = lambda a: heads(a, NSA_KV_HEADS)
    o_bt = _nsa_prompt(qbt_b, kv2(kcmp_p), kv2(vcmp_p), r3(ks_b), vst_b, r3(kw_b), vwt_b, gates_t)
    hp = _outproj(xp, [o_at, o_bt], w_out0_b, "outproj0_prompt", tm=tm_p)
    hp = _ffn(hp, norm_ffn0, ffn_g, ffn_u, ffn_d, "ffn_prompt")

    xs = x_sample.reshape(DB * T, D)
    (qa_s, ka_s, va_s, qb_s, kc_s, ks_s, kw_s, vc_s, vs_s, vw_s, gates_s, lft_s, _) = _project(
        xs, norm_mix0, tab_s, 1, DB * T, w=w0s, row_segs=segs0s, w_gate=wgate, logf=logf_args,
        name="proj0_sample")
    s3 = lambda a: a.reshape(DB, T, a.shape[-1])
    lf_s = jnp.transpose(lft_s[0].reshape(FOX_HEADS, DB, T), (1, 2, 0))
    o_a_s = _fox_decode(s3(qa_s), s3(ka_s), s3(va_s), lf_s, cache_a_k, cache_a_v, cache_a_logf, pt_flat)
    kcmp_s = _compress_paged(cache_b_cmp_k, pt_flat, DB, cmp_k_w, "compress_k_paged")
    vcmp_s = _compress_paged(cache_b_cmp_v, pt_flat, DB, cmp_v_w, "compress_v_paged")
    kvs = lambda a: a.reshape(DB, NSA_KV_HEADS, HEAD_DIM, a.shape[-1])
    o_b_s = _nsa_decode(s3(qb_s), kvs(kcmp_s), kvs(vcmp_s), cache_b_slc_k, cache_b_slc_v, s3(ks_s), s3(vs_s),
                        cache_b_swa_k, cache_b_swa_v, s3(kw_s), s3(vw_s), s3(gates_s), pt_flat, past)
    hs = _outproj(xs, [o_a_s.reshape(DB * T, fw), o_b_s.reshape(DB * T, nw)], w_out0_b, "outproj0_sample")
    hs = _ffn(hs, norm_ffn0, ffn_g, ffn_u, ffn_d, "ffn_sample")

    first_c = (S - win_c) // tm_p
    segs1p = [(0, dw, True, ((F32, SCALE),)), (dw, dw, True, f32_out), (2 * dw, dw, False, f32_out)]
    q1, k1, v1, k1t, v1t = _project(hp, norm_mix1, tab_p, npb, tm_p, w=w1, row_segs=segs1p, wt=w1c,
                                    col_segs=segs1c, col_from=(npb, first_c), name="proj1_prompt")
    groups = [_band_attention(q1.reshape(B, S, dw), k1.reshape(B, S, dw), v1.reshape(B, S, dw), dil,
                              window // dil, 512 if dil == 1 else 256, "dilated_prompt_d%d" % dil)
              for window, dil in DIL_PAIRS]
    hp = _merge_outproj(hp, [g[0] for g in groups], [g[1] for g in groups], w_out1_b, "outproj1_prompt")
    y_prompt = _moe_final(hp, norm_ffn1, moe_router, moe_g, moe_u, moe_d, norm_final, "moe_prompt").reshape(B, S, D)

    q1s, k1s, v1s = _project(hs, norm_mix1, tab_s, 1, DB * T, w=w1, row_segs=segs1, name="proj1_sample")
    o1s = _dilated_decode(s3(q1s), s3(k1s), s3(v1s), cache_c_k, cache_c_v)
    hs = _outproj(hs, [o1s.reshape(DB * T, dw)], w_out1_b, "outproj1_sample")
    y_sample = _moe_final(hs, norm_ffn1, moe_router, moe_g, moe_u, moe_d, norm_final, "moe_sample").reshape(DB, T, D)

    def state(a, h, last=None):
        a = a.reshape(a.shape[0], h, HEAD_DIM, a.shape[-1])
        if last is not None:
            a = a[..., a.shape[-1] - last:]
        return jnp.transpose(a, (0, 3, 1, 2))

    h4 = lambda a, h: a.reshape(DB, T, h, HEAD_DIM)
    nh1 = dw // HEAD_DIM
    return (y_prompt, y_sample,
            state(kat, FOX_HEADS), state(vat, FOX_HEADS), jnp.transpose(lft_p, (0, 2, 1)),
            state(kct, NSA_KV_HEADS), state(vct, NSA_KV_HEADS), state(kst, NSA_KV_HEADS), state(vst, NSA_KV_HEADS),
            state(kwt, NSA_KV_HEADS, win_b), state(vwt, NSA_KV_HEADS, win_b),
            state(k1t, nh1, win_c), state(v1t, nh1, win_c),
            h4(ka_s, FOX_HEADS), h4(va_s, FOX_HEADS), lf_s,
            h4(kc_s, NSA_KV_HEADS), h4(vc_s, NSA_KV_HEADS), h4(ks_s, NSA_KV_HEADS),
            h4(vs_s, NSA_KV_HEADS), h4(kw_s, NSA_KV_HEADS), h4(vw_s, NSA_KV_HEADS),
            h4(k1s, nh1), h4(v1s, nh1))
```

```python
import functools
import math

import numpy as np
import jax
import jax.numpy as jnp
from jax import lax
from jax.experimental import pallas as pl
from jax.experimental.pallas import tpu as pltpu

F32 = jnp.float32
BF16 = jnp.bfloat16

HEAD_DIM = 64
HALF = HEAD_DIM // 2
LANES = 128
SUBLANES = 8
ROPE_THETA = 10000.0
RMS_EPS = 1e-6
NEG = -1e30
MASK_BIG = 30000.0
SCALE = HEAD_DIM ** -0.5

FOX_HEADS = 8
NSA_HEADS = 8
NSA_KV_HEADS = 2
NSA_GROUP = NSA_HEADS // NSA_KV_HEADS
NSA_CMP_LEN = 32
NSA_CMP_STRIDE = 16
NSA_SLC_BLOCK = 64
NSA_TOPN = 16
NSA_WINDOW = 512
NSA_FORCE_BONUS = 1e3
DIL_PAIRS = ((128, 1), (512, 4), (2048, 16))
DIL_WINDOW_MAX = 2048
TOP_K = 2
QPAD = SUBLANES
NEW_PAD = LANES
PAGES_PER_STEP = 8

VMEM_LIMIT = 56 * 1024 * 1024


def _tile(n, pref):
    return pref if n % pref == 0 else n


def _cparams(sem):
    return pltpu.CompilerParams(dimension_semantics=sem, vmem_limit_bytes=VMEM_LIMIT)


def _mm(a, b):
    return jnp.dot(a.astype(BF16), b.astype(BF16), preferred_element_type=F32)


def _mm_nt(a, b):
    return lax.dot_general(a.astype(BF16), b.astype(BF16), (((1,), (1,)), ((), ())),
                           preferred_element_type=F32)


def _split3(x):
    hi = x.astype(BF16)
    r = x - hi.astype(F32)
    mid = r.astype(BF16)
    lo = (r - mid.astype(F32)).astype(BF16)
    return hi, mid, lo


def _mm3_left(x, exact_rhs):
    b = exact_rhs.astype(BF16)
    hi, mid, lo = _split3(x)
    d = lambda p: jnp.dot(p, b, preferred_element_type=F32)
    return d(hi) + d(mid) + d(lo)


def _sigmoid(z):
    return 1.0 / (1.0 + jnp.exp(-z))


def _silu(z):
    return z * _sigmoid(z)


def _log_sigmoid(z):
    return jnp.minimum(z, 0.0) - jnp.log1p(jnp.exp(-jnp.abs(z)))


def _rmsnorm(x, g):
    return x * lax.rsqrt(jnp.mean(x * x, axis=-1, keepdims=True) + RMS_EPS) * g


def _rope_rows(y, cos, sin_signed):
    n = y.shape[1]
    lane = lax.broadcasted_iota(jnp.int32, y.shape, 1)
    first = (lane % HEAD_DIM) < HALF
    rot = jnp.where(first, pltpu.roll(y, n - HALF, 1), pltpu.roll(y, HALF, 1))
    reps = n // LANES
    if reps > 1:
        cos = jnp.concatenate([cos] * reps, axis=1)
        sin_signed = jnp.concatenate([sin_signed] * reps, axis=1)
    return y * cos + rot * sin_signed


def _rope_cols(yt, cos_t, sin_t):
    out = []
    for h in range(yt.shape[0] // HEAD_DIM):
        a = yt[h * HEAD_DIM:h * HEAD_DIM + HALF]
        b = yt[h * HEAD_DIM + HALF:(h + 1) * HEAD_DIM]
        out += [a * cos_t - b * sin_t, b * cos_t + a * sin_t]
    return jnp.concatenate(out, axis=0)


def _rope_tables(pos):
    inv = jnp.exp(-math.log(ROPE_THETA) * jnp.arange(HALF, dtype=F32) / HALF)
    ang = pos.astype(F32)[:, None] * inv[None, :]
    cos, sin = jnp.cos(ang), jnp.sin(ang)
    return (jnp.concatenate([cos, cos, cos, cos], axis=1),
            jnp.concatenate([-sin, sin, -sin, sin], axis=1), cos.T, sin.T)


def _softmax_rows(s, mask):
    sm = jnp.where(mask, s, NEG)
    m = jnp.max(sm, axis=1, keepdims=True)
    p = jnp.where(mask, jnp.exp(sm - m), 0.0)
    l = jnp.sum(p, axis=1, keepdims=True)
    return p / jnp.where(l > 0.0, l, 1.0)


def _online_step(s, m_ref, l_ref):
    m_prev = m_ref[...]
    m_new = jnp.maximum(m_prev, jnp.max(s, axis=1, keepdims=True))
    alpha = jnp.exp(m_prev - m_new)
    p = jnp.exp(s - m_new)
    l_ref[...] = alpha * l_ref[...] + jnp.sum(p, axis=1, keepdims=True)
    m_ref[...] = m_new
    return alpha, p


def _online_init(m_ref, l_ref, acc_ref):
    m_ref[...] = jnp.full(m_ref.shape, NEG, F32)
    l_ref[...] = jnp.zeros_like(l_ref)
    acc_ref[...] = jnp.zeros_like(acc_ref)


def _proj_kernel(*refs, row_segs, col_segs, n_gate, with_logf, col_from):
    it = iter(refs)
    x_ref, g_ref, cos_ref, sin_ref, cost_ref, sint_ref = [next(it) for _ in range(6)]
    w_ref = next(it) if row_segs else None
    wt_ref = next(it) if col_segs else None
    wg_ref = next(it) if n_gate else None
    if with_logf:
        wft_ref, bcol_ref, wfr_ref, brow_ref = next(it), next(it), next(it), next(it)
    outs = list(it)
    xn = _rmsnorm(x_ref[...], g_ref[...]).astype(BF16)
    k = 0
    for c0, width, rope, emits in row_segs:
        y = jnp.dot(xn, w_ref[:, c0:c0 + width], preferred_element_type=F32)
        if rope:
            y = _rope_rows(y, cos_ref[...], sin_ref[...])
        for dtype, scale in emits:
            outs[k][...] = (y if scale == 1.0 else y * scale).astype(dtype)
            k += 1

    def cols():
        kk = k
        for r0, height, rope, emits in col_segs:
            yt = _mm_nt(wt_ref[r0:r0 + height, :], xn)
            if rope == "sigmoid":
                yt = _sigmoid(yt)
            elif rope:
                yt = _rope_cols(yt, cost_ref[...], sint_ref[...])
            for dtype, scale in emits:
                outs[kk][0] = (yt if scale == 1.0 else yt * scale).astype(dtype)
                kk += 1

    if col_segs:
        if col_from:
            pl.when(pl.program_id(0) % col_from[0] >= col_from[1])(cols)
        else:
            cols()
        k += sum(len(e) for _, _, _, e in col_segs)
    if n_gate:
        outs[k][...] = _sigmoid(jnp.dot(xn, wg_ref[...], preferred_element_type=F32))
        k += 1
    if with_logf:
        yt = _mm_nt(wft_ref[...], xn)
        outs[k][0] = _log_sigmoid(yt[0:FOX_HEADS] + bcol_ref[...])
        outs[k + 1][...] = _log_sigmoid(jnp.dot(xn, wfr_ref[...], preferred_element_type=F32) + brow_ref[...])


def _project(x2d, gain, tables, n_pos_blocks, tm, w=None, row_segs=(), wt=None, col_segs=(),
             w_gate=None, logf=None, col_from=None, name="proj"):
    M, D = x2d.shape
    nt = M // tm
    n_seq = nt // n_pos_blocks
    cos_t, sin_t, cos_c, sin_c = tables
    pos_map = lambda i: (i % n_pos_blocks, 0)
    posc_map = lambda i: (0, i % n_pos_blocks)
    const = lambda a: pl.BlockSpec(a.shape, lambda i: (0,) * a.ndim)
    in_specs = [pl.BlockSpec((tm, D), lambda i: (i, 0)), const(gain.reshape(1, D)),
                pl.BlockSpec((tm, LANES), pos_map), pl.BlockSpec((tm, LANES), pos_map),
                pl.BlockSpec((HALF, tm), posc_map), pl.BlockSpec((HALF, tm), posc_map)]
    args = [x2d, gain.reshape(1, D), cos_t, sin_t, cos_c, sin_c]
    for a in (w, wt, w_gate):
        if a is not None:
            in_specs.append(const(a))
            args.append(a)
    out_shape, out_specs = [], []
    for _, wd, _, emits in row_segs:
        for dtype, _ in emits:
            out_shape.append(jax.ShapeDtypeStruct((M, wd), dtype))
            out_specs.append(pl.BlockSpec((tm, wd), lambda i: (i, 0)))
    first = col_from[1] if col_from else 0
    n_cb = n_pos_blocks - first
    col_map = lambda i: (i // n_pos_blocks, 0, jnp.maximum(i % n_pos_blocks - first, 0))
    for _, ht, _, emits in col_segs:
        for dtype, _ in emits:
            out_shape.append(jax.ShapeDtypeStruct((n_seq, ht, n_cb * tm), dtype))
            out_specs.append(pl.BlockSpec((1, ht, tm), col_map))
    n_gate = 0
    if w_gate is not None:
        n_gate = w_gate.shape[1]
        out_shape.append(jax.ShapeDtypeStruct((M, n_gate), F32))
        out_specs.append(pl.BlockSpec((tm, n_gate), lambda i: (i, 0)))
    if logf is not None:
        in_specs += [const(a) for a in logf]
        args += list(logf)
        out_shape.append(jax.ShapeDtypeStruct((n_seq, FOX_HEADS, n_pos_blocks * tm), F32))
        out_specs.append(pl.BlockSpec((1, FOX_HEADS, tm), lambda i: (i // n_pos_blocks, 0, i % n_pos_blocks)))
        out_shape.append(jax.ShapeDtypeStruct((M, LANES), F32))
        out_specs.append(pl.BlockSpec((tm, LANES), lambda i: (i, 0)))
    return pl.pallas_call(
        functools.partial(_proj_kernel, row_segs=tuple(row_segs), col_segs=tuple(col_segs), n_gate=n_gate,
                          with_logf=logf is not None, col_from=col_from),
        grid=(nt,), in_specs=in_specs, out_specs=out_specs, out_shape=out_shape, name=name,
        compiler_params=_cparams(("arbitrary",)))(*args)


def _fox_prep_kernel(k_ref, lf_ref, kaug_ref, c0_ref, carry_ref):
    j = pl.program_id(1)
    tc = k_ref.shape[1]

    @pl.when(j == 0)
    def _():
        carry_ref[...] = jnp.zeros_like(carry_ref)
        c0_ref[...] = jnp.zeros_like(c0_ref)

    lane = lax.broadcasted_iota(jnp.int32, (FOX_HEADS, LANES), 1)
    start = jnp.transpose(carry_ref[...])[0:FOX_HEADS, 0:1]
    c0_ref[0] = jnp.where(lane == j, start, c0_ref[0])

    r = lax.broadcasted_iota(jnp.int32, (tc, tc), 0)
    c = lax.broadcasted_iota(jnp.int32, (tc, tc), 1)
    tri = (c <= r).astype(BF16)
    hi, mid, lo = _split3(lf_ref[0])
    d = lambda p: jnp.dot(tri, p, preferred_element_type=F32)
    local = d(hi) + d(mid) + d(lo)
    carry_ref[...] = carry_ref[...] + jnp.broadcast_to(local[tc - 1:tc, :], carry_ref.shape)
    parts = jnp.concatenate(_split3(-local), axis=1)
    k = k_ref[0]
    kr = lax.broadcasted_iota(jnp.int32, (k.shape[1], LANES), 0)
    kc = lax.broadcasted_iota(jnp.int32, (k.shape[1], LANES), 1)
    pr = lax.broadcasted_iota(jnp.int32, (3 * LANES, LANES), 0)
    pc = lax.broadcasted_iota(jnp.int32, (3 * LANES, LANES), 1)
    for h in range(FOX_HEADS):
        place_k = ((kr == HEAD_DIM * h + kc) & (kc < HEAD_DIM)).astype(BF16)
        place_c = ((pr % LANES == h) & (pc == HEAD_DIM + pr // LANES)).astype(BF16)
        kaug_ref[0, h] = (jnp.dot(k, place_k, preferred_element_type=F32)
                          + jnp.dot(parts, place_c, preferred_element_type=F32)).astype(BF16)


def _fox_prep(k_b, lf_rows, tc):
    B, S, W = k_b.shape
    assert S // tc <= LANES
    return pl.pallas_call(
        _fox_prep_kernel, grid=(B, S // tc),
        in_specs=[pl.BlockSpec((1, tc, W), lambda b, j: (b, j, 0)),
                  pl.BlockSpec((1, tc, LANES), lambda b, j: (b, j, 0))],
        out_specs=[pl.BlockSpec((1, FOX_HEADS, tc, LANES), lambda b, j: (b, 0, j, 0)),
                   pl.BlockSpec((1, FOX_HEADS, LANES), lambda b, j: (b, 0, 0))],
        out_shape=[jax.ShapeDtypeStruct((B, FOX_HEADS, S, LANES), BF16),
                   jax.ShapeDtypeStruct((B, FOX_HEADS, LANES), F32)], name="fox_prep",
        scratch_shapes=[pltpu.VMEM((SUBLANES, LANES), F32)],
        compiler_params=_cparams(("parallel", "arbitrary")))(k_b, lf_rows)


def _fox_kernel(qi_ref, ki_ref, qt_ref, kaug_ref, vt_ref, c0_ref, o_ref, qa_ref, m_ref, l_ref, acc_ref, *, nh):
    hg, step = pl.program_id(1), pl.program_id(2)
    qi, ki = qi_ref[step], ki_ref[step]
    tq, tk = qt_ref.shape[2], kaug_ref.shape[2]
    ratio = tq // tk
    lane1 = lax.broadcasted_iota(jnp.int32, (1, LANES), 1)

    @pl.when(ki == 0)
    def _():
        row = lax.broadcasted_iota(jnp.int32, (HEAD_DIM, tq), 0)
        ones = jnp.where(row < 3, 1.0, 0.0).astype(BF16)
        for h in range(nh):
            qa_ref[h] = jnp.concatenate([qt_ref[0, h * HEAD_DIM:(h + 1) * HEAD_DIM, :], ones], axis=0)
        _online_init(m_ref, l_ref, acc_ref)

    def tile(masked):
        if masked:
            live = (ki * tk + lax.broadcasted_iota(jnp.int32, (tk, tq), 0)
                    <= qi * tq + lax.broadcasted_iota(jnp.int32, (tk, tq), 1))
        for h in range(nh):
            c0 = c0_ref[0, pl.ds(nh * hg + h, 1), :]
            delta = jnp.sum(jnp.where(lane1 == ratio * qi, c0, 0.0) - jnp.where(lane1 == ki, c0, 0.0),
                            axis=1, keepdims=True)
            s = jnp.dot(kaug_ref[0, h], qa_ref[h], preferred_element_type=F32)
            if masked:
                s = jnp.where(live, s, NEG)
            m_prev = m_ref[h]
            m_new = jnp.maximum(m_prev, jnp.max(s, axis=0, keepdims=True) + delta)
            p = jnp.exp(s - (m_new - delta))
            alpha = jnp.exp(m_prev - m_new)
            l_ref[h] = alpha * l_ref[h] + jnp.sum(p, axis=0, keepdims=True)
            acc_ref[h] = alpha * acc_ref[h] + jnp.dot(vt_ref[0, h * HEAD_DIM:(h + 1) * HEAD_DIM, :],
                                                      p.astype(BF16), preferred_element_type=F32)
            m_ref[h] = m_new

    pl.when(ki < ratio * qi)(lambda: tile(False))
    pl.when(ki >= ratio * qi)(lambda: tile(True))

    @pl.when(ki == ratio * (qi + 1) - 1)
    def _():
        for h in range(nh):
            o_ref[0, h * HEAD_DIM:(h + 1) * HEAD_DIM, :] = (acc_ref[h] / l_ref[h]).astype(BF16)


def _fox_prompt(qt_b, kaug, vt_b, c0, tk, nh=4):
    B, W, S = qt_b.shape
    tq = _tile(S, 2 * tk)
    ratio = tq // tk
    pairs = [(q, k) for q in range(S // tq) for k in range(ratio * (q + 1))]
    qi_tab = jnp.asarray([p[0] for p in pairs], jnp.int32)
    ki_tab = jnp.asarray([p[1] for p in pairs], jnp.int32)
    hw = nh * HEAD_DIM
    grid_spec = pltpu.PrefetchScalarGridSpec(
        num_scalar_prefetch=2, grid=(B, W // hw, len(pairs)),
        in_specs=[pl.BlockSpec((1, hw, tq), lambda b, hg, s, qi, ki: (b, hg, qi[s])),
                  pl.BlockSpec((1, nh, tk, LANES), lambda b, hg, s, qi, ki: (b, hg, ki[s], 0)),
                  pl.BlockSpec((1, hw, tk), lambda b, hg, s, qi, ki: (b, hg, ki[s])),
                  pl.BlockSpec((1, FOX_HEADS, LANES), lambda b, hg, s, qi, ki: (b, 0, 0))],
        out_specs=pl.BlockSpec((1, hw, tq), lambda b, hg, s, qi, ki: (b, hg, qi[s])),
        scratch_shapes=[pltpu.VMEM((nh, LANES, tq), BF16), pltpu.VMEM((nh, 1, tq), F32),
                        pltpu.VMEM((nh, 1, tq), F32), pltpu.VMEM((nh, HEAD_DIM, tq), F32)])
    return pl.pallas_call(
        functools.partial(_fox_kernel, nh=nh), grid_spec=grid_spec,
        out_shape=jax.ShapeDtypeStruct(qt_b.shape, BF16), name="fox_prompt",
        compiler_params=_cparams(("parallel", "parallel", "arbitrary")))(qi_tab, ki_tab, qt_b, kaug, vt_b, c0)


def _cmp_compute(x, pea_ref, peb_ref, wa_ref, wb_ref, w2t_ref, o_ref, carry_ref):
    n = x.shape[0]
    a = jnp.dot((x + pea_ref[...]).astype(BF16), wa_ref[...], preferred_element_type=F32)
    b = jnp.dot((x + peb_ref[...]).astype(BF16), wb_ref[...], preferred_element_type=F32)
    rowi = lax.broadcasted_iota(jnp.int32, a.shape, 0)
    a_prev = jnp.where(rowi == 0, carry_ref[0:1, :], pltpu.roll(a, 1, 0))
    carry_ref[...] = jnp.broadcast_to(a[n - 1:n, :], carry_ref.shape)
    o_ref[0] = _mm_nt(w2t_ref[...], _silu(a_prev + b))


def _cmp_kernel(x_ref, pea_ref, peb_ref, wa_ref, wb_ref, w2t_ref, o_ref, carry_ref):
    @pl.when(pl.program_id(1) == 0)
    def _():
        carry_ref[...] = jnp.zeros_like(carry_ref)
    _cmp_compute(x_ref[0], pea_ref, peb_ref, wa_ref, wb_ref, w2t_ref, o_ref, carry_ref)


def _cmp_paged_kernel(pt_ref, *refs, n_pg):
    pages = refs[:n_pg]
    pea_ref, peb_ref, wa_ref, wb_ref, w2t_ref, o_ref, xs_ref, carry_ref = refs[n_pg:]
    page = pages[0].shape[3]

    @pl.when(pl.program_id(1) == 0)
    def _():
        carry_ref[...] = jnp.zeros_like(carry_ref)

    for j, r in enumerate(pages):
        xs_ref[j * page:(j + 1) * page, :] = jnp.transpose(r[0].reshape(LANES, page))
    n = n_pg * page // NSA_CMP_STRIDE
    a = jnp.zeros((n, wa_ref.shape[2]), F32)
    b = jnp.zeros((n, wb_ref.shape[2]), F32)
    for l in range(NSA_CMP_STRIDE):
        xl = xs_ref[pl.ds(l, n, stride=NSA_CMP_STRIDE), :]
        a = a + jnp.dot((xl + pea_ref[l:l + 1, :]).astype(BF16), wa_ref[l], preferred_element_type=F32)
        b = b + jnp.dot((xl + peb_ref[l:l + 1, :]).astype(BF16), wb_ref[l], preferred_element_type=F32)
    rowi = lax.broadcasted_iota(jnp.int32, a.shape, 0)
    a_prev = jnp.where(rowi == 0, carry_ref[0:1, :], pltpu.roll(a, 1, 0))
    carry_ref[...] = jnp.broadcast_to(a[n - 1:n, :], carry_ref.shape)
    o_ref[0] = _mm_nt(w2t_ref[...], _silu(a_prev + b))


def _cmp_weights(pe, w1, w2):
    eye = jnp.eye(NSA_KV_HEADS, dtype=F32)
    hid = w1.shape[2]
    half = NSA_CMP_STRIDE

    def wpart(w):
        return jnp.einsum('lde,hg->lhdge', w, eye).reshape(half * LANES, NSA_KV_HEADS * hid).astype(BF16)

    def ppart(p):
        return jnp.broadcast_to(p[:, None, :], (half, NSA_KV_HEADS, HEAD_DIM)).reshape(1, half * LANES)

    w2t = jnp.einsum('ed,hg->gdhe', w2, eye).reshape(LANES, NSA_KV_HEADS * hid).astype(BF16)
    return ppart(pe[:half]), ppart(pe[half:]), wpart(w1[:half]), wpart(w1[half:]), w2t


def _compress(x, weights, name):
    N, L, _ = x.shape
    n_ch = L // NSA_CMP_STRIDE
    xc = x[:, :n_ch * NSA_CMP_STRIDE].reshape(N, n_ch, NSA_CMP_STRIDE * LANES)
    tch = _tile(n_ch, 256)
    wspecs = [pl.BlockSpec(w.shape, lambda n, j: (0, 0)) for w in weights]
    return pl.pallas_call(
        _cmp_kernel, grid=(N, n_ch // tch),
        in_specs=[pl.BlockSpec((1, tch, xc.shape[2]), lambda n, j: (n, j, 0))] + wspecs,
        out_specs=pl.BlockSpec((1, LANES, tch), lambda n, j: (n, 0, j)),
        out_shape=jax.ShapeDtypeStruct((N, LANES, n_ch), F32), name=name,
        scratch_shapes=[pltpu.VMEM((8, weights[2].shape[1]), F32)],
        compiler_params=_cparams(("parallel", "arbitrary")))(xc, *weights)


def _compress_paged(cache, pt_flat, n_db, weights, name):
    n_pool, page = cache.shape[:2]
    assert page == LANES
    rows = page // NSA_CMP_STRIDE
    ct = jnp.transpose(cache, (0, 2, 3, 1))
    n_pages = pt_flat.shape[0] // n_db
    n_pg = math.gcd(n_pages, max(1, 256 // rows))
    pea, peb, wa, wb, w2t = weights
    per_pos = lambda a: a.reshape(NSA_CMP_STRIDE, LANES, -1)
    weights = (pea.reshape(NSA_CMP_STRIDE, LANES), peb.reshape(NSA_CMP_STRIDE, LANES), per_pos(wa), per_pos(wb), w2t)
    wspecs = [pl.BlockSpec(w.shape, lambda b, p, pt, nd=w.ndim: (0,) * nd) for w in weights]
    page_spec = lambda j: pl.BlockSpec((1,) + ct.shape[1:],
                                       lambda b, p, pt: (pt[b * n_pages + p * n_pg + j], 0, 0, 0))
    grid_spec = pltpu.PrefetchScalarGridSpec(
        num_scalar_prefetch=1, grid=(n_db, n_pages // n_pg),
        in_specs=[page_spec(j) for j in range(n_pg)] + wspecs,
        out_specs=pl.BlockSpec((1, LANES, n_pg * rows), lambda b, p, pt: (b, 0, p)),
        scratch_shapes=[pltpu.VMEM((n_pg * page, LANES), F32), pltpu.VMEM((8, wa.shape[1]), F32)])
    return pl.pallas_call(
        functools.partial(_cmp_paged_kernel, n_pg=n_pg), grid_spec=grid_spec,
        out_shape=jax.ShapeDtypeStruct((n_db, LANES, n_pages * rows), F32), name=name,
        compiler_params=_cparams(("parallel", "arbitrary")))(pt_flat, *([ct] * n_pg), *weights)


def _slc_map_rows(n_ch, ns_pad):
    i = (np.arange(n_ch)[:, None] - 1) * NSA_CMP_STRIDE
    j = np.arange(ns_pad)[None, :] * NSA_SLC_BLOCK
    shared = np.minimum(i + NSA_CMP_LEN, j + NSA_SLC_BLOCK) - np.maximum(i, j)
    m = np.clip(shared, 0, None) / NSA_CMP_LEN
    m[0, :] = 0.0
    return m.astype(np.float32)


def _select_blocks(imp, qpos, ns, n_sel):
    blk = lax.broadcasted_iota(jnp.int32, imp.shape, 1)
    cur = qpos // NSA_SLC_BLOCK
    valid = blk * NSA_SLC_BLOCK <= qpos
    forced = (blk == 0) | (blk == cur) | (blk == cur - 1)
    score = jnp.where(valid, imp + jnp.where(forced, NSA_FORCE_BONUS, 0.0), NEG)
    rank = jnp.zeros(imp.shape, jnp.int32)
    for jp in range(ns):
        sj = score[:, jp:jp + 1]
        beats = (sj > score) | ((sj == score) & (blk > jp))
        rank = rank + beats.astype(jnp.int32)
    return rank < n_sel


def _select_blocks_cols(imp, qpos, ns, n_sel):
    blk = lax.broadcasted_iota(jnp.int32, imp.shape, 0)
    cur = qpos // NSA_SLC_BLOCK
    valid = blk * NSA_SLC_BLOCK <= qpos
    forced = (blk == 0) | (blk == cur) | (blk == cur - 1)
    score = jnp.where(valid, imp + jnp.where(forced, NSA_FORCE_BONUS, 0.0), NEG)
    rank = jnp.zeros(imp.shape, jnp.int32)
    for jp in range(ns):
        sj = score[jp:jp + 1, :]
        beats = (sj > score) | ((sj == score) & (blk > jp))
        rank = rank + beats.astype(jnp.int32)
    return rank < n_sel


def _softmax_cols(s, mask):
    sm = jnp.where(mask, s, NEG)
    m = jnp.max(sm, axis=0, keepdims=True)
    p = jnp.where(mask, jnp.exp(sm - m), 0.0)
    l = jnp.sum(p, axis=0, keepdims=True)
    return p / jnp.where(l > 0.0, l, 1.0)


def _nsa_kernel(qt_ref, kct_ref, vct_ref, ks_ref, vst_ref, kw_ref, vwt_ref, gt_ref, mapt_ref, o_ref,
                qaug_ref, negm_ref, m_ref, l_ref, acc_ref, *, ns, n_sel, tk, win):
    g, i = pl.program_id(1), pl.program_id(2)
    tq = qt_ref.shape[2]
    nl = NSA_GROUP * tq
    n_ch = kct_ref.shape[3]
    per_tile = tk // NSA_SLC_BLOCK
    st = i * tq
    qpos1 = st + lax.broadcasted_iota(jnp.int32, (1, tq), 1)
    rep = lambda a: jnp.concatenate([a] * NSA_GROUP, axis=1)
    qpos = rep(qpos1)

    q4t = jnp.concatenate([qt_ref[0, j * HEAD_DIM:(j + 1) * HEAD_DIM, :] for j in range(NSA_GROUP)], axis=1)
    row = lax.broadcasted_iota(jnp.int32, (LANES, nl), 0)
    qaug_ref[0:LANES, :] = jnp.where(row // HEAD_DIM == g, jnp.concatenate([q4t, q4t], axis=0),
                                     jnp.zeros((LANES, nl), BF16))
    qaug_ref[LANES:2 * LANES, :] = jnp.zeros((LANES, nl), BF16)

    cidx = lax.broadcasted_iota(jnp.int32, (n_ch, tq), 0)
    cmask = rep(((cidx - 1) * NSA_CMP_STRIDE + NSA_CMP_LEN - 1 <= qpos1) & (cidx >= 1))
    sc = lax.dot_general(kct_ref[0, 0].astype(BF16), q4t, (((0,), (0,)), ((), ())), preferred_element_type=F32)
    pc = _softmax_cols(sc, cmask)
    oc = _mm(vct_ref[0, 0], pc)
    pstack = jnp.concatenate([pc[:, j * tq:(j + 1) * tq] for j in range(NSA_GROUP)], axis=0)
    imp = jnp.dot(mapt_ref[...], pstack.astype(BF16), preferred_element_type=F32)[0:HEAD_DIM]
    sel = _select_blocks_cols(imp, qpos1, ns, n_sel)

    negm_ref[...] = rep(jnp.where(sel, 0.0, -MASK_BIG))
    _online_init(m_ref, l_ref, acc_ref)
    onehot = ((lax.broadcasted_iota(jnp.int32, (tk, LANES), 0) // NSA_SLC_BLOCK)
              == lax.broadcasted_iota(jnp.int32, (tk, LANES), 1)).astype(BF16)

    def tile(kt, diagonal):
        k0 = pl.multiple_of(kt * tk, tk)
        nm = negm_ref[pl.ds(pl.multiple_of(kt * per_tile, per_tile), per_tile), :]
        qaug_ref[LANES:LANES + 16, :] = jnp.concatenate(
            [nm, jnp.zeros((16 - per_tile, nl), F32)], axis=0).astype(BF16)
        kaug = jnp.concatenate([ks_ref[0, pl.ds(k0, tk), :], onehot], axis=1)
        s = jnp.dot(kaug, qaug_ref[...], preferred_element_type=F32)
        if diagonal:
            kpos = k0 + lax.broadcasted_iota(jnp.int32, (tk, nl), 0)
            s = jnp.where(kpos <= qpos, s, NEG)
        m_prev = m_ref[...]
        m_new = jnp.maximum(m_prev, jnp.max(s, axis=0, keepdims=True))
        alpha = jnp.exp(m_prev - m_new)
        p = jnp.exp(s - m_new)
        l_ref[...] = alpha * l_ref[...] + jnp.sum(p, axis=0, keepdims=True)
        acc_ref[...] = alpha * acc_ref[...] + jnp.dot(vst_ref[0, :, pl.ds(k0, tk)], p.astype(BF16),
                                                      preferred_element_type=F32)
        m_ref[...] = m_new

    last = st // tk

    def body(kt, carry):
        tile(kt, False)
        return carry

    lax.fori_loop(0, last, body, 0)
    tile(last, True)
    osl = acc_ref[...] / l_ref[...]

    w0 = pl.multiple_of(jnp.maximum(st + tq - win, 0), tq)
    dist = qpos1 - (w0 + lax.broadcasted_iota(jnp.int32, (win, tq), 0))
    sw = jnp.dot(kw_ref[0, pl.ds(w0, win), :], qaug_ref[0:LANES, :], preferred_element_type=F32)
    pw = _softmax_cols(sw, rep((dist >= 0) & (dist <= NSA_WINDOW)))
    ow = jnp.dot(vwt_ref[0, :, pl.ds(w0, win)], pw.astype(BF16), preferred_element_type=F32)

    for j in range(NSA_GROUP):
        base = (NSA_GROUP * g + j) * 3
        gate = lambda r: gt_ref[0, pl.ds(base + r, 1), :]
        cols = slice(j * tq, (j + 1) * tq)
        o = gate(0) * oc[:, cols] + gate(1) * osl[:, cols] + gate(2) * ow[:, cols]
        o_ref[0, j * HEAD_DIM:(j + 1) * HEAD_DIM, :] = o.astype(BF16)


def _nsa_prompt(qt_b, kct, vct, ks_b, vst_b, kw_b, vwt_b, gates_t):
    B, W, S = qt_b.shape
    n_ch = kct.shape[3]
    ns = -(-S // NSA_SLC_BLOCK)
    assert ns <= HEAD_DIM
    n_sel = min(NSA_TOPN, ns)
    tq = _tile(S, 256)
    tk = _tile(S, 512)
    assert tk // NSA_SLC_BLOCK <= 16
    win = min(NSA_WINDOW + tq, S)
    nl = NSA_GROUP * tq
    smap_t = jnp.asarray(np.tile(_slc_map_rows(n_ch, LANES).T, (1, NSA_GROUP)), dtype=BF16)
    per_head = lambda a: pl.BlockSpec((1, 1) + a.shape[2:], lambda b, g, i: (b, g, 0, 0))
    gw = W // NSA_KV_HEADS
    return pl.pallas_call(
        functools.partial(_nsa_kernel, ns=ns, n_sel=n_sel, tk=tk, win=win),
        grid=(B, NSA_KV_HEADS, S // tq),
        in_specs=[pl.BlockSpec((1, gw, tq), lambda b, g, i: (b, g, i)),
                  per_head(kct), per_head(vct),
                  pl.BlockSpec((1, S, LANES), lambda b, g, i: (b, 0, 0)),
                  pl.BlockSpec((1, HEAD_DIM, S), lambda b, g, i: (b, g, 0)),
                  pl.BlockSpec((1, S, LANES), lambda b, g, i: (b, 0, 0)),
                  pl.BlockSpec((1, HEAD_DIM, S), lambda b, g, i: (b, g, 0)),
                  pl.BlockSpec((1, gates_t.shape[1], tq), lambda b, g, i: (b, 0, i)),
                  pl.BlockSpec(smap_t.shape, lambda b, g, i: (0, 0))],
        out_specs=pl.BlockSpec((1, gw, tq), lambda b, g, i: (b, g, i)),
        out_shape=jax.ShapeDtypeStruct(qt_b.shape, BF16), name="nsa_prompt",
        scratch_shapes=[pltpu.VMEM((2 * LANES, nl), BF16), pltpu.VMEM((HEAD_DIM, nl), F32),
                        pltpu.VMEM((1, nl), F32), pltpu.VMEM((1, nl), F32), pltpu.VMEM((HEAD_DIM, nl), F32)],
        compiler_params=_cparams(("parallel", "parallel", "arbitrary")))(
            qt_b, kct, vct, ks_b, vst_b, kw_b, vwt_b, gates_t, smap_t)


def _outproj_kernel(*refs):
    x_ref, w_ref, y_ref = refs[0], refs[-2], refs[-1]
    y = x_ref[...]
    k0 = 0
    for o_ref in refs[1:-2]:
        if len(o_ref.shape) == 3:
            kw = o_ref.shape[1]
            y = y + lax.dot_general(o_ref[0].astype(BF16), w_ref[k0:k0 + kw, :], (((0,), (0,)), ((), ())),
                                    preferred_element_type=F32)
        else:
            kw = o_ref.shape[1]
            y = y + jnp.dot(o_ref[...].astype(BF16), w_ref[k0:k0 + kw, :], preferred_element_type=F32)
        k0 += kw
    y_ref[...] = y


def _outproj(x2d, parts, w_bf, name, tm=None):
    M, D = x2d.shape
    tm = tm or _tile(M, 512)
    specs = []
    for o in parts:
        if o.ndim == 3:
            npb = o.shape[2] // tm
            specs.append(pl.BlockSpec((1, o.shape[1], tm), lambda i, npb=npb: (i // npb, 0, i % npb)))
        else:
            specs.append(pl.BlockSpec((tm, o.shape[1]), lambda i: (i, 0)))
    return pl.pallas_call(
        _outproj_kernel, grid=(M // tm,),
        in_specs=[pl.BlockSpec((tm, D), lambda i: (i, 0))] + specs + [pl.BlockSpec(w_bf.shape, lambda i: (0, 0))],
        out_specs=pl.BlockSpec((tm, D), lambda i: (i, 0)),
        out_shape=jax.ShapeDtypeStruct((M, D), F32), name=name,
        compiler_params=_cparams(("parallel",)))(x2d, *parts, w_bf)


def _ffn_kernel(x_ref, g_ref, wg_ref, wu_ref, wd_ref, y_ref, xn_ref):
    f = pl.program_id(1)

    @pl.when(f == 0)
    def _():
        x = x_ref[...]
        xn_ref[...] = _rmsnorm(x, g_ref[...]).astype(BF16)
        y_ref[...] = x

    xn = xn_ref[...]
    h = _silu(jnp.dot(xn, wg_ref[...], preferred_element_type=F32)) * \
        jnp.dot(xn, wu_ref[...], preferred_element_type=F32)
    y_ref[...] += jnp.dot(h.astype(BF16), wd_ref[...], preferred_element_type=F32)


def _ffn(x2d, gain, wg, wu, wd, name):
    M, D = x2d.shape
    Fd = wg.shape[1]
    tm = _tile(M, 512)
    nf = 2 if Fd % (2 * LANES) == 0 else 1
    fc = Fd // nf
    return pl.pallas_call(
        _ffn_kernel, grid=(M // tm, nf),
        in_specs=[pl.BlockSpec((tm, D), lambda i, f: (i, 0)), pl.BlockSpec((1, D), lambda i, f: (0, 0)),
                  pl.BlockSpec((D, fc), lambda i, f: (0, f)), pl.BlockSpec((D, fc), lambda i, f: (0, f)),
                  pl.BlockSpec((fc, D), lambda i, f: (f, 0))],
        out_specs=pl.BlockSpec((tm, D), lambda i, f: (i, 0)),
        out_shape=jax.ShapeDtypeStruct((M, D), F32), name=name,
        scratch_shapes=[pltpu.VMEM((tm, D), BF16)],
        compiler_params=_cparams(("parallel", "arbitrary")))(x2d, gain.reshape(1, D), wg, wu, wd)


def _moe_kernel(x_ref, g_ref, wrt_ref, wg_ref, wu_ref, wd_ref, gf_ref, y_ref, xn_ref, rank_ref, comb_ref,
                rankc_ref, *, n_exp, chunk):
    e = pl.program_id(1)
    tm = x_ref.shape[0]

    @pl.when(e == 0)
    def _():
        xn = _rmsnorm(x_ref[...], g_ref[...])
        xn_ref[...] = xn.astype(BF16)
        xh, xm, _ = _split3(xn)
        wh, wm, _ = _split3(wrt_ref[...])
        dn = lambda a, b: lax.dot_general(a, b, (((1,), (1,)), ((), ())), preferred_element_type=F32)
        row = lax.broadcasted_iota(jnp.int32, (LANES, tm), 0)
        logits = jnp.where(row < n_exp, dn(wh, xh) + dn(wh, xm) + dn(wm, xh), NEG)
        v1 = jnp.max(logits, axis=0, keepdims=True)
        i1 = jnp.min(jnp.where(logits == v1, row, LANES), axis=0, keepdims=True)
        rest = jnp.where(row == i1, NEG, logits)
        v2 = jnp.max(rest, axis=0, keepdims=True)
        i2 = jnp.min(jnp.where(rest == v2, row, LANES), axis=0, keepdims=True)
        ex = jnp.exp(v2 - v1)
        comb = jnp.where(row == i1, 1.0 / (1.0 + ex), jnp.where(row == i2, ex / (1.0 + ex), 0.0))
        member = ((row == i1) | (row == i2))[0:SUBLANES]
        before = (lax.broadcasted_iota(jnp.int32, (tm, tm), 0)
                  < lax.broadcasted_iota(jnp.int32, (tm, tm), 1)).astype(BF16)
        rank = jnp.dot(jnp.where(member, 1.0, 0.0).astype(BF16), before, preferred_element_type=F32)
        rank = jnp.where(member, rank, -1.0)
        rank_ref[...] = rank
        comb_ref[...] = comb[0:SUBLANES]
        rankc_ref[...] = jnp.transpose(rank)
        y_ref[...] = jnp.zeros_like(y_ref)

    rrow = rank_ref[pl.ds(e, 1), :]
    crow = comb_ref[pl.ds(e, 1), :]
    rc = rankc_ref[...]
    rcol = jnp.sum(jnp.where(lax.broadcasted_iota(jnp.int32, rc.shape, 1) == e, rc, 0.0), axis=1, keepdims=True)
    n_tok = jnp.max(rrow).astype(jnp.int32) + 1

    def body(c, carry):
        base = (c * chunk).astype(F32)
        pick = rrow == lax.broadcasted_iota(jnp.int32, (chunk, tm), 0).astype(F32) + base
        xg = jnp.dot(jnp.where(pick, 1.0, 0.0).astype(BF16), xn_ref[...],
                     preferred_element_type=F32).astype(BF16)
        wcol = jnp.sum(jnp.where(pick, crow, 0.0), axis=1, keepdims=True)
        h = _silu(jnp.dot(xg, wg_ref[0], preferred_element_type=F32)) * \
            jnp.dot(xg, wu_ref[0], preferred_element_type=F32)
        yv = jnp.dot((h * wcol).astype(BF16), wd_ref[0], preferred_element_type=F32)
        place = jnp.where(rcol == lax.broadcasted_iota(jnp.int32, (tm, chunk), 1).astype(F32) + base,
                          1.0, 0.0).astype(BF16)
        y_ref[...] += jnp.dot(place, yv.astype(BF16), preferred_element_type=F32)
        return carry

    lax.fori_loop(0, (n_tok + chunk - 1) // chunk, body, 0)

    @pl.when(e == n_exp - 1)
    def _():
        y_ref[...] = _rmsnorm(x_ref[...] + y_ref[...], gf_ref[...])


def _moe_final(x2d, gain, w_router, wg, wu, wd, gain_final, name):
    M, D = x2d.shape
    n_exp, _, Fd = wg.shape
    assert n_exp <= SUBLANES
    tm = _tile(M, 1024)
    chunk = min(320, tm) if tm >= 1024 else min(128, tm)
    wrt = jnp.zeros((LANES, D), F32).at[:n_exp].set(w_router.T)
    return pl.pallas_call(
        functools.partial(_moe_kernel, n_exp=n_exp, chunk=chunk), grid=(M // tm, n_exp),
        in_specs=[pl.BlockSpec((tm, D), lambda i, e: (i, 0)), pl.BlockSpec((1, D), lambda i, e: (0, 0)),
                  pl.BlockSpec((LANES, D), lambda i, e: (0, 0)),
                  pl.BlockSpec((1, D, Fd), lambda i, e: (e, 0, 0)),
                  pl.BlockSpec((1, D, Fd), lambda i, e: (e, 0, 0)),
                  pl.BlockSpec((1, Fd, D), lambda i, e: (e, 0, 0)),
                  pl.BlockSpec((1, D), lambda i, e: (0, 0))],
        out_specs=pl.BlockSpec((tm, D), lambda i, e: (i, 0)),
        out_shape=jax.ShapeDtypeStruct((M, D), F32), name=name,
        scratch_shapes=[pltpu.VMEM((tm, D), BF16), pltpu.VMEM((SUBLANES, tm), F32),
                        pltpu.VMEM((SUBLANES, tm), F32), pltpu.VMEM((tm, SUBLANES), F32)],
        compiler_params=_cparams(("parallel", "arbitrary")))(
            x2d, gain.reshape(1, D), wrt, wg, wu, wd, gain_final.reshape(1, D))


def _band_kernel(*refs, n_hp, dil, span, has_prev):
    group = lambda i: refs[i * n_hp:(i + 1) * n_hp]
    if has_prev:
        q_refs, kp_refs, kc_refs, vp_refs, vc_refs = [group(i) for i in range(5)]
        bias_ref, o_ref, lse_ref, o_scr = refs[5 * n_hp:]
    else:
        q_refs, kc_refs, vc_refs = [group(i) for i in range(3)]
        bias_ref, o_ref, lse_ref, o_scr = refs[3 * n_hp:]
    t, hs = pl.program_id(1), pl.program_id(2)
    tq = q_refs[0].shape[1] // dil
    if has_prev:
        rowk = lax.broadcasted_iota(jnp.int32, (span + tq, tq), 0)
        bias = bias_ref[...] + jnp.where((rowk < span) & (t == 0), NEG, 0.0)
    else:
        bias = bias_ref[span:, :]
    row = lax.broadcasted_iota(jnp.int32, (LANES, tq), 0)
    lane = lax.broadcasted_iota(jnp.int32, (tq, LANES), 1)
    first_head = hs * (2 * n_hp)

    @pl.when(hs == 0)
    def _():
        lse_ref[...] = jnp.zeros_like(lse_ref)

    def stream(r, carry):
        rows = lambda n: pl.ds(r, n, stride=dil) if dil > 1 else pl.ds(0, n)
        lses = []
        for j in range(n_hp):
            qT = jnp.transpose(q_refs[j][0, rows(tq), :]).astype(BF16)
            k, v = kc_refs[j][0, rows(tq), :], vc_refs[j][0, rows(tq), :]
            if has_prev:
                k = jnp.concatenate([kp_refs[j][0, rows(span), :], k], axis=0)
                v = jnp.concatenate([vp_refs[j][0, rows(span), :], v], axis=0)
            k = k.astype(BF16)
            vT = jnp.transpose(v).astype(BF16)
            outs = []
            for h in range(2):
                qpad = jnp.where(row // HEAD_DIM == h, qT, jnp.zeros_like(qT))
                s = jnp.dot(k, qpad, preferred_element_type=F32) + bias
                m = jnp.max(s, axis=0, keepdims=True)
                p = jnp.exp(s - m)
                l = jnp.sum(p, axis=0, keepdims=True)
                outs.append(jnp.dot(vT[h * HEAD_DIM:(h + 1) * HEAD_DIM], p.astype(BF16),
                                    preferred_element_type=F32) / l)
                lses.append(m + jnp.log(l))
            o_scr[j, rows(tq), :] = jnp.transpose(jnp.concatenate(outs, axis=0))
        stat = jnp.transpose(jnp.concatenate(lses + [jnp.zeros((LANES - len(lses), tq), F32)], axis=0))
        stat = pltpu.roll(stat, first_head, 1)
        mine = (lane >= first_head) & (lane < first_head + 2 * n_hp)
        lse_ref[0, rows(tq), :] = jnp.where(mine, stat, lse_ref[0, rows(tq), :])
        return carry

    lax.fori_loop(0, dil, stream, 0)
    for j in range(n_hp):
        o_ref[0, :, j * LANES:(j + 1) * LANES] = o_scr[j]


def _band_attention(q, k, v, dil, span, tq, name, n_hp=2):
    B, S, W = q.shape
    L = S // dil
    tq = min(tq, L)
    assert L % tq == 0 and tq % span == 0
    per_tile = tq // span
    n_pairs = W // LANES
    assert n_pairs % n_hp == 0
    i = np.arange(span + tq)[:, None]
    j = np.arange(tq)[None, :]
    bias = jnp.asarray(np.where((j - i + span >= 0) & (j - i + span <= span), 0.0, NEG), F32)
    cur = lambda jj: pl.BlockSpec((1, dil * tq, LANES), lambda b, t, hs: (b, t, hs * n_hp + jj))
    prev = lambda jj: pl.BlockSpec((1, dil * span, LANES),
                                   lambda b, t, hs: (b, jnp.maximum(t * per_tile - 1, 0), hs * n_hp + jj))
    each = lambda mk: [mk(jj) for jj in range(n_hp)]
    has_prev = L > tq
    if has_prev:
        specs = each(cur) + each(prev) + each(cur) + each(prev) + each(cur)
        args = [q] * n_hp + [k] * (2 * n_hp) + [v] * (2 * n_hp)
    else:
        specs = each(cur) * 3
        args = [q] * n_hp + [k] * n_hp + [v] * n_hp
    res = pl.pallas_call(
        functools.partial(_band_kernel, n_hp=n_hp, dil=dil, span=span, has_prev=has_prev),
        grid=(B, L // tq, n_pairs // n_hp),
        in_specs=specs + [pl.BlockSpec(bias.shape, lambda b, t, hs: (0, 0))],
        out_specs=[pl.BlockSpec((1, dil * tq, n_hp * LANES), lambda b, t, hs: (b, t, hs)),
                   pl.BlockSpec((1, dil * tq, LANES), lambda b, t, hs: (b, t, 0))],
        out_shape=[jax.ShapeDtypeStruct((B, S, W), F32), jax.ShapeDtypeStruct((B, S, LANES), F32)],
        scratch_shapes=[pltpu.VMEM((n_hp, dil * tq, LANES), F32)],
        name=name, compiler_params=_cparams(("parallel", "parallel", "arbitrary")))(*args, bias)
    return res[0].reshape(B * S, W), res[1].reshape(B * S, LANES)


def _merge_outproj_kernel(*refs, n_groups):
    x_ref = refs[0]
    o_refs, l_refs = refs[1:1 + n_groups], refs[1 + n_groups:1 + 2 * n_groups]
    w_ref, y_ref = refs[-2], refs[-1]
    W = o_refs[0].shape[1]
    lses = [r[...] for r in l_refs]
    m = functools.reduce(jnp.maximum, lses)
    es = [jnp.exp(l - m) for l in lses]
    tot = functools.reduce(lambda a, b: a + b, es)
    expand = (lax.broadcasted_iota(jnp.int32, (LANES, W), 1) // HEAD_DIM
              == lax.broadcasted_iota(jnp.int32, (LANES, W), 0)).astype(BF16)
    mix = None
    for e, o_ref in zip(es, o_refs):
        hi, mid, _ = _split3(e / tot)
        wexp = jnp.dot(hi, expand, preferred_element_type=F32) + jnp.dot(mid, expand, preferred_element_type=F32)
        mix = wexp * o_ref[...] if mix is None else mix + wexp * o_ref[...]
    y_ref[...] = x_ref[...] + jnp.dot(mix.astype(BF16), w_ref[...], preferred_element_type=F32)


def _merge_outproj(x2d, outs, lses, w_bf, name):
    M, D = x2d.shape
    W = outs[0].shape[1]
    tm = _tile(M, 256)
    row = lambda n: pl.BlockSpec((tm, n), lambda i: (i, 0))
    return pl.pallas_call(
        functools.partial(_merge_outproj_kernel, n_groups=len(outs)), grid=(M // tm,),
        in_specs=[row(D)] + [row(W)] * len(outs) + [row(LANES)] * len(lses)
        + [pl.BlockSpec(w_bf.shape, lambda i: (0, 0))],
        out_specs=row(D), out_shape=jax.ShapeDtypeStruct((M, D), F32), name=name,
        compiler_params=_cparams(("parallel",)))(x2d, *outs, *lses, w_bf)


def _head_major(x, n_heads, pad_rows):
    DB, T, _ = x.shape
    y = x.reshape(DB, T, n_heads, HEAD_DIM).transpose(0, 2, 1, 3)
    return jnp.pad(y, ((0, 0), (0, 0), (0, pad_rows - T), (0, 0)))


def _new_cols(x, n_heads):
    DB, T, _ = x.shape
    y = x.reshape(DB, T, n_heads, HEAD_DIM).transpose(0, 2, 3, 1)
    return jnp.pad(y, ((0, 0), (0, 0), (0, 0), (0, NEW_PAD - T)))


def _row_of(col_vec_row):
    return jnp.transpose(jnp.broadcast_to(col_vec_row, (SUBLANES, LANES)))[0:SUBLANES, 0:1]


def _fox_dec_kernel(pt_ref, *refs, n_pg, n_q):
    kts, vts, lfs = refs[:n_pg], refs[n_pg:2 * n_pg], refs[2 * n_pg:3 * n_pg]
    q_ref, knt_ref, vnt_ref, lfn_ref, o_ref, qbd_ref, base_ref, carry_ref, m_ref, l_ref, acc_ref = refs[3 * n_pg:]
    p = pl.program_id(1)
    nh = FOX_HEADS
    page = LANES
    rows, width = nh * QPAD, nh * HEAD_DIM
    lane = lax.broadcasted_iota(jnp.int32, (QPAD, LANES), 1)
    trow = lax.broadcasted_iota(jnp.int32, (QPAD, LANES), 0)
    r = lax.broadcasted_iota(jnp.int32, (page, page), 0)
    c = lax.broadcasted_iota(jnp.int32, (page, page), 1)
    stack = lambda xs: jnp.concatenate(xs, axis=0)
    own = (lax.broadcasted_iota(jnp.int32, (rows, width), 0) // QPAD
           == lax.broadcasted_iota(jnp.int32, (rows, width), 1) // HEAD_DIM)
    heads_of = lambda ref: ref[0].reshape(width, ref.shape[3])

    @pl.when(p == 0)
    def _():
        _online_init(m_ref, l_ref, acc_ref)
        carry_ref[...] = jnp.zeros_like(carry_ref)
        q_all = q_ref[0].reshape(rows, HEAD_DIM) * SCALE
        qbd_ref[...] = jnp.where(own, jnp.concatenate([q_all] * nh, axis=1), 0.0).astype(BF16)
        cnew = _mm3_left(lfn_ref[0], r <= c)
        bases, decs = [], []
        for h in range(nh):
            bh = _row_of(cnew[h:h + 1, :])
            bases.append(bh)
            decs.append(bh - cnew[h:h + 1, :])
        base_ref[...] = jnp.broadcast_to(stack(bases), base_ref.shape)
        live = (lane <= trow) & (lane < n_q)
        s = jnp.dot(qbd_ref[...], heads_of(knt_ref).astype(BF16), preferred_element_type=F32) + stack(decs)
        s = jnp.where(stack([live] * nh), s, NEG)
        alpha, pr = _online_step(s, m_ref, l_ref)
        acc_ref[...] = alpha * acc_ref[...] + _mm_nt(pr, heads_of(vnt_ref))

    lf_all = stack([lf[0] for lf in lfs])
    after_all = _mm3_left(lf_all, r > c)
    tot_all = after_all[:, 0:1] + lf_all[:, 0:1]
    carry = carry_ref[:, 0:1]
    decay = [None] * n_pg
    for j in reversed(range(n_pg)):
        decay[j] = carry + after_all[j * nh:(j + 1) * nh]
        carry = carry + tot_all[j * nh:(j + 1) * nh]
    carry_ref[...] = jnp.broadcast_to(carry, carry_ref.shape)
    per_row = lambda d: stack([jnp.broadcast_to(d[h:h + 1, :], (QPAD, page)) for h in range(nh)])
    s = jnp.concatenate([jnp.dot(qbd_ref[...], heads_of(kts[j]).astype(BF16), preferred_element_type=F32)
                         + per_row(decay[j]) for j in range(n_pg)], axis=1) + base_ref[:, 0:1]
    alpha, pr = _online_step(s, m_ref, l_ref)
    acc_ref[...] = alpha * acc_ref[...] + sum(_mm_nt(pr[:, j * page:(j + 1) * page], heads_of(vts[j]))
                                              for j in range(n_pg))

    @pl.when(p == pl.num_programs(1) - 1)
    def _():
        acc = jnp.where(own, acc_ref[...], 0.0)
        o_ref[0] = sum(acc[:, h * HEAD_DIM:(h + 1) * HEAD_DIM] for h in range(nh)) / l_ref[...]


def _fox_decode(q, k_new, v_new, lf_new, cache_k, cache_v, cache_lf, pt_flat):
    DB, T, W = q.shape
    n_pool, page, nh, _ = cache_k.shape
    assert page == LANES and T <= QPAD
    n_pages = pt_flat.shape[0] // DB
    n_pg = math.gcd(n_pages, PAGES_PER_STEP)
    n_steps = n_pages // n_pg
    ckt = jnp.transpose(cache_k, (0, 2, 3, 1))
    cvt = jnp.transpose(cache_v, (0, 2, 3, 1))
    clf = jnp.transpose(cache_lf, (0, 2, 1))
    qh = _head_major(q, nh, QPAD)
    knt, vnt = _new_cols(k_new, nh), _new_cols(v_new, nh)
    lfn = jnp.pad(jnp.transpose(lf_new, (0, 2, 1)), ((0, 0), (0, 0), (0, LANES - T)))
    page_idx = lambda j: (lambda b, p, pt: (pt[b * n_pages + (n_steps - 1 - p) * n_pg + j], 0, 0, 0))
    lf_idx = lambda j: (lambda b, p, pt: (pt[b * n_pages + (n_steps - 1 - p) * n_pg + j], 0, 0))
    per_db = lambda a: pl.BlockSpec((1,) + a.shape[1:], lambda b, p, pt: (b,) + (0,) * (a.ndim - 1))
    rows = nh * QPAD
    grid_spec = pltpu.PrefetchScalarGridSpec(
        num_scalar_prefetch=1, grid=(DB, n_steps),
        in_specs=([pl.BlockSpec((1, nh, HEAD_DIM, page), page_idx(j)) for j in range(n_pg)] * 2
                  + [pl.BlockSpec((1, nh, page), lf_idx(j)) for j in range(n_pg)]
                  + [per_db(qh), per_db(knt), per_db(vnt), per_db(lfn)]),
        out_specs=pl.BlockSpec((1, rows, HEAD_DIM), lambda b, p, pt: (b, 0, 0)),
        scratch_shapes=[pltpu.VMEM((rows, W), BF16), pltpu.VMEM((rows, LANES), F32), pltpu.VMEM((nh, LANES), F32),
                        pltpu.VMEM((rows, 1), F32), pltpu.VMEM((rows, 1), F32), pltpu.VMEM((rows, W), F32)])
    o = pl.pallas_call(
        functools.partial(_fox_dec_kernel, n_pg=n_pg, n_q=T), grid_spec=grid_spec,
        out_shape=jax.ShapeDtypeStruct((DB, rows, HEAD_DIM), F32), name="fox_decode",
        compiler_params=_cparams(("parallel", "arbitrary")))(
            pt_flat, *([ckt] * n_pg), *([cvt] * n_pg), *([clf] * n_pg), qh, knt, vnt, lfn)
    return o.reshape(DB, nh, QPAD, HEAD_DIM)[:, :, :T].transpose(0, 2, 1, 3).reshape(DB, T, W)


def _nsa_dec_a_kernel(q_ref, kct_ref, vct_ref, kwt_ref, vwt_ref, kwnt_ref, vwnt_ref, gt_ref, map_ref,
                      o_ref, bias_ref, *, n_q, ns, n_sel, past, n_keys):
    nr = NSA_GROUP * QPAD
    n_ch = kct_ref.shape[3]
    wb = kwt_ref.shape[3]
    t_row = lax.broadcasted_iota(jnp.int32, (nr, 1), 0) % QPAD
    imps = []
    for kv in range(NSA_KV_HEADS):
        q = (q_ref[0, kv] * SCALE).astype(BF16)
        cidx = lax.broadcasted_iota(jnp.int32, (nr, n_ch), 1)
        pc = _softmax_rows(_mm(q, kct_ref[0, kv]), cidx >= 1)
        oc = _mm_nt(pc, vct_ref[0, kv])
        pcat = jnp.concatenate([pc[j * QPAD:(j + 1) * QPAD] for j in range(NSA_GROUP)], axis=1)
        imps.append(jnp.dot(pcat.astype(BF16), map_ref[...], preferred_element_type=F32))
        sw = _mm(q, kwt_ref[0, kv])
        sn = _mm(q, kwnt_ref[0, kv])
        iw = lax.broadcasted_iota(jnp.int32, (nr, wb), 1)
        un = lax.broadcasted_iota(jnp.int32, (nr, NEW_PAD), 1)
        mw = (wb + t_row - iw <= NSA_WINDOW) & (past - wb + iw >= 0)
        mn = (un <= t_row) & (un < n_q)
        sw = jnp.where(mw, sw, NEG)
        sn = jnp.where(mn, sn, NEG)
        m = jnp.maximum(jnp.max(sw, axis=1, keepdims=True), jnp.max(sn, axis=1, keepdims=True))
        pw = jnp.where(mw, jnp.exp(sw - m), 0.0)
        pn = jnp.where(mn, jnp.exp(sn - m), 0.0)
        l = jnp.sum(pw, axis=1, keepdims=True) + jnp.sum(pn, axis=1, keepdims=True)
        ow = (_mm_nt(pw, vwt_ref[0, kv]) + _mm_nt(pn, vwnt_ref[0, kv])) / l
        gt = gt_ref[0, kv]
        o_ref[0, kv] = gt[:, 0:1] * oc + gt[:, 2:3] * ow

    imp = jnp.concatenate(imps, axis=0)
    t_sel = lax.broadcasted_iota(jnp.int32, (imp.shape[0], 1), 0) % QPAD
    sel = _select_blocks(imp, past + t_sel, ns, n_sel).astype(BF16)
    ch = 8 * LANES
    for c0 in range(0, n_keys, ch):
        w = min(ch, n_keys - c0)
        key = c0 + lax.broadcasted_iota(jnp.int32, (sel.shape[1], w), 1)
        onehot = (key // NSA_SLC_BLOCK == lax.broadcasted_iota(jnp.int32, (sel.shape[1], w), 0)).astype(BF16)
        chosen = jnp.dot(sel, onehot, preferred_element_type=F32) > 0.5
        kpos = c0 + lax.broadcasted_iota(jnp.int32, (sel.shape[0], w), 1)
        ok = chosen & (kpos <= past + t_sel) & (kpos < past + n_q)
        bias_ref[0, :, c0:c0 + w] = jnp.where(ok, 0.0, NEG)


def _nsa_dec_b_kernel(pt_ref, *refs, n_pg):
    kts, vts = refs[:n_pg], refs[n_pg:2 * n_pg]
    q_ref, knt_ref, vnt_ref, bias_ref, biasn_ref, o_ref, qbd_ref, m_ref, l_ref, acc_ref = refs[2 * n_pg:]
    p = pl.program_id(1)
    nr = NSA_GROUP * QPAD
    page = LANES
    rows, width = NSA_KV_HEADS * nr, NSA_KV_HEADS * HEAD_DIM
    own = (lax.broadcasted_iota(jnp.int32, (rows, width), 0) // nr
           == lax.broadcasted_iota(jnp.int32, (rows, width), 1) // HEAD_DIM)
    heads_of = lambda ref: ref[0].reshape(width, ref.shape[3])
    per_row = lambda b: jnp.concatenate([b[kv * QPAD:(kv + 1) * QPAD] for kv in range(NSA_KV_HEADS)
                                         for _ in range(NSA_GROUP)], axis=0)

    @pl.when(p == 0)
    def _():
        _online_init(m_ref, l_ref, acc_ref)
        q_all = q_ref[0].reshape(rows, HEAD_DIM) * SCALE
        qbd_ref[...] = jnp.where(own, jnp.concatenate([q_all] * NSA_KV_HEADS, axis=1), 0.0).astype(BF16)

    s = jnp.concatenate([jnp.dot(qbd_ref[...], heads_of(kts[j]).astype(BF16), preferred_element_type=F32)
                         for j in range(n_pg)], axis=1) + per_row(bias_ref[0])
    alpha, pr = _online_step(s, m_ref, l_ref)
    acc_ref[...] = alpha * acc_ref[...] + sum(_mm_nt(pr[:, j * page:(j + 1) * page], heads_of(vts[j]))
                                              for j in range(n_pg))

    @pl.when(p == pl.num_programs(1) - 1)
    def _():
        s = jnp.dot(qbd_ref[...], heads_of(knt_ref).astype(BF16), preferred_element_type=F32) + per_row(biasn_ref[0])
        alpha, pr = _online_step(s, m_ref, l_ref)
        acc = jnp.where(own, alpha * acc_ref[...] + _mm_nt(pr, heads_of(vnt_ref)), 0.0)
        o_ref[0] = sum(acc[:, kv * HEAD_DIM:(kv + 1) * HEAD_DIM] for kv in range(NSA_KV_HEADS)) / l_ref[...]


def _nsa_decode(qb, kct, vct, cache_ks, cache_vs, ks_new, vs_new, swa_k, swa_v, kw_new, vw_new,
                gates, pt_flat, past):
    DB, T, W = qb.shape
    n_pool, page = cache_ks.shape[:2]
    assert page == LANES and past % LANES == 0 and T <= QPAD
    n_pages = pt_flat.shape[0] // DB
    n_ch = kct.shape[3]
    ns = -(-(past + T) // NSA_SLC_BLOCK)
    ns_pad = -(-ns // LANES) * LANES
    n_sel = min(NSA_TOPN, ns)
    nr = NSA_GROUP * QPAD
    n_keys = past + NEW_PAD
    q5 = _head_major(qb, NSA_HEADS, QPAD).reshape(DB, NSA_KV_HEADS, nr, HEAD_DIM)
    g5 = _head_major(jnp.pad(gates.reshape(DB, T, NSA_HEADS, 3), ((0, 0),) * 3 + ((0, HEAD_DIM - 3),))
                     .reshape(DB, T, NSA_HEADS * HEAD_DIM), NSA_HEADS, QPAD)[..., :3]
    g5 = g5.reshape(DB, NSA_KV_HEADS, nr, 3)
    smap = jnp.asarray(np.tile(_slc_map_rows(n_ch, ns_pad), (NSA_GROUP, 1)), dtype=BF16)
    kwt = jnp.transpose(swa_k, (0, 2, 3, 1))
    vwt = jnp.transpose(swa_v, (0, 2, 3, 1))
    kwnt, vwnt = _new_cols(kw_new, NSA_KV_HEADS), _new_cols(vw_new, NSA_KV_HEADS)
    per = lambda a: pl.BlockSpec((1,) + a.shape[1:], lambda b: (b,) + (0,) * (a.ndim - 1))
    part, bias = pl.pallas_call(
        functools.partial(_nsa_dec_a_kernel, n_q=T, ns=ns, n_sel=n_sel, past=past, n_keys=n_keys), grid=(DB,),
        in_specs=[per(q5), per(kct), per(vct), per(kwt), per(vwt), per(kwnt), per(vwnt), per(g5),
                  pl.BlockSpec(smap.shape, lambda b: (0, 0))],
        out_specs=[pl.BlockSpec((1, NSA_KV_HEADS, nr, HEAD_DIM), lambda b: (b, 0, 0, 0)),
                   pl.BlockSpec((1, NSA_KV_HEADS * QPAD, n_keys), lambda b: (b, 0, 0))],
        out_shape=[jax.ShapeDtypeStruct((DB, NSA_KV_HEADS, nr, HEAD_DIM), F32),
                   jax.ShapeDtypeStruct((DB, NSA_KV_HEADS * QPAD, n_keys), F32)], name="nsa_decode_a",
        compiler_params=_cparams(("parallel",)))(q5, kct, vct, kwt, vwt, kwnt, vwnt, g5, smap)

    n_pg = math.gcd(n_pages, 2 * PAGES_PER_STEP)
    ckt = jnp.transpose(cache_ks, (0, 2, 3, 1))
    cvt = jnp.transpose(cache_vs, (0, 2, 3, 1))
    ksnt, vsnt = _new_cols(ks_new, NSA_KV_HEADS), _new_cols(vs_new, NSA_KV_HEADS)
    page_idx = lambda j: (lambda b, p, pt: (pt[b * n_pages + p * n_pg + j], 0, 0, 0))
    per_db = lambda a: pl.BlockSpec((1,) + a.shape[1:], lambda b, p, pt: (b,) + (0,) * (a.ndim - 1))
    rows = NSA_KV_HEADS * nr
    grid_spec = pltpu.PrefetchScalarGridSpec(
        num_scalar_prefetch=1, grid=(DB, n_pages // n_pg),
        in_specs=([pl.BlockSpec((1, NSA_KV_HEADS, HEAD_DIM, page), page_idx(j)) for j in range(n_pg)] * 2
                  + [per_db(q5), per_db(ksnt), per_db(vsnt),
                     pl.BlockSpec((1, NSA_KV_HEADS * QPAD, n_pg * page), lambda b, p, pt: (b, 0, p)),
                     pl.BlockSpec((1, NSA_KV_HEADS * QPAD, NEW_PAD), lambda b, p, pt: (b, 0, past // NEW_PAD))]),
        out_specs=pl.BlockSpec((1, rows, HEAD_DIM), lambda b, p, pt: (b, 0, 0)),
        scratch_shapes=[pltpu.VMEM((rows, LANES), BF16), pltpu.VMEM((rows, 1), F32), pltpu.VMEM((rows, 1), F32),
                        pltpu.VMEM((rows, LANES), F32)])
    osl = pl.pallas_call(
        functools.partial(_nsa_dec_b_kernel, n_pg=n_pg), grid_spec=grid_spec,
        out_shape=jax.ShapeDtypeStruct((DB, rows, HEAD_DIM), F32), name="nsa_decode_b",
        compiler_params=_cparams(("parallel", "arbitrary")))(
            pt_flat, *([ckt] * n_pg), *([cvt] * n_pg), q5, ksnt, vsnt, bias, bias)

    o = part + g5[..., 1:2] * osl.reshape(DB, NSA_KV_HEADS, nr, HEAD_DIM)
    return o.reshape(DB, NSA_HEADS, QPAD, HEAD_DIM)[:, :, :T].transpose(0, 2, 1, 3).reshape(DB, T, W)


def _dil_dec_kernel(q_ref, kt_ref, vt_ref, knt_ref, vnt_ref, o_ref, *, n_q, wc):
    hb = kt_ref.shape[1]
    t = lax.broadcasted_iota(jnp.int32, (QPAD, 1), 0)

    def log_mult(d, ok):
        w = jnp.zeros(d.shape, F32)
        for window, dil in DIL_PAIRS:
            w = w + ((d >= 0) & (d <= window) & (d % dil == 0)).astype(F32)
        return jnp.where(ok, w, 0.0)

    wk = log_mult(wc + t - lax.broadcasted_iota(jnp.int32, (QPAD, wc), 1), t < n_q)
    un = lax.broadcasted_iota(jnp.int32, (QPAD, NEW_PAD), 1)
    wn = log_mult(t - un, (t < n_q) & (un < n_q))
    for h in range(hb):
        q = (q_ref[0, h] * SCALE).astype(BF16)
        sk = jnp.where(wk > 0.0, _mm(q, kt_ref[0, h]), NEG)
        sn = jnp.where(wn > 0.0, _mm(q, knt_ref[0, h]), NEG)
        m = jnp.maximum(jnp.max(sk, axis=1, keepdims=True), jnp.max(sn, axis=1, keepdims=True))
        pk = wk * jnp.exp(sk - m)
        pn = wn * jnp.exp(sn - m)
        l = jnp.sum(pk, axis=1, keepdims=True) + jnp.sum(pn, axis=1, keepdims=True)
        o_ref[0, h] = (_mm_nt(pk, vt_ref[0, h]) + _mm_nt(pn, vnt_ref[0, h])) / jnp.where(l > 0.0, l, 1.0)


def _dilated_decode(q, k_new, v_new, cache_k, cache_v):
    DB, T, W = q.shape
    wc, nh = cache_k.shape[1], cache_k.shape[2]
    ckt = jnp.transpose(cache_k, (0, 2, 3, 1))
    cvt = jnp.transpose(cache_v, (0, 2, 3, 1))
    qh = _head_major(q, nh, QPAD)
    knt, vnt = _new_cols(k_new, nh), _new_cols(v_new, nh)
    hb = math.gcd(nh, 4)
    spec = lambda a: pl.BlockSpec((1, hb) + a.shape[2:], lambda b, j: (b, j, 0, 0))
    o = pl.pallas_call(
        functools.partial(_dil_dec_kernel, n_q=T, wc=wc), grid=(DB, nh // hb),
        in_specs=[spec(qh), spec(ckt), spec(cvt), spec(knt), spec(vnt)],
        out_specs=pl.BlockSpec((1, hb, QPAD, HEAD_DIM), lambda b, j: (b, j, 0, 0)),
        out_shape=jax.ShapeDtypeStruct((DB, nh, QPAD, HEAD_DIM), F32), name="dilated_decode",
        compiler_params=_cparams(("parallel", "parallel")))(qh, ckt, cvt, knt, vnt)
    return o[:, :, :T].transpose(0, 2, 1, 3).reshape(DB, T, W)


def kernel(x_prompt, x_sample, cache_a_k, cache_a_v, cache_a_logf, cache_b_cmp_k, cache_b_cmp_v, cache_b_slc_k, cache_b_slc_v, cache_b_swa_k, cache_b_swa_v, cache_c_k, cache_c_v, page_table, norm_mix0, w_in0, fox_bf, nsa_pe_k, nsa_w1_k, nsa_w2_k, nsa_pe_v, nsa_w1_v, nsa_w2_v, w_out0, norm_ffn0, ffn_w_gate, ffn_w_up, ffn_w_down, norm_mix1, w_in1, w_out1, norm_ffn1, moe_router, moe_w_gate, moe_w_up, moe_w_down, norm_final):
    B, S, D = x_prompt.shape
    DB, T, _ = x_sample.shape
    n_pages = page_table.shape[1]
    past = n_pages * cache_a_k.shape[1]
    pt_flat = page_table.reshape(-1).astype(jnp.int32)
    fw = FOX_HEADS * HEAD_DIM
    nw = NSA_HEADS * HEAD_DIM
    kvw = NSA_KV_HEADS * HEAD_DIM

    cuts = np.cumsum([0, fw, fw, fw, FOX_HEADS, nw] + [kvw] * 6 + [3 * NSA_HEADS])
    col = lambda i: w_in0[:, cuts[i]:cuts[i + 1]]
    qa_w, ka_w, va_w, fa_w, qb_w, kc_w, vc_w, ks_w, vs_w, kw_w, vw_w, gb_w = [col(i) for i in range(12)]

    f32_out = ((F32, 1.0),)
    bf16_out = ((BF16, 1.0),)

    def pack(ws, ropes, emits=None):
        widths = [w.shape[1] for w in ws]
        starts = np.concatenate([[0], np.cumsum(widths)[:-1]])
        emits = emits or [f32_out] * len(ws)
        return (jnp.concatenate(ws, axis=1).astype(BF16),
                [(int(s), int(w), r, e) for s, w, r, e in zip(starts, widths, ropes, emits)])

    both_out = ((F32, 1.0), (BF16, 1.0))
    q_out = ((BF16, SCALE),)
    w0r, segs0r = pack([kc_w, vc_w, ka_w, ks_w, kw_w], [True, False, False, True, True],
                       [f32_out] * 2 + [bf16_out] * 3)
    w0c, segs0c = pack([qa_w, ka_w, va_w, qb_w, kc_w, ks_w, kw_w, vc_w, vs_w, vw_w, gb_w],
                       [False, False, False, True, True, True, True, False, False, False, "sigmoid"],
                       [q_out, f32_out, both_out, q_out, f32_out, f32_out, f32_out, f32_out, both_out, both_out,
                        f32_out])
    w0c = w0c.T
    w0s, segs0s = pack([qa_w, ka_w, va_w, qb_w, kc_w, ks_w, kw_w, vc_w, vs_w, vw_w],
                       [False, False, False, True, True, True, True, False, False, False])
    wgate = gb_w.astype(BF16)
    wft = jnp.zeros((16, D), F32).at[:FOX_HEADS].set(fa_w.T).astype(BF16)
    wfr = jnp.zeros((D, LANES), F32).at[:, :FOX_HEADS].set(fa_w).astype(BF16)
    brow = jnp.zeros((1, LANES), F32).at[0, :FOX_HEADS].set(fox_bf)
    logf_args = (wft, fox_bf.reshape(FOX_HEADS, 1).astype(F32), wfr, brow)
    dw = w_in1.shape[1] // 3
    w1 = w_in1.astype(BF16)
    segs1 = [(0, dw, True, f32_out), (dw, dw, True, f32_out), (2 * dw, dw, False, f32_out)]
    w1c = w1[:, dw:].T
    segs1c = [(0, dw, True, f32_out), (dw, dw, False, f32_out)]
    w_out0_b, w_out1_b = w_out0.astype(BF16), w_out1.astype(BF16)
    ffn_g, ffn_u, ffn_d = ffn_w_gate.astype(BF16), ffn_w_up.astype(BF16), ffn_w_down.astype(BF16)
    moe_g, moe_u, moe_d = moe_w_gate.astype(BF16), moe_w_up.astype(BF16), moe_w_down.astype(BF16)
    cmp_k_w = _cmp_weights(nsa_pe_k, nsa_w1_k, nsa_w2_k)
    cmp_v_w = _cmp_weights(nsa_pe_v, nsa_w1_v, nsa_w2_v)

    tab_p = _rope_tables(jnp.arange(S))
    tab_s = _rope_tables(past + jnp.arange(DB * T) % T)
    tm_p = _tile(S, 512)
    npb = S // tm_p
    win_b = min(NSA_WINDOW, S)
    win_c = min(DIL_WINDOW_MAX, S)

    xp = x_prompt.reshape(B * S, D)
    (kc, vc, ka_b, ks_b, kw_b, qat_b, kat, vat, vat_b, qbt_b, kct, kst, kwt, vct, vst, vst_b, vwt, vwt_b,
     gates_t, lft_p, lf_rows) = _project(
        xp, norm_mix0, tab_p, npb, tm_p, w=w0r, row_segs=segs0r, wt=w0c, col_segs=segs0c,
        logf=logf_args, name="proj0_prompt")
    r3 = lambda a: a.reshape(B, S, a.shape[-1])
    heads = lambda a, h: a.reshape(B, h, HEAD_DIM, a.shape[-1])
    kaug, c0 = _fox_prep(r3(ka_b), r3(lf_rows), tm_p)
    o_at = _fox_prompt(qat_b, kaug, vat_b, c0, tm_p)
    kcmp_p = _compress(r3(kc), cmp_k_w, "compress_k_prompt")
    vcmp_p = _compress(r3(vc), cmp_v_w, "compress_v_prompt")
    kv2 = lambda a: heads(a, NSA_KV_HEADS)
    o_bt = _nsa_prompt(qbt_b, kv2(kcmp_p), kv2(vcmp_p), r3(ks_b), vst_b, r3(kw_b), vwt_b, gates_t)
    hp = _outproj(xp, [o_at, o_bt], w_out0_b, "outproj0_prompt", tm=tm_p)
    hp = _ffn(hp, norm_ffn0, ffn_g, ffn_u, ffn_d, "ffn_prompt")

    xs = x_sample.reshape(DB * T, D)
    (qa_s, ka_s, va_s, qb_s, kc_s, ks_s, kw_s, vc_s, vs_s, vw_s, gates_s, lft_s, _) = _project(
        xs, norm_mix0, tab_s, 1, DB * T, w=w0s, row_segs=segs0s, w_gate=wgate, logf=logf_args,
        name="proj0_sample")
    s3 = lambda a: a.reshape(DB, T, a.shape[-1])
    lf_s = jnp.transpose(lft_s[0].reshape(FOX_HEADS, DB, T), (1, 2, 0))
    o_a_s = _fox_decode(s3(qa_s), s3(ka_s), s3(va_s), lf_s, cache_a_k, cache_a_v, cache_a_logf, pt_flat)
    kcmp_s = _compress_paged(cache_b_cmp_k, pt_flat, DB, cmp_k_w, "compress_k_paged")
    vcmp_s = _compress_paged(cache_b_cmp_v, pt_flat, DB, cmp_v_w, "compress_v_paged")
    kvs = lambda a: a.reshape(DB, NSA_KV_HEADS, HEAD_DIM, a.shape[-1])
    o_b_s = _nsa_decode(s3(qb_s), kvs(kcmp_s), kvs(vcmp_s), cache_b_slc_k, cache_b_slc_v, s3(ks_s), s3(vs_s),
                        cache_b_swa_k, cache_b_swa_v, s3(kw_s), s3(vw_s), s3(gates_s), pt_flat, past)
    hs = _outproj(xs, [o_a_s.reshape(DB * T, fw), o_b_s.reshape(DB * T, nw)], w_out0_b, "outproj0_sample")
    hs = _ffn(hs, norm_ffn0, ffn_g, ffn_u, ffn_d, "ffn_sample")

    first_c = (S - win_c) // tm_p
    segs1p = [(0, dw, True, ((F32, SCALE),)), (dw, dw, True, f32_out), (2 * dw, dw, False, f32_out)]
    q1, k1, v1, k1t, v1t = _project(hp, norm_mix1, tab_p, npb, tm_p, w=w1, row_segs=segs1p, wt=w1c,
                                    col_segs=segs1c, col_from=(npb, first_c), name="proj1_prompt")
    groups = [_band_attention(q1.reshape(B, S, dw), k1.reshape(B, S, dw), v1.reshape(B, S, dw), dil,
                              window // dil, 512 if dil == 1 else 256, "dilated_prompt_d%d" % dil)
              for window, dil in DIL_PAIRS]
    hp = _merge_outproj(hp, [g[0] for g in groups], [g[1] for g in groups], w_out1_b, "outproj1_prompt")
    y_prompt = _moe_final(hp, norm_ffn1, moe_router, moe_g, moe_u, moe_d, norm_final, "moe_prompt").reshape(B, S, D)

    q1s, k1s, v1s = _project(hs, norm_mix1, tab_s, 1, DB * T, w=w1, row_segs=segs1, name="proj1_sample")
    o1s = _dilated_decode(s3(q1s), s3(k1s), s3(v1s), cache_c_k, cache_c_v)
    hs = _outproj(hs, [o1s.reshape(DB * T, dw)], w_out1_b, "outproj1_sample")
    y_sample = _moe_final(hs, norm_ffn1, moe_router, moe_g, moe_u, moe_d, norm_final, "moe_sample").reshape(DB, T, D)

    def state(a, h, last=None):
        a = a.reshape(a.shape[0], h, HEAD_DIM, a.shape[-1])
        if last is not None:
            a = a[..., a.shape[-1] - last:]
        return jnp.transpose(a, (0, 3, 1, 2))

    h4 = lambda a, h: a.reshape(DB, T, h, HEAD_DIM)
    nh1 = dw // HEAD_DIM
    return (y_prompt, y_sample,
            state(kat, FOX_HEADS), state(vat, FOX_HEADS), jnp.transpose(lft_p, (0, 2, 1)),
            state(kct, NSA_KV_HEADS), state(vct, NSA_KV_HEADS), state(kst, NSA_KV_HEADS), state(vst, NSA_KV_HEADS),
            state(kwt, NSA_KV_HEADS, win_b), state(vwt, NSA_KV_HEADS, win_b),
            state(k1t, nh1, win_c), state(v1t, nh1, win_c),
            h4(ka_s, FOX_HEADS), h4(va_s, FOX_HEADS), lf_s,
            h4(kc_s, NSA_KV_HEADS), h4(vc_s, NSA_KV_HEADS), h4(ks_s, NSA_KV_HEADS),
            h4(vs_s, NSA_KV_HEADS), h4(kw_s, NSA_KV_HEADS), h4(vw_s, NSA_KV_HEADS),
            h4(k1s, nh1), h4(v1s, nh1))
```

```python
import functools
import math

import numpy as np
import jax
import jax.numpy as jnp
from jax import lax
from jax.experimental import pallas as pl
from jax.experimental.pallas import tpu as pltpu

F32 = jnp.float32
BF16 = jnp.bfloat16

HEAD_DIM = 64
HALF = HEAD_DIM // 2
LANES = 128
SUBLANES = 8
ROPE_THETA = 10000.0
RMS_EPS = 1e-6
NEG = -1e30
MASK_BIG = 30000.0
SCALE = HEAD_DIM ** -0.5

FOX_HEADS = 8
NSA_HEADS = 8
NSA_KV_HEADS = 2
NSA_GROUP = NSA_HEADS // NSA_KV_HEADS
NSA_CMP_LEN = 32
NSA_CMP_STRIDE = 16
NSA_SLC_BLOCK = 64
NSA_TOPN = 16
NSA_WINDOW = 512
NSA_FORCE_BONUS = 1e3
DIL_PAIRS = ((128, 1), (512, 4), (2048, 16))
DIL_WINDOW_MAX = 2048
TOP_K = 2
QPAD = SUBLANES
NEW_PAD = LANES
PAGES_PER_STEP = 16

VMEM_LIMIT = 56 * 1024 * 1024


def _tile(n, pref):
    return pref if n % pref == 0 else n


def _cparams(sem):
    return pltpu.CompilerParams(dimension_semantics=sem, vmem_limit_bytes=VMEM_LIMIT)


def _mm(a, b):
    return jnp.dot(a.astype(BF16), b.astype(BF16), preferred_element_type=F32)


def _mm_nt(a, b):
    return lax.dot_general(a.astype(BF16), b.astype(BF16), (((1,), (1,)), ((), ())),
                           preferred_element_type=F32)


def _split3(x):
    hi = x.astype(BF16)
    r = x - hi.astype(F32)
    mid = r.astype(BF16)
    lo = (r - mid.astype(F32)).astype(BF16)
    return hi, mid, lo


def _mm3_left(x, exact_rhs):
    b = exact_rhs.astype(BF16)
    hi, mid, lo = _split3(x)
    d = lambda p: jnp.dot(p, b, preferred_element_type=F32)
    return d(hi) + d(mid) + d(lo)


def _sigmoid(z):
    return 1.0 / (1.0 + jnp.exp(-z))


def _silu(z):
    return z * _sigmoid(z)


def _log_sigmoid(z):
    return jnp.minimum(z, 0.0) - jnp.log1p(jnp.exp(-jnp.abs(z)))


def _rmsnorm(x, g):
    return x * lax.rsqrt(jnp.mean(x * x, axis=-1, keepdims=True) + RMS_EPS) * g


def _rope_rows(y, cos, sin_signed):
    n = y.shape[1]
    lane = lax.broadcasted_iota(jnp.int32, y.shape, 1)
    first = (lane % HEAD_DIM) < HALF
    rot = jnp.where(first, pltpu.roll(y, n - HALF, 1), pltpu.roll(y, HALF, 1))
    reps = n // LANES
    if reps > 1:
        cos = jnp.concatenate([cos] * reps, axis=1)
        sin_signed = jnp.concatenate([sin_signed] * reps, axis=1)
    return y * cos + rot * sin_signed


def _rope_cols(yt, cos_t, sin_t):
    out = []
    for h in range(yt.shape[0] // HEAD_DIM):
        a = yt[h * HEAD_DIM:h * HEAD_DIM + HALF]
        b = yt[h * HEAD_DIM + HALF:(h + 1) * HEAD_DIM]
        out += [a * cos_t - b * sin_t, b * cos_t + a * sin_t]
    return jnp.concatenate(out, axis=0)


def _rope_tables(pos):
    inv = jnp.exp(-math.log(ROPE_THETA) * jnp.arange(HALF, dtype=F32) / HALF)
    ang = pos.astype(F32)[:, None] * inv[None, :]
    cos, sin = jnp.cos(ang), jnp.sin(ang)
    return (jnp.concatenate([cos, cos, cos, cos], axis=1),
            jnp.concatenate([-sin, sin, -sin, sin], axis=1), cos.T, sin.T)


def _softmax_rows(s, mask):
    sm = jnp.where(mask, s, NEG)
    m = jnp.max(sm, axis=1, keepdims=True)
    p = jnp.where(mask, jnp.exp(sm - m), 0.0)
    l = jnp.sum(p, axis=1, keepdims=True)
    return p / jnp.where(l > 0.0, l, 1.0)


def _online_step(s, m_ref, l_ref):
    m_prev = m_ref[...]
    m_new = jnp.maximum(m_prev, jnp.max(s, axis=1, keepdims=True))
    alpha = jnp.exp(m_prev - m_new)
    p = jnp.exp(s - m_new)
    l_ref[...] = alpha * l_ref[...] + jnp.sum(p, axis=1, keepdims=True)
    m_ref[...] = m_new
    return alpha, p


def _online_init(m_ref, l_ref, acc_ref):
    m_ref[...] = jnp.full(m_ref.shape, NEG, F32)
    l_ref[...] = jnp.zeros_like(l_ref)
    acc_ref[...] = jnp.zeros_like(acc_ref)


def _proj_kernel(*refs, row_segs, col_segs, n_gate, with_logf, col_from):
    it = iter(refs)
    x_ref, g_ref, cos_ref, sin_ref, cost_ref, sint_ref = [next(it) for _ in range(6)]
    w_ref = next(it) if row_segs else None
    wt_ref = next(it) if col_segs else None
    wg_ref = next(it) if n_gate else None
    if with_logf:
        wft_ref, bcol_ref, wfr_ref, brow_ref = next(it), next(it), next(it), next(it)
    outs = list(it)
    xn = _rmsnorm(x_ref[...], g_ref[...]).astype(BF16)
    k = 0
    for c0, width, rope, emits in row_segs:
        y = jnp.dot(xn, w_ref[:, c0:c0 + width], preferred_element_type=F32)
        if rope:
            y = _rope_rows(y, cos_ref[...], sin_ref[...])
        for dtype, scale in emits:
            outs[k][...] = (y if scale == 1.0 else y * scale).astype(dtype)
            k += 1

    def cols():
        kk = k
        for r0, height, rope, emits in col_segs:
            yt = _mm_nt(wt_ref[r0:r0 + height, :], xn)
            if rope == "sigmoid":
                yt = _sigmoid(yt)
            elif rope:
                yt = _rope_cols(yt, cost_ref[...], sint_ref[...])
            for dtype, scale in emits:
                outs[kk][0] = (yt if scale == 1.0 else yt * scale).astype(dtype)
                kk += 1

    if col_segs:
        if col_from:
            pl.when(pl.program_id(0) % col_from[0] >= col_from[1])(cols)
        else:
            cols()
        k += sum(len(e) for _, _, _, e in col_segs)
    if n_gate:
        outs[k][...] = _sigmoid(jnp.dot(xn, wg_ref[...], preferred_element_type=F32))
        k += 1
    if with_logf:
        yt = _mm_nt(wft_ref[...], xn)
        outs[k][0] = _log_sigmoid(yt[0:FOX_HEADS] + bcol_ref[...])
        outs[k + 1][...] = _log_sigmoid(jnp.dot(xn, wfr_ref[...], preferred_element_type=F32) + brow_ref[...])


def _project(x2d, gain, tables, n_pos_blocks, tm, w=None, row_segs=(), wt=None, col_segs=(),
             w_gate=None, logf=None, col_from=None, name="proj"):
    M, D = x2d.shape
    nt = M // tm
    n_seq = nt // n_pos_blocks
    cos_t, sin_t, cos_c, sin_c = tables
    pos_map = lambda i: (i % n_pos_blocks, 0)
    posc_map = lambda i: (0, i % n_pos_blocks)
    const = lambda a: pl.BlockSpec(a.shape, lambda i: (0,) * a.ndim)
    in_specs = [pl.BlockSpec((tm, D), lambda i: (i, 0)), const(gain.reshape(1, D)),
                pl.BlockSpec((tm, LANES), pos_map), pl.BlockSpec((tm, LANES), pos_map),
                pl.BlockSpec((HALF, tm), posc_map), pl.BlockSpec((HALF, tm), posc_map)]
    args = [x2d, gain.reshape(1, D), cos_t, sin_t, cos_c, sin_c]
    for a in (w, wt, w_gate):
        if a is not None:
            in_specs.append(const(a))
            args.append(a)
    out_shape, out_specs = [], []
    for _, wd, _, emits in row_segs:
        for dtype, _ in emits:
            out_shape.append(jax.ShapeDtypeStruct((M, wd), dtype))
            out_specs.append(pl.BlockSpec((tm, wd), lambda i: (i, 0)))
    first = col_from[1] if col_from else 0
    n_cb = n_pos_blocks - first
    col_map = lambda i: (i // n_pos_blocks, 0, jnp.maximum(i % n_pos_blocks - first, 0))
    for _, ht, _, emits in col_segs:
        for dtype, _ in emits:
            out_shape.append(jax.ShapeDtypeStruct((n_seq, ht, n_cb * tm), dtype))
            out_specs.append(pl.BlockSpec((1, ht, tm), col_map))
    n_gate = 0
    if w_gate is not None:
        n_gate = w_gate.shape[1]
        out_shape.append(jax.ShapeDtypeStruct((M, n_gate), F32))
        out_specs.append(pl.BlockSpec((tm, n_gate), lambda i: (i, 0)))
    if logf is not None:
        in_specs += [const(a) for a in logf]
        args += list(logf)
        out_shape.append(jax.ShapeDtypeStruct((n_seq, FOX_HEADS, n_pos_blocks * tm), F32))
        out_specs.append(pl.BlockSpec((1, FOX_HEADS, tm), lambda i: (i // n_pos_blocks, 0, i % n_pos_blocks)))
        out_shape.append(jax.ShapeDtypeStruct((M, LANES), F32))
        out_specs.append(pl.BlockSpec((tm, LANES), lambda i: (i, 0)))
    return pl.pallas_call(
        functools.partial(_proj_kernel, row_segs=tuple(row_segs), col_segs=tuple(col_segs), n_gate=n_gate,
                          with_logf=logf is not None, col_from=col_from),
        grid=(nt,), in_specs=in_specs, out_specs=out_specs, out_shape=out_shape, name=name,
        compiler_params=_cparams(("arbitrary",)))(*args)


def _fox_prep_kernel(k_ref, lf_ref, kaug_ref, c0_ref, carry_ref):
    j = pl.program_id(1)
    tc = k_ref.shape[1]

    @pl.when(j == 0)
    def _():
        carry_ref[...] = jnp.zeros_like(carry_ref)
        c0_ref[...] = jnp.zeros_like(c0_ref)

    lane = lax.broadcasted_iota(jnp.int32, (FOX_HEADS, LANES), 1)
    start = jnp.transpose(carry_ref[...])[0:FOX_HEADS, 0:1]
    c0_ref[0] = jnp.where(lane == j, start, c0_ref[0])

    r = lax.broadcasted_iota(jnp.int32, (tc, tc), 0)
    c = lax.broadcasted_iota(jnp.int32, (tc, tc), 1)
    tri = (c <= r).astype(BF16)
    hi, mid, lo = _split3(lf_ref[0])
    d = lambda p: jnp.dot(tri, p, preferred_element_type=F32)
    local = d(hi) + d(mid) + d(lo)
    carry_ref[...] = carry_ref[...] + jnp.broadcast_to(local[tc - 1:tc, :], carry_ref.shape)
    parts = jnp.concatenate(_split3(-local), axis=1)
    k = k_ref[0]
    kr = lax.broadcasted_iota(jnp.int32, (k.shape[1], LANES), 0)
    kc = lax.broadcasted_iota(jnp.int32, (k.shape[1], LANES), 1)
    pr = lax.broadcasted_iota(jnp.int32, (3 * LANES, LANES), 0)
    pc = lax.broadcasted_iota(jnp.int32, (3 * LANES, LANES), 1)
    for h in range(FOX_HEADS):
        place_k = ((kr == HEAD_DIM * h + kc) & (kc < HEAD_DIM)).astype(BF16)
        place_c = ((pr % LANES == h) & (pc == HEAD_DIM + pr // LANES)).astype(BF16)
        kaug_ref[0, h] = (jnp.dot(k, place_k, preferred_element_type=F32)
                          + jnp.dot(parts, place_c, preferred_element_type=F32)).astype(BF16)


def _fox_prep(k_b, lf_rows, tc):
    B, S, W = k_b.shape
    assert S // tc <= LANES
    return pl.pallas_call(
        _fox_prep_kernel, grid=(B, S // tc),
        in_specs=[pl.BlockSpec((1, tc, W), lambda b, j: (b, j, 0)),
                  pl.BlockSpec((1, tc, LANES), lambda b, j: (b, j, 0))],
        out_specs=[pl.BlockSpec((1, FOX_HEADS, tc, LANES), lambda b, j: (b, 0, j, 0)),
                   pl.BlockSpec((1, FOX_HEADS, LANES), lambda b, j: (b, 0, 0))],
        out_shape=[jax.ShapeDtypeStruct((B, FOX_HEADS, S, LANES), BF16),
                   jax.ShapeDtypeStruct((B, FOX_HEADS, LANES), F32)], name="fox_prep",
        scratch_shapes=[pltpu.VMEM((SUBLANES, LANES), F32)],
        compiler_params=_cparams(("parallel", "arbitrary")))(k_b, lf_rows)


def _fox_kernel(qi_ref, ki_ref, qt_ref, kaug_ref, vt_ref, c0_ref, o_ref, qa_ref, m_ref, l_ref, acc_ref, *, nh):
    hg, step = pl.program_id(1), pl.program_id(2)
    qi, ki = qi_ref[step], ki_ref[step]
    tq, tk = qt_ref.shape[2], kaug_ref.shape[2]
    ratio = tq // tk
    lane1 = lax.broadcasted_iota(jnp.int32, (1, LANES), 1)

    @pl.when(ki == 0)
    def _():
        row = lax.broadcasted_iota(jnp.int32, (HEAD_DIM, tq), 0)
        ones = jnp.where(row < 3, 1.0, 0.0).astype(BF16)
        for h in range(nh):
            qa_ref[h] = jnp.concatenate([qt_ref[0, h * HEAD_DIM:(h + 1) * HEAD_DIM, :], ones], axis=0)
        _online_init(m_ref, l_ref, acc_ref)

    def tile(masked):
        if masked:
            live = (ki * tk + lax.broadcasted_iota(jnp.int32, (tk, tq), 0)
                    <= qi * tq + lax.broadcasted_iota(jnp.int32, (tk, tq), 1))
        for h in range(nh):
            c0 = c0_ref[0, pl.ds(nh * hg + h, 1), :]
            delta = jnp.sum(jnp.where(lane1 == ratio * qi, c0, 0.0) - jnp.where(lane1 == ki, c0, 0.0),
                            axis=1, keepdims=True)
            s = jnp.dot(kaug_ref[0, h], qa_ref[h], preferred_element_type=F32)
            if masked:
                s = jnp.where(live, s, NEG)
            m_prev = m_ref[h]
            m_new = jnp.maximum(m_prev, jnp.max(s, axis=0, keepdims=True) + delta)
            p = jnp.exp(s - (m_new - delta))
            alpha = jnp.exp(m_prev - m_new)
            l_ref[h] = alpha * l_ref[h] + jnp.sum(p, axis=0, keepdims=True)
            acc_ref[h] = alpha * acc_ref[h] + jnp.dot(vt_ref[0, h * HEAD_DIM:(h + 1) * HEAD_DIM, :],
                                                      p.astype(BF16), preferred_element_type=F32)
            m_ref[h] = m_new

    pl.when(ki < ratio * qi)(lambda: tile(False))
    pl.when(ki >= ratio * qi)(lambda: tile(True))

    @pl.when(ki == ratio * (qi + 1) - 1)
    def _():
        for h in range(nh):
            o_ref[0, h * HEAD_DIM:(h + 1) * HEAD_DIM, :] = (acc_ref[h] / l_ref[h]).astype(BF16)


def _fox_prompt(qt_b, kaug, vt_b, c0, tk, nh=4):
    B, W, S = qt_b.shape
    tq = _tile(S, 2 * tk)
    ratio = tq // tk
    pairs = [(q, k) for q in range(S // tq) for k in range(ratio * (q + 1))]
    qi_tab = jnp.asarray([p[0] for p in pairs], jnp.int32)
    ki_tab = jnp.asarray([p[1] for p in pairs], jnp.int32)
    hw = nh * HEAD_DIM
    grid_spec = pltpu.PrefetchScalarGridSpec(
        num_scalar_prefetch=2, grid=(B, W // hw, len(pairs)),
        in_specs=[pl.BlockSpec((1, hw, tq), lambda b, hg, s, qi, ki: (b, hg, qi[s])),
                  pl.BlockSpec((1, nh, tk, LANES), lambda b, hg, s, qi, ki: (b, hg, ki[s], 0)),
                  pl.BlockSpec((1, hw, tk), lambda b, hg, s, qi, ki: (b, hg, ki[s])),
                  pl.BlockSpec((1, FOX_HEADS, LANES), lambda b, hg, s, qi, ki: (b, 0, 0))],
        out_specs=pl.BlockSpec((1, hw, tq), lambda b, hg, s, qi, ki: (b, hg, qi[s])),
        scratch_shapes=[pltpu.VMEM((nh, LANES, tq), BF16), pltpu.VMEM((nh, 1, tq), F32),
                        pltpu.VMEM((nh, 1, tq), F32), pltpu.VMEM((nh, HEAD_DIM, tq), F32)])
    return pl.pallas_call(
        functools.partial(_fox_kernel, nh=nh), grid_spec=grid_spec,
        out_shape=jax.ShapeDtypeStruct(qt_b.shape, BF16), name="fox_prompt",
        compiler_params=_cparams(("parallel", "parallel", "arbitrary")))(qi_tab, ki_tab, qt_b, kaug, vt_b, c0)


def _cmp_compute(x, pea_ref, peb_ref, wa_ref, wb_ref, w2t_ref, o_ref, carry_ref):
    n = x.shape[0]
    a = jnp.dot((x + pea_ref[...]).astype(BF16), wa_ref[...], preferred_element_type=F32)
    b = jnp.dot((x + peb_ref[...]).astype(BF16), wb_ref[...], preferred_element_type=F32)
    rowi = lax.broadcasted_iota(jnp.int32, a.shape, 0)
    a_prev = jnp.where(rowi == 0, carry_ref[0:1, :], pltpu.roll(a, 1, 0))
    carry_ref[...] = jnp.broadcast_to(a[n - 1:n, :], carry_ref.shape)
    o_ref[0] = _mm_nt(w2t_ref[...], _silu(a_prev + b))


def _cmp_kernel(x_ref, pea_ref, peb_ref, wa_ref, wb_ref, w2t_ref, o_ref, carry_ref):
    @pl.when(pl.program_id(1) == 0)
    def _():
        carry_ref[...] = jnp.zeros_like(carry_ref)
    _cmp_compute(x_ref[0], pea_ref, peb_ref, wa_ref, wb_ref, w2t_ref, o_ref, carry_ref)


def _cmp_paged_kernel(pt_ref, *refs, n_pg):
    pages = refs[:n_pg]
    pea_ref, peb_ref, wa_ref, wb_ref, w2t_ref, o_ref, xs_ref, carry_ref = refs[n_pg:]
    page = pages[0].shape[3]

    @pl.when(pl.program_id(1) == 0)
    def _():
        carry_ref[...] = jnp.zeros_like(carry_ref)

    for j, r in enumerate(pages):
        xs_ref[j * page:(j + 1) * page, :] = jnp.transpose(r[0].reshape(LANES, page))
    n = n_pg * page // NSA_CMP_STRIDE
    a = jnp.zeros((n, wa_ref.shape[2]), F32)
    b = jnp.zeros((n, wb_ref.shape[2]), F32)
    for l in range(NSA_CMP_STRIDE):
        xl = xs_ref[pl.ds(l, n, stride=NSA_CMP_STRIDE), :]
        a = a + jnp.dot((xl + pea_ref[l:l + 1, :]).astype(BF16), wa_ref[l], preferred_element_type=F32)
        b = b + jnp.dot((xl + peb_ref[l:l + 1, :]).astype(BF16), wb_ref[l], preferred_element_type=F32)
    rowi = lax.broadcasted_iota(jnp.int32, a.shape, 0)
    a_prev = jnp.where(rowi == 0, carry_ref[0:1, :], pltpu.roll(a, 1, 0))
    carry_ref[...] = jnp.broadcast_to(a[n - 1:n, :], carry_ref.shape)
    o_ref[0] = _mm_nt(w2t_ref[...], _silu(a_prev + b))


def _cmp_weights(pe, w1, w2):
    eye = jnp.eye(NSA_KV_HEADS, dtype=F32)
    hid = w1.shape[2]
    half = NSA_CMP_STRIDE

    def wpart(w):
        return jnp.einsum('lde,hg->lhdge', w, eye).reshape(half * LANES, NSA_KV_HEADS * hid).astype(BF16)

    def ppart(p):
        return jnp.broadcast_to(p[:, None, :], (half, NSA_KV_HEADS, HEAD_DIM)).reshape(1, half * LANES)

    w2t = jnp.einsum('ed,hg->gdhe', w2, eye).reshape(LANES, NSA_KV_HEADS * hid).astype(BF16)
    return ppart(pe[:half]), ppart(pe[half:]), wpart(w1[:half]), wpart(w1[half:]), w2t


def _compress(x, weights, name):
    N, L, _ = x.shape
    n_ch = L // NSA_CMP_STRIDE
    xc = x[:, :n_ch * NSA_CMP_STRIDE].reshape(N, n_ch, NSA_CMP_STRIDE * LANES)
    tch = _tile(n_ch, 256)
    wspecs = [pl.BlockSpec(w.shape, lambda n, j: (0, 0)) for w in weights]
    return pl.pallas_call(
        _cmp_kernel, grid=(N, n_ch // tch),
        in_specs=[pl.BlockSpec((1, tch, xc.shape[2]), lambda n, j: (n, j, 0))] + wspecs,
        out_specs=pl.BlockSpec((1, LANES, tch), lambda n, j: (n, 0, j)),
        out_shape=jax.ShapeDtypeStruct((N, LANES, n_ch), F32), name=name,
        scratch_shapes=[pltpu.VMEM((8, weights[2].shape[1]), F32)],
        compiler_params=_cparams(("parallel", "arbitrary")))(xc, *weights)


def _compress_paged(cache, pt_flat, n_db, weights, name):
    n_pool, page = cache.shape[:2]
    assert page == LANES
    rows = page // NSA_CMP_STRIDE
    ct = jnp.transpose(cache, (0, 2, 3, 1))
    n_pages = pt_flat.shape[0] // n_db
    n_pg = math.gcd(n_pages, max(1, 256 // rows))
    pea, peb, wa, wb, w2t = weights
    per_pos = lambda a: a.reshape(NSA_CMP_STRIDE, LANES, -1)
    weights = (pea.reshape(NSA_CMP_STRIDE, LANES), peb.reshape(NSA_CMP_STRIDE, LANES), per_pos(wa), per_pos(wb), w2t)
    wspecs = [pl.BlockSpec(w.shape, lambda b, p, pt, nd=w.ndim: (0,) * nd) for w in weights]
    page_spec = lambda j: pl.BlockSpec((1,) + ct.shape[1:],
                                       lambda b, p, pt: (pt[b * n_pages + p * n_pg + j], 0, 0, 0))
    grid_spec = pltpu.PrefetchScalarGridSpec(
        num_scalar_prefetch=1, grid=(n_db, n_pages // n_pg),
        in_specs=[page_spec(j) for j in range(n_pg)] + wspecs,
        out_specs=pl.BlockSpec((1, LANES, n_pg * rows), lambda b, p, pt: (b, 0, p)),
        scratch_shapes=[pltpu.VMEM((n_pg * page, LANES), F32), pltpu.VMEM((8, wa.shape[1]), F32)])
    return pl.pallas_call(
        functools.partial(_cmp_paged_kernel, n_pg=n_pg), grid_spec=grid_spec,
        out_shape=jax.ShapeDtypeStruct((n_db, LANES, n_pages * rows), F32), name=name,
        compiler_params=_cparams(("parallel", "arbitrary")))(pt_flat, *([ct] * n_pg), *weights)


def _slc_map_rows(n_ch, ns_pad):
    i = (np.arange(n_ch)[:, None] - 1) * NSA_CMP_STRIDE
    j = np.arange(ns_pad)[None, :] * NSA_SLC_BLOCK
    shared = np.minimum(i + NSA_CMP_LEN, j + NSA_SLC_BLOCK) - np.maximum(i, j)
    m = np.clip(shared, 0, None) / NSA_CMP_LEN
    m[0, :] = 0.0
    return m.astype(np.float32)


def _select_blocks(imp, qpos, ns, n_sel):
    blk = lax.broadcasted_iota(jnp.int32, imp.shape, 1)
    cur = qpos // NSA_SLC_BLOCK
    valid = blk * NSA_SLC_BLOCK <= qpos
    forced = (blk == 0) | (blk == cur) | (blk == cur - 1)
    score = jnp.where(valid, imp + jnp.where(forced, NSA_FORCE_BONUS, 0.0), NEG)
    rank = jnp.zeros(imp.shape, jnp.int32)
    for jp in range(ns):
        sj = score[:, jp:jp + 1]
        beats = (sj > score) | ((sj == score) & (blk > jp))
        rank = rank + beats.astype(jnp.int32)
    return rank < n_sel


def _select_blocks_cols(imp, qpos, ns, n_sel):
    blk = lax.broadcasted_iota(jnp.int32, imp.shape, 0)
    cur = qpos // NSA_SLC_BLOCK
    valid = blk * NSA_SLC_BLOCK <= qpos
    forced = (blk == 0) | (blk == cur) | (blk == cur - 1)
    score = jnp.where(valid, imp + jnp.where(forced, NSA_FORCE_BONUS, 0.0), NEG)
    rank = jnp.zeros(imp.shape, jnp.int32)
    for jp in range(ns):
        sj = score[jp:jp + 1, :]
        beats = (sj > score) | ((sj == score) & (blk > jp))
        rank = rank + beats.astype(jnp.int32)
    return rank < n_sel


def _softmax_cols(s, mask):
    sm = jnp.where(mask, s, NEG)
    m = jnp.max(sm, axis=0, keepdims=True)
    p = jnp.where(mask, jnp.exp(sm - m), 0.0)
    l = jnp.sum(p, axis=0, keepdims=True)
    return p / jnp.where(l > 0.0, l, 1.0)


def _nsa_kernel(qt_ref, kct_ref, vct_ref, ks_ref, vst_ref, kw_ref, vwt_ref, gt_ref, mapt_ref, o_ref,
                qaug_ref, negm_ref, m_ref, l_ref, acc_ref, *, ns, n_sel, tk, win):
    g, i = pl.program_id(1), pl.program_id(2)
    tq = qt_ref.shape[2]
    nl = NSA_GROUP * tq
    n_ch = kct_ref.shape[3]
    per_tile = tk // NSA_SLC_BLOCK
    st = i * tq
    qpos1 = st + lax.broadcasted_iota(jnp.int32, (1, tq), 1)
    rep = lambda a: jnp.concatenate([a] * NSA_GROUP, axis=1)
    qpos = rep(qpos1)

    q4t = jnp.concatenate([qt_ref[0, j * HEAD_DIM:(j + 1) * HEAD_DIM, :] for j in range(NSA_GROUP)], axis=1)
    row = lax.broadcasted_iota(jnp.int32, (LANES, nl), 0)
    qaug_ref[0:LANES, :] = jnp.where(row // HEAD_DIM == g, jnp.concatenate([q4t, q4t], axis=0),
                                     jnp.zeros((LANES, nl), BF16))
    qaug_ref[LANES:2 * LANES, :] = jnp.zeros((LANES, nl), BF16)

    cidx = lax.broadcasted_iota(jnp.int32, (n_ch, tq), 0)
    cmask = rep(((cidx - 1) * NSA_CMP_STRIDE + NSA_CMP_LEN - 1 <= qpos1) & (cidx >= 1))
    sc = lax.dot_general(kct_ref[0, 0].astype(BF16), q4t, (((0,), (0,)), ((), ())), preferred_element_type=F32)
    pc = _softmax_cols(sc, cmask)
    oc = _mm(vct_ref[0, 0], pc)
    pstack = jnp.concatenate([pc[:, j * tq:(j + 1) * tq] for j in range(NSA_GROUP)], axis=0)
    imp = jnp.dot(mapt_ref[...], pstack.astype(BF16), preferred_element_type=F32)[0:HEAD_DIM]
    sel = _select_blocks_cols(imp, qpos1, ns, n_sel)

    negm_ref[...] = rep(jnp.where(sel, 0.0, -MASK_BIG))
    _online_init(m_ref, l_ref, acc_ref)
    onehot = ((lax.broadcasted_iota(jnp.int32, (tk, LANES), 0) // NSA_SLC_BLOCK)
              == lax.broadcasted_iota(jnp.int32, (tk, LANES), 1)).astype(BF16)

    def tile(kt, diagonal):
        k0 = pl.multiple_of(kt * tk, tk)
        nm = negm_ref[pl.ds(pl.multiple_of(kt * per_tile, per_tile), per_tile), :]
        qaug_ref[LANES:LANES + 16, :] = jnp.concatenate(
            [nm, jnp.zeros((16 - per_tile, nl), F32)], axis=0).astype(BF16)
        kaug = jnp.concatenate([ks_ref[0, pl.ds(k0, tk), :], onehot], axis=1)
        s = jnp.dot(kaug, qaug_ref[...], preferred_element_type=F32)
        if diagonal:
            kpos = k0 + lax.broadcasted_iota(jnp.int32, (tk, nl), 0)
            s = jnp.where(kpos <= qpos, s, NEG)
        m_prev = m_ref[...]
        m_new = jnp.maximum(m_prev, jnp.max(s, axis=0, keepdims=True))
        alpha = jnp.exp(m_prev - m_new)
        p = jnp.exp(s - m_new)
        l_ref[...] = alpha * l_ref[...] + jnp.sum(p, axis=0, keepdims=True)
        acc_ref[...] = alpha * acc_ref[...] + jnp.dot(vst_ref[0, :, pl.ds(k0, tk)], p.astype(BF16),
                                                      preferred_element_type=F32)
        m_ref[...] = m_new

    last = st // tk

    def body(kt, carry):
        tile(kt, False)
        return carry

    lax.fori_loop(0, last, body, 0)
    tile(last, True)
    osl = acc_ref[...] / l_ref[...]

    w0 = pl.multiple_of(jnp.maximum(st + tq - win, 0), tq)
    dist = qpos1 - (w0 + lax.broadcasted_iota(jnp.int32, (win, tq), 0))
    sw = jnp.dot(kw_ref[0, pl.ds(w0, win), :], qaug_ref[0:LANES, :], preferred_element_type=F32)
    pw = _softmax_cols(sw, rep((dist >= 0) & (dist <= NSA_WINDOW)))
    ow = jnp.dot(vwt_ref[0, :, pl.ds(w0, win)], pw.astype(BF16), preferred_element_type=F32)

    for j in range(NSA_GROUP):
        base = (NSA_GROUP * g + j) * 3
        gate = lambda r: gt_ref[0, pl.ds(base + r, 1), :]
        cols = slice(j * tq, (j + 1) * tq)
        o = gate(0) * oc[:, cols] + gate(1) * osl[:, cols] + gate(2) * ow[:, cols]
        o_ref[0, j * HEAD_DIM:(j + 1) * HEAD_DIM, :] = o.astype(BF16)


def _nsa_prompt(qt_b, kct, vct, ks_b, vst_b, kw_b, vwt_b, gates_t):
    B, W, S = qt_b.shape
    n_ch = kct.shape[3]
    ns = -(-S // NSA_SLC_BLOCK)
    assert ns <= HEAD_DIM
    n_sel = min(NSA_TOPN, ns)
    tq = _tile(S, 256)
    tk = _tile(S, 512)
    assert tk // NSA_SLC_BLOCK <= 16
    win = min(NSA_WINDOW + tq, S)
    nl = NSA_GROUP * tq
    smap_t = jnp.asarray(np.tile(_slc_map_rows(n_ch, LANES).T, (1, NSA_GROUP)), dtype=BF16)
    per_head = lambda a: pl.BlockSpec((1, 1) + a.shape[2:], lambda b, g, i: (b, g, 0, 0))
    gw = W // NSA_KV_HEADS
    return pl.pallas_call(
        functools.partial(_nsa_kernel, ns=ns, n_sel=n_sel, tk=tk, win=win),
        grid=(B, NSA_KV_HEADS, S // tq),
        in_specs=[pl.BlockSpec((1, gw, tq), lambda b, g, i: (b, g, i)),
                  per_head(kct), per_head(vct),
                  pl.BlockSpec((1, S, LANES), lambda b, g, i: (b, 0, 0)),
                  pl.BlockSpec((1, HEAD_DIM, S), lambda b, g, i: (b, g, 0)),
                  pl.BlockSpec((1, S, LANES), lambda b, g, i: (b, 0, 0)),
                  pl.BlockSpec((1, HEAD_DIM, S), lambda b, g, i: (b, g, 0)),
                  pl.BlockSpec((1, gates_t.shape[1], tq), lambda b, g, i: (b, 0, i)),
                  pl.BlockSpec(smap_t.shape, lambda b, g, i: (0, 0))],
        out_specs=pl.BlockSpec((1, gw, tq), lambda b, g, i: (b, g, i)),
        out_shape=jax.ShapeDtypeStruct(qt_b.shape, BF16), name="nsa_prompt",
        scratch_shapes=[pltpu.VMEM((2 * LANES, nl), BF16), pltpu.VMEM((HEAD_DIM, nl), F32),
                        pltpu.VMEM((1, nl), F32), pltpu.VMEM((1, nl), F32), pltpu.VMEM((HEAD_DIM, nl), F32)],
        compiler_params=_cparams(("parallel", "parallel", "arbitrary")))(
            qt_b, kct, vct, ks_b, vst_b, kw_b, vwt_b, gates_t, smap_t)


def _outproj_kernel(*refs):
    x_ref, w_ref, y_ref = refs[0], refs[-2], refs[-1]
    y = x_ref[...]
    k0 = 0
    for o_ref in refs[1:-2]:
        if len(o_ref.shape) == 3:
            kw = o_ref.shape[1]
            y = y + lax.dot_general(o_ref[0].astype(BF16), w_ref[k0:k0 + kw, :], (((0,), (0,)), ((), ())),
                                    preferred_element_type=F32)
        else:
            kw = o_ref.shape[1]
            y = y + jnp.dot(o_ref[...].astype(BF16), w_ref[k0:k0 + kw, :], preferred_element_type=F32)
        k0 += kw
    y_ref[...] = y


def _outproj(x2d, parts, w_bf, name, tm=None):
    M, D = x2d.shape
    tm = tm or _tile(M, 512)
    specs = []
    for o in parts:
        if o.ndim == 3:
            npb = o.shape[2] // tm
            specs.append(pl.BlockSpec((1, o.shape[1], tm), lambda i, npb=npb: (i // npb, 0, i % npb)))
        else:
            specs.append(pl.BlockSpec((tm, o.shape[1]), lambda i: (i, 0)))
    return pl.pallas_call(
        _outproj_kernel, grid=(M // tm,),
        in_specs=[pl.BlockSpec((tm, D), lambda i: (i, 0))] + specs + [pl.BlockSpec(w_bf.shape, lambda i: (0, 0))],
        out_specs=pl.BlockSpec((tm, D), lambda i: (i, 0)),
        out_shape=jax.ShapeDtypeStruct((M, D), F32), name=name,
        compiler_params=_cparams(("parallel",)))(x2d, *parts, w_bf)


def _ffn_kernel(x_ref, g_ref, wg_ref, wu_ref, wd_ref, y_ref, xn_ref):
    f = pl.program_id(1)

    @pl.when(f == 0)
    def _():
        x = x_ref[...]
        xn_ref[...] = _rmsnorm(x, g_ref[...]).astype(BF16)
        y_ref[...] = x

    xn = xn_ref[...]
    h = _silu(jnp.dot(xn, wg_ref[...], preferred_element_type=F32)) * \
        jnp.dot(xn, wu_ref[...], preferred_element_type=F32)
    y_ref[...] += jnp.dot(h.astype(BF16), wd_ref[...], preferred_element_type=F32)


def _ffn(x2d, gain, wg, wu, wd, name):
    M, D = x2d.shape
    Fd = wg.shape[1]
    tm = _tile(M, 512)
    nf = 2 if Fd % (2 * LANES) == 0 else 1
    fc = Fd // nf
    return pl.pallas_call(
        _ffn_kernel, grid=(M // tm, nf),
        in_specs=[pl.BlockSpec((tm, D), lambda i, f: (i, 0)), pl.BlockSpec((1, D), lambda i, f: (0, 0)),
                  pl.BlockSpec((D, fc), lambda i, f: (0, f)), pl.BlockSpec((D, fc), lambda i, f: (0, f)),
                  pl.BlockSpec((fc, D), lambda i, f: (f, 0))],
        out_specs=pl.BlockSpec((tm, D), lambda i, f: (i, 0)),
        out_shape=jax.ShapeDtypeStruct((M, D), F32), name=name,
        scratch_shapes=[pltpu.VMEM((tm, D), BF16)],
        compiler_params=_cparams(("parallel", "arbitrary")))(x2d, gain.reshape(1, D), wg, wu, wd)


def _moe_kernel(x_ref, g_ref, wrt_ref, wg_ref, wu_ref, wd_ref, gf_ref, y_ref, xn_ref, rank_ref, comb_ref,
                rankc_ref, *, n_exp, chunk):
    e = pl.program_id(1)
    tm = x_ref.shape[0]

    @pl.when(e == 0)
    def _():
        xn = _rmsnorm(x_ref[...], g_ref[...])
        xn_ref[...] = xn.astype(BF16)
        xh, xm, _ = _split3(xn)
        wh, wm, _ = _split3(wrt_ref[...])
        dn = lambda a, b: lax.dot_general(a, b, (((1,), (1,)), ((), ())), preferred_element_type=F32)
        row = lax.broadcasted_iota(jnp.int32, (LANES, tm), 0)
        logits = jnp.where(row < n_exp, dn(wh, xh) + dn(wh, xm) + dn(wm, xh), NEG)
        v1 = jnp.max(logits, axis=0, keepdims=True)
        i1 = jnp.min(jnp.where(logits == v1, row, LANES), axis=0, keepdims=True)
        rest = jnp.where(row == i1, NEG, logits)
        v2 = jnp.max(rest, axis=0, keepdims=True)
        i2 = jnp.min(jnp.where(rest == v2, row, LANES), axis=0, keepdims=True)
        ex = jnp.exp(v2 - v1)
        comb = jnp.where(row == i1, 1.0 / (1.0 + ex), jnp.where(row == i2, ex / (1.0 + ex), 0.0))
        member = ((row == i1) | (row == i2))[0:SUBLANES]
        before = (lax.broadcasted_iota(jnp.int32, (tm, tm), 0)
                  < lax.broadcasted_iota(jnp.int32, (tm, tm), 1)).astype(BF16)
        rank = jnp.dot(jnp.where(member, 1.0, 0.0).astype(BF16), before, preferred_element_type=F32)
        rank = jnp.where(member, rank, -1.0)
        rank_ref[...] = rank
        comb_ref[...] = comb[0:SUBLANES]
        rankc_ref[...] = jnp.transpose(rank)
        y_ref[...] = jnp.zeros_like(y_ref)

    rrow = rank_ref[pl.ds(e, 1), :]
    crow = comb_ref[pl.ds(e, 1), :]
    rc = rankc_ref[...]
    rcol = jnp.sum(jnp.where(lax.broadcasted_iota(jnp.int32, rc.shape, 1) == e, rc, 0.0), axis=1, keepdims=True)
    n_tok = jnp.max(rrow).astype(jnp.int32) + 1

    def body(c, carry):
        base = (c * chunk).astype(F32)
        pick = rrow == lax.broadcasted_iota(jnp.int32, (chunk, tm), 0).astype(F32) + base
        xg = jnp.dot(jnp.where(pick, 1.0, 0.0).astype(BF16), xn_ref[...],
                     preferred_element_type=F32).astype(BF16)
        wcol = jnp.sum(jnp.where(pick, crow, 0.0), axis=1, keepdims=True)
        h = _silu(jnp.dot(xg, wg_ref[0], preferred_element_type=F32)) * \
            jnp.dot(xg, wu_ref[0], preferred_element_type=F32)
        yv = jnp.dot((h * wcol).astype(BF16), wd_ref[0], preferred_element_type=F32)
        place = jnp.where(rcol == lax.broadcasted_iota(jnp.int32, (tm, chunk), 1).astype(F32) + base,
                          1.0, 0.0).astype(BF16)
        y_ref[...] += jnp.dot(place, yv.astype(BF16), preferred_element_type=F32)
        return carry

    lax.fori_loop(0, (n_tok + chunk - 1) // chunk, body, 0)

    @pl.when(e == n_exp - 1)
    def _():
        y_ref[...] = _rmsnorm(x_ref[...] + y_ref[...], gf_ref[...])


def _moe_final(x2d, gain, w_router, wg, wu, wd, gain_final, name):
    M, D = x2d.shape
    n_exp, _, Fd = wg.shape
    assert n_exp <= SUBLANES
    tm = _tile(M, 1024)
    chunk = min(288, tm) if tm >= 1024 else min(128, tm)
    wrt = jnp.zeros((LANES, D), F32).at[:n_exp].set(w_router.T)
    return pl.pallas_call(
        functools.partial(_moe_kernel, n_exp=n_exp, chunk=chunk), grid=(M // tm, n_exp),
        in_specs=[pl.BlockSpec((tm, D), lambda i, e: (i, 0)), pl.BlockSpec((1, D), lambda i, e: (0, 0)),
                  pl.BlockSpec((LANES, D), lambda i, e: (0, 0)),
                  pl.BlockSpec((1, D, Fd), lambda i, e: (e, 0, 0)),
                  pl.BlockSpec((1, D, Fd), lambda i, e: (e, 0, 0)),
                  pl.BlockSpec((1, Fd, D), lambda i, e: (e, 0, 0)),
                  pl.BlockSpec((1, D), lambda i, e: (0, 0))],
        out_specs=pl.BlockSpec((tm, D), lambda i, e: (i, 0)),
        out_shape=jax.ShapeDtypeStruct((M, D), F32), name=name,
        scratch_shapes=[pltpu.VMEM((tm, D), BF16), pltpu.VMEM((SUBLANES, tm), F32),
                        pltpu.VMEM((SUBLANES, tm), F32), pltpu.VMEM((tm, SUBLANES), F32)],
        compiler_params=_cparams(("parallel", "arbitrary")))(
            x2d, gain.reshape(1, D), wrt, wg, wu, wd, gain_final.reshape(1, D))


def _band_kernel(*refs, n_hp, dil, span, has_prev):
    group = lambda i: refs[i * n_hp:(i + 1) * n_hp]
    if has_prev:
        q_refs, kp_refs, kc_refs, vp_refs, vc_refs = [group(i) for i in range(5)]
        bias_ref, o_ref, lse_ref, o_scr, lse_scr = refs[5 * n_hp:]
    else:
        q_refs, kc_refs, vc_refs = [group(i) for i in range(3)]
        bias_ref, o_ref, lse_ref, o_scr, lse_scr = refs[3 * n_hp:]
    t, hs = pl.program_id(1), pl.program_id(2)
    tq = q_refs[0].shape[1] // dil
    if has_prev:
        rowk = lax.broadcasted_iota(jnp.int32, (span + tq, tq), 0)
        bias = bias_ref[...] + jnp.where((rowk < span) & (t == 0), NEG, 0.0)
    else:
        bias = bias_ref[span:, :]
    row = lax.broadcasted_iota(jnp.int32, (LANES, tq), 0)
    first_head = hs * (2 * n_hp)

    @pl.when(hs == 0)
    def _():
        lse_ref[...] = jnp.zeros_like(lse_ref)

    def stream(r, carry):
        rows = lambda n: pl.ds(r, n, stride=dil) if dil > 1 else pl.ds(0, n)
        lses = []
        for j in range(n_hp):
            qT = jnp.transpose(q_refs[j][0, rows(tq), :]).astype(BF16)
            k, v = kc_refs[j][0, rows(tq), :], vc_refs[j][0, rows(tq), :]
            if has_prev:
                k = jnp.concatenate([kp_refs[j][0, rows(span), :], k], axis=0)
                v = jnp.concatenate([vp_refs[j][0, rows(span), :], v], axis=0)
            k = k.astype(BF16)
            vT = jnp.transpose(v).astype(BF16)
            outs = []
            for h in range(2):
                qpad = jnp.where(row // HEAD_DIM == h, qT, jnp.zeros_like(qT))
                s = jnp.dot(k, qpad, preferred_element_type=F32) + bias
                m = jnp.max(s, axis=0, keepdims=True)
                p = jnp.exp(s - m)
                l = jnp.sum(p, axis=0, keepdims=True)
                outs.append(jnp.dot(vT[h * HEAD_DIM:(h + 1) * HEAD_DIM], p.astype(BF16),
                                    preferred_element_type=F32) / l)
                lses.append(m + jnp.log(l))
            o_scr[j, rows(tq), :] = jnp.transpose(jnp.concatenate(outs, axis=0))
        stat = jnp.transpose(jnp.concatenate(lses + [jnp.zeros((LANES - len(lses), tq), F32)], axis=0))
        lse_scr[rows(tq), :] = pltpu.roll(stat, first_head, 1)
        return carry

    lax.fori_loop(0, dil, stream, 0)
    for j in range(n_hp):
        o_ref[0, :, j * LANES:(j + 1) * LANES] = o_scr[j]
    lane = lax.broadcasted_iota(jnp.int32, lse_scr.shape, 1)
    mine = (lane >= first_head) & (lane < first_head + 2 * n_hp)
    lse_ref[0] = jnp.where(mine, lse_scr[...], lse_ref[0])


def _band_attention(q, k, v, dil, span, tq, name, n_hp=2):
    B, S, W = q.shape
    L = S // dil
    tq = min(tq, L)
    assert L % tq == 0 and tq % span == 0
    per_tile = tq // span
    n_pairs = W // LANES
    assert n_pairs % n_hp == 0
    i = np.arange(span + tq)[:, None]
    j = np.arange(tq)[None, :]
    bias = jnp.asarray(np.where((j - i + span >= 0) & (j - i + span <= span), 0.0, NEG), F32)
    cur = lambda jj: pl.BlockSpec((1, dil * tq, LANES), lambda b, t, hs: (b, t, hs * n_hp + jj))
    prev = lambda jj: pl.BlockSpec((1, dil * span, LANES),
                                   lambda b, t, hs: (b, jnp.maximum(t * per_tile - 1, 0), hs * n_hp + jj))
    each = lambda mk: [mk(jj) for jj in range(n_hp)]
    has_prev = L > tq
    if has_prev:
        specs = each(cur) + each(prev) + each(cur) + each(prev) + each(cur)
        args = [q] * n_hp + [k] * (2 * n_hp) + [v] * (2 * n_hp)
    else:
        specs = each(cur) * 3
        args = [q] * n_hp + [k] * n_hp + [v] * n_hp
    res = pl.pallas_call(
        functools.partial(_band_kernel, n_hp=n_hp, dil=dil, span=span, has_prev=has_prev),
        grid=(B, L // tq, n_pairs // n_hp),
        in_specs=specs + [pl.BlockSpec(bias.shape, lambda b, t, hs: (0, 0))],
        out_specs=[pl.BlockSpec((1, dil * tq, n_hp * LANES), lambda b, t, hs: (b, t, hs)),
                   pl.BlockSpec((1, dil * tq, LANES), lambda b, t, hs: (b, t, 0))],
        out_shape=[jax.ShapeDtypeStruct((B, S, W), F32), jax.ShapeDtypeStruct((B, S, LANES), F32)],
        scratch_shapes=[pltpu.VMEM((n_hp, dil * tq, LANES), F32), pltpu.VMEM((dil * tq, LANES), F32)],
        name=name, compiler_params=_cparams(("parallel", "parallel", "arbitrary")))(*args, bias)
    return res[0].reshape(B * S, W), res[1].reshape(B * S, LANES)


def _merge_outproj_kernel(*refs, n_groups):
    x_ref = refs[0]
    o_refs, l_refs = refs[1:1 + n_groups], refs[1 + n_groups:1 + 2 * n_groups]
    w_ref, y_ref = refs[-2], refs[-1]
    W = o_refs[0].shape[1]
    lses = [r[...] for r in l_refs]
    m = functools.reduce(jnp.maximum, lses)
    es = [jnp.exp(l - m) for l in lses]
    tot = functools.reduce(lambda a, b: a + b, es)
    expand = (lax.broadcasted_iota(jnp.int32, (LANES, W), 1) // HEAD_DIM
              == lax.broadcasted_iota(jnp.int32, (LANES, W), 0)).astype(BF16)
    mix = None
    for e, o_ref in zip(es, o_refs):
        hi, mid, _ = _split3(e / tot)
        wexp = jnp.dot(hi, expand, preferred_element_type=F32) + jnp.dot(mid, expand, preferred_element_type=F32)
        mix = wexp * o_ref[...] if mix is None else mix + wexp * o_ref[...]
    y_ref[...] = x_ref[...] + jnp.dot(mix.astype(BF16), w_ref[...], preferred_element_type=F32)


def _merge_outproj(x2d, outs, lses, w_bf, name):
    M, D = x2d.shape
    W = outs[0].shape[1]
    tm = _tile(M, 256)
    row = lambda n: pl.BlockSpec((tm, n), lambda i: (i, 0))
    return pl.pallas_call(
        functools.partial(_merge_outproj_kernel, n_groups=len(outs)), grid=(M // tm,),
        in_specs=[row(D)] + [row(W)] * len(outs) + [row(LANES)] * len(lses)
        + [pl.BlockSpec(w_bf.shape, lambda i: (0, 0))],
        out_specs=row(D), out_shape=jax.ShapeDtypeStruct((M, D), F32), name=name,
        compiler_params=_cparams(("parallel",)))(x2d, *outs, *lses, w_bf)


def _head_major(x, n_heads, pad_rows):
    DB, T, _ = x.shape
    y = x.reshape(DB, T, n_heads, HEAD_DIM).transpose(0, 2, 1, 3)
    return jnp.pad(y, ((0, 0), (0, 0), (0, pad_rows - T), (0, 0)))


def _new_cols(x, n_heads):
    DB, T, _ = x.shape
    y = x.reshape(DB, T, n_heads, HEAD_DIM).transpose(0, 2, 3, 1)
    return jnp.pad(y, ((0, 0), (0, 0), (0, 0), (0, NEW_PAD - T)))


def _row_of(col_vec_row):
    return jnp.transpose(jnp.broadcast_to(col_vec_row, (SUBLANES, LANES)))[0:SUBLANES, 0:1]


def _fox_dec_kernel(pt_ref, *refs, n_pg, n_q):
    kts, vts, lfs = refs[:n_pg], refs[n_pg:2 * n_pg], refs[2 * n_pg:3 * n_pg]
    q_ref, knt_ref, vnt_ref, lfn_ref, o_ref, qbd_ref, base_ref, carry_ref, m_ref, l_ref, acc_ref = refs[3 * n_pg:]
    p = pl.program_id(1)
    nh = FOX_HEADS
    page = LANES
    rows, width = nh * QPAD, nh * HEAD_DIM
    lane = lax.broadcasted_iota(jnp.int32, (QPAD, LANES), 1)
    trow = lax.broadcasted_iota(jnp.int32, (QPAD, LANES), 0)
    r = lax.broadcasted_iota(jnp.int32, (page, page), 0)
    c = lax.broadcasted_iota(jnp.int32, (page, page), 1)
    stack = lambda xs: jnp.concatenate(xs, axis=0)
    own = (lax.broadcasted_iota(jnp.int32, (rows, width), 0) // QPAD
           == lax.broadcasted_iota(jnp.int32, (rows, width), 1) // HEAD_DIM)
    heads_of = lambda ref: ref[0].reshape(width, ref.shape[3])

    @pl.when(p == 0)
    def _():
        _online_init(m_ref, l_ref, acc_ref)
        carry_ref[...] = jnp.zeros_like(carry_ref)
        q_all = q_ref[0].reshape(rows, HEAD_DIM) * SCALE
        qbd_ref[...] = jnp.where(own, jnp.concatenate([q_all] * nh, axis=1), 0.0).astype(BF16)
        cnew = _mm3_left(lfn_ref[0], r <= c)
        bases, decs = [], []
        for h in range(nh):
            bh = _row_of(cnew[h:h + 1, :])
            bases.append(bh)
            decs.append(bh - cnew[h:h + 1, :])
        base_ref[...] = jnp.broadcast_to(stack(bases), base_ref.shape)
        live = (lane <= trow) & (lane < n_q)
        s = jnp.dot(qbd_ref[...], heads_of(knt_ref).astype(BF16), preferred_element_type=F32) + stack(decs)
        s = jnp.where(stack([live] * nh), s, NEG)
        alpha, pr = _online_step(s, m_ref, l_ref)
        acc_ref[...] = alpha * acc_ref[...] + _mm_nt(pr, heads_of(vnt_ref))

    lf_all = stack([lf[0] for lf in lfs])
    after_all = _mm3_left(lf_all, r > c)
    tot_all = after_all[:, 0:1] + lf_all[:, 0:1]
    carry = carry_ref[:, 0:1]
    decay = [None] * n_pg
    for j in reversed(range(n_pg)):
        decay[j] = carry + after_all[j * nh:(j + 1) * nh]
        carry = carry + tot_all[j * nh:(j + 1) * nh]
    carry_ref[...] = jnp.broadcast_to(carry, carry_ref.shape)
    per_row = lambda d: stack([jnp.broadcast_to(d[h:h + 1, :], (QPAD, page)) for h in range(nh)])
    s = jnp.concatenate([jnp.dot(qbd_ref[...], heads_of(kts[j]).astype(BF16), preferred_element_type=F32)
                         + per_row(decay[j]) for j in range(n_pg)], axis=1) + base_ref[:, 0:1]
    alpha, pr = _online_step(s, m_ref, l_ref)
    acc_ref[...] = alpha * acc_ref[...] + sum(_mm_nt(pr[:, j * page:(j + 1) * page], heads_of(vts[j]))
                                              for j in range(n_pg))

    @pl.when(p == pl.num_programs(1) - 1)
    def _():
        acc = jnp.where(own, acc_ref[...], 0.0)
        o_ref[0] = sum(acc[:, h * HEAD_DIM:(h + 1) * HEAD_DIM] for h in range(nh)) / l_ref[...]


def _fox_decode(q, k_new, v_new, lf_new, cache_k, cache_v, cache_lf, pt_flat):
    DB, T, W = q.shape
    n_pool, page, nh, _ = cache_k.shape
    assert page == LANES and T <= QPAD
    n_pages = pt_flat.shape[0] // DB
    n_pg = math.gcd(n_pages, PAGES_PER_STEP)
    n_steps = n_pages // n_pg
    ckt = jnp.transpose(cache_k, (0, 2, 3, 1))
    cvt = jnp.transpose(cache_v, (0, 2, 3, 1))
    clf = jnp.transpose(cache_lf, (0, 2, 1))
    qh = _head_major(q, nh, QPAD)
    knt, vnt = _new_cols(k_new, nh), _new_cols(v_new, nh)
    lfn = jnp.pad(jnp.transpose(lf_new, (0, 2, 1)), ((0, 0), (0, 0), (0, LANES - T)))
    page_idx = lambda j: (lambda b, p, pt: (pt[b * n_pages + (n_steps - 1 - p) * n_pg + j], 0, 0, 0))
    lf_idx = lambda j: (lambda b, p, pt: (pt[b * n_pages + (n_steps - 1 - p) * n_pg + j], 0, 0))
    per_db = lambda a: pl.BlockSpec((1,) + a.shape[1:], lambda b, p, pt: (b,) + (0,) * (a.ndim - 1))
    rows = nh * QPAD
    grid_spec = pltpu.PrefetchScalarGridSpec(
        num_scalar_prefetch=1, grid=(DB, n_steps),
        in_specs=([pl.BlockSpec((1, nh, HEAD_DIM, page), page_idx(j)) for j in range(n_pg)] * 2
                  + [pl.BlockSpec((1, nh, page), lf_idx(j)) for j in range(n_pg)]
                  + [per_db(qh), per_db(knt), per_db(vnt), per_db(lfn)]),
        out_specs=pl.BlockSpec((1, rows, HEAD_DIM), lambda b, p, pt: (b, 0, 0)),
        scratch_shapes=[pltpu.VMEM((rows, W), BF16), pltpu.VMEM((rows, LANES), F32), pltpu.VMEM((nh, LANES), F32),
                        pltpu.VMEM((rows, 1), F32), pltpu.VMEM((rows, 1), F32), pltpu.VMEM((rows, W), F32)])
    o = pl.pallas_call(
        functools.partial(_fox_dec_kernel, n_pg=n_pg, n_q=T), grid_spec=grid_spec,
        out_shape=jax.ShapeDtypeStruct((DB, rows, HEAD_DIM), F32), name="fox_decode",
        compiler_params=_cparams(("parallel", "arbitrary")))(
            pt_flat, *([ckt] * n_pg), *([cvt] * n_pg), *([clf] * n_pg), qh, knt, vnt, lfn)
    return o.reshape(DB, nh, QPAD, HEAD_DIM)[:, :, :T].transpose(0, 2, 1, 3).reshape(DB, T, W)


def _nsa_dec_a_kernel(q_ref, kct_ref, vct_ref, kwt_ref, vwt_ref, kwnt_ref, vwnt_ref, gt_ref, map_ref,
                      o_ref, bias_ref, *, n_q, ns, n_sel, past, n_keys):
    nr = NSA_GROUP * QPAD
    n_ch = kct_ref.shape[3]
    wb = kwt_ref.shape[3]
    t_row = lax.broadcasted_iota(jnp.int32, (nr, 1), 0) % QPAD
    imps = []
    for kv in range(NSA_KV_HEADS):
        q = (q_ref[0, kv] * SCALE).astype(BF16)
        cidx = lax.broadcasted_iota(jnp.int32, (nr, n_ch), 1)
        pc = _softmax_rows(_mm(q, kct_ref[0, kv]), cidx >= 1)
        oc = _mm_nt(pc, vct_ref[0, kv])
        pcat = jnp.concatenate([pc[j * QPAD:(j + 1) * QPAD] for j in range(NSA_GROUP)], axis=1)
        imps.append(jnp.dot(pcat.astype(BF16), map_ref[...], preferred_element_type=F32))
        sw = _mm(q, kwt_ref[0, kv])
        sn = _mm(q, kwnt_ref[0, kv])
        iw = lax.broadcasted_iota(jnp.int32, (nr, wb), 1)
        un = lax.broadcasted_iota(jnp.int32, (nr, NEW_PAD), 1)
        mw = (wb + t_row - iw <= NSA_WINDOW) & (past - wb + iw >= 0)
        mn = (un <= t_row) & (un < n_q)
        sw = jnp.where(mw, sw, NEG)
        sn = jnp.where(mn, sn, NEG)
        m = jnp.maximum(jnp.max(sw, axis=1, keepdims=True), jnp.max(sn, axis=1, keepdims=True))
        pw = jnp.where(mw, jnp.exp(sw - m), 0.0)
        pn = jnp.where(mn, jnp.exp(sn - m), 0.0)
        l = jnp.sum(pw, axis=1, keepdims=True) + jnp.sum(pn, axis=1, keepdims=True)
        ow = (_mm_nt(pw, vwt_ref[0, kv]) + _mm_nt(pn, vwnt_ref[0, kv])) / l
        gt = gt_ref[0, kv]
        o_ref[0, kv] = gt[:, 0:1] * oc + gt[:, 2:3] * ow

    imp = jnp.concatenate(imps, axis=0)
    t_sel = lax.broadcasted_iota(jnp.int32, (imp.shape[0], 1), 0) % QPAD
    sel = _select_blocks(imp, past + t_sel, ns, n_sel).astype(BF16)
    ch = 8 * LANES
    for c0 in range(0, n_keys, ch):
        w = min(ch, n_keys - c0)
        key = c0 + lax.broadcasted_iota(jnp.int32, (sel.shape[1], w), 1)
        onehot = (key // NSA_SLC_BLOCK == lax.broadcasted_iota(jnp.int32, (sel.shape[1], w), 0)).astype(BF16)
        chosen = jnp.dot(sel, onehot, preferred_element_type=F32) > 0.5
        kpos = c0 + lax.broadcasted_iota(jnp.int32, (sel.shape[0], w), 1)
        ok = chosen & (kpos <= past + t_sel) & (kpos < past + n_q)
        bias_ref[0, :, c0:c0 + w] = jnp.where(ok, 0.0, NEG)


def _nsa_dec_b_kernel(pt_ref, *refs, n_pg):
    kts, vts = refs[:n_pg], refs[n_pg:2 * n_pg]
    q_ref, knt_ref, vnt_ref, bias_ref, biasn_ref, o_ref, qbd_ref, m_ref, l_ref, acc_ref = refs[2 * n_pg:]
    p = pl.program_id(1)
    nr = NSA_GROUP * QPAD
    page = LANES
    rows, width = NSA_KV_HEADS * nr, NSA_KV_HEADS * HEAD_DIM
    own = (lax.broadcasted_iota(jnp.int32, (rows, width), 0) // nr
           == lax.broadcasted_iota(jnp.int32, (rows, width), 1) // HEAD_DIM)
    heads_of = lambda ref: ref[0].reshape(width, ref.shape[3])
    per_row = lambda b: jnp.concatenate([b[kv * QPAD:(kv + 1) * QPAD] for kv in range(NSA_KV_HEADS)
                                         for _ in range(NSA_GROUP)], axis=0)

    @pl.when(p == 0)
    def _():
        _online_init(m_ref, l_ref, acc_ref)
        q_all = q_ref[0].reshape(rows, HEAD_DIM) * SCALE
        qbd_ref[...] = jnp.where(own, jnp.concatenate([q_all] * NSA_KV_HEADS, axis=1), 0.0).astype(BF16)

    s = jnp.concatenate([jnp.dot(qbd_ref[...], heads_of(kts[j]).astype(BF16), preferred_element_type=F32)
                         for j in range(n_pg)], axis=1) + per_row(bias_ref[0])
    alpha, pr = _online_step(s, m_ref, l_ref)
    acc_ref[...] = alpha * acc_ref[...] + sum(_mm_nt(pr[:, j * page:(j + 1) * page], heads_of(vts[j]))
                                              for j in range(n_pg))

    @pl.when(p == pl.num_programs(1) - 1)
    def _():
        s = jnp.dot(qbd_ref[...], heads_of(knt_ref).astype(BF16), preferred_element_type=F32) + per_row(biasn_ref[0])
        alpha, pr = _online_step(s, m_ref, l_ref)
        acc = jnp.where(own, alpha * acc_ref[...] + _mm_nt(pr, heads_of(vnt_ref)), 0.0)
        o_ref[0] = sum(acc[:, kv * HEAD_DIM:(kv + 1) * HEAD_DIM] for kv in range(NSA_KV_HEADS)) / l_ref[...]


def _nsa_decode(qb, kct, vct, cache_ks, cache_vs, ks_new, vs_new, swa_k, swa_v, kw_new, vw_new,
                gates, pt_flat, past):
    DB, T, W = qb.shape
    n_pool, page = cache_ks.shape[:2]
    assert page == LANES and past % LANES == 0 and T <= QPAD
    n_pages = pt_flat.shape[0] // DB
    n_ch = kct.shape[3]
    ns = -(-(past + T) // NSA_SLC_BLOCK)
    ns_pad = -(-ns // LANES) * LANES
    n_sel = min(NSA_TOPN, ns)
    nr = NSA_GROUP * QPAD
    n_keys = past + NEW_PAD
    q5 = _head_major(qb, NSA_HEADS, QPAD).reshape(DB, NSA_KV_HEADS, nr, HEAD_DIM)
    g5 = _head_major(jnp.pad(gates.reshape(DB, T, NSA_HEADS, 3), ((0, 0),) * 3 + ((0, HEAD_DIM - 3),))
                     .reshape(DB, T, NSA_HEADS * HEAD_DIM), NSA_HEADS, QPAD)[..., :3]
    g5 = g5.reshape(DB, NSA_KV_HEADS, nr, 3)
    smap = jnp.asarray(np.tile(_slc_map_rows(n_ch, ns_pad), (NSA_GROUP, 1)), dtype=BF16)
    kwt = jnp.transpose(swa_k, (0, 2, 3, 1))
    vwt = jnp.transpose(swa_v, (0, 2, 3, 1))
    kwnt, vwnt = _new_cols(kw_new, NSA_KV_HEADS), _new_cols(vw_new, NSA_KV_HEADS)
    per = lambda a: pl.BlockSpec((1,) + a.shape[1:], lambda b: (b,) + (0,) * (a.ndim - 1))
    part, bias = pl.pallas_call(
        functools.partial(_nsa_dec_a_kernel, n_q=T, ns=ns, n_sel=n_sel, past=past, n_keys=n_keys), grid=(DB,),
        in_specs=[per(q5), per(kct), per(vct), per(kwt), per(vwt), per(kwnt), per(vwnt), per(g5),
                  pl.BlockSpec(smap.shape, lambda b: (0, 0))],
        out_specs=[pl.BlockSpec((1, NSA_KV_HEADS, nr, HEAD_DIM), lambda b: (b, 0, 0, 0)),
                   pl.BlockSpec((1, NSA_KV_HEADS * QPAD, n_keys), lambda b: (b, 0, 0))],
        out_shape=[jax.ShapeDtypeStruct((DB, NSA_KV_HEADS, nr, HEAD_DIM), F32),
                   jax.ShapeDtypeStruct((DB, NSA_KV_HEADS * QPAD, n_keys), F32)], name="nsa_decode_a",
        compiler_params=_cparams(("parallel",)))(q5, kct, vct, kwt, vwt, kwnt, vwnt, g5, smap)

    n_pg = math.gcd(n_pages, 2 * PAGES_PER_STEP)
    ckt = jnp.transpose(cache_ks, (0, 2, 3, 1))
    cvt = jnp.transpose(cache_vs, (0, 2, 3, 1))
    ksnt, vsnt = _new_cols(ks_new, NSA_KV_HEADS), _new_cols(vs_new, NSA_KV_HEADS)
    page_idx = lambda j: (lambda b, p, pt: (pt[b * n_pages + p * n_pg + j], 0, 0, 0))
    per_db = lambda a: pl.BlockSpec((1,) + a.shape[1:], lambda b, p, pt: (b,) + (0,) * (a.ndim - 1))
    rows = NSA_KV_HEADS * nr
    grid_spec = pltpu.PrefetchScalarGridSpec(
        num_scalar_prefetch=1, grid=(DB, n_pages // n_pg),
        in_specs=([pl.BlockSpec((1, NSA_KV_HEADS, HEAD_DIM, page), page_idx(j)) for j in range(n_pg)] * 2
                  + [per_db(q5), per_db(ksnt), per_db(vsnt),
                     pl.BlockSpec((1, NSA_KV_HEADS * QPAD, n_pg * page), lambda b, p, pt: (b, 0, p)),
                     pl.BlockSpec((1, NSA_KV_HEADS * QPAD, NEW_PAD), lambda b, p, pt: (b, 0, past // NEW_PAD))]),
        out_specs=pl.BlockSpec((1, rows, HEAD_DIM), lambda b, p, pt: (b, 0, 0)),
        scratch_shapes=[pltpu.VMEM((rows, LANES), BF16), pltpu.VMEM((rows, 1), F32), pltpu.VMEM((rows, 1), F32),
                        pltpu.VMEM((rows, LANES), F32)])
    osl = pl.pallas_call(
        functools.partial(_nsa_dec_b_kernel, n_pg=n_pg), grid_spec=grid_spec,
        out_shape=jax.ShapeDtypeStruct((DB, rows, HEAD_DIM), F32), name="nsa_decode_b",
        compiler_params=_cparams(("parallel", "arbitrary")))(
            pt_flat, *([ckt] * n_pg), *([cvt] * n_pg), q5, ksnt, vsnt, bias, bias)

    o = part + g5[..., 1:2] * osl.reshape(DB, NSA_KV_HEADS, nr, HEAD_DIM)
    return o.reshape(DB, NSA_HEADS, QPAD, HEAD_DIM)[:, :, :T].transpose(0, 2, 1, 3).reshape(DB, T, W)


def _dil_dec_kernel(q_ref, kt_ref, vt_ref, knt_ref, vnt_ref, o_ref, *, n_q, wc):
    hb = kt_ref.shape[1]
    t = lax.broadcasted_iota(jnp.int32, (QPAD, 1), 0)

    def log_mult(d, ok):
        w = jnp.zeros(d.shape, F32)
        for window, dil in DIL_PAIRS:
            w = w + ((d >= 0) & (d <= window) & (d % dil == 0)).astype(F32)
        return jnp.where(ok, w, 0.0)

    wk = log_mult(wc + t - lax.broadcasted_iota(jnp.int32, (QPAD, wc), 1), t < n_q)
    un = lax.broadcasted_iota(jnp.int32, (QPAD, NEW_PAD), 1)
    wn = log_mult(t - un, (t < n_q) & (un < n_q))
    for h in range(hb):
        q = (q_ref[0, h] * SCALE).astype(BF16)
        sk = jnp.where(wk > 0.0, _mm(q, kt_ref[0, h]), NEG)
        sn = jnp.where(wn > 0.0, _mm(q, knt_ref[0, h]), NEG)
        m = jnp.maximum(jnp.max(sk, axis=1, keepdims=True), jnp.max(sn, axis=1, keepdims=True))
        pk = wk * jnp.exp(sk - m)
        pn = wn * jnp.exp(sn - m)
        l = jnp.sum(pk, axis=1, keepdims=True) + jnp.sum(pn, axis=1, keepdims=True)
        o_ref[0, h] = (_mm_nt(pk, vt_ref[0, h]) + _mm_nt(pn, vnt_ref[0, h])) / jnp.where(l > 0.0, l, 1.0)


def _dilated_decode(q, k_new, v_new, cache_k, cache_v):
    DB, T, W = q.shape
    wc, nh = cache_k.shape[1], cache_k.shape[2]
    ckt = jnp.transpose(cache_k, (0, 2, 3, 1))
    cvt = jnp.transpose(cache_v, (0, 2, 3, 1))
    qh = _head_major(q, nh, QPAD)
    knt, vnt = _new_cols(k_new, nh), _new_cols(v_new, nh)
    hb = math.gcd(nh, 4)
    spec = lambda a: pl.BlockSpec((1, hb) + a.shape[2:], lambda b, j: (b, j, 0, 0))
    o = pl.pallas_call(
        functools.partial(_dil_dec_kernel, n_q=T, wc=wc), grid=(DB, nh // hb),
        in_specs=[spec(qh), spec(ckt), spec(cvt), spec(knt), spec(vnt)],
        out_specs=pl.BlockSpec((1, hb, QPAD, HEAD_DIM), lambda b, j: (b, j, 0, 0)),
        out_shape=jax.ShapeDtypeStruct((DB, nh, QPAD, HEAD_DIM), F32), name="dilated_decode",
        compiler_params=_cparams(("parallel", "parallel")))(qh, ckt, cvt, knt, vnt)
    return o[:, :, :T].transpose(0, 2, 1, 3).reshape(DB, T, W)


def kernel(x_prompt, x_sample, cache_a_k, cache_a_v, cache_a_logf, cache_b_cmp_k, cache_b_cmp_v, cache_b_slc_k, cache_b_slc_v, cache_b_swa_k, cache_b_swa_v, cache_c_k, cache_c_v, page_table, norm_mix0, w_in0, fox_bf, nsa_pe_k, nsa_w1_k, nsa_w2_k, nsa_pe_v, nsa_w1_v, nsa_w2_v, w_out0, norm_ffn0, ffn_w_gate, ffn_w_up, ffn_w_down, norm_mix1, w_in1, w_out1, norm_ffn1, moe_router, moe_w_gate, moe_w_up, moe_w_down, norm_final):
    B, S, D = x_prompt.shape
    DB, T, _ = x_sample.shape
    n_pages = page_table.shape[1]
    past = n_pages * cache_a_k.shape[1]
    pt_flat = page_table.reshape(-1).astype(jnp.int32)
    fw = FOX_HEADS * HEAD_DIM
    nw = NSA_HEADS * HEAD_DIM
    kvw = NSA_KV_HEADS * HEAD_DIM

    cuts = np.cumsum([0, fw, fw, fw, FOX_HEADS, nw] + [kvw] * 6 + [3 * NSA_HEADS])
    col = lambda i: w_in0[:, cuts[i]:cuts[i + 1]]
    qa_w, ka_w, va_w, fa_w, qb_w, kc_w, vc_w, ks_w, vs_w, kw_w, vw_w, gb_w = [col(i) for i in range(12)]

    f32_out = ((F32, 1.0),)
    bf16_out = ((BF16, 1.0),)

    def pack(ws, ropes, emits=None):
        widths = [w.shape[1] for w in ws]
        starts = np.concatenate([[0], np.cumsum(widths)[:-1]])
        emits = emits or [f32_out] * len(ws)
        return (jnp.concatenate(ws, axis=1).astype(BF16),
                [(int(s), int(w), r, e) for s, w, r, e in zip(starts, widths, ropes, emits)])

    both_out = ((F32, 1.0), (BF16, 1.0))
    q_out = ((BF16, SCALE),)
    w0r, segs0r = pack([kc_w, vc_w, ka_w, ks_w, kw_w], [True, False, False, True, True],
                       [f32_out] * 2 + [bf16_out] * 3)
    w0c, segs0c = pack([qa_w, ka_w, va_w, qb_w, kc_w, ks_w, kw_w, vc_w, vs_w, vw_w, gb_w],
                       [False, False, False, True, True, True, True, False, False, False, "sigmoid"],
                       [q_out, f32_out, both_out, q_out, f32_out, f32_out, f32_out, f32_out, both_out, both_out,
                        f32_out])
    w0c = w0c.T
    w0s, segs0s = pack([qa_w, ka_w, va_w, qb_w, kc_w, ks_w, kw_w, vc_w, vs_w, vw_w],
                       [False, False, False, True, True, True, True, False, False, False])
    wgate = gb_w.astype(BF16)
    wft = jnp.zeros((16, D), F32).at[:FOX_HEADS].set(fa_w.T).astype(BF16)
    wfr = jnp.zeros((D, LANES), F32).at[:, :FOX_HEADS].set(fa_w).astype(BF16)
    brow = jnp.zeros((1, LANES), F32).at[0, :FOX_HEADS].set(fox_bf)
    logf_args = (wft, fox_bf.reshape(FOX_HEADS, 1).astype(F32), wfr, brow)
    dw = w_in1.shape[1] // 3
    w1 = w_in1.astype(BF16)
    segs1 = [(0, dw, True, f32_out), (dw, dw, True, f32_out), (2 * dw, dw, False, f32_out)]
    w1c = w1[:, dw:].T
    segs1c = [(0, dw, True, f32_out), (dw, dw, False, f32_out)]
    w_out0_b, w_out1_b = w_out0.astype(BF16), w_out1.astype(BF16)
    ffn_g, ffn_u, ffn_d = ffn_w_gate.astype(BF16), ffn_w_up.astype(BF16), ffn_w_down.astype(BF16)
    moe_g, moe_u, moe_d = moe_w_gate.astype(BF16), moe_w_up.astype(BF16), moe_w_down.astype(BF16)
    cmp_k_w = _cmp_weights(nsa_pe_k, nsa_w1_k, nsa_w2_k)
    cmp_v_w = _cmp_weights(nsa_pe_v, nsa_w1_v, nsa_w2_v)

    tab_p = _rope_tables(jnp.arange(S))
    tab_s = _rope_tables(past + jnp.arange(DB * T) % T)
    tm_p = _tile(S, 512)
    npb = S // tm_p
    win_b = min(NSA_WINDOW, S)
    win_c = min(DIL_WINDOW_MAX, S)

    xp = x_prompt.reshape(B * S, D)
    (kc, vc, ka_b, ks_b, kw_b, qat_b, kat, vat, vat_b, qbt_b, kct, kst, kwt, vct, vst, vst_b, vwt, vwt_b,
     gates_t, lft_p, lf_rows) = _project(
        xp, norm_mix0, tab_p, npb, tm_p, w=w0r, row_segs=segs0r, wt=w0c, col_segs=segs0c,
        logf=logf_args, name="proj0_prompt")
    r3 = lambda a: a.reshape(B, S, a.shape[-1])
    heads = lambda a, h: a.reshape(B, h, HEAD_DIM, a.shape[-1])
    kaug, c0 = _fox_prep(r3(ka_b), r3(lf_rows), tm_p)
    o_at = _fox_prompt(qat_b, kaug, vat_b, c0, tm_p)
    kcmp_p = _compress(r3(kc), cmp_k_w, "compress_k_prompt")
    vcmp_p = _compress(r3(vc), cmp_v_w, "compress_v_prompt")
    kv2 = lambda a: heads(a, NSA_KV_HEADS)
    o_bt = _nsa_prompt(qbt_b, kv2(kcmp_p), kv2(vcmp_p), r3(ks_b), vst_b, r3(kw_b), vwt_b, gates_t)
    hp = _outproj(xp, [o_at, o_bt], w_out0_b, "outproj0_prompt", tm=tm_p)
    hp = _ffn(hp, norm_ffn0, ffn_g, ffn_u, ffn_d, "ffn_prompt")

    xs = x_sample.reshape(DB * T, D)
    (qa_s, ka_s, va_s, qb_s, kc_s, ks_s, kw_s, vc_s, vs_s, vw_s, gates_s, lft_s, _) = _project(
        xs, norm_mix0, tab_s, 1, DB * T, w=w0s, row_segs=segs0s, w_gate=wgate, logf=logf_args,
        name="proj0_sample")
    s3 = lambda a: a.reshape(DB, T, a.shape[-1])
    lf_s = jnp.transpose(lft_s[0].reshape(FOX_HEADS, DB, T), (1, 2, 0))
    o_a_s = _fox_decode(s3(qa_s), s3(ka_s), s3(va_s), lf_s, cache_a_k, cache_a_v, cache_a_logf, pt_flat)
    kcmp_s = _compress_paged(cache_b_cmp_k, pt_flat, DB, cmp_k_w, "compress_k_paged")
    vcmp_s = _compress_paged(cache_b_cmp_v, pt_flat, DB, cmp_v_w, "compress_v_paged")
    kvs = lambda a: a.reshape(DB, NSA_KV_HEADS, HEAD_DIM, a.shape[-1])
    o_b_s = _nsa_decode(s3(qb_s), kvs(kcmp_s), kvs(vcmp_s), cache_b_slc_k, cache_b_slc_v, s3(ks_s), s3(vs_s),
                        cache_b_swa_k, cache_b_swa_v, s3(kw_s), s3(vw_s), s3(gates_s), pt_flat, past)
    hs = _outproj(xs, [o_a_s.reshape(DB * T, fw), o_b_s.reshape(DB * T, nw)], w_out0_b, "outproj0_sample")
    hs = _ffn(hs, norm_ffn0, ffn_g, ffn_u, ffn_d, "ffn_sample")

    first_c = (S - win_c) // tm_p
    segs1p = [(0, dw, True, ((F32, SCALE),)), (dw, dw, True, f32_out), (2 * dw, dw, False, f32_out)]
    q1, k1, v1, k1t, v1t = _project(hp, norm_mix1, tab_p, npb, tm_p, w=w1, row_segs=segs1p, wt=w1c,
                                    col_segs=segs1c, col_from=(npb, first_c), name="proj1_prompt")
    groups = [_band_attention(q1.reshape(B, S, dw), k1.reshape(B, S, dw), v1.reshape(B, S, dw), dil,
                              window // dil, 512 if dil == 1 else 256, "dilated_prompt_d%d" % dil,
                              n_hp=4 if dil < 16 else 2)
              for window, dil in DIL_PAIRS]
    hp = _merge_outproj(hp, [g[0] for g in groups], [g[1] for g in groups], w_out1_b, "outproj1_prompt")
    y_prompt = _moe_final(hp, norm_ffn1, moe_router, moe_g, moe_u, moe_d, norm_final, "moe_prompt").reshape(B, S, D)

    q1s, k1s, v1s = _project(hs, norm_mix1, tab_s, 1, DB * T, w=w1, row_segs=segs1, name="proj1_sample")
    o1s = _dilated_decode(s3(q1s), s3(k1s), s3(v1s), cache_c_k, cache_c_v)
    hs = _outproj(hs, [o1s.reshape(DB * T, dw)], w_out1_b, "outproj1_sample")
    y_sample = _moe_final(hs, norm_ffn1, moe_router, moe_g, moe_u, moe_d, norm_final, "moe_sample").reshape(DB, T, D)

    def state(a, h, last=None):
        a = a.reshape(a.shape[0], h, HEAD_DIM, a.shape[-1])
        if last is not None:
            a = a[..., a.shape[-1] - last:]
        return jnp.transpose(a, (0, 3, 1, 2))

    h4 = lambda a, h: a.reshape(DB, T, h, HEAD_DIM)
    nh1 = dw // HEAD_DIM
    return (y_prompt, y_sample,
            state(kat, FOX_HEADS), state(vat, FOX_HEADS), jnp.transpose(lft_p, (0, 2, 1)),
            state(kct, NSA_KV_HEADS), state(vct, NSA_KV_HEADS), state(kst, NSA_KV_HEADS), state(vst, NSA_KV_HEADS),
            state(kwt, NSA_KV_HEADS, win_b), state(vwt, NSA_KV_HEADS, win_b),
            state(k1t, nh1, win_c), state(v1t, nh1, win_c),
            h4(ka_s, FOX_HEADS), h4(va_s, FOX_HEADS), lf_s,
            h4(kc_s, NSA_KV_HEADS), h4(vc_s, NSA_KV_HEADS), h4(ks_s, NSA_KV_HEADS),
            h4(vs_s, NSA_KV_HEADS), h4(kw_s, NSA_KV_HEADS), h4(vw_s, NSA_KV_HEADS),
            h4(k1s, nh1), h4(v1s, nh1))
```

```python
import functools
import math

import numpy as np
import jax
import jax.numpy as jnp
from jax import lax
from jax.experimental import pallas as pl
from jax.experimental.pallas import tpu as pltpu

F32 = jnp.float32
BF16 = jnp.bfloat16

HEAD_DIM = 64
HALF = HEAD_DIM // 2
LANES = 128
SUBLANES = 8
ROPE_THETA = 10000.0
RMS_EPS = 1e-6
NEG = -1e30
MASK_BIG = 30000.0
SCALE = HEAD_DIM ** -0.5

FOX_HEADS = 8
NSA_HEADS = 8
NSA_KV_HEADS = 2
NSA_GROUP = NSA_HEADS // NSA_KV_HEADS
NSA_CMP_LEN = 32
NSA_CMP_STRIDE = 16
NSA_SLC_BLOCK = 64
NSA_TOPN = 16
NSA_WINDOW = 512
NSA_FORCE_BONUS = 1e3
DIL_PAIRS = ((128, 1), (512, 4), (2048, 16))
DIL_WINDOW_MAX = 2048
TOP_K = 2
QPAD = SUBLANES
NEW_PAD = LANES
PAGES_PER_STEP = 32

VMEM_LIMIT = 56 * 1024 * 1024


def _tile(n, pref):
    return pref if n % pref == 0 else n


def _cparams(sem):
    return pltpu.CompilerParams(dimension_semantics=sem, vmem_limit_bytes=VMEM_LIMIT)


def _mm(a, b):
    return jnp.dot(a.astype(BF16), b.astype(BF16), preferred_element_type=F32)


def _mm_nt(a, b):
    return lax.dot_general(a.astype(BF16), b.astype(BF16), (((1,), (1,)), ((), ())),
                           preferred_element_type=F32)


def _split3(x):
    hi = x.astype(BF16)
    r = x - hi.astype(F32)
    mid = r.astype(BF16)
    lo = (r - mid.astype(F32)).astype(BF16)
    return hi, mid, lo


def _mm3_left(x, exact_rhs):
    b = exact_rhs.astype(BF16)
    hi, mid, lo = _split3(x)
    d = lambda p: jnp.dot(p, b, preferred_element_type=F32)
    return d(hi) + d(mid) + d(lo)


def _sigmoid(z):
    return 1.0 / (1.0 + jnp.exp(-z))


def _silu(z):
    return z * _sigmoid(z)


def _log_sigmoid(z):
    return jnp.minimum(z, 0.0) - jnp.log1p(jnp.exp(-jnp.abs(z)))


def _rmsnorm(x, g):
    return x * lax.rsqrt(jnp.mean(x * x, axis=-1, keepdims=True) + RMS_EPS) * g


def _rope_rows(y, cos, sin_signed):
    n = y.shape[1]
    lane = lax.broadcasted_iota(jnp.int32, y.shape, 1)
    first = (lane % HEAD_DIM) < HALF
    rot = jnp.where(first, pltpu.roll(y, n - HALF, 1), pltpu.roll(y, HALF, 1))
    reps = n // LANES
    if reps > 1:
        cos = jnp.concatenate([cos] * reps, axis=1)
        sin_signed = jnp.concatenate([sin_signed] * reps, axis=1)
    return y * cos + rot * sin_signed


def _rope_cols(yt, cos_t, sin_t):
    out = []
    for h in range(yt.shape[0] // HEAD_DIM):
        a = yt[h * HEAD_DIM:h * HEAD_DIM + HALF]
        b = yt[h * HEAD_DIM + HALF:(h + 1) * HEAD_DIM]
        out += [a * cos_t - b * sin_t, b * cos_t + a * sin_t]
    return jnp.concatenate(out, axis=0)


def _rope_tables(pos):
    inv = jnp.exp(-math.log(ROPE_THETA) * jnp.arange(HALF, dtype=F32) / HALF)
    ang = pos.astype(F32)[:, None] * inv[None, :]
    cos, sin = jnp.cos(ang), jnp.sin(ang)
    return (jnp.concatenate([cos, cos, cos, cos], axis=1),
            jnp.concatenate([-sin, sin, -sin, sin], axis=1), cos.T, sin.T)


def _softmax_rows(s, mask):
    sm = jnp.where(mask, s, NEG)
    m = jnp.max(sm, axis=1, keepdims=True)
    p = jnp.where(mask, jnp.exp(sm - m), 0.0)
    l = jnp.sum(p, axis=1, keepdims=True)
    return p / jnp.where(l > 0.0, l, 1.0)


def _online_step(s, m_ref, l_ref):
    m_prev = m_ref[...]
    m_new = jnp.maximum(m_prev, jnp.max(s, axis=1, keepdims=True))
    alpha = jnp.exp(m_prev - m_new)
    p = jnp.exp(s - m_new)
    l_ref[...] = alpha * l_ref[...] + jnp.sum(p, axis=1, keepdims=True)
    m_ref[...] = m_new
    return alpha, p


def _online_init(m_ref, l_ref, acc_ref):
    m_ref[...] = jnp.full(m_ref.shape, NEG, F32)
    l_ref[...] = jnp.zeros_like(l_ref)
    acc_ref[...] = jnp.zeros_like(acc_ref)


def _proj_kernel(*refs, row_segs, col_segs, n_gate, with_logf, col_from):
    it = iter(refs)
    x_ref, g_ref, cos_ref, sin_ref, cost_ref, sint_ref = [next(it) for _ in range(6)]
    w_ref = next(it) if row_segs else None
    wt_ref = next(it) if col_segs else None
    wg_ref = next(it) if n_gate else None
    if with_logf:
        wft_ref, bcol_ref, wfr_ref, brow_ref = next(it), next(it), next(it), next(it)
    outs = list(it)
    xn = _rmsnorm(x_ref[...], g_ref[...]).astype(BF16)
    k = 0
    for c0, width, rope, emits in row_segs:
        y = jnp.dot(xn, w_ref[:, c0:c0 + width], preferred_element_type=F32)
        if rope:
            y = _rope_rows(y, cos_ref[...], sin_ref[...])
        for dtype, scale in emits:
            outs[k][...] = (y if scale == 1.0 else y * scale).astype(dtype)
            k += 1

    def cols():
        kk = k
        for r0, height, rope, emits in col_segs:
            yt = _mm_nt(wt_ref[r0:r0 + height, :], xn)
            if rope == "sigmoid":
                yt = _sigmoid(yt)
            elif rope:
                yt = _rope_cols(yt, cost_ref[...], sint_ref[...])
            for dtype, scale in emits:
                outs[kk][0] = (yt if scale == 1.0 else yt * scale).astype(dtype)
                kk += 1

    if col_segs:
        if col_from:
            pl.when(pl.program_id(0) % col_from[0] >= col_from[1])(cols)
        else:
            cols()
        k += sum(len(e) for _, _, _, e in col_segs)
    if n_gate:
        outs[k][...] = _sigmoid(jnp.dot(xn, wg_ref[...], preferred_element_type=F32))
        k += 1
    if with_logf:
        yt = _mm_nt(wft_ref[...], xn)
        outs[k][0] = _log_sigmoid(yt[0:FOX_HEADS] + bcol_ref[...])
        outs[k + 1][...] = _log_sigmoid(jnp.dot(xn, wfr_ref[...], preferred_element_type=F32) + brow_ref[...])


def _project(x2d, gain, tables, n_pos_blocks, tm, w=None, row_segs=(), wt=None, col_segs=(),
             w_gate=None, logf=None, col_from=None, name="proj"):
    M, D = x2d.shape
    nt = M // tm
    n_seq = nt // n_pos_blocks
    cos_t, sin_t, cos_c, sin_c = tables
    pos_map = lambda i: (i % n_pos_blocks, 0)
    posc_map = lambda i: (0, i % n_pos_blocks)
    const = lambda a: pl.BlockSpec(a.shape, lambda i: (0,) * a.ndim)
    in_specs = [pl.BlockSpec((tm, D), lambda i: (i, 0)), const(gain.reshape(1, D)),
                pl.BlockSpec((tm, LANES), pos_map), pl.BlockSpec((tm, LANES), pos_map),
                pl.BlockSpec((HALF, tm), posc_map), pl.BlockSpec((HALF, tm), posc_map)]
    args = [x2d, gain.reshape(1, D), cos_t, sin_t, cos_c, sin_c]
    for a in (w, wt, w_gate):
        if a is not None:
            in_specs.append(const(a))
            args.append(a)
    out_shape, out_specs = [], []
    for _, wd, _, emits in row_segs:
        for dtype, _ in emits:
            out_shape.append(jax.ShapeDtypeStruct((M, wd), dtype))
            out_specs.append(pl.BlockSpec((tm, wd), lambda i: (i, 0)))
    first = col_from[1] if col_from else 0
    n_cb = n_pos_blocks - first
    col_map = lambda i: (i // n_pos_blocks, 0, jnp.maximum(i % n_pos_blocks - first, 0))
    for _, ht, _, emits in col_segs:
        for dtype, _ in emits:
            out_shape.append(jax.ShapeDtypeStruct((n_seq, ht, n_cb * tm), dtype))
            out_specs.append(pl.BlockSpec((1, ht, tm), col_map))
    n_gate = 0
    if w_gate is not None:
        n_gate = w_gate.shape[1]
        out_shape.append(jax.ShapeDtypeStruct((M, n_gate), F32))
        out_specs.append(pl.BlockSpec((tm, n_gate), lambda i: (i, 0)))
    if logf is not None:
        in_specs += [const(a) for a in logf]
        args += list(logf)
        out_shape.append(jax.ShapeDtypeStruct((n_seq, FOX_HEADS, n_pos_blocks * tm), F32))
        out_specs.append(pl.BlockSpec((1, FOX_HEADS, tm), lambda i: (i // n_pos_blocks, 0, i % n_pos_blocks)))
        out_shape.append(jax.ShapeDtypeStruct((M, LANES), F32))
        out_specs.append(pl.BlockSpec((tm, LANES), lambda i: (i, 0)))
    return pl.pallas_call(
        functools.partial(_proj_kernel, row_segs=tuple(row_segs), col_segs=tuple(col_segs), n_gate=n_gate,
                          with_logf=logf is not None, col_from=col_from),
        grid=(nt,), in_specs=in_specs, out_specs=out_specs, out_shape=out_shape, name=name,
        compiler_params=_cparams(("arbitrary",)))(*args)


def _fox_prep_kernel(k_ref, lf_ref, kaug_ref, c0_ref, carry_ref):
    j = pl.program_id(1)
    tc = k_ref.shape[1]

    @pl.when(j == 0)
    def _():
        carry_ref[...] = jnp.zeros_like(carry_ref)
        c0_ref[...] = jnp.zeros_like(c0_ref)

    lane = lax.broadcasted_iota(jnp.int32, (FOX_HEADS, LANES), 1)
    start = jnp.transpose(carry_ref[...])[0:FOX_HEADS, 0:1]
    c0_ref[0] = jnp.where(lane == j, start, c0_ref[0])

    r = lax.broadcasted_iota(jnp.int32, (tc, tc), 0)
    c = lax.broadcasted_iota(jnp.int32, (tc, tc), 1)
    tri = (c <= r).astype(BF16)
    hi, mid, lo = _split3(lf_ref[0])
    d = lambda p: jnp.dot(tri, p, preferred_element_type=F32)
    local = d(hi) + d(mid) + d(lo)
    carry_ref[...] = carry_ref[...] + jnp.broadcast_to(local[tc - 1:tc, :], carry_ref.shape)
    parts = jnp.concatenate(_split3(-local), axis=1)
    k = k_ref[0]
    kr = lax.broadcasted_iota(jnp.int32, (k.shape[1], LANES), 0)
    kc = lax.broadcasted_iota(jnp.int32, (k.shape[1], LANES), 1)
    pr = lax.broadcasted_iota(jnp.int32, (3 * LANES, LANES), 0)
    pc = lax.broadcasted_iota(jnp.int32, (3 * LANES, LANES), 1)
    for h in range(FOX_HEADS):
        place_k = ((kr == HEAD_DIM * h + kc) & (kc < HEAD_DIM)).astype(BF16)
        place_c = ((pr % LANES == h) & (pc == HEAD_DIM + pr // LANES)).astype(BF16)
        kaug_ref[0, h] = (jnp.dot(k, place_k, preferred_element_type=F32)
                          + jnp.dot(parts, place_c, preferred_element_type=F32)).astype(BF16)


def _fox_prep(k_b, lf_rows, tc):
    B, S, W = k_b.shape
    assert S // tc <= LANES
    return pl.pallas_call(
        _fox_prep_kernel, grid=(B, S // tc),
        in_specs=[pl.BlockSpec((1, tc, W), lambda b, j: (b, j, 0)),
                  pl.BlockSpec((1, tc, LANES), lambda b, j: (b, j, 0))],
        out_specs=[pl.BlockSpec((1, FOX_HEADS, tc, LANES), lambda b, j: (b, 0, j, 0)),
                   pl.BlockSpec((1, FOX_HEADS, LANES), lambda b, j: (b, 0, 0))],
        out_shape=[jax.ShapeDtypeStruct((B, FOX_HEADS, S, LANES), BF16),
                   jax.ShapeDtypeStruct((B, FOX_HEADS, LANES), F32)], name="fox_prep",
        scratch_shapes=[pltpu.VMEM((SUBLANES, LANES), F32)],
        compiler_params=_cparams(("parallel", "arbitrary")))(k_b, lf_rows)


def _fox_kernel(qi_ref, ki_ref, qt_ref, kaug_ref, vt_ref, c0_ref, o_ref, qa_ref, m_ref, l_ref, acc_ref, *, nh):
    hg, step = pl.program_id(1), pl.program_id(2)
    qi, ki = qi_ref[step], ki_ref[step]
    tq, tk = qt_ref.shape[2], kaug_ref.shape[2]
    ratio = tq // tk
    lane1 = lax.broadcasted_iota(jnp.int32, (1, LANES), 1)

    @pl.when(ki == 0)
    def _():
        row = lax.broadcasted_iota(jnp.int32, (HEAD_DIM, tq), 0)
        ones = jnp.where(row < 3, 1.0, 0.0).astype(BF16)
        for h in range(nh):
            qa_ref[h] = jnp.concatenate([qt_ref[0, h * HEAD_DIM:(h + 1) * HEAD_DIM, :], ones], axis=0)
        _online_init(m_ref, l_ref, acc_ref)

    def tile(masked):
        if masked:
            live = (ki * tk + lax.broadcasted_iota(jnp.int32, (tk, tq), 0)
                    <= qi * tq + lax.broadcasted_iota(jnp.int32, (tk, tq), 1))
        for h in range(nh):
            c0 = c0_ref[0, pl.ds(nh * hg + h, 1), :]
            delta = jnp.sum(jnp.where(lane1 == ratio * qi, c0, 0.0) - jnp.where(lane1 == ki, c0, 0.0),
                            axis=1, keepdims=True)
            s = jnp.dot(kaug_ref[0, h], qa_ref[h], preferred_element_type=F32)
            if masked:
                s = jnp.where(live, s, NEG)
            m_prev = m_ref[h]
            m_new = jnp.maximum(m_prev, jnp.max(s, axis=0, keepdims=True) + delta)
            p = jnp.exp(s - (m_new - delta))
            alpha = jnp.exp(m_prev - m_new)
            l_ref[h] = alpha * l_ref[h] + jnp.sum(p, axis=0, keepdims=True)
            acc_ref[h] = alpha * acc_ref[h] + jnp.dot(vt_ref[0, h * HEAD_DIM:(h + 1) * HEAD_DIM, :],
                                                      p.astype(BF16), preferred_element_type=F32)
            m_ref[h] = m_new

    pl.when(ki < ratio * qi)(lambda: tile(False))
    pl.when(ki >= ratio * qi)(lambda: tile(True))

    @pl.when(ki == ratio * (qi + 1) - 1)
    def _():
        for h in range(nh):
            o_ref[0, h * HEAD_DIM:(h + 1) * HEAD_DIM, :] = (acc_ref[h] / l_ref[h]).astype(BF16)


def _fox_prompt(qt_b, kaug, vt_b, c0, tk, nh=4):
    B, W, S = qt_b.shape
    tq = _tile(S, 2 * tk)
    ratio = tq // tk
    pairs = [(q, k) for q in range(S // tq) for k in range(ratio * (q + 1))]
    qi_tab = jnp.asarray([p[0] for p in pairs], jnp.int32)
    ki_tab = jnp.asarray([p[1] for p in pairs], jnp.int32)
    hw = nh * HEAD_DIM
    grid_spec = pltpu.PrefetchScalarGridSpec(
        num_scalar_prefetch=2, grid=(B, W // hw, len(pairs)),
        in_specs=[pl.BlockSpec((1, hw, tq), lambda b, hg, s, qi, ki: (b, hg, qi[s])),
                  pl.BlockSpec((1, nh, tk, LANES), lambda b, hg, s, qi, ki: (b, hg, ki[s], 0)),
                  pl.BlockSpec((1, hw, tk), lambda b, hg, s, qi, ki: (b, hg, ki[s])),
                  pl.BlockSpec((1, FOX_HEADS, LANES), lambda b, hg, s, qi, ki: (b, 0, 0))],
        out_specs=pl.BlockSpec((1, hw, tq), lambda b, hg, s, qi, ki: (b, hg, qi[s])),
        scratch_shapes=[pltpu.VMEM((nh, LANES, tq), BF16), pltpu.VMEM((nh, 1, tq), F32),
                        pltpu.VMEM((nh, 1, tq), F32), pltpu.VMEM((nh, HEAD_DIM, tq), F32)])
    return pl.pallas_call(
        functools.partial(_fox_kernel, nh=nh), grid_spec=grid_spec,
        out_shape=jax.ShapeDtypeStruct(qt_b.shape, BF16), name="fox_prompt",
        compiler_params=_cparams(("parallel", "parallel", "arbitrary")))(qi_tab, ki_tab, qt_b, kaug, vt_b, c0)


def _cmp_compute(x, pea_ref, peb_ref, wa_ref, wb_ref, w2t_ref, o_ref, carry_ref):
    n = x.shape[0]
    a = jnp.dot((x + pea_ref[...]).astype(BF16), wa_ref[...], preferred_element_type=F32)
    b = jnp.dot((x + peb_ref[...]).astype(BF16), wb_ref[...], preferred_element_type=F32)
    rowi = lax.broadcasted_iota(jnp.int32, a.shape, 0)
    a_prev = jnp.where(rowi == 0, carry_ref[0:1, :], pltpu.roll(a, 1, 0))
    carry_ref[...] = jnp.broadcast_to(a[n - 1:n, :], carry_ref.shape)
    o_ref[0] = _mm_nt(w2t_ref[...], _silu(a_prev + b))


def _cmp_kernel(x_ref, pea_ref, peb_ref, wa_ref, wb_ref, w2t_ref, o_ref, carry_ref):
    @pl.when(pl.program_id(1) == 0)
    def _():
        carry_ref[...] = jnp.zeros_like(carry_ref)
    _cmp_compute(x_ref[0], pea_ref, peb_ref, wa_ref, wb_ref, w2t_ref, o_ref, carry_ref)


def _cmp_paged_kernel(pt_ref, *refs, n_pg):
    pages = refs[:n_pg]
    pea_ref, peb_ref, wa_ref, wb_ref, w2t_ref, o_ref, xs_ref, carry_ref = refs[n_pg:]
    page = pages[0].shape[3]

    @pl.when(pl.program_id(1) == 0)
    def _():
        carry_ref[...] = jnp.zeros_like(carry_ref)

    for j, r in enumerate(pages):
        xs_ref[j * page:(j + 1) * page, :] = jnp.transpose(r[0].reshape(LANES, page))
    n = n_pg * page // NSA_CMP_STRIDE
    a = jnp.zeros((n, wa_ref.shape[2]), F32)
    b = jnp.zeros((n, wb_ref.shape[2]), F32)
    for l in range(NSA_CMP_STRIDE):
        xl = xs_ref[pl.ds(l, n, stride=NSA_CMP_STRIDE), :]
        a = a + jnp.dot((xl + pea_ref[l:l + 1, :]).astype(BF16), wa_ref[l], preferred_element_type=F32)
        b = b + jnp.dot((xl + peb_ref[l:l + 1, :]).astype(BF16), wb_ref[l], preferred_element_type=F32)
    rowi = lax.broadcasted_iota(jnp.int32, a.shape, 0)
    a_prev = jnp.where(rowi == 0, carry_ref[0:1, :], pltpu.roll(a, 1, 0))
    carry_ref[...] = jnp.broadcast_to(a[n - 1:n, :], carry_ref.shape)
    o_ref[0] = _mm_nt(w2t_ref[...], _silu(a_prev + b))


def _cmp_weights(pe, w1, w2):
    eye = jnp.eye(NSA_KV_HEADS, dtype=F32)
    hid = w1.shape[2]
    half = NSA_CMP_STRIDE

    def wpart(w):
        return jnp.einsum('lde,hg->lhdge', w, eye).reshape(half * LANES, NSA_KV_HEADS * hid).astype(BF16)

    def ppart(p):
        return jnp.broadcast_to(p[:, None, :], (half, NSA_KV_HEADS, HEAD_DIM)).reshape(1, half * LANES)

    w2t = jnp.einsum('ed,hg->gdhe', w2, eye).reshape(LANES, NSA_KV_HEADS * hid).astype(BF16)
    return ppart(pe[:half]), ppart(pe[half:]), wpart(w1[:half]), wpart(w1[half:]), w2t


def _compress(x, weights, name):
    N, L, _ = x.shape
    n_ch = L // NSA_CMP_STRIDE
    xc = x[:, :n_ch * NSA_CMP_STRIDE].reshape(N, n_ch, NSA_CMP_STRIDE * LANES)
    tch = _tile(n_ch, 256)
    wspecs = [pl.BlockSpec(w.shape, lambda n, j: (0, 0)) for w in weights]
    return pl.pallas_call(
        _cmp_kernel, grid=(N, n_ch // tch),
        in_specs=[pl.BlockSpec((1, tch, xc.shape[2]), lambda n, j: (n, j, 0))] + wspecs,
        out_specs=pl.BlockSpec((1, LANES, tch), lambda n, j: (n, 0, j)),
        out_shape=jax.ShapeDtypeStruct((N, LANES, n_ch), F32), name=name,
        scratch_shapes=[pltpu.VMEM((8, weights[2].shape[1]), F32)],
        compiler_params=_cparams(("parallel", "arbitrary")))(xc, *weights)


def _compress_paged(cache, pt_flat, n_db, weights, name):
    n_pool, page = cache.shape[:2]
    assert page == LANES
    rows = page // NSA_CMP_STRIDE
    ct = jnp.transpose(cache, (0, 2, 3, 1))
    n_pages = pt_flat.shape[0] // n_db
    n_pg = math.gcd(n_pages, max(1, 256 // rows))
    pea, peb, wa, wb, w2t = weights
    per_pos = lambda a: a.reshape(NSA_CMP_STRIDE, LANES, -1)
    weights = (pea.reshape(NSA_CMP_STRIDE, LANES), peb.reshape(NSA_CMP_STRIDE, LANES), per_pos(wa), per_pos(wb), w2t)
    wspecs = [pl.BlockSpec(w.shape, lambda b, p, pt, nd=w.ndim: (0,) * nd) for w in weights]
    page_spec = lambda j: pl.BlockSpec((1,) + ct.shape[1:],
                                       lambda b, p, pt: (pt[b * n_pages + p * n_pg + j], 0, 0, 0))
    grid_spec = pltpu.PrefetchScalarGridSpec(
        num_scalar_prefetch=1, grid=(n_db, n_pages // n_pg),
        in_specs=[page_spec(j) for j in range(n_pg)] + wspecs,
        out_specs=pl.BlockSpec((1, LANES, n_pg * rows), lambda b, p, pt: (b, 0, p)),
        scratch_shapes=[pltpu.VMEM((n_pg * page, LANES), F32), pltpu.VMEM((8, wa.shape[1]), F32)])
    return pl.pallas_call(
        functools.partial(_cmp_paged_kernel, n_pg=n_pg), grid_spec=grid_spec,
        out_shape=jax.ShapeDtypeStruct((n_db, LANES, n_pages * rows), F32), name=name,
        compiler_params=_cparams(("parallel", "arbitrary")))(pt_flat, *([ct] * n_pg), *weights)


def _slc_map_rows(n_ch, ns_pad):
    i = (np.arange(n_ch)[:, None] - 1) * NSA_CMP_STRIDE
    j = np.arange(ns_pad)[None, :] * NSA_SLC_BLOCK
    shared = np.minimum(i + NSA_CMP_LEN, j + NSA_SLC_BLOCK) - np.maximum(i, j)
    m = np.clip(shared, 0, None) / NSA_CMP_LEN
    m[0, :] = 0.0
    return m.astype(np.float32)


def _select_blocks(imp, qpos, ns, n_sel):
    blk = lax.broadcasted_iota(jnp.int32, imp.shape, 1)
    cur = qpos // NSA_SLC_BLOCK
    valid = blk * NSA_SLC_BLOCK <= qpos
    forced = (blk == 0) | (blk == cur) | (blk == cur - 1)
    score = jnp.where(valid, imp + jnp.where(forced, NSA_FORCE_BONUS, 0.0), NEG)
    rank = jnp.zeros(imp.shape, jnp.int32)
    for jp in range(ns):
        sj = score[:, jp:jp + 1]
        beats = (sj > score) | ((sj == score) & (blk > jp))
        rank = rank + beats.astype(jnp.int32)
    return rank < n_sel


def _select_blocks_cols(imp, qpos, ns, n_sel):
    blk = lax.broadcasted_iota(jnp.int32, imp.shape, 0)
    cur = qpos // NSA_SLC_BLOCK
    valid = blk * NSA_SLC_BLOCK <= qpos
    forced = (blk == 0) | (blk == cur) | (blk == cur - 1)
    score = jnp.where(valid, imp + jnp.where(forced, NSA_FORCE_BONUS, 0.0), NEG)
    rank = jnp.zeros(imp.shape, jnp.int32)
    for jp in range(ns):
        sj = score[jp:jp + 1, :]
        beats = (sj > score) | ((sj == score) & (blk > jp))
        rank = rank + beats.astype(jnp.int32)
    return rank < n_sel


def _softmax_cols(s, mask):
    sm = jnp.where(mask, s, NEG)
    m = jnp.max(sm, axis=0, keepdims=True)
    p = jnp.where(mask, jnp.exp(sm - m), 0.0)
    l = jnp.sum(p, axis=0, keepdims=True)
    return p / jnp.where(l > 0.0, l, 1.0)


def _nsa_kernel(qt_ref, kct_ref, vct_ref, ks_ref, vst_ref, kw_ref, vwt_ref, gt_ref, mapt_ref, o_ref,
                qaug_ref, negm_ref, m_ref, l_ref, acc_ref, *, ns, n_sel, tk, win):
    g, i = pl.program_id(1), pl.program_id(2)
    tq = qt_ref.shape[2]
    nl = NSA_GROUP * tq
    n_ch = kct_ref.shape[3]
    per_tile = tk // NSA_SLC_BLOCK
    st = i * tq
    qpos1 = st + lax.broadcasted_iota(jnp.int32, (1, tq), 1)
    rep = lambda a: jnp.concatenate([a] * NSA_GROUP, axis=1)
    qpos = rep(qpos1)

    q4t = jnp.concatenate([qt_ref[0, j * HEAD_DIM:(j + 1) * HEAD_DIM, :] for j in range(NSA_GROUP)], axis=1)
    row = lax.broadcasted_iota(jnp.int32, (LANES, nl), 0)
    qaug_ref[0:LANES, :] = jnp.where(row // HEAD_DIM == g, jnp.concatenate([q4t, q4t], axis=0),
                                     jnp.zeros((LANES, nl), BF16))
    qaug_ref[LANES:2 * LANES, :] = jnp.zeros((LANES, nl), BF16)

    cidx = lax.broadcasted_iota(jnp.int32, (n_ch, tq), 0)
    cmask = rep(((cidx - 1) * NSA_CMP_STRIDE + NSA_CMP_LEN - 1 <= qpos1) & (cidx >= 1))
    sc = lax.dot_general(kct_ref[0, 0].astype(BF16), q4t, (((0,), (0,)), ((), ())), preferred_element_type=F32)
    pc = _softmax_cols(sc, cmask)
    oc = _mm(vct_ref[0, 0], pc)
    pstack = jnp.concatenate([pc[:, j * tq:(j + 1) * tq] for j in range(NSA_GROUP)], axis=0)
    imp = jnp.dot(mapt_ref[...], pstack.astype(BF16), preferred_element_type=F32)[0:HEAD_DIM]
    sel = _select_blocks_cols(imp, qpos1, ns, n_sel)

    negm_ref[...] = rep(jnp.where(sel, 0.0, -MASK_BIG))
    _online_init(m_ref, l_ref, acc_ref)
    onehot = ((lax.broadcasted_iota(jnp.int32, (tk, LANES), 0) // NSA_SLC_BLOCK)
              == lax.broadcasted_iota(jnp.int32, (tk, LANES), 1)).astype(BF16)

    def tile(kt, diagonal):
        k0 = pl.multiple_of(kt * tk, tk)
        nm = negm_ref[pl.ds(pl.multiple_of(kt * per_tile, per_tile), per_tile), :]
        qaug_ref[LANES:LANES + 16, :] = jnp.concatenate(
            [nm, jnp.zeros((16 - per_tile, nl), F32)], axis=0).astype(BF16)
        kaug = jnp.concatenate([ks_ref[0, pl.ds(k0, tk), :], onehot], axis=1)
        s = jnp.dot(kaug, qaug_ref[...], preferred_element_type=F32)
        if diagonal:
            kpos = k0 + lax.broadcasted_iota(jnp.int32, (tk, nl), 0)
            s = jnp.where(kpos <= qpos, s, NEG)
        m_prev = m_ref[...]
        m_new = jnp.maximum(m_prev, jnp.max(s, axis=0, keepdims=True))
        alpha = jnp.exp(m_prev - m_new)
        p = jnp.exp(s - m_new)
        l_ref[...] = alpha * l_ref[...] + jnp.sum(p, axis=0, keepdims=True)
        acc_ref[...] = alpha * acc_ref[...] + jnp.dot(vst_ref[0, :, pl.ds(k0, tk)], p.astype(BF16),
                                                      preferred_element_type=F32)
        m_ref[...] = m_new

    last = st // tk

    def body(kt, carry):
        tile(kt, False)
        return carry

    lax.fori_loop(0, last, body, 0)
    tile(last, True)
    osl = acc_ref[...] / l_ref[...]

    w0 = pl.multiple_of(jnp.maximum(st + tq - win, 0), tq)
    dist = qpos1 - (w0 + lax.broadcasted_iota(jnp.int32, (win, tq), 0))
    sw = jnp.dot(kw_ref[0, pl.ds(w0, win), :], qaug_ref[0:LANES, :], preferred_element_type=F32)
    pw = _softmax_cols(sw, rep((dist >= 0) & (dist <= NSA_WINDOW)))
    ow = jnp.dot(vwt_ref[0, :, pl.ds(w0, win)], pw.astype(BF16), preferred_element_type=F32)

    for j in range(NSA_GROUP):
        base = (NSA_GROUP * g + j) * 3
        gate = lambda r: gt_ref[0, pl.ds(base + r, 1), :]
        cols = slice(j * tq, (j + 1) * tq)
        o = gate(0) * oc[:, cols] + gate(1) * osl[:, cols] + gate(2) * ow[:, cols]
        o_ref[0, j * HEAD_DIM:(j + 1) * HEAD_DIM, :] = o.astype(BF16)


def _nsa_prompt(qt_b, kct, vct, ks_b, vst_b, kw_b, vwt_b, gates_t):
    B, W, S = qt_b.shape
    n_ch = kct.shape[3]
    ns = -(-S // NSA_SLC_BLOCK)
    assert ns <= HEAD_DIM
    n_sel = min(NSA_TOPN, ns)
    tq = _tile(S, 256)
    tk = _tile(S, 512)
    assert tk // NSA_SLC_BLOCK <= 16
    win = min(NSA_WINDOW + tq, S)
    nl = NSA_GROUP * tq
    smap_t = jnp.asarray(np.tile(_slc_map_rows(n_ch, LANES).T, (1, NSA_GROUP)), dtype=BF16)
    per_head = lambda a: pl.BlockSpec((1, 1) + a.shape[2:], lambda b, g, i: (b, g, 0, 0))
    gw = W // NSA_KV_HEADS
    return pl.pallas_call(
        functools.partial(_nsa_kernel, ns=ns, n_sel=n_sel, tk=tk, win=win),
        grid=(B, NSA_KV_HEADS, S // tq),
        in_specs=[pl.BlockSpec((1, gw, tq), lambda b, g, i: (b, g, i)),
                  per_head(kct), per_head(vct),
                  pl.BlockSpec((1, S, LANES), lambda b, g, i: (b, 0, 0)),
                  pl.BlockSpec((1, HEAD_DIM, S), lambda b, g, i: (b, g, 0)),
                  pl.BlockSpec((1, S, LANES), lambda b, g, i: (b, 0, 0)),
                  pl.BlockSpec((1, HEAD_DIM, S), lambda b, g, i: (b, g, 0)),
                  pl.BlockSpec((1, gates_t.shape[1], tq), lambda b, g, i: (b, 0, i)),
                  pl.BlockSpec(smap_t.shape, lambda b, g, i: (0, 0))],
        out_specs=pl.BlockSpec((1, gw, tq), lambda b, g, i: (b, g, i)),
        out_shape=jax.ShapeDtypeStruct(qt_b.shape, BF16), name="nsa_prompt",
        scratch_shapes=[pltpu.VMEM((2 * LANES, nl), BF16), pltpu.VMEM((HEAD_DIM, nl), F32),
                        pltpu.VMEM((1, nl), F32), pltpu.VMEM((1, nl), F32), pltpu.VMEM((HEAD_DIM, nl), F32)],
        compiler_params=_cparams(("parallel", "parallel", "arbitrary")))(
            qt_b, kct, vct, ks_b, vst_b, kw_b, vwt_b, gates_t, smap_t)


def _outproj_kernel(*refs):
    x_ref, w_ref, y_ref = refs[0], refs[-2], refs[-1]
    y = x_ref[...]
    k0 = 0
    for o_ref in refs[1:-2]:
        if len(o_ref.shape) == 3:
            kw = o_ref.shape[1]
            y = y + lax.dot_general(o_ref[0].astype(BF16), w_ref[k0:k0 + kw, :], (((0,), (0,)), ((), ())),
                                    preferred_element_type=F32)
        else:
            kw = o_ref.shape[1]
            y = y + jnp.dot(o_ref[...].astype(BF16), w_ref[k0:k0 + kw, :], preferred_element_type=F32)
        k0 += kw
    y_ref[...] = y


def _outproj(x2d, parts, w_bf, name, tm=None):
    M, D = x2d.shape
    tm = tm or _tile(M, 512)
    specs = []
    for o in parts:
        if o.ndim == 3:
            npb = o.shape[2] // tm
            specs.append(pl.BlockSpec((1, o.shape[1], tm), lambda i, npb=npb: (i // npb, 0, i % npb)))
        else:
            specs.append(pl.BlockSpec((tm, o.shape[1]), lambda i: (i, 0)))
    return pl.pallas_call(
        _outproj_kernel, grid=(M // tm,),
        in_specs=[pl.BlockSpec((tm, D), lambda i: (i, 0))] + specs + [pl.BlockSpec(w_bf.shape, lambda i: (0, 0))],
        out_specs=pl.BlockSpec((tm, D), lambda i: (i, 0)),
        out_shape=jax.ShapeDtypeStruct((M, D), F32), name=name,
        compiler_params=_cparams(("parallel",)))(x2d, *parts, w_bf)


def _ffn_kernel(x_ref, g_ref, wg_ref, wu_ref, wd_ref, y_ref, xn_ref):
    f = pl.program_id(1)

    @pl.when(f == 0)
    def _():
        x = x_ref[...]
        xn_ref[...] = _rmsnorm(x, g_ref[...]).astype(BF16)
        y_ref[...] = x

    xn = xn_ref[...]
    h = _silu(jnp.dot(xn, wg_ref[...], preferred_element_type=F32)) * \
        jnp.dot(xn, wu_ref[...], preferred_element_type=F32)
    y_ref[...] += jnp.dot(h.astype(BF16), wd_ref[...], preferred_element_type=F32)


def _ffn(x2d, gain, wg, wu, wd, name):
    M, D = x2d.shape
    Fd = wg.shape[1]
    tm = _tile(M, 512)
    nf = 2 if Fd % (2 * LANES) == 0 else 1
    fc = Fd // nf
    return pl.pallas_call(
        _ffn_kernel, grid=(M // tm, nf),
        in_specs=[pl.BlockSpec((tm, D), lambda i, f: (i, 0)), pl.BlockSpec((1, D), lambda i, f: (0, 0)),
                  pl.BlockSpec((D, fc), lambda i, f: (0, f)), pl.BlockSpec((D, fc), lambda i, f: (0, f)),
                  pl.BlockSpec((fc, D), lambda i, f: (f, 0))],
        out_specs=pl.BlockSpec((tm, D), lambda i, f: (i, 0)),
        out_shape=jax.ShapeDtypeStruct((M, D), F32), name=name,
        scratch_shapes=[pltpu.VMEM((tm, D), BF16)],
        compiler_params=_cparams(("parallel", "arbitrary")))(x2d, gain.reshape(1, D), wg, wu, wd)


def _moe_kernel(x_ref, g_ref, wrt_ref, wg_ref, wu_ref, wd_ref, gf_ref, y_ref, xn_ref, rank_ref, comb_ref,
                rankc_ref, *, n_exp, chunk):
    e = pl.program_id(1)
    tm = x_ref.shape[0]

    @pl.when(e == 0)
    def _():
        xn = _rmsnorm(x_ref[...], g_ref[...])
        xn_ref[...] = xn.astype(BF16)
        xh, xm, _ = _split3(xn)
        wh, wm, _ = _split3(wrt_ref[...])
        dn = lambda a, b: lax.dot_general(a, b, (((1,), (1,)), ((), ())), preferred_element_type=F32)
        row = lax.broadcasted_iota(jnp.int32, (LANES, tm), 0)
        logits = jnp.where(row < n_exp, dn(wh, xh) + dn(wh, xm) + dn(wm, xh), NEG)
        v1 = jnp.max(logits, axis=0, keepdims=True)
        i1 = jnp.min(jnp.where(logits == v1, row, LANES), axis=0, keepdims=True)
        rest = jnp.where(row == i1, NEG, logits)
        v2 = jnp.max(rest, axis=0, keepdims=True)
        i2 = jnp.min(jnp.where(rest == v2, row, LANES), axis=0, keepdims=True)
        ex = jnp.exp(v2 - v1)
        comb = jnp.where(row == i1, 1.0 / (1.0 + ex), jnp.where(row == i2, ex / (1.0 + ex), 0.0))
        member = ((row == i1) | (row == i2))[0:SUBLANES]
        before = (lax.broadcasted_iota(jnp.int32, (tm, tm), 0)
                  < lax.broadcasted_iota(jnp.int32, (tm, tm), 1)).astype(BF16)
        rank = jnp.dot(jnp.where(member, 1.0, 0.0).astype(BF16), before, preferred_element_type=F32)
        rank = jnp.where(member, rank, -1.0)
        rank_ref[...] = rank
        comb_ref[...] = comb[0:SUBLANES]
        rankc_ref[...] = jnp.transpose(rank)
        y_ref[...] = jnp.zeros_like(y_ref)

    rrow = rank_ref[pl.ds(e, 1), :]
    crow = comb_ref[pl.ds(e, 1), :]
    rc = rankc_ref[...]
    rcol = jnp.sum(jnp.where(lax.broadcasted_iota(jnp.int32, rc.shape, 1) == e, rc, 0.0), axis=1, keepdims=True)
    n_tok = jnp.max(rrow).astype(jnp.int32) + 1

    def body(c, carry):
        base = (c * chunk).astype(F32)
        pick = rrow == lax.broadcasted_iota(jnp.int32, (chunk, tm), 0).astype(F32) + base
        xg = jnp.dot(jnp.where(pick, 1.0, 0.0).astype(BF16), xn_ref[...],
                     preferred_element_type=F32).astype(BF16)
        wcol = jnp.sum(jnp.where(pick, crow, 0.0), axis=1, keepdims=True)
        h = _silu(jnp.dot(xg, wg_ref[0], preferred_element_type=F32)) * \
            jnp.dot(xg, wu_ref[0], preferred_element_type=F32)
        yv = jnp.dot((h * wcol).astype(BF16), wd_ref[0], preferred_element_type=F32)
        place = jnp.where(rcol == lax.broadcasted_iota(jnp.int32, (tm, chunk), 1).astype(F32) + base,
                          1.0, 0.0).astype(BF16)
        y_ref[...] += jnp.dot(place, yv.astype(BF16), preferred_element_type=F32)
        return carry

    lax.fori_loop(0, (n_tok + chunk - 1) // chunk, body, 0)

    @pl.when(e == n_exp - 1)
    def _():
        y_ref[...] = _rmsnorm(x_ref[...] + y_ref[...], gf_ref[...])


def _moe_final(x2d, gain, w_router, wg, wu, wd, gain_final, name):
    M, D = x2d.shape
    n_exp, _, Fd = wg.shape
    assert n_exp <= SUBLANES
    tm = _tile(M, 1024)
    chunk = min(288, tm) if tm >= 1024 else min(128, tm)
    wrt = jnp.zeros((LANES, D), F32).at[:n_exp].set(w_router.T)
    return pl.pallas_call(
        functools.partial(_moe_kernel, n_exp=n_exp, chunk=chunk), grid=(M // tm, n_exp),
        in_specs=[pl.BlockSpec((tm, D), lambda i, e: (i, 0)), pl.BlockSpec((1, D), lambda i, e: (0, 0)),
                  pl.BlockSpec((LANES, D), lambda i, e: (0, 0)),
                  pl.BlockSpec((1, D, Fd), lambda i, e: (e, 0, 0)),
                  pl.BlockSpec((1, D, Fd), lambda i, e: (e, 0, 0)),
                  pl.BlockSpec((1, Fd, D), lambda i, e: (e, 0, 0)),
                  pl.BlockSpec((1, D), lambda i, e: (0, 0))],
        out_specs=pl.BlockSpec((tm, D), lambda i, e: (i, 0)),
        out_shape=jax.ShapeDtypeStruct((M, D), F32), name=name,
        scratch_shapes=[pltpu.VMEM((tm, D), BF16), pltpu.VMEM((SUBLANES, tm), F32),
                        pltpu.VMEM((SUBLANES, tm), F32), pltpu.VMEM((tm, SUBLANES), F32)],
        compiler_params=_cparams(("parallel", "arbitrary")))(
            x2d, gain.reshape(1, D), wrt, wg, wu, wd, gain_final.reshape(1, D))


def _band_kernel(*refs, n_hp, dil, span, has_prev):
    group = lambda i: refs[i * n_hp:(i + 1) * n_hp]
    if has_prev:
        q_refs, kp_refs, kc_refs, vp_refs, vc_refs = [group(i) for i in range(5)]
        bias_ref, o_ref, lse_ref, o_scr, lse_scr = refs[5 * n_hp:]
    else:
        q_refs, kc_refs, vc_refs = [group(i) for i in range(3)]
        bias_ref, o_ref, lse_ref, o_scr, lse_scr = refs[3 * n_hp:]
    t, hs = pl.program_id(1), pl.program_id(2)
    tq = q_refs[0].shape[1] // dil
    band = bias_ref[...]
    sw = band.shape[1]
    rowk = lax.broadcasted_iota(jnp.int32, band.shape, 0)
    band_first = band + jnp.where((rowk < span) & (t == 0), NEG, 0.0)
    row = lax.broadcasted_iota(jnp.int32, (LANES, sw), 0)
    first_head = hs * (2 * n_hp)

    @pl.when(hs == 0)
    def _():
        lse_ref[...] = jnp.zeros_like(lse_ref)

    def stream(r, carry):
        rows = lambda n: pl.ds(r, n, stride=dil) if dil > 1 else pl.ds(0, n)
        lses = []
        for j in range(n_hp):
            q = q_refs[j][0, rows(tq), :]
            qTs = [jnp.transpose(q[qs * sw:(qs + 1) * sw]).astype(BF16) for qs in range(tq // sw)]
            k, v = kc_refs[j][0, rows(tq), :], vc_refs[j][0, rows(tq), :]
            if has_prev:
                k = jnp.concatenate([kp_refs[j][0, rows(span), :], k], axis=0)
                v = jnp.concatenate([vp_refs[j][0, rows(span), :], v], axis=0)
            k = k.astype(BF16)
            vT = jnp.transpose(v).astype(BF16)
            outs = []
            for h in range(2):
                o_parts, lse_parts = [], []
                for qs in range(tq // sw):
                    qpad = jnp.where(row // HEAD_DIM == h, qTs[qs], jnp.zeros_like(qTs[qs]))
                    if has_prev:
                        k0, nk, b = qs * sw, span + sw, (band_first if qs == 0 else band)
                    elif qs == 0 and tq == sw:
                        k0, nk, b = 0, sw, band[span:, :]
                    elif qs == 0:
                        k0, nk, b = 0, span + sw, jnp.concatenate([band[span:, :], jnp.full((span, sw), NEG, F32)], axis=0)
                    else:
                        k0, nk, b = qs * sw - span, span + sw, band
                    s = jnp.dot(k[k0:k0 + nk], qpad, preferred_element_type=F32) + b
                    m = jnp.max(s, axis=0, keepdims=True)
                    p = jnp.exp(s - m)
                    l = jnp.sum(p, axis=0, keepdims=True)
                    o_parts.append(jnp.dot(vT[h * HEAD_DIM:(h + 1) * HEAD_DIM, k0:k0 + nk], p.astype(BF16),
                                           preferred_element_type=F32) / l)
                    lse_parts.append(m + jnp.log(l))
                outs.append(jnp.concatenate(o_parts, axis=1))
                lses.append(jnp.concatenate(lse_parts, axis=1))
            o_scr[j, rows(tq), :] = jnp.transpose(jnp.concatenate(outs, axis=0))
        stat = jnp.transpose(jnp.concatenate(lses + [jnp.zeros((LANES - len(lses), tq), F32)], axis=0))
        lse_scr[rows(tq), :] = pltpu.roll(stat, first_head, 1)
        return carry

    if dil == 1:
        stream(0, 0)
    else:
        def two_streams(r, carry):
            stream(r, carry)
            return stream(r + dil // 2, carry)
        lax.fori_loop(0, dil // 2, two_streams, 0)
    for j in range(n_hp):
        o_ref[0, :, j * LANES:(j + 1) * LANES] = o_scr[j]
    lane = lax.broadcasted_iota(jnp.int32, lse_scr.shape, 1)
    mine = (lane >= first_head) & (lane < first_head + 2 * n_hp)
    lse_ref[0] = jnp.where(mine, lse_scr[...], lse_ref[0])


def _band_attention(q, k, v, dil, span, tq, name, n_hp=2):
    B, S, W = q.shape
    L = S // dil
    tq = min(tq, L)
    assert L % tq == 0 and tq % span == 0
    per_tile = tq // span
    n_pairs = W // LANES
    assert n_pairs % n_hp == 0
    sw = tq
    i = np.arange(span + sw)[:, None]
    j = np.arange(sw)[None, :]
    bias = jnp.asarray(np.where((j - i + span >= 0) & (j - i + span <= span), 0.0, NEG), F32)
    cur = lambda jj: pl.BlockSpec((1, dil * tq, LANES), lambda b, t, hs: (b, t, hs * n_hp + jj))
    prev = lambda jj: pl.BlockSpec((1, dil * span, LANES),
                                   lambda b, t, hs: (b, jnp.maximum(t * per_tile - 1, 0), hs * n_hp + jj))
    each = lambda mk: [mk(jj) for jj in range(n_hp)]
    has_prev = L > tq
    if has_prev:
        specs = each(cur) + each(prev) + each(cur) + each(prev) + each(cur)
        args = [q] * n_hp + [k] * (2 * n_hp) + [v] * (2 * n_hp)
    else:
        specs = each(cur) * 3
        args = [q] * n_hp + [k] * n_hp + [v] * n_hp
    res = pl.pallas_call(
        functools.partial(_band_kernel, n_hp=n_hp, dil=dil, span=span, has_prev=has_prev),
        grid=(B, L // tq, n_pairs // n_hp),
        in_specs=specs + [pl.BlockSpec(bias.shape, lambda b, t, hs: (0, 0))],
        out_specs=[pl.BlockSpec((1, dil * tq, n_hp * LANES), lambda b, t, hs: (b, t, hs)),
                   pl.BlockSpec((1, dil * tq, LANES), lambda b, t, hs: (b, t, 0))],
        out_shape=[jax.ShapeDtypeStruct((B, S, W), F32), jax.ShapeDtypeStruct((B, S, LANES), F32)],
        scratch_shapes=[pltpu.VMEM((n_hp, dil * tq, LANES), F32), pltpu.VMEM((dil * tq, LANES), F32)],
        name=name, compiler_params=_cparams(("parallel", "parallel", "arbitrary")))(*args, bias)
    return res[0].reshape(B * S, W), res[1].reshape(B * S, LANES)


def _merge_outproj_kernel(*refs, n_groups):
    x_ref = refs[0]
    o_refs, l_refs = refs[1:1 + n_groups], refs[1 + n_groups:1 + 2 * n_groups]
    w_ref, y_ref = refs[-2], refs[-1]
    W = o_refs[0].shape[1]
    lses = [r[...] for r in l_refs]
    m = functools.reduce(jnp.maximum, lses)
    es = [jnp.exp(l - m) for l in lses]
    tot = functools.reduce(lambda a, b: a + b, es)
    expand = (lax.broadcasted_iota(jnp.int32, (LANES, W), 1) // HEAD_DIM
              == lax.broadcasted_iota(jnp.int32, (LANES, W), 0)).astype(BF16)
    mix = None
    for e, o_ref in zip(es, o_refs):
        hi, mid, _ = _split3(e / tot)
        wexp = jnp.dot(hi, expand, preferred_element_type=F32) + jnp.dot(mid, expand, preferred_element_type=F32)
        mix = wexp * o_ref[...] if mix is None else mix + wexp * o_ref[...]
    y_ref[...] = x_ref[...] + jnp.dot(mix.astype(BF16), w_ref[...], preferred_element_type=F32)


def _merge_outproj(x2d, outs, lses, w_bf, name):
    M, D = x2d.shape
    W = outs[0].shape[1]
    tm = _tile(M, 256)
    row = lambda n: pl.BlockSpec((tm, n), lambda i: (i, 0))
    return pl.pallas_call(
        functools.partial(_merge_outproj_kernel, n_groups=len(outs)), grid=(M // tm,),
        in_specs=[row(D)] + [row(W)] * len(outs) + [row(LANES)] * len(lses)
        + [pl.BlockSpec(w_bf.shape, lambda i: (0, 0))],
        out_specs=row(D), out_shape=jax.ShapeDtypeStruct((M, D), F32), name=name,
        compiler_params=_cparams(("parallel",)))(x2d, *outs, *lses, w_bf)


def _head_major(x, n_heads, pad_rows):
    DB, T, _ = x.shape
    y = x.reshape(DB, T, n_heads, HEAD_DIM).transpose(0, 2, 1, 3)
    return jnp.pad(y, ((0, 0), (0, 0), (0, pad_rows - T), (0, 0)))


def _new_cols(x, n_heads):
    DB, T, _ = x.shape
    y = x.reshape(DB, T, n_heads, HEAD_DIM).transpose(0, 2, 3, 1)
    return jnp.pad(y, ((0, 0), (0, 0), (0, 0), (0, NEW_PAD - T)))


def _row_of(col_vec_row):
    return jnp.transpose(jnp.broadcast_to(col_vec_row, (SUBLANES, LANES)))[0:SUBLANES, 0:1]


def _fox_dec_kernel(pt_ref, *refs, n_pg, n_q):
    kts, vts, lfs = refs[:n_pg], refs[n_pg:2 * n_pg], refs[2 * n_pg:3 * n_pg]
    q_ref, knt_ref, vnt_ref, lfn_ref, o_ref, qbd_ref, base_ref, carry_ref, m_ref, l_ref, acc_ref = refs[3 * n_pg:]
    p = pl.program_id(1)
    nh = FOX_HEADS
    page = LANES
    rows, width = nh * QPAD, nh * HEAD_DIM
    lane = lax.broadcasted_iota(jnp.int32, (QPAD, LANES), 1)
    trow = lax.broadcasted_iota(jnp.int32, (QPAD, LANES), 0)
    r = lax.broadcasted_iota(jnp.int32, (page, page), 0)
    c = lax.broadcasted_iota(jnp.int32, (page, page), 1)
    stack = lambda xs: jnp.concatenate(xs, axis=0)
    own = (lax.broadcasted_iota(jnp.int32, (rows, width), 0) // QPAD
           == lax.broadcasted_iota(jnp.int32, (rows, width), 1) // HEAD_DIM)
    heads_of = lambda ref: ref[0].reshape(width, ref.shape[3])

    @pl.when(p == 0)
    def _():
        _online_init(m_ref, l_ref, acc_ref)
        carry_ref[...] = jnp.zeros_like(carry_ref)
        q_all = q_ref[0].reshape(rows, HEAD_DIM) * SCALE
        qbd_ref[...] = jnp.where(own, jnp.concatenate([q_all] * nh, axis=1), 0.0).astype(BF16)
        cnew = _mm3_left(lfn_ref[0], r <= c)
        bases, decs = [], []
        for h in range(nh):
            bh = _row_of(cnew[h:h + 1, :])
            bases.append(bh)
            decs.append(bh - cnew[h:h + 1, :])
        base_ref[...] = jnp.broadcast_to(stack(bases), base_ref.shape)
        live = (lane <= trow) & (lane < n_q)
        s = jnp.dot(qbd_ref[...], heads_of(knt_ref).astype(BF16), preferred_element_type=F32) + stack(decs)
        s = jnp.where(stack([live] * nh), s, NEG)
        alpha, pr = _online_step(s, m_ref, l_ref)
        acc_ref[...] = alpha * acc_ref[...] + _mm_nt(pr, heads_of(vnt_ref))

    lf_all = stack([lf[0] for lf in lfs])
    after_all = _mm3_left(lf_all, r > c)
    tot_all = after_all[:, 0:1] + lf_all[:, 0:1]
    carry = carry_ref[:, 0:1]
    decay = [None] * n_pg
    for j in reversed(range(n_pg)):
        decay[j] = carry + after_all[j * nh:(j + 1) * nh]
        carry = carry + tot_all[j * nh:(j + 1) * nh]
    carry_ref[...] = jnp.broadcast_to(carry, carry_ref.shape)
    per_row = lambda d: stack([jnp.broadcast_to(d[h:h + 1, :], (QPAD, page)) for h in range(nh)])
    s = jnp.concatenate([jnp.dot(qbd_ref[...], heads_of(kts[j]).astype(BF16), preferred_element_type=F32)
                         + per_row(decay[j]) for j in range(n_pg)], axis=1) + base_ref[:, 0:1]
    alpha, pr = _online_step(s, m_ref, l_ref)
    acc_ref[...] = alpha * acc_ref[...] + sum(_mm_nt(pr[:, j * page:(j + 1) * page], heads_of(vts[j]))
                                              for j in range(n_pg))

    @pl.when(p == pl.num_programs(1) - 1)
    def _():
        acc = jnp.where(own, acc_ref[...], 0.0)
        o_ref[0] = sum(acc[:, h * HEAD_DIM:(h + 1) * HEAD_DIM] for h in range(nh)) / l_ref[...]


def _fox_decode(q, k_new, v_new, lf_new, cache_k, cache_v, cache_lf, pt_flat):
    DB, T, W = q.shape
    n_pool, page, nh, _ = cache_k.shape
    assert page == LANES and T <= QPAD
    n_pages = pt_flat.shape[0] // DB
    n_pg = math.gcd(n_pages, PAGES_PER_STEP)
    n_steps = n_pages // n_pg
    ckt = jnp.transpose(cache_k, (0, 2, 3, 1))
    cvt = jnp.transpose(cache_v, (0, 2, 3, 1))
    clf = jnp.transpose(cache_lf, (0, 2, 1))
    qh = _head_major(q, nh, QPAD)
    knt, vnt = _new_cols(k_new, nh), _new_cols(v_new, nh)
    lfn = jnp.pad(jnp.transpose(lf_new, (0, 2, 1)), ((0, 0), (0, 0), (0, LANES - T)))
    page_idx = lambda j: (lambda b, p, pt: (pt[b * n_pages + (n_steps - 1 - p) * n_pg + j], 0, 0, 0))
    lf_idx = lambda j: (lambda b, p, pt: (pt[b * n_pages + (n_steps - 1 - p) * n_pg + j], 0, 0))
    per_db = lambda a: pl.BlockSpec((1,) + a.shape[1:], lambda b, p, pt: (b,) + (0,) * (a.ndim - 1))
    rows = nh * QPAD
    grid_spec = pltpu.PrefetchScalarGridSpec(
        num_scalar_prefetch=1, grid=(DB, n_steps),
        in_specs=([pl.BlockSpec((1, nh, HEAD_DIM, page), page_idx(j)) for j in range(n_pg)] * 2
                  + [pl.BlockSpec((1, nh, page), lf_idx(j)) for j in range(n_pg)]
                  + [per_db(qh), per_db(knt), per_db(vnt), per_db(lfn)]),
        out_specs=pl.BlockSpec((1, rows, HEAD_DIM), lambda b, p, pt: (b, 0, 0)),
        scratch_shapes=[pltpu.VMEM((rows, W), BF16), pltpu.VMEM((rows, LANES), F32), pltpu.VMEM((nh, LANES), F32),
                        pltpu.VMEM((rows, 1), F32), pltpu.VMEM((rows, 1), F32), pltpu.VMEM((rows, W), F32)])
    o = pl.pallas_call(
        functools.partial(_fox_dec_kernel, n_pg=n_pg, n_q=T), grid_spec=grid_spec,
        out_shape=jax.ShapeDtypeStruct((DB, rows, HEAD_DIM), F32), name="fox_decode",
        compiler_params=_cparams(("parallel", "arbitrary")))(
            pt_flat, *([ckt] * n_pg), *([cvt] * n_pg), *([clf] * n_pg), qh, knt, vnt, lfn)
    return o.reshape(DB, nh, QPAD, HEAD_DIM)[:, :, :T].transpose(0, 2, 1, 3).reshape(DB, T, W)


def _nsa_dec_a_kernel(q_ref, kct_ref, vct_ref, kwt_ref, vwt_ref, kwnt_ref, vwnt_ref, gt_ref, map_ref,
                      o_ref, bias_ref, *, n_q, ns, n_sel, past, n_keys):
    nr = NSA_GROUP * QPAD
    n_ch = kct_ref.shape[3]
    wb = kwt_ref.shape[3]
    t_row = lax.broadcasted_iota(jnp.int32, (nr, 1), 0) % QPAD
    imps = []
    for kv in range(NSA_KV_HEADS):
        q = (q_ref[0, kv] * SCALE).astype(BF16)
        cidx = lax.broadcasted_iota(jnp.int32, (nr, n_ch), 1)
        pc = _softmax_rows(_mm(q, kct_ref[0, kv]), cidx >= 1)
        oc = _mm_nt(pc, vct_ref[0, kv])
        pcat = jnp.concatenate([pc[j * QPAD:(j + 1) * QPAD] for j in range(NSA_GROUP)], axis=1)
        imps.append(jnp.dot(pcat.astype(BF16), map_ref[...], preferred_element_type=F32))
        sw = _mm(q, kwt_ref[0, kv])
        sn = _mm(q, kwnt_ref[0, kv])
        iw = lax.broadcasted_iota(jnp.int32, (nr, wb), 1)
        un = lax.broadcasted_iota(jnp.int32, (nr, NEW_PAD), 1)
        mw = (wb + t_row - iw <= NSA_WINDOW) & (past - wb + iw >= 0)
        mn = (un <= t_row) & (un < n_q)
        sw = jnp.where(mw, sw, NEG)
        sn = jnp.where(mn, sn, NEG)
        m = jnp.maximum(jnp.max(sw, axis=1, keepdims=True), jnp.max(sn, axis=1, keepdims=True))
        pw = jnp.where(mw, jnp.exp(sw - m), 0.0)
        pn = jnp.where(mn, jnp.exp(sn - m), 0.0)
        l = jnp.sum(pw, axis=1, keepdims=True) + jnp.sum(pn, axis=1, keepdims=True)
        ow = (_mm_nt(pw, vwt_ref[0, kv]) + _mm_nt(pn, vwnt_ref[0, kv])) / l
        gt = gt_ref[0, kv]
        o_ref[0, kv] = gt[:, 0:1] * oc + gt[:, 2:3] * ow

    imp = jnp.concatenate(imps, axis=0)
    t_sel = lax.broadcasted_iota(jnp.int32, (imp.shape[0], 1), 0) % QPAD
    sel = _select_blocks(imp, past + t_sel, ns, n_sel).astype(BF16)
    ch = 8 * LANES
    for c0 in range(0, n_keys, ch):
        w = min(ch, n_keys - c0)
        key = c0 + lax.broadcasted_iota(jnp.int32, (sel.shape[1], w), 1)
        onehot = (key // NSA_SLC_BLOCK == lax.broadcasted_iota(jnp.int32, (sel.shape[1], w), 0)).astype(BF16)
        chosen = jnp.dot(sel, onehot, preferred_element_type=F32) > 0.5
        kpos = c0 + lax.broadcasted_iota(jnp.int32, (sel.shape[0], w), 1)
        ok = chosen & (kpos <= past + t_sel) & (kpos < past + n_q)
        bias_ref[0, :, c0:c0 + w] = jnp.where(ok, 0.0, NEG)


def _nsa_dec_b_kernel(pt_ref, *refs, n_pg):
    kts, vts = refs[:n_pg], refs[n_pg:2 * n_pg]
    q_ref, knt_ref, vnt_ref, bias_ref, biasn_ref, o_ref, qbd_ref, m_ref, l_ref, acc_ref = refs[2 * n_pg:]
    p = pl.program_id(1)
    nr = NSA_GROUP * QPAD
    page = LANES
    rows, width = NSA_KV_HEADS * nr, NSA_KV_HEADS * HEAD_DIM
    own = (lax.broadcasted_iota(jnp.int32, (rows, width), 0) // nr
           == lax.broadcasted_iota(jnp.int32, (rows, width), 1) // HEAD_DIM)
    heads_of = lambda ref: ref[0].reshape(width, ref.shape[3])
    per_row = lambda b: jnp.concatenate([b[kv * QPAD:(kv + 1) * QPAD] for kv in range(NSA_KV_HEADS)
                                         for _ in range(NSA_GROUP)], axis=0)

    @pl.when(p == 0)
    def _():
        _online_init(m_ref, l_ref, acc_ref)
        q_all = q_ref[0].reshape(rows, HEAD_DIM) * SCALE
        qbd_ref[...] = jnp.where(own, jnp.concatenate([q_all] * NSA_KV_HEADS, axis=1), 0.0).astype(BF16)

    s = jnp.concatenate([jnp.dot(qbd_ref[...], heads_of(kts[j]).astype(BF16), preferred_element_type=F32)
                         for j in range(n_pg)], axis=1) + per_row(bias_ref[0])
    alpha, pr = _online_step(s, m_ref, l_ref)
    acc_ref[...] = alpha * acc_ref[...] + sum(_mm_nt(pr[:, j * page:(j + 1) * page], heads_of(vts[j]))
                                              for j in range(n_pg))

    @pl.when(p == pl.num_programs(1) - 1)
    def _():
        s = jnp.dot(qbd_ref[...], heads_of(knt_ref).astype(BF16), preferred_element_type=F32) + per_row(biasn_ref[0])
        alpha, pr = _online_step(s, m_ref, l_ref)
        acc = jnp.where(own, alpha * acc_ref[...] + _mm_nt(pr, heads_of(vnt_ref)), 0.0)
        o_ref[0] = sum(acc[:, kv * HEAD_DIM:(kv + 1) * HEAD_DIM] for kv in range(NSA_KV_HEADS)) / l_ref[...]


def _nsa_decode(qb, kct, vct, cache_ks, cache_vs, ks_new, vs_new, swa_k, swa_v, kw_new, vw_new,
                gates, pt_flat, past):
    DB, T, W = qb.shape
    n_pool, page = cache_ks.shape[:2]
    assert page == LANES and past % LANES == 0 and T <= QPAD
    n_pages = pt_flat.shape[0] // DB
    n_ch = kct.shape[3]
    ns = -(-(past + T) // NSA_SLC_BLOCK)
    ns_pad = -(-ns // LANES) * LANES
    n_sel = min(NSA_TOPN, ns)
    nr = NSA_GROUP * QPAD
    n_keys = past + NEW_PAD
    q5 = _head_major(qb, NSA_HEADS, QPAD).reshape(DB, NSA_KV_HEADS, nr, HEAD_DIM)
    g5 = _head_major(jnp.pad(gates.reshape(DB, T, NSA_HEADS, 3), ((0, 0),) * 3 + ((0, HEAD_DIM - 3),))
                     .reshape(DB, T, NSA_HEADS * HEAD_DIM), NSA_HEADS, QPAD)[..., :3]
    g5 = g5.reshape(DB, NSA_KV_HEADS, nr, 3)
    smap = jnp.asarray(np.tile(_slc_map_rows(n_ch, ns_pad), (NSA_GROUP, 1)), dtype=BF16)
    kwt = jnp.transpose(swa_k, (0, 2, 3, 1))
    vwt = jnp.transpose(swa_v, (0, 2, 3, 1))
    kwnt, vwnt = _new_cols(kw_new, NSA_KV_HEADS), _new_cols(vw_new, NSA_KV_HEADS)
    per = lambda a: pl.BlockSpec((1,) + a.shape[1:], lambda b: (b,) + (0,) * (a.ndim - 1))
    part, bias = pl.pallas_call(
        functools.partial(_nsa_dec_a_kernel, n_q=T, ns=ns, n_sel=n_sel, past=past, n_keys=n_keys), grid=(DB,),
        in_specs=[per(q5), per(kct), per(vct), per(kwt), per(vwt), per(kwnt), per(vwnt), per(g5),
                  pl.BlockSpec(smap.shape, lambda b: (0, 0))],
        out_specs=[pl.BlockSpec((1, NSA_KV_HEADS, nr, HEAD_DIM), lambda b: (b, 0, 0, 0)),
                   pl.BlockSpec((1, NSA_KV_HEADS * QPAD, n_keys), lambda b: (b, 0, 0))],
        out_shape=[jax.ShapeDtypeStruct((DB, NSA_KV_HEADS, nr, HEAD_DIM), F32),
                   jax.ShapeDtypeStruct((DB, NSA_KV_HEADS * QPAD, n_keys), F32)], name="nsa_decode_a",
        compiler_params=_cparams(("parallel",)))(q5, kct, vct, kwt, vwt, kwnt, vwnt, g5, smap)

    n_pg = math.gcd(n_pages, 2 * PAGES_PER_STEP)
    ckt = jnp.transpose(cache_ks, (0, 2, 3, 1))
    cvt = jnp.transpose(cache_vs, (0, 2, 3, 1))
    ksnt, vsnt = _new_cols(ks_new, NSA_KV_HEADS), _new_cols(vs_new, NSA_KV_HEADS)
    page_idx = lambda j: (lambda b, p, pt: (pt[b * n_pages + p * n_pg + j], 0, 0, 0))
    per_db = lambda a: pl.BlockSpec((1,) + a.shape[1:], lambda b, p, pt: (b,) + (0,) * (a.ndim - 1))
    rows = NSA_KV_HEADS * nr
    grid_spec = pltpu.PrefetchScalarGridSpec(
        num_scalar_prefetch=1, grid=(DB, n_pages // n_pg),
        in_specs=([pl.BlockSpec((1, NSA_KV_HEADS, HEAD_DIM, page), page_idx(j)) for j in range(n_pg)] * 2
                  + [per_db(q5), per_db(ksnt), per_db(vsnt),
                     pl.BlockSpec((1, NSA_KV_HEADS * QPAD, n_pg * page), lambda b, p, pt: (b, 0, p)),
                     pl.BlockSpec((1, NSA_KV_HEADS * QPAD, NEW_PAD), lambda b, p, pt: (b, 0, past // NEW_PAD))]),
        out_specs=pl.BlockSpec((1, rows, HEAD_DIM), lambda b, p, pt: (b, 0, 0)),
        scratch_shapes=[pltpu.VMEM((rows, LANES), BF16), pltpu.VMEM((rows, 1), F32), pltpu.VMEM((rows, 1), F32),
                        pltpu.VMEM((rows, LANES), F32)])
    osl = pl.pallas_call(
        functools.partial(_nsa_dec_b_kernel, n_pg=n_pg), grid_spec=grid_spec,
        out_shape=jax.ShapeDtypeStruct((DB, rows, HEAD_DIM), F32), name="nsa_decode_b",
        compiler_params=_cparams(("parallel", "arbitrary")))(
            pt_flat, *([ckt] * n_pg), *([cvt] * n_pg), q5, ksnt, vsnt, bias, bias)

    o = part + g5[..., 1:2] * osl.reshape(DB, NSA_KV_HEADS, nr, HEAD_DIM)
    return o.reshape(DB, NSA_HEADS, QPAD, HEAD_DIM)[:, :, :T].transpose(0, 2, 1, 3).reshape(DB, T, W)


def _dil_dec_kernel(q_ref, kt_ref, vt_ref, knt_ref, vnt_ref, o_ref, *, n_q, wc):
    hb = kt_ref.shape[1]
    t = lax.broadcasted_iota(jnp.int32, (QPAD, 1), 0)

    def log_mult(d, ok):
        w = jnp.zeros(d.shape, F32)
        for window, dil in DIL_PAIRS:
            w = w + ((d >= 0) & (d <= window) & (d % dil == 0)).astype(F32)
        return jnp.where(ok, w, 0.0)

    wk = log_mult(wc + t - lax.broadcasted_iota(jnp.int32, (QPAD, wc), 1), t < n_q)
    un = lax.broadcasted_iota(jnp.int32, (QPAD, NEW_PAD), 1)
    wn = log_mult(t - un, (t < n_q) & (un < n_q))
    for h in range(hb):
        q = (q_ref[0, h] * SCALE).astype(BF16)
        sk = jnp.where(wk > 0.0, _mm(q, kt_ref[0, h]), NEG)
        sn = jnp.where(wn > 0.0, _mm(q, knt_ref[0, h]), NEG)
        m = jnp.maximum(jnp.max(sk, axis=1, keepdims=True), jnp.max(sn, axis=1, keepdims=True))
        pk = wk * jnp.exp(sk - m)
        pn = wn * jnp.exp(sn - m)
        l = jnp.sum(pk, axis=1, keepdims=True) + jnp.sum(pn, axis=1, keepdims=True)
        o_ref[0, h] = (_mm_nt(pk, vt_ref[0, h]) + _mm_nt(pn, vnt_ref[0, h])) / jnp.where(l > 0.0, l, 1.0)


def _dilated_decode(q, k_new, v_new, cache_k, cache_v):
    DB, T, W = q.shape
    wc, nh = cache_k.shape[1], cache_k.shape[2]
    ckt = jnp.transpose(cache_k, (0, 2, 3, 1))
    cvt = jnp.transpose(cache_v, (0, 2, 3, 1))
    qh = _head_major(q, nh, QPAD)
    knt, vnt = _new_cols(k_new, nh), _new_cols(v_new, nh)
    hb = math.gcd(nh, 4)
    spec = lambda a: pl.BlockSpec((1, hb) + a.shape[2:], lambda b, j: (b, j, 0, 0))
    o = pl.pallas_call(
        functools.partial(_dil_dec_kernel, n_q=T, wc=wc), grid=(DB, nh // hb),
        in_specs=[spec(qh), spec(ckt), spec(cvt), spec(knt), spec(vnt)],
        out_specs=pl.BlockSpec((1, hb, QPAD, HEAD_DIM), lambda b, j: (b, j, 0, 0)),
        out_shape=jax.ShapeDtypeStruct((DB, nh, QPAD, HEAD_DIM), F32), name="dilated_decode",
        compiler_params=_cparams(("parallel", "parallel")))(qh, ckt, cvt, knt, vnt)
    return o[:, :, :T].transpose(0, 2, 1, 3).reshape(DB, T, W)


def kernel(x_prompt, x_sample, cache_a_k, cache_a_v, cache_a_logf, cache_b_cmp_k, cache_b_cmp_v, cache_b_slc_k, cache_b_slc_v, cache_b_swa_k, cache_b_swa_v, cache_c_k, cache_c_v, page_table, norm_mix0, w_in0, fox_bf, nsa_pe_k, nsa_w1_k, nsa_w2_k, nsa_pe_v, nsa_w1_v, nsa_w2_v, w_out0, norm_ffn0, ffn_w_gate, ffn_w_up, ffn_w_down, norm_mix1, w_in1, w_out1, norm_ffn1, moe_router, moe_w_gate, moe_w_up, moe_w_down, norm_final):
    B, S, D = x_prompt.shape
    DB, T, _ = x_sample.shape
    n_pages = page_table.shape[1]
    past = n_pages * cache_a_k.shape[1]
    pt_flat = page_table.reshape(-1).astype(jnp.int32)
    fw = FOX_HEADS * HEAD_DIM
    nw = NSA_HEADS * HEAD_DIM
    kvw = NSA_KV_HEADS * HEAD_DIM

    cuts = np.cumsum([0, fw, fw, fw, FOX_HEADS, nw] + [kvw] * 6 + [3 * NSA_HEADS])
    col = lambda i: w_in0[:, cuts[i]:cuts[i + 1]]
    qa_w, ka_w, va_w, fa_w, qb_w, kc_w, vc_w, ks_w, vs_w, kw_w, vw_w, gb_w = [col(i) for i in range(12)]

    f32_out = ((F32, 1.0),)
    bf16_out = ((BF16, 1.0),)

    def pack(ws, ropes, emits=None):
        widths = [w.shape[1] for w in ws]
        starts = np.concatenate([[0], np.cumsum(widths)[:-1]])
        emits = emits or [f32_out] * len(ws)
        return (jnp.concatenate(ws, axis=1).astype(BF16),
                [(int(s), int(w), r, e) for s, w, r, e in zip(starts, widths, ropes, emits)])

    both_out = ((F32, 1.0), (BF16, 1.0))
    q_out = ((BF16, SCALE),)
    w0r, segs0r = pack([kc_w, vc_w, ka_w, ks_w, kw_w], [True, False, False, True, True],
                       [f32_out] * 2 + [bf16_out] * 3)
    w0c, segs0c = pack([qa_w, ka_w, va_w, qb_w, kc_w, ks_w, kw_w, vc_w, vs_w, vw_w, gb_w],
                       [False, False, False, True, True, True, True, False, False, False, "sigmoid"],
                       [q_out, f32_out, both_out, q_out, f32_out, f32_out, f32_out, f32_out, both_out, both_out,
                        f32_out])
    w0c = w0c.T
    w0s, segs0s = pack([qa_w, ka_w, va_w, qb_w, kc_w, ks_w, kw_w, vc_w, vs_w, vw_w],
                       [False, False, False, True, True, True, True, False, False, False])
    wgate = gb_w.astype(BF16)
    wft = jnp.zeros((16, D), F32).at[:FOX_HEADS].set(fa_w.T).astype(BF16)
    wfr = jnp.zeros((D, LANES), F32).at[:, :FOX_HEADS].set(fa_w).astype(BF16)
    brow = jnp.zeros((1, LANES), F32).at[0, :FOX_HEADS].set(fox_bf)
    logf_args = (wft, fox_bf.reshape(FOX_HEADS, 1).astype(F32), wfr, brow)
    dw = w_in1.shape[1] // 3
    w1 = w_in1.astype(BF16)
    segs1 = [(0, dw, True, f32_out), (dw, dw, True, f32_out), (2 * dw, dw, False, f32_out)]
    w1c = w1[:, dw:].T
    segs1c = [(0, dw, True, f32_out), (dw, dw, False, f32_out)]
    w_out0_b, w_out1_b = w_out0.astype(BF16), w_out1.astype(BF16)
    ffn_g, ffn_u, ffn_d = ffn_w_gate.astype(BF16), ffn_w_up.astype(BF16), ffn_w_down.astype(BF16)
    moe_g, moe_u, moe_d = moe_w_gate.astype(BF16), moe_w_up.astype(BF16), moe_w_down.astype(BF16)
    cmp_k_w = _cmp_weights(nsa_pe_k, nsa_w1_k, nsa_w2_k)
    cmp_v_w = _cmp_weights(nsa_pe_v, nsa_w1_v, nsa_w2_v)

    tab_p = _rope_tables(jnp.arange(S))
    tab_s = _rope_tables(past + jnp.arange(DB * T) % T)
    tm_p = _tile(S, 512)
    npb = S // tm_p
    win_b = min(NSA_WINDOW, S)
    win_c = min(DIL_WINDOW_MAX, S)

    xp = x_prompt.reshape(B * S, D)
    (kc, vc, ka_b, ks_b, kw_b, qat_b, kat, vat, vat_b, qbt_b, kct, kst, kwt, vct, vst, vst_b, vwt, vwt_b,
     gates_t, lft_p, lf_rows) = _project(
        xp, norm_mix0, tab_p, npb, tm_p, w=w0r, row_segs=segs0r, wt=w0c, col_segs=segs0c,
        logf=logf_args, name="proj0_prompt")
    r3 = lambda a: a.reshape(B, S, a.shape[-1])
    heads = lambda a, h: a.reshape(B, h, HEAD_DIM, a.shape[-1])
    kaug, c0 = _fox_prep(r3(ka_b), r3(lf_rows), tm_p)
    o_at = _fox_prompt(qat_b, kaug, vat_b, c0, tm_p)
    kcmp_p = _compress(r3(kc), cmp_k_w, "compress_k_prompt")
    vcmp_p = _compress(r3(vc), cmp_v_w, "compress_v_prompt")
    kv2 = lambda a: heads(a, NSA_KV_HEADS)
    o_bt = _nsa_prompt(qbt_b, kv2(kcmp_p), kv2(vcmp_p), r3(ks_b), vst_b, r3(kw_b), vwt_b, gates_t)
    hp = _outproj(xp, [o_at, o_bt], w_out0_b, "outproj0_prompt", tm=tm_p)
    hp = _ffn(hp, norm_ffn0, ffn_g, ffn_u, ffn_d, "ffn_prompt")

    xs = x_sample.reshape(DB * T, D)
    (qa_s, ka_s, va_s, qb_s, kc_s, ks_s, kw_s, vc_s, vs_s, vw_s, gates_s, lft_s, _) = _project(
        xs, norm_mix0, tab_s, 1, DB * T, w=w0s, row_segs=segs0s, w_gate=wgate, logf=logf_args,
        name="proj0_sample")
    s3 = lambda a: a.reshape(DB, T, a.shape[-1])
    lf_s = jnp.transpose(lft_s[0].reshape(FOX_HEADS, DB, T), (1, 2, 0))
    o_a_s = _fox_decode(s3(qa_s), s3(ka_s), s3(va_s), lf_s, cache_a_k, cache_a_v, cache_a_logf, pt_flat)
    kcmp_s = _compress_paged(cache_b_cmp_k, pt_flat, DB, cmp_k_w, "compress_k_paged")
    vcmp_s = _compress_paged(cache_b_cmp_v, pt_flat, DB, cmp_v_w, "compress_v_paged")
    kvs = lambda a: a.reshape(DB, NSA_KV_HEADS, HEAD_DIM, a.shape[-1])
    o_b_s = _nsa_decode(s3(qb_s), kvs(kcmp_s), kvs(vcmp_s), cache_b_slc_k, cache_b_slc_v, s3(ks_s), s3(vs_s),
                        cache_b_swa_k, cache_b_swa_v, s3(kw_s), s3(vw_s), s3(gates_s), pt_flat, past)
    hs = _outproj(xs, [o_a_s.reshape(DB * T, fw), o_b_s.reshape(DB * T, nw)], w_out0_b, "outproj0_sample")
    hs = _ffn(hs, norm_ffn0, ffn_g, ffn_u, ffn_d, "ffn_sample")

    first_c = (S - win_c) // tm_p
    segs1p = [(0, dw, True, ((F32, SCALE),)), (dw, dw, True, f32_out), (2 * dw, dw, False, f32_out)]
    q1, k1, v1, k1t, v1t = _project(hp, norm_mix1, tab_p, npb, tm_p, w=w1, row_segs=segs1p, wt=w1c,
                                    col_segs=segs1c, col_from=(npb, first_c), name="proj1_prompt")
    groups = [_band_attention(q1.reshape(B, S, dw), k1.reshape(B, S, dw), v1.reshape(B, S, dw), dil,
                              window // dil, 512 if dil == 1 else 256, "dilated_prompt_d%d" % dil,
                              n_hp=4 if dil < 16 else 2)
              for window, dil in DIL_PAIRS]
    hp = _merge_outproj(hp, [g[0] for g in groups], [g[1] for g in groups], w_out1_b, "outproj1_prompt")
    y_prompt = _moe_final(hp, norm_ffn1, moe_router, moe_g, moe_u, moe_d, norm_final, "moe_prompt").reshape(B, S, D)

    q1s, k1s, v1s = _project(hs, norm_mix1, tab_s, 1, DB * T, w=w1, row_segs=segs1, name="proj1_sample")
    o1s = _dilated_decode(s3(q1s), s3(k1s), s3(v1s), cache_c_k, cache_c_v)
    hs = _outproj(hs, [o1s.reshape(DB * T, dw)], w_out1_b, "outproj1_sample")
    y_sample = _moe_final(hs, norm_ffn1, moe_router, moe_g, moe_u, moe_d, norm_final, "moe_sample").reshape(DB, T, D)

    def state(a, h, last=None):
        a = a.reshape(a.shape[0], h, HEAD_DIM, a.shape[-1])
        if last is not None:
            a = a[..., a.shape[-1] - last:]
        return jnp.transpose(a, (0, 3, 1, 2))

    h4 = lambda a, h: a.reshape(DB, T, h, HEAD_DIM)
    nh1 = dw // HEAD_DIM
    return (y_prompt, y_sample,
            state(kat, FOX_HEADS), state(vat, FOX_HEADS), jnp.transpose(lft_p, (0, 2, 1)),
            state(kct, NSA_KV_HEADS), state(vct, NSA_KV_HEADS), state(kst, NSA_KV_HEADS), state(vst, NSA_KV_HEADS),
            state(kwt, NSA_KV_HEADS, win_b), state(vwt, NSA_KV_HEADS, win_b),
            state(k1t, nh1, win_c), state(v1t, nh1, win_c),
            h4(ka_s, FOX_HEADS), h4(va_s, FOX_HEADS), lf_s,
            h4(kc_s, NSA_KV_HEADS), h4(vc_s, NSA_KV_HEADS), h4(ks_s, NSA_KV_HEADS),
            h4(vs_s, NSA_KV_HEADS), h4(kw_s, NSA_KV_HEADS), h4(vw_s, NSA_KV_HEADS),
            h4(k1s, nh1), h4(v1s, nh1))
```

```python
import functools
import math

import numpy as np
import jax
import jax.numpy as jnp
from jax import lax
from jax.experimental import pallas as pl
from jax.experimental.pallas import tpu as pltpu

F32 = jnp.float32
BF16 = jnp.bfloat16

HEAD_DIM = 64
HALF = HEAD_DIM // 2
LANES = 128
SUBLANES = 8
ROPE_THETA = 10000.0
RMS_EPS = 1e-6
NEG = -1e30
MASK_BIG = 30000.0
SCALE = HEAD_DIM ** -0.5

FOX_HEADS = 8
NSA_HEADS = 8
NSA_KV_HEADS = 2
NSA_GROUP = NSA_HEADS // NSA_KV_HEADS
NSA_CMP_LEN = 32
NSA_CMP_STRIDE = 16
NSA_SLC_BLOCK = 64
NSA_TOPN = 16
NSA_WINDOW = 512
NSA_FORCE_BONUS = 1e3
DIL_PAIRS = ((128, 1), (512, 4), (2048, 16))
DIL_WINDOW_MAX = 2048
TOP_K = 2
QPAD = SUBLANES
NEW_PAD = LANES
PAGES_PER_STEP = 32

VMEM_LIMIT = 56 * 1024 * 1024


def _tile(n, pref):
    return pref if n % pref == 0 else n


def _cparams(sem):
    return pltpu.CompilerParams(dimension_semantics=sem, vmem_limit_bytes=VMEM_LIMIT)


def _mm(a, b):
    return jnp.dot(a.astype(BF16), b.astype(BF16), preferred_element_type=F32)


def _mm_nt(a, b):
    return lax.dot_general(a.astype(BF16), b.astype(BF16), (((1,), (1,)), ((), ())),
                           preferred_element_type=F32)


def _split3(x):
    hi = x.astype(BF16)
    r = x - hi.astype(F32)
    mid = r.astype(BF16)
    lo = (r - mid.astype(F32)).astype(BF16)
    return hi, mid, lo


def _mm3_left(x, exact_rhs):
    b = exact_rhs.astype(BF16)
    hi, mid, lo = _split3(x)
    d = lambda p: jnp.dot(p, b, preferred_element_type=F32)
    return d(hi) + d(mid) + d(lo)


def _sigmoid(z):
    return 1.0 / (1.0 + jnp.exp(-z))


def _silu(z):
    return z * _sigmoid(z)


def _log_sigmoid(z):
    return jnp.minimum(z, 0.0) - jnp.log1p(jnp.exp(-jnp.abs(z)))


def _rmsnorm(x, g):
    return x * lax.rsqrt(jnp.mean(x * x, axis=-1, keepdims=True) + RMS_EPS) * g


def _rope_rows(y, cos, sin_signed):
    n = y.shape[1]
    lane = lax.broadcasted_iota(jnp.int32, y.shape, 1)
    first = (lane % HEAD_DIM) < HALF
    rot = jnp.where(first, pltpu.roll(y, n - HALF, 1), pltpu.roll(y, HALF, 1))
    reps = n // LANES
    if reps > 1:
        cos = jnp.concatenate([cos] * reps, axis=1)
        sin_signed = jnp.concatenate([sin_signed] * reps, axis=1)
    return y * cos + rot * sin_signed


def _rope_cols(yt, cos_t, sin_t):
    out = []
    for h in range(yt.shape[0] // HEAD_DIM):
        a = yt[h * HEAD_DIM:h * HEAD_DIM + HALF]
        b = yt[h * HEAD_DIM + HALF:(h + 1) * HEAD_DIM]
        out += [a * cos_t - b * sin_t, b * cos_t + a * sin_t]
    return jnp.concatenate(out, axis=0)


def _rope_tables(pos):
    inv = jnp.exp(-math.log(ROPE_THETA) * jnp.arange(HALF, dtype=F32) / HALF)
    ang = pos.astype(F32)[:, None] * inv[None, :]
    cos, sin = jnp.cos(ang), jnp.sin(ang)
    return (jnp.concatenate([cos, cos, cos, cos], axis=1),
            jnp.concatenate([-sin, sin, -sin, sin], axis=1), cos.T, sin.T)


def _softmax_rows(s, mask):
    sm = jnp.where(mask, s, NEG)
    m = jnp.max(sm, axis=1, keepdims=True)
    p = jnp.where(mask, jnp.exp(sm - m), 0.0)
    l = jnp.sum(p, axis=1, keepdims=True)
    return p / jnp.where(l > 0.0, l, 1.0)


def _online_step(s, m_ref, l_ref):
    m_prev = m_ref[...]
    m_new = jnp.maximum(m_prev, jnp.max(s, axis=1, keepdims=True))
    alpha = jnp.exp(m_prev - m_new)
    p = jnp.exp(s - m_new)
    l_ref[...] = alpha * l_ref[...] + jnp.sum(p, axis=1, keepdims=True)
    m_ref[...] = m_new
    return alpha, p


def _online_init(m_ref, l_ref, acc_ref):
    m_ref[...] = jnp.full(m_ref.shape, NEG, F32)
    l_ref[...] = jnp.zeros_like(l_ref)
    acc_ref[...] = jnp.zeros_like(acc_ref)


def _proj_kernel(*refs, row_segs, col_segs, n_gate, with_logf, col_from):
    it = iter(refs)
    x_ref, g_ref, cos_ref, sin_ref, cost_ref, sint_ref = [next(it) for _ in range(6)]
    w_ref = next(it) if row_segs else None
    wt_ref = next(it) if col_segs else None
    wg_ref = next(it) if n_gate else None
    if with_logf:
        wft_ref, bcol_ref, wfr_ref, brow_ref = next(it), next(it), next(it), next(it)
    outs = list(it)
    xn = _rmsnorm(x_ref[...], g_ref[...]).astype(BF16)
    k = 0
    for c0, width, rope, emits in row_segs:
        y = jnp.dot(xn, w_ref[:, c0:c0 + width], preferred_element_type=F32)
        if rope:
            y = _rope_rows(y, cos_ref[...], sin_ref[...])
        for dtype, scale in emits:
            outs[k][...] = (y if scale == 1.0 else y * scale).astype(dtype)
            k += 1

    def cols():
        kk = k
        for r0, height, rope, emits in col_segs:
            yt = _mm_nt(wt_ref[r0:r0 + height, :], xn)
            if rope == "sigmoid":
                yt = _sigmoid(yt)
            elif rope:
                yt = _rope_cols(yt, cost_ref[...], sint_ref[...])
            for dtype, scale in emits:
                outs[kk][0] = (yt if scale == 1.0 else yt * scale).astype(dtype)
                kk += 1

    if col_segs:
        if col_from:
            pl.when(pl.program_id(0) % col_from[0] >= col_from[1])(cols)
        else:
            cols()
        k += sum(len(e) for _, _, _, e in col_segs)
    if n_gate:
        outs[k][...] = _sigmoid(jnp.dot(xn, wg_ref[...], preferred_element_type=F32))
        k += 1
    if with_logf:
        yt = _mm_nt(wft_ref[...], xn)
        outs[k][0] = _log_sigmoid(yt[0:FOX_HEADS] + bcol_ref[...])
        outs[k + 1][...] = _log_sigmoid(jnp.dot(xn, wfr_ref[...], preferred_element_type=F32) + brow_ref[...])


def _project(x2d, gain, tables, n_pos_blocks, tm, w=None, row_segs=(), wt=None, col_segs=(),
             w_gate=None, logf=None, col_from=None, name="proj"):
    M, D = x2d.shape
    nt = M // tm
    n_seq = nt // n_pos_blocks
    cos_t, sin_t, cos_c, sin_c = tables
    pos_map = lambda i: (i % n_pos_blocks, 0)
    posc_map = lambda i: (0, i % n_pos_blocks)
    const = lambda a: pl.BlockSpec(a.shape, lambda i: (0,) * a.ndim)
    in_specs = [pl.BlockSpec((tm, D), lambda i: (i, 0)), const(gain.reshape(1, D)),
                pl.BlockSpec((tm, LANES), pos_map), pl.BlockSpec((tm, LANES), pos_map),
                pl.BlockSpec((HALF, tm), posc_map), pl.BlockSpec((HALF, tm), posc_map)]
    args = [x2d, gain.reshape(1, D), cos_t, sin_t, cos_c, sin_c]
    for a in (w, wt, w_gate):
        if a is not None:
            in_specs.append(const(a))
            args.append(a)
    out_shape, out_specs = [], []
    for _, wd, _, emits in row_segs:
        for dtype, _ in emits:
            out_shape.append(jax.ShapeDtypeStruct((M, wd), dtype))
            out_specs.append(pl.BlockSpec((tm, wd), lambda i: (i, 0)))
    first = col_from[1] if col_from else 0
    n_cb = n_pos_blocks - first
    col_map = lambda i: (i // n_pos_blocks, 0, jnp.maximum(i % n_pos_blocks - first, 0))
    for _, ht, _, emits in col_segs:
        for dtype, _ in emits:
            out_shape.append(jax.ShapeDtypeStruct((n_seq, ht, n_cb * tm), dtype))
            out_specs.append(pl.BlockSpec((1, ht, tm), col_map))
    n_gate = 0
    if w_gate is not None:
        n_gate = w_gate.shape[1]
        out_shape.append(jax.ShapeDtypeStruct((M, n_gate), F32))
        out_specs.append(pl.BlockSpec((tm, n_gate), lambda i: (i, 0)))
    if logf is not None:
        in_specs += [const(a) for a in logf]
        args += list(logf)
        out_shape.append(jax.ShapeDtypeStruct((n_seq, FOX_HEADS, n_pos_blocks * tm), F32))
        out_specs.append(pl.BlockSpec((1, FOX_HEADS, tm), lambda i: (i // n_pos_blocks, 0, i % n_pos_blocks)))
        out_shape.append(jax.ShapeDtypeStruct((M, LANES), F32))
        out_specs.append(pl.BlockSpec((tm, LANES), lambda i: (i, 0)))
    return pl.pallas_call(
        functools.partial(_proj_kernel, row_segs=tuple(row_segs), col_segs=tuple(col_segs), n_gate=n_gate,
                          with_logf=logf is not None, col_from=col_from),
        grid=(nt,), in_specs=in_specs, out_specs=out_specs, out_shape=out_shape, name=name,
        compiler_params=_cparams(("arbitrary",)))(*args)


def _fox_prep_kernel(k_ref, lf_ref, kaug_ref, c0_ref, carry_ref):
    j = pl.program_id(1)
    tc = k_ref.shape[1]

    @pl.when(j == 0)
    def _():
        carry_ref[...] = jnp.zeros_like(carry_ref)
        c0_ref[...] = jnp.zeros_like(c0_ref)

    lane = lax.broadcasted_iota(jnp.int32, (FOX_HEADS, LANES), 1)
    start = jnp.transpose(carry_ref[...])[0:FOX_HEADS, 0:1]
    c0_ref[0] = jnp.where(lane == j, start, c0_ref[0])

    r = lax.broadcasted_iota(jnp.int32, (tc, tc), 0)
    c = lax.broadcasted_iota(jnp.int32, (tc, tc), 1)
    tri = (c <= r).astype(BF16)
    hi, mid, lo = _split3(lf_ref[0])
    d = lambda p: jnp.dot(tri, p, preferred_element_type=F32)
    local = d(hi) + d(mid) + d(lo)
    carry_ref[...] = carry_ref[...] + jnp.broadcast_to(local[tc - 1:tc, :], carry_ref.shape)
    parts = jnp.concatenate(_split3(-local), axis=1)
    k = k_ref[0]
    kr = lax.broadcasted_iota(jnp.int32, (k.shape[1], LANES), 0)
    kc = lax.broadcasted_iota(jnp.int32, (k.shape[1], LANES), 1)
    pr = lax.broadcasted_iota(jnp.int32, (3 * LANES, LANES), 0)
    pc = lax.broadcasted_iota(jnp.int32, (3 * LANES, LANES), 1)
    for h in range(FOX_HEADS):
        place_k = ((kr == HEAD_DIM * h + kc) & (kc < HEAD_DIM)).astype(BF16)
        place_c = ((pr % LANES == h) & (pc == HEAD_DIM + pr // LANES)).astype(BF16)
        kaug_ref[0, h] = (jnp.dot(k, place_k, preferred_element_type=F32)
                          + jnp.dot(parts, place_c, preferred_element_type=F32)).astype(BF16)


def _fox_prep(k_b, lf_rows, tc):
    B, S, W = k_b.shape
    assert S // tc <= LANES
    return pl.pallas_call(
        _fox_prep_kernel, grid=(B, S // tc),
        in_specs=[pl.BlockSpec((1, tc, W), lambda b, j: (b, j, 0)),
                  pl.BlockSpec((1, tc, LANES), lambda b, j: (b, j, 0))],
        out_specs=[pl.BlockSpec((1, FOX_HEADS, tc, LANES), lambda b, j: (b, 0, j, 0)),
                   pl.BlockSpec((1, FOX_HEADS, LANES), lambda b, j: (b, 0, 0))],
        out_shape=[jax.ShapeDtypeStruct((B, FOX_HEADS, S, LANES), BF16),
                   jax.ShapeDtypeStruct((B, FOX_HEADS, LANES), F32)], name="fox_prep",
        scratch_shapes=[pltpu.VMEM((SUBLANES, LANES), F32)],
        compiler_params=_cparams(("parallel", "arbitrary")))(k_b, lf_rows)


def _fox_kernel(qi_ref, ki_ref, qt_ref, kaug_ref, vt_ref, c0_ref, o_ref, qa_ref, m_ref, l_ref, acc_ref, *, nh):
    hg, step = pl.program_id(1), pl.program_id(2)
    qi, ki = qi_ref[step], ki_ref[step]
    tq, tk = qt_ref.shape[2], kaug_ref.shape[2]
    ratio = tq // tk
    lane1 = lax.broadcasted_iota(jnp.int32, (1, LANES), 1)

    @pl.when(ki == 0)
    def _():
        row = lax.broadcasted_iota(jnp.int32, (HEAD_DIM, tq), 0)
        ones = jnp.where(row < 3, 1.0, 0.0).astype(BF16)
        for h in range(nh):
            qa_ref[h] = jnp.concatenate([qt_ref[0, h * HEAD_DIM:(h + 1) * HEAD_DIM, :], ones], axis=0)
        _online_init(m_ref, l_ref, acc_ref)

    def tile(masked):
        if masked:
            live = (ki * tk + lax.broadcasted_iota(jnp.int32, (tk, tq), 0)
                    <= qi * tq + lax.broadcasted_iota(jnp.int32, (tk, tq), 1))
        for h in range(nh):
            c0 = c0_ref[0, pl.ds(nh * hg + h, 1), :]
            delta = jnp.sum(jnp.where(lane1 == ratio * qi, c0, 0.0) - jnp.where(lane1 == ki, c0, 0.0),
                            axis=1, keepdims=True)
            s = jnp.dot(kaug_ref[0, h], qa_ref[h], preferred_element_type=F32)
            if masked:
                s = jnp.where(live, s, NEG)
            m_prev = m_ref[h]
            m_new = jnp.maximum(m_prev, jnp.max(s, axis=0, keepdims=True) + delta)
            p = jnp.exp(s - (m_new - delta))
            alpha = jnp.exp(m_prev - m_new)
            l_ref[h] = alpha * l_ref[h] + jnp.sum(p, axis=0, keepdims=True)
            acc_ref[h] = alpha * acc_ref[h] + jnp.dot(vt_ref[0, h * HEAD_DIM:(h + 1) * HEAD_DIM, :],
                                                      p.astype(BF16), preferred_element_type=F32)
            m_ref[h] = m_new

    pl.when(ki < ratio * qi)(lambda: tile(False))
    pl.when(ki >= ratio * qi)(lambda: tile(True))

    @pl.when(ki == ratio * (qi + 1) - 1)
    def _():
        for h in range(nh):
            o_ref[0, h * HEAD_DIM:(h + 1) * HEAD_DIM, :] = (acc_ref[h] / l_ref[h]).astype(BF16)


def _fox_prompt(qt_b, kaug, vt_b, c0, tk, nh=4):
    B, W, S = qt_b.shape
    tq = _tile(S, 2 * tk)
    ratio = tq // tk
    pairs = [(q, k) for q in range(S // tq) for k in range(ratio * (q + 1))]
    qi_tab = jnp.asarray([p[0] for p in pairs], jnp.int32)
    ki_tab = jnp.asarray([p[1] for p in pairs], jnp.int32)
    hw = nh * HEAD_DIM
    grid_spec = pltpu.PrefetchScalarGridSpec(
        num_scalar_prefetch=2, grid=(B, W // hw, len(pairs)),
        in_specs=[pl.BlockSpec((1, hw, tq), lambda b, hg, s, qi, ki: (b, hg, qi[s])),
                  pl.BlockSpec((1, nh, tk, LANES), lambda b, hg, s, qi, ki: (b, hg, ki[s], 0)),
                  pl.BlockSpec((1, hw, tk), lambda b, hg, s, qi, ki: (b, hg, ki[s])),
                  pl.BlockSpec((1, FOX_HEADS, LANES), lambda b, hg, s, qi, ki: (b, 0, 0))],
        out_specs=pl.BlockSpec((1, hw, tq), lambda b, hg, s, qi, ki: (b, hg, qi[s])),
        scratch_shapes=[pltpu.VMEM((nh, LANES, tq), BF16), pltpu.VMEM((nh, 1, tq), F32),
                        pltpu.VMEM((nh, 1, tq), F32), pltpu.VMEM((nh, HEAD_DIM, tq), F32)])
    return pl.pallas_call(
        functools.partial(_fox_kernel, nh=nh), grid_spec=grid_spec,
        out_shape=jax.ShapeDtypeStruct(qt_b.shape, BF16), name="fox_prompt",
        compiler_params=_cparams(("parallel", "parallel", "arbitrary")))(qi_tab, ki_tab, qt_b, kaug, vt_b, c0)


def _cmp_compute(x, pea_ref, peb_ref, wa_ref, wb_ref, w2t_ref, o_ref, carry_ref):
    n = x.shape[0]
    a = jnp.dot((x + pea_ref[...]).astype(BF16), wa_ref[...], preferred_element_type=F32)
    b = jnp.dot((x + peb_ref[...]).astype(BF16), wb_ref[...], preferred_element_type=F32)
    rowi = lax.broadcasted_iota(jnp.int32, a.shape, 0)
    a_prev = jnp.where(rowi == 0, carry_ref[0:1, :], pltpu.roll(a, 1, 0))
    carry_ref[...] = jnp.broadcast_to(a[n - 1:n, :], carry_ref.shape)
    o_ref[0] = _mm_nt(w2t_ref[...], _silu(a_prev + b))


def _cmp_kernel(x_ref, pea_ref, peb_ref, wa_ref, wb_ref, w2t_ref, o_ref, carry_ref):
    @pl.when(pl.program_id(1) == 0)
    def _():
        carry_ref[...] = jnp.zeros_like(carry_ref)
    _cmp_compute(x_ref[0], pea_ref, peb_ref, wa_ref, wb_ref, w2t_ref, o_ref, carry_ref)


def _cmp_paged_kernel(pt_ref, *refs, n_pg):
    pages = refs[:n_pg]
    pea_ref, peb_ref, wa_ref, wb_ref, w2t_ref, o_ref, xs_ref, carry_ref = refs[n_pg:]
    page = pages[0].shape[3]

    @pl.when(pl.program_id(1) == 0)
    def _():
        carry_ref[...] = jnp.zeros_like(carry_ref)

    for j, r in enumerate(pages):
        xs_ref[j * page:(j + 1) * page, :] = jnp.transpose(r[0].reshape(LANES, page))
    n = n_pg * page // NSA_CMP_STRIDE
    x = jnp.concatenate([xs_ref[pl.ds(l, n, stride=NSA_CMP_STRIDE), :] for l in range(NSA_CMP_STRIDE)], axis=1)
    a = jnp.dot((x + pea_ref[...]).astype(BF16), wa_ref[...], preferred_element_type=F32)
    b = jnp.dot((x + peb_ref[...]).astype(BF16), wb_ref[...], preferred_element_type=F32)
    rowi = lax.broadcasted_iota(jnp.int32, a.shape, 0)
    a_prev = jnp.where(rowi == 0, carry_ref[0:1, :], pltpu.roll(a, 1, 0))
    carry_ref[...] = jnp.broadcast_to(a[n - 1:n, :], carry_ref.shape)
    o_ref[0] = _mm_nt(w2t_ref[...], _silu(a_prev + b))


def _cmp_weights(pe, w1, w2):
    eye = jnp.eye(NSA_KV_HEADS, dtype=F32)
    hid = w1.shape[2]
    half = NSA_CMP_STRIDE

    def wpart(w):
        return jnp.einsum('lde,hg->lhdge', w, eye).reshape(half * LANES, NSA_KV_HEADS * hid).astype(BF16)

    def ppart(p):
        return jnp.broadcast_to(p[:, None, :], (half, NSA_KV_HEADS, HEAD_DIM)).reshape(1, half * LANES)

    w2t = jnp.einsum('ed,hg->gdhe', w2, eye).reshape(LANES, NSA_KV_HEADS * hid).astype(BF16)
    return ppart(pe[:half]), ppart(pe[half:]), wpart(w1[:half]), wpart(w1[half:]), w2t


def _compress(x, weights, name):
    N, L, _ = x.shape
    n_ch = L // NSA_CMP_STRIDE
    xc = x[:, :n_ch * NSA_CMP_STRIDE].reshape(N, n_ch, NSA_CMP_STRIDE * LANES)
    tch = _tile(n_ch, 256)
    wspecs = [pl.BlockSpec(w.shape, lambda n, j: (0, 0)) for w in weights]
    return pl.pallas_call(
        _cmp_kernel, grid=(N, n_ch // tch),
        in_specs=[pl.BlockSpec((1, tch, xc.shape[2]), lambda n, j: (n, j, 0))] + wspecs,
        out_specs=pl.BlockSpec((1, LANES, tch), lambda n, j: (n, 0, j)),
        out_shape=jax.ShapeDtypeStruct((N, LANES, n_ch), F32), name=name,
        scratch_shapes=[pltpu.VMEM((8, weights[2].shape[1]), F32)],
        compiler_params=_cparams(("parallel", "arbitrary")))(xc, *weights)


def _compress_paged(cache, pt_flat, n_db, weights, name):
    n_pool, page = cache.shape[:2]
    assert page == LANES
    rows = page // NSA_CMP_STRIDE
    ct = jnp.transpose(cache, (0, 2, 3, 1))
    n_pages = pt_flat.shape[0] // n_db
    n_pg = math.gcd(n_pages, max(1, 256 // rows))
    wa = weights[2]
    wspecs = [pl.BlockSpec(w.shape, lambda b, p, pt, nd=w.ndim: (0,) * nd) for w in weights]
    page_spec = lambda j: pl.BlockSpec((1,) + ct.shape[1:],
                                       lambda b, p, pt: (pt[b * n_pages + p * n_pg + j], 0, 0, 0))
    grid_spec = pltpu.PrefetchScalarGridSpec(
        num_scalar_prefetch=1, grid=(n_db, n_pages // n_pg),
        in_specs=[page_spec(j) for j in range(n_pg)] + wspecs,
        out_specs=pl.BlockSpec((1, LANES, n_pg * rows), lambda b, p, pt: (b, 0, p)),
        scratch_shapes=[pltpu.VMEM((n_pg * page, LANES), F32), pltpu.VMEM((8, wa.shape[1]), F32)])
    return pl.pallas_call(
        functools.partial(_cmp_paged_kernel, n_pg=n_pg), grid_spec=grid_spec,
        out_shape=jax.ShapeDtypeStruct((n_db, LANES, n_pages * rows), F32), name=name,
        compiler_params=_cparams(("parallel", "arbitrary")))(pt_flat, *([ct] * n_pg), *weights)


def _slc_map_rows(n_ch, ns_pad):
    i = (np.arange(n_ch)[:, None] - 1) * NSA_CMP_STRIDE
    j = np.arange(ns_pad)[None, :] * NSA_SLC_BLOCK
    shared = np.minimum(i + NSA_CMP_LEN, j + NSA_SLC_BLOCK) - np.maximum(i, j)
    m = np.clip(shared, 0, None) / NSA_CMP_LEN
    m[0, :] = 0.0
    return m.astype(np.float32)


def _select_blocks(imp, qpos, ns, n_sel):
    blk = lax.broadcasted_iota(jnp.int32, imp.shape, 1)
    cur = qpos // NSA_SLC_BLOCK
    valid = blk * NSA_SLC_BLOCK <= qpos
    forced = (blk == 0) | (blk == cur) | (blk == cur - 1)
    score = jnp.where(valid, imp + jnp.where(forced, NSA_FORCE_BONUS, 0.0), NEG)
    rank = jnp.zeros(imp.shape, jnp.int32)
    for jp in range(ns):
        sj = score[:, jp:jp + 1]
        beats = (sj > score) | ((sj == score) & (blk > jp))
        rank = rank + beats.astype(jnp.int32)
    return rank < n_sel


def _select_blocks_cols(imp, qpos, ns, n_sel):
    blk = lax.broadcasted_iota(jnp.int32, imp.shape, 0)
    cur = qpos // NSA_SLC_BLOCK
    valid = blk * NSA_SLC_BLOCK <= qpos
    forced = (blk == 0) | (blk == cur) | (blk == cur - 1)
    score = jnp.where(valid, imp + jnp.where(forced, NSA_FORCE_BONUS, 0.0), NEG)
    rank = jnp.zeros(imp.shape, jnp.int32)
    for jp in range(ns):
        sj = score[jp:jp + 1, :]
        beats = (sj > score) | ((sj == score) & (blk > jp))
        rank = rank + beats.astype(jnp.int32)
    return rank < n_sel


def _softmax_cols(s, mask):
    sm = jnp.where(mask, s, NEG)
    m = jnp.max(sm, axis=0, keepdims=True)
    p = jnp.where(mask, jnp.exp(sm - m), 0.0)
    l = jnp.sum(p, axis=0, keepdims=True)
    return p / jnp.where(l > 0.0, l, 1.0)


def _nsa_kernel(qt_ref, kct_ref, vct_ref, ks_ref, vst_ref, kw_ref, vwt_ref, gt_ref, mapt_ref, o_ref,
                qaug_ref, negm_ref, m_ref, l_ref, acc_ref, *, ns, n_sel, tk, win):
    g, i = pl.program_id(1), pl.program_id(2)
    tq = qt_ref.shape[2]
    nl = NSA_GROUP * tq
    n_ch = kct_ref.shape[3]
    per_tile = tk // NSA_SLC_BLOCK
    st = i * tq
    qpos1 = st + lax.broadcasted_iota(jnp.int32, (1, tq), 1)
    rep = lambda a: jnp.concatenate([a] * NSA_GROUP, axis=1)
    qpos = rep(qpos1)

    q4t = jnp.concatenate([qt_ref[0, j * HEAD_DIM:(j + 1) * HEAD_DIM, :] for j in range(NSA_GROUP)], axis=1)
    row = lax.broadcasted_iota(jnp.int32, (LANES, nl), 0)
    qaug_ref[0:LANES, :] = jnp.where(row // HEAD_DIM == g, jnp.concatenate([q4t, q4t], axis=0),
                                     jnp.zeros((LANES, nl), BF16))
    qaug_ref[LANES:2 * LANES, :] = jnp.zeros((LANES, nl), BF16)

    cidx = lax.broadcasted_iota(jnp.int32, (n_ch, tq), 0)
    cmask = rep(((cidx - 1) * NSA_CMP_STRIDE + NSA_CMP_LEN - 1 <= qpos1) & (cidx >= 1))
    sc = lax.dot_general(kct_ref[0, 0].astype(BF16), q4t, (((0,), (0,)), ((), ())), preferred_element_type=F32)
    pc = _softmax_cols(sc, cmask)
    oc = _mm(vct_ref[0, 0], pc)
    pstack = jnp.concatenate([pc[:, j * tq:(j + 1) * tq] for j in range(NSA_GROUP)], axis=0)
    imp = jnp.dot(mapt_ref[...], pstack.astype(BF16), preferred_element_type=F32)[0:HEAD_DIM]
    sel = _select_blocks_cols(imp, qpos1, ns, n_sel)

    negm_ref[...] = rep(jnp.where(sel, 0.0, -MASK_BIG))
    _online_init(m_ref, l_ref, acc_ref)
    onehot = ((lax.broadcasted_iota(jnp.int32, (tk, LANES), 0) // NSA_SLC_BLOCK)
              == lax.broadcasted_iota(jnp.int32, (tk, LANES), 1)).astype(BF16)

    def tile(kt, diagonal):
        k0 = pl.multiple_of(kt * tk, tk)
        nm = negm_ref[pl.ds(pl.multiple_of(kt * per_tile, per_tile), per_tile), :]
        qaug_ref[LANES:LANES + 16, :] = jnp.concatenate(
            [nm, jnp.zeros((16 - per_tile, nl), F32)], axis=0).astype(BF16)
        kaug = jnp.concatenate([ks_ref[0, pl.ds(k0, tk), :], onehot], axis=1)
        s = jnp.dot(kaug, qaug_ref[...], preferred_element_type=F32)
        if diagonal:
            kpos = k0 + lax.broadcasted_iota(jnp.int32, (tk, nl), 0)
            s = jnp.where(kpos <= qpos, s, NEG)
        m_prev = m_ref[...]
        m_new = jnp.maximum(m_prev, jnp.max(s, axis=0, keepdims=True))
        alpha = jnp.exp(m_prev - m_new)
        p = jnp.exp(s - m_new)
        l_ref[...] = alpha * l_ref[...] + jnp.sum(p, axis=0, keepdims=True)
        acc_ref[...] = alpha * acc_ref[...] + jnp.dot(vst_ref[0, :, pl.ds(k0, tk)], p.astype(BF16),
                                                      preferred_element_type=F32)
        m_ref[...] = m_new

    last = st // tk

    def body(kt, carry):
        tile(kt, False)
        return carry

    lax.fori_loop(0, last, body, 0)
    tile(last, True)
    osl = acc_ref[...] / l_ref[...]

    w0 = pl.multiple_of(jnp.maximum(st + tq - win, 0), tq)
    dist = qpos1 - (w0 + lax.broadcasted_iota(jnp.int32, (win, tq), 0))
    sw = jnp.dot(kw_ref[0, pl.ds(w0, win), :], qaug_ref[0:LANES, :], preferred_element_type=F32)
    pw = _softmax_cols(sw, rep((dist >= 0) & (dist <= NSA_WINDOW)))
    ow = jnp.dot(vwt_ref[0, :, pl.ds(w0, win)], pw.astype(BF16), preferred_element_type=F32)

    for j in range(NSA_GROUP):
        base = (NSA_GROUP * g + j) * 3
        gate = lambda r: gt_ref[0, pl.ds(base + r, 1), :]
        cols = slice(j * tq, (j + 1) * tq)
        o = gate(0) * oc[:, cols] + gate(1) * osl[:, cols] + gate(2) * ow[:, cols]
        o_ref[0, j * HEAD_DIM:(j + 1) * HEAD_DIM, :] = o.astype(BF16)


def _nsa_prompt(qt_b, kct, vct, ks_b, vst_b, kw_b, vwt_b, gates_t):
    B, W, S = qt_b.shape
    n_ch = kct.shape[3]
    ns = -(-S // NSA_SLC_BLOCK)
    assert ns <= HEAD_DIM
    n_sel = min(NSA_TOPN, ns)
    tq = _tile(S, 256)
    tk = _tile(S, 512)
    assert tk // NSA_SLC_BLOCK <= 16
    win = min(NSA_WINDOW + tq, S)
    nl = NSA_GROUP * tq
    smap_t = jnp.asarray(np.tile(_slc_map_rows(n_ch, LANES).T, (1, NSA_GROUP)), dtype=BF16)
    per_head = lambda a: pl.BlockSpec((1, 1) + a.shape[2:], lambda b, g, i: (b, g, 0, 0))
    gw = W // NSA_KV_HEADS
    return pl.pallas_call(
        functools.partial(_nsa_kernel, ns=ns, n_sel=n_sel, tk=tk, win=win),
        grid=(B, NSA_KV_HEADS, S // tq),
        in_specs=[pl.BlockSpec((1, gw, tq), lambda b, g, i: (b, g, i)),
                  per_head(kct), per_head(vct),
                  pl.BlockSpec((1, S, LANES), lambda b, g, i: (b, 0, 0)),
                  pl.BlockSpec((1, HEAD_DIM, S), lambda b, g, i: (b, g, 0)),
                  pl.BlockSpec((1, S, LANES), lambda b, g, i: (b, 0, 0)),
                  pl.BlockSpec((1, HEAD_DIM, S), lambda b, g, i: (b, g, 0)),
                  pl.BlockSpec((1, gates_t.shape[1], tq), lambda b, g, i: (b, 0, i)),
                  pl.BlockSpec(smap_t.shape, lambda b, g, i: (0, 0))],
        out_specs=pl.BlockSpec((1, gw, tq), lambda b, g, i: (b, g, i)),
        out_shape=jax.ShapeDtypeStruct(qt_b.shape, BF16), name="nsa_prompt",
        scratch_shapes=[pltpu.VMEM((2 * LANES, nl), BF16), pltpu.VMEM((HEAD_DIM, nl), F32),
                        pltpu.VMEM((1, nl), F32), pltpu.VMEM((1, nl), F32), pltpu.VMEM((HEAD_DIM, nl), F32)],
        compiler_params=_cparams(("parallel", "parallel", "arbitrary")))(
            qt_b, kct, vct, ks_b, vst_b, kw_b, vwt_b, gates_t, smap_t)


def _outproj_kernel(*refs):
    x_ref, w_ref, y_ref = refs[0], refs[-2], refs[-1]
    y = x_ref[...]
    k0 = 0
    for o_ref in refs[1:-2]:
        if len(o_ref.shape) == 3:
            kw = o_ref.shape[1]
            y = y + lax.dot_general(o_ref[0].astype(BF16), w_ref[k0:k0 + kw, :], (((0,), (0,)), ((), ())),
                                    preferred_element_type=F32)
        else:
            kw = o_ref.shape[1]
            y = y + jnp.dot(o_ref[...].astype(BF16), w_ref[k0:k0 + kw, :], preferred_element_type=F32)
        k0 += kw
    y_ref[...] = y


def _outproj(x2d, parts, w_bf, name, tm=None):
    M, D = x2d.shape
    tm = tm or _tile(M, 512)
    specs = []
    for o in parts:
        if o.ndim == 3:
            npb = o.shape[2] // tm
            specs.append(pl.BlockSpec((1, o.shape[1], tm), lambda i, npb=npb: (i // npb, 0, i % npb)))
        else:
            specs.append(pl.BlockSpec((tm, o.shape[1]), lambda i: (i, 0)))
    return pl.pallas_call(
        _outproj_kernel, grid=(M // tm,),
        in_specs=[pl.BlockSpec((tm, D), lambda i: (i, 0))] + specs + [pl.BlockSpec(w_bf.shape, lambda i: (0, 0))],
        out_specs=pl.BlockSpec((tm, D), lambda i: (i, 0)),
        out_shape=jax.ShapeDtypeStruct((M, D), F32), name=name,
        compiler_params=_cparams(("parallel",)))(x2d, *parts, w_bf)


def _ffn_kernel(x_ref, g_ref, wg_ref, wu_ref, wd_ref, y_ref, xn_ref):
    f = pl.program_id(1)

    @pl.when(f == 0)
    def _():
        x = x_ref[...]
        xn_ref[...] = _rmsnorm(x, g_ref[...]).astype(BF16)
        y_ref[...] = x

    xn = xn_ref[...]
    h = _silu(jnp.dot(xn, wg_ref[...], preferred_element_type=F32)) * \
        jnp.dot(xn, wu_ref[...], preferred_element_type=F32)
    y_ref[...] += jnp.dot(h.astype(BF16), wd_ref[...], preferred_element_type=F32)


def _ffn(x2d, gain, wg, wu, wd, name):
    M, D = x2d.shape
    Fd = wg.shape[1]
    tm = _tile(M, 512)
    nf = 2 if Fd % (2 * LANES) == 0 else 1
    fc = Fd // nf
    return pl.pallas_call(
        _ffn_kernel, grid=(M // tm, nf),
        in_specs=[pl.BlockSpec((tm, D), lambda i, f: (i, 0)), pl.BlockSpec((1, D), lambda i, f: (0, 0)),
                  pl.BlockSpec((D, fc), lambda i, f: (0, f)), pl.BlockSpec((D, fc), lambda i, f: (0, f)),
                  pl.BlockSpec((fc, D), lambda i, f: (f, 0))],
        out_specs=pl.BlockSpec((tm, D), lambda i, f: (i, 0)),
        out_shape=jax.ShapeDtypeStruct((M, D), F32), name=name,
        scratch_shapes=[pltpu.VMEM((tm, D), BF16)],
        compiler_params=_cparams(("parallel", "arbitrary")))(x2d, gain.reshape(1, D), wg, wu, wd)


def _moe_kernel(x_ref, g_ref, wrt_ref, wg_ref, wu_ref, wd_ref, gf_ref, y_ref, xn_ref, rank_ref, comb_ref,
                rankc_ref, *, n_exp, chunk):
    e = pl.program_id(1)
    tm = x_ref.shape[0]

    @pl.when(e == 0)
    def _():
        xn = _rmsnorm(x_ref[...], g_ref[...])
        xn_ref[...] = xn.astype(BF16)
        xh, xm, _ = _split3(xn)
        wh, wm, _ = _split3(wrt_ref[...])
        dn = lambda a, b: lax.dot_general(a, b, (((1,), (1,)), ((), ())), preferred_element_type=F32)
        row = lax.broadcasted_iota(jnp.int32, (LANES, tm), 0)
        logits = jnp.where(row < n_exp, dn(wh, xh) + dn(wh, xm) + dn(wm, xh), NEG)
        v1 = jnp.max(logits, axis=0, keepdims=True)
        i1 = jnp.min(jnp.where(logits == v1, row, LANES), axis=0, keepdims=True)
        rest = jnp.where(row == i1, NEG, logits)
        v2 = jnp.max(rest, axis=0, keepdims=True)
        i2 = jnp.min(jnp.where(rest == v2, row, LANES), axis=0, keepdims=True)
        ex = jnp.exp(v2 - v1)
        comb = jnp.where(row == i1, 1.0 / (1.0 + ex), jnp.where(row == i2, ex / (1.0 + ex), 0.0))
        member = ((row == i1) | (row == i2))[0:SUBLANES]
        before = (lax.broadcasted_iota(jnp.int32, (tm, tm), 0)
                  < lax.broadcasted_iota(jnp.int32, (tm, tm), 1)).astype(BF16)
        rank = jnp.dot(jnp.where(member, 1.0, 0.0).astype(BF16), before, preferred_element_type=F32)
        rank = jnp.where(member, rank, -1.0)
        rank_ref[...] = rank
        comb_ref[...] = comb[0:SUBLANES]
        rankc_ref[...] = jnp.transpose(rank)
        y_ref[...] = jnp.zeros_like(y_ref)

    rrow = rank_ref[pl.ds(e, 1), :]
    crow = comb_ref[pl.ds(e, 1), :]
    rc = rankc_ref[...]
    rcol = jnp.sum(jnp.where(lax.broadcasted_iota(jnp.int32, rc.shape, 1) == e, rc, 0.0), axis=1, keepdims=True)
    n_tok = jnp.max(rrow).astype(jnp.int32) + 1

    def body(c, carry):
        base = (c * chunk).astype(F32)
        pick = rrow == lax.broadcasted_iota(jnp.int32, (chunk, tm), 0).astype(F32) + base
        xg = jnp.dot(jnp.where(pick, 1.0, 0.0).astype(BF16), xn_ref[...],
                     preferred_element_type=F32).astype(BF16)
        wcol = jnp.sum(jnp.where(pick, crow, 0.0), axis=1, keepdims=True)
        h = _silu(jnp.dot(xg, wg_ref[0], preferred_element_type=F32)) * \
            jnp.dot(xg, wu_ref[0], preferred_element_type=F32)
        yv = jnp.dot((h * wcol).astype(BF16), wd_ref[0], preferred_element_type=F32)
        place = jnp.where(rcol == lax.broadcasted_iota(jnp.int32, (tm, chunk), 1).astype(F32) + base,
                          1.0, 0.0).astype(BF16)
        y_ref[...] += jnp.dot(place, yv.astype(BF16), preferred_element_type=F32)
        return carry

    lax.fori_loop(0, (n_tok + chunk - 1) // chunk, body, 0)

    @pl.when(e == n_exp - 1)
    def _():
        y_ref[...] = _rmsnorm(x_ref[...] + y_ref[...], gf_ref[...])


def _moe_final(x2d, gain, w_router, wg, wu, wd, gain_final, name):
    M, D = x2d.shape
    n_exp, _, Fd = wg.shape
    assert n_exp <= SUBLANES
    tm = _tile(M, 1024)
    chunk = min(288, tm) if tm >= 1024 else min(128, tm)
    wrt = jnp.zeros((LANES, D), F32).at[:n_exp].set(w_router.T)
    return pl.pallas_call(
        functools.partial(_moe_kernel, n_exp=n_exp, chunk=chunk), grid=(M // tm, n_exp),
        in_specs=[pl.BlockSpec((tm, D), lambda i, e: (i, 0)), pl.BlockSpec((1, D), lambda i, e: (0, 0)),
                  pl.BlockSpec((LANES, D), lambda i, e: (0, 0)),
                  pl.BlockSpec((1, D, Fd), lambda i, e: (e, 0, 0)),
                  pl.BlockSpec((1, D, Fd), lambda i, e: (e, 0, 0)),
                  pl.BlockSpec((1, Fd, D), lambda i, e: (e, 0, 0)),
                  pl.BlockSpec((1, D), lambda i, e: (0, 0))],
        out_specs=pl.BlockSpec((tm, D), lambda i, e: (i, 0)),
        out_shape=jax.ShapeDtypeStruct((M, D), F32), name=name,
        scratch_shapes=[pltpu.VMEM((tm, D), BF16), pltpu.VMEM((SUBLANES, tm), F32),
                        pltpu.VMEM((SUBLANES, tm), F32), pltpu.VMEM((tm, SUBLANES), F32)],
        compiler_params=_cparams(("parallel", "arbitrary")))(
            x2d, gain.reshape(1, D), wrt, wg, wu, wd, gain_final.reshape(1, D))


def _band_kernel(*refs, n_hp, dil, span, has_prev):
    group = lambda i: refs[i * n_hp:(i + 1) * n_hp]
    if has_prev:
        q_refs, kp_refs, kc_refs, vp_refs, vc_refs = [group(i) for i in range(5)]
        bias_ref, o_ref, lse_ref, o_scr, lse_scr = refs[5 * n_hp:]
    else:
        q_refs, kc_refs, vc_refs = [group(i) for i in range(3)]
        bias_ref, o_ref, lse_ref, o_scr, lse_scr = refs[3 * n_hp:]
    t, hs = pl.program_id(1), pl.program_id(2)
    tq = q_refs[0].shape[1] // dil
    band = bias_ref[...]
    sw = band.shape[1]
    rowk = lax.broadcasted_iota(jnp.int32, band.shape, 0)
    band_first = band + jnp.where((rowk < span) & (t == 0), NEG, 0.0)
    row = lax.broadcasted_iota(jnp.int32, (LANES, sw), 0)
    first_head = hs * (2 * n_hp)

    @pl.when(hs == 0)
    def _():
        lse_ref[...] = jnp.zeros_like(lse_ref)

    def stream(r, carry):
        rows = lambda n: pl.ds(r, n, stride=dil) if dil > 1 else pl.ds(0, n)
        lses = []
        for j in range(n_hp):
            q = q_refs[j][0, rows(tq), :]
            qTs = [jnp.transpose(q[qs * sw:(qs + 1) * sw]).astype(BF16) for qs in range(tq // sw)]
            k, v = kc_refs[j][0, rows(tq), :], vc_refs[j][0, rows(tq), :]
            if has_prev:
                k = jnp.concatenate([kp_refs[j][0, rows(span), :], k], axis=0)
                v = jnp.concatenate([vp_refs[j][0, rows(span), :], v], axis=0)
            k = k.astype(BF16)
            vT = jnp.transpose(v).astype(BF16)
            outs = []
            for h in range(2):
                o_parts, lse_parts = [], []
                for qs in range(tq // sw):
                    qpad = jnp.where(row // HEAD_DIM == h, qTs[qs], jnp.zeros_like(qTs[qs]))
                    if has_prev:
                        k0, nk, b = qs * sw, span + sw, (band_first if qs == 0 else band)
                    elif qs == 0 and tq == sw:
                        k0, nk, b = 0, sw, band[span:, :]
                    elif qs == 0:
                        k0, nk, b = 0, span + sw, jnp.concatenate([band[span:, :], jnp.full((span, sw), NEG, F32)], axis=0)
                    else:
                        k0, nk, b = qs * sw - span, span + sw, band
                    s = jnp.dot(k[k0:k0 + nk], qpad, preferred_element_type=F32) + b
                    m = jnp.max(s, axis=0, keepdims=True)
                    p = jnp.exp(s - m)
                    l = jnp.sum(p, axis=0, keepdims=True)
                    o_parts.append(jnp.dot(vT[h * HEAD_DIM:(h + 1) * HEAD_DIM, k0:k0 + nk], p.astype(BF16),
                                           preferred_element_type=F32) / l)
                    lse_parts.append(m + jnp.log(l))
                outs.append(jnp.concatenate(o_parts, axis=1))
                lses.append(jnp.concatenate(lse_parts, axis=1))
            o_scr[j, rows(tq), :] = jnp.transpose(jnp.concatenate(outs, axis=0))
        stat = jnp.transpose(jnp.concatenate(lses + [jnp.zeros((LANES - len(lses), tq), F32)], axis=0))
        lse_scr[rows(tq), :] = pltpu.roll(stat, first_head, 1)
        return carry

    if dil == 1:
        stream(0, 0)
    else:
        def two_streams(r, carry):
            stream(r, carry)
            return stream(r + dil // 2, carry)
        lax.fori_loop(0, dil // 2, two_streams, 0)
    for j in range(n_hp):
        o_ref[0, :, j * LANES:(j + 1) * LANES] = o_scr[j]
    lane = lax.broadcasted_iota(jnp.int32, lse_scr.shape, 1)
    mine = (lane >= first_head) & (lane < first_head + 2 * n_hp)
    lse_ref[0] = jnp.where(mine, lse_scr[...], lse_ref[0])


def _band_attention(q, k, v, dil, span, tq, name, n_hp=2):
    B, S, W = q.shape
    L = S // dil
    tq = min(tq, L)
    assert L % tq == 0 and tq % span == 0
    per_tile = tq // span
    n_pairs = W // LANES
    assert n_pairs % n_hp == 0
    sw = tq
    i = np.arange(span + sw)[:, None]
    j = np.arange(sw)[None, :]
    bias = jnp.asarray(np.where((j - i + span >= 0) & (j - i + span <= span), 0.0, NEG), F32)
    cur = lambda jj: pl.BlockSpec((1, dil * tq, LANES), lambda b, t, hs: (b, t, hs * n_hp + jj))
    prev = lambda jj: pl.BlockSpec((1, dil * span, LANES),
                                   lambda b, t, hs: (b, jnp.maximum(t * per_tile - 1, 0), hs * n_hp + jj))
    each = lambda mk: [mk(jj) for jj in range(n_hp)]
    has_prev = L > tq
    if has_prev:
        specs = each(cur) + each(prev) + each(cur) + each(prev) + each(cur)
        args = [q] * n_hp + [k] * (2 * n_hp) + [v] * (2 * n_hp)
    else:
        specs = each(cur) * 3
        args = [q] * n_hp + [k] * n_hp + [v] * n_hp
    res = pl.pallas_call(
        functools.partial(_band_kernel, n_hp=n_hp, dil=dil, span=span, has_prev=has_prev),
        grid=(B, L // tq, n_pairs // n_hp),
        in_specs=specs + [pl.BlockSpec(bias.shape, lambda b, t, hs: (0, 0))],
        out_specs=[pl.BlockSpec((1, dil * tq, n_hp * LANES), lambda b, t, hs: (b, t, hs)),
                   pl.BlockSpec((1, dil * tq, LANES), lambda b, t, hs: (b, t, 0))],
        out_shape=[jax.ShapeDtypeStruct((B, S, W), F32), jax.ShapeDtypeStruct((B, S, LANES), F32)],
        scratch_shapes=[pltpu.VMEM((n_hp, dil * tq, LANES), F32), pltpu.VMEM((dil * tq, LANES), F32)],
        name=name, compiler_params=_cparams(("parallel", "parallel", "arbitrary")))(*args, bias)
    return res[0].reshape(B * S, W), res[1].reshape(B * S, LANES)


def _merge_outproj_kernel(*refs, n_groups):
    x_ref = refs[0]
    o_refs, l_refs = refs[1:1 + n_groups], refs[1 + n_groups:1 + 2 * n_groups]
    w_ref, y_ref = refs[-2], refs[-1]
    W = o_refs[0].shape[1]
    lses = [r[...] for r in l_refs]
    m = functools.reduce(jnp.maximum, lses)
    es = [jnp.exp(l - m) for l in lses]
    tot = functools.reduce(lambda a, b: a + b, es)
    expand = (lax.broadcasted_iota(jnp.int32, (LANES, W), 1) // HEAD_DIM
              == lax.broadcasted_iota(jnp.int32, (LANES, W), 0)).astype(BF16)
    mix = None
    for e, o_ref in zip(es, o_refs):
        hi, mid, _ = _split3(e / tot)
        wexp = jnp.dot(hi, expand, preferred_element_type=F32) + jnp.dot(mid, expand, preferred_element_type=F32)
        mix = wexp * o_ref[...] if mix is None else mix + wexp * o_ref[...]
    y_ref[...] = x_ref[...] + jnp.dot(mix.astype(BF16), w_ref[...], preferred_element_type=F32)


def _merge_outproj(x2d, outs, lses, w_bf, name):
    M, D = x2d.shape
    W = outs[0].shape[1]
    tm = _tile(M, 256)
    row = lambda n: pl.BlockSpec((tm, n), lambda i: (i, 0))
    return pl.pallas_call(
        functools.partial(_merge_outproj_kernel, n_groups=len(outs)), grid=(M // tm,),
        in_specs=[row(D)] + [row(W)] * len(outs) + [row(LANES)] * len(lses)
        + [pl.BlockSpec(w_bf.shape, lambda i: (0, 0))],
        out_specs=row(D), out_shape=jax.ShapeDtypeStruct((M, D), F32), name=name,
        compiler_params=_cparams(("parallel",)))(x2d, *outs, *lses, w_bf)


def _head_major(x, n_heads, pad_rows):
    DB, T, _ = x.shape
    y = x.reshape(DB, T, n_heads, HEAD_DIM).transpose(0, 2, 1, 3)
    return jnp.pad(y, ((0, 0), (0, 0), (0, pad_rows - T), (0, 0)))


def _new_cols(x, n_heads):
    DB, T, _ = x.shape
    y = x.reshape(DB, T, n_heads, HEAD_DIM).transpose(0, 2, 3, 1)
    return jnp.pad(y, ((0, 0), (0, 0), (0, 0), (0, NEW_PAD - T)))


def _row_of(col_vec_row):
    return jnp.transpose(jnp.broadcast_to(col_vec_row, (SUBLANES, LANES)))[0:SUBLANES, 0:1]


def _fox_dec_kernel(pt_ref, *refs, n_pg, n_q):
    kts, vts, lfs = refs[:n_pg], refs[n_pg:2 * n_pg], refs[2 * n_pg:3 * n_pg]
    q_ref, knt_ref, vnt_ref, lfn_ref, o_ref, qbd_ref, base_ref, carry_ref, m_ref, l_ref, acc_ref = refs[3 * n_pg:]
    p = pl.program_id(1)
    nh = FOX_HEADS
    page = LANES
    rows, width = nh * QPAD, nh * HEAD_DIM
    lane = lax.broadcasted_iota(jnp.int32, (QPAD, LANES), 1)
    trow = lax.broadcasted_iota(jnp.int32, (QPAD, LANES), 0)
    r = lax.broadcasted_iota(jnp.int32, (page, page), 0)
    c = lax.broadcasted_iota(jnp.int32, (page, page), 1)
    stack = lambda xs: jnp.concatenate(xs, axis=0)
    own = (lax.broadcasted_iota(jnp.int32, (rows, width), 0) // QPAD
           == lax.broadcasted_iota(jnp.int32, (rows, width), 1) // HEAD_DIM)
    heads_of = lambda ref: ref[0].reshape(width, ref.shape[3])

    @pl.when(p == 0)
    def _():
        _online_init(m_ref, l_ref, acc_ref)
        carry_ref[...] = jnp.zeros_like(carry_ref)
        q_all = q_ref[0].reshape(rows, HEAD_DIM) * SCALE
        qbd_ref[...] = jnp.where(own, jnp.concatenate([q_all] * nh, axis=1), 0.0).astype(BF16)
        cnew = _mm3_left(lfn_ref[0], r <= c)
        bases, decs = [], []
        for h in range(nh):
            bh = _row_of(cnew[h:h + 1, :])
            bases.append(bh)
            decs.append(bh - cnew[h:h + 1, :])
        base_ref[...] = jnp.broadcast_to(stack(bases), base_ref.shape)
        live = (lane <= trow) & (lane < n_q)
        s = jnp.dot(qbd_ref[...], heads_of(knt_ref).astype(BF16), preferred_element_type=F32) + stack(decs)
        s = jnp.where(stack([live] * nh), s, NEG)
        alpha, pr = _online_step(s, m_ref, l_ref)
        acc_ref[...] = alpha * acc_ref[...] + _mm_nt(pr, heads_of(vnt_ref))

    lf_all = stack([lf[0] for lf in lfs])
    after_all = _mm3_left(lf_all, r > c)
    tot_all = after_all[:, 0:1] + lf_all[:, 0:1]
    carry = carry_ref[:, 0:1]
    decay = [None] * n_pg
    for j in reversed(range(n_pg)):
        decay[j] = carry + after_all[j * nh:(j + 1) * nh]
        carry = carry + tot_all[j * nh:(j + 1) * nh]
    carry_ref[...] = jnp.broadcast_to(carry, carry_ref.shape)
    per_row = lambda d: stack([jnp.broadcast_to(d[h:h + 1, :], (QPAD, page)) for h in range(nh)])
    s = jnp.concatenate([jnp.dot(qbd_ref[...], heads_of(kts[j]).astype(BF16), preferred_element_type=F32)
                         + per_row(decay[j]) for j in range(n_pg)], axis=1) + base_ref[:, 0:1]
    alpha, pr = _online_step(s, m_ref, l_ref)
    acc_ref[...] = alpha * acc_ref[...] + sum(_mm_nt(pr[:, j * page:(j + 1) * page], heads_of(vts[j]))
                                              for j in range(n_pg))

    @pl.when(p == pl.num_programs(1) - 1)
    def _():
        acc = jnp.where(own, acc_ref[...], 0.0)
        o_ref[0] = sum(acc[:, h * HEAD_DIM:(h + 1) * HEAD_DIM] for h in range(nh)) / l_ref[...]


def _fox_decode(q, k_new, v_new, lf_new, cache_k, cache_v, cache_lf, pt_flat):
    DB, T, W = q.shape
    n_pool, page, nh, _ = cache_k.shape
    assert page == LANES and T <= QPAD
    n_pages = pt_flat.shape[0] // DB
    n_pg = math.gcd(n_pages, PAGES_PER_STEP)
    n_steps = n_pages // n_pg
    ckt = jnp.transpose(cache_k, (0, 2, 3, 1))
    cvt = jnp.transpose(cache_v, (0, 2, 3, 1))
    clf = jnp.transpose(cache_lf, (0, 2, 1))
    qh = _head_major(q, nh, QPAD)
    knt, vnt = _new_cols(k_new, nh), _new_cols(v_new, nh)
    lfn = jnp.pad(jnp.transpose(lf_new, (0, 2, 1)), ((0, 0), (0, 0), (0, LANES - T)))
    page_idx = lambda j: (lambda b, p, pt: (pt[b * n_pages + (n_steps - 1 - p) * n_pg + j], 0, 0, 0))
    lf_idx = lambda j: (lambda b, p, pt: (pt[b * n_pages + (n_steps - 1 - p) * n_pg + j], 0, 0))
    per_db = lambda a: pl.BlockSpec((1,) + a.shape[1:], lambda b, p, pt: (b,) + (0,) * (a.ndim - 1))
    rows = nh * QPAD
    grid_spec = pltpu.PrefetchScalarGridSpec(
        num_scalar_prefetch=1, grid=(DB, n_steps),
        in_specs=([pl.BlockSpec((1, nh, HEAD_DIM, page), page_idx(j)) for j in range(n_pg)] * 2
                  + [pl.BlockSpec((1, nh, page), lf_idx(j)) for j in range(n_pg)]
                  + [per_db(qh), per_db(knt), per_db(vnt), per_db(lfn)]),
        out_specs=pl.BlockSpec((1, rows, HEAD_DIM), lambda b, p, pt: (b, 0, 0)),
        scratch_shapes=[pltpu.VMEM((rows, W), BF16), pltpu.VMEM((rows, LANES), F32), pltpu.VMEM((nh, LANES), F32),
                        pltpu.VMEM((rows, 1), F32), pltpu.VMEM((rows, 1), F32), pltpu.VMEM((rows, W), F32)])
    o = pl.pallas_call(
        functools.partial(_fox_dec_kernel, n_pg=n_pg, n_q=T), grid_spec=grid_spec,
        out_shape=jax.ShapeDtypeStruct((DB, rows, HEAD_DIM), F32), name="fox_decode",
        compiler_params=_cparams(("parallel", "arbitrary")))(
            pt_flat, *([ckt] * n_pg), *([cvt] * n_pg), *([clf] * n_pg), qh, knt, vnt, lfn)
    return o.reshape(DB, nh, QPAD, HEAD_DIM)[:, :, :T].transpose(0, 2, 1, 3).reshape(DB, T, W)


def _nsa_dec_a_kernel(q_ref, kct_ref, vct_ref, kwt_ref, vwt_ref, kwnt_ref, vwnt_ref, gt_ref, map_ref,
                      o_ref, bias_ref, *, n_q, ns, n_sel, past, n_keys):
    nr = NSA_GROUP * QPAD
    n_ch = kct_ref.shape[3]
    wb = kwt_ref.shape[3]
    t_row = lax.broadcasted_iota(jnp.int32, (nr, 1), 0) % QPAD
    imps = []
    for kv in range(NSA_KV_HEADS):
        q = (q_ref[0, kv] * SCALE).astype(BF16)
        cidx = lax.broadcasted_iota(jnp.int32, (nr, n_ch), 1)
        pc = _softmax_rows(_mm(q, kct_ref[0, kv]), cidx >= 1)
        oc = _mm_nt(pc, vct_ref[0, kv])
        pcat = jnp.concatenate([pc[j * QPAD:(j + 1) * QPAD] for j in range(NSA_GROUP)], axis=1)
        imps.append(jnp.dot(pcat.astype(BF16), map_ref[...], preferred_element_type=F32))
        sw = _mm(q, kwt_ref[0, kv])
        sn = _mm(q, kwnt_ref[0, kv])
        iw = lax.broadcasted_iota(jnp.int32, (nr, wb), 1)
        un = lax.broadcasted_iota(jnp.int32, (nr, NEW_PAD), 1)
        mw = (wb + t_row - iw <= NSA_WINDOW) & (past - wb + iw >= 0)
        mn = (un <= t_row) & (un < n_q)
        sw = jnp.where(mw, sw, NEG)
        sn = jnp.where(mn, sn, NEG)
        m = jnp.maximum(jnp.max(sw, axis=1, keepdims=True), jnp.max(sn, axis=1, keepdims=True))
        pw = jnp.where(mw, jnp.exp(sw - m), 0.0)
        pn = jnp.where(mn, jnp.exp(sn - m), 0.0)
        l = jnp.sum(pw, axis=1, keepdims=True) + jnp.sum(pn, axis=1, keepdims=True)
        ow = (_mm_nt(pw, vwt_ref[0, kv]) + _mm_nt(pn, vwnt_ref[0, kv])) / l
        gt = gt_ref[0, kv]
        o_ref[0, kv] = gt[:, 0:1] * oc + gt[:, 2:3] * ow

    imp = jnp.concatenate(imps, axis=0)
    t_sel = lax.broadcasted_iota(jnp.int32, (imp.shape[0], 1), 0) % QPAD
    sel = _select_blocks(imp, past + t_sel, ns, n_sel).astype(BF16)
    ch = 8 * LANES
    for c0 in range(0, n_keys, ch):
        w = min(ch, n_keys - c0)
        key = c0 + lax.broadcasted_iota(jnp.int32, (sel.shape[1], w), 1)
        onehot = (key // NSA_SLC_BLOCK == lax.broadcasted_iota(jnp.int32, (sel.shape[1], w), 0)).astype(BF16)
        chosen = jnp.dot(sel, onehot, preferred_element_type=F32) > 0.5
        kpos = c0 + lax.broadcasted_iota(jnp.int32, (sel.shape[0], w), 1)
        ok = chosen & (kpos <= past + t_sel) & (kpos < past + n_q)
        bias_ref[0, :, c0:c0 + w] = jnp.where(ok, 0.0, NEG)


def _nsa_dec_b_kernel(pt_ref, *refs, n_pg):
    kts, vts = refs[:n_pg], refs[n_pg:2 * n_pg]
    q_ref, knt_ref, vnt_ref, bias_ref, biasn_ref, o_ref, qbd_ref, m_ref, l_ref, acc_ref = refs[2 * n_pg:]
    p = pl.program_id(1)
    nr = NSA_GROUP * QPAD
    page = LANES
    rows, width = NSA_KV_HEADS * nr, NSA_KV_HEADS * HEAD_DIM
    own = (lax.broadcasted_iota(jnp.int32, (rows, width), 0) // nr
           == lax.broadcasted_iota(jnp.int32, (rows, width), 1) // HEAD_DIM)
    heads_of = lambda ref: ref[0].reshape(width, ref.shape[3])
    per_row = lambda b: jnp.concatenate([b[kv * QPAD:(kv + 1) * QPAD] for kv in range(NSA_KV_HEADS)
                                         for _ in range(NSA_GROUP)], axis=0)

    @pl.when(p == 0)
    def _():
        _online_init(m_ref, l_ref, acc_ref)
        q_all = q_ref[0].reshape(rows, HEAD_DIM) * SCALE
        qbd_ref[...] = jnp.where(own, jnp.concatenate([q_all] * NSA_KV_HEADS, axis=1), 0.0).astype(BF16)

    s = jnp.concatenate([jnp.dot(qbd_ref[...], heads_of(kts[j]).astype(BF16), preferred_element_type=F32)
                         for j in range(n_pg)], axis=1) + per_row(bias_ref[0])
    alpha, pr = _online_step(s, m_ref, l_ref)
    acc_ref[...] = alpha * acc_ref[...] + sum(_mm_nt(pr[:, j * page:(j + 1) * page], heads_of(vts[j]))
                                              for j in range(n_pg))

    @pl.when(p == pl.num_programs(1) - 1)
    def _():
        s = jnp.dot(qbd_ref[...], heads_of(knt_ref).astype(BF16), preferred_element_type=F32) + per_row(biasn_ref[0])
        alpha, pr = _online_step(s, m_ref, l_ref)
        acc = jnp.where(own, alpha * acc_ref[...] + _mm_nt(pr, heads_of(vnt_ref)), 0.0)
        o_ref[0] = sum(acc[:, kv * HEAD_DIM:(kv + 1) * HEAD_DIM] for kv in range(NSA_KV_HEADS)) / l_ref[...]


def _nsa_decode(qb, kct, vct, cache_ks, cache_vs, ks_new, vs_new, swa_k, swa_v, kw_new, vw_new,
                gates, pt_flat, past):
    DB, T, W = qb.shape
    n_pool, page = cache_ks.shape[:2]
    assert page == LANES and past % LANES == 0 and T <= QPAD
    n_pages = pt_flat.shape[0] // DB
    n_ch = kct.shape[3]
    ns = -(-(past + T) // NSA_SLC_BLOCK)
    ns_pad = -(-ns // LANES) * LANES
    n_sel = min(NSA_TOPN, ns)
    nr = NSA_GROUP * QPAD
    n_keys = past + NEW_PAD
    q5 = _head_major(qb, NSA_HEADS, QPAD).reshape(DB, NSA_KV_HEADS, nr, HEAD_DIM)
    g5 = _head_major(jnp.pad(gates.reshape(DB, T, NSA_HEADS, 3), ((0, 0),) * 3 + ((0, HEAD_DIM - 3),))
                     .reshape(DB, T, NSA_HEADS * HEAD_DIM), NSA_HEADS, QPAD)[..., :3]
    g5 = g5.reshape(DB, NSA_KV_HEADS, nr, 3)
    smap = jnp.asarray(np.tile(_slc_map_rows(n_ch, ns_pad), (NSA_GROUP, 1)), dtype=BF16)
    kwt = jnp.transpose(swa_k, (0, 2, 3, 1))
    vwt = jnp.transpose(swa_v, (0, 2, 3, 1))
    kwnt, vwnt = _new_cols(kw_new, NSA_KV_HEADS), _new_cols(vw_new, NSA_KV_HEADS)
    per = lambda a: pl.BlockSpec((1,) + a.shape[1:], lambda b: (b,) + (0,) * (a.ndim - 1))
    part, bias = pl.pallas_call(
        functools.partial(_nsa_dec_a_kernel, n_q=T, ns=ns, n_sel=n_sel, past=past, n_keys=n_keys), grid=(DB,),
        in_specs=[per(q5), per(kct), per(vct), per(kwt), per(vwt), per(kwnt), per(vwnt), per(g5),
                  pl.BlockSpec(smap.shape, lambda b: (0, 0))],
        out_specs=[pl.BlockSpec((1, NSA_KV_HEADS, nr, HEAD_DIM), lambda b: (b, 0, 0, 0)),
                   pl.BlockSpec((1, NSA_KV_HEADS * QPAD, n_keys), lambda b: (b, 0, 0))],
        out_shape=[jax.ShapeDtypeStruct((DB, NSA_KV_HEADS, nr, HEAD_DIM), F32),
                   jax.ShapeDtypeStruct((DB, NSA_KV_HEADS * QPAD, n_keys), F32)], name="nsa_decode_a",
        compiler_params=_cparams(("parallel",)))(q5, kct, vct, kwt, vwt, kwnt, vwnt, g5, smap)

    n_pg = math.gcd(n_pages, 2 * PAGES_PER_STEP)
    ckt = jnp.transpose(cache_ks, (0, 2, 3, 1))
    cvt = jnp.transpose(cache_vs, (0, 2, 3, 1))
    ksnt, vsnt = _new_cols(ks_new, NSA_KV_HEADS), _new_cols(vs_new, NSA_KV_HEADS)
    page_idx = lambda j: (lambda b, p, pt: (pt[b * n_pages + p * n_pg + j], 0, 0, 0))
    per_db = lambda a: pl.BlockSpec((1,) + a.shape[1:], lambda b, p, pt: (b,) + (0,) * (a.ndim - 1))
    rows = NSA_KV_HEADS * nr
    grid_spec = pltpu.PrefetchScalarGridSpec(
        num_scalar_prefetch=1, grid=(DB, n_pages // n_pg),
        in_specs=([pl.BlockSpec((1, NSA_KV_HEADS, HEAD_DIM, page), page_idx(j)) for j in range(n_pg)] * 2
                  + [per_db(q5), per_db(ksnt), per_db(vsnt),
                     pl.BlockSpec((1, NSA_KV_HEADS * QPAD, n_pg * page), lambda b, p, pt: (b, 0, p)),
                     pl.BlockSpec((1, NSA_KV_HEADS * QPAD, NEW_PAD), lambda b, p, pt: (b, 0, past // NEW_PAD))]),
        out_specs=pl.BlockSpec((1, rows, HEAD_DIM), lambda b, p, pt: (b, 0, 0)),
        scratch_shapes=[pltpu.VMEM((rows, LANES), BF16), pltpu.VMEM((rows, 1), F32), pltpu.VMEM((rows, 1), F32),
                        pltpu.VMEM((rows, LANES), F32)])
    osl = pl.pallas_call(
        functools.partial(_nsa_dec_b_kernel, n_pg=n_pg), grid_spec=grid_spec,
        out_shape=jax.ShapeDtypeStruct((DB, rows, HEAD_DIM), F32), name="nsa_decode_b",
        compiler_params=_cparams(("parallel", "arbitrary")))(
            pt_flat, *([ckt] * n_pg), *([cvt] * n_pg), q5, ksnt, vsnt, bias, bias)

    o = part + g5[..., 1:2] * osl.reshape(DB, NSA_KV_HEADS, nr, HEAD_DIM)
    return o.reshape(DB, NSA_HEADS, QPAD, HEAD_DIM)[:, :, :T].transpose(0, 2, 1, 3).reshape(DB, T, W)


def _dil_dec_kernel(q_ref, kt_ref, vt_ref, knt_ref, vnt_ref, o_ref, *, n_q, wc):
    hb = kt_ref.shape[1]
    t = lax.broadcasted_iota(jnp.int32, (QPAD, 1), 0)

    def log_mult(d, ok):
        w = jnp.zeros(d.shape, F32)
        for window, dil in DIL_PAIRS:
            w = w + ((d >= 0) & (d <= window) & (d % dil == 0)).astype(F32)
        return jnp.where(ok, w, 0.0)

    wk = log_mult(wc + t - lax.broadcasted_iota(jnp.int32, (QPAD, wc), 1), t < n_q)
    un = lax.broadcasted_iota(jnp.int32, (QPAD, NEW_PAD), 1)
    wn = log_mult(t - un, (t < n_q) & (un < n_q))
    for h in range(hb):
        q = (q_ref[0, h] * SCALE).astype(BF16)
        sk = jnp.where(wk > 0.0, _mm(q, kt_ref[0, h]), NEG)
        sn = jnp.where(wn > 0.0, _mm(q, knt_ref[0, h]), NEG)
        m = jnp.maximum(jnp.max(sk, axis=1, keepdims=True), jnp.max(sn, axis=1, keepdims=True))
        pk = wk * jnp.exp(sk - m)
        pn = wn * jnp.exp(sn - m)
        l = jnp.sum(pk, axis=1, keepdims=True) + jnp.sum(pn, axis=1, keepdims=True)
        o_ref[0, h] = (_mm_nt(pk, vt_ref[0, h]) + _mm_nt(pn, vnt_ref[0, h])) / jnp.where(l > 0.0, l, 1.0)


def _dilated_decode(q, k_new, v_new, cache_k, cache_v):
    DB, T, W = q.shape
    wc, nh = cache_k.shape[1], cache_k.shape[2]
    ckt = jnp.transpose(cache_k, (0, 2, 3, 1))
    cvt = jnp.transpose(cache_v, (0, 2, 3, 1))
    qh = _head_major(q, nh, QPAD)
    knt, vnt = _new_cols(k_new, nh), _new_cols(v_new, nh)
    hb = math.gcd(nh, 4)
    spec = lambda a: pl.BlockSpec((1, hb) + a.shape[2:], lambda b, j: (b, j, 0, 0))
    o = pl.pallas_call(
        functools.partial(_dil_dec_kernel, n_q=T, wc=wc), grid=(DB, nh // hb),
        in_specs=[spec(qh), spec(ckt), spec(cvt), spec(knt), spec(vnt)],
        out_specs=pl.BlockSpec((1, hb, QPAD, HEAD_DIM), lambda b, j: (b, j, 0, 0)),
        out_shape=jax.ShapeDtypeStruct((DB, nh, QPAD, HEAD_DIM), F32), name="dilated_decode",
        compiler_params=_cparams(("parallel", "parallel")))(qh, ckt, cvt, knt, vnt)
    return o[:, :, :T].transpose(0, 2, 1, 3).reshape(DB, T, W)


def kernel(x_prompt, x_sample, cache_a_k, cache_a_v, cache_a_logf, cache_b_cmp_k, cache_b_cmp_v, cache_b_slc_k, cache_b_slc_v, cache_b_swa_k, cache_b_swa_v, cache_c_k, cache_c_v, page_table, norm_mix0, w_in0, fox_bf, nsa_pe_k, nsa_w1_k, nsa_w2_k, nsa_pe_v, nsa_w1_v, nsa_w2_v, w_out0, norm_ffn0, ffn_w_gate, ffn_w_up, ffn_w_down, norm_mix1, w_in1, w_out1, norm_ffn1, moe_router, moe_w_gate, moe_w_up, moe_w_down, norm_final):
    B, S, D = x_prompt.shape
    DB, T, _ = x_sample.shape
    n_pages = page_table.shape[1]
    past = n_pages * cache_a_k.shape[1]
    pt_flat = page_table.reshape(-1).astype(jnp.int32)
    fw = FOX_HEADS * HEAD_DIM
    nw = NSA_HEADS * HEAD_DIM
    kvw = NSA_KV_HEADS * HEAD_DIM

    cuts = np.cumsum([0, fw, fw, fw, FOX_HEADS, nw] + [kvw] * 6 + [3 * NSA_HEADS])
    col = lambda i: w_in0[:, cuts[i]:cuts[i + 1]]
    qa_w, ka_w, va_w, fa_w, qb_w, kc_w, vc_w, ks_w, vs_w, kw_w, vw_w, gb_w = [col(i) for i in range(12)]

    f32_out = ((F32, 1.0),)
    bf16_out = ((BF16, 1.0),)

    def pack(ws, ropes, emits=None):
        widths = [w.shape[1] for w in ws]
        starts = np.concatenate([[0], np.cumsum(widths)[:-1]])
        emits = emits or [f32_out] * len(ws)
        return (jnp.concatenate(ws, axis=1).astype(BF16),
                [(int(s), int(w), r, e) for s, w, r, e in zip(starts, widths, ropes, emits)])

    both_out = ((F32, 1.0), (BF16, 1.0))
    q_out = ((BF16, SCALE),)
    w0r, segs0r = pack([kc_w, vc_w, ka_w, ks_w, kw_w], [True, False, False, True, True],
                       [f32_out] * 2 + [bf16_out] * 3)
    w0c, segs0c = pack([qa_w, ka_w, va_w, qb_w, kc_w, ks_w, kw_w, vc_w, vs_w, vw_w, gb_w],
                       [False, False, False, True, True, True, True, False, False, False, "sigmoid"],
                       [q_out, f32_out, both_out, q_out, f32_out, f32_out, f32_out, f32_out, both_out, both_out,
                        f32_out])
    w0c = w0c.T
    w0s, segs0s = pack([qa_w, ka_w, va_w, qb_w, kc_w, ks_w, kw_w, vc_w, vs_w, vw_w],
                       [False, False, False, True, True, True, True, False, False, False])
    wgate = gb_w.astype(BF16)
    wft = jnp.zeros((16, D), F32).at[:FOX_HEADS].set(fa_w.T).astype(BF16)
    wfr = jnp.zeros((D, LANES), F32).at[:, :FOX_HEADS].set(fa_w).astype(BF16)
    brow = jnp.zeros((1, LANES), F32).at[0, :FOX_HEADS].set(fox_bf)
    logf_args = (wft, fox_bf.reshape(FOX_HEADS, 1).astype(F32), wfr, brow)
    dw = w_in1.shape[1] // 3
    w1 = w_in1.astype(BF16)
    segs1 = [(0, dw, True, f32_out), (dw, dw, True, f32_out), (2 * dw, dw, False, f32_out)]
    w1c = w1[:, dw:].T
    segs1c = [(0, dw, True, f32_out), (dw, dw, False, f32_out)]
    w_out0_b, w_out1_b = w_out0.astype(BF16), w_out1.astype(BF16)
    ffn_g, ffn_u, ffn_d = ffn_w_gate.astype(BF16), ffn_w_up.astype(BF16), ffn_w_down.astype(BF16)
    moe_g, moe_u, moe_d = moe_w_gate.astype(BF16), moe_w_up.astype(BF16), moe_w_down.astype(BF16)
    cmp_k_w = _cmp_weights(nsa_pe_k, nsa_w1_k, nsa_w2_k)
    cmp_v_w = _cmp_weights(nsa_pe_v, nsa_w1_v, nsa_w2_v)

    tab_p = _rope_tables(jnp.arange(S))
    tab_s = _rope_tables(past + jnp.arange(DB * T) % T)
    tm_p = _tile(S, 512)
    npb = S // tm_p
    win_b = min(NSA_WINDOW, S)
    win_c = min(DIL_WINDOW_MAX, S)

    xp = x_prompt.reshape(B * S, D)
    (kc, vc, ka_b, ks_b, kw_b, qat_b, kat, vat, vat_b, qbt_b, kct, kst, kwt, vct, vst, vst_b, vwt, vwt_b,
     gates_t, lft_p, lf_rows) = _project(
        xp, norm_mix0, tab_p, npb, tm_p, w=w0r, row_segs=segs0r, wt=w0c, col_segs=segs0c,
        logf=logf_args, name="proj0_prompt")
    r3 = lambda a: a.reshape(B, S, a.shape[-1])
    heads = lambda a, h: a.reshape(B, h, HEAD_DIM, a.shape[-1])
    kaug, c0 = _fox_prep(r3(ka_b), r3(lf_rows), tm_p)
    o_at = _fox_prompt(qat_b, kaug, vat_b, c0, tm_p)
    kcmp_p = _compress(r3(kc), cmp_k_w, "compress_k_prompt")
    vcmp_p = _compress(r3(vc), cmp_v_w, "compress_v_prompt")
    kv2 = lambda a: heads(a, NSA_KV_HEADS)
    o_bt = _nsa_prompt(qbt_b, kv2(kcmp_p), kv2(vcmp_p), r3(ks_b), vst_b, r3(kw_b), vwt_b, gates_t)
    hp = _outproj(xp, [o_at, o_bt], w_out0_b, "outproj0_prompt", tm=tm_p)
    hp = _ffn(hp, norm_ffn0, ffn_g, ffn_u, ffn_d, "ffn_prompt")

    xs = x_sample.reshape(DB * T, D)
    (qa_s, ka_s, va_s, qb_s, kc_s, ks_s, kw_s, vc_s, vs_s, vw_s, gates_s, lft_s, _) = _project(
        xs, norm_mix0, tab_s, 1, DB * T, w=w0s, row_segs=segs0s, w_gate=wgate, logf=logf_args,
        name="proj0_sample")
    s3 = lambda a: a.reshape(DB, T, a.shape[-1])
    lf_s = jnp.transpose(lft_s[0].reshape(FOX_HEADS, DB, T), (1, 2, 0))
    o_a_s = _fox_decode(s3(qa_s), s3(ka_s), s3(va_s), lf_s, cache_a_k, cache_a_v, cache_a_logf, pt_flat)
    kcmp_s = _compress_paged(cache_b_cmp_k, pt_flat, DB, cmp_k_w, "compress_k_paged")
    vcmp_s = _compress_paged(cache_b_cmp_v, pt_flat, DB, cmp_v_w, "compress_v_paged")
    kvs = lambda a: a.reshape(DB, NSA_KV_HEADS, HEAD_DIM, a.shape[-1])
    o_b_s = _nsa_decode(s3(qb_s), kvs(kcmp_s), kvs(vcmp_s), cache_b_slc_k, cache_b_slc_v, s3(ks_s), s3(vs_s),
                        cache_b_swa_k, cache_b_swa_v, s3(kw_s), s3(vw_s), s3(gates_s), pt_flat, past)
    hs = _outproj(xs, [o_a_s.reshape(DB * T, fw), o_b_s.reshape(DB * T, nw)], w_out0_b, "outproj0_sample")
    hs = _ffn(hs, norm_ffn0, ffn_g, ffn_u, ffn_d, "ffn_sample")

    first_c = (S - win_c) // tm_p
    segs1p = [(0, dw, True, ((F32, SCALE),)), (dw, dw, True, f32_out), (2 * dw, dw, False, f32_out)]
    q1, k1, v1, k1t, v1t = _project(hp, norm_mix1, tab_p, npb, tm_p, w=w1, row_segs=segs1p, wt=w1c,
                                    col_segs=segs1c, col_from=(npb, first_c), name="proj1_prompt")
    groups = [_band_attention(q1.reshape(B, S, dw), k1.reshape(B, S, dw), v1.reshape(B, S, dw), dil,
                              window // dil, 512 if dil == 1 else 256, "dilated_prompt_d%d" % dil,
                              n_hp=4 if dil < 16 else 2)
              for window, dil in DIL_PAIRS]
    hp = _merge_outproj(hp, [g[0] for g in groups], [g[1] for g in groups], w_out1_b, "outproj1_prompt")
    y_prompt = _moe_final(hp, norm_ffn1, moe_router, moe_g, moe_u, moe_d, norm_final, "moe_prompt").reshape(B, S, D)

    q1s, k1s, v1s = _project(hs, norm_mix1, tab_s, 1, DB * T, w=w1, row_segs=segs1, name="proj1_sample")
    o1s = _dilated_decode(s3(q1s), s3(k1s), s3(v1s), cache_c_k, cache_c_v)
    hs = _outproj(hs, [o1s.reshape(DB * T, dw)], w_out1_b, "outproj1_sample")
    y_sample = _moe_final(hs, norm_ffn1, moe_router, moe_g, moe_u, moe_d, norm_final, "moe_sample").reshape(DB, T, D)

    def state(a, h, last=None):
        a = a.reshape(a.shape[0], h, HEAD_DIM, a.shape[-1])
        if last is not None:
            a = a[..., a.shape[-1] - last:]
        return jnp.transpose(a, (0, 3, 1, 2))

    h4 = lambda a, h: a.reshape(DB, T, h, HEAD_DIM)
    nh1 = dw // HEAD_DIM
    return (y_prompt, y_sample,
            state(kat, FOX_HEADS), state(vat, FOX_HEADS), jnp.transpose(lft_p, (0, 2, 1)),
            state(kct, NSA_KV_HEADS), state(vct, NSA_KV_HEADS), state(kst, NSA_KV_HEADS), state(vst, NSA_KV_HEADS),
            state(kwt, NSA_KV_HEADS, win_b), state(vwt, NSA_KV_HEADS, win_b),
            state(k1t, nh1, win_c), state(v1t, nh1, win_c),
            h4(ka_s, FOX_HEADS), h4(va_s, FOX_HEADS), lf_s,
            h4(kc_s, NSA_KV_HEADS), h4(vc_s, NSA_KV_HEADS), h4(ks_s, NSA_KV_HEADS),
            h4(vs_s, NSA_KV_HEADS), h4(kw_s, NSA_KV_HEADS), h4(vw_s, NSA_KV_HEADS),
            h4(k1s, nh1), h4(v1s, nh1))
```

```python
import functools
import math

import numpy as np
import jax
import jax.numpy as jnp
from jax import lax
from jax.experimental import pallas as pl
from jax.experimental.pallas import tpu as pltpu

F32 = jnp.float32
BF16 = jnp.bfloat16

HEAD_DIM = 64
HALF = HEAD_DIM // 2
LANES = 128
SUBLANES = 8
ROPE_THETA = 10000.0
RMS_EPS = 1e-6
NEG = -1e30
MASK_BIG = 30000.0
SCALE = HEAD_DIM ** -0.5

FOX_HEADS = 8
NSA_HEADS = 8
NSA_KV_HEADS = 2
NSA_GROUP = NSA_HEADS // NSA_KV_HEADS
NSA_CMP_LEN = 32
NSA_CMP_STRIDE = 16
NSA_SLC_BLOCK = 64
NSA_TOPN = 16
NSA_WINDOW = 512
NSA_FORCE_BONUS = 1e3
DIL_PAIRS = ((128, 1), (512, 4), (2048, 16))
DIL_WINDOW_MAX = 2048
TOP_K = 2
QPAD = SUBLANES
NEW_PAD = LANES
PAGES_PER_STEP = 32

VMEM_LIMIT = 56 * 1024 * 1024


def _tile(n, pref):
    return pref if n % pref == 0 else n


def _cparams(sem):
    return pltpu.CompilerParams(dimension_semantics=sem, vmem_limit_bytes=VMEM_LIMIT)


def _mm(a, b):
    return jnp.dot(a.astype(BF16), b.astype(BF16), preferred_element_type=F32)


def _mm_nt(a, b):
    return lax.dot_general(a.astype(BF16), b.astype(BF16), (((1,), (1,)), ((), ())),
                           preferred_element_type=F32)


def _split3(x):
    hi = x.astype(BF16)
    r = x - hi.astype(F32)
    mid = r.astype(BF16)
    lo = (r - mid.astype(F32)).astype(BF16)
    return hi, mid, lo


def _mm3_left(x, exact_rhs):
    b = exact_rhs.astype(BF16)
    hi, mid, lo = _split3(x)
    d = lambda p: jnp.dot(p, b, preferred_element_type=F32)
    return d(hi) + d(mid) + d(lo)


def _sigmoid(z):
    return 1.0 / (1.0 + jnp.exp(-z))


def _silu(z):
    return z * _sigmoid(z)


def _log_sigmoid(z):
    return jnp.minimum(z, 0.0) - jnp.log1p(jnp.exp(-jnp.abs(z)))


def _rmsnorm(x, g):
    return x * lax.rsqrt(jnp.mean(x * x, axis=-1, keepdims=True) + RMS_EPS) * g


def _rope_rows(y, cos, sin_signed):
    n = y.shape[1]
    lane = lax.broadcasted_iota(jnp.int32, y.shape, 1)
    first = (lane % HEAD_DIM) < HALF
    rot = jnp.where(first, pltpu.roll(y, n - HALF, 1), pltpu.roll(y, HALF, 1))
    reps = n // LANES
    if reps > 1:
        cos = jnp.concatenate([cos] * reps, axis=1)
        sin_signed = jnp.concatenate([sin_signed] * reps, axis=1)
    return y * cos + rot * sin_signed


def _rope_cols(yt, cos_t, sin_t):
    out = []
    for h in range(yt.shape[0] // HEAD_DIM):
        a = yt[h * HEAD_DIM:h * HEAD_DIM + HALF]
        b = yt[h * HEAD_DIM + HALF:(h + 1) * HEAD_DIM]
        out += [a * cos_t - b * sin_t, b * cos_t + a * sin_t]
    return jnp.concatenate(out, axis=0)


def _rope_tables(pos):
    inv = jnp.exp(-math.log(ROPE_THETA) * jnp.arange(HALF, dtype=F32) / HALF)
    ang = pos.astype(F32)[:, None] * inv[None, :]
    cos, sin = jnp.cos(ang), jnp.sin(ang)
    return (jnp.concatenate([cos, cos, cos, cos], axis=1),
            jnp.concatenate([-sin, sin, -sin, sin], axis=1), cos.T, sin.T)


def _softmax_rows(s, mask):
    sm = jnp.where(mask, s, NEG)
    m = jnp.max(sm, axis=1, keepdims=True)
    p = jnp.where(mask, jnp.exp(sm - m), 0.0)
    l = jnp.sum(p, axis=1, keepdims=True)
    return p / jnp.where(l > 0.0, l, 1.0)


def _online_step(s, m_ref, l_ref):
    m_prev = m_ref[...]
    m_new = jnp.maximum(m_prev, jnp.max(s, axis=1, keepdims=True))
    alpha = jnp.exp(m_prev - m_new)
    p = jnp.exp(s - m_new)
    l_ref[...] = alpha * l_ref[...] + jnp.sum(p, axis=1, keepdims=True)
    m_ref[...] = m_new
    return alpha, p


def _online_init(m_ref, l_ref, acc_ref):
    m_ref[...] = jnp.full(m_ref.shape, NEG, F32)
    l_ref[...] = jnp.zeros_like(l_ref)
    acc_ref[...] = jnp.zeros_like(acc_ref)


def _proj_kernel(*refs, row_segs, col_segs, n_gate, with_logf, col_from):
    it = iter(refs)
    x_ref, g_ref, cos_ref, sin_ref, cost_ref, sint_ref = [next(it) for _ in range(6)]
    w_ref = next(it) if row_segs else None
    wt_ref = next(it) if col_segs else None
    wg_ref = next(it) if n_gate else None
    if with_logf:
        wft_ref, bcol_ref, wfr_ref, brow_ref = next(it), next(it), next(it), next(it)
    outs = list(it)
    xn = _rmsnorm(x_ref[...], g_ref[...]).astype(BF16)
    k = 0
    for c0, width, rope, emits in row_segs:
        y = jnp.dot(xn, w_ref[:, c0:c0 + width], preferred_element_type=F32)
        if rope:
            y = _rope_rows(y, cos_ref[...], sin_ref[...])
        for dtype, scale in emits:
            outs[k][...] = (y if scale == 1.0 else y * scale).astype(dtype)
            k += 1

    def cols():
        kk = k
        for r0, height, rope, emits in col_segs:
            yt = _mm_nt(wt_ref[r0:r0 + height, :], xn)
            if rope == "sigmoid":
                yt = _sigmoid(yt)
            elif rope:
                yt = _rope_cols(yt, cost_ref[...], sint_ref[...])
            for dtype, scale in emits:
                outs[kk][0] = (yt if scale == 1.0 else yt * scale).astype(dtype)
                kk += 1

    if col_segs:
        if col_from:
            pl.when(pl.program_id(0) % col_from[0] >= col_from[1])(cols)
        else:
            cols()
        k += sum(len(e) for _, _, _, e in col_segs)
    if n_gate:
        outs[k][...] = _sigmoid(jnp.dot(xn, wg_ref[...], preferred_element_type=F32))
        k += 1
    if with_logf:
        yt = _mm_nt(wft_ref[...], xn)
        outs[k][0] = _log_sigmoid(yt[0:FOX_HEADS] + bcol_ref[...])
        outs[k + 1][...] = _log_sigmoid(jnp.dot(xn, wfr_ref[...], preferred_element_type=F32) + brow_ref[...])


def _project(x2d, gain, tables, n_pos_blocks, tm, w=None, row_segs=(), wt=None, col_segs=(),
             w_gate=None, logf=None, col_from=None, name="proj"):
    M, D = x2d.shape
    nt = M // tm
    n_seq = nt // n_pos_blocks
    cos_t, sin_t, cos_c, sin_c = tables
    pos_map = lambda i: (i % n_pos_blocks, 0)
    posc_map = lambda i: (0, i % n_pos_blocks)
    const = lambda a: pl.BlockSpec(a.shape, lambda i: (0,) * a.ndim)
    in_specs = [pl.BlockSpec((tm, D), lambda i: (i, 0)), const(gain.reshape(1, D)),
                pl.BlockSpec((tm, LANES), pos_map), pl.BlockSpec((tm, LANES), pos_map),
                pl.BlockSpec((HALF, tm), posc_map), pl.BlockSpec((HALF, tm), posc_map)]
    args = [x2d, gain.reshape(1, D), cos_t, sin_t, cos_c, sin_c]
    for a in (w, wt, w_gate):
        if a is not None:
            in_specs.append(const(a))
            args.append(a)
    out_shape, out_specs = [], []
    for _, wd, _, emits in row_segs:
        for dtype, _ in emits:
            out_shape.append(jax.ShapeDtypeStruct((M, wd), dtype))
            out_specs.append(pl.BlockSpec((tm, wd), lambda i: (i, 0)))
    first = col_from[1] if col_from else 0
    n_cb = n_pos_blocks - first
    col_map = lambda i: (i // n_pos_blocks, 0, jnp.maximum(i % n_pos_blocks - first, 0))
    for _, ht, _, emits in col_segs:
        for dtype, _ in emits:
            out_shape.append(jax.ShapeDtypeStruct((n_seq, ht, n_cb * tm), dtype))
            out_specs.append(pl.BlockSpec((1, ht, tm), col_map))
    n_gate = 0
    if w_gate is not None:
        n_gate = w_gate.shape[1]
        out_shape.append(jax.ShapeDtypeStruct((M, n_gate), F32))
        out_specs.append(pl.BlockSpec((tm, n_gate), lambda i: (i, 0)))
    if logf is not None:
        in_specs += [const(a) for a in logf]
        args += list(logf)
        out_shape.append(jax.ShapeDtypeStruct((n_seq, FOX_HEADS, n_pos_blocks * tm), F32))
        out_specs.append(pl.BlockSpec((1, FOX_HEADS, tm), lambda i: (i // n_pos_blocks, 0, i % n_pos_blocks)))
        out_shape.append(jax.ShapeDtypeStruct((M, LANES), F32))
        out_specs.append(pl.BlockSpec((tm, LANES), lambda i: (i, 0)))
    return pl.pallas_call(
        functools.partial(_proj_kernel, row_segs=tuple(row_segs), col_segs=tuple(col_segs), n_gate=n_gate,
                          with_logf=logf is not None, col_from=col_from),
        grid=(nt,), in_specs=in_specs, out_specs=out_specs, out_shape=out_shape, name=name,
        compiler_params=_cparams(("arbitrary",)))(*args)


def _fox_prep_kernel(k_ref, lf_ref, kaug_ref, c0_ref, carry_ref):
    j = pl.program_id(1)
    tc = k_ref.shape[1]

    @pl.when(j == 0)
    def _():
        carry_ref[...] = jnp.zeros_like(carry_ref)
        c0_ref[...] = jnp.zeros_like(c0_ref)

    lane = lax.broadcasted_iota(jnp.int32, (FOX_HEADS, LANES), 1)
    start = jnp.transpose(carry_ref[...])[0:FOX_HEADS, 0:1]
    c0_ref[0] = jnp.where(lane == j, start, c0_ref[0])

    r = lax.broadcasted_iota(jnp.int32, (tc, tc), 0)
    c = lax.broadcasted_iota(jnp.int32, (tc, tc), 1)
    tri = (c <= r).astype(BF16)
    hi, mid, lo = _split3(lf_ref[0])
    d = lambda p: jnp.dot(tri, p, preferred_element_type=F32)
    local = d(hi) + d(mid) + d(lo)
    carry_ref[...] = carry_ref[...] + jnp.broadcast_to(local[tc - 1:tc, :], carry_ref.shape)
    parts = jnp.concatenate(_split3(-local), axis=1)
    k = k_ref[0]
    kr = lax.broadcasted_iota(jnp.int32, (k.shape[1], LANES), 0)
    kc = lax.broadcasted_iota(jnp.int32, (k.shape[1], LANES), 1)
    pr = lax.broadcasted_iota(jnp.int32, (3 * LANES, LANES), 0)
    pc = lax.broadcasted_iota(jnp.int32, (3 * LANES, LANES), 1)
    for h in range(FOX_HEADS):
        place_k = ((kr == HEAD_DIM * h + kc) & (kc < HEAD_DIM)).astype(BF16)
        place_c = ((pr % LANES == h) & (pc == HEAD_DIM + pr // LANES)).astype(BF16)
        kaug_ref[0, h] = (jnp.dot(k, place_k, preferred_element_type=F32)
                          + jnp.dot(parts, place_c, preferred_element_type=F32)).astype(BF16)


def _fox_prep(k_b, lf_rows, tc):
    B, S, W = k_b.shape
    assert S // tc <= LANES
    return pl.pallas_call(
        _fox_prep_kernel, grid=(B, S // tc),
        in_specs=[pl.BlockSpec((1, tc, W), lambda b, j: (b, j, 0)),
                  pl.BlockSpec((1, tc, LANES), lambda b, j: (b, j, 0))],
        out_specs=[pl.BlockSpec((1, FOX_HEADS, tc, LANES), lambda b, j: (b, 0, j, 0)),
                   pl.BlockSpec((1, FOX_HEADS, LANES), lambda b, j: (b, 0, 0))],
        out_shape=[jax.ShapeDtypeStruct((B, FOX_HEADS, S, LANES), BF16),
                   jax.ShapeDtypeStruct((B, FOX_HEADS, LANES), F32)], name="fox_prep",
        scratch_shapes=[pltpu.VMEM((SUBLANES, LANES), F32)],
        compiler_params=_cparams(("parallel", "arbitrary")))(k_b, lf_rows)


def _fox_kernel(qi_ref, ki_ref, qt_ref, kaug_ref, vt_ref, c0_ref, o_ref, qa_ref, m_ref, l_ref, acc_ref, *, nh):
    hg, step = pl.program_id(1), pl.program_id(2)
    qi, ki = qi_ref[step], ki_ref[step]
    tq, tk = qt_ref.shape[2], kaug_ref.shape[2]
    ratio = tq // tk
    lane1 = lax.broadcasted_iota(jnp.int32, (1, LANES), 1)

    @pl.when(ki == 0)
    def _():
        row = lax.broadcasted_iota(jnp.int32, (HEAD_DIM, tq), 0)
        ones = jnp.where(row < 3, 1.0, 0.0).astype(BF16)
        for h in range(nh):
            qa_ref[h] = jnp.concatenate([qt_ref[0, h * HEAD_DIM:(h + 1) * HEAD_DIM, :], ones], axis=0)
        _online_init(m_ref, l_ref, acc_ref)

    def tile(masked):
        if masked:
            live = (ki * tk + lax.broadcasted_iota(jnp.int32, (tk, tq), 0)
                    <= qi * tq + lax.broadcasted_iota(jnp.int32, (tk, tq), 1))
        for h in range(nh):
            c0 = c0_ref[0, pl.ds(nh * hg + h, 1), :]
            delta = jnp.sum(jnp.where(lane1 == ratio * qi, c0, 0.0) - jnp.where(lane1 == ki, c0, 0.0),
                            axis=1, keepdims=True)
            s = jnp.dot(kaug_ref[0, h], qa_ref[h], preferred_element_type=F32)
            if masked:
                s = jnp.where(live, s, NEG)
            m_prev = m_ref[h]
            m_new = jnp.maximum(m_prev, jnp.max(s, axis=0, keepdims=True) + delta)
            p = jnp.exp(s - (m_new - delta))
            alpha = jnp.exp(m_prev - m_new)
            l_ref[h] = alpha * l_ref[h] + jnp.sum(p, axis=0, keepdims=True)
            acc_ref[h] = alpha * acc_ref[h] + jnp.dot(vt_ref[0, h * HEAD_DIM:(h + 1) * HEAD_DIM, :],
                                                      p.astype(BF16), preferred_element_type=F32)
            m_ref[h] = m_new

    pl.when(ki < ratio * qi)(lambda: tile(False))
    pl.when(ki >= ratio * qi)(lambda: tile(True))

    @pl.when(ki == ratio * (qi + 1) - 1)
    def _():
        for h in range(nh):
            o_ref[0, h * HEAD_DIM:(h + 1) * HEAD_DIM, :] = (acc_ref[h] / l_ref[h]).astype(BF16)


def _fox_prompt(qt_b, kaug, vt_b, c0, tk, nh=8):
    B, W, S = qt_b.shape
    tq = _tile(S, 2 * tk)
    ratio = tq // tk
    pairs = [(q, k) for q in range(S // tq) for k in range(ratio * (q + 1))]
    qi_tab = jnp.asarray([p[0] for p in pairs], jnp.int32)
    ki_tab = jnp.asarray([p[1] for p in pairs], jnp.int32)
    hw = nh * HEAD_DIM
    grid_spec = pltpu.PrefetchScalarGridSpec(
        num_scalar_prefetch=2, grid=(B, W // hw, len(pairs)),
        in_specs=[pl.BlockSpec((1, hw, tq), lambda b, hg, s, qi, ki: (b, hg, qi[s])),
                  pl.BlockSpec((1, nh, tk, LANES), lambda b, hg, s, qi, ki: (b, hg, ki[s], 0)),
                  pl.BlockSpec((1, hw, tk), lambda b, hg, s, qi, ki: (b, hg, ki[s])),
                  pl.BlockSpec((1, FOX_HEADS, LANES), lambda b, hg, s, qi, ki: (b, 0, 0))],
        out_specs=pl.BlockSpec((1, hw, tq), lambda b, hg, s, qi, ki: (b, hg, qi[s])),
        scratch_shapes=[pltpu.VMEM((nh, LANES, tq), BF16), pltpu.VMEM((nh, 1, tq), F32),
                        pltpu.VMEM((nh, 1, tq), F32), pltpu.VMEM((nh, HEAD_DIM, tq), F32)])
    return pl.pallas_call(
        functools.partial(_fox_kernel, nh=nh), grid_spec=grid_spec,
        out_shape=jax.ShapeDtypeStruct(qt_b.shape, BF16), name="fox_prompt",
        compiler_params=_cparams(("parallel", "parallel", "arbitrary")))(qi_tab, ki_tab, qt_b, kaug, vt_b, c0)


def _cmp_compute(x, pea_ref, peb_ref, wa_ref, wb_ref, w2t_ref, o_ref, carry_ref):
    n = x.shape[0]
    a = jnp.dot((x + pea_ref[...]).astype(BF16), wa_ref[...], preferred_element_type=F32)
    b = jnp.dot((x + peb_ref[...]).astype(BF16), wb_ref[...], preferred_element_type=F32)
    rowi = lax.broadcasted_iota(jnp.int32, a.shape, 0)
    a_prev = jnp.where(rowi == 0, carry_ref[0:1, :], pltpu.roll(a, 1, 0))
    carry_ref[...] = jnp.broadcast_to(a[n - 1:n, :], carry_ref.shape)
    o_ref[0] = _mm_nt(w2t_ref[...], _silu(a_prev + b))


def _cmp_kernel(x_ref, pea_ref, peb_ref, wa_ref, wb_ref, w2t_ref, o_ref, carry_ref):
    @pl.when(pl.program_id(1) == 0)
    def _():
        carry_ref[...] = jnp.zeros_like(carry_ref)
    _cmp_compute(x_ref[0], pea_ref, peb_ref, wa_ref, wb_ref, w2t_ref, o_ref, carry_ref)


def _cmp_paged_kernel(pt_ref, *refs, n_pg):
    pages = refs[:n_pg]
    pea_ref, peb_ref, wa_ref, wb_ref, w2t_ref, o_ref, xs_ref, carry_ref = refs[n_pg:]
    page = pages[0].shape[3]

    @pl.when(pl.program_id(1) == 0)
    def _():
        carry_ref[...] = jnp.zeros_like(carry_ref)

    for j, r in enumerate(pages):
        xs_ref[j * page:(j + 1) * page, :] = jnp.transpose(r[0].reshape(LANES, page))
    n = n_pg * page // NSA_CMP_STRIDE
    x = jnp.concatenate([xs_ref[pl.ds(l, n, stride=NSA_CMP_STRIDE), :] for l in range(NSA_CMP_STRIDE)], axis=1)
    a = jnp.dot((x + pea_ref[...]).astype(BF16), wa_ref[...], preferred_element_type=F32)
    b = jnp.dot((x + peb_ref[...]).astype(BF16), wb_ref[...], preferred_element_type=F32)
    rowi = lax.broadcasted_iota(jnp.int32, a.shape, 0)
    a_prev = jnp.where(rowi == 0, carry_ref[0:1, :], pltpu.roll(a, 1, 0))
    carry_ref[...] = jnp.broadcast_to(a[n - 1:n, :], carry_ref.shape)
    o_ref[0] = _mm_nt(w2t_ref[...], _silu(a_prev + b))


def _cmp_weights(pe, w1, w2):
    eye = jnp.eye(NSA_KV_HEADS, dtype=F32)
    hid = w1.shape[2]
    half = NSA_CMP_STRIDE

    def wpart(w):
        return jnp.einsum('lde,hg->lhdge', w, eye).reshape(half * LANES, NSA_KV_HEADS * hid).astype(BF16)

    def ppart(p):
        return jnp.broadcast_to(p[:, None, :], (half, NSA_KV_HEADS, HEAD_DIM)).reshape(1, half * LANES)

    w2t = jnp.einsum('ed,hg->gdhe', w2, eye).reshape(LANES, NSA_KV_HEADS * hid).astype(BF16)
    return ppart(pe[:half]), ppart(pe[half:]), wpart(w1[:half]), wpart(w1[half:]), w2t


def _compress(x, weights, name):
    N, L, _ = x.shape
    n_ch = L // NSA_CMP_STRIDE
    xc = x[:, :n_ch * NSA_CMP_STRIDE].reshape(N, n_ch, NSA_CMP_STRIDE * LANES)
    tch = _tile(n_ch, 256)
    wspecs = [pl.BlockSpec(w.shape, lambda n, j: (0, 0)) for w in weights]
    return pl.pallas_call(
        _cmp_kernel, grid=(N, n_ch // tch),
        in_specs=[pl.BlockSpec((1, tch, xc.shape[2]), lambda n, j: (n, j, 0))] + wspecs,
        out_specs=pl.BlockSpec((1, LANES, tch), lambda n, j: (n, 0, j)),
        out_shape=jax.ShapeDtypeStruct((N, LANES, n_ch), F32), name=name,
        scratch_shapes=[pltpu.VMEM((8, weights[2].shape[1]), F32)],
        compiler_params=_cparams(("parallel", "arbitrary")))(xc, *weights)


def _compress_paged(cache, pt_flat, n_db, weights, name):
    n_pool, page = cache.shape[:2]
    assert page == LANES
    rows = page // NSA_CMP_STRIDE
    ct = jnp.transpose(cache, (0, 2, 3, 1))
    n_pages = pt_flat.shape[0] // n_db
    n_pg = math.gcd(n_pages, max(1, 256 // rows))
    wa = weights[2]
    wspecs = [pl.BlockSpec(w.shape, lambda b, p, pt, nd=w.ndim: (0,) * nd) for w in weights]
    page_spec = lambda j: pl.BlockSpec((1,) + ct.shape[1:],
                                       lambda b, p, pt: (pt[b * n_pages + p * n_pg + j], 0, 0, 0))
    grid_spec = pltpu.PrefetchScalarGridSpec(
        num_scalar_prefetch=1, grid=(n_db, n_pages // n_pg),
        in_specs=[page_spec(j) for j in range(n_pg)] + wspecs,
        out_specs=pl.BlockSpec((1, LANES, n_pg * rows), lambda b, p, pt: (b, 0, p)),
        scratch_shapes=[pltpu.VMEM((n_pg * page, LANES), F32), pltpu.VMEM((8, wa.shape[1]), F32)])
    return pl.pallas_call(
        functools.partial(_cmp_paged_kernel, n_pg=n_pg), grid_spec=grid_spec,
        out_shape=jax.ShapeDtypeStruct((n_db, LANES, n_pages * rows), F32), name=name,
        compiler_params=_cparams(("parallel", "arbitrary")))(pt_flat, *([ct] * n_pg), *weights)


def _slc_map_rows(n_ch, ns_pad):
    i = (np.arange(n_ch)[:, None] - 1) * NSA_CMP_STRIDE
    j = np.arange(ns_pad)[None, :] * NSA_SLC_BLOCK
    shared = np.minimum(i + NSA_CMP_LEN, j + NSA_SLC_BLOCK) - np.maximum(i, j)
    m = np.clip(shared, 0, None) / NSA_CMP_LEN
    m[0, :] = 0.0
    return m.astype(np.float32)


def _select_blocks(imp, qpos, ns, n_sel):
    blk = lax.broadcasted_iota(jnp.int32, imp.shape, 1)
    cur = qpos // NSA_SLC_BLOCK
    valid = blk * NSA_SLC_BLOCK <= qpos
    forced = (blk == 0) | (blk == cur) | (blk == cur - 1)
    score = jnp.where(valid, imp + jnp.where(forced, NSA_FORCE_BONUS, 0.0), NEG)
    rank = jnp.zeros(imp.shape, jnp.int32)
    for jp in range(ns):
        sj = score[:, jp:jp + 1]
        beats = (sj > score) | ((sj == score) & (blk > jp))
        rank = rank + beats.astype(jnp.int32)
    return rank < n_sel


def _select_blocks_cols(imp, qpos, ns, n_sel):
    blk = lax.broadcasted_iota(jnp.int32, imp.shape, 0)
    cur = qpos // NSA_SLC_BLOCK
    valid = blk * NSA_SLC_BLOCK <= qpos
    forced = (blk == 0) | (blk == cur) | (blk == cur - 1)
    score = jnp.where(valid, imp + jnp.where(forced, NSA_FORCE_BONUS, 0.0), NEG)
    rank = jnp.zeros(imp.shape, jnp.int32)
    for jp in range(ns):
        sj = score[jp:jp + 1, :]
        beats = (sj > score) | ((sj == score) & (blk > jp))
        rank = rank + beats.astype(jnp.int32)
    return rank < n_sel


def _softmax_cols(s, mask):
    sm = jnp.where(mask, s, NEG)
    m = jnp.max(sm, axis=0, keepdims=True)
    p = jnp.where(mask, jnp.exp(sm - m), 0.0)
    l = jnp.sum(p, axis=0, keepdims=True)
    return p / jnp.where(l > 0.0, l, 1.0)


def _nsa_kernel(qt_ref, kct_ref, vct_ref, ks_ref, vst_ref, kw_ref, vwt_ref, gt_ref, mapt_ref, o_ref,
                qaug_ref, negm_ref, m_ref, l_ref, acc_ref, *, ns, n_sel, tk, win):
    g, i = pl.program_id(1), pl.program_id(2)
    tq = qt_ref.shape[2]
    nl = NSA_GROUP * tq
    n_ch = kct_ref.shape[3]
    per_tile = tk // NSA_SLC_BLOCK
    st = i * tq
    qpos1 = st + lax.broadcasted_iota(jnp.int32, (1, tq), 1)
    rep = lambda a: jnp.concatenate([a] * NSA_GROUP, axis=1)
    qpos = rep(qpos1)

    q4t = jnp.concatenate([qt_ref[0, j * HEAD_DIM:(j + 1) * HEAD_DIM, :] for j in range(NSA_GROUP)], axis=1)
    row = lax.broadcasted_iota(jnp.int32, (LANES, nl), 0)
    qaug_ref[0:LANES, :] = jnp.where(row // HEAD_DIM == g, jnp.concatenate([q4t, q4t], axis=0),
                                     jnp.zeros((LANES, nl), BF16))
    qaug_ref[LANES:2 * LANES, :] = jnp.zeros((LANES, nl), BF16)

    cidx = lax.broadcasted_iota(jnp.int32, (n_ch, tq), 0)
    cmask = rep(((cidx - 1) * NSA_CMP_STRIDE + NSA_CMP_LEN - 1 <= qpos1) & (cidx >= 1))
    sc = lax.dot_general(kct_ref[0, 0].astype(BF16), q4t, (((0,), (0,)), ((), ())), preferred_element_type=F32)
    pc = _softmax_cols(sc, cmask)
    oc = _mm(vct_ref[0, 0], pc)
    pstack = jnp.concatenate([pc[:, j * tq:(j + 1) * tq] for j in range(NSA_GROUP)], axis=0)
    imp = jnp.dot(mapt_ref[...], pstack.astype(BF16), preferred_element_type=F32)[0:HEAD_DIM]
    sel = _select_blocks_cols(imp, qpos1, ns, n_sel)

    negm_ref[...] = rep(jnp.where(sel, 0.0, -MASK_BIG))
    _online_init(m_ref, l_ref, acc_ref)
    onehot = ((lax.broadcasted_iota(jnp.int32, (tk, LANES), 0) // NSA_SLC_BLOCK)
              == lax.broadcasted_iota(jnp.int32, (tk, LANES), 1)).astype(BF16)

    def tile(kt, diagonal):
        k0 = pl.multiple_of(kt * tk, tk)
        nm = negm_ref[pl.ds(pl.multiple_of(kt * per_tile, per_tile), per_tile), :]
        qaug_ref[LANES:LANES + 16, :] = jnp.concatenate(
            [nm, jnp.zeros((16 - per_tile, nl), F32)], axis=0).astype(BF16)
        kaug = jnp.concatenate([ks_ref[0, pl.ds(k0, tk), :], onehot], axis=1)
        s = jnp.dot(kaug, qaug_ref[...], preferred_element_type=F32)
        if diagonal:
            kpos = k0 + lax.broadcasted_iota(jnp.int32, (tk, nl), 0)
            s = jnp.where(kpos <= qpos, s, NEG)
        m_prev = m_ref[...]
        m_new = jnp.maximum(m_prev, jnp.max(s, axis=0, keepdims=True))
        alpha = jnp.exp(m_prev - m_new)
        p = jnp.exp(s - m_new)
        l_ref[...] = alpha * l_ref[...] + jnp.sum(p, axis=0, keepdims=True)
        acc_ref[...] = alpha * acc_ref[...] + jnp.dot(vst_ref[0, :, pl.ds(k0, tk)], p.astype(BF16),
                                                      preferred_element_type=F32)
        m_ref[...] = m_new

    last = st // tk

    def body(kt, carry):
        tile(kt, False)
        return carry

    lax.fori_loop(0, last, body, 0)
    tile(last, True)
    osl = acc_ref[...] / l_ref[...]

    w0 = pl.multiple_of(jnp.maximum(st + tq - win, 0), tq)
    dist = qpos1 - (w0 + lax.broadcasted_iota(jnp.int32, (win, tq), 0))
    sw = jnp.dot(kw_ref[0, pl.ds(w0, win), :], qaug_ref[0:LANES, :], preferred_element_type=F32)
    pw = _softmax_cols(sw, rep((dist >= 0) & (dist <= NSA_WINDOW)))
    ow = jnp.dot(vwt_ref[0, :, pl.ds(w0, win)], pw.astype(BF16), preferred_element_type=F32)

    for j in range(NSA_GROUP):
        base = (NSA_GROUP * g + j) * 3
        gate = lambda r: gt_ref[0, pl.ds(base + r, 1), :]
        cols = slice(j * tq, (j + 1) * tq)
        o = gate(0) * oc[:, cols] + gate(1) * osl[:, cols] + gate(2) * ow[:, cols]
        o_ref[0, j * HEAD_DIM:(j + 1) * HEAD_DIM, :] = o.astype(BF16)


def _nsa_prompt(qt_b, kct, vct, ks_b, vst_b, kw_b, vwt_b, gates_t):
    B, W, S = qt_b.shape
    n_ch = kct.shape[3]
    ns = -(-S // NSA_SLC_BLOCK)
    assert ns <= HEAD_DIM
    n_sel = min(NSA_TOPN, ns)
    tq = _tile(S, 256)
    tk = _tile(S, 512)
    assert tk // NSA_SLC_BLOCK <= 16
    win = min(NSA_WINDOW + tq, S)
    nl = NSA_GROUP * tq
    smap_t = jnp.asarray(np.tile(_slc_map_rows(n_ch, LANES).T, (1, NSA_GROUP)), dtype=BF16)
    per_head = lambda a: pl.BlockSpec((1, 1) + a.shape[2:], lambda b, g, i: (b, g, 0, 0))
    gw = W // NSA_KV_HEADS
    return pl.pallas_call(
        functools.partial(_nsa_kernel, ns=ns, n_sel=n_sel, tk=tk, win=win),
        grid=(B, NSA_KV_HEADS, S // tq),
        in_specs=[pl.BlockSpec((1, gw, tq), lambda b, g, i: (b, g, i)),
                  per_head(kct), per_head(vct),
                  pl.BlockSpec((1, S, LANES), lambda b, g, i: (b, 0, 0)),
                  pl.BlockSpec((1, HEAD_DIM, S), lambda b, g, i: (b, g, 0)),
                  pl.BlockSpec((1, S, LANES), lambda b, g, i: (b, 0, 0)),
                  pl.BlockSpec((1, HEAD_DIM, S), lambda b, g, i: (b, g, 0)),
                  pl.BlockSpec((1, gates_t.shape[1], tq), lambda b, g, i: (b, 0, i)),
                  pl.BlockSpec(smap_t.shape, lambda b, g, i: (0, 0))],
        out_specs=pl.BlockSpec((1, gw, tq), lambda b, g, i: (b, g, i)),
        out_shape=jax.ShapeDtypeStruct(qt_b.shape, BF16), name="nsa_prompt",
        scratch_shapes=[pltpu.VMEM((2 * LANES, nl), BF16), pltpu.VMEM((HEAD_DIM, nl), F32),
                        pltpu.VMEM((1, nl), F32), pltpu.VMEM((1, nl), F32), pltpu.VMEM((HEAD_DIM, nl), F32)],
        compiler_params=_cparams(("parallel", "parallel", "arbitrary")))(
            qt_b, kct, vct, ks_b, vst_b, kw_b, vwt_b, gates_t, smap_t)


def _outproj_kernel(*refs):
    x_ref, w_ref, y_ref = refs[0], refs[-2], refs[-1]
    y = x_ref[...]
    k0 = 0
    for o_ref in refs[1:-2]:
        if len(o_ref.shape) == 3:
            kw = o_ref.shape[1]
            y = y + lax.dot_general(o_ref[0].astype(BF16), w_ref[k0:k0 + kw, :], (((0,), (0,)), ((), ())),
                                    preferred_element_type=F32)
        else:
            kw = o_ref.shape[1]
            y = y + jnp.dot(o_ref[...].astype(BF16), w_ref[k0:k0 + kw, :], preferred_element_type=F32)
        k0 += kw
    y_ref[...] = y


def _outproj(x2d, parts, w_bf, name, tm=None):
    M, D = x2d.shape
    tm = tm or _tile(M, 512)
    specs = []
    for o in parts:
        if o.ndim == 3:
            npb = o.shape[2] // tm
            specs.append(pl.BlockSpec((1, o.shape[1], tm), lambda i, npb=npb: (i // npb, 0, i % npb)))
        else:
            specs.append(pl.BlockSpec((tm, o.shape[1]), lambda i: (i, 0)))
    return pl.pallas_call(
        _outproj_kernel, grid=(M // tm,),
        in_specs=[pl.BlockSpec((tm, D), lambda i: (i, 0))] + specs + [pl.BlockSpec(w_bf.shape, lambda i: (0, 0))],
        out_specs=pl.BlockSpec((tm, D), lambda i: (i, 0)),
        out_shape=jax.ShapeDtypeStruct((M, D), F32), name=name,
        compiler_params=_cparams(("parallel",)))(x2d, *parts, w_bf)


def _ffn_kernel(x_ref, g_ref, wg_ref, wu_ref, wd_ref, y_ref, xn_ref):
    f = pl.program_id(1)

    @pl.when(f == 0)
    def _():
        x = x_ref[...]
        xn_ref[...] = _rmsnorm(x, g_ref[...]).astype(BF16)
        y_ref[...] = x

    xn = xn_ref[...]
    h = _silu(jnp.dot(xn, wg_ref[...], preferred_element_type=F32)) * \
        jnp.dot(xn, wu_ref[...], preferred_element_type=F32)
    y_ref[...] += jnp.dot(h.astype(BF16), wd_ref[...], preferred_element_type=F32)


def _ffn(x2d, gain, wg, wu, wd, name):
    M, D = x2d.shape
    Fd = wg.shape[1]
    tm = _tile(M, 512)
    nf = 2 if Fd % (2 * LANES) == 0 else 1
    fc = Fd // nf
    return pl.pallas_call(
        _ffn_kernel, grid=(M // tm, nf),
        in_specs=[pl.BlockSpec((tm, D), lambda i, f: (i, 0)), pl.BlockSpec((1, D), lambda i, f: (0, 0)),
                  pl.BlockSpec((D, fc), lambda i, f: (0, f)), pl.BlockSpec((D, fc), lambda i, f: (0, f)),
                  pl.BlockSpec((fc, D), lambda i, f: (f, 0))],
        out_specs=pl.BlockSpec((tm, D), lambda i, f: (i, 0)),
        out_shape=jax.ShapeDtypeStruct((M, D), F32), name=name,
        scratch_shapes=[pltpu.VMEM((tm, D), BF16)],
        compiler_params=_cparams(("parallel", "arbitrary")))(x2d, gain.reshape(1, D), wg, wu, wd)


def _moe_kernel(x_ref, g_ref, wrt_ref, wg_ref, wu_ref, wd_ref, gf_ref, y_ref, xn_ref, rank_ref, comb_ref,
                rankc_ref, *, n_exp, chunk):
    e = pl.program_id(1)
    tm = x_ref.shape[0]

    @pl.when(e == 0)
    def _():
        xn = _rmsnorm(x_ref[...], g_ref[...])
        xn_ref[...] = xn.astype(BF16)
        xh, xm, _ = _split3(xn)
        wh, wm, _ = _split3(wrt_ref[...])
        dn = lambda a, b: lax.dot_general(a, b, (((1,), (1,)), ((), ())), preferred_element_type=F32)
        row = lax.broadcasted_iota(jnp.int32, (LANES, tm), 0)
        logits = jnp.where(row < n_exp, dn(wh, xh) + dn(wh, xm) + dn(wm, xh), NEG)
        v1 = jnp.max(logits, axis=0, keepdims=True)
        i1 = jnp.min(jnp.where(logits == v1, row, LANES), axis=0, keepdims=True)
        rest = jnp.where(row == i1, NEG, logits)
        v2 = jnp.max(rest, axis=0, keepdims=True)
        i2 = jnp.min(jnp.where(rest == v2, row, LANES), axis=0, keepdims=True)
        ex = jnp.exp(v2 - v1)
        comb = jnp.where(row == i1, 1.0 / (1.0 + ex), jnp.where(row == i2, ex / (1.0 + ex), 0.0))
        member = ((row == i1) | (row == i2))[0:SUBLANES]
        before = (lax.broadcasted_iota(jnp.int32, (tm, tm), 0)
                  < lax.broadcasted_iota(jnp.int32, (tm, tm), 1)).astype(BF16)
        rank = jnp.dot(jnp.where(member, 1.0, 0.0).astype(BF16), before, preferred_element_type=F32)
        rank = jnp.where(member, rank, -1.0)
        rank_ref[...] = rank
        comb_ref[...] = comb[0:SUBLANES]
        rankc_ref[...] = jnp.transpose(rank)
        y_ref[...] = jnp.zeros_like(y_ref)

    rrow = rank_ref[pl.ds(e, 1), :]
    crow = comb_ref[pl.ds(e, 1), :]
    rc = rankc_ref[...]
    rcol = jnp.sum(jnp.where(lax.broadcasted_iota(jnp.int32, rc.shape, 1) == e, rc, 0.0), axis=1, keepdims=True)
    n_tok = jnp.max(rrow).astype(jnp.int32) + 1

    def body(c, carry):
        base = (c * chunk).astype(F32)
        pick = rrow == lax.broadcasted_iota(jnp.int32, (chunk, tm), 0).astype(F32) + base
        xg = jnp.dot(jnp.where(pick, 1.0, 0.0).astype(BF16), xn_ref[...],
                     preferred_element_type=F32).astype(BF16)
        wcol = jnp.sum(jnp.where(pick, crow, 0.0), axis=1, keepdims=True)
        h = _silu(jnp.dot(xg, wg_ref[0], preferred_element_type=F32)) * \
            jnp.dot(xg, wu_ref[0], preferred_element_type=F32)
        yv = jnp.dot((h * wcol).astype(BF16), wd_ref[0], preferred_element_type=F32)
        place = jnp.where(rcol == lax.broadcasted_iota(jnp.int32, (tm, chunk), 1).astype(F32) + base,
                          1.0, 0.0).astype(BF16)
        y_ref[...] += jnp.dot(place, yv.astype(BF16), preferred_element_type=F32)
        return carry

    lax.fori_loop(0, (n_tok + chunk - 1) // chunk, body, 0)

    @pl.when(e == n_exp - 1)
    def _():
        y_ref[...] = _rmsnorm(x_ref[...] + y_ref[...], gf_ref[...])


def _moe_final(x2d, gain, w_router, wg, wu, wd, gain_final, name):
    M, D = x2d.shape
    n_exp, _, Fd = wg.shape
    assert n_exp <= SUBLANES
    tm = _tile(M, 1024)
    chunk = min(288, tm) if tm >= 1024 else min(128, tm)
    wrt = jnp.zeros((LANES, D), F32).at[:n_exp].set(w_router.T)
    return pl.pallas_call(
        functools.partial(_moe_kernel, n_exp=n_exp, chunk=chunk), grid=(M // tm, n_exp),
        in_specs=[pl.BlockSpec((tm, D), lambda i, e: (i, 0)), pl.BlockSpec((1, D), lambda i, e: (0, 0)),
                  pl.BlockSpec((LANES, D), lambda i, e: (0, 0)),
                  pl.BlockSpec((1, D, Fd), lambda i, e: (e, 0, 0)),
                  pl.BlockSpec((1, D, Fd), lambda i, e: (e, 0, 0)),
                  pl.BlockSpec((1, Fd, D), lambda i, e: (e, 0, 0)),
                  pl.BlockSpec((1, D), lambda i, e: (0, 0))],
        out_specs=pl.BlockSpec((tm, D), lambda i, e: (i, 0)),
        out_shape=jax.ShapeDtypeStruct((M, D), F32), name=name,
        scratch_shapes=[pltpu.VMEM((tm, D), BF16), pltpu.VMEM((SUBLANES, tm), F32),
                        pltpu.VMEM((SUBLANES, tm), F32), pltpu.VMEM((tm, SUBLANES), F32)],
        compiler_params=_cparams(("parallel", "arbitrary")))(
            x2d, gain.reshape(1, D), wrt, wg, wu, wd, gain_final.reshape(1, D))


def _band_kernel(*refs, n_hp, dil, span, has_prev):
    group = lambda i: refs[i * n_hp:(i + 1) * n_hp]
    if has_prev:
        q_refs, kp_refs, kc_refs, vp_refs, vc_refs = [group(i) for i in range(5)]
        bias_ref, o_ref, lse_ref, o_scr, lse_scr = refs[5 * n_hp:]
    else:
        q_refs, kc_refs, vc_refs = [group(i) for i in range(3)]
        bias_ref, o_ref, lse_ref, o_scr, lse_scr = refs[3 * n_hp:]
    t, hs = pl.program_id(1), pl.program_id(2)
    tq = q_refs[0].shape[1] // dil
    band = bias_ref[...]
    sw = band.shape[1]
    rowk = lax.broadcasted_iota(jnp.int32, band.shape, 0)
    band_first = band + jnp.where((rowk < span) & (t == 0), NEG, 0.0)
    row = lax.broadcasted_iota(jnp.int32, (LANES, sw), 0)
    first_head = hs * (2 * n_hp)

    @pl.when(hs == 0)
    def _():
        lse_ref[...] = jnp.zeros_like(lse_ref)

    def stream(r, carry):
        rows = lambda n: pl.ds(r, n, stride=dil) if dil > 1 else pl.ds(0, n)
        lses = []
        for j in range(n_hp):
            q = q_refs[j][0, rows(tq), :]
            qTs = [jnp.transpose(q[qs * sw:(qs + 1) * sw]).astype(BF16) for qs in range(tq // sw)]
            k, v = kc_refs[j][0, rows(tq), :], vc_refs[j][0, rows(tq), :]
            if has_prev:
                k = jnp.concatenate([kp_refs[j][0, rows(span), :], k], axis=0)
                v = jnp.concatenate([vp_refs[j][0, rows(span), :], v], axis=0)
            k = k.astype(BF16)
            vT = jnp.transpose(v).astype(BF16)
            outs = []
            for h in range(2):
                o_parts, lse_parts = [], []
                for qs in range(tq // sw):
                    qpad = jnp.where(row // HEAD_DIM == h, qTs[qs], jnp.zeros_like(qTs[qs]))
                    if has_prev:
                        k0, nk, b = qs * sw, span + sw, (band_first if qs == 0 else band)
                    elif qs == 0 and tq == sw:
                        k0, nk, b = 0, sw, band[span:, :]
                    elif qs == 0:
                        k0, nk, b = 0, span + sw, jnp.concatenate([band[span:, :], jnp.full((span, sw), NEG, F32)], axis=0)
                    else:
                        k0, nk, b = qs * sw - span, span + sw, band
                    s = jnp.dot(k[k0:k0 + nk], qpad, preferred_element_type=F32) + b
                    m = jnp.max(s, axis=0, keepdims=True)
                    p = jnp.exp(s - m)
                    l = jnp.sum(p, axis=0, keepdims=True)
                    o_parts.append(jnp.dot(vT[h * HEAD_DIM:(h + 1) * HEAD_DIM, k0:k0 + nk], p.astype(BF16),
                                           preferred_element_type=F32) / l)
                    lse_parts.append(m + jnp.log(l))
                outs.append(jnp.concatenate(o_parts, axis=1))
                lses.append(jnp.concatenate(lse_parts, axis=1))
            o_scr[j, rows(tq), :] = jnp.transpose(jnp.concatenate(outs, axis=0))
        stat = jnp.transpose(jnp.concatenate(lses + [jnp.zeros((LANES - len(lses), tq), F32)], axis=0))
        lse_scr[rows(tq), :] = pltpu.roll(stat, first_head, 1)
        return carry

    if dil == 1:
        stream(0, 0)
    else:
        def two_streams(r, carry):
            stream(r, carry)
            return stream(r + dil // 2, carry)
        lax.fori_loop(0, dil // 2, two_streams, 0)
    for j in range(n_hp):
        o_ref[0, :, j * LANES:(j + 1) * LANES] = o_scr[j]
    lane = lax.broadcasted_iota(jnp.int32, lse_scr.shape, 1)
    mine = (lane >= first_head) & (lane < first_head + 2 * n_hp)
    lse_ref[0] = jnp.where(mine, lse_scr[...], lse_ref[0])


def _band_attention(q, k, v, dil, span, tq, name, n_hp=2):
    B, S, W = q.shape
    L = S // dil
    tq = min(tq, L)
    assert L % tq == 0 and tq % span == 0
    per_tile = tq // span
    n_pairs = W // LANES
    assert n_pairs % n_hp == 0
    sw = tq
    i = np.arange(span + sw)[:, None]
    j = np.arange(sw)[None, :]
    bias = jnp.asarray(np.where((j - i + span >= 0) & (j - i + span <= span), 0.0, NEG), F32)
    cur = lambda jj: pl.BlockSpec((1, dil * tq, LANES), lambda b, t, hs: (b, t, hs * n_hp + jj))
    prev = lambda jj: pl.BlockSpec((1, dil * span, LANES),
                                   lambda b, t, hs: (b, jnp.maximum(t * per_tile - 1, 0), hs * n_hp + jj))
    each = lambda mk: [mk(jj) for jj in range(n_hp)]
    has_prev = L > tq
    if has_prev:
        specs = each(cur) + each(prev) + each(cur) + each(prev) + each(cur)
        args = [q] * n_hp + [k] * (2 * n_hp) + [v] * (2 * n_hp)
    else:
        specs = each(cur) * 3
        args = [q] * n_hp + [k] * n_hp + [v] * n_hp
    res = pl.pallas_call(
        functools.partial(_band_kernel, n_hp=n_hp, dil=dil, span=span, has_prev=has_prev),
        grid=(B, L // tq, n_pairs // n_hp),
        in_specs=specs + [pl.BlockSpec(bias.shape, lambda b, t, hs: (0, 0))],
        out_specs=[pl.BlockSpec((1, dil * tq, n_hp * LANES), lambda b, t, hs: (b, t, hs)),
                   pl.BlockSpec((1, dil * tq, LANES), lambda b, t, hs: (b, t, 0))],
        out_shape=[jax.ShapeDtypeStruct((B, S, W), F32), jax.ShapeDtypeStruct((B, S, LANES), F32)],
        scratch_shapes=[pltpu.VMEM((n_hp, dil * tq, LANES), F32), pltpu.VMEM((dil * tq, LANES), F32)],
        name=name, compiler_params=_cparams(("parallel", "parallel", "arbitrary")))(*args, bias)
    return res[0].reshape(B * S, W), res[1].reshape(B * S, LANES)


def _merge_outproj_kernel(*refs, n_groups):
    x_ref = refs[0]
    o_refs, l_refs = refs[1:1 + n_groups], refs[1 + n_groups:1 + 2 * n_groups]
    w_ref, y_ref = refs[-2], refs[-1]
    W = o_refs[0].shape[1]
    lses = [r[...] for r in l_refs]
    m = functools.reduce(jnp.maximum, lses)
    es = [jnp.exp(l - m) for l in lses]
    tot = functools.reduce(lambda a, b: a + b, es)
    expand = (lax.broadcasted_iota(jnp.int32, (LANES, W), 1) // HEAD_DIM
              == lax.broadcasted_iota(jnp.int32, (LANES, W), 0)).astype(BF16)
    mix = None
    for e, o_ref in zip(es, o_refs):
        hi, mid, _ = _split3(e / tot)
        wexp = jnp.dot(hi, expand, preferred_element_type=F32) + jnp.dot(mid, expand, preferred_element_type=F32)
        mix = wexp * o_ref[...] if mix is None else mix + wexp * o_ref[...]
    y_ref[...] = x_ref[...] + jnp.dot(mix.astype(BF16), w_ref[...], preferred_element_type=F32)


def _merge_outproj(x2d, outs, lses, w_bf, name):
    M, D = x2d.shape
    W = outs[0].shape[1]
    tm = _tile(M, 256)
    row = lambda n: pl.BlockSpec((tm, n), lambda i: (i, 0))
    return pl.pallas_call(
        functools.partial(_merge_outproj_kernel, n_groups=len(outs)), grid=(M // tm,),
        in_specs=[row(D)] + [row(W)] * len(outs) + [row(LANES)] * len(lses)
        + [pl.BlockSpec(w_bf.shape, lambda i: (0, 0))],
        out_specs=row(D), out_shape=jax.ShapeDtypeStruct((M, D), F32), name=name,
        compiler_params=_cparams(("parallel",)))(x2d, *outs, *lses, w_bf)


def _head_major(x, n_heads, pad_rows):
    DB, T, _ = x.shape
    y = x.reshape(DB, T, n_heads, HEAD_DIM).transpose(0, 2, 1, 3)
    return jnp.pad(y, ((0, 0), (0, 0), (0, pad_rows - T), (0, 0)))


def _new_cols(x, n_heads):
    DB, T, _ = x.shape
    y = x.reshape(DB, T, n_heads, HEAD_DIM).transpose(0, 2, 3, 1)
    return jnp.pad(y, ((0, 0), (0, 0), (0, 0), (0, NEW_PAD - T)))


def _row_of(col_vec_row):
    return jnp.transpose(jnp.broadcast_to(col_vec_row, (SUBLANES, LANES)))[0:SUBLANES, 0:1]


def _fox_dec_kernel(pt_ref, *refs, n_pg, n_q):
    kts, vts, lfs = refs[:n_pg], refs[n_pg:2 * n_pg], refs[2 * n_pg:3 * n_pg]
    q_ref, knt_ref, vnt_ref, lfn_ref, o_ref, qbd_ref, base_ref, carry_ref, m_ref, l_ref, acc_ref = refs[3 * n_pg:]
    p = pl.program_id(1)
    nh = FOX_HEADS
    page = LANES
    rows, width = nh * QPAD, nh * HEAD_DIM
    lane = lax.broadcasted_iota(jnp.int32, (QPAD, LANES), 1)
    trow = lax.broadcasted_iota(jnp.int32, (QPAD, LANES), 0)
    r = lax.broadcasted_iota(jnp.int32, (page, page), 0)
    c = lax.broadcasted_iota(jnp.int32, (page, page), 1)
    stack = lambda xs: jnp.concatenate(xs, axis=0)
    own = (lax.broadcasted_iota(jnp.int32, (rows, width), 0) // QPAD
           == lax.broadcasted_iota(jnp.int32, (rows, width), 1) // HEAD_DIM)
    heads_of = lambda ref: ref[0].reshape(width, ref.shape[3])

    @pl.when(p == 0)
    def _():
        _online_init(m_ref, l_ref, acc_ref)
        carry_ref[...] = jnp.zeros_like(carry_ref)
        q_all = q_ref[0].reshape(rows, HEAD_DIM) * SCALE
        qbd_ref[...] = jnp.where(own, jnp.concatenate([q_all] * nh, axis=1), 0.0).astype(BF16)
        cnew = _mm3_left(lfn_ref[0], r <= c)
        bases, decs = [], []
        for h in range(nh):
            bh = _row_of(cnew[h:h + 1, :])
            bases.append(bh)
            decs.append(bh - cnew[h:h + 1, :])
        base_ref[...] = jnp.broadcast_to(stack(bases), base_ref.shape)
        live = (lane <= trow) & (lane < n_q)
        s = jnp.dot(qbd_ref[...], heads_of(knt_ref).astype(BF16), preferred_element_type=F32) + stack(decs)
        s = jnp.where(stack([live] * nh), s, NEG)
        alpha, pr = _online_step(s, m_ref, l_ref)
        acc_ref[...] = alpha * acc_ref[...] + _mm_nt(pr, heads_of(vnt_ref))

    lf_all = stack([lf[0] for lf in lfs])
    after_all = _mm3_left(lf_all, r > c)
    tot_all = after_all[:, 0:1] + lf_all[:, 0:1]
    carry = carry_ref[:, 0:1]
    decay = [None] * n_pg
    for j in reversed(range(n_pg)):
        decay[j] = carry + after_all[j * nh:(j + 1) * nh]
        carry = carry + tot_all[j * nh:(j + 1) * nh]
    carry_ref[...] = jnp.broadcast_to(carry, carry_ref.shape)
    per_row = lambda d: stack([jnp.broadcast_to(d[h:h + 1, :], (QPAD, page)) for h in range(nh)])
    s = jnp.concatenate([jnp.dot(qbd_ref[...], heads_of(kts[j]).astype(BF16), preferred_element_type=F32)
                         + per_row(decay[j]) for j in range(n_pg)], axis=1) + base_ref[:, 0:1]
    alpha, pr = _online_step(s, m_ref, l_ref)
    acc_ref[...] = alpha * acc_ref[...] + sum(_mm_nt(pr[:, j * page:(j + 1) * page], heads_of(vts[j]))
                                              for j in range(n_pg))

    @pl.when(p == pl.num_programs(1) - 1)
    def _():
        acc = jnp.where(own, acc_ref[...], 0.0)
        o_ref[0] = sum(acc[:, h * HEAD_DIM:(h + 1) * HEAD_DIM] for h in range(nh)) / l_ref[...]


def _fox_decode(q, k_new, v_new, lf_new, cache_k, cache_v, cache_lf, pt_flat):
    DB, T, W = q.shape
    n_pool, page, nh, _ = cache_k.shape
    assert page == LANES and T <= QPAD
    n_pages = pt_flat.shape[0] // DB
    n_pg = math.gcd(n_pages, PAGES_PER_STEP)
    n_steps = n_pages // n_pg
    ckt = jnp.transpose(cache_k, (0, 2, 3, 1))
    cvt = jnp.transpose(cache_v, (0, 2, 3, 1))
    clf = jnp.transpose(cache_lf, (0, 2, 1))
    qh = _head_major(q, nh, QPAD)
    knt, vnt = _new_cols(k_new, nh), _new_cols(v_new, nh)
    lfn = jnp.pad(jnp.transpose(lf_new, (0, 2, 1)), ((0, 0), (0, 0), (0, LANES - T)))
    page_idx = lambda j: (lambda b, p, pt: (pt[b * n_pages + (n_steps - 1 - p) * n_pg + j], 0, 0, 0))
    lf_idx = lambda j: (lambda b, p, pt: (pt[b * n_pages + (n_steps - 1 - p) * n_pg + j], 0, 0))
    per_db = lambda a: pl.BlockSpec((1,) + a.shape[1:], lambda b, p, pt: (b,) + (0,) * (a.ndim - 1))
    rows = nh * QPAD
    grid_spec = pltpu.PrefetchScalarGridSpec(
        num_scalar_prefetch=1, grid=(DB, n_steps),
        in_specs=([pl.BlockSpec((1, nh, HEAD_DIM, page), page_idx(j)) for j in range(n_pg)] * 2
                  + [pl.BlockSpec((1, nh, page), lf_idx(j)) for j in range(n_pg)]
                  + [per_db(qh), per_db(knt), per_db(vnt), per_db(lfn)]),
        out_specs=pl.BlockSpec((1, rows, HEAD_DIM), lambda b, p, pt: (b, 0, 0)),
        scratch_shapes=[pltpu.VMEM((rows, W), BF16), pltpu.VMEM((rows, LANES), F32), pltpu.VMEM((nh, LANES), F32),
                        pltpu.VMEM((rows, 1), F32), pltpu.VMEM((rows, 1), F32), pltpu.VMEM((rows, W), F32)])
    o = pl.pallas_call(
        functools.partial(_fox_dec_kernel, n_pg=n_pg, n_q=T), grid_spec=grid_spec,
        out_shape=jax.ShapeDtypeStruct((DB, rows, HEAD_DIM), F32), name="fox_decode",
        compiler_params=_cparams(("parallel", "arbitrary")))(
            pt_flat, *([ckt] * n_pg), *([cvt] * n_pg), *([clf] * n_pg), qh, knt, vnt, lfn)
    return o.reshape(DB, nh, QPAD, HEAD_DIM)[:, :, :T].transpose(0, 2, 1, 3).reshape(DB, T, W)


def _nsa_dec_a_kernel(q_ref, kct_ref, vct_ref, kwt_ref, vwt_ref, kwnt_ref, vwnt_ref, gt_ref, map_ref,
                      o_ref, bias_ref, *, n_q, ns, n_sel, past, n_keys):
    nr = NSA_GROUP * QPAD
    n_ch = kct_ref.shape[3]
    wb = kwt_ref.shape[3]
    t_row = lax.broadcasted_iota(jnp.int32, (nr, 1), 0) % QPAD
    imps = []
    for kv in range(NSA_KV_HEADS):
        q = (q_ref[0, kv] * SCALE).astype(BF16)
        cidx = lax.broadcasted_iota(jnp.int32, (nr, n_ch), 1)
        pc = _softmax_rows(_mm(q, kct_ref[0, kv]), cidx >= 1)
        oc = _mm_nt(pc, vct_ref[0, kv])
        pcat = jnp.concatenate([pc[j * QPAD:(j + 1) * QPAD] for j in range(NSA_GROUP)], axis=1)
        imps.append(jnp.dot(pcat.astype(BF16), map_ref[...], preferred_element_type=F32))
        sw = _mm(q, kwt_ref[0, kv])
        sn = _mm(q, kwnt_ref[0, kv])
        iw = lax.broadcasted_iota(jnp.int32, (nr, wb), 1)
        un = lax.broadcasted_iota(jnp.int32, (nr, NEW_PAD), 1)
        mw = (wb + t_row - iw <= NSA_WINDOW) & (past - wb + iw >= 0)
        mn = (un <= t_row) & (un < n_q)
        sw = jnp.where(mw, sw, NEG)
        sn = jnp.where(mn, sn, NEG)
        m = jnp.maximum(jnp.max(sw, axis=1, keepdims=True), jnp.max(sn, axis=1, keepdims=True))
        pw = jnp.where(mw, jnp.exp(sw - m), 0.0)
        pn = jnp.where(mn, jnp.exp(sn - m), 0.0)
        l = jnp.sum(pw, axis=1, keepdims=True) + jnp.sum(pn, axis=1, keepdims=True)
        ow = (_mm_nt(pw, vwt_ref[0, kv]) + _mm_nt(pn, vwnt_ref[0, kv])) / l
        gt = gt_ref[0, kv]
        o_ref[0, kv] = gt[:, 0:1] * oc + gt[:, 2:3] * ow

    imp = jnp.concatenate(imps, axis=0)
    t_sel = lax.broadcasted_iota(jnp.int32, (imp.shape[0], 1), 0) % QPAD
    sel = _select_blocks(imp, past + t_sel, ns, n_sel).astype(BF16)
    ch = 8 * LANES
    for c0 in range(0, n_keys, ch):
        w = min(ch, n_keys - c0)
        key = c0 + lax.broadcasted_iota(jnp.int32, (sel.shape[1], w), 1)
        onehot = (key // NSA_SLC_BLOCK == lax.broadcasted_iota(jnp.int32, (sel.shape[1], w), 0)).astype(BF16)
        chosen = jnp.dot(sel, onehot, preferred_element_type=F32) > 0.5
        kpos = c0 + lax.broadcasted_iota(jnp.int32, (sel.shape[0], w), 1)
        ok = chosen & (kpos <= past + t_sel) & (kpos < past + n_q)
        bias_ref[0, :, c0:c0 + w] = jnp.where(ok, 0.0, NEG)


def _nsa_dec_b_kernel(pt_ref, *refs, n_pg):
    kts, vts = refs[:n_pg], refs[n_pg:2 * n_pg]
    q_ref, knt_ref, vnt_ref, bias_ref, biasn_ref, o_ref, qbd_ref, m_ref, l_ref, acc_ref = refs[2 * n_pg:]
    p = pl.program_id(1)
    nr = NSA_GROUP * QPAD
    page = LANES
    rows, width = NSA_KV_HEADS * nr, NSA_KV_HEADS * HEAD_DIM
    own = (lax.broadcasted_iota(jnp.int32, (rows, width), 0) // nr
           == lax.broadcasted_iota(jnp.int32, (rows, width), 1) // HEAD_DIM)
    heads_of = lambda ref: ref[0].reshape(width, ref.shape[3])
    per_row = lambda b: jnp.concatenate([b[kv * QPAD:(kv + 1) * QPAD] for kv in range(NSA_KV_HEADS)
                                         for _ in range(NSA_GROUP)], axis=0)

    @pl.when(p == 0)
    def _():
        _online_init(m_ref, l_ref, acc_ref)
        q_all = q_ref[0].reshape(rows, HEAD_DIM) * SCALE
        qbd_ref[...] = jnp.where(own, jnp.concatenate([q_all] * NSA_KV_HEADS, axis=1), 0.0).astype(BF16)

    s = jnp.concatenate([jnp.dot(qbd_ref[...], heads_of(kts[j]).astype(BF16), preferred_element_type=F32)
                         for j in range(n_pg)], axis=1) + per_row(bias_ref[0])
    alpha, pr = _online_step(s, m_ref, l_ref)
    acc_ref[...] = alpha * acc_ref[...] + sum(_mm_nt(pr[:, j * page:(j + 1) * page], heads_of(vts[j]))
                                              for j in range(n_pg))

    @pl.when(p == pl.num_programs(1) - 1)
    def _():
        s = jnp.dot(qbd_ref[...], heads_of(knt_ref).astype(BF16), preferred_element_type=F32) + per_row(biasn_ref[0])
        alpha, pr = _online_step(s, m_ref, l_ref)
        acc = jnp.where(own, alpha * acc_ref[...] + _mm_nt(pr, heads_of(vnt_ref)), 0.0)
        o_ref[0] = sum(acc[:, kv * HEAD_DIM:(kv + 1) * HEAD_DIM] for kv in range(NSA_KV_HEADS)) / l_ref[...]


def _nsa_decode(qb, kct, vct, cache_ks, cache_vs, ks_new, vs_new, swa_k, swa_v, kw_new, vw_new,
                gates, pt_flat, past):
    DB, T, W = qb.shape
    n_pool, page = cache_ks.shape[:2]
    assert page == LANES and past % LANES == 0 and T <= QPAD
    n_pages = pt_flat.shape[0] // DB
    n_ch = kct.shape[3]
    ns = -(-(past + T) // NSA_SLC_BLOCK)
    ns_pad = -(-ns // LANES) * LANES
    n_sel = min(NSA_TOPN, ns)
    nr = NSA_GROUP * QPAD
    n_keys = past + NEW_PAD
    q5 = _head_major(qb, NSA_HEADS, QPAD).reshape(DB, NSA_KV_HEADS, nr, HEAD_DIM)
    g5 = _head_major(jnp.pad(gates.reshape(DB, T, NSA_HEADS, 3), ((0, 0),) * 3 + ((0, HEAD_DIM - 3),))
                     .reshape(DB, T, NSA_HEADS * HEAD_DIM), NSA_HEADS, QPAD)[..., :3]
    g5 = g5.reshape(DB, NSA_KV_HEADS, nr, 3)
    smap = jnp.asarray(np.tile(_slc_map_rows(n_ch, ns_pad), (NSA_GROUP, 1)), dtype=BF16)
    kwt = jnp.transpose(swa_k, (0, 2, 3, 1))
    vwt = jnp.transpose(swa_v, (0, 2, 3, 1))
    kwnt, vwnt = _new_cols(kw_new, NSA_KV_HEADS), _new_cols(vw_new, NSA_KV_HEADS)
    per = lambda a: pl.BlockSpec((1,) + a.shape[1:], lambda b: (b,) + (0,) * (a.ndim - 1))
    part, bias = pl.pallas_call(
        functools.partial(_nsa_dec_a_kernel, n_q=T, ns=ns, n_sel=n_sel, past=past, n_keys=n_keys), grid=(DB,),
        in_specs=[per(q5), per(kct), per(vct), per(kwt), per(vwt), per(kwnt), per(vwnt), per(g5),
                  pl.BlockSpec(smap.shape, lambda b: (0, 0))],
        out_specs=[pl.BlockSpec((1, NSA_KV_HEADS, nr, HEAD_DIM), lambda b: (b, 0, 0, 0)),
                   pl.BlockSpec((1, NSA_KV_HEADS * QPAD, n_keys), lambda b: (b, 0, 0))],
        out_shape=[jax.ShapeDtypeStruct((DB, NSA_KV_HEADS, nr, HEAD_DIM), F32),
                   jax.ShapeDtypeStruct((DB, NSA_KV_HEADS * QPAD, n_keys), F32)], name="nsa_decode_a",
        compiler_params=_cparams(("parallel",)))(q5, kct, vct, kwt, vwt, kwnt, vwnt, g5, smap)

    n_pg = math.gcd(n_pages, 2 * PAGES_PER_STEP)
    ckt = jnp.transpose(cache_ks, (0, 2, 3, 1))
    cvt = jnp.transpose(cache_vs, (0, 2, 3, 1))
    ksnt, vsnt = _new_cols(ks_new, NSA_KV_HEADS), _new_cols(vs_new, NSA_KV_HEADS)
    page_idx = lambda j: (lambda b, p, pt: (pt[b * n_pages + p * n_pg + j], 0, 0, 0))
    per_db = lambda a: pl.BlockSpec((1,) + a.shape[1:], lambda b, p, pt: (b,) + (0,) * (a.ndim - 1))
    rows = NSA_KV_HEADS * nr
    grid_spec = pltpu.PrefetchScalarGridSpec(
        num_scalar_prefetch=1, grid=(DB, n_pages // n_pg),
        in_specs=([pl.BlockSpec((1, NSA_KV_HEADS, HEAD_DIM, page), page_idx(j)) for j in range(n_pg)] * 2
                  + [per_db(q5), per_db(ksnt), per_db(vsnt),
                     pl.BlockSpec((1, NSA_KV_HEADS * QPAD, n_pg * page), lambda b, p, pt: (b, 0, p)),
                     pl.BlockSpec((1, NSA_KV_HEADS * QPAD, NEW_PAD), lambda b, p, pt: (b, 0, past // NEW_PAD))]),
        out_specs=pl.BlockSpec((1, rows, HEAD_DIM), lambda b, p, pt: (b, 0, 0)),
        scratch_shapes=[pltpu.VMEM((rows, LANES), BF16), pltpu.VMEM((rows, 1), F32), pltpu.VMEM((rows, 1), F32),
                        pltpu.VMEM((rows, LANES), F32)])
    osl = pl.pallas_call(
        functools.partial(_nsa_dec_b_kernel, n_pg=n_pg), grid_spec=grid_spec,
        out_shape=jax.ShapeDtypeStruct((DB, rows, HEAD_DIM), F32), name="nsa_decode_b",
        compiler_params=_cparams(("parallel", "arbitrary")))(
            pt_flat, *([ckt] * n_pg), *([cvt] * n_pg), q5, ksnt, vsnt, bias, bias)

    o = part + g5[..., 1:2] * osl.reshape(DB, NSA_KV_HEADS, nr, HEAD_DIM)
    return o.reshape(DB, NSA_HEADS, QPAD, HEAD_DIM)[:, :, :T].transpose(0, 2, 1, 3).reshape(DB, T, W)


def _dil_dec_kernel(q_ref, kt_ref, vt_ref, knt_ref, vnt_ref, o_ref, *, n_q, wc):
    hb = kt_ref.shape[1]
    t = lax.broadcasted_iota(jnp.int32, (QPAD, 1), 0)

    def log_mult(d, ok):
        w = jnp.zeros(d.shape, F32)
        for window, dil in DIL_PAIRS:
            w = w + ((d >= 0) & (d <= window) & (d % dil == 0)).astype(F32)
        return jnp.where(ok, w, 0.0)

    wk = log_mult(wc + t - lax.broadcasted_iota(jnp.int32, (QPAD, wc), 1), t < n_q)
    un = lax.broadcasted_iota(jnp.int32, (QPAD, NEW_PAD), 1)
    wn = log_mult(t - un, (t < n_q) & (un < n_q))
    for h in range(hb):
        q = (q_ref[0, h] * SCALE).astype(BF16)
        sk = jnp.where(wk > 0.0, _mm(q, kt_ref[0, h]), NEG)
        sn = jnp.where(wn > 0.0, _mm(q, knt_ref[0, h]), NEG)
        m = jnp.maximum(jnp.max(sk, axis=1, keepdims=True), jnp.max(sn, axis=1, keepdims=True))
        pk = wk * jnp.exp(sk - m)
        pn = wn * jnp.exp(sn - m)
        l = jnp.sum(pk, axis=1, keepdims=True) + jnp.sum(pn, axis=1, keepdims=True)
        o_ref[0, h] = (_mm_nt(pk, vt_ref[0, h]) + _mm_nt(pn, vnt_ref[0, h])) / jnp.where(l > 0.0, l, 1.0)


def _dilated_decode(q, k_new, v_new, cache_k, cache_v):
    DB, T, W = q.shape
    wc, nh = cache_k.shape[1], cache_k.shape[2]
    ckt = jnp.transpose(cache_k, (0, 2, 3, 1))
    cvt = jnp.transpose(cache_v, (0, 2, 3, 1))
    qh = _head_major(q, nh, QPAD)
    knt, vnt = _new_cols(k_new, nh), _new_cols(v_new, nh)
    hb = math.gcd(nh, 4)
    spec = lambda a: pl.BlockSpec((1, hb) + a.shape[2:], lambda b, j: (b, j, 0, 0))
    o = pl.pallas_call(
        functools.partial(_dil_dec_kernel, n_q=T, wc=wc), grid=(DB, nh // hb),
        in_specs=[spec(qh), spec(ckt), spec(cvt), spec(knt), spec(vnt)],
        out_specs=pl.BlockSpec((1, hb, QPAD, HEAD_DIM), lambda b, j: (b, j, 0, 0)),
        out_shape=jax.ShapeDtypeStruct((DB, nh, QPAD, HEAD_DIM), F32), name="dilated_decode",
        compiler_params=_cparams(("parallel", "parallel")))(qh, ckt, cvt, knt, vnt)
    return o[:, :, :T].transpose(0, 2, 1, 3).reshape(DB, T, W)


def kernel(x_prompt, x_sample, cache_a_k, cache_a_v, cache_a_logf, cache_b_cmp_k, cache_b_cmp_v, cache_b_slc_k, cache_b_slc_v, cache_b_swa_k, cache_b_swa_v, cache_c_k, cache_c_v, page_table, norm_mix0, w_in0, fox_bf, nsa_pe_k, nsa_w1_k, nsa_w2_k, nsa_pe_v, nsa_w1_v, nsa_w2_v, w_out0, norm_ffn0, ffn_w_gate, ffn_w_up, ffn_w_down, norm_mix1, w_in1, w_out1, norm_ffn1, moe_router, moe_w_gate, moe_w_up, moe_w_down, norm_final):
    B, S, D = x_prompt.shape
    DB, T, _ = x_sample.shape
    n_pages = page_table.shape[1]
    past = n_pages * cache_a_k.shape[1]
    pt_flat = page_table.reshape(-1).astype(jnp.int32)
    fw = FOX_HEADS * HEAD_DIM
    nw = NSA_HEADS * HEAD_DIM
    kvw = NSA_KV_HEADS * HEAD_DIM

    cuts = np.cumsum([0, fw, fw, fw, FOX_HEADS, nw] + [kvw] * 6 + [3 * NSA_HEADS])
    col = lambda i: w_in0[:, cuts[i]:cuts[i + 1]]
    qa_w, ka_w, va_w, fa_w, qb_w, kc_w, vc_w, ks_w, vs_w, kw_w, vw_w, gb_w = [col(i) for i in range(12)]

    f32_out = ((F32, 1.0),)
    bf16_out = ((BF16, 1.0),)

    def pack(ws, ropes, emits=None):
        widths = [w.shape[1] for w in ws]
        starts = np.concatenate([[0], np.cumsum(widths)[:-1]])
        emits = emits or [f32_out] * len(ws)
        return (jnp.concatenate(ws, axis=1).astype(BF16),
                [(int(s), int(w), r, e) for s, w, r, e in zip(starts, widths, ropes, emits)])

    both_out = ((F32, 1.0), (BF16, 1.0))
    q_out = ((BF16, SCALE),)
    w0r, segs0r = pack([kc_w, vc_w, ka_w, ks_w, kw_w], [True, False, False, True, True],
                       [f32_out] * 2 + [bf16_out] * 3)
    w0c, segs0c = pack([qa_w, ka_w, va_w, qb_w, kc_w, ks_w, kw_w, vc_w, vs_w, vw_w, gb_w],
                       [False, False, False, True, True, True, True, False, False, False, "sigmoid"],
                       [q_out, f32_out, both_out, q_out, f32_out, f32_out, f32_out, f32_out, both_out, both_out,
                        f32_out])
    w0c = w0c.T
    w0s, segs0s = pack([qa_w, ka_w, va_w, qb_w, kc_w, ks_w, kw_w, vc_w, vs_w, vw_w],
                       [False, False, False, True, True, True, True, False, False, False])
    wgate = gb_w.astype(BF16)
    wft = jnp.zeros((16, D), F32).at[:FOX_HEADS].set(fa_w.T).astype(BF16)
    wfr = jnp.zeros((D, LANES), F32).at[:, :FOX_HEADS].set(fa_w).astype(BF16)
    brow = jnp.zeros((1, LANES), F32).at[0, :FOX_HEADS].set(fox_bf)
    logf_args = (wft, fox_bf.reshape(FOX_HEADS, 1).astype(F32), wfr, brow)
    dw = w_in1.shape[1] // 3
    w1 = w_in1.astype(BF16)
    segs1 = [(0, dw, True, f32_out), (dw, dw, True, f32_out), (2 * dw, dw, False, f32_out)]
    w1c = w1[:, dw:].T
    segs1c = [(0, dw, True, f32_out), (dw, dw, False, f32_out)]
    w_out0_b, w_out1_b = w_out0.astype(BF16), w_out1.astype(BF16)
    ffn_g, ffn_u, ffn_d = ffn_w_gate.astype(BF16), ffn_w_up.astype(BF16), ffn_w_down.astype(BF16)
    moe_g, moe_u, moe_d = moe_w_gate.astype(BF16), moe_w_up.astype(BF16), moe_w_down.astype(BF16)
    cmp_k_w = _cmp_weights(nsa_pe_k, nsa_w1_k, nsa_w2_k)
    cmp_v_w = _cmp_weights(nsa_pe_v, nsa_w1_v, nsa_w2_v)

    tab_p = _rope_tables(jnp.arange(S))
    tab_s = _rope_tables(past + jnp.arange(DB * T) % T)
    tm_p = _tile(S, 512)
    npb = S // tm_p
    win_b = min(NSA_WINDOW, S)
    win_c = min(DIL_WINDOW_MAX, S)

    xp = x_prompt.reshape(B * S, D)
    (kc, vc, ka_b, ks_b, kw_b, qat_b, kat, vat, vat_b, qbt_b, kct, kst, kwt, vct, vst, vst_b, vwt, vwt_b,
     gates_t, lft_p, lf_rows) = _project(
        xp, norm_mix0, tab_p, npb, tm_p, w=w0r, row_segs=segs0r, wt=w0c, col_segs=segs0c,
        logf=logf_args, name="proj0_prompt")
    r3 = lambda a: a.reshape(B, S, a.shape[-1])
    heads = lambda a, h: a.reshape(B, h, HEAD_DIM, a.shape[-1])
    kaug, c0 = _fox_prep(r3(ka_b), r3(lf_rows), tm_p)
    o_at = _fox_prompt(qat_b, kaug, vat_b, c0, tm_p)
    kcmp_p = _compress(r3(kc), cmp_k_w, "compress_k_prompt")
    vcmp_p = _compress(r3(vc), cmp_v_w, "compress_v_prompt")
    kv2 = lambda a: heads(a, NSA_KV_HEADS)
    o_bt = _nsa_prompt(qbt_b, kv2(kcmp_p), kv2(vcmp_p), r3(ks_b), vst_b, r3(kw_b), vwt_b, gates_t)
    hp = _outproj(xp, [o_at, o_bt], w_out0_b, "outproj0_prompt", tm=tm_p)
    hp = _ffn(hp, norm_ffn0, ffn_g, ffn_u, ffn_d, "ffn_prompt")

    xs = x_sample.reshape(DB * T, D)
    (qa_s, ka_s, va_s, qb_s, kc_s, ks_s, kw_s, vc_s, vs_s, vw_s, gates_s, lft_s, _) = _project(
        xs, norm_mix0, tab_s, 1, DB * T, w=w0s, row_segs=segs0s, w_gate=wgate, logf=logf_args,
        name="proj0_sample")
    s3 = lambda a: a.reshape(DB, T, a.shape[-1])
    lf_s = jnp.transpose(lft_s[0].reshape(FOX_HEADS, DB, T), (1, 2, 0))
    o_a_s = _fox_decode(s3(qa_s), s3(ka_s), s3(va_s), lf_s, cache_a_k, cache_a_v, cache_a_logf, pt_flat)
    kcmp_s = _compress_paged(cache_b_cmp_k, pt_flat, DB, cmp_k_w, "compress_k_paged")
    vcmp_s = _compress_paged(cache_b_cmp_v, pt_flat, DB, cmp_v_w, "compress_v_paged")
    kvs = lambda a: a.reshape(DB, NSA_KV_HEADS, HEAD_DIM, a.shape[-1])
    o_b_s = _nsa_decode(s3(qb_s), kvs(kcmp_s), kvs(vcmp_s), cache_b_slc_k, cache_b_slc_v, s3(ks_s), s3(vs_s),
                        cache_b_swa_k, cache_b_swa_v, s3(kw_s), s3(vw_s), s3(gates_s), pt_flat, past)
    hs = _outproj(xs, [o_a_s.reshape(DB * T, fw), o_b_s.reshape(DB * T, nw)], w_out0_b, "outproj0_sample")
    hs = _ffn(hs, norm_ffn0, ffn_g, ffn_u, ffn_d, "ffn_sample")

    first_c = (S - win_c) // tm_p
    segs1p = [(0, dw, True, ((F32, SCALE),)), (dw, dw, True, f32_out), (2 * dw, dw, False, f32_out)]
    q1, k1, v1, k1t, v1t = _project(hp, norm_mix1, tab_p, npb, tm_p, w=w1, row_segs=segs1p, wt=w1c,
                                    col_segs=segs1c, col_from=(npb, first_c), name="proj1_prompt")
    groups = [_band_attention(q1.reshape(B, S, dw), k1.reshape(B, S, dw), v1.reshape(B, S, dw), dil,
                              window // dil, 512 if dil == 1 else 256, "dilated_prompt_d%d" % dil,
                              n_hp=4 if dil < 16 else 2)
              for window, dil in DIL_PAIRS]
    hp = _merge_outproj(hp, [g[0] for g in groups], [g[1] for g in groups], w_out1_b, "outproj1_prompt")
    y_prompt = _moe_final(hp, norm_ffn1, moe_router, moe_g, moe_u, moe_d, norm_final, "moe_prompt").reshape(B, S, D)

    q1s, k1s, v1s = _project(hs, norm_mix1, tab_s, 1, DB * T, w=w1, row_segs=segs1, name="proj1_sample")
    o1s = _dilated_decode(s3(q1s), s3(k1s), s3(v1s), cache_c_k, cache_c_v)
    hs = _outproj(hs, [o1s.reshape(DB * T, dw)], w_out1_b, "outproj1_sample")
    y_sample = _moe_final(hs, norm_ffn1, moe_router, moe_g, moe_u, moe_d, norm_final, "moe_sample").reshape(DB, T, D)

    def state(a, h, last=None):
        a = a.reshape(a.shape[0], h, HEAD_DIM, a.shape[-1])
        if last is not None:
            a = a[..., a.shape[-1] - last:]
        return jnp.transpose(a, (0, 3, 1, 2))

    h4 = lambda a, h: a.reshape(DB, T, h, HEAD_DIM)
    nh1 = dw // HEAD_DIM
    return (y_prompt, y_sample,
            state(kat, FOX_HEADS), state(vat, FOX_HEADS), jnp.transpose(lft_p, (0, 2, 1)),
            state(kct, NSA_KV_HEADS), state(vct, NSA_KV_HEADS), state(kst, NSA_KV_HEADS), state(vst, NSA_KV_HEADS),
            state(kwt, NSA_KV_HEADS, win_b), state(vwt, NSA_KV_HEADS, win_b),
            state(k1t, nh1, win_c), state(v1t, nh1, win_c),
            h4(ka_s, FOX_HEADS), h4(va_s, FOX_HEADS), lf_s,
            h4(kc_s, NSA_KV_HEADS), h4(vc_s, NSA_KV_HEADS), h4(ks_s, NSA_KV_HEADS),
            h4(vs_s, NSA_KV_HEADS), h4(kw_s, NSA_KV_HEADS), h4(vw_s, NSA_KV_HEADS),
            h4(k1s, nh1), h4(v1s, nh1))
```
